```python
import jax, jax.numpy as jnp
from jax import lax
import numpy as np

D_MODEL = 1024
BATCH = 32
SEQ = 2048
DEPTH = 2

D_CONV = D_MODEL
CONV_GROUPS = 16
CONV_WIDTH = 3
D_SGU = D_MODEL
SGU_GROUPS = 8
SGU_GROUP_DIM = D_SGU // SGU_GROUPS
CHUNK = 128
D_FF = 2816
N_BRANCH = 2
EPS = 1e-6

IN_COLS = 3 * D_CONV + 2 * D_SGU + N_BRANCH * D_MODEL

kernel_name = "hybrid_shortconv_sgu_gated_merge"


def rmsnorm(x, g):
    xf = x.astype(jnp.float32)
    y = xf * lax.rsqrt(jnp.mean(xf * xf, axis=-1, keepdims=True) + EPS)
    return (y * g.astype(jnp.float32)).astype(x.dtype)


def layernorm(x, g, b):
    xf = x.astype(jnp.float32)
    mu = jnp.mean(xf, axis=-1, keepdims=True)
    xc = xf - mu
    var = jnp.mean(xc * xc, axis=-1, keepdims=True)
    y = xc * lax.rsqrt(var + EPS)
    return (y * g.astype(jnp.float32) + b.astype(jnp.float32)).astype(x.dtype)


def causal_dwconv3(x, w):
    s = x.shape[1]
    xp = jnp.pad(x, ((0, 0), (CONV_WIDTH - 1, 0), (0, 0)))
    return xp[:, :s] * w[0] + xp[:, 1:s + 1] * w[1] + xp[:, 2:s + 2] * w[2]


def short_conv_mixer(b_gate, c_gate, xin, conv_w):
    return b_gate * causal_dwconv3(c_gate * xin, conv_w)


def spatial_gating_mixer(u, v, ln_g, ln_b, w_s, b_s):
    bsz, s, _ = v.shape
    n_chunks = s // CHUNK
    vn = layernorm(v, ln_g, ln_b).reshape(bsz, n_chunks, CHUNK, SGU_GROUPS, SGU_GROUP_DIM)
    mask = jnp.tril(jnp.ones((CHUNK, CHUNK), dtype=bool))
    w = jnp.where(mask[None], w_s, jnp.zeros((), w_s.dtype))
    mixed = jnp.einsum('gts,bnsgc->bntgc', w, vn)
    mixed = mixed + jnp.swapaxes(b_s, 0, 1)[None, None, :, :, None]
    return u * mixed.reshape(bsz, s, D_SGU)


def conv_gated_mlp(h, w_up, conv_w, w_down):
    up = causal_dwconv3(h @ w_up, conv_w)
    gate, val = jnp.split(up, 2, axis=-1)
    return (jax.nn.silu(gate) * val) @ w_down


def _fwd_setup_inputs(seed: int = 0) -> dict:
    key = jax.random.key(seed)
    ks = jax.random.split(key, 16)
    f32 = jnp.float32

    def nrm(k, shape, scale):
        return jax.random.normal(k, shape, f32) * scale

    x = jax.random.normal(ks[0], (BATCH, SEQ, D_MODEL), f32)
    mix_norm_g = 1.0 + nrm(ks[1], (DEPTH, D_MODEL), 0.02)
    w_in = nrm(ks[2], (DEPTH, D_MODEL, IN_COLS), D_MODEL ** -0.5)
    conv_a_w = nrm(ks[3], (DEPTH, CONV_WIDTH, D_CONV), CONV_WIDTH ** -0.5)
    ln_v_g = 1.0 + nrm(ks[4], (DEPTH, D_SGU), 0.02)
    ln_v_b = nrm(ks[5], (DEPTH, D_SGU), 0.02)
    w_s = nrm(ks[6], (DEPTH, SGU_GROUPS, CHUNK, CHUNK), CHUNK ** -0.5)
    b_s = 1.0 + nrm(ks[7], (DEPTH, SGU_GROUPS, CHUNK), 0.02)
    w_out = nrm(ks[8], (DEPTH, D_MODEL, D_MODEL), D_MODEL ** -0.5)
    ffn_norm_g = 1.0 + nrm(ks[9], (DEPTH, D_MODEL), 0.02)
    w_up = nrm(ks[10], (DEPTH, D_MODEL, 2 * D_FF), D_MODEL ** -0.5)
    conv_ffn_w = nrm(ks[11], (DEPTH, CONV_WIDTH, 2 * D_FF), CONV_WIDTH ** -0.5)
    w_down = nrm(ks[12], (DEPTH, D_FF, D_MODEL), D_FF ** -0.5)
    final_norm_g = 1.0 + nrm(ks[13], (D_MODEL,), 0.02)
    return {"x": x, "mix_norm_g": mix_norm_g, "w_in": w_in, "conv_a_w": conv_a_w,
            "ln_v_g": ln_v_g, "ln_v_b": ln_v_b, "w_s": w_s, "b_s": b_s,
            "w_out": w_out, "ffn_norm_g": ffn_norm_g, "w_up": w_up,
            "conv_ffn_w": conv_ffn_w, "w_down": w_down, "final_norm_g": final_norm_g}


def _fwd_reference(x, mix_norm_g, w_in, conv_a_w, ln_v_g, ln_v_b, w_s, b_s, w_out,
              ffn_norm_g, w_up, conv_ffn_w, w_down, final_norm_g):
    split_pts = [D_CONV, 2 * D_CONV, 3 * D_CONV, 3 * D_CONV + D_SGU,
                 3 * D_CONV + 2 * D_SGU, 3 * D_CONV + 2 * D_SGU + D_MODEL]
    for l in range(DEPTH):
        h = rmsnorm(x, mix_norm_g[l])
        proj = h @ w_in[l]
        b_gate, c_gate, xin, u, v, g_a, g_b = jnp.split(proj, split_pts, axis=-1)
        y_a = short_conv_mixer(b_gate, c_gate, xin, conv_a_w[l])
        y_b = spatial_gating_mixer(u, v, ln_v_g[l], ln_v_b[l], w_s[l], b_s[l])
        merged = jax.nn.sigmoid(g_a) * y_a + jax.nn.sigmoid(g_b) * y_b
        x = x + merged @ w_out[l]
        h = rmsnorm(x, ffn_norm_g[l])
        x = x + conv_gated_mlp(h, w_up[l], conv_ffn_w[l], w_down[l])
    return rmsnorm(x, final_norm_g)


import jax as _jax
import jax.numpy as _jnp

TWIN_FORMAT = 'train_step'
FWD_PARAMS = ['x', 'mix_norm_g', 'w_in', 'conv_a_w', 'ln_v_g', 'ln_v_b', 'w_s', 'b_s', 'w_out', 'ffn_norm_g', 'w_up', 'conv_ffn_w', 'w_down', 'final_norm_g']
TWIN_WEIGHTS = ['mix_norm_g', 'w_in', 'conv_a_w', 'ln_v_g', 'ln_v_b', 'w_s', 'b_s', 'w_out', 'ffn_norm_g', 'w_up', 'conv_ffn_w', 'w_down', 'final_norm_g']
TWIN_DIFF_INPUT = 'x'
TWIN_INPUTS = ['x', 'mix_norm_g', 'w_in', 'conv_a_w', 'ln_v_g', 'ln_v_b', 'w_s', 'b_s', 'w_out', 'ffn_norm_g', 'w_up', 'conv_ffn_w', 'w_down', 'final_norm_g', 'loss_target', 'm_mix_norm_g', 'm_w_in', 'm_conv_a_w', 'm_ln_v_g', 'm_ln_v_b', 'm_w_s', 'm_b_s', 'm_w_out', 'm_ffn_norm_g', 'm_w_up', 'm_conv_ffn_w', 'm_w_down', 'm_final_norm_g', 'v_mix_norm_g', 'v_w_in', 'v_conv_a_w', 'v_ln_v_g', 'v_ln_v_b', 'v_w_s', 'v_b_s', 'v_w_out', 'v_ffn_norm_g', 'v_w_up', 'v_conv_ffn_w', 'v_w_down', 'v_final_norm_g']
TWIN_OUTPUTS = ['loss', 'grad_x', 'grad_mix_norm_g', 'grad_w_in', 'grad_conv_a_w', 'grad_ln_v_g', 'grad_ln_v_b', 'grad_w_s', 'grad_b_s', 'grad_w_out', 'grad_ffn_norm_g', 'grad_w_up', 'grad_conv_ffn_w', 'grad_w_down', 'grad_final_norm_g', 'delta_mix_norm_g', 'delta_w_in', 'delta_conv_a_w', 'delta_ln_v_g', 'delta_ln_v_b', 'delta_w_s', 'delta_b_s', 'delta_w_out', 'delta_ffn_norm_g', 'delta_w_up', 'delta_conv_ffn_w', 'delta_w_down', 'delta_final_norm_g', 'new_m_mix_norm_g', 'new_m_w_in', 'new_m_conv_a_w', 'new_m_ln_v_g', 'new_m_ln_v_b', 'new_m_w_s', 'new_m_b_s', 'new_m_w_out', 'new_m_ffn_norm_g', 'new_m_w_up', 'new_m_conv_ffn_w', 'new_m_w_down', 'new_m_final_norm_g', 'new_v_mix_norm_g', 'new_v_w_in', 'new_v_conv_a_w', 'new_v_ln_v_g', 'new_v_ln_v_b', 'new_v_w_s', 'new_v_b_s', 'new_v_w_out', 'new_v_ffn_norm_g', 'new_v_w_up', 'new_v_conv_ffn_w', 'new_v_w_down', 'new_v_final_norm_g']
TWIN_LEAF_KINDS = {'loss': 'loss', 'grad_x': 'grad_x', 'grad_mix_norm_g': 'grad_w', 'grad_w_in': 'grad_w', 'grad_conv_a_w': 'grad_w', 'grad_ln_v_g': 'grad_w', 'grad_ln_v_b': 'grad_w', 'grad_w_s': 'grad_w', 'grad_b_s': 'grad_w', 'grad_w_out': 'grad_w', 'grad_ffn_norm_g': 'grad_w', 'grad_w_up': 'grad_w', 'grad_conv_ffn_w': 'grad_w', 'grad_w_down': 'grad_w', 'grad_final_norm_g': 'grad_w', 'delta_mix_norm_g': 'delta_w', 'delta_w_in': 'delta_w', 'delta_conv_a_w': 'delta_w', 'delta_ln_v_g': 'delta_w', 'delta_ln_v_b': 'delta_w', 'delta_w_s': 'delta_w', 'delta_b_s': 'delta_w', 'delta_w_out': 'delta_w', 'delta_ffn_norm_g': 'delta_w', 'delta_w_up': 'delta_w', 'delta_conv_ffn_w': 'delta_w', 'delta_w_down': 'delta_w', 'delta_final_norm_g': 'delta_w', 'new_m_mix_norm_g': 'new_m', 'new_m_w_in': 'new_m', 'new_m_conv_a_w': 'new_m', 'new_m_ln_v_g': 'new_m', 'new_m_ln_v_b': 'new_m', 'new_m_w_s': 'new_m', 'new_m_b_s': 'new_m', 'new_m_w_out': 'new_m', 'new_m_ffn_norm_g': 'new_m', 'new_m_w_up': 'new_m', 'new_m_conv_ffn_w': 'new_m', 'new_m_w_down': 'new_m', 'new_m_final_norm_g': 'new_m', 'new_v_mix_norm_g': 'new_v', 'new_v_w_in': 'new_v', 'new_v_conv_a_w': 'new_v', 'new_v_ln_v_g': 'new_v', 'new_v_ln_v_b': 'new_v', 'new_v_w_s': 'new_v', 'new_v_b_s': 'new_v', 'new_v_w_out': 'new_v', 'new_v_ffn_norm_g': 'new_v', 'new_v_w_up': 'new_v', 'new_v_conv_ffn_w': 'new_v', 'new_v_w_down': 'new_v', 'new_v_final_norm_g': 'new_v'}


def _forward(args):
    return _fwd_reference(*[args[k] for k in FWD_PARAMS])


def _output_shape():
    out = _jax.eval_shape(lambda: _forward(_fwd_setup_inputs(0)))
    return out.shape, out.dtype

N_MICROBATCH = 1
ADAM_LR = 0.001
ADAM_B1 = 0.9
ADAM_B2 = 0.999
ADAM_EPS = 1e-08
ADAM_WD = 0.01
ADAM_STEP = 10
PER_EXAMPLE_BATCH_AXIS = {'x': 0, 'loss_target': 0}
SHARED_INPUTS = []
_WEIGHT_DTYPES = {'mix_norm_g': _jnp.float32, 'w_in': _jnp.float32, 'conv_a_w': _jnp.float32, 'ln_v_g': _jnp.float32, 'ln_v_b': _jnp.float32, 'w_s': _jnp.float32, 'b_s': _jnp.float32, 'w_out': _jnp.float32, 'ffn_norm_g': _jnp.float32, 'w_up': _jnp.float32, 'conv_ffn_w': _jnp.float32, 'w_down': _jnp.float32, 'final_norm_g': _jnp.float32}
MOMENT_SCALE = {'mix_norm_g': 2.718432e-01, 'w_in': 9.916642e-02, 'conv_a_w': 1.151105e-01, 'ln_v_g': 7.860287e-02, 'ln_v_b': 8.134227e-02, 'w_s': 7.886590e-02, 'b_s': 1.143445e-01, 'w_out': 1.791832e-01, 'ffn_norm_g': 1.627712e-01, 'w_up': 6.773972e-02, 'conv_ffn_w': 6.704903e-02, 'w_down': 1.110233e-01, 'final_norm_g': 6.396012e+01}


def _to_microbatches(a, axis):
    t = _jnp.moveaxis(a, axis, 0)
    t = t.reshape((N_MICROBATCH, t.shape[0] // N_MICROBATCH) + t.shape[1:])
    return _jnp.moveaxis(t, 1, axis + 1)


def setup_inputs(seed: int = 0) -> dict:
    inp = _fwd_setup_inputs(seed)
    key = _jax.random.fold_in(_jax.random.key(seed), 7919)
    shape, _ = _output_shape()
    out = dict(inp)
    out["loss_target"] = _jax.random.normal(_jax.random.fold_in(key, 0), shape, _jnp.float32)
    for i, name in enumerate(TWIN_WEIGHTS):
        w = inp[name].astype(_jnp.float32)
        if MOMENT_SCALE is None:
            s = _jnp.sqrt(_jnp.mean(_jnp.square(w)) + 1e-30)
        else:
            s = MOMENT_SCALE[name]
        km, kv = _jax.random.split(_jax.random.fold_in(key, i + 1))
        out[name] = w
        out["m_" + name] = s * _jax.random.normal(km, w.shape, _jnp.float32)
        out["v_" + name] = (s * s) * _jax.random.uniform(kv, w.shape, _jnp.float32, 0.5, 1.5)
    if N_MICROBATCH > 1:
        for name, axis in PER_EXAMPLE_BATCH_AXIS.items():
            out[name] = _to_microbatches(out[name], axis)
    return {'x': out['x'], 'mix_norm_g': out['mix_norm_g'], 'w_in': out['w_in'], 'conv_a_w': out['conv_a_w'], 'ln_v_g': out['ln_v_g'], 'ln_v_b': out['ln_v_b'], 'w_s': out['w_s'], 'b_s': out['b_s'], 'w_out': out['w_out'], 'ffn_norm_g': out['ffn_norm_g'], 'w_up': out['w_up'], 'conv_ffn_w': out['conv_ffn_w'], 'w_down': out['w_down'], 'final_norm_g': out['final_norm_g'], 'loss_target': out['loss_target'], 'm_mix_norm_g': out['m_mix_norm_g'], 'm_w_in': out['m_w_in'], 'm_conv_a_w': out['m_conv_a_w'], 'm_ln_v_g': out['m_ln_v_g'], 'm_ln_v_b': out['m_ln_v_b'], 'm_w_s': out['m_w_s'], 'm_b_s': out['m_b_s'], 'm_w_out': out['m_w_out'], 'm_ffn_norm_g': out['m_ffn_norm_g'], 'm_w_up': out['m_w_up'], 'm_conv_ffn_w': out['m_conv_ffn_w'], 'm_w_down': out['m_w_down'], 'm_final_norm_g': out['m_final_norm_g'], 'v_mix_norm_g': out['v_mix_norm_g'], 'v_w_in': out['v_w_in'], 'v_conv_a_w': out['v_conv_a_w'], 'v_ln_v_g': out['v_ln_v_g'], 'v_ln_v_b': out['v_ln_v_b'], 'v_w_s': out['v_w_s'], 'v_b_s': out['v_b_s'], 'v_w_out': out['v_w_out'], 'v_ffn_norm_g': out['v_ffn_norm_g'], 'v_w_up': out['v_w_up'], 'v_conv_ffn_w': out['v_conv_ffn_w'], 'v_w_down': out['v_w_down'], 'v_final_norm_g': out['v_final_norm_g']}


def _loss(weights, diff, rest, loss_target):
    with _jax.named_scope("forward"):
        args = {**rest, TWIN_DIFF_INPUT: diff, **{k: w.astype(_WEIGHT_DTYPES[k]) for k, w in weights.items()}}
        y = _forward(args)
    with _jax.named_scope("loss_head"):
        err = _jnp.square(y.astype(_jnp.float32) - loss_target)
        return 0.5 * _jnp.sum(_jnp.mean(err, axis=-1)) if err.ndim else 0.5 * err


def _adamw(w, g, m, v):
    m = ADAM_B1 * m + (1.0 - ADAM_B1) * g
    v = ADAM_B2 * v + (1.0 - ADAM_B2) * _jnp.square(g)
    m_hat = m / (1.0 - ADAM_B1 ** ADAM_STEP)
    v_hat = v / (1.0 - ADAM_B2 ** ADAM_STEP)
    delta = -ADAM_LR * (m_hat / (_jnp.sqrt(v_hat) + ADAM_EPS) + ADAM_WD * w)
    return delta, m, v


def reference(x, mix_norm_g, w_in, conv_a_w, ln_v_g, ln_v_b, w_s, b_s, w_out, ffn_norm_g, w_up, conv_ffn_w, w_down, final_norm_g, loss_target, m_mix_norm_g, m_w_in, m_conv_a_w, m_ln_v_g, m_ln_v_b, m_w_s, m_b_s, m_w_out, m_ffn_norm_g, m_w_up, m_conv_ffn_w, m_w_down, m_final_norm_g, v_mix_norm_g, v_w_in, v_conv_a_w, v_ln_v_g, v_ln_v_b, v_w_s, v_b_s, v_w_out, v_ffn_norm_g, v_w_up, v_conv_ffn_w, v_w_down, v_final_norm_g):
    given = dict(x=x, mix_norm_g=mix_norm_g, w_in=w_in, conv_a_w=conv_a_w, ln_v_g=ln_v_g, ln_v_b=ln_v_b, w_s=w_s, b_s=b_s, w_out=w_out, ffn_norm_g=ffn_norm_g, w_up=w_up, conv_ffn_w=conv_ffn_w, w_down=w_down, final_norm_g=final_norm_g, loss_target=loss_target, m_mix_norm_g=m_mix_norm_g, m_w_in=m_w_in, m_conv_a_w=m_conv_a_w, m_ln_v_g=m_ln_v_g, m_ln_v_b=m_ln_v_b, m_w_s=m_w_s, m_b_s=m_b_s, m_w_out=m_w_out, m_ffn_norm_g=m_ffn_norm_g, m_w_up=m_w_up, m_conv_ffn_w=m_conv_ffn_w, m_w_down=m_w_down, m_final_norm_g=m_final_norm_g, v_mix_norm_g=v_mix_norm_g, v_w_in=v_w_in, v_conv_a_w=v_conv_a_w, v_ln_v_g=v_ln_v_g, v_ln_v_b=v_ln_v_b, v_w_s=v_w_s, v_b_s=v_b_s, v_w_out=v_w_out, v_ffn_norm_g=v_ffn_norm_g, v_w_up=v_w_up, v_conv_ffn_w=v_conv_ffn_w, v_w_down=v_w_down, v_final_norm_g=v_final_norm_g)
    weights = {n: given[n] for n in TWIN_WEIGHTS}
    shared = {n: given[n] for n in SHARED_INPUTS}
    per_example = {n: given[n] for n in ['x']}
    grad_fn = _jax.value_and_grad(_loss, argnums=(0, 1))

    def one_microbatch(ex, loss_target):
        ex = dict(ex)
        diff = ex.pop(TWIN_DIFF_INPUT)
        return grad_fn(weights, diff, {**shared, **ex}, loss_target)

    if N_MICROBATCH == 1:
        loss, (grad_w, grad_x) = one_microbatch(per_example, given["loss_target"])
    else:
        def body(carry, xs):
            loss_sum, grad_sum = carry
            l_k, (gw_k, gx_k) = one_microbatch(xs[0], xs[1])
            with _jax.named_scope("update"):
                return (loss_sum + l_k, _jax.tree.map(_jnp.add, grad_sum, gw_k)), gx_k

        init = (_jnp.zeros((), _jnp.float32), _jax.tree.map(_jnp.zeros_like, weights))
        (loss, grad_w), grad_x = _jax.lax.scan(body, init, (per_example, given["loss_target"]))
    with _jax.named_scope("update"):
        delta_w, new_m, new_v = {}, {}, {}
        for n in TWIN_WEIGHTS:
            delta_w[n], new_m[n], new_v[n] = _adamw(weights[n], grad_w[n], given["m_" + n], given["v_" + n])
    return (loss, grad_x, *[grad_w[n] for n in TWIN_WEIGHTS], *[delta_w[n] for n in TWIN_WEIGHTS],
            *[new_m[n] for n in TWIN_WEIGHTS], *[new_v[n] for n in TWIN_WEIGHTS])
```

```python
import functools

import jax
import jax.numpy as jnp
from jax import lax
from jax.experimental import pallas as pl
from jax.experimental.pallas import tpu as pltpu

EPS = 1e-6
CHUNK = 128
N_GROUPS = 8
N_DEV = 8
HALO = 8
ADAM_LR = 0.001
ADAM_B1 = 0.9
ADAM_B2 = 0.999
ADAM_EPS = 1e-08
ADAM_WD = 0.01
ADAM_STEP = 10
VMEM_LIMIT_BYTES = 56 * 1024 * 1024
F32 = jnp.float32
BF16 = jnp.bfloat16
MESH = pl.DeviceIdType.MESH
ANY = pl.BlockSpec(memory_space=pl.ANY)
NT_DIMS = (((1,), (1,)), ((), ()))
TN_DIMS = (((0,), (0,)), ((), ()))


def _params(n_grid_axes):
    return pltpu.CompilerParams(dimension_semantics=("arbitrary",) * n_grid_axes,
                                vmem_limit_bytes=VMEM_LIMIT_BYTES)


def _shift_down(cur, prev8, k):
    rolled = pltpu.roll(cur, k, 0)
    prolled = pltpu.roll(prev8, k, 0)
    row = lax.broadcasted_iota(jnp.int32, prev8.shape, 0)
    head = jnp.where(row < k, prolled, rolled[:HALO])
    return jnp.concatenate([head, rolled[HALO:]], axis=0)


def _shift_up(cur, next8, k):
    tm = cur.shape[0]
    rolled = pltpu.roll(cur, tm - k, 0)
    nrolled = pltpu.roll(next8, HALO - k, 0)
    row = lax.broadcasted_iota(jnp.int32, next8.shape, 0)
    tail = jnp.where(row >= HALO - k, nrolled, rolled[tm - HALO:])
    return jnp.concatenate([rolled[:tm - HALO], tail], axis=0)


def _conv_fwd(cur, prev8, cw):
    s1 = _shift_down(cur, prev8, 1)
    s2 = _shift_down(cur, prev8, 2)
    y = s2 * cw[0:1, :] + s1 * cw[1:2, :] + cur * cw[2:3, :]
    return y, s1, s2


def _conv_bwd(d, next8, cw):
    return d * cw[2:3, :] + _shift_up(d, next8, 1) * cw[1:2, :] + _shift_up(d, next8, 2) * cw[0:1, :]


def _colsum(a):
    return jnp.sum(a, axis=0, keepdims=True)


def _rms_stats(xv):
    r = lax.rsqrt(jnp.mean(xv * xv, axis=-1, keepdims=True) + EPS)
    return r, xv * r


def _rms_bwd(dh, xv, g):
    r, n = _rms_stats(xv)
    dn = dh * g
    dx = r * (dn - n * jnp.mean(dn * n, axis=-1, keepdims=True))
    return dx, _colsum(dh * n)


def _mixer_forward(p_ref, cprev, xiprev, cw, lng, lnb, ws_ref, bias_ref, mixed_scr, d):
    tm = p_ref.shape[0]
    b = p_ref[:, 0:d]
    c = p_ref[:, d:2 * d]
    xi = p_ref[:, 2 * d:3 * d]
    u = p_ref[:, 3 * d:4 * d]
    v = p_ref[:, 4 * d:5 * d]
    sa = jax.nn.sigmoid(p_ref[:, 5 * d:6 * d])
    sb = jax.nn.sigmoid(p_ref[:, 6 * d:7 * d])
    cx = c * xi
    conv, s1, s2 = _conv_fwd(cx, cprev * xiprev, cw)
    ya = b * conv
    mu = jnp.mean(v, axis=-1, keepdims=True)
    xc = v - mu
    rstd = lax.rsqrt(jnp.mean(xc * xc, axis=-1, keepdims=True) + EPS)
    vhat = xc * rstd
    vnb = (vhat * lng + lnb).astype(BF16)
    tril = (lax.broadcasted_iota(jnp.int32, (CHUNK, CHUNK), 0)
            >= lax.broadcasted_iota(jnp.int32, (CHUNK, CHUNK), 1))
    gd = d // N_GROUPS
    for g in range(N_GROUPS):
        wm = jnp.where(tril, ws_ref[g], 0.0).astype(BF16)
        cols = slice(g * gd, (g + 1) * gd)
        for n in range(tm // CHUNK):
            rows = slice(n * CHUNK, (n + 1) * CHUNK)
            mixed_scr[rows, cols] = (jnp.dot(wm, vnb[rows, cols], preferred_element_type=F32)
                                     + bias_ref[:, cols])
    mixed = mixed_scr[...]
    yb = u * mixed
    merged = sa * ya + sb * yb
    return dict(b=b, c=c, xi=xi, u=u, sa=sa, sb=sb, cx=cx, s1=s1, s2=s2, conv=conv, ya=ya,
                rstd=rstd, vhat=vhat, vnb=vnb, mixed=mixed, yb=yb, merged=merged, tril=tril)


def rms_matmul(x, g, w, *, name, tm, out_3d):
    t, d = x.shape
    nj, _, n = w.shape
    tm = min(tm, t)

    def body(x_ref, g_ref, w_ref, h_ref, o_ref, hs_ref):
        @pl.when(pl.program_id(1) == 0)
        def _():
            _, nrm = _rms_stats(x_ref[...])
            hv = (nrm * g_ref[...]).astype(BF16)
            hs_ref[...] = hv
            h_ref[...] = hv

        o_ref[...] = jnp.dot(hs_ref[...], w_ref[...], preferred_element_type=F32)

    if out_3d:
        out_shape = jax.ShapeDtypeStruct((nj, t, n), F32)
        out_spec = pl.BlockSpec((None, tm, n), lambda i, j: (j, i, 0))
    else:
        out_shape = jax.ShapeDtypeStruct((t, nj * n), F32)
        out_spec = pl.BlockSpec((tm, n), lambda i, j: (i, j))
    return pl.pallas_call(
        body, name=name, grid=(t // tm, nj),
        in_specs=[pl.BlockSpec((tm, d), lambda i, j: (i, 0)),
                  pl.BlockSpec((1, d), lambda i, j: (0, 0)),
                  pl.BlockSpec((None, d, n), lambda i, j: (j, 0, 0))],
        out_specs=[pl.BlockSpec((tm, d), lambda i, j: (i, 0)), out_spec],
        out_shape=[jax.ShapeDtypeStruct((t, d), BF16), out_shape],
        scratch_shapes=[pltpu.VMEM((tm, d), BF16)],
        compiler_params=_params(2),
    )(x, g, w)


def mixer_fwd(x, proj, wout, cw, lng, lnb, ws, bias, *, seq, name, tm):
    t, d = x.shape
    tm = min(tm, seq)
    tiles_per_seq = seq // tm

    def body(x_ref, p_ref, cprev_ref, xiprev_ref, wout_ref, cw_ref, lng_ref, lnb_ref, ws_ref, bias_ref,
             merged_ref, x1_ref, mixed_scr):
        keep = jnp.where(pl.program_id(0) % tiles_per_seq == 0, 0.0, 1.0)
        f = _mixer_forward(p_ref, cprev_ref[...] * keep, xiprev_ref[...], cw_ref[...], lng_ref[...],
                           lnb_ref[...], ws_ref, bias_ref, mixed_scr, d)
        mb = f["merged"].astype(BF16)
        merged_ref[...] = mb
        x1_ref[...] = x_ref[...] + jnp.dot(mb, wout_ref[...], preferred_element_type=F32)

    def halo(col):
        return pl.BlockSpec((HALO, d), lambda i: (jnp.maximum(i * (tm // HALO) - 1, 0), col))

    const2 = lambda i: (0, 0)
    return pl.pallas_call(
        body, name=name, grid=(t // tm,),
        in_specs=[pl.BlockSpec((tm, d), lambda i: (i, 0)),
                  pl.BlockSpec((tm, 7 * d), lambda i: (i, 0)),
                  halo(1), halo(2),
                  pl.BlockSpec((d, d), const2),
                  pl.BlockSpec((HALO, d), const2),
                  pl.BlockSpec((1, d), const2),
                  pl.BlockSpec((1, d), const2),
                  pl.BlockSpec((N_GROUPS, CHUNK, CHUNK), lambda i: (0, 0, 0)),
                  pl.BlockSpec((CHUNK, d), const2)],
        out_specs=[pl.BlockSpec((tm, d), lambda i: (i, 0)), pl.BlockSpec((tm, d), lambda i: (i, 0))],
        out_shape=[jax.ShapeDtypeStruct((t, d), BF16), jax.ShapeDtypeStruct((t, d), F32)],
        scratch_shapes=[pltpu.VMEM((tm, d), F32)],
        compiler_params=_params(1),
    )(x, proj, proj, proj, wout, cw, lng, lnb, ws, bias)


def ffn_fwd(x1, up0, wd, cw, *, seq, name, tm):
    t, d = x1.shape
    nj, _, f = up0.shape
    half = nj // 2
    tm = min(tm, seq)
    tiles_per_seq = seq // tm

    def body(x1_ref, up_ref, prev_ref, wd_ref, cw_ref, act_ref, x2_ref):
        keep = jnp.where(pl.program_id(0) % tiles_per_seq == 0, 0.0, 1.0)
        acc = x1_ref[...]
        for k in range(half):
            gate, _, _ = _conv_fwd(up_ref[k], prev_ref[k] * keep, cw_ref[k])
            val, _, _ = _conv_fwd(up_ref[k + half], prev_ref[k + half] * keep, cw_ref[k + half])
            a = (jax.nn.silu(gate) * val).astype(BF16)
            act_ref[k] = a
            acc = acc + jnp.dot(a, wd_ref[k], preferred_element_type=F32)
        x2_ref[...] = acc

    return pl.pallas_call(
        body, name=name, grid=(t // tm,),
        in_specs=[pl.BlockSpec((tm, d), lambda i: (i, 0)),
                  pl.BlockSpec((nj, tm, f), lambda i: (0, i, 0)),
                  pl.BlockSpec((nj, HALO, f), lambda i: (0, jnp.maximum(i * (tm // HALO) - 1, 0), 0)),
                  pl.BlockSpec((half, f, d), lambda i: (0, 0, 0)),
                  pl.BlockSpec((nj, HALO, f), lambda i: (0, 0, 0))],
        out_specs=[pl.BlockSpec((half, tm, f), lambda i: (0, i, 0)), pl.BlockSpec((tm, d), lambda i: (i, 0))],
        out_shape=[jax.ShapeDtypeStruct((half, t, f), BF16), jax.ShapeDtypeStruct((t, d), F32)],
        compiler_params=_params(1),
    )(x1, up0, up0, wd, cw)


def final_loss(x, g, target, *, name, tm):
    t, d = x.shape
    tm = min(tm, t)

    def body(x_ref, g_ref, tgt_ref, dx_ref, dg_ref, loss_ref):
        @pl.when(pl.program_id(0) == 0)
        def _():
            dg_ref[...] = jnp.zeros_like(dg_ref)
            loss_ref[...] = jnp.zeros_like(loss_ref)

        xv = x_ref[...]
        gv = g_ref[...]
        r, n = _rms_stats(xv)
        err = n * gv - tgt_ref[...]
        loss_ref[...] += 0.5 * jnp.sum(jnp.mean(err * err, axis=-1, keepdims=True))
        dy = err * (1.0 / d)
        dn = dy * gv
        dx_ref[...] = r * (dn - n * jnp.mean(dn * n, axis=-1, keepdims=True))
        dg_ref[0:1, :] += _colsum(dy * n)

    return pl.pallas_call(
        body, name=name, grid=(t // tm,),
        in_specs=[pl.BlockSpec((tm, d), lambda i: (i, 0)),
                  pl.BlockSpec((1, d), lambda i: (0, 0)),
                  pl.BlockSpec((tm, d), lambda i: (i, 0))],
        out_specs=[pl.BlockSpec((tm, d), lambda i: (i, 0)),
                   pl.BlockSpec((HALO, d), lambda i: (0, 0)),
                   pl.BlockSpec((8, 128), lambda i: (0, 0))],
        out_shape=[jax.ShapeDtypeStruct((t, d), F32), jax.ShapeDtypeStruct((HALO, d), F32),
                   jax.ShapeDtypeStruct((8, 128), F32)],
        compiler_params=_params(1),
    )(x, g, target)


def ffn_bwd(dx2, up0, wd, cw, *, seq, name, tm):
    t, d = dx2.shape
    nj, _, f = up0.shape
    half = nj // 2
    tm = min(tm, seq)
    tiles_per_seq = seq // tm
    nt = t // tm

    def body(dx_ref, up_ref, prev_ref, wd_ref, cw_ref, dup_ref, dcw_ref, carry_ref):
        i = pl.program_id(0)
        tile = nt - 1 - i

        @pl.when(i == 0)
        def _():
            dcw_ref[...] = jnp.zeros_like(dcw_ref)
            carry_ref[...] = jnp.zeros_like(carry_ref)

        keep_prev = jnp.where(tile % tiles_per_seq == 0, 0.0, 1.0)
        keep_next = jnp.where(tile % tiles_per_seq == tiles_per_seq - 1, 0.0, 1.0)
        dxb = dx_ref[...].astype(BF16)

        def through_conv(j, dup):
            next8 = carry_ref[j] * keep_next
            carry_ref[j] = dup[:HALO]
            dup_ref[j] = _conv_bwd(dup, next8, cw_ref[j]).astype(BF16)

        for k in range(half):
            gate, g1, g2 = _conv_fwd(up_ref[k], prev_ref[k] * keep_prev, cw_ref[k])
            val, v1, v2 = _conv_fwd(up_ref[k + half], prev_ref[k + half] * keep_prev, cw_ref[k + half])
            dact = lax.dot_general(dxb, wd_ref[k], NT_DIMS, preferred_element_type=F32)
            sg = jax.nn.sigmoid(gate)
            dgate = dact * val * (sg * (1.0 + gate * (1.0 - sg)))
            dval = dact * (gate * sg)
            dcw_ref[k, 0:1, :] += _colsum(dgate * g2)
            dcw_ref[k, 1:2, :] += _colsum(dgate * g1)
            dcw_ref[k, 2:3, :] += _colsum(dgate * up_ref[k])
            dcw_ref[k + half, 0:1, :] += _colsum(dval * v2)
            dcw_ref[k + half, 1:2, :] += _colsum(dval * v1)
            dcw_ref[k + half, 2:3, :] += _colsum(dval * up_ref[k + half])
            through_conv(k, dgate)
            through_conv(k + half, dval)

    rev = lambda i: nt - 1 - i
    return pl.pallas_call(
        body, name=name, grid=(nt,),
        in_specs=[pl.BlockSpec((tm, d), lambda i: (rev(i), 0)),
                  pl.BlockSpec((nj, tm, f), lambda i: (0, rev(i), 0)),
                  pl.BlockSpec((nj, HALO, f), lambda i: (0, jnp.maximum(rev(i) * (tm // HALO) - 1, 0), 0)),
                  pl.BlockSpec((half, f, d), lambda i: (0, 0, 0)),
                  pl.BlockSpec((nj, HALO, f), lambda i: (0, 0, 0))],
        out_specs=[pl.BlockSpec((nj, tm, f), lambda i: (0, rev(i), 0)),
                   pl.BlockSpec((nj, HALO, f), lambda i: (0, 0, 0))],
        out_shape=[jax.ShapeDtypeStruct((nj, t, f), BF16), jax.ShapeDtypeStruct((nj, HALO, f), F32)],
        scratch_shapes=[pltpu.VMEM((nj, HALO, f), F32)],
        compiler_params=_params(1),
    )(dx2, up0, up0, wd, cw)


def mixer_bwd(dx1, proj, wout, cw, lng, lnb, ws, wst, bias, *, seq, name, tm):
    t, d = dx1.shape
    tm = min(tm, seq)
    tiles_per_seq = seq // tm
    nt = t // tm
    gd = d // N_GROUPS

    def body(dx_ref, p_ref, cprev_ref, xiprev_ref, wout_ref, cw_ref, lng_ref, lnb_ref, ws_ref, wst_ref, bias_ref,
             dp_ref, dcw_ref, dln_ref, dws_ref, dbs_ref, mixed_scr, dvn_scr, carry_ref, dbs_acc):
        i = pl.program_id(0)
        tile = nt - 1 - i

        @pl.when(i == 0)
        def _():
            dcw_ref[...] = jnp.zeros_like(dcw_ref)
            dln_ref[...] = jnp.zeros_like(dln_ref)
            dws_ref[...] = jnp.zeros_like(dws_ref)
            dbs_acc[...] = jnp.zeros_like(dbs_acc)
            carry_ref[...] = jnp.zeros_like(carry_ref)

        keep_prev = jnp.where(tile % tiles_per_seq == 0, 0.0, 1.0)
        keep_next = jnp.where(tile % tiles_per_seq == tiles_per_seq - 1, 0.0, 1.0)
        cw = cw_ref[...]
        lng = lng_ref[...]
        f = _mixer_forward(p_ref, cprev_ref[...] * keep_prev, xiprev_ref[...], cw, lng, lnb_ref[...],
                           ws_ref, bias_ref, mixed_scr, d)
        dmerged = lax.dot_general(dx_ref[...].astype(BF16), wout_ref[...], NT_DIMS, preferred_element_type=F32)
        sa, sb = f["sa"], f["sb"]
        dp_ref[:, 5 * d:6 * d] = (dmerged * f["ya"] * (sa * (1.0 - sa))).astype(BF16)
        dp_ref[:, 6 * d:7 * d] = (dmerged * f["yb"] * (sb * (1.0 - sb))).astype(BF16)
        dya = dmerged * sa
        dyb = dmerged * sb
        dp_ref[:, 0:d] = (dya * f["conv"]).astype(BF16)
        dconv = dya * f["b"]
        dcw_ref[0:1, :] += _colsum(dconv * f["s2"])
        dcw_ref[1:2, :] += _colsum(dconv * f["s1"])
        dcw_ref[2:3, :] += _colsum(dconv * f["cx"])
        next8 = carry_ref[...] * keep_next
        carry_ref[...] = dconv[:HALO]
        dcx = _conv_bwd(dconv, next8, cw)
        dp_ref[:, d:2 * d] = (dcx * f["xi"]).astype(BF16)
        dp_ref[:, 2 * d:3 * d] = (dcx * f["c"]).astype(BF16)
        dp_ref[:, 3 * d:4 * d] = (dyb * f["mixed"]).astype(BF16)
        dmixed = dyb * f["u"]
        dmb = dmixed.astype(BF16)
        vnb = f["vnb"]
        tril = f["tril"]
        triu = (lax.broadcasted_iota(jnp.int32, (CHUNK, CHUNK), 0)
                <= lax.broadcasted_iota(jnp.int32, (CHUNK, CHUNK), 1))
        dbs_tile = dmixed[0:CHUNK]
        for n in range(1, tm // CHUNK):
            dbs_tile = dbs_tile + dmixed[n * CHUNK:(n + 1) * CHUNK]
        dbs_acc[...] += dbs_tile
        for g in range(N_GROUPS):
            wmt = jnp.where(triu, wst_ref[g], 0.0).astype(BF16)
            cols = slice(g * gd, (g + 1) * gd)
            dw = jnp.zeros((CHUNK, CHUNK), F32)
            for n in range(tm // CHUNK):
                rows = slice(n * CHUNK, (n + 1) * CHUNK)
                dvn_scr[rows, cols] = jnp.dot(wmt, dmb[rows, cols], preferred_element_type=F32)
                dw = dw + lax.dot_general(dmb[rows, cols], vnb[rows, cols], NT_DIMS, preferred_element_type=F32)
            dws_ref[g] += jnp.where(tril, dw, 0.0)
        dvn = dvn_scr[...]
        vhat = f["vhat"]
        dln_ref[0:1, :] += _colsum(dvn * vhat)
        dln_ref[1:2, :] += _colsum(dvn)
        dvh = dvn * lng
        dv = f["rstd"] * (dvh - jnp.mean(dvh, axis=-1, keepdims=True)
                          - vhat * jnp.mean(dvh * vhat, axis=-1, keepdims=True))
        dp_ref[:, 4 * d:5 * d] = dv.astype(BF16)

        @pl.when(i == nt - 1)
        def _():
            for g in range(N_GROUPS):
                cols = slice(g * gd, (g + 1) * gd)
                s = jnp.sum(dbs_acc[:, cols], axis=1, keepdims=True)
                dbs_ref[:, cols] = jnp.broadcast_to(s, (CHUNK, gd))

    rev = lambda i: nt - 1 - i

    def halo(col):
        return pl.BlockSpec((HALO, d), lambda i: (jnp.maximum(rev(i) * (tm // HALO) - 1, 0), col))

    const2 = lambda i: (0, 0)
    const3 = lambda i: (0, 0, 0)
    return pl.pallas_call(
        body, name=name, grid=(nt,),
        in_specs=[pl.BlockSpec((tm, d), lambda i: (rev(i), 0)),
                  pl.BlockSpec((tm, 7 * d), lambda i: (rev(i), 0)),
                  halo(1), halo(2),
                  pl.BlockSpec((d, d), const2),
                  pl.BlockSpec((HALO, d), const2),
                  pl.BlockSpec((1, d), const2),
                  pl.BlockSpec((1, d), const2),
                  pl.BlockSpec((N_GROUPS, CHUNK, CHUNK), const3),
                  pl.BlockSpec((N_GROUPS, CHUNK, CHUNK), const3),
                  pl.BlockSpec((CHUNK, d), const2)],
        out_specs=[pl.BlockSpec((tm, 7 * d), lambda i: (rev(i), 0)),
                   pl.BlockSpec((HALO, d), const2),
                   pl.BlockSpec((HALO, d), const2),
                   pl.BlockSpec((N_GROUPS, CHUNK, CHUNK), const3),
                   pl.BlockSpec((CHUNK, d), const2)],
        out_shape=[jax.ShapeDtypeStruct((t, 7 * d), BF16),
                   jax.ShapeDtypeStruct((HALO, d), F32),
                   jax.ShapeDtypeStruct((HALO, d), F32),
                   jax.ShapeDtypeStruct((N_GROUPS, CHUNK, CHUNK), F32),
                   jax.ShapeDtypeStruct((CHUNK, d), F32)],
        scratch_shapes=[pltpu.VMEM((tm, d), F32), pltpu.VMEM((tm, d), F32),
                        pltpu.VMEM((HALO, d), F32), pltpu.VMEM((CHUNK, d), F32)],
        compiler_params=_params(1),
    )(dx1, proj, proj, proj, wout, cw, lng, lnb, ws, wst, bias)


def dgrad_rms(dy, w, x, g, res, *, name, tm, dy_3d):
    t, d = x.shape
    nj, _, n = w.shape
    tm = min(tm, t)

    def body(dy_ref, w_ref, x_ref, g_ref, res_ref, dx_ref, dg_ref, acc_ref):
        i, j = pl.program_id(0), pl.program_id(1)

        @pl.when((i == 0) & (j == 0))
        def _():
            dg_ref[...] = jnp.zeros_like(dg_ref)

        part = lax.dot_general(dy_ref[...], w_ref[...], NT_DIMS, preferred_element_type=F32)

        @pl.when(j == 0)
        def _():
            acc_ref[...] = part

        @pl.when(j > 0)
        def _():
            acc_ref[...] += part

        @pl.when(j == nj - 1)
        def _():
            dx, dg = _rms_bwd(acc_ref[...], x_ref[...], g_ref[...])
            dx_ref[...] = res_ref[...] + dx
            dg_ref[0:1, :] += dg

    if dy_3d:
        dy_spec = pl.BlockSpec((None, tm, n), lambda i, j: (j, i, 0))
    else:
        dy_spec = pl.BlockSpec((tm, n), lambda i, j: (i, j))
    return pl.pallas_call(
        body, name=name, grid=(t // tm, nj),
        in_specs=[dy_spec,
                  pl.BlockSpec((None, d, n), lambda i, j: (j, 0, 0)),
                  pl.BlockSpec((tm, d), lambda i, j: (i, 0)),
                  pl.BlockSpec((1, d), lambda i, j: (0, 0)),
                  pl.BlockSpec((tm, d), lambda i, j: (i, 0))],
        out_specs=[pl.BlockSpec((tm, d), lambda i, j: (i, 0)), pl.BlockSpec((HALO, d), lambda i, j: (0, 0))],
        out_shape=[jax.ShapeDtypeStruct((t, d), F32), jax.ShapeDtypeStruct((HALO, d), F32)],
        scratch_shapes=[pltpu.VMEM((tm, d), F32)],
        compiler_params=_params(2),
    )(dy, w, x, g, res)


def wgrad(a, b, *, nj, a_mode, b_mode, name, tm):
    def describe(arr, mode):
        if mode == "full":
            return arr.shape[0], arr.shape[1], pl.BlockSpec((tm_, arr.shape[1]), lambda j, s: (s, 0))
        if mode == "cols":
            c = arr.shape[1] // nj
            return arr.shape[0], c, pl.BlockSpec((tm_, c), lambda j, s: (s, j))
        return arr.shape[1], arr.shape[2], pl.BlockSpec((None, tm_, arr.shape[2]), lambda j, s: (j, s, 0))

    t = a.shape[0] if a_mode != "lead" else a.shape[1]
    tm_ = min(tm, t)
    _, k, a_spec = describe(a, a_mode)
    _, n, b_spec = describe(b, b_mode)

    def body(a_ref, b_ref, o_ref):
        part = lax.dot_general(a_ref[...], b_ref[...], TN_DIMS, preferred_element_type=F32)

        @pl.when(pl.program_id(1) == 0)
        def _():
            o_ref[...] = part

        @pl.when(pl.program_id(1) > 0)
        def _():
            o_ref[...] += part

    return pl.pallas_call(
        body, name=name, grid=(nj, t // tm_),
        in_specs=[a_spec, b_spec],
        out_specs=pl.BlockSpec((None, k, n), lambda j, s: (j, 0, 0)),
        out_shape=jax.ShapeDtypeStruct((nj, k, n), F32),
        compiler_params=_params(2),
    )(a, b)


def _adamw_math(w, g, m, v):
    m = ADAM_B1 * m + (1.0 - ADAM_B1) * g
    v = ADAM_B2 * v + (1.0 - ADAM_B2) * (g * g)
    m_hat = m / (1.0 - ADAM_B1 ** ADAM_STEP)
    v_hat = v / (1.0 - ADAM_B2 ** ADAM_STEP)
    delta = -ADAM_LR * (m_hat / (jnp.sqrt(v_hat) + ADAM_EPS) + ADAM_WD * w)
    return delta, m, v


def _row_tile(rows, at_most):
    if rows <= at_most:
        return rows
    return max(k for k in range(8, at_most + 1, 8) if rows % k == 0)


def _sum_in_device_order(ref):
    total = ref[0]
    for s in range(1, N_DEV):
        total = total + ref[s]
    return total


def adamw_sharded(recv0, recv1, w, m, v, *, name, tr):
    _, r, c = w.shape
    tr = _row_tile(r, tr)
    ni = r // tr

    def body(r0_ref, r1_ref, w_ref, m_ref, v_ref, g_ref, d_ref, nm_ref, nv_ref):
        def finish(g):
            delta, nm, nv = _adamw_math(w_ref[...], g, m_ref[...], v_ref[...])
            g_ref[...] = g
            d_ref[...] = delta
            nm_ref[...] = nm
            nv_ref[...] = nv

        @pl.when(pl.program_id(0) == 0)
        def _():
            finish(_sum_in_device_order(r0_ref))

        @pl.when(pl.program_id(0) == 1)
        def _():
            finish(_sum_in_device_order(r1_ref))

    lay = pl.BlockSpec((None, tr, c), lambda l, i: (l, i, 0))
    return pl.pallas_call(
        body, name=name, grid=(2, ni),
        in_specs=[pl.BlockSpec((N_DEV, tr, c), lambda l, i: (0, i * (1 - l) + (ni - 1) * l, 0)),
                  pl.BlockSpec((N_DEV, tr, c), lambda l, i: (0, i * l, 0)),
                  lay, lay, lay],
        out_specs=[lay, lay, lay, lay],
        out_shape=[jax.ShapeDtypeStruct(w.shape, F32)] * 4,
        compiler_params=_params(2),
    )(recv0, recv1, w, m, v)


def adamw_small(g, w, m, v, *, name):
    def body(g_ref, w_ref, m_ref, v_ref, d_ref, nm_ref, nv_ref):
        delta, nm, nv = _adamw_math(w_ref[...], g_ref[...], m_ref[...], v_ref[...])
        d_ref[...] = delta
        nm_ref[...] = nm
        nv_ref[...] = nv

    return pl.pallas_call(
        body, name=name,
        out_shape=[jax.ShapeDtypeStruct(w.shape, F32)] * 3,
        compiler_params=pltpu.CompilerParams(vmem_limit_bytes=VMEM_LIMIT_BYTES),
    )(g, w, m, v)


def sum_devices(parts, *, name, tr):
    _, r, c = parts.shape
    tr = min(tr, r)

    def body(p_ref, o_ref):
        o_ref[...] = _sum_in_device_order(p_ref)

    return pl.pallas_call(
        body, name=name, grid=(r // tr,),
        in_specs=[pl.BlockSpec((N_DEV, tr, c), lambda i: (0, i, 0))],
        out_specs=pl.BlockSpec((tr, c), lambda i: (i, 0)),
        out_shape=jax.ShapeDtypeStruct((r, c), F32),
        compiler_params=_params(1),
    )(parts)


def _my_place():
    return lax.axis_index("x"), lax.axis_index("y"), lax.axis_index("c")


def all_gather(arrays, *, name):
    n = len(arrays)

    def body(*refs):
        ins, outs = refs[:n], refs[n:2 * n]
        send_sems, recv_sems, local_sems = refs[2 * n:]
        x, y, c = _my_place()
        me, sibling = (x, y, c), (x, y, 1 - c)
        chips = [(1 - x, y), (x, 1 - y), (1 - x, 1 - y)]
        waits = []
        for a in range(n):
            def slot(place, a=a):
                px, py, pc = place
                return outs[a].at[4 * px + 2 * py + pc]

            def copy(k, block, to, src=None, a=a, slot=slot):
                return pltpu.make_async_remote_copy(
                    src_ref=slot(block) if src is None else src, dst_ref=slot(block),
                    send_sem=send_sems.at[a, k], recv_sem=recv_sems.at[a, k],
                    device_id=to, device_id_type=MESH)

            mine = pltpu.make_async_copy(ins[a], slot(me), local_sems.at[a])
            mine.start()
            first = [copy(0, me, sibling, src=ins[a])]
            first += [copy(1 + j, me, (*chip, c), src=ins[a]) for j, chip in enumerate(chips)]
            for cp in first:
                cp.start()
            waits.append((copy, mine, first))
        sends = []
        for a in range(n):
            copy, mine, first = waits[a]
            passed = [copy(4 + j, (*chip, c), sibling) for j, chip in enumerate(chips)]
            for j, chip in enumerate(chips):
                copy(1 + j, (*chip, c), me).wait_recv()
                passed[j].start()
            sends.append(first + passed)
        for a in range(n):
            copy, mine, first = waits[a]
            copy(0, sibling, me).wait_recv()
            for j, chip in enumerate(chips):
                copy(4 + j, (*chip, 1 - c), me).wait_recv()
            for cp in sends[a]:
                cp.wait_send()
            mine.wait()

    return pl.pallas_call(
        body, name=name,
        in_specs=[ANY] * n, out_specs=[ANY] * n,
        out_shape=[jax.ShapeDtypeStruct((N_DEV,) + a.shape, a.dtype) for a in arrays],
        scratch_shapes=[pltpu.SemaphoreType.DMA((n, 7)), pltpu.SemaphoreType.DMA((n, 7)),
                        pltpu.SemaphoreType.DMA((n,))],
        compiler_params=pltpu.CompilerParams(has_side_effects=True),
    )(*arrays)


def all_to_all(arrays, *, name):
    n = len(arrays)

    def body(*refs):
        ins, outs = refs[:n], refs[n:2 * n]
        send_sems, recv_sems, local_sems = refs[2 * n:]
        x, y, c = _my_place()
        me = 4 * x + 2 * y + c

        def peer(r):
            fx, fy, fc = (r >> 2) & 1, (r >> 1) & 1, r & 1
            return (1 - x if fx else x, 1 - y if fy else y, 1 - c if fc else c)

        def copy(a, r):
            px, py, pc = peer(r)
            return pltpu.make_async_remote_copy(
                src_ref=ins[a].at[4 * px + 2 * py + pc], dst_ref=outs[a].at[me],
                send_sem=send_sems.at[a, r - 1], recv_sem=recv_sems.at[a, r - 1],
                device_id=(px, py, pc), device_id_type=MESH)

        def landed(a, r):
            px, py, pc = peer(r)
            return pltpu.make_async_remote_copy(
                src_ref=ins[a].at[me], dst_ref=outs[a].at[4 * px + 2 * py + pc],
                send_sem=send_sems.at[a, r - 1], recv_sem=recv_sems.at[a, r - 1],
                device_id=(px, py, pc), device_id_type=MESH)

        mine = [pltpu.make_async_copy(ins[a].at[me], outs[a].at[me], local_sems.at[a]) for a in range(n)]
        for a in range(n):
            mine[a].start()
            for r in range(1, N_DEV):
                copy(a, r).start()
        for a in range(n):
            for r in range(1, N_DEV):
                landed(a, r).wait_recv()
                copy(a, r).wait_send()
            mine[a].wait()

    return pl.pallas_call(
        body, name=name,
        in_specs=[ANY] * n, out_specs=[ANY] * n,
        out_shape=[jax.ShapeDtypeStruct(a.shape, a.dtype) for a in arrays],
        scratch_shapes=[pltpu.SemaphoreType.DMA((n, 7)), pltpu.SemaphoreType.DMA((n, 7)),
                        pltpu.SemaphoreType.DMA((n,))],
        compiler_params=pltpu.CompilerParams(has_side_effects=True),
    )(*arrays)


def _pad_rows(a, rows):
    pad = [(0, 0)] * a.ndim
    pad[-2] = (0, rows - a.shape[-2])
    return jnp.pad(a, pad)


def kernel(x, mix_norm_g, w_in, conv_a_w, ln_v_g, ln_v_b, w_s, b_s, w_out, ffn_norm_g, w_up, conv_ffn_w, w_down, final_norm_g, loss_target, m_mix_norm_g, m_w_in, m_conv_a_w, m_ln_v_g, m_ln_v_b, m_w_s, m_b_s, m_w_out, m_ffn_norm_g, m_w_up, m_conv_ffn_w, m_w_down, m_final_norm_g, v_mix_norm_g, v_w_in, v_conv_a_w, v_ln_v_g, v_ln_v_b, v_w_s, v_b_s, v_w_out, v_ffn_norm_g, v_w_up, v_conv_ffn_w, v_w_down, v_final_norm_g):
    nb, seq, d = x.shape
    t = nb * seq
    depth = w_in.shape[0]
    f = w_up.shape[2]
    me = 4 * lax.axis_index("x") + 2 * lax.axis_index("y") + lax.axis_index("c")
    xt = x.reshape(t, d)
    tgt = loss_target.reshape(t, d)

    conv_pack = jnp.concatenate([_pad_rows(conv_a_w, HALO), _pad_rows(conv_ffn_w, HALO)], axis=-1)
    shards = []
    for l in range(depth):
        shards += [w_in[l].astype(BF16), w_out[l].astype(BF16), w_up[l].astype(BF16), w_down[l].astype(BF16)]
    gathered = all_gather(shards + [conv_pack], name="gather_weights")
    conv_g = gathered[-1]
    ca = conv_g.shape[-1] - f
    layers = []
    for l in range(depth):
        win_g, wout_g, wup_g, wd_g = gathered[4 * l:4 * l + 4]
        cw_a = jnp.transpose(conv_g[:, l, :, :ca], (1, 0, 2)).reshape(HALO, d)
        cw_f = conv_g[:, l, :, ca:]
        layers.append(dict(
            win=win_g, wout=wout_g.reshape(d, d), wup=wup_g,
            wd=wd_g.reshape(N_DEV // 2, 2 * wd_g.shape[1], d),
            cw_a=cw_a, cw_f=cw_f,
            mix_g=mix_norm_g[l][None], ffn_g=ffn_norm_g[l][None],
            lng=ln_v_g[l][None], lnb=ln_v_b[l][None],
            ws=w_s[l], wst=jnp.swapaxes(w_s[l], 1, 2),
            bias=jnp.repeat(b_s[l].T, d // N_GROUPS, axis=1)))

    saved = []
    cur = xt
    for l, p in enumerate(layers):
        h, proj = rms_matmul(cur, p["mix_g"], p["win"], name=f"rms_proj_{l}", tm=1024, out_3d=False)
        merged, x1 = mixer_fwd(cur, proj, p["wout"], p["cw_a"], p["lng"], p["lnb"], p["ws"], p["bias"],
                               seq=seq, name=f"mixer_fwd_{l}", tm=256)
        h2, up0 = rms_matmul(x1, p["ffn_g"], p["wup"], name=f"rms_up_{l}", tm=1024, out_3d=True)
        act, x2 = ffn_fwd(x1, up0, p["wd"], p["cw_f"], seq=seq, name=f"ffn_fwd_{l}", tm=256)
        saved.append(dict(x0=cur, h=h, proj=proj, merged=merged, x1=x1, h2=h2, up0=up0, act=act))
        cur = x2
    dx, d_final_g, loss_tile = final_loss(cur, final_norm_g[None], tgt, name="final_loss", tm=512)

    part = [None] * depth
    for l in reversed(range(depth)):
        p, s = layers[l], saved[l]
        dup0, dcw_f = ffn_bwd(dx, s["up0"], p["wd"], p["cw_f"], seq=seq, name=f"ffn_bwd_{l}", tm=256)
        g_wd = wgrad(s["act"], dx, nj=N_DEV // 2, a_mode="lead", b_mode="full", name=f"wgrad_down_{l}", tm=512)
        g_wup = wgrad(s["h2"], dup0, nj=N_DEV, a_mode="full", b_mode="lead", name=f"wgrad_up_{l}", tm=512)
        dx1, d_ffn_g = dgrad_rms(dup0, p["wup"], s["x1"], p["ffn_g"], dx, name=f"dgrad_up_{l}", tm=1024, dy_3d=True)
        dproj, dcw_a, dln, dws, dbs = mixer_bwd(dx1, s["proj"], p["wout"], p["cw_a"], p["lng"], p["lnb"], p["ws"],
                                                 p["wst"], p["bias"], seq=seq, name=f"mixer_bwd_{l}", tm=128)
        g_wout = wgrad(s["merged"], dx1, nj=1, a_mode="full", b_mode="full", name=f"wgrad_out_{l}", tm=512)
        g_win = wgrad(s["h"], dproj, nj=N_DEV, a_mode="full", b_mode="cols", name=f"wgrad_in_{l}", tm=512)
        dx, d_mix_g = dgrad_rms(dproj, p["win"], s["x0"], p["mix_g"], dx1, name=f"dgrad_in_{l}", tm=1024, dy_3d=False)
        part[l] = dict(
            win=g_win, wout=g_wout.reshape(N_DEV, d // N_DEV, d), wup=g_wup,
            wd=g_wd.reshape(N_DEV, g_wd.shape[1] // 2, d),
            small=jnp.concatenate([
                dws.reshape(N_GROUPS * CHUNK * CHUNK // d, d),
                d_mix_g[0:1], d_ffn_g[0:1], dln[0:2], dcw_a[0:3],
                dbs[:, ::d // N_GROUPS].T.reshape(1, d)], axis=0),
            cw_f=dcw_f.reshape(N_DEV * HALO, f))
    grad_x = dx.reshape(nb, seq, d)

    recv = all_to_all([part[l][k] for l in range(depth) for k in ("win", "wout", "wup", "wd")], name="exchange_grads")
    loss_row = jnp.zeros((1, d), F32).at[0, 0].set(loss_tile[0, 0])
    small = jnp.concatenate([part[l]["small"] for l in range(depth)] + [d_final_g[0:1], loss_row], axis=0)
    small = _pad_rows(small, -(-small.shape[0] // 8) * 8)
    cwf = jnp.concatenate([part[l]["cw_f"] for l in range(depth)], axis=0)
    small_all, cwf_all = all_gather([small, cwf], name="gather_small_grads")
    small_sum = sum_devices(small_all, name="sum_small", tr=512)
    cwf_sum = sum_devices(cwf_all, name="sum_conv_ffn", tr=512)

    rows_ws = N_GROUPS * CHUNK * CHUNK // d
    per_layer = rows_ws + 8
    def small_of(l, a, b):
        return small_sum[l * per_layer + rows_ws + a:l * per_layer + rows_ws + b]
    g_ws = jnp.stack([small_sum[l * per_layer:l * per_layer + rows_ws].reshape(N_GROUPS, CHUNK, CHUNK)
                      for l in range(depth)])
    g_mix = jnp.concatenate([small_of(l, 0, 1) for l in range(depth)])
    g_ffn = jnp.concatenate([small_of(l, 1, 2) for l in range(depth)])
    g_lng = jnp.concatenate([small_of(l, 2, 3) for l in range(depth)])
    g_lnb = jnp.concatenate([small_of(l, 3, 4) for l in range(depth)])
    g_cwa_full = jnp.stack([small_of(l, 4, 7) for l in range(depth)])
    g_cwa = lax.dynamic_slice_in_dim(g_cwa_full, me * ca, ca, axis=2)
    g_bs = jnp.stack([small_of(l, 7, 8).reshape(N_GROUPS, CHUNK) for l in range(depth)])
    g_final = small_sum[depth * per_layer]
    loss = small_sum[depth * per_layer + 1, 0]
    cwf_sum = cwf_sum.reshape(depth, N_DEV, HALO, f)
    g_cwf = lax.dynamic_index_in_dim(cwf_sum, me, axis=1, keepdims=False)[:, :3]

    def big(i, w, m, v, name):
        return adamw_sharded(recv[i], recv[4 + i], w, m, v, name=name, tr=128)

    u_win = big(0, w_in, m_w_in, v_w_in, "adamw_w_in")
    u_wout = big(1, w_out, m_w_out, v_w_out, "adamw_w_out")
    u_wup = big(2, w_up, m_w_up, v_w_up, "adamw_w_up")
    u_wd = big(3, w_down, m_w_down, v_w_down, "adamw_w_down")

    def small_update(g, w, m, v, name):
        shape = w.shape
        two_d = (-1, shape[-1]) if w.ndim > 1 else (1, shape[0])
        out = adamw_small(g.reshape(two_d), w.reshape(two_d), m.reshape(two_d), v.reshape(two_d), name=name)
        return (g.reshape(shape),) + tuple(o.reshape(shape) for o in out)

    u_mix = small_update(g_mix, mix_norm_g, m_mix_norm_g, v_mix_norm_g, "adamw_mix_norm_g")
    u_cwa = small_update(g_cwa, conv_a_w, m_conv_a_w, v_conv_a_w, "adamw_conv_a_w")
    u_lng = small_update(g_lng, ln_v_g, m_ln_v_g, v_ln_v_g, "adamw_ln_v_g")
    u_lnb = small_update(g_lnb, ln_v_b, m_ln_v_b, v_ln_v_b, "adamw_ln_v_b")
    u_ws = small_update(g_ws, w_s, m_w_s, v_w_s, "adamw_w_s")
    u_bs = small_update(g_bs, b_s, m_b_s, v_b_s, "adamw_b_s")
    u_ffn = small_update(g_ffn, ffn_norm_g, m_ffn_norm_g, v_ffn_norm_g, "adamw_ffn_norm_g")
    u_cwf = small_update(g_cwf, conv_ffn_w, m_conv_ffn_w, v_conv_ffn_w, "adamw_conv_ffn_w")
    u_final = small_update(g_final, final_norm_g, m_final_norm_g, v_final_norm_g, "adamw_final_norm_g")

    ordered = [u_mix, u_win, u_cwa, u_lng, u_lnb, u_ws, u_bs, u_wout, u_ffn, u_wup, u_cwf, u_wd, u_final]
    return (loss, grad_x, *[u[0] for u in ordered], *[u[1] for u in ordered],
            *[u[2] for u in ordered], *[u[3] for u in ordered])
```

```python
import functools

import jax
import jax.numpy as jnp
from jax import lax
from jax.experimental import pallas as pl
from jax.experimental.pallas import tpu as pltpu

EPS = 1e-6
CHUNK = 128
N_GROUPS = 8
N_DEV = 8
HALO = 8
ADAM_LR = 0.001
ADAM_B1 = 0.9
ADAM_B2 = 0.999
ADAM_EPS = 1e-08
ADAM_WD = 0.01
ADAM_STEP = 10
VMEM_LIMIT_BYTES = 56 * 1024 * 1024
F32 = jnp.float32
BF16 = jnp.bfloat16
MESH = pl.DeviceIdType.MESH
ANY = pl.BlockSpec(memory_space=pl.ANY)
HBM_SPEC = pl.BlockSpec(memory_space=pltpu.HBM)
SEM_SPEC = pl.BlockSpec(memory_space=pltpu.SEMAPHORE)
DATAFLOW = pltpu.SideEffectType.DATAFLOW_SIDE_EFFECTING
NT_DIMS = (((1,), (1,)), ((), ()))
TN_DIMS = (((0,), (0,)), ((), ()))


def _params(n_grid_axes):
    return pltpu.CompilerParams(dimension_semantics=("arbitrary",) * n_grid_axes,
                                vmem_limit_bytes=VMEM_LIMIT_BYTES)


def _shift_down(cur, prev8, k):
    rolled = pltpu.roll(cur, k, 0)
    prolled = pltpu.roll(prev8, k, 0)
    row = lax.broadcasted_iota(jnp.int32, prev8.shape, 0)
    head = jnp.where(row < k, prolled, rolled[:HALO])
    return jnp.concatenate([head, rolled[HALO:]], axis=0)


def _shift_up(cur, next8, k):
    tm = cur.shape[0]
    rolled = pltpu.roll(cur, tm - k, 0)
    nrolled = pltpu.roll(next8, HALO - k, 0)
    row = lax.broadcasted_iota(jnp.int32, next8.shape, 0)
    tail = jnp.where(row >= HALO - k, nrolled, rolled[tm - HALO:])
    return jnp.concatenate([rolled[:tm - HALO], tail], axis=0)


def _conv_fwd(cur, prev8, cw):
    s1 = _shift_down(cur, prev8, 1)
    s2 = _shift_down(cur, prev8, 2)
    y = s2 * cw[0:1, :] + s1 * cw[1:2, :] + cur * cw[2:3, :]
    return y, s1, s2


def _conv_bwd(d, next8, cw):
    return d * cw[2:3, :] + _shift_up(d, next8, 1) * cw[1:2, :] + _shift_up(d, next8, 2) * cw[0:1, :]


def _colsum(a):
    return jnp.sum(a, axis=0, keepdims=True)


def _rms_stats(xv):
    r = lax.rsqrt(jnp.mean(xv * xv, axis=-1, keepdims=True) + EPS)
    return r, xv * r


def _rms_bwd(dh, xv, g):
    r, n = _rms_stats(xv)
    dn = dh * g
    dx = r * (dn - n * jnp.mean(dn * n, axis=-1, keepdims=True))
    return dx, _colsum(dh * n)


def _mixer_forward(p_ref, cprev, xiprev, cw, lng, lnb, ws_ref, bias_ref, mixed_scr, d):
    tm = p_ref.shape[0]
    b = p_ref[:, 0:d]
    c = p_ref[:, d:2 * d]
    xi = p_ref[:, 2 * d:3 * d]
    u = p_ref[:, 3 * d:4 * d]
    v = p_ref[:, 4 * d:5 * d]
    sa = jax.nn.sigmoid(p_ref[:, 5 * d:6 * d])
    sb = jax.nn.sigmoid(p_ref[:, 6 * d:7 * d])
    cx = c * xi
    conv, s1, s2 = _conv_fwd(cx, cprev * xiprev, cw)
    ya = b * conv
    mu = jnp.mean(v, axis=-1, keepdims=True)
    xc = v - mu
    rstd = lax.rsqrt(jnp.mean(xc * xc, axis=-1, keepdims=True) + EPS)
    vhat = xc * rstd
    vnb = (vhat * lng + lnb).astype(BF16)
    tril = (lax.broadcasted_iota(jnp.int32, (CHUNK, CHUNK), 0)
            >= lax.broadcasted_iota(jnp.int32, (CHUNK, CHUNK), 1))
    gd = d // N_GROUPS
    for g in range(N_GROUPS):
        wm = jnp.where(tril, ws_ref[g], 0.0).astype(BF16)
        cols = slice(g * gd, (g + 1) * gd)
        for n in range(tm // CHUNK):
            rows = slice(n * CHUNK, (n + 1) * CHUNK)
            mixed_scr[rows, cols] = (jnp.dot(wm, vnb[rows, cols], preferred_element_type=F32)
                                     + bias_ref[:, cols])
    mixed = mixed_scr[...]
    yb = u * mixed
    merged = sa * ya + sb * yb
    return dict(b=b, c=c, xi=xi, u=u, sa=sa, sb=sb, cx=cx, s1=s1, s2=s2, conv=conv, ya=ya,
                rstd=rstd, vhat=vhat, vnb=vnb, mixed=mixed, yb=yb, merged=merged, tril=tril)


def rms_matmul(x, g, w, *, name, tm, out_3d):
    t, d = x.shape
    nj, _, n = w.shape
    tm = min(tm, t)

    def body(x_ref, g_ref, w_ref, h_ref, o_ref, hs_ref):
        @pl.when(pl.program_id(1) == 0)
        def _():
            _, nrm = _rms_stats(x_ref[...])
            hv = (nrm * g_ref[...]).astype(BF16)
            hs_ref[...] = hv
            h_ref[...] = hv

        o_ref[...] = jnp.dot(hs_ref[...], w_ref[...], preferred_element_type=F32)

    if out_3d:
        out_shape = jax.ShapeDtypeStruct((nj, t, n), F32)
        out_spec = pl.BlockSpec((None, tm, n), lambda i, j: (j, i, 0))
    else:
        out_shape = jax.ShapeDtypeStruct((t, nj * n), F32)
        out_spec = pl.BlockSpec((tm, n), lambda i, j: (i, j))
    return pl.pallas_call(
        body, name=name, grid=(t // tm, nj),
        in_specs=[pl.BlockSpec((tm, d), lambda i, j: (i, 0)),
                  pl.BlockSpec((1, d), lambda i, j: (0, 0)),
                  pl.BlockSpec((None, d, n), lambda i, j: (j, 0, 0))],
        out_specs=[pl.BlockSpec((tm, d), lambda i, j: (i, 0)), out_spec],
        out_shape=[jax.ShapeDtypeStruct((t, d), BF16), out_shape],
        scratch_shapes=[pltpu.VMEM((tm, d), BF16)],
        compiler_params=_params(2),
    )(x, g, w)


def mixer_fwd(x, proj, wout, cw, lng, lnb, ws, bias, *, seq, name, tm):
    t, d = x.shape
    tm = min(tm, seq)
    tiles_per_seq = seq // tm

    def body(x_ref, p_ref, cprev_ref, xiprev_ref, wout_ref, cw_ref, lng_ref, lnb_ref, ws_ref, bias_ref,
             merged_ref, x1_ref, mixed_scr):
        keep = jnp.where(pl.program_id(0) % tiles_per_seq == 0, 0.0, 1.0)
        f = _mixer_forward(p_ref, cprev_ref[...] * keep, xiprev_ref[...], cw_ref[...], lng_ref[...],
                           lnb_ref[...], ws_ref, bias_ref, mixed_scr, d)
        mb = f["merged"].astype(BF16)
        merged_ref[...] = mb
        x1_ref[...] = x_ref[...] + jnp.dot(mb, wout_ref[...], preferred_element_type=F32)

    def halo(col):
        return pl.BlockSpec((HALO, d), lambda i: (jnp.maximum(i * (tm // HALO) - 1, 0), col))

    const2 = lambda i: (0, 0)
    return pl.pallas_call(
        body, name=name, grid=(t // tm,),
        in_specs=[pl.BlockSpec((tm, d), lambda i: (i, 0)),
                  pl.BlockSpec((tm, 7 * d), lambda i: (i, 0)),
                  halo(1), halo(2),
                  pl.BlockSpec((d, d), const2),
                  pl.BlockSpec((HALO, d), const2),
                  pl.BlockSpec((1, d), const2),
                  pl.BlockSpec((1, d), const2),
                  pl.BlockSpec((N_GROUPS, CHUNK, CHUNK), lambda i: (0, 0, 0)),
                  pl.BlockSpec((CHUNK, d), const2)],
        out_specs=[pl.BlockSpec((tm, d), lambda i: (i, 0)), pl.BlockSpec((tm, d), lambda i: (i, 0))],
        out_shape=[jax.ShapeDtypeStruct((t, d), BF16), jax.ShapeDtypeStruct((t, d), F32)],
        scratch_shapes=[pltpu.VMEM((tm, d), F32)],
        compiler_params=_params(1),
    )(x, proj, proj, proj, wout, cw, lng, lnb, ws, bias)


def ffn_fwd(x1, up0, wd, cw, *, seq, name, tm):
    t, d = x1.shape
    nj, _, f = up0.shape
    half = nj // 2
    tm = min(tm, seq)
    tiles_per_seq = seq // tm

    def body(x1_ref, up_ref, prev_ref, wd_ref, cw_ref, act_ref, x2_ref):
        keep = jnp.where(pl.program_id(0) % tiles_per_seq == 0, 0.0, 1.0)
        acc = x1_ref[...]
        for k in range(half):
            gate, _, _ = _conv_fwd(up_ref[k], prev_ref[k] * keep, cw_ref[k])
            val, _, _ = _conv_fwd(up_ref[k + half], prev_ref[k + half] * keep, cw_ref[k + half])
            a = (jax.nn.silu(gate) * val).astype(BF16)
            act_ref[k] = a
            acc = acc + jnp.dot(a, wd_ref[k], preferred_element_type=F32)
        x2_ref[...] = acc

    return pl.pallas_call(
        body, name=name, grid=(t // tm,),
        in_specs=[pl.BlockSpec((tm, d), lambda i: (i, 0)),
                  pl.BlockSpec((nj, tm, f), lambda i: (0, i, 0)),
                  pl.BlockSpec((nj, HALO, f), lambda i: (0, jnp.maximum(i * (tm // HALO) - 1, 0), 0)),
                  pl.BlockSpec((half, f, d), lambda i: (0, 0, 0)),
                  pl.BlockSpec((nj, HALO, f), lambda i: (0, 0, 0))],
        out_specs=[pl.BlockSpec((half, tm, f), lambda i: (0, i, 0)), pl.BlockSpec((tm, d), lambda i: (i, 0))],
        out_shape=[jax.ShapeDtypeStruct((half, t, f), BF16), jax.ShapeDtypeStruct((t, d), F32)],
        compiler_params=_params(1),
    )(x1, up0, up0, wd, cw)


def final_loss(x, g, target, *, name, tm):
    t, d = x.shape
    tm = min(tm, t)

    def body(x_ref, g_ref, tgt_ref, dx_ref, dg_ref, loss_ref):
        @pl.when(pl.program_id(0) == 0)
        def _():
            dg_ref[...] = jnp.zeros_like(dg_ref)
            loss_ref[...] = jnp.zeros_like(loss_ref)

        xv = x_ref[...]
        gv = g_ref[...]
        r, n = _rms_stats(xv)
        err = n * gv - tgt_ref[...]
        loss_ref[...] += 0.5 * jnp.sum(jnp.mean(err * err, axis=-1, keepdims=True))
        dy = err * (1.0 / d)
        dn = dy * gv
        dx_ref[...] = r * (dn - n * jnp.mean(dn * n, axis=-1, keepdims=True))
        dg_ref[0:1, :] += _colsum(dy * n)

    return pl.pallas_call(
        body, name=name, grid=(t // tm,),
        in_specs=[pl.BlockSpec((tm, d), lambda i: (i, 0)),
                  pl.BlockSpec((1, d), lambda i: (0, 0)),
                  pl.BlockSpec((tm, d), lambda i: (i, 0))],
        out_specs=[pl.BlockSpec((tm, d), lambda i: (i, 0)),
                   pl.BlockSpec((HALO, d), lambda i: (0, 0)),
                   pl.BlockSpec((8, 128), lambda i: (0, 0))],
        out_shape=[jax.ShapeDtypeStruct((t, d), F32), jax.ShapeDtypeStruct((HALO, d), F32),
                   jax.ShapeDtypeStruct((8, 128), F32)],
        compiler_params=_params(1),
    )(x, g, target)


def ffn_bwd(dx2, up0, wd, cw, *, seq, name, tm):
    t, d = dx2.shape
    nj, _, f = up0.shape
    half = nj // 2
    tm = min(tm, seq)
    tiles_per_seq = seq // tm
    nt = t // tm

    def body(dx_ref, up_ref, prev_ref, wd_ref, cw_ref, dup_ref, dcw_ref, carry_ref):
        i = pl.program_id(0)
        tile = nt - 1 - i

        @pl.when(i == 0)
        def _():
            dcw_ref[...] = jnp.zeros_like(dcw_ref)
            carry_ref[...] = jnp.zeros_like(carry_ref)

        keep_prev = jnp.where(tile % tiles_per_seq == 0, 0.0, 1.0)
        keep_next = jnp.where(tile % tiles_per_seq == tiles_per_seq - 1, 0.0, 1.0)
        dxb = dx_ref[...].astype(BF16)

        def through_conv(j, dup):
            next8 = carry_ref[j] * keep_next
            carry_ref[j] = dup[:HALO]
            dup_ref[j] = _conv_bwd(dup, next8, cw_ref[j]).astype(BF16)

        for k in range(half):
            gate, g1, g2 = _conv_fwd(up_ref[k], prev_ref[k] * keep_prev, cw_ref[k])
            val, v1, v2 = _conv_fwd(up_ref[k + half], prev_ref[k + half] * keep_prev, cw_ref[k + half])
            dact = lax.dot_general(dxb, wd_ref[k], NT_DIMS, preferred_element_type=F32)
            sg = jax.nn.sigmoid(gate)
            dgate = dact * val * (sg * (1.0 + gate * (1.0 - sg)))
            dval = dact * (gate * sg)
            dcw_ref[k, 0:1, :] += _colsum(dgate * g2)
            dcw_ref[k, 1:2, :] += _colsum(dgate * g1)
            dcw_ref[k, 2:3, :] += _colsum(dgate * up_ref[k])
            dcw_ref[k + half, 0:1, :] += _colsum(dval * v2)
            dcw_ref[k + half, 1:2, :] += _colsum(dval * v1)
            dcw_ref[k + half, 2:3, :] += _colsum(dval * up_ref[k + half])
            through_conv(k, dgate)
            through_conv(k + half, dval)

    rev = lambda i: nt - 1 - i
    return pl.pallas_call(
        body, name=name, grid=(nt,),
        in_specs=[pl.BlockSpec((tm, d), lambda i: (rev(i), 0)),
                  pl.BlockSpec((nj, tm, f), lambda i: (0, rev(i), 0)),
                  pl.BlockSpec((nj, HALO, f), lambda i: (0, jnp.maximum(rev(i) * (tm // HALO) - 1, 0), 0)),
                  pl.BlockSpec((half, f, d), lambda i: (0, 0, 0)),
                  pl.BlockSpec((nj, HALO, f), lambda i: (0, 0, 0))],
        out_specs=[pl.BlockSpec((nj, tm, f), lambda i: (0, rev(i), 0)),
                   pl.BlockSpec((nj, HALO, f), lambda i: (0, 0, 0))],
        out_shape=[jax.ShapeDtypeStruct((nj, t, f), BF16), jax.ShapeDtypeStruct((nj, HALO, f), F32)],
        scratch_shapes=[pltpu.VMEM((nj, HALO, f), F32)],
        compiler_params=_params(1),
    )(dx2, up0, up0, wd, cw)


def mixer_bwd(dx1, proj, wout, cw, lng, lnb, ws, wst, bias, *, seq, name, tm):
    t, d = dx1.shape
    tm = min(tm, seq)
    tiles_per_seq = seq // tm
    nt = t // tm
    gd = d // N_GROUPS

    def body(dx_ref, p_ref, cprev_ref, xiprev_ref, wout_ref, cw_ref, lng_ref, lnb_ref, ws_ref, wst_ref, bias_ref,
             dp_ref, dcw_ref, dln_ref, dws_ref, dbs_ref, mixed_scr, dvn_scr, carry_ref, dbs_acc):
        i = pl.program_id(0)
        tile = nt - 1 - i

        @pl.when(i == 0)
        def _():
            dcw_ref[...] = jnp.zeros_like(dcw_ref)
            dln_ref[...] = jnp.zeros_like(dln_ref)
            dws_ref[...] = jnp.zeros_like(dws_ref)
            dbs_acc[...] = jnp.zeros_like(dbs_acc)
            carry_ref[...] = jnp.zeros_like(carry_ref)

        keep_prev = jnp.where(tile % tiles_per_seq == 0, 0.0, 1.0)
        keep_next = jnp.where(tile % tiles_per_seq == tiles_per_seq - 1, 0.0, 1.0)
        cw = cw_ref[...]
        lng = lng_ref[...]
        f = _mixer_forward(p_ref, cprev_ref[...] * keep_prev, xiprev_ref[...], cw, lng, lnb_ref[...],
                           ws_ref, bias_ref, mixed_scr, d)
        dmerged = lax.dot_general(dx_ref[...].astype(BF16), wout_ref[...], NT_DIMS, preferred_element_type=F32)
        sa, sb = f["sa"], f["sb"]
        dp_ref[:, 5 * d:6 * d] = (dmerged * f["ya"] * (sa * (1.0 - sa))).astype(BF16)
        dp_ref[:, 6 * d:7 * d] = (dmerged * f["yb"] * (sb * (1.0 - sb))).astype(BF16)
        dya = dmerged * sa
        dyb = dmerged * sb
        dp_ref[:, 0:d] = (dya * f["conv"]).astype(BF16)
        dconv = dya * f["b"]
        dcw_ref[0:1, :] += _colsum(dconv * f["s2"])
        dcw_ref[1:2, :] += _colsum(dconv * f["s1"])
        dcw_ref[2:3, :] += _colsum(dconv * f["cx"])
        next8 = carry_ref[...] * keep_next
        carry_ref[...] = dconv[:HALO]
        dcx = _conv_bwd(dconv, next8, cw)
        dp_ref[:, d:2 * d] = (dcx * f["xi"]).astype(BF16)
        dp_ref[:, 2 * d:3 * d] = (dcx * f["c"]).astype(BF16)
        dp_ref[:, 3 * d:4 * d] = (dyb * f["mixed"]).astype(BF16)
        dmixed = dyb * f["u"]
        dmb = dmixed.astype(BF16)
        vnb = f["vnb"]
        tril = f["tril"]
        triu = (lax.broadcasted_iota(jnp.int32, (CHUNK, CHUNK), 0)
                <= lax.broadcasted_iota(jnp.int32, (CHUNK, CHUNK), 1))
        dbs_tile = dmixed[0:CHUNK]
        for n in range(1, tm // CHUNK):
            dbs_tile = dbs_tile + dmixed[n * CHUNK:(n + 1) * CHUNK]
        dbs_acc[...] += dbs_tile
        for g in range(N_GROUPS):
            wmt = jnp.where(triu, wst_ref[g], 0.0).astype(BF16)
            cols = slice(g * gd, (g + 1) * gd)
            dw = jnp.zeros((CHUNK, CHUNK), F32)
            for n in range(tm // CHUNK):
                rows = slice(n * CHUNK, (n + 1) * CHUNK)
                dvn_scr[rows, cols] = jnp.dot(wmt, dmb[rows, cols], preferred_element_type=F32)
                dw = dw + lax.dot_general(dmb[rows, cols], vnb[rows, cols], NT_DIMS, preferred_element_type=F32)
            dws_ref[g] += jnp.where(tril, dw, 0.0)
        dvn = dvn_scr[...]
        vhat = f["vhat"]
        dln_ref[0:1, :] += _colsum(dvn * vhat)
        dln_ref[1:2, :] += _colsum(dvn)
        dvh = dvn * lng
        dv = f["rstd"] * (dvh - jnp.mean(dvh, axis=-1, keepdims=True)
                          - vhat * jnp.mean(dvh * vhat, axis=-1, keepdims=True))
        dp_ref[:, 4 * d:5 * d] = dv.astype(BF16)

        @pl.when(i == nt - 1)
        def _():
            for g in range(N_GROUPS):
                cols = slice(g * gd, (g + 1) * gd)
                s = jnp.sum(dbs_acc[:, cols], axis=1, keepdims=True)
                dbs_ref[:, cols] = jnp.broadcast_to(s, (CHUNK, gd))

    rev = lambda i: nt - 1 - i

    def halo(col):
        return pl.BlockSpec((HALO, d), lambda i: (jnp.maximum(rev(i) * (tm // HALO) - 1, 0), col))

    const2 = lambda i: (0, 0)
    const3 = lambda i: (0, 0, 0)
    return pl.pallas_call(
        body, name=name, grid=(nt,),
        in_specs=[pl.BlockSpec((tm, d), lambda i: (rev(i), 0)),
                  pl.BlockSpec((tm, 7 * d), lambda i: (rev(i), 0)),
                  halo(1), halo(2),
                  pl.BlockSpec((d, d), const2),
                  pl.BlockSpec((HALO, d), const2),
                  pl.BlockSpec((1, d), const2),
                  pl.BlockSpec((1, d), const2),
                  pl.BlockSpec((N_GROUPS, CHUNK, CHUNK), const3),
                  pl.BlockSpec((N_GROUPS, CHUNK, CHUNK), const3),
                  pl.BlockSpec((CHUNK, d), const2)],
        out_specs=[pl.BlockSpec((tm, 7 * d), lambda i: (rev(i), 0)),
                   pl.BlockSpec((HALO, d), const2),
                   pl.BlockSpec((HALO, d), const2),
                   pl.BlockSpec((N_GROUPS, CHUNK, CHUNK), const3),
                   pl.BlockSpec((CHUNK, d), const2)],
        out_shape=[jax.ShapeDtypeStruct((t, 7 * d), BF16),
                   jax.ShapeDtypeStruct((HALO, d), F32),
                   jax.ShapeDtypeStruct((HALO, d), F32),
                   jax.ShapeDtypeStruct((N_GROUPS, CHUNK, CHUNK), F32),
                   jax.ShapeDtypeStruct((CHUNK, d), F32)],
        scratch_shapes=[pltpu.VMEM((tm, d), F32), pltpu.VMEM((tm, d), F32),
                        pltpu.VMEM((HALO, d), F32), pltpu.VMEM((CHUNK, d), F32)],
        compiler_params=_params(1),
    )(dx1, proj, proj, proj, wout, cw, lng, lnb, ws, wst, bias)


def dgrad_rms(dy, w, x, g, res, *, name, tm, dy_3d):
    t, d = x.shape
    nj, _, n = w.shape
    tm = min(tm, t)

    def body(dy_ref, w_ref, x_ref, g_ref, res_ref, dx_ref, dg_ref, acc_ref):
        i, j = pl.program_id(0), pl.program_id(1)

        @pl.when((i == 0) & (j == 0))
        def _():
            dg_ref[...] = jnp.zeros_like(dg_ref)

        part = lax.dot_general(dy_ref[...], w_ref[...], NT_DIMS, preferred_element_type=F32)

        @pl.when(j == 0)
        def _():
            acc_ref[...] = part

        @pl.when(j > 0)
        def _():
            acc_ref[...] += part

        @pl.when(j == nj - 1)
        def _():
            dx, dg = _rms_bwd(acc_ref[...], x_ref[...], g_ref[...])
            dx_ref[...] = res_ref[...] + dx
            dg_ref[0:1, :] += dg

    if dy_3d:
        dy_spec = pl.BlockSpec((None, tm, n), lambda i, j: (j, i, 0))
    else:
        dy_spec = pl.BlockSpec((tm, n), lambda i, j: (i, j))
    return pl.pallas_call(
        body, name=name, grid=(t // tm, nj),
        in_specs=[dy_spec,
                  pl.BlockSpec((None, d, n), lambda i, j: (j, 0, 0)),
                  pl.BlockSpec((tm, d), lambda i, j: (i, 0)),
                  pl.BlockSpec((1, d), lambda i, j: (0, 0)),
                  pl.BlockSpec((tm, d), lambda i, j: (i, 0))],
        out_specs=[pl.BlockSpec((tm, d), lambda i, j: (i, 0)), pl.BlockSpec((HALO, d), lambda i, j: (0, 0))],
        out_shape=[jax.ShapeDtypeStruct((t, d), F32), jax.ShapeDtypeStruct((HALO, d), F32)],
        scratch_shapes=[pltpu.VMEM((tm, d), F32)],
        compiler_params=_params(2),
    )(dy, w, x, g, res)


def wgrad(a, b, *, nj, a_mode, b_mode, name, tm):
    def describe(arr, mode):
        if mode == "full":
            return arr.shape[0], arr.shape[1], pl.BlockSpec((tm_, arr.shape[1]), lambda j, s: (s, 0))
        if mode == "cols":
            c = arr.shape[1] // nj
            return arr.shape[0], c, pl.BlockSpec((tm_, c), lambda j, s: (s, j))
        return arr.shape[1], arr.shape[2], pl.BlockSpec((None, tm_, arr.shape[2]), lambda j, s: (j, s, 0))

    t = a.shape[0] if a_mode != "lead" else a.shape[1]
    tm_ = min(tm, t)
    _, k, a_spec = describe(a, a_mode)
    _, n, b_spec = describe(b, b_mode)

    def body(a_ref, b_ref, o_ref):
        part = lax.dot_general(a_ref[...], b_ref[...], TN_DIMS, preferred_element_type=F32)

        @pl.when(pl.program_id(1) == 0)
        def _():
            o_ref[...] = part

        @pl.when(pl.program_id(1) > 0)
        def _():
            o_ref[...] += part

    return pl.pallas_call(
        body, name=name, grid=(nj, t // tm_),
        in_specs=[a_spec, b_spec],
        out_specs=pl.BlockSpec((None, k, n), lambda j, s: (j, 0, 0)),
        out_shape=jax.ShapeDtypeStruct((nj, k, n), F32),
        compiler_params=_params(2),
    )(a, b)


def _adamw_math(w, g, m, v):
    m = ADAM_B1 * m + (1.0 - ADAM_B1) * g
    v = ADAM_B2 * v + (1.0 - ADAM_B2) * (g * g)
    m_hat = m / (1.0 - ADAM_B1 ** ADAM_STEP)
    v_hat = v / (1.0 - ADAM_B2 ** ADAM_STEP)
    delta = -ADAM_LR * (m_hat / (jnp.sqrt(v_hat) + ADAM_EPS) + ADAM_WD * w)
    return delta, m, v


def _row_tile(rows, at_most):
    if rows <= at_most:
        return rows
    return max(k for k in range(8, at_most + 1, 8) if rows % k == 0)


def _sum_in_device_order(ref):
    total = ref[0]
    for s in range(1, N_DEV):
        total = total + ref[s]
    return total


def adamw_sharded(me, own0, recv0, own1, recv1, w, m, v, *, name, tr):
    _, r, c = w.shape
    tr = _row_tile(r, tr)
    ni = r // tr

    def body(me_ref, o0_ref, r0_ref, o1_ref, r1_ref, w_ref, m_ref, v_ref, g_ref, d_ref, nm_ref, nv_ref):
        def finish(own_ref, recv_ref):
            g = None
            for s in range(N_DEV):
                term = jnp.where(me_ref[0] == s, own_ref[...], recv_ref[s])
                g = term if g is None else g + term
            delta, nm, nv = _adamw_math(w_ref[...], g, m_ref[...], v_ref[...])
            g_ref[...] = g
            d_ref[...] = delta
            nm_ref[...] = nm
            nv_ref[...] = nv

        @pl.when(pl.program_id(0) == 0)
        def _():
            finish(o0_ref, r0_ref)

        @pl.when(pl.program_id(0) == 1)
        def _():
            finish(o1_ref, r1_ref)

    row0 = lambda l, i: i * (1 - l) + (ni - 1) * l
    row1 = lambda l, i: i * l
    lay = pl.BlockSpec((None, tr, c), lambda l, i, me_ref: (l, i, 0))
    grid_spec = pltpu.PrefetchScalarGridSpec(
        num_scalar_prefetch=1, grid=(2, ni),
        in_specs=[pl.BlockSpec((None, tr, c), lambda l, i, me_ref: (me_ref[0], row0(l, i), 0)),
                  pl.BlockSpec((N_DEV, tr, c), lambda l, i, me_ref: (0, row0(l, i), 0)),
                  pl.BlockSpec((None, tr, c), lambda l, i, me_ref: (me_ref[0], row1(l, i), 0)),
                  pl.BlockSpec((N_DEV, tr, c), lambda l, i, me_ref: (0, row1(l, i), 0)),
                  lay, lay, lay],
        out_specs=[lay, lay, lay, lay])
    return pl.pallas_call(
        body, name=name, grid_spec=grid_spec,
        out_shape=[jax.ShapeDtypeStruct(w.shape, F32)] * 4,
        compiler_params=_params(2),
    )(me, own0, recv0, own1, recv1, w, m, v)


def adamw_small(g, w, m, v, *, name):
    def body(g_ref, w_ref, m_ref, v_ref, d_ref, nm_ref, nv_ref):
        delta, nm, nv = _adamw_math(w_ref[...], g_ref[...], m_ref[...], v_ref[...])
        d_ref[...] = delta
        nm_ref[...] = nm
        nv_ref[...] = nv

    return pl.pallas_call(
        body, name=name,
        out_shape=[jax.ShapeDtypeStruct(w.shape, F32)] * 3,
        compiler_params=pltpu.CompilerParams(vmem_limit_bytes=VMEM_LIMIT_BYTES),
    )(g, w, m, v)


def sum_devices(parts, *, name, tr):
    _, r, c = parts.shape
    tr = min(tr, r)

    def body(p_ref, o_ref):
        o_ref[...] = _sum_in_device_order(p_ref)

    return pl.pallas_call(
        body, name=name, grid=(r // tr,),
        in_specs=[pl.BlockSpec((N_DEV, tr, c), lambda i: (0, i, 0))],
        out_specs=pl.BlockSpec((tr, c), lambda i: (i, 0)),
        out_shape=jax.ShapeDtypeStruct((r, c), F32),
        compiler_params=_params(1),
    )(parts)


def _my_place():
    return lax.axis_index("x"), lax.axis_index("y"), lax.axis_index("c")


def all_gather(arrays, *, name):
    n = len(arrays)

    def body(*refs):
        ins, outs = refs[:n], refs[n:2 * n]
        send_sems, recv_sems, local_sems = refs[2 * n:]
        x, y, c = _my_place()
        me, sibling = (x, y, c), (x, y, 1 - c)
        chips = [(1 - x, y), (x, 1 - y), (1 - x, 1 - y)]
        waits = []
        for a in range(n):
            def slot(place, a=a):
                px, py, pc = place
                return outs[a].at[4 * px + 2 * py + pc]

            def copy(k, block, to, src=None, a=a, slot=slot):
                return pltpu.make_async_remote_copy(
                    src_ref=slot(block) if src is None else src, dst_ref=slot(block),
                    send_sem=send_sems.at[a, k], recv_sem=recv_sems.at[a, k],
                    device_id=to, device_id_type=MESH)

            mine = pltpu.make_async_copy(ins[a], slot(me), local_sems.at[a])
            mine.start()
            first = [copy(0, me, sibling, src=ins[a])]
            first += [copy(1 + j, me, (*chip, c), src=ins[a]) for j, chip in enumerate(chips)]
            for cp in first:
                cp.start()
            waits.append((copy, mine, first))
        sends = []
        for a in range(n):
            copy, mine, first = waits[a]
            passed = [copy(4 + j, (*chip, c), sibling) for j, chip in enumerate(chips)]
            for j, chip in enumerate(chips):
                copy(1 + j, (*chip, c), me).wait_recv()
                passed[j].start()
            sends.append(first + passed)
        for a in range(n):
            copy, mine, first = waits[a]
            copy(0, sibling, me).wait_recv()
            for j, chip in enumerate(chips):
                copy(4 + j, (*chip, 1 - c), me).wait_recv()
            for cp in sends[a]:
                cp.wait_send()
            mine.wait()

    return pl.pallas_call(
        body, name=name,
        in_specs=[ANY] * n, out_specs=[ANY] * n,
        out_shape=[jax.ShapeDtypeStruct((N_DEV,) + a.shape, a.dtype) for a in arrays],
        scratch_shapes=[pltpu.SemaphoreType.DMA((n, 7)), pltpu.SemaphoreType.DMA((n, 7)),
                        pltpu.SemaphoreType.DMA((n,))],
        compiler_params=pltpu.CompilerParams(has_side_effects=True),
    )(*arrays)


def _peer_place(r, x, y, c):
    fx, fy, fc = (r >> 2) & 1, (r >> 1) & 1, r & 1
    return (1 - x if fx else x, 1 - y if fy else y, 1 - c if fc else c)


def own_slots(blocks, *, name):
    n = len(blocks)

    def body(*refs):
        ins, outs, sems = refs[:n], refs[n:2 * n], refs[2 * n]
        x, y, c = _my_place()
        copies = [pltpu.make_async_copy(ins[a], outs[a].at[4 * x + 2 * y + c], sems.at[a]) for a in range(n)]
        for cp in copies:
            cp.start()
        for cp in copies:
            cp.wait()

    return pl.pallas_call(
        body, name=name, in_specs=[ANY] * n, out_specs=[ANY] * n,
        out_shape=[jax.ShapeDtypeStruct((N_DEV,) + b.shape, b.dtype) for b in blocks],
        scratch_shapes=[pltpu.SemaphoreType.DMA((n,))],
    )(*blocks)


def _split_copy(r, src, land, send_sem, recv_sem, scatter, arriving):
    x, y, c = _my_place()
    px, py, pc = _peer_place(r, x, y, c)
    theirs, mine = 4 * px + 2 * py + pc, 4 * x + 2 * y + c
    return pltpu.make_async_remote_copy(
        src_ref=src.at[theirs] if scatter else src, dst_ref=land.at[theirs if arriving else mine],
        send_sem=send_sem.at[r - 1], recv_sem=recv_sem.at[r - 1],
        device_id=(px, py, pc), device_id_type=MESH)


def start_copies(srcs, lands, *, scatter, name):
    n = len(srcs)

    def body(*refs):
        src, land = refs[:n], refs[n:2 * n]
        send_sems, recv_sems = refs[2 * n:3 * n], refs[3 * n:4 * n]
        token = refs[6 * n]
        for a in range(n):
            for r in range(1, N_DEV):
                _split_copy(r, src[a], land[a], send_sems[a], recv_sems[a], scatter, False).start()
        token[...] = jnp.zeros_like(token)

    outs = pl.pallas_call(
        body, name=name,
        in_specs=[HBM_SPEC] * (2 * n),
        out_specs=[SEM_SPEC] * (2 * n) + [HBM_SPEC] * (2 * n) + [pl.BlockSpec(memory_space=pltpu.VMEM)],
        out_shape=([pltpu.SemaphoreType.DMA((N_DEV - 1,))] * (2 * n)
                   + [pltpu.HBM(a.shape, a.dtype) for a in list(srcs) + list(lands)]
                   + [jax.ShapeDtypeStruct((8, 128), F32)]),
        input_output_aliases={i: 2 * n + i for i in range(2 * n)},
        compiler_params=pltpu.CompilerParams(has_side_effects=DATAFLOW),
    )(*[pltpu.with_memory_space_constraint(a, pltpu.HBM) for a in list(srcs) + list(lands)])
    return dict(send=outs[:n], recv=outs[n:2 * n], src=list(outs[2 * n:3 * n]), land=list(outs[3 * n:4 * n]),
                token=outs[4 * n], scatter=scatter)


def finish_copies(started, which, after, *, name):
    n = len(which)
    scatter = started["scatter"]

    def body(*refs):
        src, land = refs[:n], refs[n:2 * n]
        send_sems, recv_sems = refs[2 * n:3 * n], refs[3 * n:4 * n]
        for a in range(n):
            for r in range(1, N_DEV):
                cp = _split_copy(r, src[a], land[a], send_sems[a], recv_sems[a], scatter, True)
                cp.wait_send()
                cp.wait_recv()

    srcs = [started["src"][i] for i in which]
    lands = [started["land"][i] for i in which]
    outs = pl.pallas_call(
        body, name=name,
        in_specs=[HBM_SPEC] * (2 * n) + [SEM_SPEC] * (2 * n) + [ANY],
        out_specs=[HBM_SPEC] * (2 * n),
        out_shape=[pltpu.HBM(a.shape, a.dtype) for a in srcs + lands],
        input_output_aliases={i: i for i in range(2 * n)},
        compiler_params=pltpu.CompilerParams(has_side_effects=DATAFLOW),
    )(*srcs, *lands, *[started["send"][i] for i in which], *[started["recv"][i] for i in which], after)
    return list(outs[:n]), list(outs[n:])


def _pad_rows(a, rows):
    pad = [(0, 0)] * a.ndim
    pad[-2] = (0, rows - a.shape[-2])
    return jnp.pad(a, pad)


def kernel(x, mix_norm_g, w_in, conv_a_w, ln_v_g, ln_v_b, w_s, b_s, w_out, ffn_norm_g, w_up, conv_ffn_w, w_down, final_norm_g, loss_target, m_mix_norm_g, m_w_in, m_conv_a_w, m_ln_v_g, m_ln_v_b, m_w_s, m_b_s, m_w_out, m_ffn_norm_g, m_w_up, m_conv_ffn_w, m_w_down, m_final_norm_g, v_mix_norm_g, v_w_in, v_conv_a_w, v_ln_v_g, v_ln_v_b, v_w_s, v_b_s, v_w_out, v_ffn_norm_g, v_w_up, v_conv_ffn_w, v_w_down, v_final_norm_g):
    nb, seq, d = x.shape
    t = nb * seq
    depth = w_in.shape[0]
    f = w_up.shape[2]
    me = 4 * lax.axis_index("x") + 2 * lax.axis_index("y") + lax.axis_index("c")
    xt = x.reshape(t, d)
    tgt = loss_target.reshape(t, d)

    conv_pack = jnp.concatenate([_pad_rows(conv_a_w, HALO), _pad_rows(conv_ffn_w, HALO)], axis=-1)
    shards, slot_of = [], {}
    for l in range(depth):
        for key, w in (("win", w_in), ("conv", None), ("wout", w_out), ("wup", w_up), ("wd", w_down)):
            if key == "conv":
                if l == 0:
                    slot_of["conv"] = len(shards)
                    shards.append(conv_pack)
                continue
            slot_of[key, l] = len(shards)
            shards.append(w[l].astype(BF16))
    gathering = start_copies(shards, own_slots(shards, name="own_weight_slots"), scatter=False, name="gather_start")

    def gathered(keys, after, name):
        return finish_copies(gathering, [slot_of[k] for k in keys], after, name=name)[1]

    saved, layers = [], []
    cur = xt
    after = gathering["token"]
    for l in range(depth):
        p = dict(mix_g=mix_norm_g[l][None], ffn_g=ffn_norm_g[l][None], lng=ln_v_g[l][None], lnb=ln_v_b[l][None],
                 ws=w_s[l], wst=jnp.swapaxes(w_s[l], 1, 2),
                 bias=jnp.repeat(b_s[l].T, d // N_GROUPS, axis=1))
        (p["win"],) = gathered([("win", l)], after, f"wait_w_in_{l}")
        h, proj = rms_matmul(cur, p["mix_g"], p["win"], name=f"rms_proj_{l}", tm=1024, out_3d=False)
        if l == 0:
            wout_g, conv_g = gathered([("wout", l), "conv"], h, f"wait_w_out_{l}")
            ca = conv_g.shape[-1] - f
        else:
            (wout_g,) = gathered([("wout", l)], h, f"wait_w_out_{l}")
        p["wout"] = wout_g.reshape(d, d)
        p["cw_a"] = jnp.transpose(conv_g[:, l, :, :ca], (1, 0, 2)).reshape(HALO, d)
        p["cw_f"] = conv_g[:, l, :, ca:]
        merged, x1 = mixer_fwd(cur, proj, p["wout"], p["cw_a"], p["lng"], p["lnb"], p["ws"], p["bias"],
                               seq=seq, name=f"mixer_fwd_{l}", tm=256)
        (p["wup"],) = gathered([("wup", l)], merged, f"wait_w_up_{l}")
        h2, up0 = rms_matmul(x1, p["ffn_g"], p["wup"], name=f"rms_up_{l}", tm=1024, out_3d=True)
        (wd_g,) = gathered([("wd", l)], h2, f"wait_w_down_{l}")
        p["wd"] = wd_g.reshape(N_DEV // 2, 2 * wd_g.shape[1], d)
        act, x2 = ffn_fwd(x1, up0, p["wd"], p["cw_f"], seq=seq, name=f"ffn_fwd_{l}", tm=256)
        saved.append(dict(x0=cur, h=h, proj=proj, merged=merged, x1=x1, h2=h2, up0=up0, act=act))
        layers.append(p)
        cur, after = x2, act
    dx, d_final_g, loss_tile = final_loss(cur, final_norm_g[None], tgt, name="final_loss", tm=512)

    def exchange(parts, name):
        return start_copies(parts, [lax.empty(a.shape, a.dtype) for a in parts], scatter=True, name=name)

    def tied(g, started):
        return g + started["token"][0:1, 0:1]

    part = [None] * depth
    for l in reversed(range(depth)):
        p, s = layers[l], saved[l]
        dup0, dcw_f = ffn_bwd(dx, s["up0"], p["wd"], p["cw_f"], seq=seq, name=f"ffn_bwd_{l}", tm=256)
        g_wd = wgrad(s["act"], dx, nj=N_DEV // 2, a_mode="lead", b_mode="full", name=f"wgrad_down_{l}", tm=512)
        g_wup = wgrad(s["h2"], dup0, nj=N_DEV, a_mode="full", b_mode="lead", name=f"wgrad_up_{l}", tm=512)
        ffn_ex = exchange([g_wd.reshape(N_DEV, g_wd.shape[1] // 2, d), g_wup], f"exchange_ffn_{l}")
        dx1, d_ffn_g = dgrad_rms(dup0, p["wup"], s["x1"], tied(p["ffn_g"], ffn_ex), dx,
                                 name=f"dgrad_up_{l}", tm=1024, dy_3d=True)
        dproj, dcw_a, dln, dws, dbs = mixer_bwd(dx1, s["proj"], p["wout"], p["cw_a"], p["lng"], p["lnb"], p["ws"],
                                                 p["wst"], p["bias"], seq=seq, name=f"mixer_bwd_{l}", tm=128)
        g_wout = wgrad(s["merged"], dx1, nj=1, a_mode="full", b_mode="full", name=f"wgrad_out_{l}", tm=512)
        g_win = wgrad(s["h"], dproj, nj=N_DEV, a_mode="full", b_mode="cols", name=f"wgrad_in_{l}", tm=512)
        mix_ex = exchange([g_wout.reshape(N_DEV, d // N_DEV, d), g_win], f"exchange_mix_{l}")
        dx, d_mix_g = dgrad_rms(dproj, p["win"], s["x0"], tied(p["mix_g"], mix_ex), dx1,
                                name=f"dgrad_in_{l}", tm=1024, dy_3d=False)
        part[l] = dict(
            ffn_ex=ffn_ex, mix_ex=mix_ex,
            small=jnp.concatenate([
                dws.reshape(N_GROUPS * CHUNK * CHUNK // d, d),
                d_mix_g[0:1], d_ffn_g[0:1], dln[0:2], dcw_a[0:3],
                dbs[:, ::d // N_GROUPS].T.reshape(1, d)], axis=0),
            cw_f=dcw_f.reshape(N_DEV * HALO, f))
    grad_x = dx.reshape(nb, seq, d)

    loss_row = jnp.zeros((1, d), F32).at[0, 0].set(loss_tile[0, 0])
    small = jnp.concatenate([part[l]["small"] for l in range(depth)] + [d_final_g[0:1], loss_row], axis=0)
    small = _pad_rows(small, -(-small.shape[0] // 8) * 8)
    cwf = jnp.concatenate([part[l]["cw_f"] for l in range(depth)], axis=0)
    small_all, cwf_all = all_gather([small, cwf], name="gather_small_grads")
    small_sum = sum_devices(small_all, name="sum_small", tr=512)
    cwf_sum = sum_devices(cwf_all, name="sum_conv_ffn", tr=512)

    rows_ws = N_GROUPS * CHUNK * CHUNK // d
    per_layer = rows_ws + 8
    def small_of(l, a, b):
        return small_sum[l * per_layer + rows_ws + a:l * per_layer + rows_ws + b]
    g_ws = jnp.stack([small_sum[l * per_layer:l * per_layer + rows_ws].reshape(N_GROUPS, CHUNK, CHUNK)
                      for l in range(depth)])
    g_mix = jnp.concatenate([small_of(l, 0, 1) for l in range(depth)])
    g_ffn = jnp.concatenate([small_of(l, 1, 2) for l in range(depth)])
    g_lng = jnp.concatenate([small_of(l, 2, 3) for l in range(depth)])
    g_lnb = jnp.concatenate([small_of(l, 3, 4) for l in range(depth)])
    g_cwa_full = jnp.stack([small_of(l, 4, 7) for l in range(depth)])
    g_cwa = lax.dynamic_slice_in_dim(g_cwa_full, me * ca, ca, axis=2)
    g_bs = jnp.stack([small_of(l, 7, 8).reshape(N_GROUPS, CHUNK) for l in range(depth)])
    g_final = small_sum[depth * per_layer]
    loss = small_sum[depth * per_layer + 1, 0]
    cwf_sum = cwf_sum.reshape(depth, N_DEV, HALO, f)
    g_cwf = lax.dynamic_index_in_dim(cwf_sum, me, axis=1, keepdims=False)[:, :3]

    me_arr = me.astype(jnp.int32).reshape(1)
    own, recv = {}, {}

    def arrived(l, ex, keys, after):
        srcs, lands = finish_copies(part[l][ex], [0, 1], after, name=f"wait_{ex}_{l}")
        for k, key in enumerate(keys):
            own[key, l], recv[key, l] = srcs[k], lands[k]
        return lands[1]

    def big(key, w, m, v, name):
        return adamw_sharded(me_arr, own[key, 0], recv[key, 0], own[key, 1], recv[key, 1], w, m, v, name=name, tr=128)

    after = grad_x
    for l in reversed(range(depth)):
        after = arrived(l, "ffn_ex", ("wd", "wup"), after)
        if l > 0:
            after = arrived(l, "mix_ex", ("wout", "win"), after)
    u_wd = big("wd", w_down, m_w_down, v_w_down, "adamw_w_down")
    u_wup = big("wup", w_up, m_w_up, v_w_up, "adamw_w_up")
    arrived(0, "mix_ex", ("wout", "win"), u_wup[1])
    u_wout = big("wout", w_out, m_w_out, v_w_out, "adamw_w_out")
    u_win = big("win", w_in, m_w_in, v_w_in, "adamw_w_in")

    def small_update(g, w, m, v, name):
        shape = w.shape
        two_d = (-1, shape[-1]) if w.ndim > 1 else (1, shape[0])
        out = adamw_small(g.reshape(two_d), w.reshape(two_d), m.reshape(two_d), v.reshape(two_d), name=name)
        return (g.reshape(shape),) + tuple(o.reshape(shape) for o in out)

    u_mix = small_update(g_mix, mix_norm_g, m_mix_norm_g, v_mix_norm_g, "adamw_mix_norm_g")
    u_cwa = small_update(g_cwa, conv_a_w, m_conv_a_w, v_conv_a_w, "adamw_conv_a_w")
    u_lng = small_update(g_lng, ln_v_g, m_ln_v_g, v_ln_v_g, "adamw_ln_v_g")
    u_lnb = small_update(g_lnb, ln_v_b, m_ln_v_b, v_ln_v_b, "adamw_ln_v_b")
    u_ws = small_update(g_ws, w_s, m_w_s, v_w_s, "adamw_w_s")
    u_bs = small_update(g_bs, b_s, m_b_s, v_b_s, "adamw_b_s")
    u_ffn = small_update(g_ffn, ffn_norm_g, m_ffn_norm_g, v_ffn_norm_g, "adamw_ffn_norm_g")
    u_cwf = small_update(g_cwf, conv_ffn_w, m_conv_ffn_w, v_conv_ffn_w, "adamw_conv_ffn_w")
    u_final = small_update(g_final, final_norm_g, m_final_norm_g, v_final_norm_g, "adamw_final_norm_g")

    ordered = [u_mix, u_win, u_cwa, u_lng, u_lnb, u_ws, u_bs, u_wout, u_ffn, u_wup, u_cwf, u_wd, u_final]
    return (loss, grad_x, *[u[0] for u in ordered], *[u[1] for u in ordered],
            *[u[2] for u in ordered], *[u[3] for u in ordered])
```

```python
import functools

import jax
import jax.numpy as jnp
from jax import lax
from jax.experimental import pallas as pl
from jax.experimental.pallas import tpu as pltpu

EPS = 1e-6
CHUNK = 128
N_GROUPS = 8
N_DEV = 8
HALO = 8
ADAM_LR = 0.001
ADAM_B1 = 0.9
ADAM_B2 = 0.999
ADAM_EPS = 1e-08
ADAM_WD = 0.01
ADAM_STEP = 10
VMEM_LIMIT_BYTES = 56 * 1024 * 1024
F32 = jnp.float32
BF16 = jnp.bfloat16
MESH = pl.DeviceIdType.MESH
ANY = pl.BlockSpec(memory_space=pl.ANY)
HBM_SPEC = pl.BlockSpec(memory_space=pltpu.HBM)
SEM_SPEC = pl.BlockSpec(memory_space=pltpu.SEMAPHORE)
DATAFLOW = pltpu.SideEffectType.DATAFLOW_SIDE_EFFECTING
NT_DIMS = (((1,), (1,)), ((), ()))
TN_DIMS = (((0,), (0,)), ((), ()))


def _params(n_grid_axes):
    return pltpu.CompilerParams(dimension_semantics=("arbitrary",) * n_grid_axes,
                                vmem_limit_bytes=VMEM_LIMIT_BYTES)


def _shift_down(cur, prev8, k):
    rolled = pltpu.roll(cur, k, 0)
    prolled = pltpu.roll(prev8, k, 0)
    row = lax.broadcasted_iota(jnp.int32, prev8.shape, 0)
    head = jnp.where(row < k, prolled, rolled[:HALO])
    return jnp.concatenate([head, rolled[HALO:]], axis=0)


def _shift_up(cur, next8, k):
    tm = cur.shape[0]
    rolled = pltpu.roll(cur, tm - k, 0)
    nrolled = pltpu.roll(next8, HALO - k, 0)
    row = lax.broadcasted_iota(jnp.int32, next8.shape, 0)
    tail = jnp.where(row >= HALO - k, nrolled, rolled[tm - HALO:])
    return jnp.concatenate([rolled[:tm - HALO], tail], axis=0)


def _conv_fwd(cur, prev8, cw):
    s1 = _shift_down(cur, prev8, 1)
    s2 = _shift_down(cur, prev8, 2)
    y = s2 * cw[0:1, :] + s1 * cw[1:2, :] + cur * cw[2:3, :]
    return y, s1, s2


def _conv_bwd(d, next8, cw):
    return d * cw[2:3, :] + _shift_up(d, next8, 1) * cw[1:2, :] + _shift_up(d, next8, 2) * cw[0:1, :]


def _colsum(a):
    return jnp.sum(a, axis=0, keepdims=True)


def _rms_stats(xv):
    r = lax.rsqrt(jnp.mean(xv * xv, axis=-1, keepdims=True) + EPS)
    return r, xv * r


def _rms_bwd(dh, xv, g):
    r, n = _rms_stats(xv)
    dn = dh * g
    dx = r * (dn - n * jnp.mean(dn * n, axis=-1, keepdims=True))
    return dx, _colsum(dh * n)


def _mixer_forward(p_ref, cprev, xiprev, cw, lng, lnb, ws_ref, bias_ref, mixed_scr, d):
    tm = p_ref.shape[0]
    b = p_ref[:, 0:d]
    c = p_ref[:, d:2 * d]
    xi = p_ref[:, 2 * d:3 * d]
    u = p_ref[:, 3 * d:4 * d]
    v = p_ref[:, 4 * d:5 * d]
    sa = jax.nn.sigmoid(p_ref[:, 5 * d:6 * d])
    sb = jax.nn.sigmoid(p_ref[:, 6 * d:7 * d])
    cx = c * xi
    conv, s1, s2 = _conv_fwd(cx, cprev * xiprev, cw)
    ya = b * conv
    mu = jnp.mean(v, axis=-1, keepdims=True)
    xc = v - mu
    rstd = lax.rsqrt(jnp.mean(xc * xc, axis=-1, keepdims=True) + EPS)
    vhat = xc * rstd
    vnb = (vhat * lng + lnb).astype(BF16)
    tril = (lax.broadcasted_iota(jnp.int32, (CHUNK, CHUNK), 0)
            >= lax.broadcasted_iota(jnp.int32, (CHUNK, CHUNK), 1))
    gd = d // N_GROUPS
    for g in range(N_GROUPS):
        wm = jnp.where(tril, ws_ref[g], 0.0).astype(BF16)
        cols = slice(g * gd, (g + 1) * gd)
        for n in range(tm // CHUNK):
            rows = slice(n * CHUNK, (n + 1) * CHUNK)
            mixed_scr[rows, cols] = (jnp.dot(wm, vnb[rows, cols], preferred_element_type=F32)
                                     + bias_ref[:, cols])
    mixed = mixed_scr[...]
    yb = u * mixed
    merged = sa * ya + sb * yb
    return dict(b=b, c=c, xi=xi, u=u, sa=sa, sb=sb, cx=cx, s1=s1, s2=s2, conv=conv, ya=ya,
                rstd=rstd, vhat=vhat, vnb=vnb, mixed=mixed, yb=yb, merged=merged, tril=tril)


def rms_matmul(x, g, w, *, name, tm, out_3d):
    t, d = x.shape
    nj, _, n = w.shape
    tm = min(tm, t)

    def body(x_ref, g_ref, w_ref, h_ref, o_ref, hs_ref):
        @pl.when(pl.program_id(1) == 0)
        def _():
            _, nrm = _rms_stats(x_ref[...])
            hv = (nrm * g_ref[...]).astype(BF16)
            hs_ref[...] = hv
            h_ref[...] = hv

        o_ref[...] = jnp.dot(hs_ref[...], w_ref[...], preferred_element_type=F32)

    if out_3d:
        out_shape = jax.ShapeDtypeStruct((nj, t, n), F32)
        out_spec = pl.BlockSpec((None, tm, n), lambda i, j: (j, i, 0))
    else:
        out_shape = jax.ShapeDtypeStruct((t, nj * n), F32)
        out_spec = pl.BlockSpec((tm, n), lambda i, j: (i, j))
    return pl.pallas_call(
        body, name=name, grid=(t // tm, nj),
        in_specs=[pl.BlockSpec((tm, d), lambda i, j: (i, 0)),
                  pl.BlockSpec((1, d), lambda i, j: (0, 0)),
                  pl.BlockSpec((None, d, n), lambda i, j: (j, 0, 0))],
        out_specs=[pl.BlockSpec((tm, d), lambda i, j: (i, 0)), out_spec],
        out_shape=[jax.ShapeDtypeStruct((t, d), BF16), out_shape],
        scratch_shapes=[pltpu.VMEM((tm, d), BF16)],
        compiler_params=_params(2),
    )(x, g, w)


def mixer_fwd(x, proj, wout, cw, lng, lnb, ws, bias, *, seq, name, tm):
    t, d = x.shape
    tm = min(tm, seq)
    tiles_per_seq = seq // tm

    def body(x_ref, p_ref, cprev_ref, xiprev_ref, wout_ref, cw_ref, lng_ref, lnb_ref, ws_ref, bias_ref,
             merged_ref, x1_ref, mixed_scr):
        keep = jnp.where(pl.program_id(0) % tiles_per_seq == 0, 0.0, 1.0)
        f = _mixer_forward(p_ref, cprev_ref[...] * keep, xiprev_ref[...], cw_ref[...], lng_ref[...],
                           lnb_ref[...], ws_ref, bias_ref, mixed_scr, d)
        mb = f["merged"].astype(BF16)
        merged_ref[...] = mb
        x1_ref[...] = x_ref[...] + jnp.dot(mb, wout_ref[...], preferred_element_type=F32)

    def halo(col):
        return pl.BlockSpec((HALO, d), lambda i: (jnp.maximum(i * (tm // HALO) - 1, 0), col))

    const2 = lambda i: (0, 0)
    return pl.pallas_call(
        body, name=name, grid=(t // tm,),
        in_specs=[pl.BlockSpec((tm, d), lambda i: (i, 0)),
                  pl.BlockSpec((tm, 7 * d), lambda i: (i, 0)),
                  halo(1), halo(2),
                  pl.BlockSpec((d, d), const2),
                  pl.BlockSpec((HALO, d), const2),
                  pl.BlockSpec((1, d), const2),
                  pl.BlockSpec((1, d), const2),
                  pl.BlockSpec((N_GROUPS, CHUNK, CHUNK), lambda i: (0, 0, 0)),
                  pl.BlockSpec((CHUNK, d), const2)],
        out_specs=[pl.BlockSpec((tm, d), lambda i: (i, 0)), pl.BlockSpec((tm, d), lambda i: (i, 0))],
        out_shape=[jax.ShapeDtypeStruct((t, d), BF16), jax.ShapeDtypeStruct((t, d), F32)],
        scratch_shapes=[pltpu.VMEM((tm, d), F32)],
        compiler_params=_params(1),
    )(x, proj, proj, proj, wout, cw, lng, lnb, ws, bias)


def ffn_fwd(x1, up0, wd, cw, *, seq, name, tm):
    t, d = x1.shape
    nj, _, f = up0.shape
    half = nj // 2
    tm = min(tm, seq)
    tiles_per_seq = seq // tm

    def body(x1_ref, up_ref, prev_ref, wd_ref, cw_ref, act_ref, x2_ref):
        keep = jnp.where(pl.program_id(0) % tiles_per_seq == 0, 0.0, 1.0)
        acc = x1_ref[...]
        for k in range(half):
            gate, _, _ = _conv_fwd(up_ref[k], prev_ref[k] * keep, cw_ref[k])
            val, _, _ = _conv_fwd(up_ref[k + half], prev_ref[k + half] * keep, cw_ref[k + half])
            a = (jax.nn.silu(gate) * val).astype(BF16)
            act_ref[k] = a
            acc = acc + jnp.dot(a, wd_ref[k], preferred_element_type=F32)
        x2_ref[...] = acc

    return pl.pallas_call(
        body, name=name, grid=(t // tm,),
        in_specs=[pl.BlockSpec((tm, d), lambda i: (i, 0)),
                  pl.BlockSpec((nj, tm, f), lambda i: (0, i, 0)),
                  pl.BlockSpec((nj, HALO, f), lambda i: (0, jnp.maximum(i * (tm // HALO) - 1, 0), 0)),
                  pl.BlockSpec((half, f, d), lambda i: (0, 0, 0)),
                  pl.BlockSpec((nj, HALO, f), lambda i: (0, 0, 0))],
        out_specs=[pl.BlockSpec((half, tm, f), lambda i: (0, i, 0)), pl.BlockSpec((tm, d), lambda i: (i, 0))],
        out_shape=[jax.ShapeDtypeStruct((half, t, f), BF16), jax.ShapeDtypeStruct((t, d), F32)],
        compiler_params=_params(1),
    )(x1, up0, up0, wd, cw)


def final_loss(x, g, target, *, name, tm):
    t, d = x.shape
    tm = min(tm, t)

    def body(x_ref, g_ref, tgt_ref, dx_ref, dg_ref, loss_ref):
        @pl.when(pl.program_id(0) == 0)
        def _():
            dg_ref[...] = jnp.zeros_like(dg_ref)
            loss_ref[...] = jnp.zeros_like(loss_ref)

        xv = x_ref[...]
        gv = g_ref[...]
        r, n = _rms_stats(xv)
        err = n * gv - tgt_ref[...]
        loss_ref[...] += 0.5 * jnp.sum(jnp.mean(err * err, axis=-1, keepdims=True))
        dy = err * (1.0 / d)
        dn = dy * gv
        dx_ref[...] = r * (dn - n * jnp.mean(dn * n, axis=-1, keepdims=True))
        dg_ref[0:1, :] += _colsum(dy * n)

    return pl.pallas_call(
        body, name=name, grid=(t // tm,),
        in_specs=[pl.BlockSpec((tm, d), lambda i: (i, 0)),
                  pl.BlockSpec((1, d), lambda i: (0, 0)),
                  pl.BlockSpec((tm, d), lambda i: (i, 0))],
        out_specs=[pl.BlockSpec((tm, d), lambda i: (i, 0)),
                   pl.BlockSpec((HALO, d), lambda i: (0, 0)),
                   pl.BlockSpec((8, 128), lambda i: (0, 0))],
        out_shape=[jax.ShapeDtypeStruct((t, d), F32), jax.ShapeDtypeStruct((HALO, d), F32),
                   jax.ShapeDtypeStruct((8, 128), F32)],
        compiler_params=_params(1),
    )(x, g, target)


def ffn_bwd(dx2, up0, wd, cw, *, seq, name, tm):
    t, d = dx2.shape
    nj, _, f = up0.shape
    half = nj // 2
    tm = min(tm, seq)
    tiles_per_seq = seq // tm
    nt = t // tm

    def body(dx_ref, up_ref, prev_ref, wd_ref, cw_ref, dup_ref, dcw_ref, carry_ref):
        i = pl.program_id(0)
        tile = nt - 1 - i

        @pl.when(i == 0)
        def _():
            dcw_ref[...] = jnp.zeros_like(dcw_ref)
            carry_ref[...] = jnp.zeros_like(carry_ref)

        keep_prev = jnp.where(tile % tiles_per_seq == 0, 0.0, 1.0)
        keep_next = jnp.where(tile % tiles_per_seq == tiles_per_seq - 1, 0.0, 1.0)
        dxb = dx_ref[...].astype(BF16)

        def through_conv(j, dup):
            next8 = carry_ref[j] * keep_next
            carry_ref[j] = dup[:HALO]
            dup_ref[j] = _conv_bwd(dup, next8, cw_ref[j]).astype(BF16)

        for k in range(half):
            gate, g1, g2 = _conv_fwd(up_ref[k], prev_ref[k] * keep_prev, cw_ref[k])
            val, v1, v2 = _conv_fwd(up_ref[k + half], prev_ref[k + half] * keep_prev, cw_ref[k + half])
            dact = lax.dot_general(dxb, wd_ref[k], NT_DIMS, preferred_element_type=F32)
            sg = jax.nn.sigmoid(gate)
            dgate = dact * val * (sg * (1.0 + gate * (1.0 - sg)))
            dval = dact * (gate * sg)
            dcw_ref[k, 0:1, :] += _colsum(dgate * g2)
            dcw_ref[k, 1:2, :] += _colsum(dgate * g1)
            dcw_ref[k, 2:3, :] += _colsum(dgate * up_ref[k])
            dcw_ref[k + half, 0:1, :] += _colsum(dval * v2)
            dcw_ref[k + half, 1:2, :] += _colsum(dval * v1)
            dcw_ref[k + half, 2:3, :] += _colsum(dval * up_ref[k + half])
            through_conv(k, dgate)
            through_conv(k + half, dval)

    rev = lambda i: nt - 1 - i
    return pl.pallas_call(
        body, name=name, grid=(nt,),
        in_specs=[pl.BlockSpec((tm, d), lambda i: (rev(i), 0)),
                  pl.BlockSpec((nj, tm, f), lambda i: (0, rev(i), 0)),
                  pl.BlockSpec((nj, HALO, f), lambda i: (0, jnp.maximum(rev(i) * (tm // HALO) - 1, 0), 0)),
                  pl.BlockSpec((half, f, d), lambda i: (0, 0, 0)),
                  pl.BlockSpec((nj, HALO, f), lambda i: (0, 0, 0))],
        out_specs=[pl.BlockSpec((nj, tm, f), lambda i: (0, rev(i), 0)),
                   pl.BlockSpec((nj, HALO, f), lambda i: (0, 0, 0))],
        out_shape=[jax.ShapeDtypeStruct((nj, t, f), BF16), jax.ShapeDtypeStruct((nj, HALO, f), F32)],
        scratch_shapes=[pltpu.VMEM((nj, HALO, f), F32)],
        compiler_params=_params(1),
    )(dx2, up0, up0, wd, cw)


def mixer_bwd(dx1, proj, wout, cw, lng, lnb, ws, wst, bias, *, seq, name, tm):
    t, d = dx1.shape
    tm = min(tm, seq)
    tiles_per_seq = seq // tm
    nt = t // tm
    gd = d // N_GROUPS

    def body(dx_ref, p_ref, cprev_ref, xiprev_ref, wout_ref, cw_ref, lng_ref, lnb_ref, ws_ref, wst_ref, bias_ref,
             dp_ref, dcw_ref, dln_ref, dws_ref, dbs_ref, mixed_scr, dvn_scr, carry_ref, dbs_acc):
        i = pl.program_id(0)
        tile = nt - 1 - i

        @pl.when(i == 0)
        def _():
            dcw_ref[...] = jnp.zeros_like(dcw_ref)
            dln_ref[...] = jnp.zeros_like(dln_ref)
            dws_ref[...] = jnp.zeros_like(dws_ref)
            dbs_acc[...] = jnp.zeros_like(dbs_acc)
            carry_ref[...] = jnp.zeros_like(carry_ref)

        keep_prev = jnp.where(tile % tiles_per_seq == 0, 0.0, 1.0)
        keep_next = jnp.where(tile % tiles_per_seq == tiles_per_seq - 1, 0.0, 1.0)
        cw = cw_ref[...]
        lng = lng_ref[...]
        f = _mixer_forward(p_ref, cprev_ref[...] * keep_prev, xiprev_ref[...], cw, lng, lnb_ref[...],
                           ws_ref, bias_ref, mixed_scr, d)
        dmerged = lax.dot_general(dx_ref[...].astype(BF16), wout_ref[...], NT_DIMS, preferred_element_type=F32)
        sa, sb = f["sa"], f["sb"]
        dp_ref[:, 5 * d:6 * d] = (dmerged * f["ya"] * (sa * (1.0 - sa))).astype(BF16)
        dp_ref[:, 6 * d:7 * d] = (dmerged * f["yb"] * (sb * (1.0 - sb))).astype(BF16)
        dya = dmerged * sa
        dyb = dmerged * sb
        dp_ref[:, 0:d] = (dya * f["conv"]).astype(BF16)
        dconv = dya * f["b"]
        dcw_ref[0:1, :] += _colsum(dconv * f["s2"])
        dcw_ref[1:2, :] += _colsum(dconv * f["s1"])
        dcw_ref[2:3, :] += _colsum(dconv * f["cx"])
        next8 = carry_ref[...] * keep_next
        carry_ref[...] = dconv[:HALO]
        dcx = _conv_bwd(dconv, next8, cw)
        dp_ref[:, d:2 * d] = (dcx * f["xi"]).astype(BF16)
        dp_ref[:, 2 * d:3 * d] = (dcx * f["c"]).astype(BF16)
        dp_ref[:, 3 * d:4 * d] = (dyb * f["mixed"]).astype(BF16)
        dmixed = dyb * f["u"]
        dmb = dmixed.astype(BF16)
        vnb = f["vnb"]
        tril = f["tril"]
        triu = (lax.broadcasted_iota(jnp.int32, (CHUNK, CHUNK), 0)
                <= lax.broadcasted_iota(jnp.int32, (CHUNK, CHUNK), 1))
        dbs_tile = dmixed[0:CHUNK]
        for n in range(1, tm // CHUNK):
            dbs_tile = dbs_tile + dmixed[n * CHUNK:(n + 1) * CHUNK]
        dbs_acc[...] += dbs_tile
        for g in range(N_GROUPS):
            wmt = jnp.where(triu, wst_ref[g], 0.0).astype(BF16)
            cols = slice(g * gd, (g + 1) * gd)
            dw = jnp.zeros((CHUNK, CHUNK), F32)
            for n in range(tm // CHUNK):
                rows = slice(n * CHUNK, (n + 1) * CHUNK)
                dvn_scr[rows, cols] = jnp.dot(wmt, dmb[rows, cols], preferred_element_type=F32)
                dw = dw + lax.dot_general(dmb[rows, cols], vnb[rows, cols], NT_DIMS, preferred_element_type=F32)
            dws_ref[g] += jnp.where(tril, dw, 0.0)
        dvn = dvn_scr[...]
        vhat = f["vhat"]
        dln_ref[0:1, :] += _colsum(dvn * vhat)
        dln_ref[1:2, :] += _colsum(dvn)
        dvh = dvn * lng
        dv = f["rstd"] * (dvh - jnp.mean(dvh, axis=-1, keepdims=True)
                          - vhat * jnp.mean(dvh * vhat, axis=-1, keepdims=True))
        dp_ref[:, 4 * d:5 * d] = dv.astype(BF16)

        @pl.when(i == nt - 1)
        def _():
            for g in range(N_GROUPS):
                cols = slice(g * gd, (g + 1) * gd)
                s = jnp.sum(dbs_acc[:, cols], axis=1, keepdims=True)
                dbs_ref[:, cols] = jnp.broadcast_to(s, (CHUNK, gd))

    rev = lambda i: nt - 1 - i

    def halo(col):
        return pl.BlockSpec((HALO, d), lambda i: (jnp.maximum(rev(i) * (tm // HALO) - 1, 0), col))

    const2 = lambda i: (0, 0)
    const3 = lambda i: (0, 0, 0)
    return pl.pallas_call(
        body, name=name, grid=(nt,),
        in_specs=[pl.BlockSpec((tm, d), lambda i: (rev(i), 0)),
                  pl.BlockSpec((tm, 7 * d), lambda i: (rev(i), 0)),
                  halo(1), halo(2),
                  pl.BlockSpec((d, d), const2),
                  pl.BlockSpec((HALO, d), const2),
                  pl.BlockSpec((1, d), const2),
                  pl.BlockSpec((1, d), const2),
                  pl.BlockSpec((N_GROUPS, CHUNK, CHUNK), const3),
                  pl.BlockSpec((N_GROUPS, CHUNK, CHUNK), const3),
                  pl.BlockSpec((CHUNK, d), const2)],
        out_specs=[pl.BlockSpec((tm, 7 * d), lambda i: (rev(i), 0)),
                   pl.BlockSpec((HALO, d), const2),
                   pl.BlockSpec((HALO, d), const2),
                   pl.BlockSpec((N_GROUPS, CHUNK, CHUNK), const3),
                   pl.BlockSpec((CHUNK, d), const2)],
        out_shape=[jax.ShapeDtypeStruct((t, 7 * d), BF16),
                   jax.ShapeDtypeStruct((HALO, d), F32),
                   jax.ShapeDtypeStruct((HALO, d), F32),
                   jax.ShapeDtypeStruct((N_GROUPS, CHUNK, CHUNK), F32),
                   jax.ShapeDtypeStruct((CHUNK, d), F32)],
        scratch_shapes=[pltpu.VMEM((tm, d), F32), pltpu.VMEM((tm, d), F32),
                        pltpu.VMEM((HALO, d), F32), pltpu.VMEM((CHUNK, d), F32)],
        compiler_params=_params(1),
    )(dx1, proj, proj, proj, wout, cw, lng, lnb, ws, wst, bias)


def dgrad_rms(dy, w, x, g, res, *, name, tm, dy_3d):
    t, d = x.shape
    nj, _, n = w.shape
    tm = min(tm, t)

    def body(dy_ref, w_ref, x_ref, g_ref, res_ref, dx_ref, dg_ref, acc_ref):
        i, j = pl.program_id(0), pl.program_id(1)

        @pl.when((i == 0) & (j == 0))
        def _():
            dg_ref[...] = jnp.zeros_like(dg_ref)

        part = lax.dot_general(dy_ref[...], w_ref[...], NT_DIMS, preferred_element_type=F32)

        @pl.when(j == 0)
        def _():
            acc_ref[...] = part

        @pl.when(j > 0)
        def _():
            acc_ref[...] += part

        @pl.when(j == nj - 1)
        def _():
            dx, dg = _rms_bwd(acc_ref[...], x_ref[...], g_ref[...])
            dx_ref[...] = res_ref[...] + dx
            dg_ref[0:1, :] += dg

    if dy_3d:
        dy_spec = pl.BlockSpec((None, tm, n), lambda i, j: (j, i, 0))
    else:
        dy_spec = pl.BlockSpec((tm, n), lambda i, j: (i, j))
    return pl.pallas_call(
        body, name=name, grid=(t // tm, nj),
        in_specs=[dy_spec,
                  pl.BlockSpec((None, d, n), lambda i, j: (j, 0, 0)),
                  pl.BlockSpec((tm, d), lambda i, j: (i, 0)),
                  pl.BlockSpec((1, d), lambda i, j: (0, 0)),
                  pl.BlockSpec((tm, d), lambda i, j: (i, 0))],
        out_specs=[pl.BlockSpec((tm, d), lambda i, j: (i, 0)), pl.BlockSpec((HALO, d), lambda i, j: (0, 0))],
        out_shape=[jax.ShapeDtypeStruct((t, d), F32), jax.ShapeDtypeStruct((HALO, d), F32)],
        scratch_shapes=[pltpu.VMEM((tm, d), F32)],
        compiler_params=_params(2),
    )(dy, w, x, g, res)


def wgrad(a, b, *, nj, a_mode, b_mode, name, tm):
    def describe(arr, mode):
        if mode == "full":
            return arr.shape[0], arr.shape[1], pl.BlockSpec((tm_, arr.shape[1]), lambda j, s: (s, 0))
        if mode == "cols":
            c = arr.shape[1] // nj
            return arr.shape[0], c, pl.BlockSpec((tm_, c), lambda j, s: (s, j))
        return arr.shape[1], arr.shape[2], pl.BlockSpec((None, tm_, arr.shape[2]), lambda j, s: (j, s, 0))

    t = a.shape[0] if a_mode != "lead" else a.shape[1]
    tm_ = min(tm, t)
    _, k, a_spec = describe(a, a_mode)
    _, n, b_spec = describe(b, b_mode)

    ns = t // tm_

    def body(a_ref, b_ref, o_ref, acc_ref):
        s = pl.program_id(1)
        part = lax.dot_general(a_ref[...], b_ref[...], TN_DIMS, preferred_element_type=F32)
        if ns == 1:
            o_ref[...] = part.astype(BF16)
            return

        @pl.when(s == 0)
        def _():
            acc_ref[...] = part

        @pl.when((s > 0) & (s < ns - 1))
        def _():
            acc_ref[...] += part

        @pl.when(s == ns - 1)
        def _():
            o_ref[...] = (acc_ref[...] + part).astype(BF16)

    return pl.pallas_call(
        body, name=name, grid=(nj, ns),
        in_specs=[a_spec, b_spec],
        out_specs=pl.BlockSpec((None, k, n), lambda j, s: (j, 0, 0)),
        out_shape=jax.ShapeDtypeStruct((nj, k, n), BF16),
        scratch_shapes=[pltpu.VMEM((k, n), F32)],
        compiler_params=_params(2),
    )(a, b)


def _adamw_math(w, g, m, v):
    m = ADAM_B1 * m + (1.0 - ADAM_B1) * g
    v = ADAM_B2 * v + (1.0 - ADAM_B2) * (g * g)
    m_hat = m / (1.0 - ADAM_B1 ** ADAM_STEP)
    v_hat = v / (1.0 - ADAM_B2 ** ADAM_STEP)
    delta = -ADAM_LR * (m_hat / (jnp.sqrt(v_hat) + ADAM_EPS) + ADAM_WD * w)
    return delta, m, v


def _row_tile(rows, at_most):
    if rows <= at_most:
        return rows
    return max(k for k in range(16, at_most + 1, 16) if rows % k == 0)


def _sum_in_device_order(ref):
    total = ref[0]
    for s in range(1, N_DEV):
        total = total + ref[s]
    return total


def adamw_sharded(me, own0, recv0, own1, recv1, w, m, v, *, name, tr):
    _, r, c = w.shape
    tr = _row_tile(r, tr)
    ni = r // tr

    def body(me_ref, o0_ref, r0_ref, o1_ref, r1_ref, w_ref, m_ref, v_ref, g_ref, d_ref, nm_ref, nv_ref):
        def finish(own_ref, recv_ref):
            g = None
            for s in range(N_DEV):
                term = jnp.where(me_ref[0] == s, own_ref[...], recv_ref[s]).astype(F32)
                g = term if g is None else g + term
            delta, nm, nv = _adamw_math(w_ref[...], g, m_ref[...], v_ref[...])
            g_ref[...] = g
            d_ref[...] = delta
            nm_ref[...] = nm
            nv_ref[...] = nv

        @pl.when(pl.program_id(0) == 0)
        def _():
            finish(o0_ref, r0_ref)

        @pl.when(pl.program_id(0) == 1)
        def _():
            finish(o1_ref, r1_ref)

    row0 = lambda l, i: i * (1 - l) + (ni - 1) * l
    row1 = lambda l, i: i * l
    lay = pl.BlockSpec((None, tr, c), lambda l, i, me_ref: (l, i, 0))
    grid_spec = pltpu.PrefetchScalarGridSpec(
        num_scalar_prefetch=1, grid=(2, ni),
        in_specs=[pl.BlockSpec((None, tr, c), lambda l, i, me_ref: (me_ref[0], row0(l, i), 0)),
                  pl.BlockSpec((N_DEV, tr, c), lambda l, i, me_ref: (0, row0(l, i), 0)),
                  pl.BlockSpec((None, tr, c), lambda l, i, me_ref: (me_ref[0], row1(l, i), 0)),
                  pl.BlockSpec((N_DEV, tr, c), lambda l, i, me_ref: (0, row1(l, i), 0)),
                  lay, lay, lay],
        out_specs=[lay, lay, lay, lay])
    return pl.pallas_call(
        body, name=name, grid_spec=grid_spec,
        out_shape=[jax.ShapeDtypeStruct(w.shape, F32)] * 4,
        compiler_params=_params(2),
    )(me, own0, recv0, own1, recv1, w, m, v)


def adamw_small(g, w, m, v, *, name):
    def body(g_ref, w_ref, m_ref, v_ref, d_ref, nm_ref, nv_ref):
        delta, nm, nv = _adamw_math(w_ref[...], g_ref[...], m_ref[...], v_ref[...])
        d_ref[...] = delta
        nm_ref[...] = nm
        nv_ref[...] = nv

    return pl.pallas_call(
        body, name=name,
        out_shape=[jax.ShapeDtypeStruct(w.shape, F32)] * 3,
        compiler_params=pltpu.CompilerParams(vmem_limit_bytes=VMEM_LIMIT_BYTES),
    )(g, w, m, v)


def sum_devices(parts, *, name, tr):
    _, r, c = parts.shape
    tr = min(tr, r)

    def body(p_ref, o_ref):
        o_ref[...] = _sum_in_device_order(p_ref)

    return pl.pallas_call(
        body, name=name, grid=(r // tr,),
        in_specs=[pl.BlockSpec((N_DEV, tr, c), lambda i: (0, i, 0))],
        out_specs=pl.BlockSpec((tr, c), lambda i: (i, 0)),
        out_shape=jax.ShapeDtypeStruct((r, c), F32),
        compiler_params=_params(1),
    )(parts)


def _my_place():
    return lax.axis_index("x"), lax.axis_index("y"), lax.axis_index("c")


def all_gather(arrays, *, name):
    n = len(arrays)

    def body(*refs):
        ins, outs = refs[:n], refs[n:2 * n]
        send_sems, recv_sems, local_sems = refs[2 * n:]
        x, y, c = _my_place()
        me, sibling = (x, y, c), (x, y, 1 - c)
        chips = [(1 - x, y), (x, 1 - y), (1 - x, 1 - y)]
        waits = []
        for a in range(n):
            def slot(place, a=a):
                px, py, pc = place
                return outs[a].at[4 * px + 2 * py + pc]

            def copy(k, block, to, src=None, a=a, slot=slot):
                return pltpu.make_async_remote_copy(
                    src_ref=slot(block) if src is None else src, dst_ref=slot(block),
                    send_sem=send_sems.at[a, k], recv_sem=recv_sems.at[a, k],
                    device_id=to, device_id_type=MESH)

            mine = pltpu.make_async_copy(ins[a], slot(me), local_sems.at[a])
            mine.start()
            first = [copy(0, me, sibling, src=ins[a])]
            first += [copy(1 + j, me, (*chip, c), src=ins[a]) for j, chip in enumerate(chips)]
            for cp in first:
                cp.start()
            waits.append((copy, mine, first))
        sends = []
        for a in range(n):
            copy, mine, first = waits[a]
            passed = [copy(4 + j, (*chip, c), sibling) for j, chip in enumerate(chips)]
            for j, chip in enumerate(chips):
                copy(1 + j, (*chip, c), me).wait_recv()
                passed[j].start()
            sends.append(first + passed)
        for a in range(n):
            copy, mine, first = waits[a]
            copy(0, sibling, me).wait_recv()
            for j, chip in enumerate(chips):
                copy(4 + j, (*chip, 1 - c), me).wait_recv()
            for cp in sends[a]:
                cp.wait_send()
            mine.wait()

    return pl.pallas_call(
        body, name=name,
        in_specs=[ANY] * n, out_specs=[ANY] * n,
        out_shape=[jax.ShapeDtypeStruct((N_DEV,) + a.shape, a.dtype) for a in arrays],
        scratch_shapes=[pltpu.SemaphoreType.DMA((n, 7)), pltpu.SemaphoreType.DMA((n, 7)),
                        pltpu.SemaphoreType.DMA((n,))],
        compiler_params=pltpu.CompilerParams(has_side_effects=True),
    )(*arrays)


def _peer_place(r, x, y, c):
    fx, fy, fc = (r >> 2) & 1, (r >> 1) & 1, r & 1
    return (1 - x if fx else x, 1 - y if fy else y, 1 - c if fc else c)


def own_slot(me, w, layer, dtype, *, name, tr):
    _, r, c = w.shape
    tr = _row_tile(r, tr)

    def body(me_ref, w_ref, o_ref):
        o_ref[...] = w_ref[...].astype(dtype)

    grid_spec = pltpu.PrefetchScalarGridSpec(
        num_scalar_prefetch=1, grid=(r // tr,),
        in_specs=[pl.BlockSpec((None, tr, c), lambda i, me_ref: (layer, i, 0))],
        out_specs=pl.BlockSpec((None, tr, c), lambda i, me_ref: (me_ref[0], i, 0)))
    return pl.pallas_call(
        body, name=name, grid_spec=grid_spec,
        out_shape=jax.ShapeDtypeStruct((N_DEV, r, c), dtype),
        compiler_params=_params(1),
    )(me, w)


def _split_copy(r, src, land, send_sem, recv_sem, scatter, arriving):
    x, y, c = _my_place()
    px, py, pc = _peer_place(r, x, y, c)
    theirs, mine = 4 * px + 2 * py + pc, 4 * x + 2 * y + c
    return pltpu.make_async_remote_copy(
        src_ref=src.at[theirs] if scatter else land.at[mine], dst_ref=land.at[theirs if arriving else mine],
        send_sem=send_sem.at[r - 1], recv_sem=recv_sem.at[r - 1],
        device_id=(px, py, pc), device_id_type=MESH)


def start_copies(srcs, lands, *, scatter, name):
    n = len(lands)
    bufs = (list(srcs) if scatter else []) + list(lands)
    nb = len(bufs)

    def body(*refs):
        src = refs[:n] if scatter else [None] * n
        land = refs[nb - n:nb]
        send_sems, recv_sems = refs[nb:nb + n], refs[nb + n:nb + 2 * n]
        token = refs[2 * nb + 2 * n]
        for a in range(n):
            for r in range(1, N_DEV):
                _split_copy(r, src[a], land[a], send_sems[a], recv_sems[a], scatter, False).start()
        token[...] = jnp.zeros_like(token)

    outs = pl.pallas_call(
        body, name=name,
        in_specs=[HBM_SPEC] * nb,
        out_specs=[SEM_SPEC] * (2 * n) + [HBM_SPEC] * nb + [pl.BlockSpec(memory_space=pltpu.VMEM)],
        out_shape=([pltpu.SemaphoreType.DMA((N_DEV - 1,))] * (2 * n)
                   + [pltpu.HBM(a.shape, a.dtype) for a in bufs]
                   + [jax.ShapeDtypeStruct((8, 128), F32)]),
        input_output_aliases={i: 2 * n + i for i in range(nb)},
        compiler_params=pltpu.CompilerParams(has_side_effects=DATAFLOW),
    )(*[pltpu.with_memory_space_constraint(a, pltpu.HBM) for a in bufs])
    thru = list(outs[2 * n:2 * n + nb])
    return dict(send=outs[:n], recv=outs[n:2 * n], src=thru[:n] if scatter else None, land=thru[nb - n:],
                token=outs[2 * n + nb], scatter=scatter)


def finish_copies(started, which, after, *, name):
    n = len(which)
    scatter = started["scatter"]
    bufs = ([started["src"][i] for i in which] if scatter else []) + [started["land"][i] for i in which]
    nb = len(bufs)

    def body(*refs):
        src = refs[:n] if scatter else [None] * n
        land = refs[nb - n:nb]
        send_sems, recv_sems = refs[nb:nb + n], refs[nb + n:nb + 2 * n]
        for a in range(n):
            for r in range(1, N_DEV):
                cp = _split_copy(r, src[a], land[a], send_sems[a], recv_sems[a], scatter, True)
                cp.wait_send()
                cp.wait_recv()

    outs = pl.pallas_call(
        body, name=name,
        in_specs=[HBM_SPEC] * nb + [SEM_SPEC] * (2 * n) + [ANY],
        out_specs=[HBM_SPEC] * nb,
        out_shape=[pltpu.HBM(a.shape, a.dtype) for a in bufs],
        input_output_aliases={i: i for i in range(nb)},
        compiler_params=pltpu.CompilerParams(has_side_effects=DATAFLOW),
    )(*bufs, *[started["send"][i] for i in which], *[started["recv"][i] for i in which], after)
    return (list(outs[:n]) if scatter else None), list(outs[nb - n:])


def _pad_rows(a, rows):
    pad = [(0, 0)] * a.ndim
    pad[-2] = (0, rows - a.shape[-2])
    return jnp.pad(a, pad)


def kernel(x, mix_norm_g, w_in, conv_a_w, ln_v_g, ln_v_b, w_s, b_s, w_out, ffn_norm_g, w_up, conv_ffn_w, w_down, final_norm_g, loss_target, m_mix_norm_g, m_w_in, m_conv_a_w, m_ln_v_g, m_ln_v_b, m_w_s, m_b_s, m_w_out, m_ffn_norm_g, m_w_up, m_conv_ffn_w, m_w_down, m_final_norm_g, v_mix_norm_g, v_w_in, v_conv_a_w, v_ln_v_g, v_ln_v_b, v_w_s, v_b_s, v_w_out, v_ffn_norm_g, v_w_up, v_conv_ffn_w, v_w_down, v_final_norm_g):
    nb, seq, d = x.shape
    t = nb * seq
    depth = w_in.shape[0]
    f = w_up.shape[2]
    me = 4 * lax.axis_index("x") + 2 * lax.axis_index("y") + lax.axis_index("c")
    xt = x.reshape(t, d)
    tgt = loss_target.reshape(t, d)

    conv_pack = jnp.concatenate([_pad_rows(conv_a_w, HALO), _pad_rows(conv_ffn_w, HALO)], axis=-1)
    me_arr = me.astype(jnp.int32).reshape(1)
    zones, slot_of = [], {}
    for l in range(depth):
        for key, w in (("win", w_in), ("conv", None), ("wout", w_out), ("wup", w_up), ("wd", w_down)):
            if key == "conv":
                if l == 0:
                    slot_of["conv"] = len(zones)
                    packed = conv_pack.reshape(1, depth * HALO, conv_pack.shape[-1])
                    zones.append(own_slot(me_arr, packed, 0, F32, name="own_slot_conv", tr=256))
                continue
            slot_of[key, l] = len(zones)
            zones.append(own_slot(me_arr, w, l, BF16, name=f"own_slot_{key}_{l}", tr=256))
    gathering = start_copies(None, zones, scatter=False, name="gather_start")

    def gathered(keys, after, name):
        return finish_copies(gathering, [slot_of[k] for k in keys], after, name=name)[1]

    saved, layers = [], []
    cur = xt
    after = gathering["token"]
    for l in range(depth):
        p = dict(mix_g=mix_norm_g[l][None], ffn_g=ffn_norm_g[l][None], lng=ln_v_g[l][None], lnb=ln_v_b[l][None],
                 ws=w_s[l], wst=jnp.swapaxes(w_s[l], 1, 2),
                 bias=jnp.repeat(b_s[l].T, d // N_GROUPS, axis=1))
        (p["win"],) = gathered([("win", l)], after, f"wait_w_in_{l}")
        h, proj = rms_matmul(cur, p["mix_g"], p["win"], name=f"rms_proj_{l}", tm=1024, out_3d=False)
        if l == 0:
            wout_g, conv_g = gathered([("wout", l), "conv"], h, f"wait_w_out_{l}")
            conv_g = conv_g.reshape(N_DEV, depth, HALO, -1)
            ca = conv_g.shape[-1] - f
        else:
            (wout_g,) = gathered([("wout", l)], h, f"wait_w_out_{l}")
        p["wout"] = wout_g.reshape(d, d)
        p["cw_a"] = jnp.transpose(conv_g[:, l, :, :ca], (1, 0, 2)).reshape(HALO, d)
        p["cw_f"] = conv_g[:, l, :, ca:]
        merged, x1 = mixer_fwd(cur, proj, p["wout"], p["cw_a"], p["lng"], p["lnb"], p["ws"], p["bias"],
                               seq=seq, name=f"mixer_fwd_{l}", tm=256)
        (p["wup"],) = gathered([("wup", l)], merged, f"wait_w_up_{l}")
        h2, up0 = rms_matmul(x1, p["ffn_g"], p["wup"], name=f"rms_up_{l}", tm=1024, out_3d=True)
        (wd_g,) = gathered([("wd", l)], h2, f"wait_w_down_{l}")
        p["wd"] = wd_g.reshape(N_DEV // 2, 2 * wd_g.shape[1], d)
        act, x2 = ffn_fwd(x1, up0, p["wd"], p["cw_f"], seq=seq, name=f"ffn_fwd_{l}", tm=256)
        saved.append(dict(x0=cur, h=h, proj=proj, merged=merged, x1=x1, h2=h2, up0=up0, act=act))
        layers.append(p)
        cur, after = x2, act
    dx, d_final_g, loss_tile = final_loss(cur, final_norm_g[None], tgt, name="final_loss", tm=512)

    def exchange(parts, name):
        return start_copies(parts, [lax.empty(a.shape, a.dtype) for a in parts], scatter=True, name=name)

    def tied(g, started):
        return g + started["token"][0:1, 0:1]

    part = [None] * depth
    for l in reversed(range(depth)):
        p, s = layers[l], saved[l]
        dup0, dcw_f = ffn_bwd(dx, s["up0"], p["wd"], p["cw_f"], seq=seq, name=f"ffn_bwd_{l}", tm=256)
        g_wd = wgrad(s["act"], dx, nj=N_DEV // 2, a_mode="lead", b_mode="full", name=f"wgrad_down_{l}", tm=2048)
        g_wup = wgrad(s["h2"], dup0, nj=N_DEV, a_mode="full", b_mode="lead", name=f"wgrad_up_{l}", tm=2048)
        ffn_ex = exchange([g_wd.reshape(N_DEV, g_wd.shape[1] // 2, d), g_wup], f"exchange_ffn_{l}")
        dx1, d_ffn_g = dgrad_rms(dup0, p["wup"], s["x1"], tied(p["ffn_g"], ffn_ex), dx,
                                 name=f"dgrad_up_{l}", tm=1024, dy_3d=True)
        dproj, dcw_a, dln, dws, dbs = mixer_bwd(dx1, s["proj"], p["wout"], p["cw_a"], p["lng"], p["lnb"], p["ws"],
                                                 p["wst"], p["bias"], seq=seq, name=f"mixer_bwd_{l}", tm=128)
        g_wout = wgrad(s["merged"], dx1, nj=1, a_mode="full", b_mode="full", name=f"wgrad_out_{l}", tm=2048)
        g_win = wgrad(s["h"], dproj, nj=N_DEV, a_mode="full", b_mode="cols", name=f"wgrad_in_{l}", tm=2048)
        mix_ex = exchange([g_wout.reshape(N_DEV, d // N_DEV, d), g_win], f"exchange_mix_{l}")
        dx, d_mix_g = dgrad_rms(dproj, p["win"], s["x0"], tied(p["mix_g"], mix_ex), dx1,
                                name=f"dgrad_in_{l}", tm=1024, dy_3d=False)
        part[l] = dict(
            ffn_ex=ffn_ex, mix_ex=mix_ex,
            small=jnp.concatenate([
                dws.reshape(N_GROUPS * CHUNK * CHUNK // d, d),
                d_mix_g[0:1], d_ffn_g[0:1], dln[0:2], dcw_a[0:3],
                dbs[:, ::d // N_GROUPS].T.reshape(1, d)], axis=0),
            cw_f=dcw_f.reshape(N_DEV * HALO, f))
    grad_x = dx.reshape(nb, seq, d)

    loss_row = jnp.zeros((1, d), F32).at[0, 0].set(loss_tile[0, 0])
    small = jnp.concatenate([part[l]["small"] for l in range(depth)] + [d_final_g[0:1], loss_row], axis=0)
    small = _pad_rows(small, -(-small.shape[0] // 8) * 8)
    cwf = jnp.concatenate([part[l]["cw_f"] for l in range(depth)], axis=0)
    small_all, cwf_all = all_gather([small, cwf], name="gather_small_grads")
    small_sum = sum_devices(small_all, name="sum_small", tr=512)
    cwf_sum = sum_devices(cwf_all, name="sum_conv_ffn", tr=512)

    rows_ws = N_GROUPS * CHUNK * CHUNK // d
    per_layer = rows_ws + 8
    def small_of(l, a, b):
        return small_sum[l * per_layer + rows_ws + a:l * per_layer + rows_ws + b]
    g_ws = jnp.stack([small_sum[l * per_layer:l * per_layer + rows_ws].reshape(N_GROUPS, CHUNK, CHUNK)
                      for l in range(depth)])
    g_mix = jnp.concatenate([small_of(l, 0, 1) for l in range(depth)])
    g_ffn = jnp.concatenate([small_of(l, 1, 2) for l in range(depth)])
    g_lng = jnp.concatenate([small_of(l, 2, 3) for l in range(depth)])
    g_lnb = jnp.concatenate([small_of(l, 3, 4) for l in range(depth)])
    g_cwa_full = jnp.stack([small_of(l, 4, 7) for l in range(depth)])
    g_cwa = lax.dynamic_slice_in_dim(g_cwa_full, me * ca, ca, axis=2)
    g_bs = jnp.stack([small_of(l, 7, 8).reshape(N_GROUPS, CHUNK) for l in range(depth)])
    g_final = small_sum[depth * per_layer]
    loss = small_sum[depth * per_layer + 1, 0]
    cwf_sum = cwf_sum.reshape(depth, N_DEV, HALO, f)
    g_cwf = lax.dynamic_index_in_dim(cwf_sum, me, axis=1, keepdims=False)[:, :3]

    own, recv = {}, {}

    def arrived(l, ex, keys, after):
        srcs, lands = finish_copies(part[l][ex], [0, 1], after, name=f"wait_{ex}_{l}")
        for k, key in enumerate(keys):
            own[key, l], recv[key, l] = srcs[k], lands[k]
        return lands[1]

    def big(key, w, m, v, name):
        return adamw_sharded(me_arr, own[key, 0], recv[key, 0], own[key, 1], recv[key, 1], w, m, v, name=name, tr=256)

    after = grad_x
    for l in reversed(range(depth)):
        after = arrived(l, "ffn_ex", ("wd", "wup"), after)
        if l > 0:
            after = arrived(l, "mix_ex", ("wout", "win"), after)
    u_wd = big("wd", w_down, m_w_down, v_w_down, "adamw_w_down")
    u_wup = big("wup", w_up, m_w_up, v_w_up, "adamw_w_up")
    arrived(0, "mix_ex", ("wout", "win"), u_wup[1])
    u_wout = big("wout", w_out, m_w_out, v_w_out, "adamw_w_out")
    u_win = big("win", w_in, m_w_in, v_w_in, "adamw_w_in")

    def small_update(g, w, m, v, name):
        shape = w.shape
        two_d = (-1, shape[-1]) if w.ndim > 1 else (1, shape[0])
        out = adamw_small(g.reshape(two_d), w.reshape(two_d), m.reshape(two_d), v.reshape(two_d), name=name)
        return (g.reshape(shape),) + tuple(o.reshape(shape) for o in out)

    u_mix = small_update(g_mix, mix_norm_g, m_mix_norm_g, v_mix_norm_g, "adamw_mix_norm_g")
    u_cwa = small_update(g_cwa, conv_a_w, m_conv_a_w, v_conv_a_w, "adamw_conv_a_w")
    u_lng = small_update(g_lng, ln_v_g, m_ln_v_g, v_ln_v_g, "adamw_ln_v_g")
    u_lnb = small_update(g_lnb, ln_v_b, m_ln_v_b, v_ln_v_b, "adamw_ln_v_b")
    u_ws = small_update(g_ws, w_s, m_w_s, v_w_s, "adamw_w_s")
    u_bs = small_update(g_bs, b_s, m_b_s, v_b_s, "adamw_b_s")
    u_ffn = small_update(g_ffn, ffn_norm_g, m_ffn_norm_g, v_ffn_norm_g, "adamw_ffn_norm_g")
    u_cwf = small_update(g_cwf, conv_ffn_w, m_conv_ffn_w, v_conv_ffn_w, "adamw_conv_ffn_w")
    u_final = small_update(g_final, final_norm_g, m_final_norm_g, v_final_norm_g, "adamw_final_norm_g")

    ordered = [u_mix, u_win, u_cwa, u_lng, u_lnb, u_ws, u_bs, u_wout, u_ffn, u_wup, u_cwf, u_wd, u_final]
    return (loss, grad_x, *[u[0] for u in ordered], *[u[1] for u in ordered],
            *[u[2] for u in ordered], *[u[3] for u in ordered])
```

```python
import functools

import jax
import jax.numpy as jnp
from jax import lax
from jax.experimental import pallas as pl
from jax.experimental.pallas import tpu as pltpu

EPS = 1e-6
CHUNK = 128
N_GROUPS = 8
N_DEV = 8
HALO = 8
ADAM_LR = 0.001
ADAM_B1 = 0.9
ADAM_B2 = 0.999
ADAM_EPS = 1e-08
ADAM_WD = 0.01
ADAM_STEP = 10
VMEM_LIMIT_BYTES = 56 * 1024 * 1024
F32 = jnp.float32
BF16 = jnp.bfloat16
MESH = pl.DeviceIdType.MESH
ANY = pl.BlockSpec(memory_space=pl.ANY)
HBM_SPEC = pl.BlockSpec(memory_space=pltpu.HBM)
SEM_SPEC = pl.BlockSpec(memory_space=pltpu.SEMAPHORE)
DATAFLOW = pltpu.SideEffectType.DATAFLOW_SIDE_EFFECTING
NT_DIMS = (((1,), (1,)), ((), ()))
TN_DIMS = (((0,), (0,)), ((), ()))


def _params(n_grid_axes):
    return pltpu.CompilerParams(dimension_semantics=("arbitrary",) * n_grid_axes,
                                vmem_limit_bytes=VMEM_LIMIT_BYTES)


def _shift_down(cur, prev8, k):
    rolled = pltpu.roll(cur, k, 0)
    prolled = pltpu.roll(prev8, k, 0)
    row = lax.broadcasted_iota(jnp.int32, prev8.shape, 0)
    head = jnp.where(row < k, prolled, rolled[:HALO])
    return jnp.concatenate([head, rolled[HALO:]], axis=0)


def _shift_up(cur, next8, k):
    tm = cur.shape[0]
    rolled = pltpu.roll(cur, tm - k, 0)
    nrolled = pltpu.roll(next8, HALO - k, 0)
    row = lax.broadcasted_iota(jnp.int32, next8.shape, 0)
    tail = jnp.where(row >= HALO - k, nrolled, rolled[tm - HALO:])
    return jnp.concatenate([rolled[:tm - HALO], tail], axis=0)


def _conv_fwd(cur, prev8, cw):
    s1 = _shift_down(cur, prev8, 1)
    s2 = _shift_down(cur, prev8, 2)
    y = s2 * cw[0:1, :] + s1 * cw[1:2, :] + cur * cw[2:3, :]
    return y, s1, s2


def _conv_bwd(d, next8, cw):
    return d * cw[2:3, :] + _shift_up(d, next8, 1) * cw[1:2, :] + _shift_up(d, next8, 2) * cw[0:1, :]


def _colsum(a):
    return jnp.sum(a, axis=0, keepdims=True)


def _rms_stats(xv):
    r = lax.rsqrt(jnp.mean(xv * xv, axis=-1, keepdims=True) + EPS)
    return r, xv * r


def _rms_bwd(dh, xv, g):
    r, n = _rms_stats(xv)
    dn = dh * g
    dx = r * (dn - n * jnp.mean(dn * n, axis=-1, keepdims=True))
    return dx, _colsum(dh * n)


def _mixer_forward(p_ref, cprev, xiprev, cw, lng, lnb, ws_ref, bias_ref, mixed_scr, d):
    tm = p_ref.shape[0]
    b = p_ref[:, 0:d]
    c = p_ref[:, d:2 * d]
    xi = p_ref[:, 2 * d:3 * d]
    u = p_ref[:, 3 * d:4 * d]
    v = p_ref[:, 4 * d:5 * d]
    sa = jax.nn.sigmoid(p_ref[:, 5 * d:6 * d])
    sb = jax.nn.sigmoid(p_ref[:, 6 * d:7 * d])
    cx = c * xi
    conv, s1, s2 = _conv_fwd(cx, cprev * xiprev, cw)
    ya = b * conv
    mu = jnp.mean(v, axis=-1, keepdims=True)
    xc = v - mu
    rstd = lax.rsqrt(jnp.mean(xc * xc, axis=-1, keepdims=True) + EPS)
    vhat = xc * rstd
    vnb = (vhat * lng + lnb).astype(BF16)
    tril = (lax.broadcasted_iota(jnp.int32, (CHUNK, CHUNK), 0)
            >= lax.broadcasted_iota(jnp.int32, (CHUNK, CHUNK), 1))
    gd = d // N_GROUPS
    for g in range(N_GROUPS):
        wm = jnp.where(tril, ws_ref[g], 0.0).astype(BF16)
        cols = slice(g * gd, (g + 1) * gd)
        for n in range(tm // CHUNK):
            rows = slice(n * CHUNK, (n + 1) * CHUNK)
            mixed_scr[rows, cols] = (jnp.dot(wm, vnb[rows, cols], preferred_element_type=F32)
                                     + bias_ref[:, cols])
    mixed = mixed_scr[...]
    yb = u * mixed
    merged = sa * ya + sb * yb
    return dict(b=b, c=c, xi=xi, u=u, sa=sa, sb=sb, cx=cx, s1=s1, s2=s2, conv=conv, ya=ya,
                rstd=rstd, vhat=vhat, vnb=vnb, mixed=mixed, yb=yb, merged=merged, tril=tril)


def _once(block_shape, index_map):
    return pl.BlockSpec(block_shape, index_map, pipeline_mode=pl.Buffered(1))


def mixer_fwd(x, g, win, wout, cw, lng, lnb, ws, bias, *, seq, name, tm):
    t, d = x.shape
    nj, _, n = win.shape
    tm = min(tm, seq)
    tiles_per_seq = seq // tm

    def body(x_ref, g_ref, win_ref, wout_ref, cw_ref, lng_ref, lnb_ref, ws_ref, bias_ref,
             h_ref, p_ref, merged_ref, x1_ref, mixed_scr, carry_ref):
        @pl.when(pl.program_id(0) == 0)
        def _():
            carry_ref[...] = jnp.zeros_like(carry_ref)

        keep = jnp.where(pl.program_id(0) % tiles_per_seq == 0, 0.0, 1.0)
        xv = x_ref[...]
        _, nrm = _rms_stats(xv)
        hb = (nrm * g_ref[...]).astype(BF16)
        h_ref[...] = hb
        for j in range(nj):
            p_ref[:, j * n:(j + 1) * n] = jnp.dot(hb, win_ref[j], preferred_element_type=F32)
        f = _mixer_forward(p_ref, carry_ref[...] * keep, 1.0, cw_ref[...], lng_ref[...],
                           lnb_ref[...], ws_ref, bias_ref, mixed_scr, d)
        carry_ref[...] = f["cx"][tm - HALO:]
        mb = f["merged"].astype(BF16)
        merged_ref[...] = mb
        x1_ref[...] = xv + jnp.dot(mb, wout_ref[...], preferred_element_type=F32)

    const2 = lambda i: (0, 0)
    const3 = lambda i: (0, 0, 0)
    row = lambda i: (i, 0)
    return pl.pallas_call(
        body, name=name, grid=(t // tm,),
        in_specs=[pl.BlockSpec((tm, d), row),
                  _once((1, d), const2),
                  _once((nj, d, n), const3),
                  _once((d, d), const2),
                  _once((HALO, d), const2),
                  _once((1, d), const2),
                  _once((1, d), const2),
                  _once((N_GROUPS, CHUNK, CHUNK), const3),
                  _once((CHUNK, d), const2)],
        out_specs=[pl.BlockSpec((tm, d), row), pl.BlockSpec((tm, nj * n), row),
                   pl.BlockSpec((tm, d), row), pl.BlockSpec((tm, d), row)],
        out_shape=[jax.ShapeDtypeStruct((t, d), BF16), jax.ShapeDtypeStruct((t, nj * n), F32),
                   jax.ShapeDtypeStruct((t, d), BF16), jax.ShapeDtypeStruct((t, d), F32)],
        scratch_shapes=[pltpu.VMEM((tm, d), F32), pltpu.VMEM((HALO, d), F32)],
        compiler_params=_params(1),
    )(x, g, win, wout, cw, lng, lnb, ws, bias)


def ffn_fwd(x1, g, wup, wd, cw, *, seq, name, tm):
    t, d = x1.shape
    nj, _, f = wup.shape
    half = nj // 2
    tm = min(tm, seq)
    tiles_per_seq = seq // tm

    def body(x1_ref, g_ref, wup_ref, wd_ref, cw_ref, h2_ref, up_ref, act_ref, x2_ref, carry_ref):
        @pl.when(pl.program_id(0) == 0)
        def _():
            carry_ref[...] = jnp.zeros_like(carry_ref)

        keep = jnp.where(pl.program_id(0) % tiles_per_seq == 0, 0.0, 1.0)
        xv = x1_ref[...]
        _, nrm = _rms_stats(xv)
        hb = (nrm * g_ref[...]).astype(BF16)
        h2_ref[...] = hb

        def conv_of(j):
            up0 = jnp.dot(hb, wup_ref[j], preferred_element_type=F32)
            up_ref[j] = up0
            y, _, _ = _conv_fwd(up0, carry_ref[j] * keep, cw_ref[j])
            carry_ref[j] = up0[tm - HALO:]
            return y

        acc = xv
        for k in range(half):
            a = (jax.nn.silu(conv_of(k)) * conv_of(k + half)).astype(BF16)
            act_ref[k] = a
            acc = acc + jnp.dot(a, wd_ref[k], preferred_element_type=F32)
        x2_ref[...] = acc

    const3 = lambda i: (0, 0, 0)
    return pl.pallas_call(
        body, name=name, grid=(t // tm,),
        in_specs=[pl.BlockSpec((tm, d), lambda i: (i, 0)),
                  _once((1, d), lambda i: (0, 0)),
                  _once((nj, d, f), const3),
                  _once((half, f, d), const3),
                  _once((nj, HALO, f), const3)],
        out_specs=[pl.BlockSpec((tm, d), lambda i: (i, 0)),
                   pl.BlockSpec((nj, tm, f), lambda i: (0, i, 0)),
                   pl.BlockSpec((half, tm, f), lambda i: (0, i, 0)),
                   pl.BlockSpec((tm, d), lambda i: (i, 0))],
        out_shape=[jax.ShapeDtypeStruct((t, d), BF16), jax.ShapeDtypeStruct((nj, t, f), F32),
                   jax.ShapeDtypeStruct((half, t, f), BF16), jax.ShapeDtypeStruct((t, d), F32)],
        scratch_shapes=[pltpu.VMEM((nj, HALO, f), F32)],
        compiler_params=_params(1),
    )(x1, g, wup, wd, cw)


def final_loss(x, g, target, *, name, tm):
    t, d = x.shape
    tm = min(tm, t)

    def body(x_ref, g_ref, tgt_ref, dx_ref, dg_ref, loss_ref):
        @pl.when(pl.program_id(0) == 0)
        def _():
            dg_ref[...] = jnp.zeros_like(dg_ref)
            loss_ref[...] = jnp.zeros_like(loss_ref)

        xv = x_ref[...]
        gv = g_ref[...]
        r, n = _rms_stats(xv)
        err = n * gv - tgt_ref[...]
        loss_ref[...] += 0.5 * jnp.sum(jnp.mean(err * err, axis=-1, keepdims=True))
        dy = err * (1.0 / d)
        dn = dy * gv
        dx_ref[...] = r * (dn - n * jnp.mean(dn * n, axis=-1, keepdims=True))
        dg_ref[0:1, :] += _colsum(dy * n)

    return pl.pallas_call(
        body, name=name, grid=(t // tm,),
        in_specs=[pl.BlockSpec((tm, d), lambda i: (i, 0)),
                  pl.BlockSpec((1, d), lambda i: (0, 0)),
                  pl.BlockSpec((tm, d), lambda i: (i, 0))],
        out_specs=[pl.BlockSpec((tm, d), lambda i: (i, 0)),
                   pl.BlockSpec((HALO, d), lambda i: (0, 0)),
                   pl.BlockSpec((8, 128), lambda i: (0, 0))],
        out_shape=[jax.ShapeDtypeStruct((t, d), F32), jax.ShapeDtypeStruct((HALO, d), F32),
                   jax.ShapeDtypeStruct((8, 128), F32)],
        compiler_params=_params(1),
    )(x, g, target)


def ffn_bwd(dx2, up0, wd, cw, wup, x1, g, *, seq, name, tm):
    t, d = dx2.shape
    nj, _, f = up0.shape
    half = nj // 2
    tm = min(tm, seq)
    tiles_per_seq = seq // tm
    nt = t // tm

    def body(dx_ref, up_ref, prev_ref, wd_ref, cw_ref, wup_ref, x1_ref, g_ref,
             dup_ref, dcw_ref, dx1_ref, dg_ref, carry_ref):
        i = pl.program_id(0)
        tile = nt - 1 - i

        @pl.when(i == 0)
        def _():
            dcw_ref[...] = jnp.zeros_like(dcw_ref)
            dg_ref[...] = jnp.zeros_like(dg_ref)
            carry_ref[...] = jnp.zeros_like(carry_ref)

        keep_prev = jnp.where(tile % tiles_per_seq == 0, 0.0, 1.0)
        keep_next = jnp.where(tile % tiles_per_seq == tiles_per_seq - 1, 0.0, 1.0)
        dx2v = dx_ref[...]
        dxb = dx2v.astype(BF16)
        dh = [jnp.zeros((tm, d), F32)]

        def through_conv(j, dup):
            next8 = carry_ref[j] * keep_next
            carry_ref[j] = dup[:HALO]
            dup0 = _conv_bwd(dup, next8, cw_ref[j]).astype(BF16)
            dup_ref[j] = dup0
            dh[0] = dh[0] + lax.dot_general(dup0, wup_ref[j], NT_DIMS, preferred_element_type=F32)

        for k in range(half):
            gate, g1, g2 = _conv_fwd(up_ref[k], prev_ref[k] * keep_prev, cw_ref[k])
            val, v1, v2 = _conv_fwd(up_ref[k + half], prev_ref[k + half] * keep_prev, cw_ref[k + half])
            dact = lax.dot_general(dxb, wd_ref[k], NT_DIMS, preferred_element_type=F32)
            sg = jax.nn.sigmoid(gate)
            dgate = dact * val * (sg * (1.0 + gate * (1.0 - sg)))
            dval = dact * (gate * sg)
            dcw_ref[k, 0:1, :] += _colsum(dgate * g2)
            dcw_ref[k, 1:2, :] += _colsum(dgate * g1)
            dcw_ref[k, 2:3, :] += _colsum(dgate * up_ref[k])
            dcw_ref[k + half, 0:1, :] += _colsum(dval * v2)
            dcw_ref[k + half, 1:2, :] += _colsum(dval * v1)
            dcw_ref[k + half, 2:3, :] += _colsum(dval * up_ref[k + half])
            through_conv(k, dgate)
            through_conv(k + half, dval)

        dx, dg = _rms_bwd(dh[0], x1_ref[...], g_ref[...])
        dx1_ref[...] = dx2v + dx
        dg_ref[0:1, :] += dg

    rev = lambda i: nt - 1 - i
    return pl.pallas_call(
        body, name=name, grid=(nt,),
        in_specs=[pl.BlockSpec((tm, d), lambda i: (rev(i), 0)),
                  pl.BlockSpec((nj, tm, f), lambda i: (0, rev(i), 0)),
                  pl.BlockSpec((nj, HALO, f), lambda i: (0, jnp.maximum(rev(i) * (tm // HALO) - 1, 0), 0)),
                  _once((half, f, d), lambda i: (0, 0, 0)),
                  _once((nj, HALO, f), lambda i: (0, 0, 0)),
                  _once((nj, d, f), lambda i: (0, 0, 0)),
                  pl.BlockSpec((tm, d), lambda i: (rev(i), 0)),
                  _once((1, d), lambda i: (0, 0))],
        out_specs=[pl.BlockSpec((nj, tm, f), lambda i: (0, rev(i), 0)),
                   pl.BlockSpec((nj, HALO, f), lambda i: (0, 0, 0)),
                   pl.BlockSpec((tm, d), lambda i: (rev(i), 0)),
                   pl.BlockSpec((HALO, d), lambda i: (0, 0))],
        out_shape=[jax.ShapeDtypeStruct((nj, t, f), BF16), jax.ShapeDtypeStruct((nj, HALO, f), F32),
                   jax.ShapeDtypeStruct((t, d), F32), jax.ShapeDtypeStruct((HALO, d), F32)],
        scratch_shapes=[pltpu.VMEM((nj, HALO, f), F32)],
        compiler_params=_params(1),
    )(dx2, up0, up0, wd, cw, wup, x1, g)


def mixer_bwd(dx1, proj, wout, cw, lng, lnb, ws, wst, bias, *, seq, name, tm):
    t, d = dx1.shape
    tm = min(tm, seq)
    tiles_per_seq = seq // tm
    nt = t // tm
    gd = d // N_GROUPS

    def body(dx_ref, p_ref, cprev_ref, xiprev_ref, wout_ref, cw_ref, lng_ref, lnb_ref, ws_ref, wst_ref, bias_ref,
             dp_ref, dcw_ref, dln_ref, dws_ref, dbs_ref, mixed_scr, dvn_scr, carry_ref, dbs_acc):
        i = pl.program_id(0)
        tile = nt - 1 - i

        @pl.when(i == 0)
        def _():
            dcw_ref[...] = jnp.zeros_like(dcw_ref)
            dln_ref[...] = jnp.zeros_like(dln_ref)
            dws_ref[...] = jnp.zeros_like(dws_ref)
            dbs_acc[...] = jnp.zeros_like(dbs_acc)
            carry_ref[...] = jnp.zeros_like(carry_ref)

        keep_prev = jnp.where(tile % tiles_per_seq == 0, 0.0, 1.0)
        keep_next = jnp.where(tile % tiles_per_seq == tiles_per_seq - 1, 0.0, 1.0)
        cw = cw_ref[...]
        lng = lng_ref[...]
        f = _mixer_forward(p_ref, cprev_ref[...] * keep_prev, xiprev_ref[...], cw, lng, lnb_ref[...],
                           ws_ref, bias_ref, mixed_scr, d)
        dmerged = lax.dot_general(dx_ref[...].astype(BF16), wout_ref[...], NT_DIMS, preferred_element_type=F32)
        sa, sb = f["sa"], f["sb"]
        dp_ref[:, 5 * d:6 * d] = (dmerged * f["ya"] * (sa * (1.0 - sa))).astype(BF16)
        dp_ref[:, 6 * d:7 * d] = (dmerged * f["yb"] * (sb * (1.0 - sb))).astype(BF16)
        dya = dmerged * sa
        dyb = dmerged * sb
        dp_ref[:, 0:d] = (dya * f["conv"]).astype(BF16)
        dconv = dya * f["b"]
        dcw_ref[0:1, :] += _colsum(dconv * f["s2"])
        dcw_ref[1:2, :] += _colsum(dconv * f["s1"])
        dcw_ref[2:3, :] += _colsum(dconv * f["cx"])
        next8 = carry_ref[...] * keep_next
        carry_ref[...] = dconv[:HALO]
        dcx = _conv_bwd(dconv, next8, cw)
        dp_ref[:, d:2 * d] = (dcx * f["xi"]).astype(BF16)
        dp_ref[:, 2 * d:3 * d] = (dcx * f["c"]).astype(BF16)
        dp_ref[:, 3 * d:4 * d] = (dyb * f["mixed"]).astype(BF16)
        dmixed = dyb * f["u"]
        dmb = dmixed.astype(BF16)
        vnb = f["vnb"]
        tril = f["tril"]
        triu = (lax.broadcasted_iota(jnp.int32, (CHUNK, CHUNK), 0)
                <= lax.broadcasted_iota(jnp.int32, (CHUNK, CHUNK), 1))
        dbs_tile = dmixed[0:CHUNK]
        for n in range(1, tm // CHUNK):
            dbs_tile = dbs_tile + dmixed[n * CHUNK:(n + 1) * CHUNK]
        dbs_acc[...] += dbs_tile
        for g in range(N_GROUPS):
            wmt = jnp.where(triu, wst_ref[g], 0.0).astype(BF16)
            cols = slice(g * gd, (g + 1) * gd)
            dw = jnp.zeros((CHUNK, CHUNK), F32)
            for n in range(tm // CHUNK):
                rows = slice(n * CHUNK, (n + 1) * CHUNK)
                dvn_scr[rows, cols] = jnp.dot(wmt, dmb[rows, cols], preferred_element_type=F32)
                dw = dw + lax.dot_general(dmb[rows, cols], vnb[rows, cols], NT_DIMS, preferred_element_type=F32)
            dws_ref[g] += jnp.where(tril, dw, 0.0)
        dvn = dvn_scr[...]
        vhat = f["vhat"]
        dln_ref[0:1, :] += _colsum(dvn * vhat)
        dln_ref[1:2, :] += _colsum(dvn)
        dvh = dvn * lng
        dv = f["rstd"] * (dvh - jnp.mean(dvh, axis=-1, keepdims=True)
                          - vhat * jnp.mean(dvh * vhat, axis=-1, keepdims=True))
        dp_ref[:, 4 * d:5 * d] = dv.astype(BF16)

        @pl.when(i == nt - 1)
        def _():
            for g in range(N_GROUPS):
                cols = slice(g * gd, (g + 1) * gd)
                s = jnp.sum(dbs_acc[:, cols], axis=1, keepdims=True)
                dbs_ref[:, cols] = jnp.broadcast_to(s, (CHUNK, gd))

    rev = lambda i: nt - 1 - i

    def halo(col):
        return pl.BlockSpec((HALO, d), lambda i: (jnp.maximum(rev(i) * (tm // HALO) - 1, 0), col))

    const2 = lambda i: (0, 0)
    const3 = lambda i: (0, 0, 0)
    row = lambda i: (rev(i), 0)
    return pl.pallas_call(
        body, name=name, grid=(nt,),
        in_specs=[pl.BlockSpec((tm, d), row),
                  pl.BlockSpec((tm, 7 * d), row),
                  halo(1), halo(2),
                  _once((d, d), const2),
                  _once((HALO, d), const2),
                  _once((1, d), const2),
                  _once((1, d), const2),
                  _once((N_GROUPS, CHUNK, CHUNK), const3),
                  _once((N_GROUPS, CHUNK, CHUNK), const3),
                  _once((CHUNK, d), const2)],
        out_specs=[pl.BlockSpec((tm, 7 * d), row),
                   pl.BlockSpec((HALO, d), const2),
                   pl.BlockSpec((HALO, d), const2),
                   pl.BlockSpec((N_GROUPS, CHUNK, CHUNK), const3),
                   pl.BlockSpec((CHUNK, d), const2)],
        out_shape=[jax.ShapeDtypeStruct((t, 7 * d), BF16),
                   jax.ShapeDtypeStruct((HALO, d), F32),
                   jax.ShapeDtypeStruct((HALO, d), F32),
                   jax.ShapeDtypeStruct((N_GROUPS, CHUNK, CHUNK), F32),
                   jax.ShapeDtypeStruct((CHUNK, d), F32)],
        scratch_shapes=[pltpu.VMEM((tm, d), F32), pltpu.VMEM((tm, d), F32),
                        pltpu.VMEM((HALO, d), F32), pltpu.VMEM((CHUNK, d), F32)],
        compiler_params=_params(1),
    )(dx1, proj, proj, proj, wout, cw, lng, lnb, ws, wst, bias)


def dgrad_rms(dy, w, x, g, res, *, name, tm):
    t, d = x.shape
    nj, _, n = w.shape
    tm = min(tm, t)

    def body(dy_ref, w_ref, x_ref, g_ref, res_ref, dx_ref, dg_ref, acc_ref):
        i, j = pl.program_id(0), pl.program_id(1)

        @pl.when((i == 0) & (j == 0))
        def _():
            dg_ref[...] = jnp.zeros_like(dg_ref)

        part = lax.dot_general(dy_ref[...], w_ref[...], NT_DIMS, preferred_element_type=F32)

        @pl.when(j == 0)
        def _():
            acc_ref[...] = part

        @pl.when(j > 0)
        def _():
            acc_ref[...] += part

        @pl.when(j == nj - 1)
        def _():
            dx, dg = _rms_bwd(acc_ref[...], x_ref[...], g_ref[...])
            dx_ref[...] = res_ref[...] + dx
            dg_ref[0:1, :] += dg

    return pl.pallas_call(
        body, name=name, grid=(t // tm, nj),
        in_specs=[pl.BlockSpec((tm, n), lambda i, j: (i, j)),
                  pl.BlockSpec((None, d, n), lambda i, j: (j, 0, 0)),
                  pl.BlockSpec((tm, d), lambda i, j: (i, 0)),
                  pl.BlockSpec((1, d), lambda i, j: (0, 0)),
                  pl.BlockSpec((tm, d), lambda i, j: (i, 0))],
        out_specs=[pl.BlockSpec((tm, d), lambda i, j: (i, 0)), pl.BlockSpec((HALO, d), lambda i, j: (0, 0))],
        out_shape=[jax.ShapeDtypeStruct((t, d), F32), jax.ShapeDtypeStruct((HALO, d), F32)],
        scratch_shapes=[pltpu.VMEM((tm, d), F32)],
        compiler_params=_params(2),
    )(dy, w, x, g, res)


def wgrad(a, b, *, nj, a_mode, b_mode, name, tm):
    def describe(arr, mode):
        if mode == "full":
            return arr.shape[0], arr.shape[1], pl.BlockSpec((tm_, arr.shape[1]), lambda j, s: (s, 0))
        if mode == "cols":
            c = arr.shape[1] // nj
            return arr.shape[0], c, pl.BlockSpec((tm_, c), lambda j, s: (s, j))
        return arr.shape[1], arr.shape[2], pl.BlockSpec((None, tm_, arr.shape[2]), lambda j, s: (j, s, 0))

    t = a.shape[0] if a_mode != "lead" else a.shape[1]
    tm_ = min(tm, t)
    _, k, a_spec = describe(a, a_mode)
    _, n, b_spec = describe(b, b_mode)

    ns = t // tm_

    def body(a_ref, b_ref, o_ref, acc_ref):
        s = pl.program_id(1)
        part = lax.dot_general(a_ref[...], b_ref[...], TN_DIMS, preferred_element_type=F32)
        if ns == 1:
            o_ref[...] = part.astype(BF16)
            return

        @pl.when(s == 0)
        def _():
            acc_ref[...] = part

        @pl.when((s > 0) & (s < ns - 1))
        def _():
            acc_ref[...] += part

        @pl.when(s == ns - 1)
        def _():
            o_ref[...] = (acc_ref[...] + part).astype(BF16)

    return pl.pallas_call(
        body, name=name, grid=(nj, ns),
        in_specs=[a_spec, b_spec],
        out_specs=pl.BlockSpec((None, k, n), lambda j, s: (j, 0, 0)),
        out_shape=jax.ShapeDtypeStruct((nj, k, n), BF16),
        scratch_shapes=[pltpu.VMEM((k, n), F32)],
        compiler_params=_params(2),
    )(a, b)


def _adamw_math(w, g, m, v):
    m = ADAM_B1 * m + (1.0 - ADAM_B1) * g
    v = ADAM_B2 * v + (1.0 - ADAM_B2) * (g * g)
    m_hat = m / (1.0 - ADAM_B1 ** ADAM_STEP)
    v_hat = v / (1.0 - ADAM_B2 ** ADAM_STEP)
    delta = -ADAM_LR * (m_hat / (jnp.sqrt(v_hat) + ADAM_EPS) + ADAM_WD * w)
    return delta, m, v


def _row_tile(rows, at_most):
    if rows <= at_most:
        return rows
    return max(k for k in range(16, at_most + 1, 16) if rows % k == 0)


def _sum_in_device_order(ref):
    total = ref[0]
    for s in range(1, N_DEV):
        total = total + ref[s]
    return total


def adamw_sharded(me, own0, recv0, own1, recv1, w, m, v, *, name, tr):
    _, r, c = w.shape
    tr = _row_tile(r, tr)
    ni = r // tr

    def body(me_ref, o0_ref, r0_ref, o1_ref, r1_ref, w_ref, m_ref, v_ref, g_ref, d_ref, nm_ref, nv_ref):
        def finish(own_ref, recv_ref):
            g = None
            for s in range(N_DEV):
                term = jnp.where(me_ref[0] == s, own_ref[...], recv_ref[s]).astype(F32)
                g = term if g is None else g + term
            delta, nm, nv = _adamw_math(w_ref[...], g, m_ref[...], v_ref[...])
            g_ref[...] = g
            d_ref[...] = delta
            nm_ref[...] = nm
            nv_ref[...] = nv

        @pl.when(pl.program_id(0) == 0)
        def _():
            finish(o0_ref, r0_ref)

        @pl.when(pl.program_id(0) == 1)
        def _():
            finish(o1_ref, r1_ref)

    row0 = lambda l, i: i * (1 - l) + (ni - 1) * l
    row1 = lambda l, i: i * l
    lay = pl.BlockSpec((None, tr, c), lambda l, i, me_ref: (l, i, 0))
    grid_spec = pltpu.PrefetchScalarGridSpec(
        num_scalar_prefetch=1, grid=(2, ni),
        in_specs=[pl.BlockSpec((None, tr, c), lambda l, i, me_ref: (me_ref[0], row0(l, i), 0)),
                  pl.BlockSpec((N_DEV, tr, c), lambda l, i, me_ref: (0, row0(l, i), 0)),
                  pl.BlockSpec((None, tr, c), lambda l, i, me_ref: (me_ref[0], row1(l, i), 0)),
                  pl.BlockSpec((N_DEV, tr, c), lambda l, i, me_ref: (0, row1(l, i), 0)),
                  lay, lay, lay],
        out_specs=[lay, lay, lay, lay])
    return pl.pallas_call(
        body, name=name, grid_spec=grid_spec,
        out_shape=[jax.ShapeDtypeStruct(w.shape, F32)] * 4,
        compiler_params=_params(2),
    )(me, own0, recv0, own1, recv1, w, m, v)


def adamw_small(g, w, m, v, *, name):
    def body(g_ref, w_ref, m_ref, v_ref, d_ref, nm_ref, nv_ref):
        delta, nm, nv = _adamw_math(w_ref[...], g_ref[...], m_ref[...], v_ref[...])
        d_ref[...] = delta
        nm_ref[...] = nm
        nv_ref[...] = nv

    return pl.pallas_call(
        body, name=name,
        out_shape=[jax.ShapeDtypeStruct(w.shape, F32)] * 3,
        compiler_params=pltpu.CompilerParams(vmem_limit_bytes=VMEM_LIMIT_BYTES),
    )(g, w, m, v)


def sum_devices(parts, *, name, tr):
    _, r, c = parts.shape
    tr = min(tr, r)

    def body(p_ref, o_ref):
        o_ref[...] = _sum_in_device_order(p_ref)

    return pl.pallas_call(
        body, name=name, grid=(r // tr,),
        in_specs=[pl.BlockSpec((N_DEV, tr, c), lambda i: (0, i, 0))],
        out_specs=pl.BlockSpec((tr, c), lambda i: (i, 0)),
        out_shape=jax.ShapeDtypeStruct((r, c), F32),
        compiler_params=_params(1),
    )(parts)


def _my_place():
    return lax.axis_index("x"), lax.axis_index("y"), lax.axis_index("c")


def all_gather(arrays, *, name):
    n = len(arrays)

    def body(*refs):
        ins, outs = refs[:n], refs[n:2 * n]
        send_sems, recv_sems, local_sems = refs[2 * n:]
        x, y, c = _my_place()
        me, sibling = (x, y, c), (x, y, 1 - c)
        chips = [(1 - x, y), (x, 1 - y), (1 - x, 1 - y)]
        waits = []
        for a in range(n):
            def slot(place, a=a):
                px, py, pc = place
                return outs[a].at[4 * px + 2 * py + pc]

            def copy(k, block, to, src=None, a=a, slot=slot):
                return pltpu.make_async_remote_copy(
                    src_ref=slot(block) if src is None else src, dst_ref=slot(block),
                    send_sem=send_sems.at[a, k], recv_sem=recv_sems.at[a, k],
                    device_id=to, device_id_type=MESH)

            mine = pltpu.make_async_copy(ins[a], slot(me), local_sems.at[a])
            mine.start()
            first = [copy(0, me, sibling, src=ins[a])]
            first += [copy(1 + j, me, (*chip, c), src=ins[a]) for j, chip in enumerate(chips)]
            for cp in first:
                cp.start()
            waits.append((copy, mine, first))
        sends = []
        for a in range(n):
            copy, mine, first = waits[a]
            passed = [copy(4 + j, (*chip, c), sibling) for j, chip in enumerate(chips)]
            for j, chip in enumerate(chips):
                copy(1 + j, (*chip, c), me).wait_recv()
                passed[j].start()
            sends.append(first + passed)
        for a in range(n):
            copy, mine, first = waits[a]
            copy(0, sibling, me).wait_recv()
            for j, chip in enumerate(chips):
                copy(4 + j, (*chip, 1 - c), me).wait_recv()
            for cp in sends[a]:
                cp.wait_send()
            mine.wait()

    return pl.pallas_call(
        body, name=name,
        in_specs=[ANY] * n, out_specs=[ANY] * n,
        out_shape=[jax.ShapeDtypeStruct((N_DEV,) + a.shape, a.dtype) for a in arrays],
        scratch_shapes=[pltpu.SemaphoreType.DMA((n, 7)), pltpu.SemaphoreType.DMA((n, 7)),
                        pltpu.SemaphoreType.DMA((n,))],
        compiler_params=pltpu.CompilerParams(has_side_effects=True),
    )(*arrays)


def _peer_place(r, x, y, c):
    fx, fy, fc = (r >> 2) & 1, (r >> 1) & 1, r & 1
    return (1 - x if fx else x, 1 - y if fy else y, 1 - c if fc else c)


def own_slot(me, w, layer, dtype, *, name, tr):
    _, r, c = w.shape
    tr = _row_tile(r, tr)

    def body(me_ref, w_ref, o_ref):
        o_ref[...] = w_ref[...].astype(dtype)

    grid_spec = pltpu.PrefetchScalarGridSpec(
        num_scalar_prefetch=1, grid=(r // tr,),
        in_specs=[pl.BlockSpec((None, tr, c), lambda i, me_ref: (layer, i, 0))],
        out_specs=pl.BlockSpec((None, tr, c), lambda i, me_ref: (me_ref[0], i, 0)))
    return pl.pallas_call(
        body, name=name, grid_spec=grid_spec,
        out_shape=jax.ShapeDtypeStruct((N_DEV, r, c), dtype),
        compiler_params=_params(1),
    )(me, w)


def _split_copy(r, src, land, send_sem, recv_sem, scatter, arriving):
    x, y, c = _my_place()
    px, py, pc = _peer_place(r, x, y, c)
    theirs, mine = 4 * px + 2 * py + pc, 4 * x + 2 * y + c
    return pltpu.make_async_remote_copy(
        src_ref=src.at[theirs] if scatter else land.at[mine], dst_ref=land.at[theirs if arriving else mine],
        send_sem=send_sem.at[r - 1], recv_sem=recv_sem.at[r - 1],
        device_id=(px, py, pc), device_id_type=MESH)


def start_copies(srcs, lands, *, scatter, name):
    n = len(lands)
    bufs = (list(srcs) if scatter else []) + list(lands)
    nb = len(bufs)

    def body(*refs):
        src = refs[:n] if scatter else [None] * n
        land = refs[nb - n:nb]
        send_sems, recv_sems = refs[nb:nb + n], refs[nb + n:nb + 2 * n]
        token = refs[2 * nb + 2 * n]
        for a in range(n):
            for r in range(1, N_DEV):
                _split_copy(r, src[a], land[a], send_sems[a], recv_sems[a], scatter, False).start()
        token[...] = jnp.zeros_like(token)

    outs = pl.pallas_call(
        body, name=name,
        in_specs=[HBM_SPEC] * nb,
        out_specs=[SEM_SPEC] * (2 * n) + [HBM_SPEC] * nb + [pl.BlockSpec(memory_space=pltpu.VMEM)],
        out_shape=([pltpu.SemaphoreType.DMA((N_DEV - 1,))] * (2 * n)
                   + [pltpu.HBM(a.shape, a.dtype) for a in bufs]
                   + [jax.ShapeDtypeStruct((8, 128), F32)]),
        input_output_aliases={i: 2 * n + i for i in range(nb)},
        compiler_params=pltpu.CompilerParams(has_side_effects=DATAFLOW),
    )(*[pltpu.with_memory_space_constraint(a, pltpu.HBM) for a in bufs])
    thru = list(outs[2 * n:2 * n + nb])
    return dict(send=outs[:n], recv=outs[n:2 * n], src=thru[:n] if scatter else None, land=thru[nb - n:],
                token=outs[2 * n + nb], scatter=scatter)


def finish_copies(started, which, after, *, name):
    n = len(which)
    scatter = started["scatter"]
    bufs = ([started["src"][i] for i in which] if scatter else []) + [started["land"][i] for i in which]
    nb = len(bufs)

    def body(*refs):
        src = refs[:n] if scatter else [None] * n
        land = refs[nb - n:nb]
        send_sems, recv_sems = refs[nb:nb + n], refs[nb + n:nb + 2 * n]
        for a in range(n):
            for r in range(1, N_DEV):
                cp = _split_copy(r, src[a], land[a], send_sems[a], recv_sems[a], scatter, True)
                cp.wait_send()
                cp.wait_recv()

    outs = pl.pallas_call(
        body, name=name,
        in_specs=[HBM_SPEC] * nb + [SEM_SPEC] * (2 * n) + [ANY],
        out_specs=[HBM_SPEC] * nb,
        out_shape=[pltpu.HBM(a.shape, a.dtype) for a in bufs],
        input_output_aliases={i: i for i in range(nb)},
        compiler_params=pltpu.CompilerParams(has_side_effects=DATAFLOW),
    )(*bufs, *[started["send"][i] for i in which], *[started["recv"][i] for i in which], after)
    return (list(outs[:n]) if scatter else None), list(outs[nb - n:])


def _pad_rows(a, rows):
    pad = [(0, 0)] * a.ndim
    pad[-2] = (0, rows - a.shape[-2])
    return jnp.pad(a, pad)


def kernel(x, mix_norm_g, w_in, conv_a_w, ln_v_g, ln_v_b, w_s, b_s, w_out, ffn_norm_g, w_up, conv_ffn_w, w_down, final_norm_g, loss_target, m_mix_norm_g, m_w_in, m_conv_a_w, m_ln_v_g, m_ln_v_b, m_w_s, m_b_s, m_w_out, m_ffn_norm_g, m_w_up, m_conv_ffn_w, m_w_down, m_final_norm_g, v_mix_norm_g, v_w_in, v_conv_a_w, v_ln_v_g, v_ln_v_b, v_w_s, v_b_s, v_w_out, v_ffn_norm_g, v_w_up, v_conv_ffn_w, v_w_down, v_final_norm_g):
    nb, seq, d = x.shape
    t = nb * seq
    depth = w_in.shape[0]
    f = w_up.shape[2]
    me = 4 * lax.axis_index("x") + 2 * lax.axis_index("y") + lax.axis_index("c")
    xt = x.reshape(t, d)
    tgt = loss_target.reshape(t, d)

    conv_pack = jnp.concatenate([_pad_rows(conv_a_w, HALO), _pad_rows(conv_ffn_w, HALO)], axis=-1)
    me_arr = me.astype(jnp.int32).reshape(1)
    zones, slot_of = [], {}
    for l in range(depth):
        for key, w in (("win", w_in), ("conv", None), ("wout", w_out), ("wup", w_up), ("wd", w_down)):
            if key == "conv":
                if l == 0:
                    slot_of["conv"] = len(zones)
                    packed = conv_pack.reshape(1, depth * HALO, conv_pack.shape[-1])
                    zones.append(own_slot(me_arr, packed, 0, F32, name="own_slot_conv", tr=256))
                continue
            slot_of[key, l] = len(zones)
            zones.append(own_slot(me_arr, w, l, BF16, name=f"own_slot_{key}_{l}", tr=256))
    gathering = start_copies(None, zones, scatter=False, name="gather_start")

    def gathered(keys, after, name):
        return finish_copies(gathering, [slot_of[k] for k in keys], after, name=name)[1]

    saved, layers = [], []
    cur = xt
    after = gathering["token"]
    for l in range(depth):
        p = dict(mix_g=mix_norm_g[l][None], ffn_g=ffn_norm_g[l][None], lng=ln_v_g[l][None], lnb=ln_v_b[l][None],
                 ws=w_s[l], wst=jnp.swapaxes(w_s[l], 1, 2),
                 bias=jnp.repeat(b_s[l].T, d // N_GROUPS, axis=1))
        if l == 0:
            p["win"], wout_g, conv_g = gathered([("win", l), ("wout", l), "conv"], after, f"wait_w_mixer_{l}")
            conv_g = conv_g.reshape(N_DEV, depth, HALO, -1)
            ca = conv_g.shape[-1] - f
        else:
            p["win"], wout_g = gathered([("win", l), ("wout", l)], after, f"wait_w_mixer_{l}")
        p["wout"] = wout_g.reshape(d, d)
        p["cw_a"] = jnp.transpose(conv_g[:, l, :, :ca], (1, 0, 2)).reshape(HALO, d)
        p["cw_f"] = conv_g[:, l, :, ca:]
        h, proj, merged, x1 = mixer_fwd(cur, p["mix_g"], p["win"], p["wout"], p["cw_a"], p["lng"], p["lnb"], p["ws"],
                                        p["bias"], seq=seq, name=f"mixer_fwd_{l}", tm=128)
        p["wup"], wd_g = gathered([("wup", l), ("wd", l)], merged, f"wait_w_ffn_{l}")
        p["wd"] = wd_g.reshape(N_DEV // 2, 2 * wd_g.shape[1], d)
        h2, up0, act, x2 = ffn_fwd(x1, p["ffn_g"], p["wup"], p["wd"], p["cw_f"], seq=seq, name=f"ffn_fwd_{l}", tm=256)
        saved.append(dict(x0=cur, h=h, proj=proj, merged=merged, x1=x1, h2=h2, up0=up0, act=act))
        layers.append(p)
        cur, after = x2, act
    dx, d_final_g, loss_tile = final_loss(cur, final_norm_g[None], tgt, name="final_loss", tm=512)

    def exchange(parts, name):
        return start_copies(parts, [lax.empty(a.shape, a.dtype) for a in parts], scatter=True, name=name)

    def tied(g, started):
        return g + started["token"][0:1, 0:1]

    part = [None] * depth
    for l in reversed(range(depth)):
        p, s = layers[l], saved[l]
        dup0, dcw_f, dx1, d_ffn_g = ffn_bwd(dx, s["up0"], p["wd"], p["cw_f"], p["wup"], s["x1"], p["ffn_g"],
                                            seq=seq, name=f"ffn_bwd_{l}", tm=256)
        g_wd = wgrad(s["act"], dx, nj=N_DEV // 2, a_mode="lead", b_mode="full", name=f"wgrad_down_{l}", tm=2048)
        g_wup = wgrad(s["h2"], dup0, nj=N_DEV, a_mode="full", b_mode="lead", name=f"wgrad_up_{l}", tm=2048)
        ffn_ex = exchange([g_wd.reshape(N_DEV, g_wd.shape[1] // 2, d), g_wup], f"exchange_ffn_{l}")
        dproj, dcw_a, dln, dws, dbs = mixer_bwd(dx1, s["proj"], p["wout"], tied(p["cw_a"], ffn_ex), p["lng"], p["lnb"],
                                                 p["ws"], p["wst"], p["bias"], seq=seq, name=f"mixer_bwd_{l}", tm=128)
        g_wout = wgrad(s["merged"], dx1, nj=1, a_mode="full", b_mode="full", name=f"wgrad_out_{l}", tm=2048)
        g_win = wgrad(s["h"], dproj, nj=N_DEV, a_mode="full", b_mode="cols", name=f"wgrad_in_{l}", tm=2048)
        mix_ex = exchange([g_wout.reshape(N_DEV, d // N_DEV, d), g_win], f"exchange_mix_{l}")
        dx, d_mix_g = dgrad_rms(dproj, p["win"], s["x0"], tied(p["mix_g"], mix_ex), dx1, name=f"dgrad_in_{l}", tm=1024)
        part[l] = dict(
            ffn_ex=ffn_ex, mix_ex=mix_ex,
            small=jnp.concatenate([
                dws.reshape(N_GROUPS * CHUNK * CHUNK // d, d),
                d_mix_g[0:1], d_ffn_g[0:1], dln[0:2], dcw_a[0:3],
                dbs[:, ::d // N_GROUPS].T.reshape(1, d)], axis=0),
            cw_f=dcw_f.reshape(N_DEV * HALO, f))
    grad_x = dx.reshape(nb, seq, d)

    loss_row = jnp.zeros((1, d), F32).at[0, 0].set(loss_tile[0, 0])
    small = jnp.concatenate([part[l]["small"] for l in range(depth)] + [d_final_g[0:1], loss_row], axis=0)
    small = _pad_rows(small, -(-small.shape[0] // 8) * 8)
    cwf = jnp.concatenate([part[l]["cw_f"] for l in range(depth)], axis=0)
    small_all, cwf_all = all_gather([small, cwf], name="gather_small_grads")
    small_sum = sum_devices(small_all, name="sum_small", tr=512)
    cwf_sum = sum_devices(cwf_all, name="sum_conv_ffn", tr=512)


    rows_ws = N_GROUPS * CHUNK * CHUNK // d
    per_layer = rows_ws + 8
    def small_of(l, a, b):
        return small_sum[l * per_layer + rows_ws + a:l * per_layer + rows_ws + b]
    g_ws = jnp.stack([small_sum[l * per_layer:l * per_layer + rows_ws].reshape(N_GROUPS, CHUNK, CHUNK)
                      for l in range(depth)])
    g_mix = jnp.concatenate([small_of(l, 0, 1) for l in range(depth)])
    g_ffn = jnp.concatenate([small_of(l, 1, 2) for l in range(depth)])
    g_lng = jnp.concatenate([small_of(l, 2, 3) for l in range(depth)])
    g_lnb = jnp.concatenate([small_of(l, 3, 4) for l in range(depth)])
    g_cwa_full = jnp.stack([small_of(l, 4, 7) for l in range(depth)])
    g_cwa = lax.dynamic_slice_in_dim(g_cwa_full, me * ca, ca, axis=2)
    g_bs = jnp.stack([small_of(l, 7, 8).reshape(N_GROUPS, CHUNK) for l in range(depth)])
    g_final = small_sum[depth * per_layer]
    loss = small_sum[depth * per_layer + 1, 0]
    cwf_sum = cwf_sum.reshape(depth, N_DEV, HALO, f)
    g_cwf = lax.dynamic_index_in_dim(cwf_sum, me, axis=1, keepdims=False)[:, :3]

    own, recv = {}, {}

    def arrived(l, ex, keys, after):
        srcs, lands = finish_copies(part[l][ex], [0, 1], after, name=f"wait_{ex}_{l}")
        for k, key in enumerate(keys):
            own[key, l], recv[key, l] = srcs[k], lands[k]
        return lands[1]

    def big(key, w, m, v, name):
        return adamw_sharded(me_arr, own[key, 0], recv[key, 0], own[key, 1], recv[key, 1], w, m, v, name=name, tr=256)

    after = grad_x
    for l in reversed(range(depth)):
        after = arrived(l, "ffn_ex", ("wd", "wup"), after)
        if l > 0:
            after = arrived(l, "mix_ex", ("wout", "win"), after)
    u_wd = big("wd", w_down, m_w_down, v_w_down, "adamw_w_down")
    u_wup = big("wup", w_up, m_w_up, v_w_up, "adamw_w_up")
    arrived(0, "mix_ex", ("wout", "win"), u_wup[1])
    u_wout = big("wout", w_out, m_w_out, v_w_out, "adamw_w_out")
    u_win = big("win", w_in, m_w_in, v_w_in, "adamw_w_in")

    def small_update(g, w, m, v, name):
        shape = w.shape
        two_d = (-1, shape[-1]) if w.ndim > 1 else (1, shape[0])
        out = adamw_small(g.reshape(two_d), w.reshape(two_d), m.reshape(two_d), v.reshape(two_d), name=name)
        return (g.reshape(shape),) + tuple(o.reshape(shape) for o in out)

    u_mix = small_update(g_mix, mix_norm_g, m_mix_norm_g, v_mix_norm_g, "adamw_mix_norm_g")
    u_cwa = small_update(g_cwa, conv_a_w, m_conv_a_w, v_conv_a_w, "adamw_conv_a_w")
    u_lng = small_update(g_lng, ln_v_g, m_ln_v_g, v_ln_v_g, "adamw_ln_v_g")
    u_lnb = small_update(g_lnb, ln_v_b, m_ln_v_b, v_ln_v_b, "adamw_ln_v_b")
    u_ws = small_update(g_ws, w_s, m_w_s, v_w_s, "adamw_w_s")
    u_bs = small_update(g_bs, b_s, m_b_s, v_b_s, "adamw_b_s")
    u_ffn = small_update(g_ffn, ffn_norm_g, m_ffn_norm_g, v_ffn_norm_g, "adamw_ffn_norm_g")
    u_cwf = small_update(g_cwf, conv_ffn_w, m_conv_ffn_w, v_conv_ffn_w, "adamw_conv_ffn_w")
    u_final = small_update(g_final, final_norm_g, m_final_norm_g, v_final_norm_g, "adamw_final_norm_g")

    ordered = [u_mix, u_win, u_cwa, u_lng, u_lnb, u_ws, u_bs, u_wout, u_ffn, u_wup, u_cwf, u_wd, u_final]
    return (loss, grad_x, *[u[0] for u in ordered], *[u[1] for u in ordered],
            *[u[2] for u in ordered], *[u[3] for u in ordered])
```

```python
import functools

import jax
import jax.numpy as jnp
from jax import lax
from jax.experimental import pallas as pl
from jax.experimental.pallas import tpu as pltpu

EPS = 1e-6
CHUNK = 128
N_GROUPS = 8
N_DEV = 8
HALO = 8
ADAM_LR = 0.001
ADAM_B1 = 0.9
ADAM_B2 = 0.999
ADAM_EPS = 1e-08
ADAM_WD = 0.01
ADAM_STEP = 10
VMEM_LIMIT_BYTES = 56 * 1024 * 1024
F32 = jnp.float32
BF16 = jnp.bfloat16
MESH = pl.DeviceIdType.MESH
ANY = pl.BlockSpec(memory_space=pl.ANY)
HBM_SPEC = pl.BlockSpec(memory_space=pltpu.HBM)
SEM_SPEC = pl.BlockSpec(memory_space=pltpu.SEMAPHORE)
DATAFLOW = pltpu.SideEffectType.DATAFLOW_SIDE_EFFECTING
NT_DIMS = (((1,), (1,)), ((), ()))
TN_DIMS = (((0,), (0,)), ((), ()))


def _params(n_grid_axes):
    return pltpu.CompilerParams(dimension_semantics=("arbitrary",) * n_grid_axes,
                                vmem_limit_bytes=VMEM_LIMIT_BYTES)


def _shift_down(cur, prev8, k):
    rolled = pltpu.roll(cur, k, 0)
    prolled = pltpu.roll(prev8, k, 0)
    row = lax.broadcasted_iota(jnp.int32, prev8.shape, 0)
    head = jnp.where(row < k, prolled, rolled[:HALO])
    return jnp.concatenate([head, rolled[HALO:]], axis=0)


def _shift_up(cur, next8, k):
    tm = cur.shape[0]
    rolled = pltpu.roll(cur, tm - k, 0)
    nrolled = pltpu.roll(next8, HALO - k, 0)
    row = lax.broadcasted_iota(jnp.int32, next8.shape, 0)
    tail = jnp.where(row >= HALO - k, nrolled, rolled[tm - HALO:])
    return jnp.concatenate([rolled[:tm - HALO], tail], axis=0)


def _conv_fwd(cur, prev8, cw):
    s1 = _shift_down(cur, prev8, 1)
    s2 = _shift_down(cur, prev8, 2)
    y = s2 * cw[0:1, :] + s1 * cw[1:2, :] + cur * cw[2:3, :]
    return y, s1, s2


def _conv_bwd(d, next8, cw):
    u1 = _shift_up(d, next8, 1)
    u2 = _shift_up(d, next8, 2)
    return d * cw[2:3, :] + u1 * cw[1:2, :] + u2 * cw[0:1, :], u1, u2


def _colsum(a):
    return jnp.sum(a, axis=0, keepdims=True)


def _rms_stats(xv):
    r = lax.rsqrt(jnp.mean(xv * xv, axis=-1, keepdims=True) + EPS)
    return r, xv * r


def _rms_bwd(dh, xv, g):
    r, n = _rms_stats(xv)
    dn = dh * g
    dx = r * (dn - n * jnp.mean(dn * n, axis=-1, keepdims=True))
    return dx, _colsum(dh * n)


def _mixer_forward(p_ref, cprev, xiprev, cw, lng, lnb, ws_ref, bias_ref, mixed_scr, d):
    tm = p_ref.shape[0]
    b = p_ref[:, 0:d]
    c = p_ref[:, d:2 * d]
    xi = p_ref[:, 2 * d:3 * d]
    u = p_ref[:, 3 * d:4 * d]
    v = p_ref[:, 4 * d:5 * d]
    sa = jax.nn.sigmoid(p_ref[:, 5 * d:6 * d])
    sb = jax.nn.sigmoid(p_ref[:, 6 * d:7 * d])
    cx = c * xi
    conv, s1, s2 = _conv_fwd(cx, cprev * xiprev, cw)
    ya = b * conv
    mu = jnp.mean(v, axis=-1, keepdims=True)
    xc = v - mu
    rstd = lax.rsqrt(jnp.mean(xc * xc, axis=-1, keepdims=True) + EPS)
    vhat = xc * rstd
    vnb = (vhat * lng + lnb).astype(BF16)
    tril = (lax.broadcasted_iota(jnp.int32, (CHUNK, CHUNK), 0)
            >= lax.broadcasted_iota(jnp.int32, (CHUNK, CHUNK), 1))
    gd = d // N_GROUPS
    for g in range(N_GROUPS):
        wm = jnp.where(tril, ws_ref[g], 0.0).astype(BF16)
        cols = slice(g * gd, (g + 1) * gd)
        for n in range(tm // CHUNK):
            rows = slice(n * CHUNK, (n + 1) * CHUNK)
            mixed_scr[rows, cols] = (jnp.dot(wm, vnb[rows, cols], preferred_element_type=F32)
                                     + bias_ref[:, cols])
    mixed = mixed_scr[...]
    yb = u * mixed
    merged = sa * ya + sb * yb
    return dict(b=b, c=c, xi=xi, u=u, sa=sa, sb=sb, cx=cx, s1=s1, s2=s2, conv=conv, ya=ya,
                rstd=rstd, vhat=vhat, vnb=vnb, mixed=mixed, yb=yb, merged=merged, tril=tril)


def _once(block_shape, index_map):
    return pl.BlockSpec(block_shape, index_map, pipeline_mode=pl.Buffered(1))


def mixer_fwd(x, g, win, wout, cw, lng, lnb, ws, bias, *, seq, name, tm):
    t, d = x.shape
    nj, _, n = win.shape
    tm = min(tm, seq)
    tiles_per_seq = seq // tm

    def body(x_ref, g_ref, win_ref, wout_ref, cw_ref, lng_ref, lnb_ref, ws_ref, bias_ref,
             h_ref, p_ref, merged_ref, x1_ref, mixed_scr, carry_ref):
        @pl.when(pl.program_id(0) == 0)
        def _():
            carry_ref[...] = jnp.zeros_like(carry_ref)

        keep = jnp.where(pl.program_id(0) % tiles_per_seq == 0, 0.0, 1.0)
        xv = x_ref[...]
        _, nrm = _rms_stats(xv)
        hb = (nrm * g_ref[...]).astype(BF16)
        h_ref[...] = hb
        for j in range(0, nj, 2):
            pair = jnp.concatenate([win_ref[j], win_ref[j + 1]], axis=1)
            p_ref[:, j * n:(j + 2) * n] = jnp.dot(hb, pair, preferred_element_type=F32)
        f = _mixer_forward(p_ref, carry_ref[...] * keep, 1.0, cw_ref[...], lng_ref[...],
                           lnb_ref[...], ws_ref, bias_ref, mixed_scr, d)
        carry_ref[...] = f["cx"][tm - HALO:]
        mb = f["merged"].astype(BF16)
        merged_ref[...] = mb
        x1_ref[...] = xv + jnp.dot(mb, wout_ref[...], preferred_element_type=F32)

    const2 = lambda i: (0, 0)
    const3 = lambda i: (0, 0, 0)
    row = lambda i: (i, 0)
    return pl.pallas_call(
        body, name=name, grid=(t // tm,),
        in_specs=[pl.BlockSpec((tm, d), row),
                  _once((1, d), const2),
                  _once((nj, d, n), const3),
                  _once((d, d), const2),
                  _once((HALO, d), const2),
                  _once((1, d), const2),
                  _once((1, d), const2),
                  _once((N_GROUPS, CHUNK, CHUNK), const3),
                  _once((CHUNK, d), const2)],
        out_specs=[pl.BlockSpec((tm, d), row), pl.BlockSpec((tm, nj * n), row),
                   pl.BlockSpec((tm, d), row), pl.BlockSpec((tm, d), row)],
        out_shape=[jax.ShapeDtypeStruct((t, d), BF16), jax.ShapeDtypeStruct((t, nj * n), F32),
                   jax.ShapeDtypeStruct((t, d), BF16), jax.ShapeDtypeStruct((t, d), F32)],
        scratch_shapes=[pltpu.VMEM((tm, d), F32), pltpu.VMEM((HALO, d), F32)],
        compiler_params=_params(1),
    )(x, g, win, wout, cw, lng, lnb, ws, bias)


def ffn_fwd(x1, g, wup, wd, cw, *, seq, name, tm):
    t, d = x1.shape
    nj, _, f = wup.shape
    half = nj // 2
    tm = min(tm, seq)
    tiles_per_seq = seq // tm

    def body(x1_ref, g_ref, wup_ref, wd_ref, cw_ref, h2_ref, up_ref, upc_ref, act_ref, x2_ref, carry_ref):
        @pl.when(pl.program_id(0) == 0)
        def _():
            carry_ref[...] = jnp.zeros_like(carry_ref)

        keep = jnp.where(pl.program_id(0) % tiles_per_seq == 0, 0.0, 1.0)
        xv = x1_ref[...]
        _, nrm = _rms_stats(xv)
        hb = (nrm * g_ref[...]).astype(BF16)
        h2_ref[...] = hb

        def conv_of(j):
            up0 = jnp.dot(hb, wup_ref[j], preferred_element_type=F32)
            up_ref[j] = up0
            y, _, _ = _conv_fwd(up0, carry_ref[j] * keep, cw_ref[j])
            carry_ref[j] = up0[tm - HALO:]
            upc_ref[j] = y.astype(BF16)
            return y

        acc = xv
        for k in range(half):
            a = (jax.nn.silu(conv_of(k)) * conv_of(k + half)).astype(BF16)
            act_ref[k] = a
            acc = acc + jnp.dot(a, wd_ref[k], preferred_element_type=F32)
        x2_ref[...] = acc

    const3 = lambda i: (0, 0, 0)
    return pl.pallas_call(
        body, name=name, grid=(t // tm,),
        in_specs=[pl.BlockSpec((tm, d), lambda i: (i, 0)),
                  _once((1, d), lambda i: (0, 0)),
                  _once((nj, d, f), const3),
                  _once((half, f, d), const3),
                  _once((nj, HALO, f), const3)],
        out_specs=[pl.BlockSpec((tm, d), lambda i: (i, 0)),
                   pl.BlockSpec((nj, tm, f), lambda i: (0, i, 0)),
                   pl.BlockSpec((nj, tm, f), lambda i: (0, i, 0)),
                   pl.BlockSpec((half, tm, f), lambda i: (0, i, 0)),
                   pl.BlockSpec((tm, d), lambda i: (i, 0))],
        out_shape=[jax.ShapeDtypeStruct((t, d), BF16), jax.ShapeDtypeStruct((nj, t, f), F32),
                   jax.ShapeDtypeStruct((nj, t, f), BF16),
                   jax.ShapeDtypeStruct((half, t, f), BF16), jax.ShapeDtypeStruct((t, d), F32)],
        scratch_shapes=[pltpu.VMEM((nj, HALO, f), F32)],
        compiler_params=_params(1),
    )(x1, g, wup, wd, cw)


def final_loss(x, g, target, *, name, tm):
    t, d = x.shape
    tm = min(tm, t)

    def body(x_ref, g_ref, tgt_ref, dx_ref, dg_ref, loss_ref):
        @pl.when(pl.program_id(0) == 0)
        def _():
            dg_ref[...] = jnp.zeros_like(dg_ref)
            loss_ref[...] = jnp.zeros_like(loss_ref)

        xv = x_ref[...]
        gv = g_ref[...]
        r, n = _rms_stats(xv)
        err = n * gv - tgt_ref[...]
        loss_ref[...] += 0.5 * jnp.sum(jnp.mean(err * err, axis=-1, keepdims=True))
        dy = err * (1.0 / d)
        dn = dy * gv
        dx_ref[...] = r * (dn - n * jnp.mean(dn * n, axis=-1, keepdims=True))
        dg_ref[0:1, :] += _colsum(dy * n)

    return pl.pallas_call(
        body, name=name, grid=(t // tm,),
        in_specs=[pl.BlockSpec((tm, d), lambda i: (i, 0)),
                  pl.BlockSpec((1, d), lambda i: (0, 0)),
                  pl.BlockSpec((tm, d), lambda i: (i, 0))],
        out_specs=[pl.BlockSpec((tm, d), lambda i: (i, 0)),
                   pl.BlockSpec((HALO, d), lambda i: (0, 0)),
                   pl.BlockSpec((8, 128), lambda i: (0, 0))],
        out_shape=[jax.ShapeDtypeStruct((t, d), F32), jax.ShapeDtypeStruct((HALO, d), F32),
                   jax.ShapeDtypeStruct((8, 128), F32)],
        compiler_params=_params(1),
    )(x, g, target)


def ffn_bwd(dx2, up0, upc, wd, cw, wup, x1, g, *, seq, name, tm):
    t, d = dx2.shape
    nj, _, f = up0.shape
    half = nj // 2
    tm = min(tm, seq)
    tiles_per_seq = seq // tm
    nt = t // tm

    def body(dx_ref, up_ref, upc_ref, wd_ref, cw_ref, wup_ref, x1_ref, g_ref,
             dup_ref, dcw_ref, dx1_ref, dg_ref, carry_ref):
        i = pl.program_id(0)
        tile = nt - 1 - i

        @pl.when(i == 0)
        def _():
            dcw_ref[...] = jnp.zeros_like(dcw_ref)
            dg_ref[...] = jnp.zeros_like(dg_ref)
            carry_ref[...] = jnp.zeros_like(carry_ref)

        keep_next = jnp.where(tile % tiles_per_seq == tiles_per_seq - 1, 0.0, 1.0)
        dx2v = dx_ref[...]
        dxb = dx2v.astype(BF16)
        dh = [jnp.zeros((tm, d), F32)]

        def through_conv(j, dup):
            next8 = carry_ref[j] * keep_next
            carry_ref[j] = dup[:HALO]
            dup0, u1, u2 = _conv_bwd(dup, next8, cw_ref[j])
            up0 = up_ref[j]
            dcw_ref[j, 0:1, :] += _colsum(u2 * up0)
            dcw_ref[j, 1:2, :] += _colsum(u1 * up0)
            dcw_ref[j, 2:3, :] += _colsum(dup * up0)
            dup0 = dup0.astype(BF16)
            dup_ref[j] = dup0
            dh[0] = dh[0] + lax.dot_general(dup0, wup_ref[j], NT_DIMS, preferred_element_type=F32)

        for k in range(half):
            gate = upc_ref[k].astype(F32)
            val = upc_ref[k + half].astype(F32)
            dact = lax.dot_general(dxb, wd_ref[k], NT_DIMS, preferred_element_type=F32)
            sg = jax.nn.sigmoid(gate)
            through_conv(k, dact * val * (sg * (1.0 + gate * (1.0 - sg))))
            through_conv(k + half, dact * (gate * sg))

        dx, dg = _rms_bwd(dh[0], x1_ref[...], g_ref[...])
        dx1_ref[...] = dx2v + dx
        dg_ref[0:1, :] += dg

    rev = lambda i: nt - 1 - i
    return pl.pallas_call(
        body, name=name, grid=(nt,),
        in_specs=[pl.BlockSpec((tm, d), lambda i: (rev(i), 0)),
                  pl.BlockSpec((nj, tm, f), lambda i: (0, rev(i), 0)),
                  pl.BlockSpec((nj, tm, f), lambda i: (0, rev(i), 0)),
                  _once((half, f, d), lambda i: (0, 0, 0)),
                  _once((nj, HALO, f), lambda i: (0, 0, 0)),
                  _once((nj, d, f), lambda i: (0, 0, 0)),
                  pl.BlockSpec((tm, d), lambda i: (rev(i), 0)),
                  _once((1, d), lambda i: (0, 0))],
        out_specs=[pl.BlockSpec((nj, tm, f), lambda i: (0, rev(i), 0)),
                   pl.BlockSpec((nj, HALO, f), lambda i: (0, 0, 0)),
                   pl.BlockSpec((tm, d), lambda i: (rev(i), 0)),
                   pl.BlockSpec((HALO, d), lambda i: (0, 0))],
        out_shape=[jax.ShapeDtypeStruct((nj, t, f), BF16), jax.ShapeDtypeStruct((nj, HALO, f), F32),
                   jax.ShapeDtypeStruct((t, d), F32), jax.ShapeDtypeStruct((HALO, d), F32)],
        scratch_shapes=[pltpu.VMEM((nj, HALO, f), F32)],
        compiler_params=_params(1),
    )(dx2, up0, upc, wd, cw, wup, x1, g)


def mixer_bwd(dx1, proj, wout, cw, lng, lnb, ws, wst, bias, *, seq, name, tm):
    t, d = dx1.shape
    tm = min(tm, seq)
    tiles_per_seq = seq // tm
    nt = t // tm
    gd = d // N_GROUPS

    def body(dx_ref, p_ref, cprev_ref, xiprev_ref, wout_ref, cw_ref, lng_ref, lnb_ref, ws_ref, wst_ref, bias_ref,
             dp_ref, dcw_ref, dln_ref, dws_ref, dbs_ref, mixed_scr, dvn_scr, carry_ref, dbs_acc):
        i = pl.program_id(0)
        tile = nt - 1 - i

        @pl.when(i == 0)
        def _():
            dcw_ref[...] = jnp.zeros_like(dcw_ref)
            dln_ref[...] = jnp.zeros_like(dln_ref)
            dws_ref[...] = jnp.zeros_like(dws_ref)
            dbs_acc[...] = jnp.zeros_like(dbs_acc)
            carry_ref[...] = jnp.zeros_like(carry_ref)

        keep_prev = jnp.where(tile % tiles_per_seq == 0, 0.0, 1.0)
        keep_next = jnp.where(tile % tiles_per_seq == tiles_per_seq - 1, 0.0, 1.0)
        cw = cw_ref[...]
        lng = lng_ref[...]
        f = _mixer_forward(p_ref, cprev_ref[...] * keep_prev, xiprev_ref[...], cw, lng, lnb_ref[...],
                           ws_ref, bias_ref, mixed_scr, d)
        dmerged = lax.dot_general(dx_ref[...].astype(BF16), wout_ref[...], NT_DIMS, preferred_element_type=F32)
        sa, sb = f["sa"], f["sb"]
        dp_ref[:, 5 * d:6 * d] = (dmerged * f["ya"] * (sa * (1.0 - sa))).astype(BF16)
        dp_ref[:, 6 * d:7 * d] = (dmerged * f["yb"] * (sb * (1.0 - sb))).astype(BF16)
        dya = dmerged * sa
        dyb = dmerged * sb
        dp_ref[:, 0:d] = (dya * f["conv"]).astype(BF16)
        dconv = dya * f["b"]
        dcw_ref[0:1, :] += _colsum(dconv * f["s2"])
        dcw_ref[1:2, :] += _colsum(dconv * f["s1"])
        dcw_ref[2:3, :] += _colsum(dconv * f["cx"])
        next8 = carry_ref[...] * keep_next
        carry_ref[...] = dconv[:HALO]
        dcx, _, _ = _conv_bwd(dconv, next8, cw)
        dp_ref[:, d:2 * d] = (dcx * f["xi"]).astype(BF16)
        dp_ref[:, 2 * d:3 * d] = (dcx * f["c"]).astype(BF16)
        dp_ref[:, 3 * d:4 * d] = (dyb * f["mixed"]).astype(BF16)
        dmixed = dyb * f["u"]
        dmb = dmixed.astype(BF16)
        vnb = f["vnb"]
        tril = f["tril"]
        triu = (lax.broadcasted_iota(jnp.int32, (CHUNK, CHUNK), 0)
                <= lax.broadcasted_iota(jnp.int32, (CHUNK, CHUNK), 1))
        dbs_tile = dmixed[0:CHUNK]
        for n in range(1, tm // CHUNK):
            dbs_tile = dbs_tile + dmixed[n * CHUNK:(n + 1) * CHUNK]
        dbs_acc[...] += dbs_tile
        for g in range(N_GROUPS):
            wmt = jnp.where(triu, wst_ref[g], 0.0).astype(BF16)
            cols = slice(g * gd, (g + 1) * gd)
            dw = jnp.zeros((CHUNK, CHUNK), F32)
            for n in range(tm // CHUNK):
                rows = slice(n * CHUNK, (n + 1) * CHUNK)
                dvn_scr[rows, cols] = jnp.dot(wmt, dmb[rows, cols], preferred_element_type=F32)
                dw = dw + lax.dot_general(dmb[rows, cols], vnb[rows, cols], NT_DIMS, preferred_element_type=F32)
            dws_ref[g] += jnp.where(tril, dw, 0.0)
        dvn = dvn_scr[...]
        vhat = f["vhat"]
        dln_ref[0:1, :] += _colsum(dvn * vhat)
        dln_ref[1:2, :] += _colsum(dvn)
        dvh = dvn * lng
        dv = f["rstd"] * (dvh - jnp.mean(dvh, axis=-1, keepdims=True)
                          - vhat * jnp.mean(dvh * vhat, axis=-1, keepdims=True))
        dp_ref[:, 4 * d:5 * d] = dv.astype(BF16)

        @pl.when(i == nt - 1)
        def _():
            for g in range(N_GROUPS):
                cols = slice(g * gd, (g + 1) * gd)
                s = jnp.sum(dbs_acc[:, cols], axis=1, keepdims=True)
                dbs_ref[:, cols] = jnp.broadcast_to(s, (CHUNK, gd))

    rev = lambda i: nt - 1 - i

    def halo(col):
        return pl.BlockSpec((HALO, d), lambda i: (jnp.maximum(rev(i) * (tm // HALO) - 1, 0), col))

    const2 = lambda i: (0, 0)
    const3 = lambda i: (0, 0, 0)
    row = lambda i: (rev(i), 0)
    return pl.pallas_call(
        body, name=name, grid=(nt,),
        in_specs=[pl.BlockSpec((tm, d), row),
                  pl.BlockSpec((tm, 7 * d), row),
                  halo(1), halo(2),
                  _once((d, d), const2),
                  _once((HALO, d), const2),
                  _once((1, d), const2),
                  _once((1, d), const2),
                  _once((N_GROUPS, CHUNK, CHUNK), const3),
                  _once((N_GROUPS, CHUNK, CHUNK), const3),
                  _once((CHUNK, d), const2)],
        out_specs=[pl.BlockSpec((tm, 7 * d), row),
                   pl.BlockSpec((HALO, d), const2),
                   pl.BlockSpec((HALO, d), const2),
                   pl.BlockSpec((N_GROUPS, CHUNK, CHUNK), const3),
                   pl.BlockSpec((CHUNK, d), const2)],
        out_shape=[jax.ShapeDtypeStruct((t, 7 * d), BF16),
                   jax.ShapeDtypeStruct((HALO, d), F32),
                   jax.ShapeDtypeStruct((HALO, d), F32),
                   jax.ShapeDtypeStruct((N_GROUPS, CHUNK, CHUNK), F32),
                   jax.ShapeDtypeStruct((CHUNK, d), F32)],
        scratch_shapes=[pltpu.VMEM((tm, d), F32), pltpu.VMEM((tm, d), F32),
                        pltpu.VMEM((HALO, d), F32), pltpu.VMEM((CHUNK, d), F32)],
        compiler_params=_params(1),
    )(dx1, proj, proj, proj, wout, cw, lng, lnb, ws, wst, bias)


def dgrad_rms(dy, w, x, g, res, *, name, tm):
    t, d = x.shape
    n = w.shape[2]
    w = w.reshape(w.shape[0] // 2, 2, d, n)
    nj = w.shape[0]
    tm = min(tm, t)

    def body(dy_ref, w_ref, x_ref, g_ref, res_ref, dx_ref, dg_ref, acc_ref):
        i, j = pl.program_id(0), pl.program_id(1)

        @pl.when((i == 0) & (j == 0))
        def _():
            dg_ref[...] = jnp.zeros_like(dg_ref)

        pair = jnp.concatenate([w_ref[0], w_ref[1]], axis=1)
        part = lax.dot_general(dy_ref[...], pair, NT_DIMS, preferred_element_type=F32)

        @pl.when(j == 0)
        def _():
            acc_ref[...] = part

        @pl.when(j > 0)
        def _():
            acc_ref[...] += part

        @pl.when(j == nj - 1)
        def _():
            dx, dg = _rms_bwd(acc_ref[...], x_ref[...], g_ref[...])
            dx_ref[...] = res_ref[...] + dx
            dg_ref[0:1, :] += dg

    return pl.pallas_call(
        body, name=name, grid=(t // tm, nj),
        in_specs=[pl.BlockSpec((tm, 2 * n), lambda i, j: (i, j)),
                  pl.BlockSpec((None, 2, d, n), lambda i, j: (j, 0, 0, 0)),
                  pl.BlockSpec((tm, d), lambda i, j: (i, 0)),
                  pl.BlockSpec((1, d), lambda i, j: (0, 0)),
                  pl.BlockSpec((tm, d), lambda i, j: (i, 0))],
        out_specs=[pl.BlockSpec((tm, d), lambda i, j: (i, 0)), pl.BlockSpec((HALO, d), lambda i, j: (0, 0))],
        out_shape=[jax.ShapeDtypeStruct((t, d), F32), jax.ShapeDtypeStruct((HALO, d), F32)],
        scratch_shapes=[pltpu.VMEM((tm, d), F32)],
        compiler_params=_params(2),
    )(dy, w, x, g, res)


def wgrad(a, b, *, nj, a_mode, b_mode, name, tm, split=1):
    def describe(arr, mode):
        if mode == "full":
            return arr.shape[0], arr.shape[1], pl.BlockSpec((tm_, arr.shape[1]), lambda j, s: (s, 0))
        if mode == "cols":
            c = arr.shape[1] // nj
            return arr.shape[0], c, pl.BlockSpec((tm_, c), lambda j, s: (s, j))
        return arr.shape[1], arr.shape[2], pl.BlockSpec((None, tm_, arr.shape[2]), lambda j, s: (j, s, 0))

    t = a.shape[0] if a_mode != "lead" else a.shape[1]
    tm_ = min(tm, t)
    _, k, a_spec = describe(a, a_mode)
    _, n, b_spec = describe(b, b_mode)

    ns = t // tm_
    nc = n // split

    def body(a_ref, b_ref, o_ref, acc_ref):
        s = pl.program_id(1)
        part = lax.dot_general(a_ref[...], b_ref[...], TN_DIMS, preferred_element_type=F32)

        def finish(total):
            for q in range(split):
                o_ref[q] = total[:, q * nc:(q + 1) * nc].astype(BF16)

        if ns == 1:
            finish(part)
            return

        @pl.when(s == 0)
        def _():
            acc_ref[...] = part

        @pl.when((s > 0) & (s < ns - 1))
        def _():
            acc_ref[...] += part

        @pl.when(s == ns - 1)
        def _():
            finish(acc_ref[...] + part)

    return pl.pallas_call(
        body, name=name, grid=(nj, ns),
        in_specs=[a_spec, b_spec],
        out_specs=pl.BlockSpec((split, k, nc), lambda j, s: (j, 0, 0)),
        out_shape=jax.ShapeDtypeStruct((nj * split, k, nc), BF16),
        scratch_shapes=[pltpu.VMEM((k, n), F32)],
        compiler_params=_params(2),
    )(a, b)


def _adamw_math(w, g, m, v):
    m = ADAM_B1 * m + (1.0 - ADAM_B1) * g
    v = ADAM_B2 * v + (1.0 - ADAM_B2) * (g * g)
    m_hat = m / (1.0 - ADAM_B1 ** ADAM_STEP)
    v_hat = v / (1.0 - ADAM_B2 ** ADAM_STEP)
    delta = -ADAM_LR * (m_hat / (jnp.sqrt(v_hat) + ADAM_EPS) + ADAM_WD * w)
    return delta, m, v


def _row_tile(rows, at_most):
    if rows <= at_most:
        return rows
    return max(k for k in range(16, at_most + 1, 16) if rows % k == 0)


def _sum_in_device_order(ref):
    total = ref[0]
    for s in range(1, N_DEV):
        total = total + ref[s]
    return total


def adamw_sharded(me, own0, recv0, own1, recv1, w, m, v, *, name, tr):
    _, r, c = w.shape
    tr = _row_tile(r, tr)
    ni = r // tr

    def body(me_ref, o0_ref, r0_ref, o1_ref, r1_ref, w_ref, m_ref, v_ref, g_ref, d_ref, nm_ref, nv_ref):
        def finish(own_ref, recv_ref):
            g = None
            for s in range(N_DEV):
                term = jnp.where(me_ref[0] == s, own_ref[...], recv_ref[s]).astype(F32)
                g = term if g is None else g + term
            delta, nm, nv = _adamw_math(w_ref[...], g, m_ref[...], v_ref[...])
            g_ref[...] = g
            d_ref[...] = delta
            nm_ref[...] = nm
            nv_ref[...] = nv

        @pl.when(pl.program_id(0) == 0)
        def _():
            finish(o0_ref, r0_ref)

        @pl.when(pl.program_id(0) == 1)
        def _():
            finish(o1_ref, r1_ref)

    row0 = lambda l, i: i * (1 - l) + (ni - 1) * l
    row1 = lambda l, i: i * l
    lay = pl.BlockSpec((None, tr, c), lambda l, i, me_ref: (l, i, 0))
    grid_spec = pltpu.PrefetchScalarGridSpec(
        num_scalar_prefetch=1, grid=(2, ni),
        in_specs=[pl.BlockSpec((None, tr, c), lambda l, i, me_ref: (me_ref[0], row0(l, i), 0)),
                  pl.BlockSpec((N_DEV, tr, c), lambda l, i, me_ref: (0, row0(l, i), 0)),
                  pl.BlockSpec((None, tr, c), lambda l, i, me_ref: (me_ref[0], row1(l, i), 0)),
                  pl.BlockSpec((N_DEV, tr, c), lambda l, i, me_ref: (0, row1(l, i), 0)),
                  lay, lay, lay],
        out_specs=[lay, lay, lay, lay])
    return pl.pallas_call(
        body, name=name, grid_spec=grid_spec,
        out_shape=[jax.ShapeDtypeStruct(w.shape, F32)] * 4,
        compiler_params=_params(2),
    )(me, own0, recv0, own1, recv1, w, m, v)


def adamw_small(g, w, m, v, *, name):
    def body(g_ref, w_ref, m_ref, v_ref, d_ref, nm_ref, nv_ref):
        delta, nm, nv = _adamw_math(w_ref[...], g_ref[...], m_ref[...], v_ref[...])
        d_ref[...] = delta
        nm_ref[...] = nm
        nv_ref[...] = nv

    return pl.pallas_call(
        body, name=name,
        out_shape=[jax.ShapeDtypeStruct(w.shape, F32)] * 3,
        compiler_params=pltpu.CompilerParams(vmem_limit_bytes=VMEM_LIMIT_BYTES),
    )(g, w, m, v)


def sum_devices(parts, *, name, tr):
    _, r, c = parts.shape
    tr = min(tr, r)

    def body(p_ref, o_ref):
        o_ref[...] = _sum_in_device_order(p_ref)

    return pl.pallas_call(
        body, name=name, grid=(r // tr,),
        in_specs=[pl.BlockSpec((N_DEV, tr, c), lambda i: (0, i, 0))],
        out_specs=pl.BlockSpec((tr, c), lambda i: (i, 0)),
        out_shape=jax.ShapeDtypeStruct((r, c), F32),
        compiler_params=_params(1),
    )(parts)


def _my_place():
    return lax.axis_index("x"), lax.axis_index("y"), lax.axis_index("c")


def all_gather(arrays, *, name):
    n = len(arrays)

    def body(*refs):
        ins, outs = refs[:n], refs[n:2 * n]
        send_sems, recv_sems, local_sems = refs[2 * n:]
        x, y, c = _my_place()
        me, sibling = (x, y, c), (x, y, 1 - c)
        chips = [(1 - x, y), (x, 1 - y), (1 - x, 1 - y)]
        waits = []
        for a in range(n):
            def slot(place, a=a):
                px, py, pc = place
                return outs[a].at[4 * px + 2 * py + pc]

            def copy(k, block, to, src=None, a=a, slot=slot):
                return pltpu.make_async_remote_copy(
                    src_ref=slot(block) if src is None else src, dst_ref=slot(block),
                    send_sem=send_sems.at[a, k], recv_sem=recv_sems.at[a, k],
                    device_id=to, device_id_type=MESH)

            mine = pltpu.make_async_copy(ins[a], slot(me), local_sems.at[a])
            mine.start()
            first = [copy(0, me, sibling, src=ins[a])]
            first += [copy(1 + j, me, (*chip, c), src=ins[a]) for j, chip in enumerate(chips)]
            for cp in first:
                cp.start()
            waits.append((copy, mine, first))
        sends = []
        for a in range(n):
            copy, mine, first = waits[a]
            passed = [copy(4 + j, (*chip, c), sibling) for j, chip in enumerate(chips)]
            for j, chip in enumerate(chips):
                copy(1 + j, (*chip, c), me).wait_recv()
                passed[j].start()
            sends.append(first + passed)
        for a in range(n):
            copy, mine, first = waits[a]
            copy(0, sibling, me).wait_recv()
            for j, chip in enumerate(chips):
                copy(4 + j, (*chip, 1 - c), me).wait_recv()
            for cp in sends[a]:
                cp.wait_send()
            mine.wait()

    return pl.pallas_call(
        body, name=name,
        in_specs=[ANY] * n, out_specs=[ANY] * n,
        out_shape=[jax.ShapeDtypeStruct((N_DEV,) + a.shape, a.dtype) for a in arrays],
        scratch_shapes=[pltpu.SemaphoreType.DMA((n, 7)), pltpu.SemaphoreType.DMA((n, 7)),
                        pltpu.SemaphoreType.DMA((n,))],
        compiler_params=pltpu.CompilerParams(has_side_effects=True),
    )(*arrays)


def _peer_place(r, x, y, c):
    fx, fy, fc = (r >> 2) & 1, (r >> 1) & 1, r & 1
    return (1 - x if fx else x, 1 - y if fy else y, 1 - c if fc else c)


def own_slot(me, w, layer, dtype, *, name, tr):
    _, r, c = w.shape
    tr = _row_tile(r, tr)

    def body(me_ref, w_ref, o_ref):
        o_ref[...] = w_ref[...].astype(dtype)

    grid_spec = pltpu.PrefetchScalarGridSpec(
        num_scalar_prefetch=1, grid=(r // tr,),
        in_specs=[pl.BlockSpec((None, tr, c), lambda i, me_ref: (layer, i, 0))],
        out_specs=pl.BlockSpec((None, tr, c), lambda i, me_ref: (me_ref[0], i, 0)))
    return pl.pallas_call(
        body, name=name, grid_spec=grid_spec,
        out_shape=jax.ShapeDtypeStruct((N_DEV, r, c), dtype),
        compiler_params=_params(1),
    )(me, w)


def _split_copy(r, src, land, send_sem, recv_sem, scatter, arriving):
    x, y, c = _my_place()
    px, py, pc = _peer_place(r, x, y, c)
    theirs, mine = 4 * px + 2 * py + pc, 4 * x + 2 * y + c
    return pltpu.make_async_remote_copy(
        src_ref=src.at[theirs] if scatter else land.at[mine], dst_ref=land.at[theirs if arriving else mine],
        send_sem=send_sem.at[r - 1], recv_sem=recv_sem.at[r - 1],
        device_id=(px, py, pc), device_id_type=MESH)


def start_copies(srcs, lands, *, scatter, name):
    n = len(lands)
    bufs = (list(srcs) if scatter else []) + list(lands)
    nb = len(bufs)

    def body(*refs):
        src = refs[:n] if scatter else [None] * n
        land = refs[nb - n:nb]
        send_sems, recv_sems = refs[nb:nb + n], refs[nb + n:nb + 2 * n]
        token = refs[2 * nb + 2 * n]
        for a in range(n):
            for r in range(1, N_DEV):
                _split_copy(r, src[a], land[a], send_sems[a], recv_sems[a], scatter, False).start()
        token[...] = jnp.zeros_like(token)

    outs = pl.pallas_call(
        body, name=name,
        in_specs=[HBM_SPEC] * nb,
        out_specs=[SEM_SPEC] * (2 * n) + [HBM_SPEC] * nb + [pl.BlockSpec(memory_space=pltpu.VMEM)],
        out_shape=([pltpu.SemaphoreType.DMA((N_DEV - 1,))] * (2 * n)
                   + [pltpu.HBM(a.shape, a.dtype) for a in bufs]
                   + [jax.ShapeDtypeStruct((8, 128), F32)]),
        input_output_aliases={i: 2 * n + i for i in range(nb)},
        compiler_params=pltpu.CompilerParams(has_side_effects=DATAFLOW),
    )(*[pltpu.with_memory_space_constraint(a, pltpu.HBM) for a in bufs])
    thru = list(outs[2 * n:2 * n + nb])
    return dict(send=outs[:n], recv=outs[n:2 * n], src=thru[:n] if scatter else None, land=thru[nb - n:],
                token=outs[2 * n + nb], scatter=scatter)


def finish_copies(started, which, after, *, name):
    n = len(which)
    scatter = started["scatter"]
    bufs = ([started["src"][i] for i in which] if scatter else []) + [started["land"][i] for i in which]
    nb = len(bufs)

    def body(*refs):
        src = refs[:n] if scatter else [None] * n
        land = refs[nb - n:nb]
        send_sems, recv_sems = refs[nb:nb + n], refs[nb + n:nb + 2 * n]
        for a in range(n):
            for r in range(1, N_DEV):
                cp = _split_copy(r, src[a], land[a], send_sems[a], recv_sems[a], scatter, True)
                cp.wait_send()
                cp.wait_recv()

    outs = pl.pallas_call(
        body, name=name,
        in_specs=[HBM_SPEC] * nb + [SEM_SPEC] * (2 * n) + [ANY],
        out_specs=[HBM_SPEC] * nb,
        out_shape=[pltpu.HBM(a.shape, a.dtype) for a in bufs],
        input_output_aliases={i: i for i in range(nb)},
        compiler_params=pltpu.CompilerParams(has_side_effects=DATAFLOW),
    )(*bufs, *[started["send"][i] for i in which], *[started["recv"][i] for i in which], after)
    return (list(outs[:n]) if scatter else None), list(outs[nb - n:])


def _pad_rows(a, rows):
    pad = [(0, 0)] * a.ndim
    pad[-2] = (0, rows - a.shape[-2])
    return jnp.pad(a, pad)


def kernel(x, mix_norm_g, w_in, conv_a_w, ln_v_g, ln_v_b, w_s, b_s, w_out, ffn_norm_g, w_up, conv_ffn_w, w_down, final_norm_g, loss_target, m_mix_norm_g, m_w_in, m_conv_a_w, m_ln_v_g, m_ln_v_b, m_w_s, m_b_s, m_w_out, m_ffn_norm_g, m_w_up, m_conv_ffn_w, m_w_down, m_final_norm_g, v_mix_norm_g, v_w_in, v_conv_a_w, v_ln_v_g, v_ln_v_b, v_w_s, v_b_s, v_w_out, v_ffn_norm_g, v_w_up, v_conv_ffn_w, v_w_down, v_final_norm_g):
    nb, seq, d = x.shape
    t = nb * seq
    depth = w_in.shape[0]
    f = w_up.shape[2]
    me = 4 * lax.axis_index("x") + 2 * lax.axis_index("y") + lax.axis_index("c")
    xt = x.reshape(t, d)
    tgt = loss_target.reshape(t, d)

    conv_pack = jnp.concatenate([_pad_rows(conv_a_w, HALO), _pad_rows(conv_ffn_w, HALO)], axis=-1)
    me_arr = me.astype(jnp.int32).reshape(1)
    zones, slot_of = [], {}
    for l in range(depth):
        for key, w in (("win", w_in), ("conv", None), ("wout", w_out), ("wup", w_up), ("wd", w_down)):
            if key == "conv":
                if l == 0:
                    slot_of["conv"] = len(zones)
                    packed = conv_pack.reshape(1, depth * HALO, conv_pack.shape[-1])
                    zones.append(own_slot(me_arr, packed, 0, F32, name="own_slot_conv", tr=256))
                continue
            slot_of[key, l] = len(zones)
            zones.append(own_slot(me_arr, w, l, BF16, name=f"own_slot_{key}_{l}", tr=256))
    gathering = start_copies(None, zones, scatter=False, name="gather_start")

    def gathered(keys, after, name):
        return finish_copies(gathering, [slot_of[k] for k in keys], after, name=name)[1]

    saved, layers = [], []
    cur = xt
    after = gathering["token"]
    for l in range(depth):
        p = dict(mix_g=mix_norm_g[l][None], ffn_g=ffn_norm_g[l][None], lng=ln_v_g[l][None], lnb=ln_v_b[l][None],
                 ws=w_s[l], wst=jnp.swapaxes(w_s[l], 1, 2),
                 bias=jnp.repeat(b_s[l].T, d // N_GROUPS, axis=1))
        if l == 0:
            p["win"], wout_g, conv_g = gathered([("win", l), ("wout", l), "conv"], after, f"wait_w_mixer_{l}")
            conv_g = conv_g.reshape(N_DEV, depth, HALO, -1)
            ca = conv_g.shape[-1] - f
        else:
            p["win"], wout_g = gathered([("win", l), ("wout", l)], after, f"wait_w_mixer_{l}")
        p["wout"] = wout_g.reshape(d, d)
        p["cw_a"] = jnp.transpose(conv_g[:, l, :, :ca], (1, 0, 2)).reshape(HALO, d)
        p["cw_f"] = conv_g[:, l, :, ca:]
        h, proj, merged, x1 = mixer_fwd(cur, p["mix_g"], p["win"], p["wout"], p["cw_a"], p["lng"], p["lnb"], p["ws"],
                                        p["bias"], seq=seq, name=f"mixer_fwd_{l}", tm=128)
        p["wup"], wd_g = gathered([("wup", l), ("wd", l)], merged, f"wait_w_ffn_{l}")
        p["wd"] = wd_g.reshape(N_DEV // 2, 2 * wd_g.shape[1], d)
        h2, up0, upc, act, x2 = ffn_fwd(x1, p["ffn_g"], p["wup"], p["wd"], p["cw_f"],
                                        seq=seq, name=f"ffn_fwd_{l}", tm=256)
        saved.append(dict(x0=cur, h=h, proj=proj, merged=merged, x1=x1, h2=h2, up0=up0, upc=upc, act=act))
        layers.append(p)
        cur, after = x2, act
    dx, d_final_g, loss_tile = final_loss(cur, final_norm_g[None], tgt, name="final_loss", tm=512)

    def exchange(parts, name):
        return start_copies(parts, [lax.empty(a.shape, a.dtype) for a in parts], scatter=True, name=name)

    def tied(g, started):
        return g + started["token"][0:1, 0:1]

    part = [None] * depth
    for l in reversed(range(depth)):
        p, s = layers[l], saved[l]
        dup0, dcw_f, dx1, d_ffn_g = ffn_bwd(dx, s["up0"], s["upc"], p["wd"], p["cw_f"], p["wup"], s["x1"], p["ffn_g"],
                                            seq=seq, name=f"ffn_bwd_{l}", tm=256)
        g_wd = wgrad(s["act"], dx, nj=N_DEV // 2, a_mode="lead", b_mode="full", name=f"wgrad_down_{l}", tm=2048)
        g_wup = wgrad(s["h2"], dup0, nj=N_DEV, a_mode="full", b_mode="lead", name=f"wgrad_up_{l}", tm=2048)
        ffn_ex = exchange([g_wd.reshape(N_DEV, g_wd.shape[1] // 2, d), g_wup], f"exchange_ffn_{l}")
        dproj, dcw_a, dln, dws, dbs = mixer_bwd(dx1, s["proj"], p["wout"], tied(p["cw_a"], ffn_ex), p["lng"], p["lnb"],
                                                 p["ws"], p["wst"], p["bias"], seq=seq, name=f"mixer_bwd_{l}", tm=128)
        g_wout = wgrad(s["merged"], dx1, nj=1, a_mode="full", b_mode="full", name=f"wgrad_out_{l}", tm=2048)
        g_win = wgrad(s["h"], dproj, nj=N_DEV // 2, a_mode="full", b_mode="cols", name=f"wgrad_in_{l}", tm=2048, split=2)
        mix_ex = exchange([g_wout.reshape(N_DEV, d // N_DEV, d), g_win], f"exchange_mix_{l}")
        dx, d_mix_g = dgrad_rms(dproj, p["win"], s["x0"], tied(p["mix_g"], mix_ex), dx1, name=f"dgrad_in_{l}", tm=1024)
        part[l] = dict(
            ffn_ex=ffn_ex, mix_ex=mix_ex,
            small=jnp.concatenate([
                dws.reshape(N_GROUPS * CHUNK * CHUNK // d, d),
                d_mix_g[0:1], d_ffn_g[0:1], dln[0:2], dcw_a[0:3],
                dbs[:, ::d // N_GROUPS].T.reshape(1, d)], axis=0),
            cw_f=dcw_f.reshape(N_DEV * HALO, f))
    grad_x = dx.reshape(nb, seq, d)

    loss_row = jnp.zeros((1, d), F32).at[0, 0].set(loss_tile[0, 0])
    small = jnp.concatenate([part[l]["small"] for l in range(depth)] + [d_final_g[0:1], loss_row], axis=0)
    small = _pad_rows(small, -(-small.shape[0] // 8) * 8)
    cwf = jnp.concatenate([part[l]["cw_f"] for l in range(depth)], axis=0)
    small_all, cwf_all = all_gather([small, cwf], name="gather_small_grads")
    small_sum = sum_devices(small_all, name="sum_small", tr=512)
    cwf_sum = sum_devices(cwf_all, name="sum_conv_ffn", tr=512)


    rows_ws = N_GROUPS * CHUNK * CHUNK // d
    per_layer = rows_ws + 8
    def small_of(l, a, b):
        return small_sum[l * per_layer + rows_ws + a:l * per_layer + rows_ws + b]
    g_ws = jnp.stack([small_sum[l * per_layer:l * per_layer + rows_ws].reshape(N_GROUPS, CHUNK, CHUNK)
                      for l in range(depth)])
    g_mix = jnp.concatenate([small_of(l, 0, 1) for l in range(depth)])
    g_ffn = jnp.concatenate([small_of(l, 1, 2) for l in range(depth)])
    g_lng = jnp.concatenate([small_of(l, 2, 3) for l in range(depth)])
    g_lnb = jnp.concatenate([small_of(l, 3, 4) for l in range(depth)])
    g_cwa_full = jnp.stack([small_of(l, 4, 7) for l in range(depth)])
    g_cwa = lax.dynamic_slice_in_dim(g_cwa_full, me * ca, ca, axis=2)
    g_bs = jnp.stack([small_of(l, 7, 8).reshape(N_GROUPS, CHUNK) for l in range(depth)])
    g_final = small_sum[depth * per_layer]
    loss = small_sum[depth * per_layer + 1, 0]
    cwf_sum = cwf_sum.reshape(depth, N_DEV, HALO, f)
    g_cwf = lax.dynamic_index_in_dim(cwf_sum, me, axis=1, keepdims=False)[:, :3]

    own, recv = {}, {}

    def arrived(l, ex, keys, after):
        srcs, lands = finish_copies(part[l][ex], [0, 1], after, name=f"wait_{ex}_{l}")
        for k, key in enumerate(keys):
            own[key, l], recv[key, l] = srcs[k], lands[k]
        return lands[1]

    def big(key, w, m, v, name):
        return adamw_sharded(me_arr, own[key, 0], recv[key, 0], own[key, 1], recv[key, 1], w, m, v, name=name, tr=256)

    after = grad_x
    for l in reversed(range(depth)):
        after = arrived(l, "ffn_ex", ("wd", "wup"), after)
        if l > 0:
            after = arrived(l, "mix_ex", ("wout", "win"), after)
    u_wd = big("wd", w_down, m_w_down, v_w_down, "adamw_w_down")
    u_wup = big("wup", w_up, m_w_up, v_w_up, "adamw_w_up")
    arrived(0, "mix_ex", ("wout", "win"), u_wup[1])
    u_wout = big("wout", w_out, m_w_out, v_w_out, "adamw_w_out")
    u_win = big("win", w_in, m_w_in, v_w_in, "adamw_w_in")

    def small_update(g, w, m, v, name):
        shape = w.shape
        two_d = (-1, shape[-1]) if w.ndim > 1 else (1, shape[0])
        out = adamw_small(g.reshape(two_d), w.reshape(two_d), m.reshape(two_d), v.reshape(two_d), name=name)
        return (g.reshape(shape),) + tuple(o.reshape(shape) for o in out)

    u_mix = small_update(g_mix, mix_norm_g, m_mix_norm_g, v_mix_norm_g, "adamw_mix_norm_g")
    u_cwa = small_update(g_cwa, conv_a_w, m_conv_a_w, v_conv_a_w, "adamw_conv_a_w")
    u_lng = small_update(g_lng, ln_v_g, m_ln_v_g, v_ln_v_g, "adamw_ln_v_g")
    u_lnb = small_update(g_lnb, ln_v_b, m_ln_v_b, v_ln_v_b, "adamw_ln_v_b")
    u_ws = small_update(g_ws, w_s, m_w_s, v_w_s, "adamw_w_s")
    u_bs = small_update(g_bs, b_s, m_b_s, v_b_s, "adamw_b_s")
    u_ffn = small_update(g_ffn, ffn_norm_g, m_ffn_norm_g, v_ffn_norm_g, "adamw_ffn_norm_g")
    u_cwf = small_update(g_cwf, conv_ffn_w, m_conv_ffn_w, v_conv_ffn_w, "adamw_conv_ffn_w")
    u_final = small_update(g_final, final_norm_g, m_final_norm_g, v_final_norm_g, "adamw_final_norm_g")

    ordered = [u_mix, u_win, u_cwa, u_lng, u_lnb, u_ws, u_bs, u_wout, u_ffn, u_wup, u_cwf, u_wd, u_final]
    return (loss, grad_x, *[u[0] for u in ordered], *[u[1] for u in ordered],
            *[u[2] for u in ordered], *[u[3] for u in ordered])
```

```python
import functools

import jax
import jax.numpy as jnp
from jax import lax
from jax.experimental import pallas as pl
from jax.experimental.pallas import tpu as pltpu

EPS = 1e-6
CHUNK = 128
N_GROUPS = 8
N_DEV = 8
HALO = 8
ADAM_LR = 0.001
ADAM_B1 = 0.9
ADAM_B2 = 0.999
ADAM_EPS = 1e-08
ADAM_WD = 0.01
ADAM_STEP = 10
VMEM_LIMIT_BYTES = 56 * 1024 * 1024
F32 = jnp.float32
BF16 = jnp.bfloat16
MESH = pl.DeviceIdType.MESH
ANY = pl.BlockSpec(memory_space=pl.ANY)
HBM_SPEC = pl.BlockSpec(memory_space=pltpu.HBM)
SEM_SPEC = pl.BlockSpec(memory_space=pltpu.SEMAPHORE)
DATAFLOW = pltpu.SideEffectType.DATAFLOW_SIDE_EFFECTING
NT_DIMS = (((1,), (1,)), ((), ()))
TN_DIMS = (((0,), (0,)), ((), ()))


def _params(n_grid_axes):
    return pltpu.CompilerParams(dimension_semantics=("arbitrary",) * n_grid_axes,
                                vmem_limit_bytes=VMEM_LIMIT_BYTES)


def _shift_down(cur, prev8, k):
    rolled = pltpu.roll(cur, k, 0)
    prolled = pltpu.roll(prev8, k, 0)
    row = lax.broadcasted_iota(jnp.int32, prev8.shape, 0)
    head = jnp.where(row < k, prolled, rolled[:HALO])
    return jnp.concatenate([head, rolled[HALO:]], axis=0)


def _shift_up(cur, next8, k):
    tm = cur.shape[0]
    rolled = pltpu.roll(cur, tm - k, 0)
    nrolled = pltpu.roll(next8, HALO - k, 0)
    row = lax.broadcasted_iota(jnp.int32, next8.shape, 0)
    tail = jnp.where(row >= HALO - k, nrolled, rolled[tm - HALO:])
    return jnp.concatenate([rolled[:tm - HALO], tail], axis=0)


def _conv_fwd(cur, prev8, cw):
    s1 = _shift_down(cur, prev8, 1)
    s2 = _shift_down(cur, prev8, 2)
    y = s2 * cw[0:1, :] + s1 * cw[1:2, :] + cur * cw[2:3, :]
    return y, s1, s2


def _conv_bwd(d, next8, cw):
    u1 = _shift_up(d, next8, 1)
    u2 = _shift_up(d, next8, 2)
    return d * cw[2:3, :] + u1 * cw[1:2, :] + u2 * cw[0:1, :], u1, u2


def _colsum(a):
    return jnp.sum(a, axis=0, keepdims=True)


def _rms_stats(xv):
    r = lax.rsqrt(jnp.mean(xv * xv, axis=-1, keepdims=True) + EPS)
    return r, xv * r


def _rms_bwd(dh, xv, g):
    r, n = _rms_stats(xv)
    dn = dh * g
    dx = r * (dn - n * jnp.mean(dn * n, axis=-1, keepdims=True))
    return dx, _colsum(dh * n)


def _mixer_forward(p_ref, cprev, xiprev, cw, lng, lnb, ws_ref, bias_ref, mixed_scr, d):
    tm = p_ref.shape[0]
    b = p_ref[:, 0:d]
    c = p_ref[:, d:2 * d]
    xi = p_ref[:, 2 * d:3 * d]
    u = p_ref[:, 3 * d:4 * d]
    v = p_ref[:, 4 * d:5 * d]
    sa = jax.nn.sigmoid(p_ref[:, 5 * d:6 * d])
    sb = jax.nn.sigmoid(p_ref[:, 6 * d:7 * d])
    cx = c * xi
    conv, s1, s2 = _conv_fwd(cx, cprev * xiprev, cw)
    ya = b * conv
    mu = jnp.mean(v, axis=-1, keepdims=True)
    xc = v - mu
    rstd = lax.rsqrt(jnp.mean(xc * xc, axis=-1, keepdims=True) + EPS)
    vhat = xc * rstd
    vnb = (vhat * lng + lnb).astype(BF16)
    tril = (lax.broadcasted_iota(jnp.int32, (CHUNK, CHUNK), 0)
            >= lax.broadcasted_iota(jnp.int32, (CHUNK, CHUNK), 1))
    gd = d // N_GROUPS
    for g in range(N_GROUPS):
        wm = jnp.where(tril, ws_ref[g], 0.0).astype(BF16)
        cols = slice(g * gd, (g + 1) * gd)
        for n in range(tm // CHUNK):
            rows = slice(n * CHUNK, (n + 1) * CHUNK)
            mixed_scr[rows, cols] = (jnp.dot(wm, vnb[rows, cols], preferred_element_type=F32)
                                     + bias_ref[:, cols])
    mixed = mixed_scr[...]
    yb = u * mixed
    merged = sa * ya + sb * yb
    return dict(b=b, c=c, xi=xi, u=u, sa=sa, sb=sb, cx=cx, s1=s1, s2=s2, conv=conv, ya=ya,
                rstd=rstd, vhat=vhat, vnb=vnb, mixed=mixed, yb=yb, merged=merged, tril=tril)


def _once(block_shape, index_map):
    return pl.BlockSpec(block_shape, index_map, pipeline_mode=pl.Buffered(1))


def mixer_fwd(x, g, win, wout, cw, lng, lnb, ws, bias, *, seq, name, tm):
    t, d = x.shape
    nj, _, n = win.shape
    tm = min(tm, seq)
    tiles_per_seq = seq // tm

    def body(x_ref, g_ref, win_ref, wout_ref, cw_ref, lng_ref, lnb_ref, ws_ref, bias_ref,
             h_ref, p_ref, merged_ref, x1_ref, mixed_scr, carry_ref):
        @pl.when(pl.program_id(0) == 0)
        def _():
            carry_ref[...] = jnp.zeros_like(carry_ref)

        keep = jnp.where(pl.program_id(0) % tiles_per_seq == 0, 0.0, 1.0)
        xv = x_ref[...]
        _, nrm = _rms_stats(xv)
        hb = (nrm * g_ref[...]).astype(BF16)
        h_ref[...] = hb
        for j in range(0, nj, 2):
            pair = jnp.concatenate([win_ref[j], win_ref[j + 1]], axis=1)
            p_ref[:, j * n:(j + 2) * n] = jnp.dot(hb, pair, preferred_element_type=F32)
        f = _mixer_forward(p_ref, carry_ref[...] * keep, 1.0, cw_ref[...], lng_ref[...],
                           lnb_ref[...], ws_ref, bias_ref, mixed_scr, d)
        carry_ref[...] = f["cx"][tm - HALO:]
        mb = f["merged"].astype(BF16)
        merged_ref[...] = mb
        x1_ref[...] = xv + jnp.dot(mb, wout_ref[...], preferred_element_type=F32)

    const2 = lambda i: (0, 0)
    const3 = lambda i: (0, 0, 0)
    row = lambda i: (i, 0)
    return pl.pallas_call(
        body, name=name, grid=(t // tm,),
        in_specs=[pl.BlockSpec((tm, d), row),
                  _once((1, d), const2),
                  _once((nj, d, n), const3),
                  _once((d, d), const2),
                  _once((HALO, d), const2),
                  _once((1, d), const2),
                  _once((1, d), const2),
                  _once((N_GROUPS, CHUNK, CHUNK), const3),
                  _once((CHUNK, d), const2)],
        out_specs=[pl.BlockSpec((tm, d), row), pl.BlockSpec((tm, nj * n), row),
                   pl.BlockSpec((tm, d), row), pl.BlockSpec((tm, d), row)],
        out_shape=[jax.ShapeDtypeStruct((t, d), BF16), jax.ShapeDtypeStruct((t, nj * n), F32),
                   jax.ShapeDtypeStruct((t, d), BF16), jax.ShapeDtypeStruct((t, d), F32)],
        scratch_shapes=[pltpu.VMEM((tm, d), F32), pltpu.VMEM((HALO, d), F32)],
        compiler_params=_params(1),
    )(x, g, win, wout, cw, lng, lnb, ws, bias)


def ffn_fwd(x1, g, wup, wd, cw, *, seq, name, tm):
    t, d = x1.shape
    nj, f, _ = wup.shape
    half = nj // 2
    tm = min(tm, seq)
    tiles_per_seq = seq // tm

    def body(x1_ref, g_ref, wup_ref, wd_ref, cw_ref, h2_ref, up_ref, upc_ref, act_ref, x2_ref, carry_ref):
        @pl.when(pl.program_id(0) == 0)
        def _():
            carry_ref[...] = jnp.zeros_like(carry_ref)

        keep = jnp.where(pl.program_id(0) % tiles_per_seq == 0, 0.0, 1.0)
        xv = x1_ref[...]
        _, nrm = _rms_stats(xv)
        hb = (nrm * g_ref[...]).astype(BF16)
        h2_ref[...] = hb

        def conv_of(j):
            up0 = lax.dot_general(hb, wup_ref[j], NT_DIMS, preferred_element_type=F32)
            up_ref[j] = up0
            y, _, _ = _conv_fwd(up0, carry_ref[j] * keep, cw_ref[j])
            carry_ref[j] = up0[tm - HALO:]
            upc_ref[j] = y.astype(BF16)
            return y

        acc = xv
        for k in range(half):
            a = (jax.nn.silu(conv_of(k)) * conv_of(k + half)).astype(BF16)
            act_ref[k] = a
            acc = acc + jnp.dot(a, wd_ref[k], preferred_element_type=F32)
        x2_ref[...] = acc

    const3 = lambda i: (0, 0, 0)
    return pl.pallas_call(
        body, name=name, grid=(t // tm,),
        in_specs=[pl.BlockSpec((tm, d), lambda i: (i, 0)),
                  _once((1, d), lambda i: (0, 0)),
                  _once((nj, f, d), const3),
                  _once((half, f, d), const3),
                  _once((nj, HALO, f), const3)],
        out_specs=[pl.BlockSpec((tm, d), lambda i: (i, 0)),
                   pl.BlockSpec((nj, tm, f), lambda i: (0, i, 0)),
                   pl.BlockSpec((nj, tm, f), lambda i: (0, i, 0)),
                   pl.BlockSpec((half, tm, f), lambda i: (0, i, 0)),
                   pl.BlockSpec((tm, d), lambda i: (i, 0))],
        out_shape=[jax.ShapeDtypeStruct((t, d), BF16), jax.ShapeDtypeStruct((nj, t, f), F32),
                   jax.ShapeDtypeStruct((nj, t, f), BF16),
                   jax.ShapeDtypeStruct((half, t, f), BF16), jax.ShapeDtypeStruct((t, d), F32)],
        scratch_shapes=[pltpu.VMEM((nj, HALO, f), F32)],
        compiler_params=_params(1),
    )(x1, g, wup, wd, cw)


def final_loss(x, g, target, *, name, tm):
    t, d = x.shape
    tm = min(tm, t)

    def body(x_ref, g_ref, tgt_ref, dx_ref, dg_ref, loss_ref):
        @pl.when(pl.program_id(0) == 0)
        def _():
            dg_ref[...] = jnp.zeros_like(dg_ref)
            loss_ref[...] = jnp.zeros_like(loss_ref)

        xv = x_ref[...]
        gv = g_ref[...]
        r, n = _rms_stats(xv)
        err = n * gv - tgt_ref[...]
        loss_ref[...] += 0.5 * jnp.sum(jnp.mean(err * err, axis=-1, keepdims=True))
        dy = err * (1.0 / d)
        dn = dy * gv
        dx_ref[...] = r * (dn - n * jnp.mean(dn * n, axis=-1, keepdims=True))
        dg_ref[0:1, :] += _colsum(dy * n)

    return pl.pallas_call(
        body, name=name, grid=(t // tm,),
        in_specs=[pl.BlockSpec((tm, d), lambda i: (i, 0)),
                  pl.BlockSpec((1, d), lambda i: (0, 0)),
                  pl.BlockSpec((tm, d), lambda i: (i, 0))],
        out_specs=[pl.BlockSpec((tm, d), lambda i: (i, 0)),
                   pl.BlockSpec((HALO, d), lambda i: (0, 0)),
                   pl.BlockSpec((8, 128), lambda i: (0, 0))],
        out_shape=[jax.ShapeDtypeStruct((t, d), F32), jax.ShapeDtypeStruct((HALO, d), F32),
                   jax.ShapeDtypeStruct((8, 128), F32)],
        compiler_params=_params(1),
    )(x, g, target)


def ffn_bwd(dx2, up0, upc, wd, cw, wup, x1, g, *, seq, name, tm):
    t, d = dx2.shape
    nj, _, f = up0.shape
    half = nj // 2
    tm = min(tm, seq)
    tiles_per_seq = seq // tm
    nt = t // tm

    def body(dx_ref, up_ref, upc_ref, wd_ref, cw_ref, wup_ref, x1_ref, g_ref,
             dup_ref, dcw_ref, dx1_ref, dg_ref, carry_ref):
        i = pl.program_id(0)
        tile = nt - 1 - i

        @pl.when(i == 0)
        def _():
            dcw_ref[...] = jnp.zeros_like(dcw_ref)
            dg_ref[...] = jnp.zeros_like(dg_ref)
            carry_ref[...] = jnp.zeros_like(carry_ref)

        keep_next = jnp.where(tile % tiles_per_seq == tiles_per_seq - 1, 0.0, 1.0)
        dx2v = dx_ref[...]
        dxb = dx2v.astype(BF16)
        dh = [jnp.zeros((tm, d), F32)]

        def through_conv(j, dup):
            next8 = carry_ref[j] * keep_next
            carry_ref[j] = dup[:HALO]
            dup0, u1, u2 = _conv_bwd(dup, next8, cw_ref[j])
            up0 = up_ref[j]
            dcw_ref[j, 0:1, :] += _colsum(u2 * up0)
            dcw_ref[j, 1:2, :] += _colsum(u1 * up0)
            dcw_ref[j, 2:3, :] += _colsum(dup * up0)
            dup0 = dup0.astype(BF16)
            dup_ref[j] = dup0
            dh[0] = dh[0] + jnp.dot(dup0, wup_ref[j], preferred_element_type=F32)

        for k in range(half):
            gate = upc_ref[k].astype(F32)
            val = upc_ref[k + half].astype(F32)
            dact = lax.dot_general(dxb, wd_ref[k], NT_DIMS, preferred_element_type=F32)
            sg = jax.nn.sigmoid(gate)
            through_conv(k, dact * val * (sg * (1.0 + gate * (1.0 - sg))))
            through_conv(k + half, dact * (gate * sg))

        dx, dg = _rms_bwd(dh[0], x1_ref[...], g_ref[...])
        dx1_ref[...] = dx2v + dx
        dg_ref[0:1, :] += dg

    rev = lambda i: nt - 1 - i
    return pl.pallas_call(
        body, name=name, grid=(nt,),
        in_specs=[pl.BlockSpec((tm, d), lambda i: (rev(i), 0)),
                  pl.BlockSpec((nj, tm, f), lambda i: (0, rev(i), 0)),
                  pl.BlockSpec((nj, tm, f), lambda i: (0, rev(i), 0)),
                  _once((half, f, d), lambda i: (0, 0, 0)),
                  _once((nj, HALO, f), lambda i: (0, 0, 0)),
                  _once((nj, f, d), lambda i: (0, 0, 0)),
                  pl.BlockSpec((tm, d), lambda i: (rev(i), 0)),
                  _once((1, d), lambda i: (0, 0))],
        out_specs=[pl.BlockSpec((nj, tm, f), lambda i: (0, rev(i), 0)),
                   pl.BlockSpec((nj, HALO, f), lambda i: (0, 0, 0)),
                   pl.BlockSpec((tm, d), lambda i: (rev(i), 0)),
                   pl.BlockSpec((HALO, d), lambda i: (0, 0))],
        out_shape=[jax.ShapeDtypeStruct((nj, t, f), BF16), jax.ShapeDtypeStruct((nj, HALO, f), F32),
                   jax.ShapeDtypeStruct((t, d), F32), jax.ShapeDtypeStruct((HALO, d), F32)],
        scratch_shapes=[pltpu.VMEM((nj, HALO, f), F32)],
        compiler_params=_params(1),
    )(dx2, up0, upc, wd, cw, wup, x1, g)


def mixer_bwd(dx1, proj, wout, cw, lng, lnb, ws, wst, bias, *, seq, name, tm):
    t, d = dx1.shape
    tm = min(tm, seq)
    tiles_per_seq = seq // tm
    nt = t // tm
    gd = d // N_GROUPS

    def body(dx_ref, p_ref, cprev_ref, xiprev_ref, wout_ref, cw_ref, lng_ref, lnb_ref, ws_ref, wst_ref, bias_ref,
             dp_ref, dcw_ref, dln_ref, dws_ref, dbs_ref, mixed_scr, dvn_scr, carry_ref, dbs_acc):
        i = pl.program_id(0)
        tile = nt - 1 - i

        @pl.when(i == 0)
        def _():
            dcw_ref[...] = jnp.zeros_like(dcw_ref)
            dln_ref[...] = jnp.zeros_like(dln_ref)
            dws_ref[...] = jnp.zeros_like(dws_ref)
            dbs_acc[...] = jnp.zeros_like(dbs_acc)
            carry_ref[...] = jnp.zeros_like(carry_ref)

        keep_prev = jnp.where(tile % tiles_per_seq == 0, 0.0, 1.0)
        keep_next = jnp.where(tile % tiles_per_seq == tiles_per_seq - 1, 0.0, 1.0)
        cw = cw_ref[...]
        lng = lng_ref[...]
        f = _mixer_forward(p_ref, cprev_ref[...] * keep_prev, xiprev_ref[...], cw, lng, lnb_ref[...],
                           ws_ref, bias_ref, mixed_scr, d)
        dmerged = lax.dot_general(dx_ref[...].astype(BF16), wout_ref[...], NT_DIMS, preferred_element_type=F32)
        sa, sb = f["sa"], f["sb"]
        dp_ref[:, 5 * d:6 * d] = (dmerged * f["ya"] * (sa * (1.0 - sa))).astype(BF16)
        dp_ref[:, 6 * d:7 * d] = (dmerged * f["yb"] * (sb * (1.0 - sb))).astype(BF16)
        dya = dmerged * sa
        dyb = dmerged * sb
        dp_ref[:, 0:d] = (dya * f["conv"]).astype(BF16)
        dconv = dya * f["b"]
        dcw_ref[0:1, :] += _colsum(dconv * f["s2"])
        dcw_ref[1:2, :] += _colsum(dconv * f["s1"])
        dcw_ref[2:3, :] += _colsum(dconv * f["cx"])
        next8 = carry_ref[...] * keep_next
        carry_ref[...] = dconv[:HALO]
        dcx, _, _ = _conv_bwd(dconv, next8, cw)
        dp_ref[:, d:2 * d] = (dcx * f["xi"]).astype(BF16)
        dp_ref[:, 2 * d:3 * d] = (dcx * f["c"]).astype(BF16)
        dp_ref[:, 3 * d:4 * d] = (dyb * f["mixed"]).astype(BF16)
        dmixed = dyb * f["u"]
        dmb = dmixed.astype(BF16)
        vnb = f["vnb"]
        tril = f["tril"]
        triu = (lax.broadcasted_iota(jnp.int32, (CHUNK, CHUNK), 0)
                <= lax.broadcasted_iota(jnp.int32, (CHUNK, CHUNK), 1))
        dbs_tile = dmixed[0:CHUNK]
        for n in range(1, tm // CHUNK):
            dbs_tile = dbs_tile + dmixed[n * CHUNK:(n + 1) * CHUNK]
        dbs_acc[...] += dbs_tile
        for g in range(N_GROUPS):
            wmt = jnp.where(triu, wst_ref[g], 0.0).astype(BF16)
            cols = slice(g * gd, (g + 1) * gd)
            dw = jnp.zeros((CHUNK, CHUNK), F32)
            for n in range(tm // CHUNK):
                rows = slice(n * CHUNK, (n + 1) * CHUNK)
                dvn_scr[rows, cols] = jnp.dot(wmt, dmb[rows, cols], preferred_element_type=F32)
                dw = dw + lax.dot_general(dmb[rows, cols], vnb[rows, cols], NT_DIMS, preferred_element_type=F32)
            dws_ref[g] += jnp.where(tril, dw, 0.0)
        dvn = dvn_scr[...]
        vhat = f["vhat"]
        dln_ref[0:1, :] += _colsum(dvn * vhat)
        dln_ref[1:2, :] += _colsum(dvn)
        dvh = dvn * lng
        dv = f["rstd"] * (dvh - jnp.mean(dvh, axis=-1, keepdims=True)
                          - vhat * jnp.mean(dvh * vhat, axis=-1, keepdims=True))
        dp_ref[:, 4 * d:5 * d] = dv.astype(BF16)

        @pl.when(i == nt - 1)
        def _():
            for g in range(N_GROUPS):
                cols = slice(g * gd, (g + 1) * gd)
                s = jnp.sum(dbs_acc[:, cols], axis=1, keepdims=True)
                dbs_ref[:, cols] = jnp.broadcast_to(s, (CHUNK, gd))

    rev = lambda i: nt - 1 - i

    def halo(col):
        return pl.BlockSpec((HALO, d), lambda i: (jnp.maximum(rev(i) * (tm // HALO) - 1, 0), col))

    const2 = lambda i: (0, 0)
    const3 = lambda i: (0, 0, 0)
    row = lambda i: (rev(i), 0)
    return pl.pallas_call(
        body, name=name, grid=(nt,),
        in_specs=[pl.BlockSpec((tm, d), row),
                  pl.BlockSpec((tm, 7 * d), row),
                  halo(1), halo(2),
                  _once((d, d), const2),
                  _once((HALO, d), const2),
                  _once((1, d), const2),
                  _once((1, d), const2),
                  _once((N_GROUPS, CHUNK, CHUNK), const3),
                  _once((N_GROUPS, CHUNK, CHUNK), const3),
                  _once((CHUNK, d), const2)],
        out_specs=[pl.BlockSpec((tm, 7 * d), row),
                   pl.BlockSpec((HALO, d), const2),
                   pl.BlockSpec((HALO, d), const2),
                   pl.BlockSpec((N_GROUPS, CHUNK, CHUNK), const3),
                   pl.BlockSpec((CHUNK, d), const2)],
        out_shape=[jax.ShapeDtypeStruct((t, 7 * d), BF16),
                   jax.ShapeDtypeStruct((HALO, d), F32),
                   jax.ShapeDtypeStruct((HALO, d), F32),
                   jax.ShapeDtypeStruct((N_GROUPS, CHUNK, CHUNK), F32),
                   jax.ShapeDtypeStruct((CHUNK, d), F32)],
        scratch_shapes=[pltpu.VMEM((tm, d), F32), pltpu.VMEM((tm, d), F32),
                        pltpu.VMEM((HALO, d), F32), pltpu.VMEM((CHUNK, d), F32)],
        compiler_params=_params(1),
    )(dx1, proj, proj, proj, wout, cw, lng, lnb, ws, wst, bias)


def dgrad_rms(dy, w, x, g, res, *, name, tm):
    t, d = x.shape
    n = w.shape[2]
    w = w.reshape(w.shape[0] // 2, 2, d, n)
    nj = w.shape[0]
    tm = min(tm, t)

    def body(dy_ref, w_ref, x_ref, g_ref, res_ref, dx_ref, dg_ref, acc_ref):
        i, j = pl.program_id(0), pl.program_id(1)

        @pl.when((i == 0) & (j == 0))
        def _():
            dg_ref[...] = jnp.zeros_like(dg_ref)

        pair = jnp.concatenate([w_ref[0], w_ref[1]], axis=1)
        part = lax.dot_general(dy_ref[...], pair, NT_DIMS, preferred_element_type=F32)

        @pl.when(j == 0)
        def _():
            acc_ref[...] = part

        @pl.when(j > 0)
        def _():
            acc_ref[...] += part

        @pl.when(j == nj - 1)
        def _():
            dx, dg = _rms_bwd(acc_ref[...], x_ref[...], g_ref[...])
            dx_ref[...] = res_ref[...] + dx
            dg_ref[0:1, :] += dg

    return pl.pallas_call(
        body, name=name, grid=(t // tm, nj),
        in_specs=[pl.BlockSpec((tm, 2 * n), lambda i, j: (i, j)),
                  pl.BlockSpec((None, 2, d, n), lambda i, j: (j, 0, 0, 0)),
                  pl.BlockSpec((tm, d), lambda i, j: (i, 0)),
                  pl.BlockSpec((1, d), lambda i, j: (0, 0)),
                  pl.BlockSpec((tm, d), lambda i, j: (i, 0))],
        out_specs=[pl.BlockSpec((tm, d), lambda i, j: (i, 0)), pl.BlockSpec((HALO, d), lambda i, j: (0, 0))],
        out_shape=[jax.ShapeDtypeStruct((t, d), F32), jax.ShapeDtypeStruct((HALO, d), F32)],
        scratch_shapes=[pltpu.VMEM((tm, d), F32)],
        compiler_params=_params(2),
    )(dy, w, x, g, res)


def wgrad(a, b, *, nj, a_mode, b_mode, name, tm, split=1):
    def describe(arr, mode):
        if mode == "full":
            return arr.shape[0], arr.shape[1], pl.BlockSpec((tm_, arr.shape[1]), lambda j, s: (s, 0))
        if mode == "cols":
            c = arr.shape[1] // nj
            return arr.shape[0], c, pl.BlockSpec((tm_, c), lambda j, s: (s, j))
        return arr.shape[1], arr.shape[2], pl.BlockSpec((None, tm_, arr.shape[2]), lambda j, s: (j, s, 0))

    t = a.shape[0] if a_mode != "lead" else a.shape[1]
    tm_ = min(tm, t)
    _, k, a_spec = describe(a, a_mode)
    _, n, b_spec = describe(b, b_mode)

    ns = t // tm_
    nc = n // split

    def body(a_ref, b_ref, o_ref, acc_ref):
        s = pl.program_id(1)
        part = lax.dot_general(a_ref[...], b_ref[...], TN_DIMS, preferred_element_type=F32)

        def finish(total):
            for q in range(split):
                o_ref[q] = total[:, q * nc:(q + 1) * nc].astype(BF16)

        if ns == 1:
            finish(part)
            return

        @pl.when(s == 0)
        def _():
            acc_ref[...] = part

        @pl.when((s > 0) & (s < ns - 1))
        def _():
            acc_ref[...] += part

        @pl.when(s == ns - 1)
        def _():
            finish(acc_ref[...] + part)

    return pl.pallas_call(
        body, name=name, grid=(nj, ns),
        in_specs=[a_spec, b_spec],
        out_specs=pl.BlockSpec((split, k, nc), lambda j, s: (j, 0, 0)),
        out_shape=jax.ShapeDtypeStruct((nj * split, k, nc), BF16),
        scratch_shapes=[pltpu.VMEM((k, n), F32)],
        compiler_params=_params(2),
    )(a, b)


def _adamw_math(w, g, m, v):
    m = ADAM_B1 * m + (1.0 - ADAM_B1) * g
    v = ADAM_B2 * v + (1.0 - ADAM_B2) * (g * g)
    m_hat = m / (1.0 - ADAM_B1 ** ADAM_STEP)
    v_hat = v / (1.0 - ADAM_B2 ** ADAM_STEP)
    delta = -ADAM_LR * (m_hat / (jnp.sqrt(v_hat) + ADAM_EPS) + ADAM_WD * w)
    return delta, m, v


def _row_tile(rows, at_most):
    if rows <= at_most:
        return rows
    return max(k for k in range(16, at_most + 1, 16) if rows % k == 0)


def _sum_in_device_order(ref):
    total = ref[0]
    for s in range(1, N_DEV):
        total = total + ref[s]
    return total


def adamw_sharded(me, own0, recv0, own1, recv1, w, m, v, *, name, tr):
    _, r, c = w.shape
    tr = _row_tile(r, tr)
    ni = r // tr

    def body(me_ref, o0_ref, r0_ref, o1_ref, r1_ref, w_ref, m_ref, v_ref, g_ref, d_ref, nm_ref, nv_ref):
        def finish(own_ref, recv_ref):
            g = None
            for s in range(N_DEV):
                term = jnp.where(me_ref[0] == s, own_ref[...], recv_ref[s]).astype(F32)
                g = term if g is None else g + term
            delta, nm, nv = _adamw_math(w_ref[...], g, m_ref[...], v_ref[...])
            g_ref[...] = g
            d_ref[...] = delta
            nm_ref[...] = nm
            nv_ref[...] = nv

        @pl.when(pl.program_id(0) == 0)
        def _():
            finish(o0_ref, r0_ref)

        @pl.when(pl.program_id(0) == 1)
        def _():
            finish(o1_ref, r1_ref)

    row0 = lambda l, i: i * (1 - l) + (ni - 1) * l
    row1 = lambda l, i: i * l
    lay = pl.BlockSpec((None, tr, c), lambda l, i, me_ref: (l, i, 0))
    grid_spec = pltpu.PrefetchScalarGridSpec(
        num_scalar_prefetch=1, grid=(2, ni),
        in_specs=[pl.BlockSpec((None, tr, c), lambda l, i, me_ref: (me_ref[0], row0(l, i), 0)),
                  pl.BlockSpec((N_DEV, tr, c), lambda l, i, me_ref: (0, row0(l, i), 0)),
                  pl.BlockSpec((None, tr, c), lambda l, i, me_ref: (me_ref[0], row1(l, i), 0)),
                  pl.BlockSpec((N_DEV, tr, c), lambda l, i, me_ref: (0, row1(l, i), 0)),
                  lay, lay, lay],
        out_specs=[lay, lay, lay, lay])
    return pl.pallas_call(
        body, name=name, grid_spec=grid_spec,
        out_shape=[jax.ShapeDtypeStruct(w.shape, F32)] * 4,
        compiler_params=_params(2),
    )(me, own0, recv0, own1, recv1, w, m, v)


def adamw_small(g, w, m, v, *, name):
    def body(g_ref, w_ref, m_ref, v_ref, d_ref, nm_ref, nv_ref):
        delta, nm, nv = _adamw_math(w_ref[...], g_ref[...], m_ref[...], v_ref[...])
        d_ref[...] = delta
        nm_ref[...] = nm
        nv_ref[...] = nv

    return pl.pallas_call(
        body, name=name,
        out_shape=[jax.ShapeDtypeStruct(w.shape, F32)] * 3,
        compiler_params=pltpu.CompilerParams(vmem_limit_bytes=VMEM_LIMIT_BYTES),
    )(g, w, m, v)


def sum_devices(parts, *, name, tr):
    _, r, c = parts.shape
    tr = min(tr, r)

    def body(p_ref, o_ref):
        o_ref[...] = _sum_in_device_order(p_ref)

    return pl.pallas_call(
        body, name=name, grid=(r // tr,),
        in_specs=[pl.BlockSpec((N_DEV, tr, c), lambda i: (0, i, 0))],
        out_specs=pl.BlockSpec((tr, c), lambda i: (i, 0)),
        out_shape=jax.ShapeDtypeStruct((r, c), F32),
        compiler_params=_params(1),
    )(parts)


def _my_place():
    return lax.axis_index("x"), lax.axis_index("y"), lax.axis_index("c")


def all_gather(arrays, *, name):
    n = len(arrays)

    def body(*refs):
        ins, outs = refs[:n], refs[n:2 * n]
        send_sems, recv_sems, local_sems = refs[2 * n:]
        x, y, c = _my_place()
        me, sibling = (x, y, c), (x, y, 1 - c)
        chips = [(1 - x, y), (x, 1 - y), (1 - x, 1 - y)]
        waits = []
        for a in range(n):
            def slot(place, a=a):
                px, py, pc = place
                return outs[a].at[4 * px + 2 * py + pc]

            def copy(k, block, to, src=None, a=a, slot=slot):
                return pltpu.make_async_remote_copy(
                    src_ref=slot(block) if src is None else src, dst_ref=slot(block),
                    send_sem=send_sems.at[a, k], recv_sem=recv_sems.at[a, k],
                    device_id=to, device_id_type=MESH)

            mine = pltpu.make_async_copy(ins[a], slot(me), local_sems.at[a])
            mine.start()
            first = [copy(0, me, sibling, src=ins[a])]
            first += [copy(1 + j, me, (*chip, c), src=ins[a]) for j, chip in enumerate(chips)]
            for cp in first:
                cp.start()
            waits.append((copy, mine, first))
        sends = []
        for a in range(n):
            copy, mine, first = waits[a]
            passed = [copy(4 + j, (*chip, c), sibling) for j, chip in enumerate(chips)]
            for j, chip in enumerate(chips):
                copy(1 + j, (*chip, c), me).wait_recv()
                passed[j].start()
            sends.append(first + passed)
        for a in range(n):
            copy, mine, first = waits[a]
            copy(0, sibling, me).wait_recv()
            for j, chip in enumerate(chips):
                copy(4 + j, (*chip, 1 - c), me).wait_recv()
            for cp in sends[a]:
                cp.wait_send()
            mine.wait()

    return pl.pallas_call(
        body, name=name,
        in_specs=[ANY] * n, out_specs=[ANY] * n,
        out_shape=[jax.ShapeDtypeStruct((N_DEV,) + a.shape, a.dtype) for a in arrays],
        scratch_shapes=[pltpu.SemaphoreType.DMA((n, 7)), pltpu.SemaphoreType.DMA((n, 7)),
                        pltpu.SemaphoreType.DMA((n,))],
        compiler_params=pltpu.CompilerParams(has_side_effects=True),
    )(*arrays)


def _peer_place(r, x, y, c):
    fx, fy, fc = (r >> 2) & 1, (r >> 1) & 1, r & 1
    return (1 - x if fx else x, 1 - y if fy else y, 1 - c if fc else c)


def own_slot(me, w, layer, dtype, *, name, tr):
    _, r, c = w.shape
    tr = _row_tile(r, tr)

    def body(me_ref, w_ref, o_ref):
        o_ref[...] = w_ref[...].astype(dtype)

    grid_spec = pltpu.PrefetchScalarGridSpec(
        num_scalar_prefetch=1, grid=(r // tr,),
        in_specs=[pl.BlockSpec((None, tr, c), lambda i, me_ref: (layer, i, 0))],
        out_specs=pl.BlockSpec((None, tr, c), lambda i, me_ref: (me_ref[0], i, 0)))
    return pl.pallas_call(
        body, name=name, grid_spec=grid_spec,
        out_shape=jax.ShapeDtypeStruct((N_DEV, r, c), dtype),
        compiler_params=_params(1),
    )(me, w)


EXCHANGES = {
    "scatter": [(0, r) for r in range(1, N_DEV)],
    "gather": [(0, r) for r in range(1, N_DEV)],
    "gather_chips": [(0, r) for r in (1, 2, 4, 6)],
    "gather_forward": [(q, 1) for q in (2, 4, 6)],
}


def _split_copy(k, entry, src, land, send_sem, recv_sem, arriving):
    slot, peer = entry
    x, y, c = _my_place()

    def index(relation):
        px, py, pc = _peer_place(relation, x, y, c)
        return 4 * px + 2 * py + pc

    return pltpu.make_async_remote_copy(
        src_ref=land.at[index(slot)] if src is None else src.at[index(peer)],
        dst_ref=land.at[index(slot ^ peer if arriving else slot)],
        send_sem=send_sem.at[k], recv_sem=recv_sem.at[k],
        device_id=_peer_place(peer, x, y, c), device_id_type=MESH)


def start_copies(srcs, lands, *, mode, name, after=None):
    n = len(lands)
    entries = EXCHANGES[mode]
    bufs = (list(srcs) if srcs is not None else []) + list(lands)
    nb = len(bufs)

    def body(*refs):
        src = refs[:n] if srcs is not None else [None] * n
        land = refs[nb - n:nb]
        outs = refs[nb + len(extra):]
        send_sems, recv_sems = outs[:n], outs[n:2 * n]
        token = outs[2 * n + nb]
        for a in range(n):
            for k, entry in enumerate(entries):
                _split_copy(k, entry, src[a], land[a], send_sems[a], recv_sems[a], False).start()
        token[...] = jnp.zeros_like(token)

    extra = [] if after is None else [after]
    outs = pl.pallas_call(
        body, name=name,
        in_specs=[HBM_SPEC] * nb + [ANY] * len(extra),
        out_specs=[SEM_SPEC] * (2 * n) + [HBM_SPEC] * nb + [pl.BlockSpec(memory_space=pltpu.VMEM)],
        out_shape=([pltpu.SemaphoreType.DMA((len(entries),))] * (2 * n)
                   + [pltpu.HBM(a.shape, a.dtype) for a in bufs]
                   + [jax.ShapeDtypeStruct((8, 128), F32)]),
        input_output_aliases={i: 2 * n + i for i in range(nb)},
        compiler_params=pltpu.CompilerParams(has_side_effects=DATAFLOW),
    )(*[pltpu.with_memory_space_constraint(a, pltpu.HBM) for a in bufs], *extra)
    thru = list(outs[2 * n:2 * n + nb])
    return dict(send=outs[:n], recv=outs[n:2 * n], src=thru[:n] if srcs is not None else None, land=thru[nb - n:],
                token=outs[2 * n + nb], mode=mode)


def finish_copies(started, which, after, *, name):
    n = len(which)
    entries = EXCHANGES[started["mode"]]
    has_src = started["src"] is not None
    bufs = ([started["src"][i] for i in which] if has_src else []) + [started["land"][i] for i in which]
    nb = len(bufs)

    def body(*refs):
        src = refs[:n] if has_src else [None] * n
        land = refs[nb - n:nb]
        send_sems, recv_sems = refs[nb:nb + n], refs[nb + n:nb + 2 * n]
        for a in range(n):
            for k, entry in enumerate(entries):
                cp = _split_copy(k, entry, src[a], land[a], send_sems[a], recv_sems[a], True)
                cp.wait_send()
                cp.wait_recv()

    outs = pl.pallas_call(
        body, name=name,
        in_specs=[HBM_SPEC] * nb + [SEM_SPEC] * (2 * n) + [ANY],
        out_specs=[HBM_SPEC] * nb,
        out_shape=[pltpu.HBM(a.shape, a.dtype) for a in bufs],
        input_output_aliases={i: i for i in range(nb)},
        compiler_params=pltpu.CompilerParams(has_side_effects=DATAFLOW),
    )(*bufs, *[started["send"][i] for i in which], *[started["recv"][i] for i in which], after)
    return (list(outs[:n]) if has_src else None), list(outs[nb - n:])


def _pad_rows(a, rows):
    pad = [(0, 0)] * a.ndim
    pad[-2] = (0, rows - a.shape[-2])
    return jnp.pad(a, pad)


def kernel(x, mix_norm_g, w_in, conv_a_w, ln_v_g, ln_v_b, w_s, b_s, w_out, ffn_norm_g, w_up, conv_ffn_w, w_down, final_norm_g, loss_target, m_mix_norm_g, m_w_in, m_conv_a_w, m_ln_v_g, m_ln_v_b, m_w_s, m_b_s, m_w_out, m_ffn_norm_g, m_w_up, m_conv_ffn_w, m_w_down, m_final_norm_g, v_mix_norm_g, v_w_in, v_conv_a_w, v_ln_v_g, v_ln_v_b, v_w_s, v_b_s, v_w_out, v_ffn_norm_g, v_w_up, v_conv_ffn_w, v_w_down, v_final_norm_g):
    nb, seq, d = x.shape
    t = nb * seq
    depth = w_in.shape[0]
    f = w_up.shape[2]
    me = 4 * lax.axis_index("x") + 2 * lax.axis_index("y") + lax.axis_index("c")
    xt = x.reshape(t, d)
    tgt = loss_target.reshape(t, d)

    conv_pack = jnp.concatenate([_pad_rows(conv_a_w, HALO), _pad_rows(conv_ffn_w, HALO)], axis=-1)
    me_arr = me.astype(jnp.int32).reshape(1)
    w_up_t, m_w_up_t, v_w_up_t = (jnp.swapaxes(a, 1, 2) for a in (w_up, m_w_up, v_w_up))
    zones, slot_of = [], {}
    for l in range(depth):
        for key, w in (("win", w_in), ("conv", None), ("wout", w_out), ("wup", w_up_t), ("wd", w_down)):
            if key == "conv":
                if l == 0:
                    slot_of["conv"] = len(zones)
                    packed = conv_pack.reshape(1, depth * HALO, conv_pack.shape[-1])
                    zones.append(own_slot(me_arr, packed, 0, F32, name="own_slot_conv", tr=256))
                continue
            slot_of[key, l] = len(zones)
            zones.append(own_slot(me_arr, w, l, BF16, name=f"own_slot_{key}_{l}", tr=256))
    first = [slot_of["win", 0], slot_of["wout", 0], slot_of["conv"]]
    rest = [i for i in range(len(zones)) if i not in first]
    to_chips = start_copies(None, [zones[i] for i in first], mode="gather_chips", name="gather_first_chips")
    gathering = start_copies(None, [zones[i] for i in rest], mode="gather", name="gather_start", after=to_chips["token"])
    _, at_chips = finish_copies(to_chips, [0, 1, 2], gathering["token"], name="wait_first_chips")
    to_sibling = start_copies(None, at_chips, mode="gather_forward", name="gather_first_forward")

    def gathered(keys, after, name):
        return finish_copies(gathering, [rest.index(slot_of[k]) for k in keys], after, name=name)[1]

    saved, layers = [], []
    cur = xt
    for l in range(depth):
        p = dict(mix_g=mix_norm_g[l][None], ffn_g=ffn_norm_g[l][None], lng=ln_v_g[l][None], lnb=ln_v_b[l][None],
                 ws=w_s[l], wst=jnp.swapaxes(w_s[l], 1, 2),
                 bias=jnp.repeat(b_s[l].T, d // N_GROUPS, axis=1))
        if l == 0:
            _, (p["win"], wout_g, conv_g) = finish_copies(to_sibling, [0, 1, 2], to_sibling["token"],
                                                          name=f"wait_w_mixer_{l}")
            conv_g = conv_g.reshape(N_DEV, depth, HALO, -1)
            ca = conv_g.shape[-1] - f
        else:
            p["win"], wout_g = gathered([("win", l), ("wout", l)], after, f"wait_w_mixer_{l}")
        p["wout"] = wout_g.reshape(d, d)
        p["cw_a"] = jnp.transpose(conv_g[:, l, :, :ca], (1, 0, 2)).reshape(HALO, d)
        p["cw_f"] = conv_g[:, l, :, ca:]
        h, proj, merged, x1 = mixer_fwd(cur, p["mix_g"], p["win"], p["wout"], p["cw_a"], p["lng"], p["lnb"], p["ws"],
                                        p["bias"], seq=seq, name=f"mixer_fwd_{l}", tm=128)
        p["wup"], wd_g = gathered([("wup", l), ("wd", l)], merged, f"wait_w_ffn_{l}")
        p["wd"] = wd_g.reshape(N_DEV // 2, 2 * wd_g.shape[1], d)
        h2, up0, upc, act, x2 = ffn_fwd(x1, p["ffn_g"], p["wup"], p["wd"], p["cw_f"],
                                        seq=seq, name=f"ffn_fwd_{l}", tm=256)
        saved.append(dict(x0=cur, h=h, proj=proj, merged=merged, x1=x1, h2=h2, up0=up0, upc=upc, act=act))
        layers.append(p)
        cur, after = x2, act
    dx, d_final_g, loss_tile = final_loss(cur, final_norm_g[None], tgt, name="final_loss", tm=512)

    def exchange(parts, name):
        return start_copies(parts, [lax.empty(a.shape, a.dtype) for a in parts], mode="scatter", name=name)

    def tied(g, started):
        return g + started["token"][0:1, 0:1]

    part = [None] * depth
    for l in reversed(range(depth)):
        p, s = layers[l], saved[l]
        dup0, dcw_f, dx1, d_ffn_g = ffn_bwd(dx, s["up0"], s["upc"], p["wd"], p["cw_f"], p["wup"], s["x1"], p["ffn_g"],
                                            seq=seq, name=f"ffn_bwd_{l}", tm=256)
        g_wd = wgrad(s["act"], dx, nj=N_DEV // 2, a_mode="lead", b_mode="full", name=f"wgrad_down_{l}", tm=2048)
        g_wup = wgrad(dup0, s["h2"], nj=N_DEV, a_mode="lead", b_mode="full", name=f"wgrad_up_{l}", tm=2048)
        ffn_ex = exchange([g_wd.reshape(N_DEV, g_wd.shape[1] // 2, d), g_wup], f"exchange_ffn_{l}")
        dproj, dcw_a, dln, dws, dbs = mixer_bwd(dx1, s["proj"], p["wout"], tied(p["cw_a"], ffn_ex), p["lng"], p["lnb"],
                                                 p["ws"], p["wst"], p["bias"], seq=seq, name=f"mixer_bwd_{l}", tm=128)
        g_wout = wgrad(s["merged"], dx1, nj=1, a_mode="full", b_mode="full", name=f"wgrad_out_{l}", tm=2048)
        g_win = wgrad(s["h"], dproj, nj=N_DEV // 2, a_mode="full", b_mode="cols", name=f"wgrad_in_{l}", tm=2048, split=2)
        mix_ex = exchange([g_wout.reshape(N_DEV, d // N_DEV, d), g_win], f"exchange_mix_{l}")
        dx, d_mix_g = dgrad_rms(dproj, p["win"], s["x0"], tied(p["mix_g"], mix_ex), dx1, name=f"dgrad_in_{l}", tm=1024)
        part[l] = dict(
            ffn_ex=ffn_ex, mix_ex=mix_ex,
            small=jnp.concatenate([
                dws.reshape(N_GROUPS * CHUNK * CHUNK // d, d),
                d_mix_g[0:1], d_ffn_g[0:1], dln[0:2], dcw_a[0:3],
                dbs[:, ::d // N_GROUPS].T.reshape(1, d)], axis=0),
            cw_f=dcw_f.reshape(N_DEV * HALO, f))
    grad_x = dx.reshape(nb, seq, d)

    loss_row = jnp.zeros((1, d), F32).at[0, 0].set(loss_tile[0, 0])
    small = jnp.concatenate([part[l]["small"] for l in range(depth)] + [d_final_g[0:1], loss_row], axis=0)
    small = _pad_rows(small, -(-small.shape[0] // 8) * 8)
    cwf = jnp.concatenate([part[l]["cw_f"] for l in range(depth)], axis=0)
    small_all, cwf_all = all_gather([small, cwf], name="gather_small_grads")
    small_sum = sum_devices(small_all, name="sum_small", tr=512)
    cwf_sum = sum_devices(cwf_all, name="sum_conv_ffn", tr=512)


    rows_ws = N_GROUPS * CHUNK * CHUNK // d
    per_layer = rows_ws + 8
    def small_of(l, a, b):
        return small_sum[l * per_layer + rows_ws + a:l * per_layer + rows_ws + b]
    g_ws = jnp.stack([small_sum[l * per_layer:l * per_layer + rows_ws].reshape(N_GROUPS, CHUNK, CHUNK)
                      for l in range(depth)])
    g_mix = jnp.concatenate([small_of(l, 0, 1) for l in range(depth)])
    g_ffn = jnp.concatenate([small_of(l, 1, 2) for l in range(depth)])
    g_lng = jnp.concatenate([small_of(l, 2, 3) for l in range(depth)])
    g_lnb = jnp.concatenate([small_of(l, 3, 4) for l in range(depth)])
    g_cwa_full = jnp.stack([small_of(l, 4, 7) for l in range(depth)])
    g_cwa = lax.dynamic_slice_in_dim(g_cwa_full, me * ca, ca, axis=2)
    g_bs = jnp.stack([small_of(l, 7, 8).reshape(N_GROUPS, CHUNK) for l in range(depth)])
    g_final = small_sum[depth * per_layer]
    loss = small_sum[depth * per_layer + 1, 0]
    cwf_sum = cwf_sum.reshape(depth, N_DEV, HALO, f)
    g_cwf = lax.dynamic_index_in_dim(cwf_sum, me, axis=1, keepdims=False)[:, :3]

    own, recv = {}, {}

    def arrived(l, ex, keys, after):
        srcs, lands = finish_copies(part[l][ex], [0, 1], after, name=f"wait_{ex}_{l}")
        for k, key in enumerate(keys):
            own[key, l], recv[key, l] = srcs[k], lands[k]
        return lands[1]

    def big(key, w, m, v, name):
        return adamw_sharded(me_arr, own[key, 0], recv[key, 0], own[key, 1], recv[key, 1], w, m, v, name=name, tr=256)

    after = grad_x
    for l in reversed(range(depth)):
        after = arrived(l, "ffn_ex", ("wd", "wup"), after)
        if l > 0:
            after = arrived(l, "mix_ex", ("wout", "win"), after)
    u_wd = big("wd", w_down, m_w_down, v_w_down, "adamw_w_down")
    u_wup = tuple(jnp.swapaxes(a, 1, 2) for a in big("wup", w_up_t, m_w_up_t, v_w_up_t, "adamw_w_up"))
    arrived(0, "mix_ex", ("wout", "win"), u_wup[1])
    u_wout = big("wout", w_out, m_w_out, v_w_out, "adamw_w_out")
    u_win = big("win", w_in, m_w_in, v_w_in, "adamw_w_in")

    def small_update(g, w, m, v, name):
        shape = w.shape
        two_d = (-1, shape[-1]) if w.ndim > 1 else (1, shape[0])
        out = adamw_small(g.reshape(two_d), w.reshape(two_d), m.reshape(two_d), v.reshape(two_d), name=name)
        return (g.reshape(shape),) + tuple(o.reshape(shape) for o in out)

    u_mix = small_update(g_mix, mix_norm_g, m_mix_norm_g, v_mix_norm_g, "adamw_mix_norm_g")
    u_cwa = small_update(g_cwa, conv_a_w, m_conv_a_w, v_conv_a_w, "adamw_conv_a_w")
    u_lng = small_update(g_lng, ln_v_g, m_ln_v_g, v_ln_v_g, "adamw_ln_v_g")
    u_lnb = small_update(g_lnb, ln_v_b, m_ln_v_b, v_ln_v_b, "adamw_ln_v_b")
    u_ws = small_update(g_ws, w_s, m_w_s, v_w_s, "adamw_w_s")
    u_bs = small_update(g_bs, b_s, m_b_s, v_b_s, "adamw_b_s")
    u_ffn = small_update(g_ffn, ffn_norm_g, m_ffn_norm_g, v_ffn_norm_g, "adamw_ffn_norm_g")
    u_cwf = small_update(g_cwf, conv_ffn_w, m_conv_ffn_w, v_conv_ffn_w, "adamw_conv_ffn_w")
    u_final = small_update(g_final, final_norm_g, m_final_norm_g, v_final_norm_g, "adamw_final_norm_g")

    ordered = [u_mix, u_win, u_cwa, u_lng, u_lnb, u_ws, u_bs, u_wout, u_ffn, u_wup, u_cwf, u_wd, u_final]
    return (loss, grad_x, *[u[0] for u in ordered], *[u[1] for u in ordered],
            *[u[2] for u in ordered], *[u[3] for u in ordered])
```

```python
import functools

import jax
import jax.numpy as jnp
from jax import lax
from jax.experimental import pallas as pl
from jax.experimental.pallas import tpu as pltpu

EPS = 1e-6
CHUNK = 128
N_GROUPS = 8
N_DEV = 8
HALO = 8
ADAM_LR = 0.001
ADAM_B1 = 0.9
ADAM_B2 = 0.999
ADAM_EPS = 1e-08
ADAM_WD = 0.01
ADAM_STEP = 10
VMEM_LIMIT_BYTES = 56 * 1024 * 1024
F32 = jnp.float32
BF16 = jnp.bfloat16
MESH = pl.DeviceIdType.MESH
ANY = pl.BlockSpec(memory_space=pl.ANY)
HBM_SPEC = pl.BlockSpec(memory_space=pltpu.HBM)
SEM_SPEC = pl.BlockSpec(memory_space=pltpu.SEMAPHORE)
DATAFLOW = pltpu.SideEffectType.DATAFLOW_SIDE_EFFECTING
NT_DIMS = (((1,), (1,)), ((), ()))
TN_DIMS = (((0,), (0,)), ((), ()))


def _params(n_grid_axes):
    return pltpu.CompilerParams(dimension_semantics=("arbitrary",) * n_grid_axes,
                                vmem_limit_bytes=VMEM_LIMIT_BYTES)


def _shift_down(cur, prev8, k):
    rolled = pltpu.roll(cur, k, 0)
    prolled = pltpu.roll(prev8, k, 0)
    row = lax.broadcasted_iota(jnp.int32, prev8.shape, 0)
    head = jnp.where(row < k, prolled, rolled[:HALO])
    return jnp.concatenate([head, rolled[HALO:]], axis=0)


def _shift_up(cur, next8, k):
    tm = cur.shape[0]
    rolled = pltpu.roll(cur, tm - k, 0)
    nrolled = pltpu.roll(next8, HALO - k, 0)
    row = lax.broadcasted_iota(jnp.int32, next8.shape, 0)
    tail = jnp.where(row >= HALO - k, nrolled, rolled[tm - HALO:])
    return jnp.concatenate([rolled[:tm - HALO], tail], axis=0)


def _conv_fwd(cur, prev8, cw):
    s1 = _shift_down(cur, prev8, 1)
    s2 = _shift_down(cur, prev8, 2)
    y = s2 * cw[0:1, :] + s1 * cw[1:2, :] + cur * cw[2:3, :]
    return y, s1, s2


def _conv_bwd(d, next8, cw):
    u1 = _shift_up(d, next8, 1)
    u2 = _shift_up(d, next8, 2)
    return d * cw[2:3, :] + u1 * cw[1:2, :] + u2 * cw[0:1, :], u1, u2


def _colsum(a):
    return jnp.sum(a, axis=0, keepdims=True)


def _rms_stats(xv):
    r = lax.rsqrt(jnp.mean(xv * xv, axis=-1, keepdims=True) + EPS)
    return r, xv * r


def _rms_bwd(dh, xv, g):
    r, n = _rms_stats(xv)
    dn = dh * g
    dx = r * (dn - n * jnp.mean(dn * n, axis=-1, keepdims=True))
    return dx, _colsum(dh * n)


def _mixer_forward(p_ref, cprev, xiprev, cw, lng, lnb, ws_ref, bias_ref, mixed_scr, d):
    tm = p_ref.shape[0]
    b = p_ref[:, 0:d]
    c = p_ref[:, d:2 * d]
    xi = p_ref[:, 2 * d:3 * d]
    u = p_ref[:, 3 * d:4 * d]
    v = p_ref[:, 4 * d:5 * d]
    sa = jax.nn.sigmoid(p_ref[:, 5 * d:6 * d])
    sb = jax.nn.sigmoid(p_ref[:, 6 * d:7 * d])
    cx = c * xi
    conv, s1, s2 = _conv_fwd(cx, cprev * xiprev, cw)
    ya = b * conv
    mu = jnp.mean(v, axis=-1, keepdims=True)
    xc = v - mu
    rstd = lax.rsqrt(jnp.mean(xc * xc, axis=-1, keepdims=True) + EPS)
    vhat = xc * rstd
    vnb = (vhat * lng + lnb).astype(BF16)
    tril = (lax.broadcasted_iota(jnp.int32, (CHUNK, CHUNK), 0)
            >= lax.broadcasted_iota(jnp.int32, (CHUNK, CHUNK), 1))
    gd = d // N_GROUPS
    for g in range(N_GROUPS):
        wm = jnp.where(tril, ws_ref[g], 0.0).astype(BF16)
        cols = slice(g * gd, (g + 1) * gd)
        for n in range(tm // CHUNK):
            rows = slice(n * CHUNK, (n + 1) * CHUNK)
            mixed_scr[rows, cols] = (jnp.dot(wm, vnb[rows, cols], preferred_element_type=F32)
                                     + bias_ref[:, cols])
    mixed = mixed_scr[...]
    yb = u * mixed
    merged = sa * ya + sb * yb
    return dict(b=b, c=c, xi=xi, u=u, sa=sa, sb=sb, cx=cx, s1=s1, s2=s2, conv=conv, ya=ya,
                rstd=rstd, vhat=vhat, vnb=vnb, mixed=mixed, yb=yb, merged=merged, tril=tril)


def _once(block_shape, index_map):
    return pl.BlockSpec(block_shape, index_map, pipeline_mode=pl.Buffered(1))


def mixer_fwd(x, g, win, wout, cw, lng, lnb, ws, bias, *, seq, name, tm):
    t, d = x.shape
    nj, _, n = win.shape
    tm = min(tm, seq)
    tiles_per_seq = seq // tm

    def body(x_ref, g_ref, win_ref, wout_ref, cw_ref, lng_ref, lnb_ref, ws_ref, bias_ref,
             h_ref, p_ref, merged_ref, x1_ref, mixed_scr, carry_ref):
        @pl.when(pl.program_id(0) == 0)
        def _():
            carry_ref[...] = jnp.zeros_like(carry_ref)

        keep = jnp.where(pl.program_id(0) % tiles_per_seq == 0, 0.0, 1.0)
        xv = x_ref[...]
        _, nrm = _rms_stats(xv)
        hb = (nrm * g_ref[...]).astype(BF16)
        h_ref[...] = hb
        for j in range(0, nj, 2):
            pair = jnp.concatenate([win_ref[j], win_ref[j + 1]], axis=1)
            p_ref[:, j * n:(j + 2) * n] = jnp.dot(hb, pair, preferred_element_type=F32)
        f = _mixer_forward(p_ref, carry_ref[...] * keep, 1.0, cw_ref[...], lng_ref[...],
                           lnb_ref[...], ws_ref, bias_ref, mixed_scr, d)
        carry_ref[...] = f["cx"][tm - HALO:]
        mb = f["merged"].astype(BF16)
        merged_ref[...] = mb
        x1_ref[...] = xv + jnp.dot(mb, wout_ref[...], preferred_element_type=F32)

    const2 = lambda i: (0, 0)
    const3 = lambda i: (0, 0, 0)
    row = lambda i: (i, 0)
    return pl.pallas_call(
        body, name=name, grid=(t // tm,),
        in_specs=[pl.BlockSpec((tm, d), row),
                  _once((1, d), const2),
                  _once((nj, d, n), const3),
                  _once((d, d), const2),
                  _once((HALO, d), const2),
                  _once((1, d), const2),
                  _once((1, d), const2),
                  _once((N_GROUPS, CHUNK, CHUNK), const3),
                  _once((CHUNK, d), const2)],
        out_specs=[pl.BlockSpec((tm, d), row), pl.BlockSpec((tm, nj * n), row),
                   pl.BlockSpec((tm, d), row), pl.BlockSpec((tm, d), row)],
        out_shape=[jax.ShapeDtypeStruct((t, d), BF16), jax.ShapeDtypeStruct((t, nj * n), F32),
                   jax.ShapeDtypeStruct((t, d), BF16), jax.ShapeDtypeStruct((t, d), F32)],
        scratch_shapes=[pltpu.VMEM((tm, d), F32), pltpu.VMEM((HALO, d), F32)],
        compiler_params=_params(1),
    )(x, g, win, wout, cw, lng, lnb, ws, bias)


def ffn_fwd(x1, g, wup, wd, cw, *, seq, name, tm):
    t, d = x1.shape
    nj, f, _ = wup.shape
    half = nj // 2
    tm = min(tm, seq)
    tiles_per_seq = seq // tm

    def body(x1_ref, g_ref, wup_ref, wd_ref, cw_ref, h2_ref, up_ref, upc_ref, act_ref, x2_ref, carry_ref):
        @pl.when(pl.program_id(0) == 0)
        def _():
            carry_ref[...] = jnp.zeros_like(carry_ref)

        keep = jnp.where(pl.program_id(0) % tiles_per_seq == 0, 0.0, 1.0)
        xv = x1_ref[...]
        _, nrm = _rms_stats(xv)
        hb = (nrm * g_ref[...]).astype(BF16)
        h2_ref[...] = hb

        for j in range(nj):
            up_ref[j] = lax.dot_general(hb, wup_ref[j], NT_DIMS, preferred_element_type=F32)

        def conv_of(j):
            up0 = up_ref[j]
            y, _, _ = _conv_fwd(up0, carry_ref[j] * keep, cw_ref[j])
            carry_ref[j] = up0[tm - HALO:]
            upc_ref[j] = y.astype(BF16)
            return y

        acc = xv
        for k in range(half):
            a = (jax.nn.silu(conv_of(k)) * conv_of(k + half)).astype(BF16)
            act_ref[k] = a
            acc = acc + jnp.dot(a, wd_ref[k], preferred_element_type=F32)
        x2_ref[...] = acc

    const3 = lambda i: (0, 0, 0)
    return pl.pallas_call(
        body, name=name, grid=(t // tm,),
        in_specs=[pl.BlockSpec((tm, d), lambda i: (i, 0)),
                  _once((1, d), lambda i: (0, 0)),
                  _once((nj, f, d), const3),
                  _once((half, f, d), const3),
                  _once((nj, HALO, f), const3)],
        out_specs=[pl.BlockSpec((tm, d), lambda i: (i, 0)),
                   pl.BlockSpec((nj, tm, f), lambda i: (0, i, 0)),
                   pl.BlockSpec((nj, tm, f), lambda i: (0, i, 0)),
                   pl.BlockSpec((half, tm, f), lambda i: (0, i, 0)),
                   pl.BlockSpec((tm, d), lambda i: (i, 0))],
        out_shape=[jax.ShapeDtypeStruct((t, d), BF16), jax.ShapeDtypeStruct((nj, t, f), F32),
                   jax.ShapeDtypeStruct((nj, t, f), BF16),
                   jax.ShapeDtypeStruct((half, t, f), BF16), jax.ShapeDtypeStruct((t, d), F32)],
        scratch_shapes=[pltpu.VMEM((nj, HALO, f), F32)],
        compiler_params=_params(1),
    )(x1, g, wup, wd, cw)


def final_loss(x, g, target, *, name, tm):
    t, d = x.shape
    tm = min(tm, t)

    def body(x_ref, g_ref, tgt_ref, dx_ref, dg_ref, loss_ref):
        @pl.when(pl.program_id(0) == 0)
        def _():
            dg_ref[...] = jnp.zeros_like(dg_ref)
            loss_ref[...] = jnp.zeros_like(loss_ref)

        xv = x_ref[...]
        gv = g_ref[...]
        r, n = _rms_stats(xv)
        err = n * gv - tgt_ref[...]
        loss_ref[...] += 0.5 * jnp.sum(jnp.mean(err * err, axis=-1, keepdims=True))
        dy = err * (1.0 / d)
        dn = dy * gv
        dx_ref[...] = r * (dn - n * jnp.mean(dn * n, axis=-1, keepdims=True))
        dg_ref[0:1, :] += _colsum(dy * n)

    return pl.pallas_call(
        body, name=name, grid=(t // tm,),
        in_specs=[pl.BlockSpec((tm, d), lambda i: (i, 0)),
                  pl.BlockSpec((1, d), lambda i: (0, 0)),
                  pl.BlockSpec((tm, d), lambda i: (i, 0))],
        out_specs=[pl.BlockSpec((tm, d), lambda i: (i, 0)),
                   pl.BlockSpec((HALO, d), lambda i: (0, 0)),
                   pl.BlockSpec((8, 128), lambda i: (0, 0))],
        out_shape=[jax.ShapeDtypeStruct((t, d), F32), jax.ShapeDtypeStruct((HALO, d), F32),
                   jax.ShapeDtypeStruct((8, 128), F32)],
        compiler_params=_params(1),
    )(x, g, target)


def ffn_bwd(dx2, up0, upc, wd, cw, wup, x1, g, *, seq, name, tm):
    t, d = dx2.shape
    nj, _, f = up0.shape
    half = nj // 2
    tm = min(tm, seq)
    tiles_per_seq = seq // tm
    nt = t // tm

    def body(dx_ref, up_ref, upc_ref, wd_ref, cw_ref, wup_ref, x1_ref, g_ref,
             dup_ref, dcw_ref, dx1_ref, dg_ref, carry_ref):
        i = pl.program_id(0)
        tile = nt - 1 - i

        @pl.when(i == 0)
        def _():
            dcw_ref[...] = jnp.zeros_like(dcw_ref)
            dg_ref[...] = jnp.zeros_like(dg_ref)
            carry_ref[...] = jnp.zeros_like(carry_ref)

        keep_next = jnp.where(tile % tiles_per_seq == tiles_per_seq - 1, 0.0, 1.0)
        dx2v = dx_ref[...]
        dxb = dx2v.astype(BF16)
        dh = [jnp.zeros((tm, d), F32)]

        def through_conv(j, dup):
            next8 = carry_ref[j] * keep_next
            carry_ref[j] = dup[:HALO]
            dup0, u1, u2 = _conv_bwd(dup, next8, cw_ref[j])
            up0 = up_ref[j]
            dcw_ref[j, 0:1, :] += _colsum(u2 * up0)
            dcw_ref[j, 1:2, :] += _colsum(u1 * up0)
            dcw_ref[j, 2:3, :] += _colsum(dup * up0)
            dup0 = dup0.astype(BF16)
            dup_ref[j] = dup0
            dh[0] = dh[0] + jnp.dot(dup0, wup_ref[j], preferred_element_type=F32)

        dacts = [lax.dot_general(dxb, wd_ref[k], NT_DIMS, preferred_element_type=F32) for k in range(half)]
        for k in range(half):
            gate = upc_ref[k].astype(F32)
            val = upc_ref[k + half].astype(F32)
            dact = dacts[k]
            sg = jax.nn.sigmoid(gate)
            through_conv(k, dact * val * (sg * (1.0 + gate * (1.0 - sg))))
            through_conv(k + half, dact * (gate * sg))

        dx, dg = _rms_bwd(dh[0], x1_ref[...], g_ref[...])
        dx1_ref[...] = dx2v + dx
        dg_ref[0:1, :] += dg

    rev = lambda i: nt - 1 - i
    return pl.pallas_call(
        body, name=name, grid=(nt,),
        in_specs=[pl.BlockSpec((tm, d), lambda i: (rev(i), 0)),
                  pl.BlockSpec((nj, tm, f), lambda i: (0, rev(i), 0)),
                  pl.BlockSpec((nj, tm, f), lambda i: (0, rev(i), 0)),
                  _once((half, f, d), lambda i: (0, 0, 0)),
                  _once((nj, HALO, f), lambda i: (0, 0, 0)),
                  _once((nj, f, d), lambda i: (0, 0, 0)),
                  pl.BlockSpec((tm, d), lambda i: (rev(i), 0)),
                  _once((1, d), lambda i: (0, 0))],
        out_specs=[pl.BlockSpec((nj, tm, f), lambda i: (0, rev(i), 0)),
                   pl.BlockSpec((nj, HALO, f), lambda i: (0, 0, 0)),
                   pl.BlockSpec((tm, d), lambda i: (rev(i), 0)),
                   pl.BlockSpec((HALO, d), lambda i: (0, 0))],
        out_shape=[jax.ShapeDtypeStruct((nj, t, f), BF16), jax.ShapeDtypeStruct((nj, HALO, f), F32),
                   jax.ShapeDtypeStruct((t, d), F32), jax.ShapeDtypeStruct((HALO, d), F32)],
        scratch_shapes=[pltpu.VMEM((nj, HALO, f), F32)],
        compiler_params=_params(1),
    )(dx2, up0, upc, wd, cw, wup, x1, g)


def mixer_bwd(dx1, proj, wout, cw, lng, lnb, ws, wst, bias, *, seq, name, tm):
    t, d = dx1.shape
    tm = min(tm, seq)
    tiles_per_seq = seq // tm
    nt = t // tm
    gd = d // N_GROUPS

    def body(dx_ref, p_ref, cprev_ref, xiprev_ref, wout_ref, cw_ref, lng_ref, lnb_ref, ws_ref, wst_ref, bias_ref,
             dp_ref, dcw_ref, dln_ref, dws_ref, dbs_ref, mixed_scr, dvn_scr, carry_ref, dbs_acc):
        i = pl.program_id(0)
        tile = nt - 1 - i

        @pl.when(i == 0)
        def _():
            dcw_ref[...] = jnp.zeros_like(dcw_ref)
            dln_ref[...] = jnp.zeros_like(dln_ref)
            dws_ref[...] = jnp.zeros_like(dws_ref)
            dbs_acc[...] = jnp.zeros_like(dbs_acc)
            carry_ref[...] = jnp.zeros_like(carry_ref)

        keep_prev = jnp.where(tile % tiles_per_seq == 0, 0.0, 1.0)
        keep_next = jnp.where(tile % tiles_per_seq == tiles_per_seq - 1, 0.0, 1.0)
        cw = cw_ref[...]
        lng = lng_ref[...]
        f = _mixer_forward(p_ref, cprev_ref[...] * keep_prev, xiprev_ref[...], cw, lng, lnb_ref[...],
                           ws_ref, bias_ref, mixed_scr, d)
        dmerged = lax.dot_general(dx_ref[...].astype(BF16), wout_ref[...], NT_DIMS, preferred_element_type=F32)
        sa, sb = f["sa"], f["sb"]
        dp_ref[:, 5 * d:6 * d] = (dmerged * f["ya"] * (sa * (1.0 - sa))).astype(BF16)
        dp_ref[:, 6 * d:7 * d] = (dmerged * f["yb"] * (sb * (1.0 - sb))).astype(BF16)
        dya = dmerged * sa
        dyb = dmerged * sb
        dp_ref[:, 0:d] = (dya * f["conv"]).astype(BF16)
        dconv = dya * f["b"]
        dcw_ref[0:1, :] += _colsum(dconv * f["s2"])
        dcw_ref[1:2, :] += _colsum(dconv * f["s1"])
        dcw_ref[2:3, :] += _colsum(dconv * f["cx"])
        next8 = carry_ref[...] * keep_next
        carry_ref[...] = dconv[:HALO]
        dcx, _, _ = _conv_bwd(dconv, next8, cw)
        dp_ref[:, d:2 * d] = (dcx * f["xi"]).astype(BF16)
        dp_ref[:, 2 * d:3 * d] = (dcx * f["c"]).astype(BF16)
        dp_ref[:, 3 * d:4 * d] = (dyb * f["mixed"]).astype(BF16)
        dmixed = dyb * f["u"]
        dmb = dmixed.astype(BF16)
        vnb = f["vnb"]
        tril = f["tril"]
        triu = (lax.broadcasted_iota(jnp.int32, (CHUNK, CHUNK), 0)
                <= lax.broadcasted_iota(jnp.int32, (CHUNK, CHUNK), 1))
        dbs_tile = dmixed[0:CHUNK]
        for n in range(1, tm // CHUNK):
            dbs_tile = dbs_tile + dmixed[n * CHUNK:(n + 1) * CHUNK]
        dbs_acc[...] += dbs_tile
        for g in range(N_GROUPS):
            wmt = jnp.where(triu, wst_ref[g], 0.0).astype(BF16)
            cols = slice(g * gd, (g + 1) * gd)
            dw = jnp.zeros((CHUNK, CHUNK), F32)
            for n in range(tm // CHUNK):
                rows = slice(n * CHUNK, (n + 1) * CHUNK)
                dvn_scr[rows, cols] = jnp.dot(wmt, dmb[rows, cols], preferred_element_type=F32)
                dw = dw + lax.dot_general(dmb[rows, cols], vnb[rows, cols], NT_DIMS, preferred_element_type=F32)
            dws_ref[g] += jnp.where(tril, dw, 0.0)
        dvn = dvn_scr[...]
        vhat = f["vhat"]
        dln_ref[0:1, :] += _colsum(dvn * vhat)
        dln_ref[1:2, :] += _colsum(dvn)
        dvh = dvn * lng
        dv = f["rstd"] * (dvh - jnp.mean(dvh, axis=-1, keepdims=True)
                          - vhat * jnp.mean(dvh * vhat, axis=-1, keepdims=True))
        dp_ref[:, 4 * d:5 * d] = dv.astype(BF16)

        @pl.when(i == nt - 1)
        def _():
            for g in range(N_GROUPS):
                cols = slice(g * gd, (g + 1) * gd)
                s = jnp.sum(dbs_acc[:, cols], axis=1, keepdims=True)
                dbs_ref[:, cols] = jnp.broadcast_to(s, (CHUNK, gd))

    rev = lambda i: nt - 1 - i

    def halo(col):
        return pl.BlockSpec((HALO, d), lambda i: (jnp.maximum(rev(i) * (tm // HALO) - 1, 0), col))

    const2 = lambda i: (0, 0)
    const3 = lambda i: (0, 0, 0)
    row = lambda i: (rev(i), 0)
    return pl.pallas_call(
        body, name=name, grid=(nt,),
        in_specs=[pl.BlockSpec((tm, d), row),
                  pl.BlockSpec((tm, 7 * d), row),
                  halo(1), halo(2),
                  _once((d, d), const2),
                  _once((HALO, d), const2),
                  _once((1, d), const2),
                  _once((1, d), const2),
                  _once((N_GROUPS, CHUNK, CHUNK), const3),
                  _once((N_GROUPS, CHUNK, CHUNK), const3),
                  _once((CHUNK, d), const2)],
        out_specs=[pl.BlockSpec((tm, 7 * d), row),
                   pl.BlockSpec((HALO, d), const2),
                   pl.BlockSpec((HALO, d), const2),
                   pl.BlockSpec((N_GROUPS, CHUNK, CHUNK), const3),
                   pl.BlockSpec((CHUNK, d), const2)],
        out_shape=[jax.ShapeDtypeStruct((t, 7 * d), BF16),
                   jax.ShapeDtypeStruct((HALO, d), F32),
                   jax.ShapeDtypeStruct((HALO, d), F32),
                   jax.ShapeDtypeStruct((N_GROUPS, CHUNK, CHUNK), F32),
                   jax.ShapeDtypeStruct((CHUNK, d), F32)],
        scratch_shapes=[pltpu.VMEM((tm, d), F32), pltpu.VMEM((tm, d), F32),
                        pltpu.VMEM((HALO, d), F32), pltpu.VMEM((CHUNK, d), F32)],
        compiler_params=_params(1),
    )(dx1, proj, proj, proj, wout, cw, lng, lnb, ws, wst, bias)


def dgrad_rms(dy, w, x, g, res, *, name, tm):
    t, d = x.shape
    n = w.shape[2]
    w = w.reshape(w.shape[0] // 2, 2, d, n)
    nj = w.shape[0]
    tm = min(tm, t)

    def body(dy_ref, w_ref, x_ref, g_ref, res_ref, dx_ref, dg_ref, acc_ref):
        i, j = pl.program_id(0), pl.program_id(1)

        @pl.when((i == 0) & (j == 0))
        def _():
            dg_ref[...] = jnp.zeros_like(dg_ref)

        pair = jnp.concatenate([w_ref[0], w_ref[1]], axis=1)
        part = lax.dot_general(dy_ref[...], pair, NT_DIMS, preferred_element_type=F32)

        @pl.when(j == 0)
        def _():
            acc_ref[...] = part

        @pl.when(j > 0)
        def _():
            acc_ref[...] += part

        @pl.when(j == nj - 1)
        def _():
            dx, dg = _rms_bwd(acc_ref[...], x_ref[...], g_ref[...])
            dx_ref[...] = res_ref[...] + dx
            dg_ref[0:1, :] += dg

    return pl.pallas_call(
        body, name=name, grid=(t // tm, nj),
        in_specs=[pl.BlockSpec((tm, 2 * n), lambda i, j: (i, j)),
                  pl.BlockSpec((None, 2, d, n), lambda i, j: (j, 0, 0, 0)),
                  pl.BlockSpec((tm, d), lambda i, j: (i, 0)),
                  pl.BlockSpec((1, d), lambda i, j: (0, 0)),
                  pl.BlockSpec((tm, d), lambda i, j: (i, 0))],
        out_specs=[pl.BlockSpec((tm, d), lambda i, j: (i, 0)), pl.BlockSpec((HALO, d), lambda i, j: (0, 0))],
        out_shape=[jax.ShapeDtypeStruct((t, d), F32), jax.ShapeDtypeStruct((HALO, d), F32)],
        scratch_shapes=[pltpu.VMEM((tm, d), F32)],
        compiler_params=_params(2),
    )(dy, w, x, g, res)


def wgrad(a, b, *, nj, a_mode, b_mode, name, tm, split=1):
    def describe(arr, mode):
        if mode == "full":
            return arr.shape[0], arr.shape[1], pl.BlockSpec((tm_, arr.shape[1]), lambda j, s: (s, 0))
        if mode == "cols":
            c = arr.shape[1] // nj
            return arr.shape[0], c, pl.BlockSpec((tm_, c), lambda j, s: (s, j))
        return arr.shape[1], arr.shape[2], pl.BlockSpec((None, tm_, arr.shape[2]), lambda j, s: (j, s, 0))

    t = a.shape[0] if a_mode != "lead" else a.shape[1]
    tm_ = min(tm, t)
    _, k, a_spec = describe(a, a_mode)
    _, n, b_spec = describe(b, b_mode)

    ns = t // tm_
    nc = n // split

    def body(a_ref, b_ref, o_ref, acc_ref):
        s = pl.program_id(1)
        part = lax.dot_general(a_ref[...], b_ref[...], TN_DIMS, preferred_element_type=F32)

        def finish(total):
            for q in range(split):
                o_ref[q] = total[:, q * nc:(q + 1) * nc].astype(BF16)

        if ns == 1:
            finish(part)
            return

        @pl.when(s == 0)
        def _():
            acc_ref[...] = part

        @pl.when((s > 0) & (s < ns - 1))
        def _():
            acc_ref[...] += part

        @pl.when(s == ns - 1)
        def _():
            finish(acc_ref[...] + part)

    return pl.pallas_call(
        body, name=name, grid=(nj, ns),
        in_specs=[a_spec, b_spec],
        out_specs=pl.BlockSpec((split, k, nc), lambda j, s: (j, 0, 0)),
        out_shape=jax.ShapeDtypeStruct((nj * split, k, nc), BF16),
        scratch_shapes=[pltpu.VMEM((k, n), F32)],
        compiler_params=_params(2),
    )(a, b)


def _adamw_math(w, g, m, v):
    m = ADAM_B1 * m + (1.0 - ADAM_B1) * g
    v = ADAM_B2 * v + (1.0 - ADAM_B2) * (g * g)
    m_hat = m / (1.0 - ADAM_B1 ** ADAM_STEP)
    v_hat = v / (1.0 - ADAM_B2 ** ADAM_STEP)
    delta = -ADAM_LR * (m_hat / (jnp.sqrt(v_hat) + ADAM_EPS) + ADAM_WD * w)
    return delta, m, v


def _row_tile(rows, at_most):
    if rows <= at_most:
        return rows
    return max(k for k in range(16, at_most + 1, 16) if rows % k == 0)


def _sum_in_device_order(ref):
    total = ref[0]
    for s in range(1, N_DEV):
        total = total + ref[s]
    return total


def adamw_sharded(me, own0, recv0, own1, recv1, w, m, v, *, name, tr):
    _, r, c = w.shape
    tr = _row_tile(r, tr)
    ni = r // tr

    def body(me_ref, o0_ref, r0_ref, o1_ref, r1_ref, w_ref, m_ref, v_ref, g_ref, d_ref, nm_ref, nv_ref):
        def finish(own_ref, recv_ref):
            g = None
            for s in range(N_DEV):
                term = jnp.where(me_ref[0] == s, own_ref[...], recv_ref[s]).astype(F32)
                g = term if g is None else g + term
            delta, nm, nv = _adamw_math(w_ref[...], g, m_ref[...], v_ref[...])
            g_ref[...] = g
            d_ref[...] = delta
            nm_ref[...] = nm
            nv_ref[...] = nv

        @pl.when(pl.program_id(0) == 0)
        def _():
            finish(o0_ref, r0_ref)

        @pl.when(pl.program_id(0) == 1)
        def _():
            finish(o1_ref, r1_ref)

    row0 = lambda l, i: i * (1 - l) + (ni - 1) * l
    row1 = lambda l, i: i * l
    lay = pl.BlockSpec((None, tr, c), lambda l, i, me_ref: (l, i, 0))
    grid_spec = pltpu.PrefetchScalarGridSpec(
        num_scalar_prefetch=1, grid=(2, ni),
        in_specs=[pl.BlockSpec((None, tr, c), lambda l, i, me_ref: (me_ref[0], row0(l, i), 0)),
                  pl.BlockSpec((N_DEV, tr, c), lambda l, i, me_ref: (0, row0(l, i), 0)),
                  pl.BlockSpec((None, tr, c), lambda l, i, me_ref: (me_ref[0], row1(l, i), 0)),
                  pl.BlockSpec((N_DEV, tr, c), lambda l, i, me_ref: (0, row1(l, i), 0)),
                  lay, lay, lay],
        out_specs=[lay, lay, lay, lay])
    return pl.pallas_call(
        body, name=name, grid_spec=grid_spec,
        out_shape=[jax.ShapeDtypeStruct(w.shape, F32)] * 4,
        compiler_params=_params(2),
    )(me, own0, recv0, own1, recv1, w, m, v)


def adamw_small(g, w, m, v, *, name):
    def body(g_ref, w_ref, m_ref, v_ref, d_ref, nm_ref, nv_ref):
        delta, nm, nv = _adamw_math(w_ref[...], g_ref[...], m_ref[...], v_ref[...])
        d_ref[...] = delta
        nm_ref[...] = nm
        nv_ref[...] = nv

    return pl.pallas_call(
        body, name=name,
        out_shape=[jax.ShapeDtypeStruct(w.shape, F32)] * 3,
        compiler_params=pltpu.CompilerParams(vmem_limit_bytes=VMEM_LIMIT_BYTES),
    )(g, w, m, v)


def sum_devices(parts, *, name, tr):
    _, r, c = parts.shape
    tr = min(tr, r)

    def body(p_ref, o_ref):
        o_ref[...] = _sum_in_device_order(p_ref)

    return pl.pallas_call(
        body, name=name, grid=(r // tr,),
        in_specs=[pl.BlockSpec((N_DEV, tr, c), lambda i: (0, i, 0))],
        out_specs=pl.BlockSpec((tr, c), lambda i: (i, 0)),
        out_shape=jax.ShapeDtypeStruct((r, c), F32),
        compiler_params=_params(1),
    )(parts)


def _my_place():
    return lax.axis_index("x"), lax.axis_index("y"), lax.axis_index("c")


def all_gather(arrays, *, name):
    n = len(arrays)

    def body(*refs):
        ins, outs = refs[:n], refs[n:2 * n]
        send_sems, recv_sems, local_sems = refs[2 * n:]
        x, y, c = _my_place()
        me, sibling = (x, y, c), (x, y, 1 - c)
        chips = [(1 - x, y), (x, 1 - y), (1 - x, 1 - y)]
        waits = []
        for a in range(n):
            def slot(place, a=a):
                px, py, pc = place
                return outs[a].at[4 * px + 2 * py + pc]

            def copy(k, block, to, src=None, a=a, slot=slot):
                return pltpu.make_async_remote_copy(
                    src_ref=slot(block) if src is None else src, dst_ref=slot(block),
                    send_sem=send_sems.at[a, k], recv_sem=recv_sems.at[a, k],
                    device_id=to, device_id_type=MESH)

            mine = pltpu.make_async_copy(ins[a], slot(me), local_sems.at[a])
            mine.start()
            first = [copy(0, me, sibling, src=ins[a])]
            first += [copy(1 + j, me, (*chip, c), src=ins[a]) for j, chip in enumerate(chips)]
            for cp in first:
                cp.start()
            waits.append((copy, mine, first))
        sends = []
        for a in range(n):
            copy, mine, first = waits[a]
            passed = [copy(4 + j, (*chip, c), sibling) for j, chip in enumerate(chips)]
            for j, chip in enumerate(chips):
                copy(1 + j, (*chip, c), me).wait_recv()
                passed[j].start()
            sends.append(first + passed)
        for a in range(n):
            copy, mine, first = waits[a]
            copy(0, sibling, me).wait_recv()
            for j, chip in enumerate(chips):
                copy(4 + j, (*chip, 1 - c), me).wait_recv()
            for cp in sends[a]:
                cp.wait_send()
            mine.wait()

    return pl.pallas_call(
        body, name=name,
        in_specs=[ANY] * n, out_specs=[ANY] * n,
        out_shape=[jax.ShapeDtypeStruct((N_DEV,) + a.shape, a.dtype) for a in arrays],
        scratch_shapes=[pltpu.SemaphoreType.DMA((n, 7)), pltpu.SemaphoreType.DMA((n, 7)),
                        pltpu.SemaphoreType.DMA((n,))],
        compiler_params=pltpu.CompilerParams(has_side_effects=True),
    )(*arrays)


def _peer_place(r, x, y, c):
    fx, fy, fc = (r >> 2) & 1, (r >> 1) & 1, r & 1
    return (1 - x if fx else x, 1 - y if fy else y, 1 - c if fc else c)


def own_slot(me, w, layer, dtype, *, name, tr):
    _, r, c = w.shape
    tr = _row_tile(r, tr)

    def body(me_ref, w_ref, o_ref):
        o_ref[...] = w_ref[...].astype(dtype)

    grid_spec = pltpu.PrefetchScalarGridSpec(
        num_scalar_prefetch=1, grid=(r // tr,),
        in_specs=[pl.BlockSpec((None, tr, c), lambda i, me_ref: (layer, i, 0))],
        out_specs=pl.BlockSpec((None, tr, c), lambda i, me_ref: (me_ref[0], i, 0)))
    return pl.pallas_call(
        body, name=name, grid_spec=grid_spec,
        out_shape=jax.ShapeDtypeStruct((N_DEV, r, c), dtype),
        compiler_params=_params(1),
    )(me, w)


EXCHANGES = {
    "scatter": [(0, r) for r in range(1, N_DEV)],
    "gather": [(0, r) for r in range(1, N_DEV)],
    "gather_chips": [(0, r) for r in (1, 2, 4, 6)],
    "gather_forward": [(q, 1) for q in (2, 4, 6)],
}


def _split_copy(k, entry, src, land, send_sem, recv_sem, arriving):
    slot, peer = entry
    x, y, c = _my_place()

    def index(relation):
        px, py, pc = _peer_place(relation, x, y, c)
        return 4 * px + 2 * py + pc

    return pltpu.make_async_remote_copy(
        src_ref=land.at[index(slot)] if src is None else src.at[index(peer)],
        dst_ref=land.at[index(slot ^ peer if arriving else slot)],
        send_sem=send_sem.at[k], recv_sem=recv_sem.at[k],
        device_id=_peer_place(peer, x, y, c), device_id_type=MESH)


def start_copies(srcs, lands, *, mode, name, after=None):
    n = len(lands)
    entries = EXCHANGES[mode]
    bufs = (list(srcs) if srcs is not None else []) + list(lands)
    nb = len(bufs)

    def body(*refs):
        src = refs[:n] if srcs is not None else [None] * n
        land = refs[nb - n:nb]
        outs = refs[nb + len(extra):]
        send_sems, recv_sems = outs[:n], outs[n:2 * n]
        token = outs[2 * n + nb]
        for a in range(n):
            for k, entry in enumerate(entries):
                _split_copy(k, entry, src[a], land[a], send_sems[a], recv_sems[a], False).start()
        token[...] = jnp.zeros_like(token)

    extra = [] if after is None else [after]
    outs = pl.pallas_call(
        body, name=name,
        in_specs=[HBM_SPEC] * nb + [ANY] * len(extra),
        out_specs=[SEM_SPEC] * (2 * n) + [HBM_SPEC] * nb + [pl.BlockSpec(memory_space=pltpu.VMEM)],
        out_shape=([pltpu.SemaphoreType.DMA((len(entries),))] * (2 * n)
                   + [pltpu.HBM(a.shape, a.dtype) for a in bufs]
                   + [jax.ShapeDtypeStruct((8, 128), F32)]),
        input_output_aliases={i: 2 * n + i for i in range(nb)},
        compiler_params=pltpu.CompilerParams(has_side_effects=DATAFLOW),
    )(*[pltpu.with_memory_space_constraint(a, pltpu.HBM) for a in bufs], *extra)
    thru = list(outs[2 * n:2 * n + nb])
    return dict(send=outs[:n], recv=outs[n:2 * n], src=thru[:n] if srcs is not None else None, land=thru[nb - n:],
                token=outs[2 * n + nb], mode=mode)


def finish_copies(started, which, after, *, name):
    n = len(which)
    entries = EXCHANGES[started["mode"]]
    has_src = started["src"] is not None
    bufs = ([started["src"][i] for i in which] if has_src else []) + [started["land"][i] for i in which]
    nb = len(bufs)

    def body(*refs):
        src = refs[:n] if has_src else [None] * n
        land = refs[nb - n:nb]
        send_sems, recv_sems = refs[nb:nb + n], refs[nb + n:nb + 2 * n]
        for a in range(n):
            for k, entry in enumerate(entries):
                cp = _split_copy(k, entry, src[a], land[a], send_sems[a], recv_sems[a], True)
                cp.wait_send()
                cp.wait_recv()

    outs = pl.pallas_call(
        body, name=name,
        in_specs=[HBM_SPEC] * nb + [SEM_SPEC] * (2 * n) + [ANY],
        out_specs=[HBM_SPEC] * nb,
        out_shape=[pltpu.HBM(a.shape, a.dtype) for a in bufs],
        input_output_aliases={i: i for i in range(nb)},
        compiler_params=pltpu.CompilerParams(has_side_effects=DATAFLOW),
    )(*bufs, *[started["send"][i] for i in which], *[started["recv"][i] for i in which], after)
    return (list(outs[:n]) if has_src else None), list(outs[nb - n:])


def _pad_rows(a, rows):
    pad = [(0, 0)] * a.ndim
    pad[-2] = (0, rows - a.shape[-2])
    return jnp.pad(a, pad)


def kernel(x, mix_norm_g, w_in, conv_a_w, ln_v_g, ln_v_b, w_s, b_s, w_out, ffn_norm_g, w_up, conv_ffn_w, w_down, final_norm_g, loss_target, m_mix_norm_g, m_w_in, m_conv_a_w, m_ln_v_g, m_ln_v_b, m_w_s, m_b_s, m_w_out, m_ffn_norm_g, m_w_up, m_conv_ffn_w, m_w_down, m_final_norm_g, v_mix_norm_g, v_w_in, v_conv_a_w, v_ln_v_g, v_ln_v_b, v_w_s, v_b_s, v_w_out, v_ffn_norm_g, v_w_up, v_conv_ffn_w, v_w_down, v_final_norm_g):
    nb, seq, d = x.shape
    t = nb * seq
    depth = w_in.shape[0]
    f = w_up.shape[2]
    me = 4 * lax.axis_index("x") + 2 * lax.axis_index("y") + lax.axis_index("c")
    xt = x.reshape(t, d)
    tgt = loss_target.reshape(t, d)

    conv_pack = jnp.concatenate([_pad_rows(conv_a_w, HALO), _pad_rows(conv_ffn_w, HALO)], axis=-1)
    me_arr = me.astype(jnp.int32).reshape(1)
    w_up_t, m_w_up_t, v_w_up_t = (jnp.swapaxes(a, 1, 2) for a in (w_up, m_w_up, v_w_up))
    zones, slot_of = [], {}
    for l in range(depth):
        for key, w in (("win", w_in), ("conv", None), ("wout", w_out), ("wup", w_up_t), ("wd", w_down)):
            if key == "conv":
                if l == 0:
                    slot_of["conv"] = len(zones)
                    packed = conv_pack.reshape(1, depth * HALO, conv_pack.shape[-1])
                    zones.append(own_slot(me_arr, packed, 0, F32, name="own_slot_conv", tr=256))
                continue
            slot_of[key, l] = len(zones)
            zones.append(own_slot(me_arr, w, l, BF16, name=f"own_slot_{key}_{l}", tr=256))
    first = [slot_of["win", 0], slot_of["wout", 0], slot_of["conv"]]
    rest = [i for i in range(len(zones)) if i not in first]
    to_chips = start_copies(None, [zones[i] for i in first], mode="gather_chips", name="gather_first_chips")
    gathering = start_copies(None, [zones[i] for i in rest], mode="gather", name="gather_start", after=to_chips["token"])
    _, at_chips = finish_copies(to_chips, [0, 1, 2], gathering["token"], name="wait_first_chips")
    to_sibling = start_copies(None, at_chips, mode="gather_forward", name="gather_first_forward")

    def gathered(keys, after, name):
        return finish_copies(gathering, [rest.index(slot_of[k]) for k in keys], after, name=name)[1]

    saved, layers = [], []
    cur = xt
    for l in range(depth):
        p = dict(mix_g=mix_norm_g[l][None], ffn_g=ffn_norm_g[l][None], lng=ln_v_g[l][None], lnb=ln_v_b[l][None],
                 ws=w_s[l], wst=jnp.swapaxes(w_s[l], 1, 2),
                 bias=jnp.repeat(b_s[l].T, d // N_GROUPS, axis=1))
        if l == 0:
            _, (p["win"], wout_g, conv_g) = finish_copies(to_sibling, [0, 1, 2], to_sibling["token"],
                                                          name=f"wait_w_mixer_{l}")
            conv_g = conv_g.reshape(N_DEV, depth, HALO, -1)
            ca = conv_g.shape[-1] - f
        else:
            p["win"], wout_g = gathered([("win", l), ("wout", l)], after, f"wait_w_mixer_{l}")
        p["wout"] = wout_g.reshape(d, d)
        p["cw_a"] = jnp.transpose(conv_g[:, l, :, :ca], (1, 0, 2)).reshape(HALO, d)
        p["cw_f"] = conv_g[:, l, :, ca:]
        h, proj, merged, x1 = mixer_fwd(cur, p["mix_g"], p["win"], p["wout"], p["cw_a"], p["lng"], p["lnb"], p["ws"],
                                        p["bias"], seq=seq, name=f"mixer_fwd_{l}", tm=256)
        p["wup"], wd_g = gathered([("wup", l), ("wd", l)], merged, f"wait_w_ffn_{l}")
        p["wd"] = wd_g.reshape(N_DEV // 2, 2 * wd_g.shape[1], d)
        h2, up0, upc, act, x2 = ffn_fwd(x1, p["ffn_g"], p["wup"], p["wd"], p["cw_f"],
                                        seq=seq, name=f"ffn_fwd_{l}", tm=256)
        saved.append(dict(x0=cur, h=h, proj=proj, merged=merged, x1=x1, h2=h2, up0=up0, upc=upc, act=act))
        layers.append(p)
        cur, after = x2, act
    dx, d_final_g, loss_tile = final_loss(cur, final_norm_g[None], tgt, name="final_loss", tm=512)

    def exchange(parts, name):
        return start_copies(parts, [lax.empty(a.shape, a.dtype) for a in parts], mode="scatter", name=name)

    def tied(g, started):
        return g + started["token"][0:1, 0:1]

    part = [None] * depth
    for l in reversed(range(depth)):
        p, s = layers[l], saved[l]
        dup0, dcw_f, dx1, d_ffn_g = ffn_bwd(dx, s["up0"], s["upc"], p["wd"], p["cw_f"], p["wup"], s["x1"], p["ffn_g"],
                                            seq=seq, name=f"ffn_bwd_{l}", tm=256)
        g_wd = wgrad(s["act"], dx, nj=N_DEV // 2, a_mode="lead", b_mode="full", name=f"wgrad_down_{l}", tm=2048)
        g_wup = wgrad(dup0, s["h2"], nj=N_DEV, a_mode="lead", b_mode="full", name=f"wgrad_up_{l}", tm=2048)
        ffn_ex = exchange([g_wd.reshape(N_DEV, g_wd.shape[1] // 2, d), g_wup], f"exchange_ffn_{l}")
        dproj, dcw_a, dln, dws, dbs = mixer_bwd(dx1, s["proj"], p["wout"], tied(p["cw_a"], ffn_ex), p["lng"], p["lnb"],
                                                 p["ws"], p["wst"], p["bias"], seq=seq, name=f"mixer_bwd_{l}", tm=256)
        g_wout = wgrad(s["merged"], dx1, nj=1, a_mode="full", b_mode="full", name=f"wgrad_out_{l}", tm=2048)
        g_win = wgrad(s["h"], dproj, nj=N_DEV // 2, a_mode="full", b_mode="cols", name=f"wgrad_in_{l}", tm=2048, split=2)
        mix_ex = exchange([g_wout.reshape(N_DEV, d // N_DEV, d), g_win], f"exchange_mix_{l}")
        dx, d_mix_g = dgrad_rms(dproj, p["win"], s["x0"], tied(p["mix_g"], mix_ex), dx1, name=f"dgrad_in_{l}", tm=1024)
        part[l] = dict(
            ffn_ex=ffn_ex, mix_ex=mix_ex,
            small=jnp.concatenate([
                dws.reshape(N_GROUPS * CHUNK * CHUNK // d, d),
                d_mix_g[0:1], d_ffn_g[0:1], dln[0:2], dcw_a[0:3],
                dbs[:, ::d // N_GROUPS].T.reshape(1, d)], axis=0),
            cw_f=dcw_f.reshape(N_DEV * HALO, f))
    grad_x = dx.reshape(nb, seq, d)

    loss_row = jnp.zeros((1, d), F32).at[0, 0].set(loss_tile[0, 0])
    small = jnp.concatenate([part[l]["small"] for l in range(depth)] + [d_final_g[0:1], loss_row], axis=0)
    small = _pad_rows(small, -(-small.shape[0] // 8) * 8)
    cwf = jnp.concatenate([part[l]["cw_f"] for l in range(depth)], axis=0)
    small_all, cwf_all = all_gather([small, cwf], name="gather_small_grads")
    small_sum = sum_devices(small_all, name="sum_small", tr=512)
    cwf_sum = sum_devices(cwf_all, name="sum_conv_ffn", tr=512)


    rows_ws = N_GROUPS * CHUNK * CHUNK // d
    per_layer = rows_ws + 8
    def small_of(l, a, b):
        return small_sum[l * per_layer + rows_ws + a:l * per_layer + rows_ws + b]
    g_ws = jnp.stack([small_sum[l * per_layer:l * per_layer + rows_ws].reshape(N_GROUPS, CHUNK, CHUNK)
                      for l in range(depth)])
    g_mix = jnp.concatenate([small_of(l, 0, 1) for l in range(depth)])
    g_ffn = jnp.concatenate([small_of(l, 1, 2) for l in range(depth)])
    g_lng = jnp.concatenate([small_of(l, 2, 3) for l in range(depth)])
    g_lnb = jnp.concatenate([small_of(l, 3, 4) for l in range(depth)])
    g_cwa_full = jnp.stack([small_of(l, 4, 7) for l in range(depth)])
    g_cwa = lax.dynamic_slice_in_dim(g_cwa_full, me * ca, ca, axis=2)
    g_bs = jnp.stack([small_of(l, 7, 8).reshape(N_GROUPS, CHUNK) for l in range(depth)])
    g_final = small_sum[depth * per_layer]
    loss = small_sum[depth * per_layer + 1, 0]
    cwf_sum = cwf_sum.reshape(depth, N_DEV, HALO, f)
    g_cwf = lax.dynamic_index_in_dim(cwf_sum, me, axis=1, keepdims=False)[:, :3]

    own, recv = {}, {}

    def arrived(l, ex, keys, after):
        srcs, lands = finish_copies(part[l][ex], [0, 1], after, name=f"wait_{ex}_{l}")
        for k, key in enumerate(keys):
            own[key, l], recv[key, l] = srcs[k], lands[k]
        return lands[1]

    def big(key, w, m, v, name):
        return adamw_sharded(me_arr, own[key, 0], recv[key, 0], own[key, 1], recv[key, 1], w, m, v, name=name, tr=256)

    after = grad_x
    for l in reversed(range(depth)):
        after = arrived(l, "ffn_ex", ("wd", "wup"), after)
        if l > 0:
            after = arrived(l, "mix_ex", ("wout", "win"), after)
    u_wd = big("wd", w_down, m_w_down, v_w_down, "adamw_w_down")
    u_wup = tuple(jnp.swapaxes(a, 1, 2) for a in big("wup", w_up_t, m_w_up_t, v_w_up_t, "adamw_w_up"))
    arrived(0, "mix_ex", ("wout", "win"), u_wup[1])
    u_wout = big("wout", w_out, m_w_out, v_w_out, "adamw_w_out")
    u_win = big("win", w_in, m_w_in, v_w_in, "adamw_w_in")

    def small_update(g, w, m, v, name):
        shape = w.shape
        two_d = (-1, shape[-1]) if w.ndim > 1 else (1, shape[0])
        out = adamw_small(g.reshape(two_d), w.reshape(two_d), m.reshape(two_d), v.reshape(two_d), name=name)
        return (g.reshape(shape),) + tuple(o.reshape(shape) for o in out)

    u_mix = small_update(g_mix, mix_norm_g, m_mix_norm_g, v_mix_norm_g, "adamw_mix_norm_g")
    u_cwa = small_update(g_cwa, conv_a_w, m_conv_a_w, v_conv_a_w, "adamw_conv_a_w")
    u_lng = small_update(g_lng, ln_v_g, m_ln_v_g, v_ln_v_g, "adamw_ln_v_g")
    u_lnb = small_update(g_lnb, ln_v_b, m_ln_v_b, v_ln_v_b, "adamw_ln_v_b")
    u_ws = small_update(g_ws, w_s, m_w_s, v_w_s, "adamw_w_s")
    u_bs = small_update(g_bs, b_s, m_b_s, v_b_s, "adamw_b_s")
    u_ffn = small_update(g_ffn, ffn_norm_g, m_ffn_norm_g, v_ffn_norm_g, "adamw_ffn_norm_g")
    u_cwf = small_update(g_cwf, conv_ffn_w, m_conv_ffn_w, v_conv_ffn_w, "adamw_conv_ffn_w")
    u_final = small_update(g_final, final_norm_g, m_final_norm_g, v_final_norm_g, "adamw_final_norm_g")

    ordered = [u_mix, u_win, u_cwa, u_lng, u_lnb, u_ws, u_bs, u_wout, u_ffn, u_wup, u_cwf, u_wd, u_final]
    return (loss, grad_x, *[u[0] for u in ordered], *[u[1] for u in ordered],
            *[u[2] for u in ordered], *[u[3] for u in ordered])
```

```python
import functools

import jax
import jax.numpy as jnp
from jax import lax
from jax.experimental import pallas as pl
from jax.experimental.pallas import tpu as pltpu

EPS = 1e-6
CHUNK = 128
N_GROUPS = 8
N_DEV = 8
HALO = 8
ADAM_LR = 0.001
ADAM_B1 = 0.9
ADAM_B2 = 0.999
ADAM_EPS = 1e-08
ADAM_WD = 0.01
ADAM_STEP = 10
VMEM_LIMIT_BYTES = 56 * 1024 * 1024
F32 = jnp.float32
BF16 = jnp.bfloat16
MESH = pl.DeviceIdType.MESH
ANY = pl.BlockSpec(memory_space=pl.ANY)
HBM_SPEC = pl.BlockSpec(memory_space=pltpu.HBM)
SEM_SPEC = pl.BlockSpec(memory_space=pltpu.SEMAPHORE)
DATAFLOW = pltpu.SideEffectType.DATAFLOW_SIDE_EFFECTING
NT_DIMS = (((1,), (1,)), ((), ()))
TN_DIMS = (((0,), (0,)), ((), ()))


def _params(n_grid_axes):
    return pltpu.CompilerParams(dimension_semantics=("arbitrary",) * n_grid_axes,
                                vmem_limit_bytes=VMEM_LIMIT_BYTES)


def _shift_down(cur, prev8, k):
    rolled = pltpu.roll(cur, k, 0)
    prolled = pltpu.roll(prev8, k, 0)
    row = lax.broadcasted_iota(jnp.int32, prev8.shape, 0)
    head = jnp.where(row < k, prolled, rolled[:HALO])
    return jnp.concatenate([head, rolled[HALO:]], axis=0)


def _shift_up(cur, next8, k):
    tm = cur.shape[0]
    rolled = pltpu.roll(cur, tm - k, 0)
    nrolled = pltpu.roll(next8, HALO - k, 0)
    row = lax.broadcasted_iota(jnp.int32, next8.shape, 0)
    tail = jnp.where(row >= HALO - k, nrolled, rolled[tm - HALO:])
    return jnp.concatenate([rolled[:tm - HALO], tail], axis=0)


def _conv_fwd(cur, prev8, cw):
    s1 = _shift_down(cur, prev8, 1)
    s2 = _shift_down(cur, prev8, 2)
    y = s2 * cw[0:1, :] + s1 * cw[1:2, :] + cur * cw[2:3, :]
    return y, s1, s2


def _conv_bwd(d, next8, cw):
    u1 = _shift_up(d, next8, 1)
    u2 = _shift_up(d, next8, 2)
    return d * cw[2:3, :] + u1 * cw[1:2, :] + u2 * cw[0:1, :], u1, u2


def _colsum(a):
    return jnp.sum(a, axis=0, keepdims=True)


def _rms_stats(xv):
    r = lax.rsqrt(jnp.mean(xv * xv, axis=-1, keepdims=True) + EPS)
    return r, xv * r


def _rms_bwd(dh, xv, g):
    r, n = _rms_stats(xv)
    dn = dh * g
    dx = r * (dn - n * jnp.mean(dn * n, axis=-1, keepdims=True))
    return dx, _colsum(dh * n)


def _mixer_forward(p_ref, cprev, xiprev, cw, lng, lnb, ws_ref, bias_ref, mixed_scr, d):
    tm = p_ref.shape[0]
    b = p_ref[:, 0:d]
    c = p_ref[:, d:2 * d]
    xi = p_ref[:, 2 * d:3 * d]
    u = p_ref[:, 3 * d:4 * d]
    v = p_ref[:, 4 * d:5 * d]
    sa = jax.nn.sigmoid(p_ref[:, 5 * d:6 * d])
    sb = jax.nn.sigmoid(p_ref[:, 6 * d:7 * d])
    cx = c * xi
    conv, s1, s2 = _conv_fwd(cx, cprev * xiprev, cw)
    ya = b * conv
    mu = jnp.mean(v, axis=-1, keepdims=True)
    xc = v - mu
    rstd = lax.rsqrt(jnp.mean(xc * xc, axis=-1, keepdims=True) + EPS)
    vhat = xc * rstd
    vnb = (vhat * lng + lnb).astype(BF16)
    tril = (lax.broadcasted_iota(jnp.int32, (CHUNK, CHUNK), 0)
            >= lax.broadcasted_iota(jnp.int32, (CHUNK, CHUNK), 1))
    gd = d // N_GROUPS
    for g in range(N_GROUPS):
        wm = jnp.where(tril, ws_ref[g], 0.0).astype(BF16)
        cols = slice(g * gd, (g + 1) * gd)
        for n in range(tm // CHUNK):
            rows = slice(n * CHUNK, (n + 1) * CHUNK)
            mixed_scr[rows, cols] = (jnp.dot(wm, vnb[rows, cols], preferred_element_type=F32)
                                     + bias_ref[:, cols])
    mixed = mixed_scr[...]
    yb = u * mixed
    merged = sa * ya + sb * yb
    return dict(b=b, c=c, xi=xi, u=u, sa=sa, sb=sb, cx=cx, s1=s1, s2=s2, conv=conv, ya=ya,
                rstd=rstd, vhat=vhat, vnb=vnb, mixed=mixed, yb=yb, merged=merged, tril=tril)


def _once(block_shape, index_map):
    return pl.BlockSpec(block_shape, index_map, pipeline_mode=pl.Buffered(1))


def mixer_fwd(x, g, win, wout, cw, lng, lnb, ws, bias, *, seq, name, tm):
    t, d = x.shape
    nj, _, n = win.shape
    tm = min(tm, seq)
    tiles_per_seq = seq // tm

    def body(x_ref, g_ref, win_ref, wout_ref, cw_ref, lng_ref, lnb_ref, ws_ref, bias_ref,
             h_ref, p_ref, merged_ref, x1_ref, mixed_scr, carry_ref):
        @pl.when(pl.program_id(0) == 0)
        def _():
            carry_ref[...] = jnp.zeros_like(carry_ref)

        keep = jnp.where(pl.program_id(0) % tiles_per_seq == 0, 0.0, 1.0)
        xv = x_ref[...]
        _, nrm = _rms_stats(xv)
        hb = (nrm * g_ref[...]).astype(BF16)
        h_ref[...] = hb
        for j in range(0, nj, 2):
            pair = jnp.concatenate([win_ref[j], win_ref[j + 1]], axis=1)
            p_ref[:, j * n:(j + 2) * n] = jnp.dot(hb, pair, preferred_element_type=F32)
        f = _mixer_forward(p_ref, carry_ref[...] * keep, 1.0, cw_ref[...], lng_ref[...],
                           lnb_ref[...], ws_ref, bias_ref, mixed_scr, d)
        carry_ref[...] = f["cx"][tm - HALO:]
        mb = f["merged"].astype(BF16)
        merged_ref[...] = mb
        x1_ref[...] = xv + jnp.dot(mb, wout_ref[...], preferred_element_type=F32)

    const2 = lambda i: (0, 0)
    const3 = lambda i: (0, 0, 0)
    row = lambda i: (i, 0)
    return pl.pallas_call(
        body, name=name, grid=(t // tm,),
        in_specs=[pl.BlockSpec((tm, d), row),
                  _once((1, d), const2),
                  _once((nj, d, n), const3),
                  _once((d, d), const2),
                  _once((HALO, d), const2),
                  _once((1, d), const2),
                  _once((1, d), const2),
                  _once((N_GROUPS, CHUNK, CHUNK), const3),
                  _once((CHUNK, d), const2)],
        out_specs=[pl.BlockSpec((tm, d), row), pl.BlockSpec((tm, nj * n), row),
                   pl.BlockSpec((tm, d), row), pl.BlockSpec((tm, d), row)],
        out_shape=[jax.ShapeDtypeStruct((t, d), BF16), jax.ShapeDtypeStruct((t, nj * n), F32),
                   jax.ShapeDtypeStruct((t, d), BF16), jax.ShapeDtypeStruct((t, d), F32)],
        scratch_shapes=[pltpu.VMEM((tm, d), F32), pltpu.VMEM((HALO, d), F32)],
        compiler_params=_params(1),
    )(x, g, win, wout, cw, lng, lnb, ws, bias)


def ffn_fwd(x1, g, wup, wd, cw, *, seq, name, tm, head=None):
    t, d = x1.shape
    nj, f, _ = wup.shape
    half = nj // 2
    tm = min(tm, seq)
    tiles_per_seq = seq // tm

    def body(x1_ref, g_ref, wup_ref, wd_ref, cw_ref, *rest):
        if head is None:
            h2_ref, up_ref, upc_ref, act_ref, x2_ref, carry_ref = rest
        else:
            gf_ref, tgt_ref, h2_ref, up_ref, upc_ref, act_ref, x2_ref, dgf_ref, loss_ref, carry_ref = rest

        @pl.when(pl.program_id(0) == 0)
        def _():
            carry_ref[...] = jnp.zeros_like(carry_ref)
            if head is not None:
                dgf_ref[...] = jnp.zeros_like(dgf_ref)
                loss_ref[...] = jnp.zeros_like(loss_ref)

        keep = jnp.where(pl.program_id(0) % tiles_per_seq == 0, 0.0, 1.0)
        xv = x1_ref[...]
        _, nrm = _rms_stats(xv)
        hb = (nrm * g_ref[...]).astype(BF16)
        h2_ref[...] = hb

        for j in range(nj):
            up_ref[j] = lax.dot_general(hb, wup_ref[j], NT_DIMS, preferred_element_type=F32)

        def conv_of(j):
            up0 = up_ref[j]
            y, _, _ = _conv_fwd(up0, carry_ref[j] * keep, cw_ref[j])
            carry_ref[j] = up0[tm - HALO:]
            upc_ref[j] = y.astype(BF16)
            return y

        acc = xv
        for k in range(half):
            a = (jax.nn.silu(conv_of(k)) * conv_of(k + half)).astype(BF16)
            act_ref[k] = a
            acc = acc + jnp.dot(a, wd_ref[k], preferred_element_type=F32)
        if head is None:
            x2_ref[...] = acc
        else:
            gv = gf_ref[...]
            r, n = _rms_stats(acc)
            err = n * gv - tgt_ref[...]
            loss_ref[...] += 0.5 * jnp.sum(jnp.mean(err * err, axis=-1, keepdims=True))
            dy = err * (1.0 / d)
            dn = dy * gv
            x2_ref[...] = r * (dn - n * jnp.mean(dn * n, axis=-1, keepdims=True))
            dgf_ref[0:1, :] += _colsum(dy * n)

    const3 = lambda i: (0, 0, 0)
    row = lambda i: (i, 0)
    in_specs = [pl.BlockSpec((tm, d), row), _once((1, d), lambda i: (0, 0)), _once((nj, f, d), const3),
                _once((half, f, d), const3), _once((nj, HALO, f), const3)]
    out_specs = [pl.BlockSpec((tm, d), row), pl.BlockSpec((nj, tm, f), lambda i: (0, i, 0)),
                 pl.BlockSpec((nj, tm, f), lambda i: (0, i, 0)), pl.BlockSpec((half, tm, f), lambda i: (0, i, 0)),
                 pl.BlockSpec((tm, d), row)]
    out_shape = [jax.ShapeDtypeStruct((t, d), BF16), jax.ShapeDtypeStruct((nj, t, f), F32),
                 jax.ShapeDtypeStruct((nj, t, f), BF16), jax.ShapeDtypeStruct((half, t, f), BF16),
                 jax.ShapeDtypeStruct((t, d), F32)]
    args = [x1, g, wup, wd, cw]
    if head is not None:
        in_specs += [_once((1, d), lambda i: (0, 0)), pl.BlockSpec((tm, d), row)]
        out_specs += [pl.BlockSpec((HALO, d), lambda i: (0, 0)), pl.BlockSpec((8, 128), lambda i: (0, 0))]
        out_shape += [jax.ShapeDtypeStruct((HALO, d), F32), jax.ShapeDtypeStruct((8, 128), F32)]
        args += list(head)
    return pl.pallas_call(
        body, name=name, grid=(t // tm,), in_specs=in_specs, out_specs=out_specs, out_shape=out_shape,
        scratch_shapes=[pltpu.VMEM((nj, HALO, f), F32)],
        compiler_params=_params(1),
    )(*args)


def ffn_bwd(dx2, up0, upc, wd, cw, wup, x1, g, *, seq, name, tm):
    t, d = dx2.shape
    nj, _, f = up0.shape
    half = nj // 2
    tm = min(tm, seq)
    tiles_per_seq = seq // tm
    nt = t // tm

    def body(dx_ref, up_ref, upc_ref, wd_ref, cw_ref, wup_ref, x1_ref, g_ref,
             dup_ref, dcw_ref, dx1_ref, dg_ref, carry_ref):
        i = pl.program_id(0)
        tile = nt - 1 - i

        @pl.when(i == 0)
        def _():
            dcw_ref[...] = jnp.zeros_like(dcw_ref)
            dg_ref[...] = jnp.zeros_like(dg_ref)
            carry_ref[...] = jnp.zeros_like(carry_ref)

        keep_next = jnp.where(tile % tiles_per_seq == tiles_per_seq - 1, 0.0, 1.0)
        dx2v = dx_ref[...]
        dxb = dx2v.astype(BF16)
        dh = [jnp.zeros((tm, d), F32)]

        def through_conv(j, dup):
            next8 = carry_ref[j] * keep_next
            carry_ref[j] = dup[:HALO]
            dup0, u1, u2 = _conv_bwd(dup, next8, cw_ref[j])
            up0 = up_ref[j]
            dcw_ref[j, 0:1, :] += _colsum(u2 * up0)
            dcw_ref[j, 1:2, :] += _colsum(u1 * up0)
            dcw_ref[j, 2:3, :] += _colsum(dup * up0)
            dup0 = dup0.astype(BF16)
            dup_ref[j] = dup0
            dh[0] = dh[0] + jnp.dot(dup0, wup_ref[j], preferred_element_type=F32)

        dacts = [lax.dot_general(dxb, wd_ref[k], NT_DIMS, preferred_element_type=F32) for k in range(half)]
        for k in range(half):
            gate = upc_ref[k].astype(F32)
            val = upc_ref[k + half].astype(F32)
            dact = dacts[k]
            sg = jax.nn.sigmoid(gate)
            through_conv(k, dact * val * (sg * (1.0 + gate * (1.0 - sg))))
            through_conv(k + half, dact * (gate * sg))

        dx, dg = _rms_bwd(dh[0], x1_ref[...], g_ref[...])
        dx1_ref[...] = dx2v + dx
        dg_ref[0:1, :] += dg

    rev = lambda i: nt - 1 - i
    return pl.pallas_call(
        body, name=name, grid=(nt,),
        in_specs=[pl.BlockSpec((tm, d), lambda i: (rev(i), 0)),
                  pl.BlockSpec((nj, tm, f), lambda i: (0, rev(i), 0)),
                  pl.BlockSpec((nj, tm, f), lambda i: (0, rev(i), 0)),
                  _once((half, f, d), lambda i: (0, 0, 0)),
                  _once((nj, HALO, f), lambda i: (0, 0, 0)),
                  _once((nj, f, d), lambda i: (0, 0, 0)),
                  pl.BlockSpec((tm, d), lambda i: (rev(i), 0)),
                  _once((1, d), lambda i: (0, 0))],
        out_specs=[pl.BlockSpec((nj, tm, f), lambda i: (0, rev(i), 0)),
                   pl.BlockSpec((nj, HALO, f), lambda i: (0, 0, 0)),
                   pl.BlockSpec((tm, d), lambda i: (rev(i), 0)),
                   pl.BlockSpec((HALO, d), lambda i: (0, 0))],
        out_shape=[jax.ShapeDtypeStruct((nj, t, f), BF16), jax.ShapeDtypeStruct((nj, HALO, f), F32),
                   jax.ShapeDtypeStruct((t, d), F32), jax.ShapeDtypeStruct((HALO, d), F32)],
        scratch_shapes=[pltpu.VMEM((nj, HALO, f), F32)],
        compiler_params=_params(1),
    )(dx2, up0, upc, wd, cw, wup, x1, g)


def mixer_bwd(dx1, proj, wout, cw, lng, lnb, ws, wst, bias, *, seq, name, tm):
    t, d = dx1.shape
    tm = min(tm, seq)
    tiles_per_seq = seq // tm
    nt = t // tm
    gd = d // N_GROUPS

    def body(dx_ref, p_ref, cprev_ref, xiprev_ref, wout_ref, cw_ref, lng_ref, lnb_ref, ws_ref, wst_ref, bias_ref,
             dp_ref, dcw_ref, dln_ref, dws_ref, dbs_ref, mixed_scr, dvn_scr, carry_ref, dbs_acc):
        i = pl.program_id(0)
        tile = nt - 1 - i

        @pl.when(i == 0)
        def _():
            dcw_ref[...] = jnp.zeros_like(dcw_ref)
            dln_ref[...] = jnp.zeros_like(dln_ref)
            dws_ref[...] = jnp.zeros_like(dws_ref)
            dbs_acc[...] = jnp.zeros_like(dbs_acc)
            carry_ref[...] = jnp.zeros_like(carry_ref)

        keep_prev = jnp.where(tile % tiles_per_seq == 0, 0.0, 1.0)
        keep_next = jnp.where(tile % tiles_per_seq == tiles_per_seq - 1, 0.0, 1.0)
        cw = cw_ref[...]
        lng = lng_ref[...]
        f = _mixer_forward(p_ref, cprev_ref[...] * keep_prev, xiprev_ref[...], cw, lng, lnb_ref[...],
                           ws_ref, bias_ref, mixed_scr, d)
        dmerged = lax.dot_general(dx_ref[...].astype(BF16), wout_ref[...], NT_DIMS, preferred_element_type=F32)
        sa, sb = f["sa"], f["sb"]
        dp_ref[:, 5 * d:6 * d] = (dmerged * f["ya"] * (sa * (1.0 - sa))).astype(BF16)
        dp_ref[:, 6 * d:7 * d] = (dmerged * f["yb"] * (sb * (1.0 - sb))).astype(BF16)
        dya = dmerged * sa
        dyb = dmerged * sb
        dp_ref[:, 0:d] = (dya * f["conv"]).astype(BF16)
        dconv = dya * f["b"]
        dcw_ref[0:1, :] += _colsum(dconv * f["s2"])
        dcw_ref[1:2, :] += _colsum(dconv * f["s1"])
        dcw_ref[2:3, :] += _colsum(dconv * f["cx"])
        next8 = carry_ref[...] * keep_next
        carry_ref[...] = dconv[:HALO]
        dcx, _, _ = _conv_bwd(dconv, next8, cw)
        dp_ref[:, d:2 * d] = (dcx * f["xi"]).astype(BF16)
        dp_ref[:, 2 * d:3 * d] = (dcx * f["c"]).astype(BF16)
        dp_ref[:, 3 * d:4 * d] = (dyb * f["mixed"]).astype(BF16)
        dmixed = dyb * f["u"]
        dmb = dmixed.astype(BF16)
        vnb = f["vnb"]
        tril = f["tril"]
        triu = (lax.broadcasted_iota(jnp.int32, (CHUNK, CHUNK), 0)
                <= lax.broadcasted_iota(jnp.int32, (CHUNK, CHUNK), 1))
        dbs_tile = dmixed[0:CHUNK]
        for n in range(1, tm // CHUNK):
            dbs_tile = dbs_tile + dmixed[n * CHUNK:(n + 1) * CHUNK]
        dbs_acc[...] += dbs_tile
        for g in range(N_GROUPS):
            wmt = jnp.where(triu, wst_ref[g], 0.0).astype(BF16)
            cols = slice(g * gd, (g + 1) * gd)
            dw = jnp.zeros((CHUNK, CHUNK), F32)
            for n in range(tm // CHUNK):
                rows = slice(n * CHUNK, (n + 1) * CHUNK)
                dvn_scr[rows, cols] = jnp.dot(wmt, dmb[rows, cols], preferred_element_type=F32)
                dw = dw + lax.dot_general(dmb[rows, cols], vnb[rows, cols], NT_DIMS, preferred_element_type=F32)
            dws_ref[g] += jnp.where(tril, dw, 0.0)
        dvn = dvn_scr[...]
        vhat = f["vhat"]
        dln_ref[0:1, :] += _colsum(dvn * vhat)
        dln_ref[1:2, :] += _colsum(dvn)
        dvh = dvn * lng
        dv = f["rstd"] * (dvh - jnp.mean(dvh, axis=-1, keepdims=True)
                          - vhat * jnp.mean(dvh * vhat, axis=-1, keepdims=True))
        dp_ref[:, 4 * d:5 * d] = dv.astype(BF16)

        @pl.when(i == nt - 1)
        def _():
            for g in range(N_GROUPS):
                cols = slice(g * gd, (g + 1) * gd)
                s = jnp.sum(dbs_acc[:, cols], axis=1, keepdims=True)
                dbs_ref[:, cols] = jnp.broadcast_to(s, (CHUNK, gd))

    rev = lambda i: nt - 1 - i

    def halo(col):
        return pl.BlockSpec((HALO, d), lambda i: (jnp.maximum(rev(i) * (tm // HALO) - 1, 0), col))

    const2 = lambda i: (0, 0)
    const3 = lambda i: (0, 0, 0)
    row = lambda i: (rev(i), 0)
    return pl.pallas_call(
        body, name=name, grid=(nt,),
        in_specs=[pl.BlockSpec((tm, d), row),
                  pl.BlockSpec((tm, 7 * d), row),
                  halo(1), halo(2),
                  _once((d, d), const2),
                  _once((HALO, d), const2),
                  _once((1, d), const2),
                  _once((1, d), const2),
                  _once((N_GROUPS, CHUNK, CHUNK), const3),
                  _once((N_GROUPS, CHUNK, CHUNK), const3),
                  _once((CHUNK, d), const2)],
        out_specs=[pl.BlockSpec((tm, 7 * d), row),
                   pl.BlockSpec((HALO, d), const2),
                   pl.BlockSpec((HALO, d), const2),
                   pl.BlockSpec((N_GROUPS, CHUNK, CHUNK), const3),
                   pl.BlockSpec((CHUNK, d), const2)],
        out_shape=[jax.ShapeDtypeStruct((t, 7 * d), BF16),
                   jax.ShapeDtypeStruct((HALO, d), F32),
                   jax.ShapeDtypeStruct((HALO, d), F32),
                   jax.ShapeDtypeStruct((N_GROUPS, CHUNK, CHUNK), F32),
                   jax.ShapeDtypeStruct((CHUNK, d), F32)],
        scratch_shapes=[pltpu.VMEM((tm, d), F32), pltpu.VMEM((tm, d), F32),
                        pltpu.VMEM((HALO, d), F32), pltpu.VMEM((CHUNK, d), F32)],
        compiler_params=_params(1),
    )(dx1, proj, proj, proj, wout, cw, lng, lnb, ws, wst, bias)


def dgrad_rms(dy, w, x, g, res, *, name, tm):
    t, d = x.shape
    n = w.shape[2]
    w = w.reshape(w.shape[0] // 2, 2, d, n)
    nj = w.shape[0]
    tm = min(tm, t)

    def body(dy_ref, w_ref, x_ref, g_ref, res_ref, dx_ref, dg_ref, acc_ref):
        i, j = pl.program_id(0), pl.program_id(1)

        @pl.when((i == 0) & (j == 0))
        def _():
            dg_ref[...] = jnp.zeros_like(dg_ref)

        pair = jnp.concatenate([w_ref[0], w_ref[1]], axis=1)
        part = lax.dot_general(dy_ref[...], pair, NT_DIMS, preferred_element_type=F32)

        @pl.when(j == 0)
        def _():
            acc_ref[...] = part

        @pl.when(j > 0)
        def _():
            acc_ref[...] += part

        @pl.when(j == nj - 1)
        def _():
            dx, dg = _rms_bwd(acc_ref[...], x_ref[...], g_ref[...])
            dx_ref[...] = res_ref[...] + dx
            dg_ref[0:1, :] += dg

    return pl.pallas_call(
        body, name=name, grid=(t // tm, nj),
        in_specs=[pl.BlockSpec((tm, 2 * n), lambda i, j: (i, j)),
                  pl.BlockSpec((None, 2, d, n), lambda i, j: (j, 0, 0, 0)),
                  pl.BlockSpec((tm, d), lambda i, j: (i, 0)),
                  pl.BlockSpec((1, d), lambda i, j: (0, 0)),
                  pl.BlockSpec((tm, d), lambda i, j: (i, 0))],
        out_specs=[pl.BlockSpec((tm, d), lambda i, j: (i, 0)), pl.BlockSpec((HALO, d), lambda i, j: (0, 0))],
        out_shape=[jax.ShapeDtypeStruct((t, d), F32), jax.ShapeDtypeStruct((HALO, d), F32)],
        scratch_shapes=[pltpu.VMEM((tm, d), F32)],
        compiler_params=_params(2),
    )(dy, w, x, g, res)


def wgrad(a, b, *, nj, a_mode, b_mode, name, tm, split=1):
    def describe(arr, mode):
        if mode == "full":
            return arr.shape[0], arr.shape[1], pl.BlockSpec((tm_, arr.shape[1]), lambda j, s: (s, 0))
        if mode == "cols":
            c = arr.shape[1] // nj
            return arr.shape[0], c, pl.BlockSpec((tm_, c), lambda j, s: (s, j))
        return arr.shape[1], arr.shape[2], pl.BlockSpec((None, tm_, arr.shape[2]), lambda j, s: (j, s, 0))

    t = a.shape[0] if a_mode != "lead" else a.shape[1]
    tm_ = min(tm, t)
    _, k, a_spec = describe(a, a_mode)
    _, n, b_spec = describe(b, b_mode)

    ns = t // tm_
    nc = n // split

    def body(a_ref, b_ref, o_ref, acc_ref):
        s = pl.program_id(1)
        part = lax.dot_general(a_ref[...], b_ref[...], TN_DIMS, preferred_element_type=F32)

        def finish(total):
            for q in range(split):
                o_ref[q] = total[:, q * nc:(q + 1) * nc].astype(BF16)

        if ns == 1:
            finish(part)
            return

        @pl.when(s == 0)
        def _():
            acc_ref[...] = part

        @pl.when((s > 0) & (s < ns - 1))
        def _():
            acc_ref[...] += part

        @pl.when(s == ns - 1)
        def _():
            finish(acc_ref[...] + part)

    return pl.pallas_call(
        body, name=name, grid=(nj, ns),
        in_specs=[a_spec, b_spec],
        out_specs=pl.BlockSpec((split, k, nc), lambda j, s: (j, 0, 0)),
        out_shape=jax.ShapeDtypeStruct((nj * split, k, nc), BF16),
        scratch_shapes=[pltpu.VMEM((k, n), F32)],
        compiler_params=_params(2),
    )(a, b)


def _adamw_math(w, g, m, v):
    m = ADAM_B1 * m + (1.0 - ADAM_B1) * g
    v = ADAM_B2 * v + (1.0 - ADAM_B2) * (g * g)
    m_hat = m / (1.0 - ADAM_B1 ** ADAM_STEP)
    v_hat = v / (1.0 - ADAM_B2 ** ADAM_STEP)
    delta = -ADAM_LR * (m_hat / (jnp.sqrt(v_hat) + ADAM_EPS) + ADAM_WD * w)
    return delta, m, v


def _row_tile(rows, at_most):
    if rows <= at_most:
        return rows
    return max(k for k in range(16, at_most + 1, 16) if rows % k == 0)


def _sum_in_device_order(ref):
    total = ref[0]
    for s in range(1, N_DEV):
        total = total + ref[s]
    return total


def adamw_sharded(me, own0, recv0, own1, recv1, w, m, v, *, name, tr):
    _, r, c = w.shape
    tr = _row_tile(r, tr)
    ni = r // tr

    def body(me_ref, o0_ref, r0_ref, o1_ref, r1_ref, w_ref, m_ref, v_ref, g_ref, d_ref, nm_ref, nv_ref):
        def finish(own_ref, recv_ref):
            g = None
            for s in range(N_DEV):
                term = jnp.where(me_ref[0] == s, own_ref[...], recv_ref[s]).astype(F32)
                g = term if g is None else g + term
            delta, nm, nv = _adamw_math(w_ref[...], g, m_ref[...], v_ref[...])
            g_ref[...] = g
            d_ref[...] = delta
            nm_ref[...] = nm
            nv_ref[...] = nv

        @pl.when(pl.program_id(0) == 0)
        def _():
            finish(o0_ref, r0_ref)

        @pl.when(pl.program_id(0) == 1)
        def _():
            finish(o1_ref, r1_ref)

    row0 = lambda l, i: i * (1 - l) + (ni - 1) * l
    row1 = lambda l, i: i * l
    lay = pl.BlockSpec((None, tr, c), lambda l, i, me_ref: (l, i, 0))
    grid_spec = pltpu.PrefetchScalarGridSpec(
        num_scalar_prefetch=1, grid=(2, ni),
        in_specs=[pl.BlockSpec((None, tr, c), lambda l, i, me_ref: (me_ref[0], row0(l, i), 0)),
                  pl.BlockSpec((N_DEV, tr, c), lambda l, i, me_ref: (0, row0(l, i), 0)),
                  pl.BlockSpec((None, tr, c), lambda l, i, me_ref: (me_ref[0], row1(l, i), 0)),
                  pl.BlockSpec((N_DEV, tr, c), lambda l, i, me_ref: (0, row1(l, i), 0)),
                  lay, lay, lay],
        out_specs=[lay, lay, lay, lay])
    return pl.pallas_call(
        body, name=name, grid_spec=grid_spec,
        out_shape=[jax.ShapeDtypeStruct(w.shape, F32)] * 4,
        compiler_params=_params(2),
    )(me, own0, recv0, own1, recv1, w, m, v)


def adamw_small(g, w, m, v, *, name):
    def body(g_ref, w_ref, m_ref, v_ref, d_ref, nm_ref, nv_ref):
        delta, nm, nv = _adamw_math(w_ref[...], g_ref[...], m_ref[...], v_ref[...])
        d_ref[...] = delta
        nm_ref[...] = nm
        nv_ref[...] = nv

    return pl.pallas_call(
        body, name=name,
        out_shape=[jax.ShapeDtypeStruct(w.shape, F32)] * 3,
        compiler_params=pltpu.CompilerParams(vmem_limit_bytes=VMEM_LIMIT_BYTES),
    )(g, w, m, v)


def sum_devices(parts, *, name, tr):
    _, r, c = parts.shape
    tr = min(tr, r)

    def body(p_ref, o_ref):
        o_ref[...] = _sum_in_device_order(p_ref)

    return pl.pallas_call(
        body, name=name, grid=(r // tr,),
        in_specs=[pl.BlockSpec((N_DEV, tr, c), lambda i: (0, i, 0))],
        out_specs=pl.BlockSpec((tr, c), lambda i: (i, 0)),
        out_shape=jax.ShapeDtypeStruct((r, c), F32),
        compiler_params=_params(1),
    )(parts)


def _my_place():
    return lax.axis_index("x"), lax.axis_index("y"), lax.axis_index("c")


def all_gather(arrays, after, *, name):
    n = len(arrays)

    def body(*refs):
        ins, outs = refs[:n], refs[n + 1:2 * n + 1]
        send_sems, recv_sems, local_sems = refs[2 * n + 1:]
        x, y, c = _my_place()
        me, sibling = (x, y, c), (x, y, 1 - c)
        chips = [(1 - x, y), (x, 1 - y), (1 - x, 1 - y)]
        waits = []
        for a in range(n):
            def slot(place, a=a):
                px, py, pc = place
                return outs[a].at[4 * px + 2 * py + pc]

            def copy(k, block, to, src=None, a=a, slot=slot):
                return pltpu.make_async_remote_copy(
                    src_ref=slot(block) if src is None else src, dst_ref=slot(block),
                    send_sem=send_sems.at[a, k], recv_sem=recv_sems.at[a, k],
                    device_id=to, device_id_type=MESH)

            mine = pltpu.make_async_copy(ins[a], slot(me), local_sems.at[a])
            mine.start()
            first = [copy(0, me, sibling, src=ins[a])]
            first += [copy(1 + j, me, (*chip, c), src=ins[a]) for j, chip in enumerate(chips)]
            for cp in first:
                cp.start()
            waits.append((copy, mine, first))
        sends = []
        for a in range(n):
            copy, mine, first = waits[a]
            passed = [copy(4 + j, (*chip, c), sibling) for j, chip in enumerate(chips)]
            for j, chip in enumerate(chips):
                copy(1 + j, (*chip, c), me).wait_recv()
                passed[j].start()
            sends.append(first + passed)
        for a in range(n):
            copy, mine, first = waits[a]
            copy(0, sibling, me).wait_recv()
            for j, chip in enumerate(chips):
                copy(4 + j, (*chip, 1 - c), me).wait_recv()
            for cp in sends[a]:
                cp.wait_send()
            mine.wait()

    return pl.pallas_call(
        body, name=name,
        in_specs=[ANY] * (n + 1), out_specs=[ANY] * n,
        out_shape=[jax.ShapeDtypeStruct((N_DEV,) + a.shape, a.dtype) for a in arrays],
        scratch_shapes=[pltpu.SemaphoreType.DMA((n, 7)), pltpu.SemaphoreType.DMA((n, 7)),
                        pltpu.SemaphoreType.DMA((n,))],
        compiler_params=pltpu.CompilerParams(has_side_effects=True),
    )(*arrays, after)


def _peer_place(r, x, y, c):
    fx, fy, fc = (r >> 2) & 1, (r >> 1) & 1, r & 1
    return (1 - x if fx else x, 1 - y if fy else y, 1 - c if fc else c)


def own_slot(me, w, layer, dtype, *, name, tr):
    _, r, c = w.shape
    tr = _row_tile(r, tr)

    def body(me_ref, w_ref, o_ref):
        o_ref[...] = w_ref[...].astype(dtype)

    grid_spec = pltpu.PrefetchScalarGridSpec(
        num_scalar_prefetch=1, grid=(r // tr,),
        in_specs=[pl.BlockSpec((None, tr, c), lambda i, me_ref: (layer, i, 0))],
        out_specs=pl.BlockSpec((None, tr, c), lambda i, me_ref: (me_ref[0], i, 0)))
    return pl.pallas_call(
        body, name=name, grid_spec=grid_spec,
        out_shape=jax.ShapeDtypeStruct((N_DEV, r, c), dtype),
        compiler_params=_params(1),
    )(me, w)


EXCHANGES = {
    "scatter": [(0, r) for r in range(1, N_DEV)],
    "gather": [(0, r) for r in range(1, N_DEV)],
    "gather_chips": [(0, r) for r in (1, 2, 4, 6)],
    "gather_forward": [(q, 1) for q in (2, 4, 6)],
}


def _split_copy(k, entry, src, land, send_sem, recv_sem, arriving):
    slot, peer = entry
    x, y, c = _my_place()

    def index(relation):
        px, py, pc = _peer_place(relation, x, y, c)
        return 4 * px + 2 * py + pc

    return pltpu.make_async_remote_copy(
        src_ref=land.at[index(slot)] if src is None else src.at[index(peer)],
        dst_ref=land.at[index(slot ^ peer if arriving else slot)],
        send_sem=send_sem.at[k], recv_sem=recv_sem.at[k],
        device_id=_peer_place(peer, x, y, c), device_id_type=MESH)


def start_copies(srcs, lands, *, mode, name, after=None):
    n = len(lands)
    entries = EXCHANGES[mode]
    bufs = (list(srcs) if srcs is not None else []) + list(lands)
    nb = len(bufs)

    def body(*refs):
        src = refs[:n] if srcs is not None else [None] * n
        land = refs[nb - n:nb]
        outs = refs[nb + len(extra):]
        send_sems, recv_sems = outs[:n], outs[n:2 * n]
        token = outs[2 * n + nb]
        for a in range(n):
            for k, entry in enumerate(entries):
                _split_copy(k, entry, src[a], land[a], send_sems[a], recv_sems[a], False).start()
        token[...] = jnp.zeros_like(token)

    extra = [] if after is None else [after]
    outs = pl.pallas_call(
        body, name=name,
        in_specs=[HBM_SPEC] * nb + [ANY] * len(extra),
        out_specs=[SEM_SPEC] * (2 * n) + [HBM_SPEC] * nb + [pl.BlockSpec(memory_space=pltpu.VMEM)],
        out_shape=([pltpu.SemaphoreType.DMA((len(entries),))] * (2 * n)
                   + [pltpu.HBM(a.shape, a.dtype) for a in bufs]
                   + [jax.ShapeDtypeStruct((8, 128), F32)]),
        input_output_aliases={i: 2 * n + i for i in range(nb)},
        compiler_params=pltpu.CompilerParams(has_side_effects=DATAFLOW),
    )(*[pltpu.with_memory_space_constraint(a, pltpu.HBM) for a in bufs], *extra)
    thru = list(outs[2 * n:2 * n + nb])
    return dict(send=outs[:n], recv=outs[n:2 * n], src=thru[:n] if srcs is not None else None, land=thru[nb - n:],
                token=outs[2 * n + nb], mode=mode)


def finish_copies(started, which, after, *, name):
    n = len(which)
    entries = EXCHANGES[started["mode"]]
    has_src = started["src"] is not None
    bufs = ([started["src"][i] for i in which] if has_src else []) + [started["land"][i] for i in which]
    nb = len(bufs)

    def body(*refs):
        src = refs[:n] if has_src else [None] * n
        land = refs[nb - n:nb]
        send_sems, recv_sems = refs[nb:nb + n], refs[nb + n:nb + 2 * n]
        for a in range(n):
            for k, entry in enumerate(entries):
                cp = _split_copy(k, entry, src[a], land[a], send_sems[a], recv_sems[a], True)
                cp.wait_send()
                cp.wait_recv()

    outs = pl.pallas_call(
        body, name=name,
        in_specs=[HBM_SPEC] * nb + [SEM_SPEC] * (2 * n) + [ANY],
        out_specs=[HBM_SPEC] * nb,
        out_shape=[pltpu.HBM(a.shape, a.dtype) for a in bufs],
        input_output_aliases={i: i for i in range(nb)},
        compiler_params=pltpu.CompilerParams(has_side_effects=DATAFLOW),
    )(*bufs, *[started["send"][i] for i in which], *[started["recv"][i] for i in which], after)
    return (list(outs[:n]) if has_src else None), list(outs[nb - n:])


def _pad_rows(a, rows):
    pad = [(0, 0)] * a.ndim
    pad[-2] = (0, rows - a.shape[-2])
    return jnp.pad(a, pad)


def kernel(x, mix_norm_g, w_in, conv_a_w, ln_v_g, ln_v_b, w_s, b_s, w_out, ffn_norm_g, w_up, conv_ffn_w, w_down, final_norm_g, loss_target, m_mix_norm_g, m_w_in, m_conv_a_w, m_ln_v_g, m_ln_v_b, m_w_s, m_b_s, m_w_out, m_ffn_norm_g, m_w_up, m_conv_ffn_w, m_w_down, m_final_norm_g, v_mix_norm_g, v_w_in, v_conv_a_w, v_ln_v_g, v_ln_v_b, v_w_s, v_b_s, v_w_out, v_ffn_norm_g, v_w_up, v_conv_ffn_w, v_w_down, v_final_norm_g):
    nb, seq, d = x.shape
    t = nb * seq
    depth = w_in.shape[0]
    f = w_up.shape[2]
    me = 4 * lax.axis_index("x") + 2 * lax.axis_index("y") + lax.axis_index("c")
    xt = x.reshape(t, d)
    tgt = loss_target.reshape(t, d)

    conv_pack = jnp.concatenate([_pad_rows(conv_a_w, HALO), _pad_rows(conv_ffn_w, HALO)], axis=-1)
    me_arr = me.astype(jnp.int32).reshape(1)
    w_up_t, m_w_up_t, v_w_up_t = (jnp.swapaxes(a, 1, 2) for a in (w_up, m_w_up, v_w_up))
    zones, slot_of = [], {}
    for l in range(depth):
        for key, w in (("win", w_in), ("conv", None), ("wout", w_out), ("wup", w_up_t), ("wd", w_down)):
            if key == "conv":
                if l == 0:
                    slot_of["conv"] = len(zones)
                    packed = conv_pack.reshape(1, depth * HALO, conv_pack.shape[-1])
                    zones.append(own_slot(me_arr, packed, 0, F32, name="own_slot_conv", tr=256))
                continue
            slot_of[key, l] = len(zones)
            zones.append(own_slot(me_arr, w, l, BF16, name=f"own_slot_{key}_{l}", tr=256))
    first = [slot_of["win", 0], slot_of["wout", 0], slot_of["conv"]]
    rest = [i for i in range(len(zones)) if i not in first]
    to_chips = start_copies(None, [zones[i] for i in first], mode="gather_chips", name="gather_first_chips")
    gathering = start_copies(None, [zones[i] for i in rest], mode="gather", name="gather_start", after=to_chips["token"])
    _, at_chips = finish_copies(to_chips, [0, 1, 2], gathering["token"], name="wait_first_chips")
    to_sibling = start_copies(None, at_chips, mode="gather_forward", name="gather_first_forward")

    def gathered(keys, after, name):
        return finish_copies(gathering, [rest.index(slot_of[k]) for k in keys], after, name=name)[1]

    saved, layers = [], []
    cur = xt
    for l in range(depth):
        p = dict(mix_g=mix_norm_g[l][None], ffn_g=ffn_norm_g[l][None], lng=ln_v_g[l][None], lnb=ln_v_b[l][None],
                 ws=w_s[l], wst=jnp.swapaxes(w_s[l], 1, 2),
                 bias=jnp.repeat(b_s[l].T, d // N_GROUPS, axis=1))
        if l == 0:
            _, (p["win"], wout_g, conv_g) = finish_copies(to_sibling, [0, 1, 2], to_sibling["token"],
                                                          name=f"wait_w_mixer_{l}")
            conv_g = conv_g.reshape(N_DEV, depth, HALO, -1)
            ca = conv_g.shape[-1] - f
        else:
            p["win"], wout_g = gathered([("win", l), ("wout", l)], after, f"wait_w_mixer_{l}")
        p["wout"] = wout_g.reshape(d, d)
        p["cw_a"] = jnp.transpose(conv_g[:, l, :, :ca], (1, 0, 2)).reshape(HALO, d)
        p["cw_f"] = conv_g[:, l, :, ca:]
        h, proj, merged, x1 = mixer_fwd(cur, p["mix_g"], p["win"], p["wout"], p["cw_a"], p["lng"], p["lnb"], p["ws"],
                                        p["bias"], seq=seq, name=f"mixer_fwd_{l}", tm=256)
        p["wup"], wd_g = gathered([("wup", l), ("wd", l)], merged, f"wait_w_ffn_{l}")
        p["wd"] = wd_g.reshape(N_DEV // 2, 2 * wd_g.shape[1], d)
        head = (final_norm_g[None], tgt) if l == depth - 1 else None
        h2, up0, upc, act, x2, *of_loss = ffn_fwd(x1, p["ffn_g"], p["wup"], p["wd"], p["cw_f"],
                                                  seq=seq, name=f"ffn_fwd_{l}", tm=256, head=head)
        saved.append(dict(x0=cur, h=h, proj=proj, merged=merged, x1=x1, h2=h2, up0=up0, upc=upc, act=act))
        layers.append(p)
        cur, after = x2, act
    dx = cur
    d_final_g, loss_tile = of_loss

    def exchange(parts, name):
        return start_copies(parts, [lax.empty(a.shape, a.dtype) for a in parts], mode="scatter", name=name)

    def tied(g, started):
        return g + started["token"][0:1, 0:1]

    part = [None] * depth
    for l in reversed(range(depth)):
        p, s = layers[l], saved[l]
        dup0, dcw_f, dx1, d_ffn_g = ffn_bwd(dx, s["up0"], s["upc"], p["wd"], p["cw_f"], p["wup"], s["x1"], p["ffn_g"],
                                            seq=seq, name=f"ffn_bwd_{l}", tm=256)
        g_wd = wgrad(s["act"], dx, nj=N_DEV // 2, a_mode="lead", b_mode="full", name=f"wgrad_down_{l}", tm=2048)
        g_wup = wgrad(dup0, s["h2"], nj=N_DEV, a_mode="lead", b_mode="full", name=f"wgrad_up_{l}", tm=2048)
        ffn_ex = exchange([g_wd.reshape(N_DEV, g_wd.shape[1] // 2, d), g_wup], f"exchange_ffn_{l}")
        dproj, dcw_a, dln, dws, dbs = mixer_bwd(dx1, s["proj"], p["wout"], tied(p["cw_a"], ffn_ex), p["lng"], p["lnb"],
                                                 p["ws"], p["wst"], p["bias"], seq=seq, name=f"mixer_bwd_{l}", tm=256)
        g_wout = wgrad(s["merged"], dx1, nj=1, a_mode="full", b_mode="full", name=f"wgrad_out_{l}", tm=2048)
        g_win = wgrad(s["h"], dproj, nj=N_DEV // 2, a_mode="full", b_mode="cols", name=f"wgrad_in_{l}", tm=2048, split=2)
        mix_ex = exchange([g_wout.reshape(N_DEV, d // N_DEV, d), g_win], f"exchange_mix_{l}")
        dx, d_mix_g = dgrad_rms(dproj, p["win"], s["x0"], tied(p["mix_g"], mix_ex), dx1, name=f"dgrad_in_{l}", tm=1024)
        part[l] = dict(
            ffn_ex=ffn_ex, mix_ex=mix_ex,
            small=jnp.concatenate([
                dws.reshape(N_GROUPS * CHUNK * CHUNK // d, d),
                d_mix_g[0:1], d_ffn_g[0:1], dln[0:2], dcw_a[0:3],
                dbs[:, ::d // N_GROUPS].T.reshape(1, d)], axis=0),
            cw_f=dcw_f.reshape(N_DEV * HALO, f))
    grad_x = dx.reshape(nb, seq, d)

    own, recv = {}, {}

    def arrived(l, ex, keys, after):
        srcs, lands = finish_copies(part[l][ex], [0, 1], after, name=f"wait_{ex}_{l}")
        for k, key in enumerate(keys):
            own[key, l], recv[key, l] = srcs[k], lands[k]
        return lands[1]

    def big(key, w, m, v, name):
        return adamw_sharded(me_arr, own[key, 0], recv[key, 0], own[key, 1], recv[key, 1], w, m, v, name=name, tr=256)

    after = grad_x
    for l in reversed(range(depth)):
        after = arrived(l, "ffn_ex", ("wd", "wup"), after)
        if l > 0:
            after = arrived(l, "mix_ex", ("wout", "win"), after)
    u_wd = big("wd", w_down, m_w_down, v_w_down, "adamw_w_down")
    u_wup = tuple(jnp.swapaxes(a, 1, 2) for a in big("wup", w_up_t, m_w_up_t, v_w_up_t, "adamw_w_up"))

    loss_row = jnp.zeros((1, d), F32).at[0, 0].set(loss_tile[0, 0])
    small = jnp.concatenate([part[l]["small"] for l in range(depth)] + [d_final_g[0:1], loss_row], axis=0)
    small = _pad_rows(small, -(-small.shape[0] // 8) * 8)
    cwf = jnp.concatenate([part[l]["cw_f"] for l in range(depth)], axis=0)
    small_all, cwf_all = all_gather([small, cwf], u_wup[0], name="gather_small_grads")
    small_sum = sum_devices(small_all, name="sum_small", tr=512)
    cwf_sum = sum_devices(cwf_all, name="sum_conv_ffn", tr=512)


    rows_ws = N_GROUPS * CHUNK * CHUNK // d
    per_layer = rows_ws + 8
    def small_of(l, a, b):
        return small_sum[l * per_layer + rows_ws + a:l * per_layer + rows_ws + b]
    g_ws = jnp.stack([small_sum[l * per_layer:l * per_layer + rows_ws].reshape(N_GROUPS, CHUNK, CHUNK)
                      for l in range(depth)])
    g_mix = jnp.concatenate([small_of(l, 0, 1) for l in range(depth)])
    g_ffn = jnp.concatenate([small_of(l, 1, 2) for l in range(depth)])
    g_lng = jnp.concatenate([small_of(l, 2, 3) for l in range(depth)])
    g_lnb = jnp.concatenate([small_of(l, 3, 4) for l in range(depth)])
    g_cwa_full = jnp.stack([small_of(l, 4, 7) for l in range(depth)])
    g_cwa = lax.dynamic_slice_in_dim(g_cwa_full, me * ca, ca, axis=2)
    g_bs = jnp.stack([small_of(l, 7, 8).reshape(N_GROUPS, CHUNK) for l in range(depth)])
    g_final = small_sum[depth * per_layer]
    loss = small_sum[depth * per_layer + 1, 0]
    cwf_sum = cwf_sum.reshape(depth, N_DEV, HALO, f)
    g_cwf = lax.dynamic_index_in_dim(cwf_sum, me, axis=1, keepdims=False)[:, :3]

    arrived(0, "mix_ex", ("wout", "win"), small_sum)
    u_wout = big("wout", w_out, m_w_out, v_w_out, "adamw_w_out")
    u_win = big("win", w_in, m_w_in, v_w_in, "adamw_w_in")

    def small_update(g, w, m, v, name):
        shape = w.shape
        two_d = (-1, shape[-1]) if w.ndim > 1 else (1, shape[0])
        out = adamw_small(g.reshape(two_d), w.reshape(two_d), m.reshape(two_d), v.reshape(two_d), name=name)
        return (g.reshape(shape),) + tuple(o.reshape(shape) for o in out)

    u_mix = small_update(g_mix, mix_norm_g, m_mix_norm_g, v_mix_norm_g, "adamw_mix_norm_g")
    u_cwa = small_update(g_cwa, conv_a_w, m_conv_a_w, v_conv_a_w, "adamw_conv_a_w")
    u_lng = small_update(g_lng, ln_v_g, m_ln_v_g, v_ln_v_g, "adamw_ln_v_g")
    u_lnb = small_update(g_lnb, ln_v_b, m_ln_v_b, v_ln_v_b, "adamw_ln_v_b")
    u_ws = small_update(g_ws, w_s, m_w_s, v_w_s, "adamw_w_s")
    u_bs = small_update(g_bs, b_s, m_b_s, v_b_s, "adamw_b_s")
    u_ffn = small_update(g_ffn, ffn_norm_g, m_ffn_norm_g, v_ffn_norm_g, "adamw_ffn_norm_g")
    u_cwf = small_update(g_cwf, conv_ffn_w, m_conv_ffn_w, v_conv_ffn_w, "adamw_conv_ffn_w")
    u_final = small_update(g_final, final_norm_g, m_final_norm_g, v_final_norm_g, "adamw_final_norm_g")

    ordered = [u_mix, u_win, u_cwa, u_lng, u_lnb, u_ws, u_bs, u_wout, u_ffn, u_wup, u_cwf, u_wd, u_final]
    return (loss, grad_x, *[u[0] for u in ordered], *[u[1] for u in ordered],
            *[u[2] for u in ordered], *[u[3] for u in ordered])
```

```python
import functools

import jax
import jax.numpy as jnp
from jax import lax
from jax.experimental import pallas as pl
from jax.experimental.pallas import tpu as pltpu

EPS = 1e-6
CHUNK = 128
N_GROUPS = 8
N_DEV = 8
HALO = 8
ADAM_LR = 0.001
ADAM_B1 = 0.9
ADAM_B2 = 0.999
ADAM_EPS = 1e-08
ADAM_WD = 0.01
ADAM_STEP = 10
VMEM_LIMIT_BYTES = 56 * 1024 * 1024
F32 = jnp.float32
BF16 = jnp.bfloat16
MESH = pl.DeviceIdType.MESH
ANY = pl.BlockSpec(memory_space=pl.ANY)
HBM_SPEC = pl.BlockSpec(memory_space=pltpu.HBM)
SEM_SPEC = pl.BlockSpec(memory_space=pltpu.SEMAPHORE)
DATAFLOW = pltpu.SideEffectType.DATAFLOW_SIDE_EFFECTING
NT_DIMS = (((1,), (1,)), ((), ()))
TN_DIMS = (((0,), (0,)), ((), ()))


def _params(n_grid_axes):
    return pltpu.CompilerParams(dimension_semantics=("arbitrary",) * n_grid_axes,
                                vmem_limit_bytes=VMEM_LIMIT_BYTES)


def _shift_down(cur, prev8, k):
    rolled = pltpu.roll(cur, k, 0)
    prolled = pltpu.roll(prev8, k, 0)
    row = lax.broadcasted_iota(jnp.int32, prev8.shape, 0)
    head = jnp.where(row < k, prolled, rolled[:HALO])
    return jnp.concatenate([head, rolled[HALO:]], axis=0)


def _shift_up(cur, next8, k):
    tm = cur.shape[0]
    rolled = pltpu.roll(cur, tm - k, 0)
    nrolled = pltpu.roll(next8, HALO - k, 0)
    row = lax.broadcasted_iota(jnp.int32, next8.shape, 0)
    tail = jnp.where(row >= HALO - k, nrolled, rolled[tm - HALO:])
    return jnp.concatenate([rolled[:tm - HALO], tail], axis=0)


def _conv_fwd(cur, prev8, cw):
    s1 = _shift_down(cur, prev8, 1)
    s2 = _shift_down(cur, prev8, 2)
    y = s2 * cw[0:1, :] + s1 * cw[1:2, :] + cur * cw[2:3, :]
    return y, s1, s2


def _conv_bwd(d, next8, cw):
    u1 = _shift_up(d, next8, 1)
    u2 = _shift_up(d, next8, 2)
    return d * cw[2:3, :] + u1 * cw[1:2, :] + u2 * cw[0:1, :], u1, u2


def _colsum(a):
    return jnp.sum(a, axis=0, keepdims=True)


def _rms_stats(xv):
    r = lax.rsqrt(jnp.mean(xv * xv, axis=-1, keepdims=True) + EPS)
    return r, xv * r


def _rms_bwd(dh, xv, g):
    r, n = _rms_stats(xv)
    dn = dh * g
    dx = r * (dn - n * jnp.mean(dn * n, axis=-1, keepdims=True))
    return dx, _colsum(dh * n)


def _mixer_forward(p_ref, cprev, xiprev, cw, lng, lnb, ws_ref, bias_ref, mixed_scr, d):
    tm = p_ref.shape[0]
    b = p_ref[:, 0:d]
    c = p_ref[:, d:2 * d]
    xi = p_ref[:, 2 * d:3 * d]
    u = p_ref[:, 3 * d:4 * d]
    v = p_ref[:, 4 * d:5 * d]
    sa = jax.nn.sigmoid(p_ref[:, 5 * d:6 * d])
    sb = jax.nn.sigmoid(p_ref[:, 6 * d:7 * d])
    cx = c * xi
    conv, s1, s2 = _conv_fwd(cx, cprev * xiprev, cw)
    ya = b * conv
    mu = jnp.mean(v, axis=-1, keepdims=True)
    xc = v - mu
    rstd = lax.rsqrt(jnp.mean(xc * xc, axis=-1, keepdims=True) + EPS)
    vhat = xc * rstd
    vnb = (vhat * lng + lnb).astype(BF16)
    tril = (lax.broadcasted_iota(jnp.int32, (CHUNK, CHUNK), 0)
            >= lax.broadcasted_iota(jnp.int32, (CHUNK, CHUNK), 1))
    gd = d // N_GROUPS
    for g in range(N_GROUPS):
        wm = jnp.where(tril, ws_ref[g], 0.0).astype(BF16)
        cols = slice(g * gd, (g + 1) * gd)
        for n in range(tm // CHUNK):
            rows = slice(n * CHUNK, (n + 1) * CHUNK)
            mixed_scr[rows, cols] = (jnp.dot(wm, vnb[rows, cols], preferred_element_type=F32)
                                     + bias_ref[:, cols])
    mixed = mixed_scr[...]
    yb = u * mixed
    merged = sa * ya + sb * yb
    return dict(b=b, c=c, xi=xi, u=u, sa=sa, sb=sb, cx=cx, s1=s1, s2=s2, conv=conv, ya=ya,
                rstd=rstd, vhat=vhat, vnb=vnb, mixed=mixed, yb=yb, merged=merged, tril=tril)


def _once(block_shape, index_map):
    return pl.BlockSpec(block_shape, index_map, pipeline_mode=pl.Buffered(1))


def mixer_fwd(x, g, win, wout, cw, lng, lnb, ws, bias, *, seq, name, tm):
    t, d = x.shape
    nj, _, n = win.shape
    tm = min(tm, seq)
    tiles_per_seq = seq // tm

    def body(x_ref, g_ref, win_ref, wout_ref, cw_ref, lng_ref, lnb_ref, ws_ref, bias_ref,
             h_ref, p_ref, merged_ref, x1_ref, mixed_scr, carry_ref):
        @pl.when(pl.program_id(0) == 0)
        def _():
            carry_ref[...] = jnp.zeros_like(carry_ref)

        keep = jnp.where(pl.program_id(0) % tiles_per_seq == 0, 0.0, 1.0)
        xv = x_ref[...]
        _, nrm = _rms_stats(xv)
        hb = (nrm * g_ref[...]).astype(BF16)
        h_ref[...] = hb
        for j in range(0, nj, 2):
            pair = jnp.concatenate([win_ref[j], win_ref[j + 1]], axis=1)
            p_ref[:, j * n:(j + 2) * n] = jnp.dot(hb, pair, preferred_element_type=F32)
        f = _mixer_forward(p_ref, carry_ref[...] * keep, 1.0, cw_ref[...], lng_ref[...],
                           lnb_ref[...], ws_ref, bias_ref, mixed_scr, d)
        carry_ref[...] = f["cx"][tm - HALO:]
        mb = f["merged"].astype(BF16)
        merged_ref[...] = mb
        x1_ref[...] = xv + jnp.dot(mb, wout_ref[...], preferred_element_type=F32)

    const2 = lambda i: (0, 0)
    const3 = lambda i: (0, 0, 0)
    row = lambda i: (i, 0)
    return pl.pallas_call(
        body, name=name, grid=(t // tm,),
        in_specs=[pl.BlockSpec((tm, d), row),
                  _once((1, d), const2),
                  _once((nj, d, n), const3),
                  _once((d, d), const2),
                  _once((HALO, d), const2),
                  _once((1, d), const2),
                  _once((1, d), const2),
                  _once((N_GROUPS, CHUNK, CHUNK), const3),
                  _once((CHUNK, d), const2)],
        out_specs=[pl.BlockSpec((tm, d), row), pl.BlockSpec((tm, nj * n), row),
                   pl.BlockSpec((tm, d), row), pl.BlockSpec((tm, d), row)],
        out_shape=[jax.ShapeDtypeStruct((t, d), BF16), jax.ShapeDtypeStruct((t, nj * n), F32),
                   jax.ShapeDtypeStruct((t, d), BF16), jax.ShapeDtypeStruct((t, d), F32)],
        scratch_shapes=[pltpu.VMEM((tm, d), F32), pltpu.VMEM((HALO, d), F32)],
        compiler_params=_params(1),
    )(x, g, win, wout, cw, lng, lnb, ws, bias)


def ffn_fwd(x1, g, wup, wd, cw, *, seq, name, tm, head=None):
    t, d = x1.shape
    nj, f, _ = wup.shape
    half = nj // 2
    tm = min(tm, seq)
    tiles_per_seq = seq // tm

    def body(x1_ref, g_ref, wup_ref, wd_ref, cw_ref, *rest):
        if head is None:
            h2_ref, up_ref, upc_ref, act_ref, x2_ref, carry_ref = rest
        else:
            gf_ref, tgt_ref, h2_ref, up_ref, upc_ref, act_ref, x2_ref, dgf_ref, loss_ref, carry_ref = rest

        @pl.when(pl.program_id(0) == 0)
        def _():
            carry_ref[...] = jnp.zeros_like(carry_ref)
            if head is not None:
                dgf_ref[...] = jnp.zeros_like(dgf_ref)
                loss_ref[...] = jnp.zeros_like(loss_ref)

        keep = jnp.where(pl.program_id(0) % tiles_per_seq == 0, 0.0, 1.0)
        xv = x1_ref[...]
        _, nrm = _rms_stats(xv)
        hb = (nrm * g_ref[...]).astype(BF16)
        h2_ref[...] = hb

        for j in range(nj):
            up_ref[j] = lax.dot_general(hb, wup_ref[j], NT_DIMS, preferred_element_type=F32)

        def conv_of(j):
            up0 = up_ref[j]
            y, _, _ = _conv_fwd(up0, carry_ref[j] * keep, cw_ref[j])
            carry_ref[j] = up0[tm - HALO:]
            upc_ref[j] = y.astype(BF16)
            return y

        acc = xv
        for k in range(half):
            a = (jax.nn.silu(conv_of(k)) * conv_of(k + half)).astype(BF16)
            act_ref[k] = a
            acc = acc + jnp.dot(a, wd_ref[k], preferred_element_type=F32)
        if head is None:
            x2_ref[...] = acc
        else:
            gv = gf_ref[...]
            r, n = _rms_stats(acc)
            err = n * gv - tgt_ref[...]
            loss_ref[...] += 0.5 * jnp.sum(jnp.mean(err * err, axis=-1, keepdims=True))
            dy = err * (1.0 / d)
            dn = dy * gv
            x2_ref[...] = r * (dn - n * jnp.mean(dn * n, axis=-1, keepdims=True))
            dgf_ref[0:1, :] += _colsum(dy * n)

    const3 = lambda i: (0, 0, 0)
    row = lambda i: (i, 0)
    in_specs = [pl.BlockSpec((tm, d), row), _once((1, d), lambda i: (0, 0)), _once((nj, f, d), const3),
                _once((half, f, d), const3), _once((nj, HALO, f), const3)]
    out_specs = [pl.BlockSpec((tm, d), row), pl.BlockSpec((nj, tm, f), lambda i: (0, i, 0)),
                 pl.BlockSpec((nj, tm, f), lambda i: (0, i, 0)), pl.BlockSpec((half, tm, f), lambda i: (0, i, 0)),
                 pl.BlockSpec((tm, d), row)]
    out_shape = [jax.ShapeDtypeStruct((t, d), BF16), jax.ShapeDtypeStruct((nj, t, f), F32),
                 jax.ShapeDtypeStruct((nj, t, f), BF16), jax.ShapeDtypeStruct((half, t, f), BF16),
                 jax.ShapeDtypeStruct((t, d), F32)]
    args = [x1, g, wup, wd, cw]
    if head is not None:
        in_specs += [_once((1, d), lambda i: (0, 0)), pl.BlockSpec((tm, d), row)]
        out_specs += [pl.BlockSpec((HALO, d), lambda i: (0, 0)), pl.BlockSpec((8, 128), lambda i: (0, 0))]
        out_shape += [jax.ShapeDtypeStruct((HALO, d), F32), jax.ShapeDtypeStruct((8, 128), F32)]
        args += list(head)
    return pl.pallas_call(
        body, name=name, grid=(t // tm,), in_specs=in_specs, out_specs=out_specs, out_shape=out_shape,
        scratch_shapes=[pltpu.VMEM((nj, HALO, f), F32)],
        compiler_params=_params(1),
    )(*args)


def ffn_bwd(dx2, up0, upc, wd, cw, wup, x1, g, *, seq, name, tm):
    t, d = dx2.shape
    nj, _, f = up0.shape
    half = nj // 2
    tm = min(tm, seq)
    tiles_per_seq = seq // tm
    nt = t // tm

    def body(dx_ref, up_ref, upc_ref, wd_ref, cw_ref, wup_ref, x1_ref, g_ref,
             dup_ref, dcw_ref, dx1_ref, dg_ref, carry_ref):
        i = pl.program_id(0)
        tile = nt - 1 - i

        @pl.when(i == 0)
        def _():
            dcw_ref[...] = jnp.zeros_like(dcw_ref)
            dg_ref[...] = jnp.zeros_like(dg_ref)
            carry_ref[...] = jnp.zeros_like(carry_ref)

        keep_next = jnp.where(tile % tiles_per_seq == tiles_per_seq - 1, 0.0, 1.0)
        dx2v = dx_ref[...]
        dxb = dx2v.astype(BF16)
        dh = [jnp.zeros((tm, d), F32)]

        def through_conv(j, dup):
            next8 = carry_ref[j] * keep_next
            carry_ref[j] = dup[:HALO]
            dup0, u1, u2 = _conv_bwd(dup, next8, cw_ref[j])
            up0 = up_ref[j]
            dcw_ref[j, 0:1, :] += _colsum(u2 * up0)
            dcw_ref[j, 1:2, :] += _colsum(u1 * up0)
            dcw_ref[j, 2:3, :] += _colsum(dup * up0)
            dup0 = dup0.astype(BF16)
            dup_ref[j] = dup0
            dh[0] = dh[0] + jnp.dot(dup0, wup_ref[j], preferred_element_type=F32)

        dacts = [lax.dot_general(dxb, wd_ref[k], NT_DIMS, preferred_element_type=F32) for k in range(half)]
        for k in range(half):
            gate = upc_ref[k].astype(F32)
            val = upc_ref[k + half].astype(F32)
            dact = dacts[k]
            sg = jax.nn.sigmoid(gate)
            through_conv(k, dact * val * (sg * (1.0 + gate * (1.0 - sg))))
            through_conv(k + half, dact * (gate * sg))

        dx, dg = _rms_bwd(dh[0], x1_ref[...], g_ref[...])
        dx1_ref[...] = dx2v + dx
        dg_ref[0:1, :] += dg

    rev = lambda i: nt - 1 - i
    return pl.pallas_call(
        body, name=name, grid=(nt,),
        in_specs=[pl.BlockSpec((tm, d), lambda i: (rev(i), 0)),
                  pl.BlockSpec((nj, tm, f), lambda i: (0, rev(i), 0)),
                  pl.BlockSpec((nj, tm, f), lambda i: (0, rev(i), 0)),
                  _once((half, f, d), lambda i: (0, 0, 0)),
                  _once((nj, HALO, f), lambda i: (0, 0, 0)),
                  _once((nj, f, d), lambda i: (0, 0, 0)),
                  pl.BlockSpec((tm, d), lambda i: (rev(i), 0)),
                  _once((1, d), lambda i: (0, 0))],
        out_specs=[pl.BlockSpec((nj, tm, f), lambda i: (0, rev(i), 0)),
                   pl.BlockSpec((nj, HALO, f), lambda i: (0, 0, 0)),
                   pl.BlockSpec((tm, d), lambda i: (rev(i), 0)),
                   pl.BlockSpec((HALO, d), lambda i: (0, 0))],
        out_shape=[jax.ShapeDtypeStruct((nj, t, f), BF16), jax.ShapeDtypeStruct((nj, HALO, f), F32),
                   jax.ShapeDtypeStruct((t, d), F32), jax.ShapeDtypeStruct((HALO, d), F32)],
        scratch_shapes=[pltpu.VMEM((nj, HALO, f), F32)],
        compiler_params=_params(1),
    )(dx2, up0, upc, wd, cw, wup, x1, g)


def mixer_bwd(dx1, proj, wout, cw, lng, lnb, ws, wst, bias, *, seq, name, tm, on_to_x0=None):
    t, d = dx1.shape
    tm = min(tm, seq)
    tiles_per_seq = seq // tm
    nt = t // tm
    gd = d // N_GROUPS
    if on_to_x0 is not None:
        nj, _, wn = on_to_x0[0].shape

    def body(dx_ref, p_ref, cprev_ref, xiprev_ref, wout_ref, cw_ref, lng_ref, lnb_ref, ws_ref, wst_ref, bias_ref,
             *rest):
        if on_to_x0 is None:
            dp_ref, dcw_ref, dln_ref, dws_ref, dbs_ref, mixed_scr, dvn_scr, carry_ref, dbs_acc = rest
        else:
            (win_ref, x0_ref, g_ref, dp_ref, dcw_ref, dln_ref, dws_ref, dbs_ref, dx0_ref, dg_ref,
             mixed_scr, dvn_scr, carry_ref, dbs_acc) = rest
        i = pl.program_id(0)
        tile = nt - 1 - i

        @pl.when(i == 0)
        def _():
            dcw_ref[...] = jnp.zeros_like(dcw_ref)
            dln_ref[...] = jnp.zeros_like(dln_ref)
            dws_ref[...] = jnp.zeros_like(dws_ref)
            dbs_acc[...] = jnp.zeros_like(dbs_acc)
            if on_to_x0 is not None:
                dg_ref[...] = jnp.zeros_like(dg_ref)
            carry_ref[...] = jnp.zeros_like(carry_ref)

        keep_prev = jnp.where(tile % tiles_per_seq == 0, 0.0, 1.0)
        keep_next = jnp.where(tile % tiles_per_seq == tiles_per_seq - 1, 0.0, 1.0)
        cw = cw_ref[...]
        lng = lng_ref[...]
        f = _mixer_forward(p_ref, cprev_ref[...] * keep_prev, xiprev_ref[...], cw, lng, lnb_ref[...],
                           ws_ref, bias_ref, mixed_scr, d)
        dmerged = lax.dot_general(dx_ref[...].astype(BF16), wout_ref[...], NT_DIMS, preferred_element_type=F32)
        sa, sb = f["sa"], f["sb"]
        dp_ref[:, 5 * d:6 * d] = (dmerged * f["ya"] * (sa * (1.0 - sa))).astype(BF16)
        dp_ref[:, 6 * d:7 * d] = (dmerged * f["yb"] * (sb * (1.0 - sb))).astype(BF16)
        dya = dmerged * sa
        dyb = dmerged * sb
        dp_ref[:, 0:d] = (dya * f["conv"]).astype(BF16)
        dconv = dya * f["b"]
        dcw_ref[0:1, :] += _colsum(dconv * f["s2"])
        dcw_ref[1:2, :] += _colsum(dconv * f["s1"])
        dcw_ref[2:3, :] += _colsum(dconv * f["cx"])
        next8 = carry_ref[...] * keep_next
        carry_ref[...] = dconv[:HALO]
        dcx, _, _ = _conv_bwd(dconv, next8, cw)
        dp_ref[:, d:2 * d] = (dcx * f["xi"]).astype(BF16)
        dp_ref[:, 2 * d:3 * d] = (dcx * f["c"]).astype(BF16)
        dp_ref[:, 3 * d:4 * d] = (dyb * f["mixed"]).astype(BF16)
        dmixed = dyb * f["u"]
        dmb = dmixed.astype(BF16)
        vnb = f["vnb"]
        tril = f["tril"]
        triu = (lax.broadcasted_iota(jnp.int32, (CHUNK, CHUNK), 0)
                <= lax.broadcasted_iota(jnp.int32, (CHUNK, CHUNK), 1))
        dbs_tile = dmixed[0:CHUNK]
        for n in range(1, tm // CHUNK):
            dbs_tile = dbs_tile + dmixed[n * CHUNK:(n + 1) * CHUNK]
        dbs_acc[...] += dbs_tile
        for g in range(N_GROUPS):
            wmt = jnp.where(triu, wst_ref[g], 0.0).astype(BF16)
            cols = slice(g * gd, (g + 1) * gd)
            dw = jnp.zeros((CHUNK, CHUNK), F32)
            for n in range(tm // CHUNK):
                rows = slice(n * CHUNK, (n + 1) * CHUNK)
                dvn_scr[rows, cols] = jnp.dot(wmt, dmb[rows, cols], preferred_element_type=F32)
                dw = dw + lax.dot_general(dmb[rows, cols], vnb[rows, cols], NT_DIMS, preferred_element_type=F32)
            dws_ref[g] += jnp.where(tril, dw, 0.0)
        dvn = dvn_scr[...]
        vhat = f["vhat"]
        dln_ref[0:1, :] += _colsum(dvn * vhat)
        dln_ref[1:2, :] += _colsum(dvn)
        dvh = dvn * lng
        dv = f["rstd"] * (dvh - jnp.mean(dvh, axis=-1, keepdims=True)
                          - vhat * jnp.mean(dvh * vhat, axis=-1, keepdims=True))
        dp_ref[:, 4 * d:5 * d] = dv.astype(BF16)
        if on_to_x0 is not None:
            dh = jnp.zeros((tm, d), F32)
            for j in range(0, nj, 2):
                pair = jnp.concatenate([win_ref[j], win_ref[j + 1]], axis=1)
                dh = dh + lax.dot_general(dp_ref[:, j * wn:(j + 2) * wn], pair, NT_DIMS, preferred_element_type=F32)
            dx, dg = _rms_bwd(dh, x0_ref[...], g_ref[...])
            dx0_ref[...] = dx_ref[...] + dx
            dg_ref[0:1, :] += dg

        @pl.when(i == nt - 1)
        def _():
            for g in range(N_GROUPS):
                cols = slice(g * gd, (g + 1) * gd)
                s = jnp.sum(dbs_acc[:, cols], axis=1, keepdims=True)
                dbs_ref[:, cols] = jnp.broadcast_to(s, (CHUNK, gd))

    rev = lambda i: nt - 1 - i

    def halo(col):
        return pl.BlockSpec((HALO, d), lambda i: (jnp.maximum(rev(i) * (tm // HALO) - 1, 0), col))

    const2 = lambda i: (0, 0)
    const3 = lambda i: (0, 0, 0)
    row = lambda i: (rev(i), 0)
    in_specs = [pl.BlockSpec((tm, d), row), pl.BlockSpec((tm, 7 * d), row), halo(1), halo(2),
                _once((d, d), const2), _once((HALO, d), const2), _once((1, d), const2), _once((1, d), const2),
                _once((N_GROUPS, CHUNK, CHUNK), const3), _once((N_GROUPS, CHUNK, CHUNK), const3),
                _once((CHUNK, d), const2)]
    out_specs = [pl.BlockSpec((tm, 7 * d), row), pl.BlockSpec((HALO, d), const2), pl.BlockSpec((HALO, d), const2),
                 pl.BlockSpec((N_GROUPS, CHUNK, CHUNK), const3), pl.BlockSpec((CHUNK, d), const2)]
    out_shape = [jax.ShapeDtypeStruct((t, 7 * d), BF16), jax.ShapeDtypeStruct((HALO, d), F32),
                 jax.ShapeDtypeStruct((HALO, d), F32), jax.ShapeDtypeStruct((N_GROUPS, CHUNK, CHUNK), F32),
                 jax.ShapeDtypeStruct((CHUNK, d), F32)]
    args = [dx1, proj, proj, proj, wout, cw, lng, lnb, ws, wst, bias]
    if on_to_x0 is not None:
        in_specs += [_once((nj, d, wn), const3), pl.BlockSpec((tm, d), row), _once((1, d), const2)]
        out_specs += [pl.BlockSpec((tm, d), row), pl.BlockSpec((HALO, d), const2)]
        out_shape += [jax.ShapeDtypeStruct((t, d), F32), jax.ShapeDtypeStruct((HALO, d), F32)]
        args += list(on_to_x0)
    return pl.pallas_call(
        body, name=name, grid=(nt,), in_specs=in_specs, out_specs=out_specs, out_shape=out_shape,
        scratch_shapes=[pltpu.VMEM((tm, d), F32), pltpu.VMEM((tm, d), F32),
                        pltpu.VMEM((HALO, d), F32), pltpu.VMEM((CHUNK, d), F32)],
        compiler_params=_params(1),
    )(*args)


def dgrad_rms(dy, w, x, g, res, *, name, tm):
    t, d = x.shape
    n = w.shape[2]
    w = w.reshape(w.shape[0] // 2, 2, d, n)
    nj = w.shape[0]
    tm = min(tm, t)

    def body(dy_ref, w_ref, x_ref, g_ref, res_ref, dx_ref, dg_ref, acc_ref):
        i, j = pl.program_id(0), pl.program_id(1)

        @pl.when((i == 0) & (j == 0))
        def _():
            dg_ref[...] = jnp.zeros_like(dg_ref)

        pair = jnp.concatenate([w_ref[0], w_ref[1]], axis=1)
        part = lax.dot_general(dy_ref[...], pair, NT_DIMS, preferred_element_type=F32)

        @pl.when(j == 0)
        def _():
            acc_ref[...] = part

        @pl.when(j > 0)
        def _():
            acc_ref[...] += part

        @pl.when(j == nj - 1)
        def _():
            dx, dg = _rms_bwd(acc_ref[...], x_ref[...], g_ref[...])
            dx_ref[...] = res_ref[...] + dx
            dg_ref[0:1, :] += dg

    return pl.pallas_call(
        body, name=name, grid=(t // tm, nj),
        in_specs=[pl.BlockSpec((tm, 2 * n), lambda i, j: (i, j)),
                  pl.BlockSpec((None, 2, d, n), lambda i, j: (j, 0, 0, 0)),
                  pl.BlockSpec((tm, d), lambda i, j: (i, 0)),
                  pl.BlockSpec((1, d), lambda i, j: (0, 0)),
                  pl.BlockSpec((tm, d), lambda i, j: (i, 0))],
        out_specs=[pl.BlockSpec((tm, d), lambda i, j: (i, 0)), pl.BlockSpec((HALO, d), lambda i, j: (0, 0))],
        out_shape=[jax.ShapeDtypeStruct((t, d), F32), jax.ShapeDtypeStruct((HALO, d), F32)],
        scratch_shapes=[pltpu.VMEM((tm, d), F32)],
        compiler_params=_params(2),
    )(dy, w, x, g, res)


def wgrad(a, b, *, nj, a_mode, b_mode, name, tm, split=1):
    def describe(arr, mode):
        if mode == "full":
            return arr.shape[0], arr.shape[1], pl.BlockSpec((tm_, arr.shape[1]), lambda j, s: (s, 0))
        if mode == "cols":
            c = arr.shape[1] // nj
            return arr.shape[0], c, pl.BlockSpec((tm_, c), lambda j, s: (s, j))
        return arr.shape[1], arr.shape[2], pl.BlockSpec((None, tm_, arr.shape[2]), lambda j, s: (j, s, 0))

    t = a.shape[0] if a_mode != "lead" else a.shape[1]
    tm_ = min(tm, t)
    _, k, a_spec = describe(a, a_mode)
    _, n, b_spec = describe(b, b_mode)

    ns = t // tm_
    nc = n // split

    def body(a_ref, b_ref, o_ref, acc_ref):
        s = pl.program_id(1)
        part = lax.dot_general(a_ref[...], b_ref[...], TN_DIMS, preferred_element_type=F32)

        def finish(total):
            for q in range(split):
                o_ref[q] = total[:, q * nc:(q + 1) * nc].astype(BF16)

        if ns == 1:
            finish(part)
            return

        @pl.when(s == 0)
        def _():
            acc_ref[...] = part

        @pl.when((s > 0) & (s < ns - 1))
        def _():
            acc_ref[...] += part

        @pl.when(s == ns - 1)
        def _():
            finish(acc_ref[...] + part)

    return pl.pallas_call(
        body, name=name, grid=(nj, ns),
        in_specs=[a_spec, b_spec],
        out_specs=pl.BlockSpec((split, k, nc), lambda j, s: (j, 0, 0)),
        out_shape=jax.ShapeDtypeStruct((nj * split, k, nc), BF16),
        scratch_shapes=[pltpu.VMEM((k, n), F32)],
        compiler_params=_params(2),
    )(a, b)


def _adamw_math(w, g, m, v):
    m = ADAM_B1 * m + (1.0 - ADAM_B1) * g
    v = ADAM_B2 * v + (1.0 - ADAM_B2) * (g * g)
    m_hat = m / (1.0 - ADAM_B1 ** ADAM_STEP)
    v_hat = v / (1.0 - ADAM_B2 ** ADAM_STEP)
    delta = -ADAM_LR * (m_hat / (jnp.sqrt(v_hat) + ADAM_EPS) + ADAM_WD * w)
    return delta, m, v


def _row_tile(rows, at_most):
    if rows <= at_most:
        return rows
    return max(k for k in range(16, at_most + 1, 16) if rows % k == 0)


def _sum_in_device_order(ref):
    total = ref[0]
    for s in range(1, N_DEV):
        total = total + ref[s]
    return total


def adamw_sharded(me, own0, recv0, own1, recv1, w, m, v, *, name, tr):
    _, r, c = w.shape
    tr = _row_tile(r, tr)
    ni = r // tr

    def body(me_ref, o0_ref, r0_ref, o1_ref, r1_ref, w_ref, m_ref, v_ref, g_ref, d_ref, nm_ref, nv_ref):
        def finish(own_ref, recv_ref):
            g = None
            for s in range(N_DEV):
                term = jnp.where(me_ref[0] == s, own_ref[...], recv_ref[s]).astype(F32)
                g = term if g is None else g + term
            delta, nm, nv = _adamw_math(w_ref[...], g, m_ref[...], v_ref[...])
            g_ref[...] = g
            d_ref[...] = delta
            nm_ref[...] = nm
            nv_ref[...] = nv

        @pl.when(pl.program_id(0) == 0)
        def _():
            finish(o0_ref, r0_ref)

        @pl.when(pl.program_id(0) == 1)
        def _():
            finish(o1_ref, r1_ref)

    row0 = lambda l, i: i * (1 - l) + (ni - 1) * l
    row1 = lambda l, i: i * l
    lay = pl.BlockSpec((None, tr, c), lambda l, i, me_ref: (l, i, 0))
    grid_spec = pltpu.PrefetchScalarGridSpec(
        num_scalar_prefetch=1, grid=(2, ni),
        in_specs=[pl.BlockSpec((None, tr, c), lambda l, i, me_ref: (me_ref[0], row0(l, i), 0)),
                  pl.BlockSpec((N_DEV, tr, c), lambda l, i, me_ref: (0, row0(l, i), 0)),
                  pl.BlockSpec((None, tr, c), lambda l, i, me_ref: (me_ref[0], row1(l, i), 0)),
                  pl.BlockSpec((N_DEV, tr, c), lambda l, i, me_ref: (0, row1(l, i), 0)),
                  lay, lay, lay],
        out_specs=[lay, lay, lay, lay])
    return pl.pallas_call(
        body, name=name, grid_spec=grid_spec,
        out_shape=[jax.ShapeDtypeStruct(w.shape, F32)] * 4,
        compiler_params=_params(2),
    )(me, own0, recv0, own1, recv1, w, m, v)


def adamw_small(g, w, m, v, *, name):
    def body(g_ref, w_ref, m_ref, v_ref, d_ref, nm_ref, nv_ref):
        delta, nm, nv = _adamw_math(w_ref[...], g_ref[...], m_ref[...], v_ref[...])
        d_ref[...] = delta
        nm_ref[...] = nm
        nv_ref[...] = nv

    return pl.pallas_call(
        body, name=name,
        out_shape=[jax.ShapeDtypeStruct(w.shape, F32)] * 3,
        compiler_params=pltpu.CompilerParams(vmem_limit_bytes=VMEM_LIMIT_BYTES),
    )(g, w, m, v)


def sum_devices(parts, *, name, tr):
    _, r, c = parts.shape
    tr = min(tr, r)

    def body(p_ref, o_ref):
        o_ref[...] = _sum_in_device_order(p_ref)

    return pl.pallas_call(
        body, name=name, grid=(r // tr,),
        in_specs=[pl.BlockSpec((N_DEV, tr, c), lambda i: (0, i, 0))],
        out_specs=pl.BlockSpec((tr, c), lambda i: (i, 0)),
        out_shape=jax.ShapeDtypeStruct((r, c), F32),
        compiler_params=_params(1),
    )(parts)


def _my_place():
    return lax.axis_index("x"), lax.axis_index("y"), lax.axis_index("c")


def all_gather(arrays, after, *, name):
    n = len(arrays)

    def body(*refs):
        ins, outs = refs[:n], refs[n + 1:2 * n + 1]
        send_sems, recv_sems, local_sems = refs[2 * n + 1:]
        x, y, c = _my_place()
        me, sibling = (x, y, c), (x, y, 1 - c)
        chips = [(1 - x, y), (x, 1 - y), (1 - x, 1 - y)]
        waits = []
        for a in range(n):
            def slot(place, a=a):
                px, py, pc = place
                return outs[a].at[4 * px + 2 * py + pc]

            def copy(k, block, to, src=None, a=a, slot=slot):
                return pltpu.make_async_remote_copy(
                    src_ref=slot(block) if src is None else src, dst_ref=slot(block),
                    send_sem=send_sems.at[a, k], recv_sem=recv_sems.at[a, k],
                    device_id=to, device_id_type=MESH)

            mine = pltpu.make_async_copy(ins[a], slot(me), local_sems.at[a])
            mine.start()
            first = [copy(0, me, sibling, src=ins[a])]
            first += [copy(1 + j, me, (*chip, c), src=ins[a]) for j, chip in enumerate(chips)]
            for cp in first:
                cp.start()
            waits.append((copy, mine, first))
        sends = []
        for a in range(n):
            copy, mine, first = waits[a]
            passed = [copy(4 + j, (*chip, c), sibling) for j, chip in enumerate(chips)]
            for j, chip in enumerate(chips):
                copy(1 + j, (*chip, c), me).wait_recv()
                passed[j].start()
            sends.append(first + passed)
        for a in range(n):
            copy, mine, first = waits[a]
            copy(0, sibling, me).wait_recv()
            for j, chip in enumerate(chips):
                copy(4 + j, (*chip, 1 - c), me).wait_recv()
            for cp in sends[a]:
                cp.wait_send()
            mine.wait()

    return pl.pallas_call(
        body, name=name,
        in_specs=[ANY] * (n + 1), out_specs=[ANY] * n,
        out_shape=[jax.ShapeDtypeStruct((N_DEV,) + a.shape, a.dtype) for a in arrays],
        scratch_shapes=[pltpu.SemaphoreType.DMA((n, 7)), pltpu.SemaphoreType.DMA((n, 7)),
                        pltpu.SemaphoreType.DMA((n,))],
        compiler_params=pltpu.CompilerParams(has_side_effects=True),
    )(*arrays, after)


def _peer_place(r, x, y, c):
    fx, fy, fc = (r >> 2) & 1, (r >> 1) & 1, r & 1
    return (1 - x if fx else x, 1 - y if fy else y, 1 - c if fc else c)


def own_slot(me, w, layer, dtype, *, name, tr):
    _, r, c = w.shape
    tr = _row_tile(r, tr)

    def body(me_ref, w_ref, o_ref):
        o_ref[...] = w_ref[...].astype(dtype)

    grid_spec = pltpu.PrefetchScalarGridSpec(
        num_scalar_prefetch=1, grid=(r // tr,),
        in_specs=[pl.BlockSpec((None, tr, c), lambda i, me_ref: (layer, i, 0))],
        out_specs=pl.BlockSpec((None, tr, c), lambda i, me_ref: (me_ref[0], i, 0)))
    return pl.pallas_call(
        body, name=name, grid_spec=grid_spec,
        out_shape=jax.ShapeDtypeStruct((N_DEV, r, c), dtype),
        compiler_params=_params(1),
    )(me, w)


EXCHANGES = {
    "scatter": [(0, r) for r in range(1, N_DEV)],
    "gather": [(0, r) for r in range(1, N_DEV)],
    "gather_chips": [(0, r) for r in (1, 2, 4, 6)],
    "gather_forward": [(q, 1) for q in (2, 4, 6)],
}


def _split_copy(k, entry, src, land, send_sem, recv_sem, arriving):
    slot, peer = entry
    x, y, c = _my_place()

    def index(relation):
        px, py, pc = _peer_place(relation, x, y, c)
        return 4 * px + 2 * py + pc

    return pltpu.make_async_remote_copy(
        src_ref=land.at[index(slot)] if src is None else src.at[index(peer)],
        dst_ref=land.at[index(slot ^ peer if arriving else slot)],
        send_sem=send_sem.at[k], recv_sem=recv_sem.at[k],
        device_id=_peer_place(peer, x, y, c), device_id_type=MESH)


def start_copies(srcs, lands, *, mode, name, after=None):
    n = len(lands)
    entries = EXCHANGES[mode]
    bufs = (list(srcs) if srcs is not None else []) + list(lands)
    nb = len(bufs)

    def body(*refs):
        src = refs[:n] if srcs is not None else [None] * n
        land = refs[nb - n:nb]
        outs = refs[nb + len(extra):]
        send_sems, recv_sems = outs[:n], outs[n:2 * n]
        token = outs[2 * n + nb]
        for a in range(n):
            for k, entry in enumerate(entries):
                _split_copy(k, entry, src[a], land[a], send_sems[a], recv_sems[a], False).start()
        token[...] = jnp.zeros_like(token)

    extra = [] if after is None else [after]
    outs = pl.pallas_call(
        body, name=name,
        in_specs=[HBM_SPEC] * nb + [ANY] * len(extra),
        out_specs=[SEM_SPEC] * (2 * n) + [HBM_SPEC] * nb + [pl.BlockSpec(memory_space=pltpu.VMEM)],
        out_shape=([pltpu.SemaphoreType.DMA((len(entries),))] * (2 * n)
                   + [pltpu.HBM(a.shape, a.dtype) for a in bufs]
                   + [jax.ShapeDtypeStruct((8, 128), F32)]),
        input_output_aliases={i: 2 * n + i for i in range(nb)},
        compiler_params=pltpu.CompilerParams(has_side_effects=DATAFLOW),
    )(*[pltpu.with_memory_space_constraint(a, pltpu.HBM) for a in bufs], *extra)
    thru = list(outs[2 * n:2 * n + nb])
    return dict(send=outs[:n], recv=outs[n:2 * n], src=thru[:n] if srcs is not None else None, land=thru[nb - n:],
                token=outs[2 * n + nb], mode=mode)


def finish_copies(started, which, after, *, name):
    n = len(which)
    entries = EXCHANGES[started["mode"]]
    has_src = started["src"] is not None
    bufs = ([started["src"][i] for i in which] if has_src else []) + [started["land"][i] for i in which]
    nb = len(bufs)

    def body(*refs):
        src = refs[:n] if has_src else [None] * n
        land = refs[nb - n:nb]
        send_sems, recv_sems = refs[nb:nb + n], refs[nb + n:nb + 2 * n]
        for a in range(n):
            for k, entry in enumerate(entries):
                cp = _split_copy(k, entry, src[a], land[a], send_sems[a], recv_sems[a], True)
                cp.wait_send()
                cp.wait_recv()

    outs = pl.pallas_call(
        body, name=name,
        in_specs=[HBM_SPEC] * nb + [SEM_SPEC] * (2 * n) + [ANY],
        out_specs=[HBM_SPEC] * nb,
        out_shape=[pltpu.HBM(a.shape, a.dtype) for a in bufs],
        input_output_aliases={i: i for i in range(nb)},
        compiler_params=pltpu.CompilerParams(has_side_effects=DATAFLOW),
    )(*bufs, *[started["send"][i] for i in which], *[started["recv"][i] for i in which], after)
    return (list(outs[:n]) if has_src else None), list(outs[nb - n:])


def _pad_rows(a, rows):
    pad = [(0, 0)] * a.ndim
    pad[-2] = (0, rows - a.shape[-2])
    return jnp.pad(a, pad)


def kernel(x, mix_norm_g, w_in, conv_a_w, ln_v_g, ln_v_b, w_s, b_s, w_out, ffn_norm_g, w_up, conv_ffn_w, w_down, final_norm_g, loss_target, m_mix_norm_g, m_w_in, m_conv_a_w, m_ln_v_g, m_ln_v_b, m_w_s, m_b_s, m_w_out, m_ffn_norm_g, m_w_up, m_conv_ffn_w, m_w_down, m_final_norm_g, v_mix_norm_g, v_w_in, v_conv_a_w, v_ln_v_g, v_ln_v_b, v_w_s, v_b_s, v_w_out, v_ffn_norm_g, v_w_up, v_conv_ffn_w, v_w_down, v_final_norm_g):
    nb, seq, d = x.shape
    t = nb * seq
    depth = w_in.shape[0]
    f = w_up.shape[2]
    me = 4 * lax.axis_index("x") + 2 * lax.axis_index("y") + lax.axis_index("c")
    xt = x.reshape(t, d)
    tgt = loss_target.reshape(t, d)

    conv_pack = jnp.concatenate([_pad_rows(conv_a_w, HALO), _pad_rows(conv_ffn_w, HALO)], axis=-1)
    me_arr = me.astype(jnp.int32).reshape(1)
    w_up_t, m_w_up_t, v_w_up_t = (jnp.swapaxes(a, 1, 2) for a in (w_up, m_w_up, v_w_up))
    zones, slot_of = [], {}
    for l in range(depth):
        for key, w in (("win", w_in), ("conv", None), ("wout", w_out), ("wup", w_up_t), ("wd", w_down)):
            if key == "conv":
                if l == 0:
                    slot_of["conv"] = len(zones)
                    packed = conv_pack.reshape(1, depth * HALO, conv_pack.shape[-1])
                    zones.append(own_slot(me_arr, packed, 0, F32, name="own_slot_conv", tr=256))
                continue
            slot_of[key, l] = len(zones)
            zones.append(own_slot(me_arr, w, l, BF16, name=f"own_slot_{key}_{l}", tr=256))
    first = [slot_of["win", 0], slot_of["wout", 0], slot_of["conv"]]
    rest = [i for i in range(len(zones)) if i not in first]
    to_chips = start_copies(None, [zones[i] for i in first], mode="gather_chips", name="gather_first_chips")
    gathering = start_copies(None, [zones[i] for i in rest], mode="gather", name="gather_start", after=to_chips["token"])
    _, at_chips = finish_copies(to_chips, [0, 1, 2], gathering["token"], name="wait_first_chips")
    to_sibling = start_copies(None, at_chips, mode="gather_forward", name="gather_first_forward")

    def gathered(keys, after, name):
        return finish_copies(gathering, [rest.index(slot_of[k]) for k in keys], after, name=name)[1]

    saved, layers = [], []
    cur = xt
    for l in range(depth):
        p = dict(mix_g=mix_norm_g[l][None], ffn_g=ffn_norm_g[l][None], lng=ln_v_g[l][None], lnb=ln_v_b[l][None],
                 ws=w_s[l], wst=jnp.swapaxes(w_s[l], 1, 2),
                 bias=jnp.repeat(b_s[l].T, d // N_GROUPS, axis=1))
        if l == 0:
            _, (p["win"], wout_g, conv_g) = finish_copies(to_sibling, [0, 1, 2], to_sibling["token"],
                                                          name=f"wait_w_mixer_{l}")
            conv_g = conv_g.reshape(N_DEV, depth, HALO, -1)
            ca = conv_g.shape[-1] - f
        else:
            p["win"], wout_g = gathered([("win", l), ("wout", l)], after, f"wait_w_mixer_{l}")
        p["wout"] = wout_g.reshape(d, d)
        p["cw_a"] = jnp.transpose(conv_g[:, l, :, :ca], (1, 0, 2)).reshape(HALO, d)
        p["cw_f"] = conv_g[:, l, :, ca:]
        h, proj, merged, x1 = mixer_fwd(cur, p["mix_g"], p["win"], p["wout"], p["cw_a"], p["lng"], p["lnb"], p["ws"],
                                        p["bias"], seq=seq, name=f"mixer_fwd_{l}", tm=256)
        p["wup"], wd_g = gathered([("wup", l), ("wd", l)], merged, f"wait_w_ffn_{l}")
        p["wd"] = wd_g.reshape(N_DEV // 2, 2 * wd_g.shape[1], d)
        head = (final_norm_g[None], tgt) if l == depth - 1 else None
        h2, up0, upc, act, x2, *of_loss = ffn_fwd(x1, p["ffn_g"], p["wup"], p["wd"], p["cw_f"],
                                                  seq=seq, name=f"ffn_fwd_{l}", tm=256, head=head)
        saved.append(dict(x0=cur, h=h, proj=proj, merged=merged, x1=x1, h2=h2, up0=up0, upc=upc, act=act))
        layers.append(p)
        cur, after = x2, act
    dx = cur
    d_final_g, loss_tile = of_loss

    def exchange(parts, name):
        return start_copies(parts, [lax.empty(a.shape, a.dtype) for a in parts], mode="scatter", name=name)

    def tied(g, started):
        return g + started["token"][0:1, 0:1]

    part = [None] * depth
    mix_ex = None
    for l in reversed(range(depth)):
        p, s = layers[l], saved[l]
        ffn_g = p["ffn_g"] if mix_ex is None else tied(p["ffn_g"], mix_ex)
        dup0, dcw_f, dx1, d_ffn_g = ffn_bwd(dx, s["up0"], s["upc"], p["wd"], p["cw_f"], p["wup"], s["x1"], ffn_g,
                                            seq=seq, name=f"ffn_bwd_{l}", tm=256)
        g_wd = wgrad(s["act"], dx, nj=N_DEV // 2, a_mode="lead", b_mode="full", name=f"wgrad_down_{l}", tm=2048)
        g_wup = wgrad(dup0, s["h2"], nj=N_DEV, a_mode="lead", b_mode="full", name=f"wgrad_up_{l}", tm=2048)
        ffn_ex = exchange([g_wd.reshape(N_DEV, g_wd.shape[1] // 2, d), g_wup], f"exchange_ffn_{l}")
        fused = l > 0
        dproj, dcw_a, dln, dws, dbs, *to_x0 = mixer_bwd(
            dx1, s["proj"], p["wout"], tied(p["cw_a"], ffn_ex), p["lng"], p["lnb"], p["ws"], p["wst"], p["bias"],
            seq=seq, name=f"mixer_bwd_{l}", tm=256, on_to_x0=(p["win"], s["x0"], p["mix_g"]) if fused else None)
        g_wout = wgrad(s["merged"], dx1, nj=1, a_mode="full", b_mode="full", name=f"wgrad_out_{l}", tm=2048)
        g_win = wgrad(s["h"], dproj, nj=N_DEV // 2, a_mode="full", b_mode="cols", name=f"wgrad_in_{l}", tm=2048, split=2)
        mix_ex = exchange([g_wout.reshape(N_DEV, d // N_DEV, d), g_win], f"exchange_mix_{l}")
        if fused:
            dx, d_mix_g = to_x0
        else:
            dx, d_mix_g = dgrad_rms(dproj, p["win"], s["x0"], tied(p["mix_g"], mix_ex), dx1,
                                    name=f"dgrad_in_{l}", tm=1024)
        part[l] = dict(
            ffn_ex=ffn_ex, mix_ex=mix_ex,
            small=jnp.concatenate([
                dws.reshape(N_GROUPS * CHUNK * CHUNK // d, d),
                d_mix_g[0:1], d_ffn_g[0:1], dln[0:2], dcw_a[0:3],
                dbs[:, ::d // N_GROUPS].T.reshape(1, d)], axis=0),
            cw_f=dcw_f.reshape(N_DEV * HALO, f))
    grad_x = dx.reshape(nb, seq, d)

    own, recv = {}, {}

    def arrived(l, ex, keys, after):
        srcs, lands = finish_copies(part[l][ex], [0, 1], after, name=f"wait_{ex}_{l}")
        for k, key in enumerate(keys):
            own[key, l], recv[key, l] = srcs[k], lands[k]
        return lands[1]

    def big(key, w, m, v, name):
        return adamw_sharded(me_arr, own[key, 0], recv[key, 0], own[key, 1], recv[key, 1], w, m, v, name=name, tr=256)

    after = grad_x
    for l in reversed(range(depth)):
        after = arrived(l, "ffn_ex", ("wd", "wup"), after)
        if l > 0:
            after = arrived(l, "mix_ex", ("wout", "win"), after)
    u_wd = big("wd", w_down, m_w_down, v_w_down, "adamw_w_down")
    u_wup = tuple(jnp.swapaxes(a, 1, 2) for a in big("wup", w_up_t, m_w_up_t, v_w_up_t, "adamw_w_up"))

    loss_row = jnp.zeros((1, d), F32).at[0, 0].set(loss_tile[0, 0])
    small = jnp.concatenate([part[l]["small"] for l in range(depth)] + [d_final_g[0:1], loss_row], axis=0)
    small = _pad_rows(small, -(-small.shape[0] // 8) * 8)
    cwf = jnp.concatenate([part[l]["cw_f"] for l in range(depth)], axis=0)
    small_all, cwf_all = all_gather([small, cwf], u_wup[0], name="gather_small_grads")
    small_sum = sum_devices(small_all, name="sum_small", tr=512)
    cwf_sum = sum_devices(cwf_all, name="sum_conv_ffn", tr=512)


    rows_ws = N_GROUPS * CHUNK * CHUNK // d
    per_layer = rows_ws + 8
    def small_of(l, a, b):
        return small_sum[l * per_layer + rows_ws + a:l * per_layer + rows_ws + b]
    g_ws = jnp.stack([small_sum[l * per_layer:l * per_layer + rows_ws].reshape(N_GROUPS, CHUNK, CHUNK)
                      for l in range(depth)])
    g_mix = jnp.concatenate([small_of(l, 0, 1) for l in range(depth)])
    g_ffn = jnp.concatenate([small_of(l, 1, 2) for l in range(depth)])
    g_lng = jnp.concatenate([small_of(l, 2, 3) for l in range(depth)])
    g_lnb = jnp.concatenate([small_of(l, 3, 4) for l in range(depth)])
    g_cwa_full = jnp.stack([small_of(l, 4, 7) for l in range(depth)])
    g_cwa = lax.dynamic_slice_in_dim(g_cwa_full, me * ca, ca, axis=2)
    g_bs = jnp.stack([small_of(l, 7, 8).reshape(N_GROUPS, CHUNK) for l in range(depth)])
    g_final = small_sum[depth * per_layer]
    loss = small_sum[depth * per_layer + 1, 0]
    cwf_sum = cwf_sum.reshape(depth, N_DEV, HALO, f)
    g_cwf = lax.dynamic_index_in_dim(cwf_sum, me, axis=1, keepdims=False)[:, :3]

    arrived(0, "mix_ex", ("wout", "win"), small_sum)
    u_wout = big("wout", w_out, m_w_out, v_w_out, "adamw_w_out")
    u_win = big("win", w_in, m_w_in, v_w_in, "adamw_w_in")

    def small_update(g, w, m, v, name):
        shape = w.shape
        two_d = (-1, shape[-1]) if w.ndim > 1 else (1, shape[0])
        out = adamw_small(g.reshape(two_d), w.reshape(two_d), m.reshape(two_d), v.reshape(two_d), name=name)
        return (g.reshape(shape),) + tuple(o.reshape(shape) for o in out)

    u_mix = small_update(g_mix, mix_norm_g, m_mix_norm_g, v_mix_norm_g, "adamw_mix_norm_g")
    u_cwa = small_update(g_cwa, conv_a_w, m_conv_a_w, v_conv_a_w, "adamw_conv_a_w")
    u_lng = small_update(g_lng, ln_v_g, m_ln_v_g, v_ln_v_g, "adamw_ln_v_g")
    u_lnb = small_update(g_lnb, ln_v_b, m_ln_v_b, v_ln_v_b, "adamw_ln_v_b")
    u_ws = small_update(g_ws, w_s, m_w_s, v_w_s, "adamw_w_s")
    u_bs = small_update(g_bs, b_s, m_b_s, v_b_s, "adamw_b_s")
    u_ffn = small_update(g_ffn, ffn_norm_g, m_ffn_norm_g, v_ffn_norm_g, "adamw_ffn_norm_g")
    u_cwf = small_update(g_cwf, conv_ffn_w, m_conv_ffn_w, v_conv_ffn_w, "adamw_conv_ffn_w")
    u_final = small_update(g_final, final_norm_g, m_final_norm_g, v_final_norm_g, "adamw_final_norm_g")

    ordered = [u_mix, u_win, u_cwa, u_lng, u_lnb, u_ws, u_bs, u_wout, u_ffn, u_wup, u_cwf, u_wd, u_final]
    return (loss, grad_x, *[u[0] for u in ordered], *[u[1] for u in ordered],
            *[u[2] for u in ordered], *[u[3] for u in ordered])
```

```python
import functools

import jax
import jax.numpy as jnp
from jax import lax
from jax.experimental import pallas as pl
from jax.experimental.pallas import tpu as pltpu

EPS = 1e-6
CHUNK = 128
N_GROUPS = 8
N_DEV = 8
HALO = 8
ADAM_LR = 0.001
ADAM_B1 = 0.9
ADAM_B2 = 0.999
ADAM_EPS = 1e-08
ADAM_WD = 0.01
ADAM_STEP = 10
VMEM_LIMIT_BYTES = 56 * 1024 * 1024
F32 = jnp.float32
BF16 = jnp.bfloat16
MESH = pl.DeviceIdType.MESH
ANY = pl.BlockSpec(memory_space=pl.ANY)
HBM_SPEC = pl.BlockSpec(memory_space=pltpu.HBM)
SEM_SPEC = pl.BlockSpec(memory_space=pltpu.SEMAPHORE)
DATAFLOW = pltpu.SideEffectType.DATAFLOW_SIDE_EFFECTING
NT_DIMS = (((1,), (1,)), ((), ()))
TN_DIMS = (((0,), (0,)), ((), ()))


def _params(n_grid_axes):
    return pltpu.CompilerParams(dimension_semantics=("arbitrary",) * n_grid_axes,
                                vmem_limit_bytes=VMEM_LIMIT_BYTES)


def _shift_down(cur, prev8, k):
    rolled = pltpu.roll(cur, k, 0)
    prolled = pltpu.roll(prev8, k, 0)
    row = lax.broadcasted_iota(jnp.int32, prev8.shape, 0)
    head = jnp.where(row < k, prolled, rolled[:HALO])
    return jnp.concatenate([head, rolled[HALO:]], axis=0)


def _shift_up(cur, next8, k):
    tm = cur.shape[0]
    rolled = pltpu.roll(cur, tm - k, 0)
    nrolled = pltpu.roll(next8, HALO - k, 0)
    row = lax.broadcasted_iota(jnp.int32, next8.shape, 0)
    tail = jnp.where(row >= HALO - k, nrolled, rolled[tm - HALO:])
    return jnp.concatenate([rolled[:tm - HALO], tail], axis=0)


def _conv_fwd(cur, prev8, cw):
    s1 = _shift_down(cur, prev8, 1)
    s2 = _shift_down(cur, prev8, 2)
    y = s2 * cw[0:1, :] + s1 * cw[1:2, :] + cur * cw[2:3, :]
    return y, s1, s2


def _conv_bwd(d, next8, cw):
    u1 = _shift_up(d, next8, 1)
    u2 = _shift_up(d, next8, 2)
    return d * cw[2:3, :] + u1 * cw[1:2, :] + u2 * cw[0:1, :], u1, u2


def _colsum(a):
    return jnp.sum(a, axis=0, keepdims=True)


def _rms_stats(xv):
    r = lax.rsqrt(jnp.mean(xv * xv, axis=-1, keepdims=True) + EPS)
    return r, xv * r


def _rms_bwd(dh, xv, g):
    r, n = _rms_stats(xv)
    dn = dh * g
    dx = r * (dn - n * jnp.mean(dn * n, axis=-1, keepdims=True))
    return dx, _colsum(dh * n)


def _mixer_forward(p_ref, cprev, xiprev, cw, lng, lnb, ws_ref, bias_ref, mixed_scr, d):
    tm = p_ref.shape[0]
    b = p_ref[:, 0:d]
    c = p_ref[:, d:2 * d]
    xi = p_ref[:, 2 * d:3 * d]
    u = p_ref[:, 3 * d:4 * d]
    v = p_ref[:, 4 * d:5 * d]
    sa = jax.nn.sigmoid(p_ref[:, 5 * d:6 * d])
    sb = jax.nn.sigmoid(p_ref[:, 6 * d:7 * d])
    cx = c * xi
    conv, s1, s2 = _conv_fwd(cx, cprev * xiprev, cw)
    ya = b * conv
    mu = jnp.mean(v, axis=-1, keepdims=True)
    xc = v - mu
    rstd = lax.rsqrt(jnp.mean(xc * xc, axis=-1, keepdims=True) + EPS)
    vhat = xc * rstd
    vnb = (vhat * lng + lnb).astype(BF16)
    tril = (lax.broadcasted_iota(jnp.int32, (CHUNK, CHUNK), 0)
            >= lax.broadcasted_iota(jnp.int32, (CHUNK, CHUNK), 1))
    gd = d // N_GROUPS
    for g in range(N_GROUPS):
        wm = jnp.where(tril, ws_ref[g], 0.0).astype(BF16)
        cols = slice(g * gd, (g + 1) * gd)
        for n in range(tm // CHUNK):
            rows = slice(n * CHUNK, (n + 1) * CHUNK)
            mixed_scr[rows, cols] = (jnp.dot(wm, vnb[rows, cols], preferred_element_type=F32)
                                     + bias_ref[:, cols])
    mixed = mixed_scr[...]
    yb = u * mixed
    merged = sa * ya + sb * yb
    return dict(b=b, c=c, xi=xi, u=u, sa=sa, sb=sb, cx=cx, s1=s1, s2=s2, conv=conv, ya=ya,
                rstd=rstd, vhat=vhat, vnb=vnb, mixed=mixed, yb=yb, merged=merged, tril=tril)


def _once(block_shape, index_map):
    return pl.BlockSpec(block_shape, index_map, pipeline_mode=pl.Buffered(1))


def mixer_fwd(x, g, win, wout, cw, lng, lnb, ws, bias, *, seq, name, tm):
    t, d = x.shape
    nj, _, n = win.shape
    tm = min(tm, seq)
    tiles_per_seq = seq // tm

    def body(x_ref, g_ref, win_ref, wout_ref, cw_ref, lng_ref, lnb_ref, ws_ref, bias_ref,
             h_ref, p_ref, merged_ref, x1_ref, mixed_scr, carry_ref):
        @pl.when(pl.program_id(0) == 0)
        def _():
            carry_ref[...] = jnp.zeros_like(carry_ref)

        keep = jnp.where(pl.program_id(0) % tiles_per_seq == 0, 0.0, 1.0)
        xv = x_ref[...]
        _, nrm = _rms_stats(xv)
        hb = (nrm * g_ref[...]).astype(BF16)
        h_ref[...] = hb
        for j in range(0, nj, 2):
            pair = jnp.concatenate([win_ref[j], win_ref[j + 1]], axis=1)
            p_ref[:, j * n:(j + 2) * n] = jnp.dot(hb, pair, preferred_element_type=F32)
        f = _mixer_forward(p_ref, carry_ref[...] * keep, 1.0, cw_ref[...], lng_ref[...],
                           lnb_ref[...], ws_ref, bias_ref, mixed_scr, d)
        carry_ref[...] = f["cx"][tm - HALO:]
        mb = f["merged"].astype(BF16)
        merged_ref[...] = mb
        x1_ref[...] = xv + jnp.dot(mb, wout_ref[...], preferred_element_type=F32)

    const2 = lambda i: (0, 0)
    const3 = lambda i: (0, 0, 0)
    row = lambda i: (i, 0)
    return pl.pallas_call(
        body, name=name, grid=(t // tm,),
        in_specs=[pl.BlockSpec((tm, d), row),
                  _once((1, d), const2),
                  _once((nj, d, n), const3),
                  _once((d, d), const2),
                  _once((HALO, d), const2),
                  _once((1, d), const2),
                  _once((1, d), const2),
                  _once((N_GROUPS, CHUNK, CHUNK), const3),
                  _once((CHUNK, d), const2)],
        out_specs=[pl.BlockSpec((tm, d), row), pl.BlockSpec((tm, nj * n), row),
                   pl.BlockSpec((tm, d), row), pl.BlockSpec((tm, d), row)],
        out_shape=[jax.ShapeDtypeStruct((t, d), BF16), jax.ShapeDtypeStruct((t, nj * n), F32),
                   jax.ShapeDtypeStruct((t, d), BF16), jax.ShapeDtypeStruct((t, d), F32)],
        scratch_shapes=[pltpu.VMEM((tm, d), F32), pltpu.VMEM((HALO, d), F32)],
        compiler_params=_params(1),
    )(x, g, win, wout, cw, lng, lnb, ws, bias)


def ffn_fwd(x1, g, wup, wd, cw, *, seq, name, tm, head=None):
    t, d = x1.shape
    nj, f, _ = wup.shape
    half = nj // 2
    tm = min(tm, seq)
    tiles_per_seq = seq // tm

    def body(x1_ref, g_ref, wup_ref, wd_ref, cw_ref, *rest):
        if head is None:
            h2_ref, up_ref, upc_ref, act_ref, x2_ref, carry_ref = rest
        else:
            gf_ref, tgt_ref, h2_ref, up_ref, upc_ref, act_ref, x2_ref, dgf_ref, loss_ref, carry_ref = rest

        @pl.when(pl.program_id(0) == 0)
        def _():
            carry_ref[...] = jnp.zeros_like(carry_ref)
            if head is not None:
                dgf_ref[...] = jnp.zeros_like(dgf_ref)
                loss_ref[...] = jnp.zeros_like(loss_ref)

        keep = jnp.where(pl.program_id(0) % tiles_per_seq == 0, 0.0, 1.0)
        xv = x1_ref[...]
        _, nrm = _rms_stats(xv)
        hb = (nrm * g_ref[...]).astype(BF16)
        h2_ref[...] = hb

        for j in range(nj):
            up_ref[j] = lax.dot_general(hb, wup_ref[j], NT_DIMS, preferred_element_type=F32)

        def conv_of(j):
            up0 = up_ref[j]
            y, _, _ = _conv_fwd(up0, carry_ref[j] * keep, cw_ref[j])
            carry_ref[j] = up0[tm - HALO:]
            upc_ref[j] = y.astype(BF16)
            return y

        acc = xv
        for k in range(half):
            a = (jax.nn.silu(conv_of(k)) * conv_of(k + half)).astype(BF16)
            act_ref[k] = a
            acc = acc + jnp.dot(a, wd_ref[k], preferred_element_type=F32)
        if head is None:
            x2_ref[...] = acc
        else:
            gv = gf_ref[...]
            r, n = _rms_stats(acc)
            err = n * gv - tgt_ref[...]
            loss_ref[...] += 0.5 * jnp.sum(jnp.mean(err * err, axis=-1, keepdims=True))
            dy = err * (1.0 / d)
            dn = dy * gv
            x2_ref[...] = r * (dn - n * jnp.mean(dn * n, axis=-1, keepdims=True))
            dgf_ref[0:1, :] += _colsum(dy * n)

    const3 = lambda i: (0, 0, 0)
    row = lambda i: (i, 0)
    in_specs = [pl.BlockSpec((tm, d), row), _once((1, d), lambda i: (0, 0)), _once((nj, f, d), const3),
                _once((half, f, d), const3), _once((nj, HALO, f), const3)]
    out_specs = [pl.BlockSpec((tm, d), row), pl.BlockSpec((nj, tm, f), lambda i: (0, i, 0)),
                 pl.BlockSpec((nj, tm, f), lambda i: (0, i, 0)), pl.BlockSpec((half, tm, f), lambda i: (0, i, 0)),
                 pl.BlockSpec((tm, d), row)]
    out_shape = [jax.ShapeDtypeStruct((t, d), BF16), jax.ShapeDtypeStruct((nj, t, f), F32),
                 jax.ShapeDtypeStruct((nj, t, f), BF16), jax.ShapeDtypeStruct((half, t, f), BF16),
                 jax.ShapeDtypeStruct((t, d), F32)]
    args = [x1, g, wup, wd, cw]
    if head is not None:
        in_specs += [_once((1, d), lambda i: (0, 0)), pl.BlockSpec((tm, d), row)]
        out_specs += [pl.BlockSpec((HALO, d), lambda i: (0, 0)), pl.BlockSpec((8, 128), lambda i: (0, 0))]
        out_shape += [jax.ShapeDtypeStruct((HALO, d), F32), jax.ShapeDtypeStruct((8, 128), F32)]
        args += list(head)
    return pl.pallas_call(
        body, name=name, grid=(t // tm,), in_specs=in_specs, out_specs=out_specs, out_shape=out_shape,
        scratch_shapes=[pltpu.VMEM((nj, HALO, f), F32)],
        compiler_params=_params(1),
    )(*args)


def ffn_bwd(dx2, up0, upc, wd, cw, wup, x1, g, *, seq, name, tm):
    t, d = dx2.shape
    nj, _, f = up0.shape
    half = nj // 2
    tm = min(tm, seq)
    tiles_per_seq = seq // tm
    nt = t // tm

    def body(dx_ref, up_ref, upc_ref, wd_ref, cw_ref, wup_ref, x1_ref, g_ref,
             dup_ref, dcw_ref, dx1_ref, dg_ref, carry_ref):
        i = pl.program_id(0)
        tile = nt - 1 - i

        @pl.when(i == 0)
        def _():
            dcw_ref[...] = jnp.zeros_like(dcw_ref)
            dg_ref[...] = jnp.zeros_like(dg_ref)
            carry_ref[...] = jnp.zeros_like(carry_ref)

        keep_next = jnp.where(tile % tiles_per_seq == tiles_per_seq - 1, 0.0, 1.0)
        dx2v = dx_ref[...]
        dxb = dx2v.astype(BF16)
        dh = [jnp.zeros((tm, d), F32)]

        def through_conv(j, dup):
            next8 = carry_ref[j] * keep_next
            carry_ref[j] = dup[:HALO]
            dup0, u1, u2 = _conv_bwd(dup, next8, cw_ref[j])
            up0 = up_ref[j]
            dcw_ref[j, 0:1, :] += _colsum(u2 * up0)
            dcw_ref[j, 1:2, :] += _colsum(u1 * up0)
            dcw_ref[j, 2:3, :] += _colsum(dup * up0)
            dup0 = dup0.astype(BF16)
            dup_ref[j] = dup0
            dh[0] = dh[0] + jnp.dot(dup0, wup_ref[j], preferred_element_type=F32)

        dacts = [lax.dot_general(dxb, wd_ref[k], NT_DIMS, preferred_element_type=F32) for k in range(half)]
        for k in range(half):
            gate = upc_ref[k].astype(F32)
            val = upc_ref[k + half].astype(F32)
            dact = dacts[k]
            sg = jax.nn.sigmoid(gate)
            through_conv(k, dact * val * (sg * (1.0 + gate * (1.0 - sg))))
            through_conv(k + half, dact * (gate * sg))

        dx, dg = _rms_bwd(dh[0], x1_ref[...], g_ref[...])
        dx1_ref[...] = dx2v + dx
        dg_ref[0:1, :] += dg

    rev = lambda i: nt - 1 - i
    return pl.pallas_call(
        body, name=name, grid=(nt,),
        in_specs=[pl.BlockSpec((tm, d), lambda i: (rev(i), 0)),
                  pl.BlockSpec((nj, tm, f), lambda i: (0, rev(i), 0)),
                  pl.BlockSpec((nj, tm, f), lambda i: (0, rev(i), 0)),
                  _once((half, f, d), lambda i: (0, 0, 0)),
                  _once((nj, HALO, f), lambda i: (0, 0, 0)),
                  _once((nj, f, d), lambda i: (0, 0, 0)),
                  pl.BlockSpec((tm, d), lambda i: (rev(i), 0)),
                  _once((1, d), lambda i: (0, 0))],
        out_specs=[pl.BlockSpec((nj, tm, f), lambda i: (0, rev(i), 0)),
                   pl.BlockSpec((nj, HALO, f), lambda i: (0, 0, 0)),
                   pl.BlockSpec((tm, d), lambda i: (rev(i), 0)),
                   pl.BlockSpec((HALO, d), lambda i: (0, 0))],
        out_shape=[jax.ShapeDtypeStruct((nj, t, f), BF16), jax.ShapeDtypeStruct((nj, HALO, f), F32),
                   jax.ShapeDtypeStruct((t, d), F32), jax.ShapeDtypeStruct((HALO, d), F32)],
        scratch_shapes=[pltpu.VMEM((nj, HALO, f), F32)],
        compiler_params=_params(1),
    )(dx2, up0, upc, wd, cw, wup, x1, g)


def mixer_bwd(dx1, proj, wout, cw, lng, lnb, ws, wst, bias, *, seq, name, tm, on_to_x0=None):
    t, d = dx1.shape
    tm = min(tm, seq)
    tiles_per_seq = seq // tm
    nt = t // tm
    gd = d // N_GROUPS
    if on_to_x0 is not None:
        nj, _, wn = on_to_x0[0].shape

    def body(dx_ref, p_ref, cprev_ref, xiprev_ref, wout_ref, cw_ref, lng_ref, lnb_ref, ws_ref, wst_ref, bias_ref,
             *rest):
        if on_to_x0 is None:
            dp_ref, dcw_ref, dln_ref, dws_ref, dbs_ref, mixed_scr, dvn_scr, carry_ref, dbs_acc = rest
        else:
            (win_ref, x0_ref, g_ref, dp_ref, dcw_ref, dln_ref, dws_ref, dbs_ref, dx0_ref, dg_ref,
             mixed_scr, dvn_scr, carry_ref, dbs_acc) = rest
        i = pl.program_id(0)
        tile = nt - 1 - i

        @pl.when(i == 0)
        def _():
            dcw_ref[...] = jnp.zeros_like(dcw_ref)
            dln_ref[...] = jnp.zeros_like(dln_ref)
            dws_ref[...] = jnp.zeros_like(dws_ref)
            dbs_acc[...] = jnp.zeros_like(dbs_acc)
            if on_to_x0 is not None:
                dg_ref[...] = jnp.zeros_like(dg_ref)
            carry_ref[...] = jnp.zeros_like(carry_ref)

        keep_prev = jnp.where(tile % tiles_per_seq == 0, 0.0, 1.0)
        keep_next = jnp.where(tile % tiles_per_seq == tiles_per_seq - 1, 0.0, 1.0)
        cw = cw_ref[...]
        lng = lng_ref[...]
        f = _mixer_forward(p_ref, cprev_ref[...] * keep_prev, xiprev_ref[...], cw, lng, lnb_ref[...],
                           ws_ref, bias_ref, mixed_scr, d)
        dmerged = lax.dot_general(dx_ref[...].astype(BF16), wout_ref[...], NT_DIMS, preferred_element_type=F32)
        sa, sb = f["sa"], f["sb"]
        dp_ref[:, 5 * d:6 * d] = (dmerged * f["ya"] * (sa * (1.0 - sa))).astype(BF16)
        dp_ref[:, 6 * d:7 * d] = (dmerged * f["yb"] * (sb * (1.0 - sb))).astype(BF16)
        dya = dmerged * sa
        dyb = dmerged * sb
        dp_ref[:, 0:d] = (dya * f["conv"]).astype(BF16)
        dconv = dya * f["b"]
        dcw_ref[0:1, :] += _colsum(dconv * f["s2"])
        dcw_ref[1:2, :] += _colsum(dconv * f["s1"])
        dcw_ref[2:3, :] += _colsum(dconv * f["cx"])
        next8 = carry_ref[...] * keep_next
        carry_ref[...] = dconv[:HALO]
        dcx, _, _ = _conv_bwd(dconv, next8, cw)
        dp_ref[:, d:2 * d] = (dcx * f["xi"]).astype(BF16)
        dp_ref[:, 2 * d:3 * d] = (dcx * f["c"]).astype(BF16)
        dp_ref[:, 3 * d:4 * d] = (dyb * f["mixed"]).astype(BF16)
        dmixed = dyb * f["u"]
        dmb = dmixed.astype(BF16)
        vnb = f["vnb"]
        tril = f["tril"]
        triu = (lax.broadcasted_iota(jnp.int32, (CHUNK, CHUNK), 0)
                <= lax.broadcasted_iota(jnp.int32, (CHUNK, CHUNK), 1))
        dbs_tile = dmixed[0:CHUNK]
        for n in range(1, tm // CHUNK):
            dbs_tile = dbs_tile + dmixed[n * CHUNK:(n + 1) * CHUNK]
        dbs_acc[...] += dbs_tile
        for g in range(N_GROUPS):
            wmt = jnp.where(triu, wst_ref[g], 0.0).astype(BF16)
            cols = slice(g * gd, (g + 1) * gd)
            dw = jnp.zeros((CHUNK, CHUNK), F32)
            for n in range(tm // CHUNK):
                rows = slice(n * CHUNK, (n + 1) * CHUNK)
                dvn_scr[rows, cols] = jnp.dot(wmt, dmb[rows, cols], preferred_element_type=F32)
                dw = dw + lax.dot_general(dmb[rows, cols], vnb[rows, cols], NT_DIMS, preferred_element_type=F32)
            dws_ref[g] += jnp.where(tril, dw, 0.0)
        dvn = dvn_scr[...]
        vhat = f["vhat"]
        dln_ref[0:1, :] += _colsum(dvn * vhat)
        dln_ref[1:2, :] += _colsum(dvn)
        dvh = dvn * lng
        dv = f["rstd"] * (dvh - jnp.mean(dvh, axis=-1, keepdims=True)
                          - vhat * jnp.mean(dvh * vhat, axis=-1, keepdims=True))
        dp_ref[:, 4 * d:5 * d] = dv.astype(BF16)
        if on_to_x0 is not None:
            dh = jnp.zeros((tm, d), F32)
            for j in range(0, nj, 2):
                pair = jnp.concatenate([win_ref[j], win_ref[j + 1]], axis=1)
                dh = dh + lax.dot_general(dp_ref[:, j * wn:(j + 2) * wn], pair, NT_DIMS, preferred_element_type=F32)
            dx, dg = _rms_bwd(dh, x0_ref[...], g_ref[...])
            dx0_ref[...] = dx_ref[...] + dx
            dg_ref[0:1, :] += dg

        @pl.when(i == nt - 1)
        def _():
            for g in range(N_GROUPS):
                cols = slice(g * gd, (g + 1) * gd)
                s = jnp.sum(dbs_acc[:, cols], axis=1, keepdims=True)
                dbs_ref[:, cols] = jnp.broadcast_to(s, (CHUNK, gd))

    rev = lambda i: nt - 1 - i

    def halo(col):
        return pl.BlockSpec((HALO, d), lambda i: (jnp.maximum(rev(i) * (tm // HALO) - 1, 0), col))

    const2 = lambda i: (0, 0)
    const3 = lambda i: (0, 0, 0)
    row = lambda i: (rev(i), 0)
    in_specs = [pl.BlockSpec((tm, d), row), pl.BlockSpec((tm, 7 * d), row), halo(1), halo(2),
                _once((d, d), const2), _once((HALO, d), const2), _once((1, d), const2), _once((1, d), const2),
                _once((N_GROUPS, CHUNK, CHUNK), const3), _once((N_GROUPS, CHUNK, CHUNK), const3),
                _once((CHUNK, d), const2)]
    out_specs = [pl.BlockSpec((tm, 7 * d), row), pl.BlockSpec((HALO, d), const2), pl.BlockSpec((HALO, d), const2),
                 pl.BlockSpec((N_GROUPS, CHUNK, CHUNK), const3), pl.BlockSpec((CHUNK, d), const2)]
    out_shape = [jax.ShapeDtypeStruct((t, 7 * d), BF16), jax.ShapeDtypeStruct((HALO, d), F32),
                 jax.ShapeDtypeStruct((HALO, d), F32), jax.ShapeDtypeStruct((N_GROUPS, CHUNK, CHUNK), F32),
                 jax.ShapeDtypeStruct((CHUNK, d), F32)]
    args = [dx1, proj, proj, proj, wout, cw, lng, lnb, ws, wst, bias]
    if on_to_x0 is not None:
        in_specs += [_once((nj, d, wn), const3), pl.BlockSpec((tm, d), row), _once((1, d), const2)]
        out_specs += [pl.BlockSpec((tm, d), row), pl.BlockSpec((HALO, d), const2)]
        out_shape += [jax.ShapeDtypeStruct((t, d), F32), jax.ShapeDtypeStruct((HALO, d), F32)]
        args += list(on_to_x0)
    return pl.pallas_call(
        body, name=name, grid=(nt,), in_specs=in_specs, out_specs=out_specs, out_shape=out_shape,
        scratch_shapes=[pltpu.VMEM((tm, d), F32), pltpu.VMEM((tm, d), F32),
                        pltpu.VMEM((HALO, d), F32), pltpu.VMEM((CHUNK, d), F32)],
        compiler_params=_params(1),
    )(*args)


def dgrad_rms(dy, w, x, g, res, *, name, tm):
    t, d = x.shape
    n = w.shape[2]
    w = w.reshape(w.shape[0] // 2, 2, d, n)
    nj = w.shape[0]
    tm = min(tm, t)

    def body(dy_ref, w_ref, x_ref, g_ref, res_ref, dx_ref, dg_ref, acc_ref):
        i, j = pl.program_id(0), pl.program_id(1)

        @pl.when((i == 0) & (j == 0))
        def _():
            dg_ref[...] = jnp.zeros_like(dg_ref)

        pair = jnp.concatenate([w_ref[0], w_ref[1]], axis=1)
        part = lax.dot_general(dy_ref[...], pair, NT_DIMS, preferred_element_type=F32)

        @pl.when(j == 0)
        def _():
            acc_ref[...] = part

        @pl.when(j > 0)
        def _():
            acc_ref[...] += part

        @pl.when(j == nj - 1)
        def _():
            dx, dg = _rms_bwd(acc_ref[...], x_ref[...], g_ref[...])
            dx_ref[...] = res_ref[...] + dx
            dg_ref[0:1, :] += dg

    return pl.pallas_call(
        body, name=name, grid=(t // tm, nj),
        in_specs=[pl.BlockSpec((tm, 2 * n), lambda i, j: (i, j)),
                  pl.BlockSpec((None, 2, d, n), lambda i, j: (j, 0, 0, 0)),
                  pl.BlockSpec((tm, d), lambda i, j: (i, 0)),
                  pl.BlockSpec((1, d), lambda i, j: (0, 0)),
                  pl.BlockSpec((tm, d), lambda i, j: (i, 0))],
        out_specs=[pl.BlockSpec((tm, d), lambda i, j: (i, 0)), pl.BlockSpec((HALO, d), lambda i, j: (0, 0))],
        out_shape=[jax.ShapeDtypeStruct((t, d), F32), jax.ShapeDtypeStruct((HALO, d), F32)],
        scratch_shapes=[pltpu.VMEM((tm, d), F32)],
        compiler_params=_params(2),
    )(dy, w, x, g, res)


def wgrad(a, b, *, nj, a_mode, b_mode, name, tm, split=1):
    def describe(arr, mode):
        if mode == "full":
            return arr.shape[0], arr.shape[1], pl.BlockSpec((tm_, arr.shape[1]), lambda j, s: (s, 0))
        if mode == "cols":
            c = arr.shape[1] // nj
            return arr.shape[0], c, pl.BlockSpec((tm_, c), lambda j, s: (s, j))
        return arr.shape[1], arr.shape[2], pl.BlockSpec((None, tm_, arr.shape[2]), lambda j, s: (j, s, 0))

    t = a.shape[0] if a_mode != "lead" else a.shape[1]
    tm_ = min(tm, t)
    _, k, a_spec = describe(a, a_mode)
    _, n, b_spec = describe(b, b_mode)

    ns = t // tm_
    nc = n // split

    def body(a_ref, b_ref, o_ref, acc_ref):
        s = pl.program_id(1)
        part = lax.dot_general(a_ref[...], b_ref[...], TN_DIMS, preferred_element_type=F32)

        def finish(total):
            for q in range(split):
                o_ref[q] = total[:, q * nc:(q + 1) * nc].astype(BF16)

        if ns == 1:
            finish(part)
            return

        @pl.when(s == 0)
        def _():
            acc_ref[...] = part

        @pl.when((s > 0) & (s < ns - 1))
        def _():
            acc_ref[...] += part

        @pl.when(s == ns - 1)
        def _():
            finish(acc_ref[...] + part)

    return pl.pallas_call(
        body, name=name, grid=(nj, ns),
        in_specs=[a_spec, b_spec],
        out_specs=pl.BlockSpec((split, k, nc), lambda j, s: (j, 0, 0)),
        out_shape=jax.ShapeDtypeStruct((nj * split, k, nc), BF16),
        scratch_shapes=[pltpu.VMEM((k, n), F32)],
        compiler_params=_params(2),
    )(a, b)


def _adamw_math(w, g, m, v):
    m = ADAM_B1 * m + (1.0 - ADAM_B1) * g
    v = ADAM_B2 * v + (1.0 - ADAM_B2) * (g * g)
    m_hat = m / (1.0 - ADAM_B1 ** ADAM_STEP)
    v_hat = v / (1.0 - ADAM_B2 ** ADAM_STEP)
    delta = -ADAM_LR * (m_hat / (jnp.sqrt(v_hat) + ADAM_EPS) + ADAM_WD * w)
    return delta, m, v


def _row_tile(rows, at_most):
    if rows <= at_most:
        return rows
    return max(k for k in range(16, at_most + 1, 16) if rows % k == 0)


def _sum_in_device_order(ref):
    total = ref[0]
    for s in range(1, N_DEV):
        total = total + ref[s]
    return total


def adamw_sharded(me, own0, recv0, own1, recv1, w, m, v, *, name, tr):
    _, r, c = w.shape
    tr = _row_tile(r, tr)
    ni = r // tr

    def body(me_ref, o0_ref, r0_ref, o1_ref, r1_ref, w_ref, m_ref, v_ref, g_ref, d_ref, nm_ref, nv_ref):
        def finish(own_ref, recv_ref):
            g = None
            for s in range(N_DEV):
                term = jnp.where(me_ref[0] == s, own_ref[...], recv_ref[s]).astype(F32)
                g = term if g is None else g + term
            delta, nm, nv = _adamw_math(w_ref[...], g, m_ref[...], v_ref[...])
            g_ref[...] = g
            d_ref[...] = delta
            nm_ref[...] = nm
            nv_ref[...] = nv

        @pl.when(pl.program_id(0) == 0)
        def _():
            finish(o0_ref, r0_ref)

        @pl.when(pl.program_id(0) == 1)
        def _():
            finish(o1_ref, r1_ref)

    row0 = lambda l, i: i * (1 - l) + (ni - 1) * l
    row1 = lambda l, i: i * l
    lay = pl.BlockSpec((None, tr, c), lambda l, i, me_ref: (l, i, 0))
    grid_spec = pltpu.PrefetchScalarGridSpec(
        num_scalar_prefetch=1, grid=(2, ni),
        in_specs=[pl.BlockSpec((None, tr, c), lambda l, i, me_ref: (me_ref[0], row0(l, i), 0)),
                  pl.BlockSpec((N_DEV, tr, c), lambda l, i, me_ref: (0, row0(l, i), 0)),
                  pl.BlockSpec((None, tr, c), lambda l, i, me_ref: (me_ref[0], row1(l, i), 0)),
                  pl.BlockSpec((N_DEV, tr, c), lambda l, i, me_ref: (0, row1(l, i), 0)),
                  lay, lay, lay],
        out_specs=[lay, lay, lay, lay])
    return pl.pallas_call(
        body, name=name, grid_spec=grid_spec,
        out_shape=[jax.ShapeDtypeStruct(w.shape, F32)] * 4,
        compiler_params=_params(2),
    )(me, own0, recv0, own1, recv1, w, m, v)


def sum_chunks(me, own, recv, *, name):
    _, r, c = own.shape

    def body(me_ref, o_ref, r_ref, out_ref):
        total = None
        for s in range(N_DEV):
            term = jnp.where(me_ref[0] == s, o_ref[...], r_ref[s]).astype(F32)
            total = term if total is None else total + term
        out_ref[...] = total

    grid_spec = pltpu.PrefetchScalarGridSpec(
        num_scalar_prefetch=1, grid=(1,),
        in_specs=[pl.BlockSpec((None, r, c), lambda i, me_ref: (me_ref[0], 0, 0)),
                  pl.BlockSpec((N_DEV, r, c), lambda i, me_ref: (0, 0, 0))],
        out_specs=pl.BlockSpec((r, c), lambda i, me_ref: (0, 0)))
    return pl.pallas_call(
        body, name=name, grid_spec=grid_spec,
        out_shape=jax.ShapeDtypeStruct((r, c), F32),
        compiler_params=_params(1),
    )(me, own, recv)


def adamw_small(g, w, m, v, *, name):
    def body(g_ref, w_ref, m_ref, v_ref, d_ref, nm_ref, nv_ref):
        delta, nm, nv = _adamw_math(w_ref[...], g_ref[...], m_ref[...], v_ref[...])
        d_ref[...] = delta
        nm_ref[...] = nm
        nv_ref[...] = nv

    return pl.pallas_call(
        body, name=name,
        out_shape=[jax.ShapeDtypeStruct(w.shape, F32)] * 3,
        compiler_params=pltpu.CompilerParams(vmem_limit_bytes=VMEM_LIMIT_BYTES),
    )(g, w, m, v)


def sum_devices(parts, *, name, tr):
    _, r, c = parts.shape
    tr = min(tr, r)

    def body(p_ref, o_ref):
        o_ref[...] = _sum_in_device_order(p_ref)

    return pl.pallas_call(
        body, name=name, grid=(r // tr,),
        in_specs=[pl.BlockSpec((N_DEV, tr, c), lambda i: (0, i, 0))],
        out_specs=pl.BlockSpec((tr, c), lambda i: (i, 0)),
        out_shape=jax.ShapeDtypeStruct((r, c), F32),
        compiler_params=_params(1),
    )(parts)


def _my_place():
    return lax.axis_index("x"), lax.axis_index("y"), lax.axis_index("c")


def all_gather(arrays, after, *, name):
    n = len(arrays)

    def body(*refs):
        ins, outs = refs[:n], refs[n + 1:2 * n + 1]
        send_sems, recv_sems, local_sems = refs[2 * n + 1:]
        x, y, c = _my_place()
        me, sibling = (x, y, c), (x, y, 1 - c)
        chips = [(1 - x, y), (x, 1 - y), (1 - x, 1 - y)]
        waits = []
        for a in range(n):
            def slot(place, a=a):
                px, py, pc = place
                return outs[a].at[4 * px + 2 * py + pc]

            def copy(k, block, to, src=None, a=a, slot=slot):
                return pltpu.make_async_remote_copy(
                    src_ref=slot(block) if src is None else src, dst_ref=slot(block),
                    send_sem=send_sems.at[a, k], recv_sem=recv_sems.at[a, k],
                    device_id=to, device_id_type=MESH)

            mine = pltpu.make_async_copy(ins[a], slot(me), local_sems.at[a])
            mine.start()
            first = [copy(0, me, sibling, src=ins[a])]
            first += [copy(1 + j, me, (*chip, c), src=ins[a]) for j, chip in enumerate(chips)]
            for cp in first:
                cp.start()
            waits.append((copy, mine, first))
        sends = []
        for a in range(n):
            copy, mine, first = waits[a]
            passed = [copy(4 + j, (*chip, c), sibling) for j, chip in enumerate(chips)]
            for j, chip in enumerate(chips):
                copy(1 + j, (*chip, c), me).wait_recv()
                passed[j].start()
            sends.append(first + passed)
        for a in range(n):
            copy, mine, first = waits[a]
            copy(0, sibling, me).wait_recv()
            for j, chip in enumerate(chips):
                copy(4 + j, (*chip, 1 - c), me).wait_recv()
            for cp in sends[a]:
                cp.wait_send()
            mine.wait()

    return pl.pallas_call(
        body, name=name,
        in_specs=[ANY] * (n + 1), out_specs=[ANY] * n,
        out_shape=[jax.ShapeDtypeStruct((N_DEV,) + a.shape, a.dtype) for a in arrays],
        scratch_shapes=[pltpu.SemaphoreType.DMA((n, 7)), pltpu.SemaphoreType.DMA((n, 7)),
                        pltpu.SemaphoreType.DMA((n,))],
        compiler_params=pltpu.CompilerParams(has_side_effects=True),
    )(*arrays, after)


def _peer_place(r, x, y, c):
    fx, fy, fc = (r >> 2) & 1, (r >> 1) & 1, r & 1
    return (1 - x if fx else x, 1 - y if fy else y, 1 - c if fc else c)


def own_slot(me, w, layer, dtype, *, name, tr):
    _, r, c = w.shape
    tr = _row_tile(r, tr)

    def body(me_ref, w_ref, o_ref):
        o_ref[...] = w_ref[...].astype(dtype)

    grid_spec = pltpu.PrefetchScalarGridSpec(
        num_scalar_prefetch=1, grid=(r // tr,),
        in_specs=[pl.BlockSpec((None, tr, c), lambda i, me_ref: (layer, i, 0))],
        out_specs=pl.BlockSpec((None, tr, c), lambda i, me_ref: (me_ref[0], i, 0)))
    return pl.pallas_call(
        body, name=name, grid_spec=grid_spec,
        out_shape=jax.ShapeDtypeStruct((N_DEV, r, c), dtype),
        compiler_params=_params(1),
    )(me, w)


EXCHANGES = {
    "scatter": [(0, r) for r in range(1, N_DEV)],
    "gather": [(0, r) for r in range(1, N_DEV)],
    "gather_chips": [(0, r) for r in (1, 2, 4, 6)],
    "gather_forward": [(q, 1) for q in (2, 4, 6)],
}


def _split_copy(k, entry, src, land, send_sem, recv_sem, arriving):
    slot, peer = entry
    x, y, c = _my_place()

    def index(relation):
        px, py, pc = _peer_place(relation, x, y, c)
        return 4 * px + 2 * py + pc

    return pltpu.make_async_remote_copy(
        src_ref=land.at[index(slot)] if src is None else src.at[index(peer)],
        dst_ref=land.at[index(slot ^ peer if arriving else slot)],
        send_sem=send_sem.at[k], recv_sem=recv_sem.at[k],
        device_id=_peer_place(peer, x, y, c), device_id_type=MESH)


def start_copies(srcs, lands, *, mode, name, after=None):
    n = len(lands)
    entries = EXCHANGES[mode]
    bufs = (list(srcs) if srcs is not None else []) + list(lands)
    nb = len(bufs)

    def body(*refs):
        src = refs[:n] if srcs is not None else [None] * n
        land = refs[nb - n:nb]
        outs = refs[nb + len(extra):]
        send_sems, recv_sems = outs[:n], outs[n:2 * n]
        token = outs[2 * n + nb]
        for a in range(n):
            for k, entry in enumerate(entries):
                _split_copy(k, entry, src[a], land[a], send_sems[a], recv_sems[a], False).start()
        token[...] = jnp.zeros_like(token)

    extra = [] if after is None else [after]
    outs = pl.pallas_call(
        body, name=name,
        in_specs=[HBM_SPEC] * nb + [ANY] * len(extra),
        out_specs=[SEM_SPEC] * (2 * n) + [HBM_SPEC] * nb + [pl.BlockSpec(memory_space=pltpu.VMEM)],
        out_shape=([pltpu.SemaphoreType.DMA((len(entries),))] * (2 * n)
                   + [pltpu.HBM(a.shape, a.dtype) for a in bufs]
                   + [jax.ShapeDtypeStruct((8, 128), F32)]),
        input_output_aliases={i: 2 * n + i for i in range(nb)},
        compiler_params=pltpu.CompilerParams(has_side_effects=DATAFLOW),
    )(*[pltpu.with_memory_space_constraint(a, pltpu.HBM) for a in bufs], *extra)
    thru = list(outs[2 * n:2 * n + nb])
    return dict(send=outs[:n], recv=outs[n:2 * n], src=thru[:n] if srcs is not None else None, land=thru[nb - n:],
                token=outs[2 * n + nb], mode=mode)


def finish_copies(started, which, after, *, name):
    n = len(which)
    entries = EXCHANGES[started["mode"]]
    has_src = started["src"] is not None
    bufs = ([started["src"][i] for i in which] if has_src else []) + [started["land"][i] for i in which]
    nb = len(bufs)

    def body(*refs):
        src = refs[:n] if has_src else [None] * n
        land = refs[nb - n:nb]
        send_sems, recv_sems = refs[nb:nb + n], refs[nb + n:nb + 2 * n]
        for a in range(n):
            for k, entry in enumerate(entries):
                cp = _split_copy(k, entry, src[a], land[a], send_sems[a], recv_sems[a], True)
                cp.wait_send()
                cp.wait_recv()

    outs = pl.pallas_call(
        body, name=name,
        in_specs=[HBM_SPEC] * nb + [SEM_SPEC] * (2 * n) + [ANY],
        out_specs=[HBM_SPEC] * nb,
        out_shape=[pltpu.HBM(a.shape, a.dtype) for a in bufs],
        input_output_aliases={i: i for i in range(nb)},
        compiler_params=pltpu.CompilerParams(has_side_effects=DATAFLOW),
    )(*bufs, *[started["send"][i] for i in which], *[started["recv"][i] for i in which], after)
    return (list(outs[:n]) if has_src else None), list(outs[nb - n:])


def _pad_rows(a, rows):
    pad = [(0, 0)] * a.ndim
    pad[-2] = (0, rows - a.shape[-2])
    return jnp.pad(a, pad)


def kernel(x, mix_norm_g, w_in, conv_a_w, ln_v_g, ln_v_b, w_s, b_s, w_out, ffn_norm_g, w_up, conv_ffn_w, w_down, final_norm_g, loss_target, m_mix_norm_g, m_w_in, m_conv_a_w, m_ln_v_g, m_ln_v_b, m_w_s, m_b_s, m_w_out, m_ffn_norm_g, m_w_up, m_conv_ffn_w, m_w_down, m_final_norm_g, v_mix_norm_g, v_w_in, v_conv_a_w, v_ln_v_g, v_ln_v_b, v_w_s, v_b_s, v_w_out, v_ffn_norm_g, v_w_up, v_conv_ffn_w, v_w_down, v_final_norm_g):
    nb, seq, d = x.shape
    t = nb * seq
    depth = w_in.shape[0]
    f = w_up.shape[2]
    me = 4 * lax.axis_index("x") + 2 * lax.axis_index("y") + lax.axis_index("c")
    xt = x.reshape(t, d)
    tgt = loss_target.reshape(t, d)

    conv_pack = jnp.concatenate([_pad_rows(conv_a_w, HALO), _pad_rows(conv_ffn_w, HALO)], axis=-1)
    me_arr = me.astype(jnp.int32).reshape(1)
    w_up_t, m_w_up_t, v_w_up_t = (jnp.swapaxes(a, 1, 2) for a in (w_up, m_w_up, v_w_up))
    zones, slot_of = [], {}
    for l in range(depth):
        for key, w in (("win", w_in), ("conv", None), ("wout", w_out), ("wup", w_up_t), ("wd", w_down)):
            if key == "conv":
                if l == 0:
                    slot_of["conv"] = len(zones)
                    packed = conv_pack.reshape(1, depth * HALO, conv_pack.shape[-1])
                    zones.append(own_slot(me_arr, packed, 0, F32, name="own_slot_conv", tr=256))
                continue
            slot_of[key, l] = len(zones)
            zones.append(own_slot(me_arr, w, l, BF16, name=f"own_slot_{key}_{l}", tr=256))
    first = [slot_of["win", 0], slot_of["wout", 0], slot_of["conv"]]
    rest = [i for i in range(len(zones)) if i not in first]
    to_chips = start_copies(None, [zones[i] for i in first], mode="gather_chips", name="gather_first_chips")
    gathering = start_copies(None, [zones[i] for i in rest], mode="gather", name="gather_start", after=to_chips["token"])
    _, at_chips = finish_copies(to_chips, [0, 1, 2], gathering["token"], name="wait_first_chips")
    to_sibling = start_copies(None, at_chips, mode="gather_forward", name="gather_first_forward")

    def gathered(keys, after, name):
        return finish_copies(gathering, [rest.index(slot_of[k]) for k in keys], after, name=name)[1]

    saved, layers = [], []
    cur = xt
    for l in range(depth):
        p = dict(mix_g=mix_norm_g[l][None], ffn_g=ffn_norm_g[l][None], lng=ln_v_g[l][None], lnb=ln_v_b[l][None],
                 ws=w_s[l], wst=jnp.swapaxes(w_s[l], 1, 2),
                 bias=jnp.repeat(b_s[l].T, d // N_GROUPS, axis=1))
        if l == 0:
            _, (p["win"], wout_g, conv_g) = finish_copies(to_sibling, [0, 1, 2], to_sibling["token"],
                                                          name=f"wait_w_mixer_{l}")
            conv_g = conv_g.reshape(N_DEV, depth, HALO, -1)
            ca = conv_g.shape[-1] - f
        else:
            p["win"], wout_g = gathered([("win", l), ("wout", l)], after, f"wait_w_mixer_{l}")
        p["wout"] = wout_g.reshape(d, d)
        p["cw_a"] = jnp.transpose(conv_g[:, l, :, :ca], (1, 0, 2)).reshape(HALO, d)
        p["cw_f"] = conv_g[:, l, :, ca:]
        h, proj, merged, x1 = mixer_fwd(cur, p["mix_g"], p["win"], p["wout"], p["cw_a"], p["lng"], p["lnb"], p["ws"],
                                        p["bias"], seq=seq, name=f"mixer_fwd_{l}", tm=256)
        p["wup"], wd_g = gathered([("wup", l), ("wd", l)], merged, f"wait_w_ffn_{l}")
        p["wd"] = wd_g.reshape(N_DEV // 2, 2 * wd_g.shape[1], d)
        head = (final_norm_g[None], tgt) if l == depth - 1 else None
        h2, up0, upc, act, x2, *of_loss = ffn_fwd(x1, p["ffn_g"], p["wup"], p["wd"], p["cw_f"],
                                                  seq=seq, name=f"ffn_fwd_{l}", tm=256, head=head)
        saved.append(dict(x0=cur, h=h, proj=proj, merged=merged, x1=x1, h2=h2, up0=up0, upc=upc, act=act))
        layers.append(p)
        cur, after = x2, act
    dx = cur
    d_final_g, loss_tile = of_loss

    def exchange(parts, name):
        return start_copies(parts, [lax.empty(a.shape, a.dtype) for a in parts], mode="scatter", name=name)

    def tied(g, started):
        return g + started["token"][0:1, 0:1]

    part = [None] * depth
    mix_ex = None
    for l in reversed(range(depth)):
        p, s = layers[l], saved[l]
        ffn_g = p["ffn_g"] if mix_ex is None else tied(p["ffn_g"], mix_ex)
        dup0, dcw_f, dx1, d_ffn_g = ffn_bwd(dx, s["up0"], s["upc"], p["wd"], p["cw_f"], p["wup"], s["x1"], ffn_g,
                                            seq=seq, name=f"ffn_bwd_{l}", tm=256)
        g_wd = wgrad(s["act"], dx, nj=N_DEV // 2, a_mode="lead", b_mode="full", name=f"wgrad_down_{l}", tm=2048)
        g_wup = wgrad(dup0, s["h2"], nj=N_DEV, a_mode="lead", b_mode="full", name=f"wgrad_up_{l}", tm=2048)
        ffn_ex = exchange([g_wd.reshape(N_DEV, g_wd.shape[1] // 2, d), g_wup], f"exchange_ffn_{l}")
        fused = l > 0
        dproj, dcw_a, dln, dws, dbs, *to_x0 = mixer_bwd(
            dx1, s["proj"], p["wout"], tied(p["cw_a"], ffn_ex), p["lng"], p["lnb"], p["ws"], p["wst"], p["bias"],
            seq=seq, name=f"mixer_bwd_{l}", tm=256, on_to_x0=(p["win"], s["x0"], p["mix_g"]) if fused else None)
        g_wout = wgrad(s["merged"], dx1, nj=1, a_mode="full", b_mode="full", name=f"wgrad_out_{l}", tm=2048)
        g_win = wgrad(s["h"], dproj, nj=N_DEV // 2, a_mode="full", b_mode="cols", name=f"wgrad_in_{l}", tm=2048, split=2)
        cwa_chunks = jnp.transpose(dcw_a.reshape(HALO, N_DEV, d // N_DEV), (1, 0, 2))
        mix_ex = exchange([g_wout.reshape(N_DEV, d // N_DEV, d), g_win, dws, dcw_f, cwa_chunks], f"exchange_mix_{l}")
        if fused:
            dx, d_mix_g = to_x0
        else:
            dx, d_mix_g = dgrad_rms(dproj, p["win"], s["x0"], tied(p["mix_g"], mix_ex), dx1,
                                    name=f"dgrad_in_{l}", tm=1024)
        part[l] = dict(
            ffn_ex=ffn_ex, mix_ex=mix_ex,
            vectors=jnp.concatenate([d_mix_g[0:1], d_ffn_g[0:1], dln[0:2],
                                     dbs[:, ::d // N_GROUPS].T.reshape(1, d)], axis=0))
    grad_x = dx.reshape(nb, seq, d)

    own, recv = {}, {}

    def arrived(l, ex, keys, after):
        srcs, lands = finish_copies(part[l][ex], list(range(len(keys))), after, name=f"wait_{ex}_{l}")
        for k, key in enumerate(keys):
            own[key, l], recv[key, l] = srcs[k], lands[k]
        return lands[1]

    def big(key, w, m, v, name):
        return adamw_sharded(me_arr, own[key, 0], recv[key, 0], own[key, 1], recv[key, 1], w, m, v, name=name, tr=256)

    mix_keys = ("wout", "win", "ws", "cwf", "cwa")
    after = grad_x
    for l in reversed(range(depth)):
        after = arrived(l, "ffn_ex", ("wd", "wup"), after)
        if l > 0:
            after = arrived(l, "mix_ex", mix_keys, after)
    u_wd = big("wd", w_down, m_w_down, v_w_down, "adamw_w_down")
    u_wup = tuple(jnp.swapaxes(a, 1, 2) for a in big("wup", w_up_t, m_w_up_t, v_w_up_t, "adamw_w_up"))
    arrived(0, "mix_ex", mix_keys, u_wup[1])
    u_wout = big("wout", w_out, m_w_out, v_w_out, "adamw_w_out")
    u_win = big("win", w_in, m_w_in, v_w_in, "adamw_w_in")

    def owned(key, l):
        return sum_chunks(me_arr, own[key, l], recv[key, l], name=f"sum_{key}_{l}")

    g_cwf = jnp.stack([owned("cwf", l)[:3] for l in range(depth)])
    g_cwa = jnp.stack([owned("cwa", l)[:3] for l in range(depth)])
    ws_rows = CHUNK * CHUNK // d
    ws_mine = jnp.concatenate([owned("ws", l).reshape(ws_rows, d) for l in range(depth)], axis=0)
    loss_row = jnp.zeros((1, d), F32).at[0, 0].set(loss_tile[0, 0])
    vectors = jnp.concatenate([part[l]["vectors"] for l in range(depth)] + [d_final_g[0:1], loss_row], axis=0)
    vectors = _pad_rows(vectors, -(-vectors.shape[0] // 8) * 8)
    vectors_all, ws_all = all_gather([vectors, ws_mine], u_win[0], name="gather_small_grads")
    vec_sum = sum_devices(vectors_all, name="sum_small", tr=512)
    g_ws = jnp.transpose(ws_all.reshape(N_DEV, depth, CHUNK, CHUNK), (1, 0, 2, 3))
    per_layer = part[0]["vectors"].shape[0]
    g_mix, g_ffn, g_lng, g_lnb = (jnp.stack([vec_sum[l * per_layer + k] for l in range(depth)]) for k in range(4))
    g_bs = jnp.stack([vec_sum[l * per_layer + 4].reshape(N_GROUPS, CHUNK) for l in range(depth)])
    g_final = vec_sum[depth * per_layer]
    loss = vec_sum[depth * per_layer + 1, 0]

    def small_update(g, w, m, v, name):
        shape = w.shape
        two_d = (-1, shape[-1]) if w.ndim > 1 else (1, shape[0])
        out = adamw_small(g.reshape(two_d), w.reshape(two_d), m.reshape(two_d), v.reshape(two_d), name=name)
        return (g.reshape(shape),) + tuple(o.reshape(shape) for o in out)

    u_mix = small_update(g_mix, mix_norm_g, m_mix_norm_g, v_mix_norm_g, "adamw_mix_norm_g")
    u_cwa = small_update(g_cwa, conv_a_w, m_conv_a_w, v_conv_a_w, "adamw_conv_a_w")
    u_lng = small_update(g_lng, ln_v_g, m_ln_v_g, v_ln_v_g, "adamw_ln_v_g")
    u_lnb = small_update(g_lnb, ln_v_b, m_ln_v_b, v_ln_v_b, "adamw_ln_v_b")
    u_ws = small_update(g_ws, w_s, m_w_s, v_w_s, "adamw_w_s")
    u_bs = small_update(g_bs, b_s, m_b_s, v_b_s, "adamw_b_s")
    u_ffn = small_update(g_ffn, ffn_norm_g, m_ffn_norm_g, v_ffn_norm_g, "adamw_ffn_norm_g")
    u_cwf = small_update(g_cwf, conv_ffn_w, m_conv_ffn_w, v_conv_ffn_w, "adamw_conv_ffn_w")
    u_final = small_update(g_final, final_norm_g, m_final_norm_g, v_final_norm_g, "adamw_final_norm_g")

    ordered = [u_mix, u_win, u_cwa, u_lng, u_lnb, u_ws, u_bs, u_wout, u_ffn, u_wup, u_cwf, u_wd, u_final]
    return (loss, grad_x, *[u[0] for u in ordered], *[u[1] for u in ordered],
            *[u[2] for u in ordered], *[u[3] for u in ordered])
```

```python
import functools

import jax
import jax.numpy as jnp
from jax import lax
from jax.experimental import pallas as pl
from jax.experimental.pallas import tpu as pltpu

EPS = 1e-6
CHUNK = 128
N_GROUPS = 8
N_DEV = 8
HALO = 8
ADAM_LR = 0.001
ADAM_B1 = 0.9
ADAM_B2 = 0.999
ADAM_EPS = 1e-08
ADAM_WD = 0.01
ADAM_STEP = 10
VMEM_LIMIT_BYTES = 56 * 1024 * 1024
F32 = jnp.float32
BF16 = jnp.bfloat16
MESH = pl.DeviceIdType.MESH
ANY = pl.BlockSpec(memory_space=pl.ANY)
HBM_SPEC = pl.BlockSpec(memory_space=pltpu.HBM)
SEM_SPEC = pl.BlockSpec(memory_space=pltpu.SEMAPHORE)
DATAFLOW = pltpu.SideEffectType.DATAFLOW_SIDE_EFFECTING
NT_DIMS = (((1,), (1,)), ((), ()))
TN_DIMS = (((0,), (0,)), ((), ()))


def _params(n_grid_axes):
    return pltpu.CompilerParams(dimension_semantics=("arbitrary",) * n_grid_axes,
                                vmem_limit_bytes=VMEM_LIMIT_BYTES)


def _shift_down(cur, prev8, k):
    rolled = pltpu.roll(cur, k, 0)
    prolled = pltpu.roll(prev8, k, 0)
    row = lax.broadcasted_iota(jnp.int32, prev8.shape, 0)
    head = jnp.where(row < k, prolled, rolled[:HALO])
    return jnp.concatenate([head, rolled[HALO:]], axis=0)


def _shift_up(cur, next8, k):
    tm = cur.shape[0]
    rolled = pltpu.roll(cur, tm - k, 0)
    nrolled = pltpu.roll(next8, HALO - k, 0)
    row = lax.broadcasted_iota(jnp.int32, next8.shape, 0)
    tail = jnp.where(row >= HALO - k, nrolled, rolled[tm - HALO:])
    return jnp.concatenate([rolled[:tm - HALO], tail], axis=0)


def _conv_fwd(cur, prev8, cw):
    s1 = _shift_down(cur, prev8, 1)
    s2 = _shift_down(cur, prev8, 2)
    y = s2 * cw[0:1, :] + s1 * cw[1:2, :] + cur * cw[2:3, :]
    return y, s1, s2


def _conv_bwd(d, next8, cw):
    u1 = _shift_up(d, next8, 1)
    u2 = _shift_up(d, next8, 2)
    return d * cw[2:3, :] + u1 * cw[1:2, :] + u2 * cw[0:1, :], u1, u2


def _colsum(a):
    return jnp.sum(a, axis=0, keepdims=True)


def _rms_stats(xv):
    r = lax.rsqrt(jnp.mean(xv * xv, axis=-1, keepdims=True) + EPS)
    return r, xv * r


def _rms_bwd(dh, xv, g):
    r, n = _rms_stats(xv)
    dn = dh * g
    dx = r * (dn - n * jnp.mean(dn * n, axis=-1, keepdims=True))
    return dx, _colsum(dh * n)


def _mixer_forward(p_ref, cprev, xiprev, cw, lng, lnb, ws_ref, bias_ref, mixed_scr, d):
    tm = p_ref.shape[0]
    b = p_ref[:, 0:d]
    c = p_ref[:, d:2 * d]
    xi = p_ref[:, 2 * d:3 * d]
    u = p_ref[:, 3 * d:4 * d]
    v = p_ref[:, 4 * d:5 * d]
    sa = jax.nn.sigmoid(p_ref[:, 5 * d:6 * d])
    sb = jax.nn.sigmoid(p_ref[:, 6 * d:7 * d])
    cx = c * xi
    conv, s1, s2 = _conv_fwd(cx, cprev * xiprev, cw)
    ya = b * conv
    mu = jnp.mean(v, axis=-1, keepdims=True)
    xc = v - mu
    rstd = lax.rsqrt(jnp.mean(xc * xc, axis=-1, keepdims=True) + EPS)
    vhat = xc * rstd
    vnb = (vhat * lng + lnb).astype(BF16)
    tril = (lax.broadcasted_iota(jnp.int32, (CHUNK, CHUNK), 0)
            >= lax.broadcasted_iota(jnp.int32, (CHUNK, CHUNK), 1))
    gd = d // N_GROUPS
    for g in range(N_GROUPS):
        wm = jnp.where(tril, ws_ref[g], 0.0).astype(BF16)
        cols = slice(g * gd, (g + 1) * gd)
        for n in range(tm // CHUNK):
            rows = slice(n * CHUNK, (n + 1) * CHUNK)
            mixed_scr[rows, cols] = (jnp.dot(wm, vnb[rows, cols], preferred_element_type=F32)
                                     + bias_ref[:, cols])
    mixed = mixed_scr[...]
    yb = u * mixed
    merged = sa * ya + sb * yb
    return dict(b=b, c=c, xi=xi, u=u, sa=sa, sb=sb, cx=cx, s1=s1, s2=s2, conv=conv, ya=ya,
                rstd=rstd, vhat=vhat, vnb=vnb, mixed=mixed, yb=yb, merged=merged, tril=tril)


KEPT = ("conv", "mixed", "vnb", "sa", "sb")


def _once(block_shape, index_map):
    return pl.BlockSpec(block_shape, index_map, pipeline_mode=pl.Buffered(1))


def mixer_fwd(x, g, win, wout, cw, lng, lnb, ws, bias, *, seq, name, tm):
    t, d = x.shape
    nj, _, n = win.shape
    tm = min(tm, seq)
    tiles_per_seq = seq // tm

    def body(x_ref, g_ref, win_ref, wout_ref, cw_ref, lng_ref, lnb_ref, ws_ref, bias_ref,
             h_ref, p_ref, merged_ref, x1_ref, kept_ref, mixed_scr, carry_ref):
        @pl.when(pl.program_id(0) == 0)
        def _():
            carry_ref[...] = jnp.zeros_like(carry_ref)

        keep = jnp.where(pl.program_id(0) % tiles_per_seq == 0, 0.0, 1.0)
        xv = x_ref[...]
        _, nrm = _rms_stats(xv)
        hb = (nrm * g_ref[...]).astype(BF16)
        h_ref[...] = hb
        for j in range(0, nj, 2):
            pair = jnp.concatenate([win_ref[j], win_ref[j + 1]], axis=1)
            p_ref[:, j * n:(j + 2) * n] = jnp.dot(hb, pair, preferred_element_type=F32)
        f = _mixer_forward(p_ref, carry_ref[...] * keep, 1.0, cw_ref[...], lng_ref[...],
                           lnb_ref[...], ws_ref, bias_ref, mixed_scr, d)
        carry_ref[...] = f["cx"][tm - HALO:]
        for k, key in enumerate(KEPT):
            kept_ref[:, k * d:(k + 1) * d] = f[key].astype(BF16)
        mb = f["merged"].astype(BF16)
        merged_ref[...] = mb
        x1_ref[...] = xv + jnp.dot(mb, wout_ref[...], preferred_element_type=F32)

    const2 = lambda i: (0, 0)
    const3 = lambda i: (0, 0, 0)
    row = lambda i: (i, 0)
    return pl.pallas_call(
        body, name=name, grid=(t // tm,),
        in_specs=[pl.BlockSpec((tm, d), row),
                  _once((1, d), const2),
                  _once((nj, d, n), const3),
                  _once((d, d), const2),
                  _once((HALO, d), const2),
                  _once((1, d), const2),
                  _once((1, d), const2),
                  _once((N_GROUPS, CHUNK, CHUNK), const3),
                  _once((CHUNK, d), const2)],
        out_specs=[pl.BlockSpec((tm, d), row), pl.BlockSpec((tm, nj * n), row),
                   pl.BlockSpec((tm, d), row), pl.BlockSpec((tm, d), row),
                   pl.BlockSpec((tm, len(KEPT) * d), row)],
        out_shape=[jax.ShapeDtypeStruct((t, d), BF16), jax.ShapeDtypeStruct((t, nj * n), F32),
                   jax.ShapeDtypeStruct((t, d), BF16), jax.ShapeDtypeStruct((t, d), F32),
                   jax.ShapeDtypeStruct((t, len(KEPT) * d), BF16)],
        scratch_shapes=[pltpu.VMEM((tm, d), F32), pltpu.VMEM((HALO, d), F32)],
        compiler_params=_params(1),
    )(x, g, win, wout, cw, lng, lnb, ws, bias)


def ffn_fwd(x1, g, wup, wd, cw, *, seq, name, tm, head=None):
    t, d = x1.shape
    nj, f, _ = wup.shape
    half = nj // 2
    tm = min(tm, seq)
    tiles_per_seq = seq // tm

    def body(x1_ref, g_ref, wup_ref, wd_ref, cw_ref, *rest):
        if head is None:
            h2_ref, up_ref, fac_ref, act_ref, x2_ref, carry_ref = rest
        else:
            gf_ref, tgt_ref, h2_ref, up_ref, fac_ref, act_ref, x2_ref, dgf_ref, loss_ref, carry_ref = rest

        @pl.when(pl.program_id(0) == 0)
        def _():
            carry_ref[...] = jnp.zeros_like(carry_ref)
            if head is not None:
                dgf_ref[...] = jnp.zeros_like(dgf_ref)
                loss_ref[...] = jnp.zeros_like(loss_ref)

        keep = jnp.where(pl.program_id(0) % tiles_per_seq == 0, 0.0, 1.0)
        xv = x1_ref[...]
        _, nrm = _rms_stats(xv)
        hb = (nrm * g_ref[...]).astype(BF16)
        h2_ref[...] = hb

        for j in range(nj):
            up_ref[j] = lax.dot_general(hb, wup_ref[j], NT_DIMS, preferred_element_type=F32)

        def conv_of(j):
            up0 = up_ref[j]
            y, _, _ = _conv_fwd(up0, carry_ref[j] * keep, cw_ref[j])
            carry_ref[j] = up0[tm - HALO:]
            return y

        acc = xv
        for k in range(half):
            gate, val = conv_of(k), conv_of(k + half)
            sg = jax.nn.sigmoid(gate)
            silu = gate * sg
            fac_ref[k] = (val * (sg * (1.0 + gate * (1.0 - sg)))).astype(BF16)
            fac_ref[k + half] = silu.astype(BF16)
            a = (silu * val).astype(BF16)
            act_ref[k] = a
            acc = acc + jnp.dot(a, wd_ref[k], preferred_element_type=F32)
        if head is None:
            x2_ref[...] = acc
        else:
            gv = gf_ref[...]
            r, n = _rms_stats(acc)
            err = n * gv - tgt_ref[...]
            loss_ref[...] += 0.5 * jnp.sum(jnp.mean(err * err, axis=-1, keepdims=True))
            dy = err * (1.0 / d)
            dn = dy * gv
            x2_ref[...] = r * (dn - n * jnp.mean(dn * n, axis=-1, keepdims=True))
            dgf_ref[0:1, :] += _colsum(dy * n)

    const3 = lambda i: (0, 0, 0)
    row = lambda i: (i, 0)
    in_specs = [pl.BlockSpec((tm, d), row), _once((1, d), lambda i: (0, 0)), _once((nj, f, d), const3),
                _once((half, f, d), const3), _once((nj, HALO, f), const3)]
    out_specs = [pl.BlockSpec((tm, d), row), pl.BlockSpec((nj, tm, f), lambda i: (0, i, 0)),
                 pl.BlockSpec((nj, tm, f), lambda i: (0, i, 0)), pl.BlockSpec((half, tm, f), lambda i: (0, i, 0)),
                 pl.BlockSpec((tm, d), row)]
    out_shape = [jax.ShapeDtypeStruct((t, d), BF16), jax.ShapeDtypeStruct((nj, t, f), F32),
                 jax.ShapeDtypeStruct((nj, t, f), BF16), jax.ShapeDtypeStruct((half, t, f), BF16),
                 jax.ShapeDtypeStruct((t, d), F32)]
    args = [x1, g, wup, wd, cw]
    if head is not None:
        in_specs += [_once((1, d), lambda i: (0, 0)), pl.BlockSpec((tm, d), row)]
        out_specs += [pl.BlockSpec((HALO, d), lambda i: (0, 0)), pl.BlockSpec((8, 128), lambda i: (0, 0))]
        out_shape += [jax.ShapeDtypeStruct((HALO, d), F32), jax.ShapeDtypeStruct((8, 128), F32)]
        args += list(head)
    return pl.pallas_call(
        body, name=name, grid=(t // tm,), in_specs=in_specs, out_specs=out_specs, out_shape=out_shape,
        scratch_shapes=[pltpu.VMEM((nj, HALO, f), F32)],
        compiler_params=_params(1),
    )(*args)


def ffn_bwd(dx2, up0, fac, wd, cw, wup, x1, g, *, seq, name, tm):
    t, d = dx2.shape
    nj, _, f = up0.shape
    half = nj // 2
    tm = min(tm, seq)
    tiles_per_seq = seq // tm
    nt = t // tm

    def body(dx_ref, up_ref, fac_ref, wd_ref, cw_ref, wup_ref, x1_ref, g_ref,
             dup_ref, dcw_ref, dx1_ref, dg_ref, carry_ref):
        i = pl.program_id(0)
        tile = nt - 1 - i

        @pl.when(i == 0)
        def _():
            dcw_ref[...] = jnp.zeros_like(dcw_ref)
            dg_ref[...] = jnp.zeros_like(dg_ref)
            carry_ref[...] = jnp.zeros_like(carry_ref)

        keep_next = jnp.where(tile % tiles_per_seq == tiles_per_seq - 1, 0.0, 1.0)
        dx2v = dx_ref[...]
        dxb = dx2v.astype(BF16)
        dh = [jnp.zeros((tm, d), F32)]

        def through_conv(j, dup):
            next8 = carry_ref[j] * keep_next
            carry_ref[j] = dup[:HALO]
            dup0, u1, u2 = _conv_bwd(dup, next8, cw_ref[j])
            up0 = up_ref[j]
            dcw_ref[j, 0:1, :] += _colsum(u2 * up0)
            dcw_ref[j, 1:2, :] += _colsum(u1 * up0)
            dcw_ref[j, 2:3, :] += _colsum(dup * up0)
            dup0 = dup0.astype(BF16)
            dup_ref[j] = dup0
            dh[0] = dh[0] + jnp.dot(dup0, wup_ref[j], preferred_element_type=F32)

        dacts = [lax.dot_general(dxb, wd_ref[k], NT_DIMS, preferred_element_type=F32) for k in range(half)]
        for k in range(half):
            through_conv(k, dacts[k] * fac_ref[k].astype(F32))
            through_conv(k + half, dacts[k] * fac_ref[k + half].astype(F32))

        dx, dg = _rms_bwd(dh[0], x1_ref[...], g_ref[...])
        dx1_ref[...] = dx2v + dx
        dg_ref[0:1, :] += dg

    rev = lambda i: nt - 1 - i
    return pl.pallas_call(
        body, name=name, grid=(nt,),
        in_specs=[pl.BlockSpec((tm, d), lambda i: (rev(i), 0)),
                  pl.BlockSpec((nj, tm, f), lambda i: (0, rev(i), 0)),
                  pl.BlockSpec((nj, tm, f), lambda i: (0, rev(i), 0)),
                  _once((half, f, d), lambda i: (0, 0, 0)),
                  _once((nj, HALO, f), lambda i: (0, 0, 0)),
                  _once((nj, f, d), lambda i: (0, 0, 0)),
                  pl.BlockSpec((tm, d), lambda i: (rev(i), 0)),
                  _once((1, d), lambda i: (0, 0))],
        out_specs=[pl.BlockSpec((nj, tm, f), lambda i: (0, rev(i), 0)),
                   pl.BlockSpec((nj, HALO, f), lambda i: (0, 0, 0)),
                   pl.BlockSpec((tm, d), lambda i: (rev(i), 0)),
                   pl.BlockSpec((HALO, d), lambda i: (0, 0))],
        out_shape=[jax.ShapeDtypeStruct((nj, t, f), BF16), jax.ShapeDtypeStruct((nj, HALO, f), F32),
                   jax.ShapeDtypeStruct((t, d), F32), jax.ShapeDtypeStruct((HALO, d), F32)],
        scratch_shapes=[pltpu.VMEM((nj, HALO, f), F32)],
        compiler_params=_params(1),
    )(dx2, up0, fac, wd, cw, wup, x1, g)


def mixer_bwd(dx1, proj, kept, wout, cw, lng, wst, *, seq, name, tm, on_to_x0=None):
    t, d = dx1.shape
    tm = min(tm, seq)
    tiles_per_seq = seq // tm
    nt = t // tm
    gd = d // N_GROUPS
    if on_to_x0 is not None:
        nj, _, wn = on_to_x0[0].shape

    def body(dx_ref, p_ref, k_ref, wout_ref, cw_ref, lng_ref, wst_ref, *rest):
        if on_to_x0 is None:
            dp_ref, dcw_ref, dln_ref, dws_ref, dbs_ref, dvn_scr, carry_ref, dbs_acc = rest
        else:
            (win_ref, x0_ref, g_ref, dp_ref, dcw_ref, dln_ref, dws_ref, dbs_ref, dx0_ref, dg_ref,
             dvn_scr, carry_ref, dbs_acc) = rest
        i = pl.program_id(0)
        tile = nt - 1 - i

        @pl.when(i == 0)
        def _():
            dcw_ref[...] = jnp.zeros_like(dcw_ref)
            dln_ref[...] = jnp.zeros_like(dln_ref)
            dws_ref[...] = jnp.zeros_like(dws_ref)
            dbs_acc[...] = jnp.zeros_like(dbs_acc)
            if on_to_x0 is not None:
                dg_ref[...] = jnp.zeros_like(dg_ref)
            carry_ref[...] = jnp.zeros_like(carry_ref)

        keep_next = jnp.where(tile % tiles_per_seq == tiles_per_seq - 1, 0.0, 1.0)
        cw = cw_ref[...]
        lng = lng_ref[...]
        b = p_ref[:, 0:d]
        c = p_ref[:, d:2 * d]
        xi = p_ref[:, 2 * d:3 * d]
        u = p_ref[:, 3 * d:4 * d]
        v = p_ref[:, 4 * d:5 * d]
        conv, mixed, vnb, sa, sb = (k_ref[:, k * d:(k + 1) * d] for k in range(len(KEPT)))
        conv, mixed, sa, sb = (a.astype(F32) for a in (conv, mixed, sa, sb))
        xc = v - jnp.mean(v, axis=-1, keepdims=True)
        rstd = lax.rsqrt(jnp.mean(xc * xc, axis=-1, keepdims=True) + EPS)
        vhat = xc * rstd
        dmerged = lax.dot_general(dx_ref[...].astype(BF16), wout_ref[...], NT_DIMS, preferred_element_type=F32)
        dp_ref[:, 5 * d:6 * d] = (dmerged * (b * conv) * (sa * (1.0 - sa))).astype(BF16)
        dp_ref[:, 6 * d:7 * d] = (dmerged * (u * mixed) * (sb * (1.0 - sb))).astype(BF16)
        dya = dmerged * sa
        dyb = dmerged * sb
        dp_ref[:, 0:d] = (dya * conv).astype(BF16)
        dconv = dya * b
        next8 = carry_ref[...] * keep_next
        carry_ref[...] = dconv[:HALO]
        dcx, u1, u2 = _conv_bwd(dconv, next8, cw)
        cx = c * xi
        dcw_ref[0:1, :] += _colsum(u2 * cx)
        dcw_ref[1:2, :] += _colsum(u1 * cx)
        dcw_ref[2:3, :] += _colsum(dconv * cx)
        dp_ref[:, d:2 * d] = (dcx * xi).astype(BF16)
        dp_ref[:, 2 * d:3 * d] = (dcx * c).astype(BF16)
        dp_ref[:, 3 * d:4 * d] = (dyb * mixed).astype(BF16)
        dmixed = dyb * u
        dmb = dmixed.astype(BF16)
        tril = (lax.broadcasted_iota(jnp.int32, (CHUNK, CHUNK), 0)
                >= lax.broadcasted_iota(jnp.int32, (CHUNK, CHUNK), 1))
        triu = (lax.broadcasted_iota(jnp.int32, (CHUNK, CHUNK), 0)
                <= lax.broadcasted_iota(jnp.int32, (CHUNK, CHUNK), 1))
        dbs_tile = dmixed[0:CHUNK]
        for n in range(1, tm // CHUNK):
            dbs_tile = dbs_tile + dmixed[n * CHUNK:(n + 1) * CHUNK]
        dbs_acc[...] += dbs_tile
        for g in range(N_GROUPS):
            wmt = jnp.where(triu, wst_ref[g], 0.0).astype(BF16)
            cols = slice(g * gd, (g + 1) * gd)
            dw = jnp.zeros((CHUNK, CHUNK), F32)
            for n in range(tm // CHUNK):
                rows = slice(n * CHUNK, (n + 1) * CHUNK)
                dvn_scr[rows, cols] = jnp.dot(wmt, dmb[rows, cols], preferred_element_type=F32)
                dw = dw + lax.dot_general(dmb[rows, cols], vnb[rows, cols], NT_DIMS, preferred_element_type=F32)
            dws_ref[g] += jnp.where(tril, dw, 0.0)
        dvn = dvn_scr[...]
        dln_ref[0:1, :] += _colsum(dvn * vhat)
        dln_ref[1:2, :] += _colsum(dvn)
        dvh = dvn * lng
        dv = rstd * (dvh - jnp.mean(dvh, axis=-1, keepdims=True)
                     - vhat * jnp.mean(dvh * vhat, axis=-1, keepdims=True))
        dp_ref[:, 4 * d:5 * d] = dv.astype(BF16)
        if on_to_x0 is not None:
            dh = jnp.zeros((tm, d), F32)
            for j in range(0, nj, 2):
                pair = jnp.concatenate([win_ref[j], win_ref[j + 1]], axis=1)
                dh = dh + lax.dot_general(dp_ref[:, j * wn:(j + 2) * wn], pair, NT_DIMS, preferred_element_type=F32)
            dx, dg = _rms_bwd(dh, x0_ref[...], g_ref[...])
            dx0_ref[...] = dx_ref[...] + dx
            dg_ref[0:1, :] += dg

        @pl.when(i == nt - 1)
        def _():
            for g in range(N_GROUPS):
                cols = slice(g * gd, (g + 1) * gd)
                s = jnp.sum(dbs_acc[:, cols], axis=1, keepdims=True)
                dbs_ref[:, cols] = jnp.broadcast_to(s, (CHUNK, gd))

    rev = lambda i: nt - 1 - i
    const2 = lambda i: (0, 0)
    const3 = lambda i: (0, 0, 0)
    row = lambda i: (rev(i), 0)
    in_specs = [pl.BlockSpec((tm, d), row), pl.BlockSpec((tm, 5 * d), row), pl.BlockSpec((tm, len(KEPT) * d), row),
                _once((d, d), const2), _once((HALO, d), const2), _once((1, d), const2),
                _once((N_GROUPS, CHUNK, CHUNK), const3)]
    out_specs = [pl.BlockSpec((tm, 7 * d), row), pl.BlockSpec((HALO, d), const2), pl.BlockSpec((HALO, d), const2),
                 pl.BlockSpec((N_GROUPS, CHUNK, CHUNK), const3), pl.BlockSpec((CHUNK, d), const2)]
    out_shape = [jax.ShapeDtypeStruct((t, 7 * d), BF16), jax.ShapeDtypeStruct((HALO, d), F32),
                 jax.ShapeDtypeStruct((HALO, d), F32), jax.ShapeDtypeStruct((N_GROUPS, CHUNK, CHUNK), F32),
                 jax.ShapeDtypeStruct((CHUNK, d), F32)]
    args = [dx1, proj, kept, wout, cw, lng, wst]
    if on_to_x0 is not None:
        in_specs += [_once((nj, d, wn), const3), pl.BlockSpec((tm, d), row), _once((1, d), const2)]
        out_specs += [pl.BlockSpec((tm, d), row), pl.BlockSpec((HALO, d), const2)]
        out_shape += [jax.ShapeDtypeStruct((t, d), F32), jax.ShapeDtypeStruct((HALO, d), F32)]
        args += list(on_to_x0)
    return pl.pallas_call(
        body, name=name, grid=(nt,), in_specs=in_specs, out_specs=out_specs, out_shape=out_shape,
        scratch_shapes=[pltpu.VMEM((tm, d), F32), pltpu.VMEM((HALO, d), F32), pltpu.VMEM((CHUNK, d), F32)],
        compiler_params=_params(1),
    )(*args)


def dgrad_rms(dy, w, x, g, res, *, name, tm):
    t, d = x.shape
    n = w.shape[2]
    w = w.reshape(w.shape[0] // 2, 2, d, n)
    nj = w.shape[0]
    tm = min(tm, t)

    def body(dy_ref, w_ref, x_ref, g_ref, res_ref, dx_ref, dg_ref, acc_ref):
        i, j = pl.program_id(0), pl.program_id(1)

        @pl.when((i == 0) & (j == 0))
        def _():
            dg_ref[...] = jnp.zeros_like(dg_ref)

        pair = jnp.concatenate([w_ref[0], w_ref[1]], axis=1)
        part = lax.dot_general(dy_ref[...], pair, NT_DIMS, preferred_element_type=F32)

        @pl.when(j == 0)
        def _():
            acc_ref[...] = part

        @pl.when(j > 0)
        def _():
            acc_ref[...] += part

        @pl.when(j == nj - 1)
        def _():
            dx, dg = _rms_bwd(acc_ref[...], x_ref[...], g_ref[...])
            dx_ref[...] = res_ref[...] + dx
            dg_ref[0:1, :] += dg

    return pl.pallas_call(
        body, name=name, grid=(t // tm, nj),
        in_specs=[pl.BlockSpec((tm, 2 * n), lambda i, j: (i, j)),
                  pl.BlockSpec((None, 2, d, n), lambda i, j: (j, 0, 0, 0)),
                  pl.BlockSpec((tm, d), lambda i, j: (i, 0)),
                  pl.BlockSpec((1, d), lambda i, j: (0, 0)),
                  pl.BlockSpec((tm, d), lambda i, j: (i, 0))],
        out_specs=[pl.BlockSpec((tm, d), lambda i, j: (i, 0)), pl.BlockSpec((HALO, d), lambda i, j: (0, 0))],
        out_shape=[jax.ShapeDtypeStruct((t, d), F32), jax.ShapeDtypeStruct((HALO, d), F32)],
        scratch_shapes=[pltpu.VMEM((tm, d), F32)],
        compiler_params=_params(2),
    )(dy, w, x, g, res)


def wgrad(a, b, *, nj, a_mode, b_mode, name, tm, split=1):
    def describe(arr, mode):
        if mode == "full":
            return arr.shape[0], arr.shape[1], pl.BlockSpec((tm_, arr.shape[1]), lambda j, s: (s, 0))
        if mode == "cols":
            c = arr.shape[1] // nj
            return arr.shape[0], c, pl.BlockSpec((tm_, c), lambda j, s: (s, j))
        return arr.shape[1], arr.shape[2], pl.BlockSpec((None, tm_, arr.shape[2]), lambda j, s: (j, s, 0))

    t = a.shape[0] if a_mode != "lead" else a.shape[1]
    tm_ = min(tm, t)
    _, k, a_spec = describe(a, a_mode)
    _, n, b_spec = describe(b, b_mode)

    ns = t // tm_
    nc = n // split

    def body(a_ref, b_ref, o_ref, acc_ref):
        s = pl.program_id(1)
        part = lax.dot_general(a_ref[...], b_ref[...], TN_DIMS, preferred_element_type=F32)

        def finish(total):
            for q in range(split):
                o_ref[q] = total[:, q * nc:(q + 1) * nc].astype(BF16)

        if ns == 1:
            finish(part)
            return

        @pl.when(s == 0)
        def _():
            acc_ref[...] = part

        @pl.when((s > 0) & (s < ns - 1))
        def _():
            acc_ref[...] += part

        @pl.when(s == ns - 1)
        def _():
            finish(acc_ref[...] + part)

    return pl.pallas_call(
        body, name=name, grid=(nj, ns),
        in_specs=[a_spec, b_spec],
        out_specs=pl.BlockSpec((split, k, nc), lambda j, s: (j, 0, 0)),
        out_shape=jax.ShapeDtypeStruct((nj * split, k, nc), BF16),
        scratch_shapes=[pltpu.VMEM((k, n), F32)],
        compiler_params=_params(2),
    )(a, b)


def _adamw_math(w, g, m, v):
    m = ADAM_B1 * m + (1.0 - ADAM_B1) * g
    v = ADAM_B2 * v + (1.0 - ADAM_B2) * (g * g)
    m_hat = m / (1.0 - ADAM_B1 ** ADAM_STEP)
    v_hat = v / (1.0 - ADAM_B2 ** ADAM_STEP)
    delta = -ADAM_LR * (m_hat / (jnp.sqrt(v_hat) + ADAM_EPS) + ADAM_WD * w)
    return delta, m, v


def _row_tile(rows, at_most):
    if rows <= at_most:
        return rows
    return max(k for k in range(16, at_most + 1, 16) if rows % k == 0)


def _sum_in_device_order(ref):
    total = ref[0]
    for s in range(1, N_DEV):
        total = total + ref[s]
    return total


def adamw_sharded(me, own0, recv0, own1, recv1, w, m, v, *, name, tr):
    _, r, c = w.shape
    tr = _row_tile(r, tr)
    ni = r // tr

    def body(me_ref, o0_ref, r0_ref, o1_ref, r1_ref, w_ref, m_ref, v_ref, g_ref, d_ref, nm_ref, nv_ref):
        def finish(own_ref, recv_ref):
            g = None
            for s in range(N_DEV):
                term = jnp.where(me_ref[0] == s, own_ref[...], recv_ref[s]).astype(F32)
                g = term if g is None else g + term
            delta, nm, nv = _adamw_math(w_ref[...], g, m_ref[...], v_ref[...])
            g_ref[...] = g
            d_ref[...] = delta
            nm_ref[...] = nm
            nv_ref[...] = nv

        @pl.when(pl.program_id(0) == 0)
        def _():
            finish(o0_ref, r0_ref)

        @pl.when(pl.program_id(0) == 1)
        def _():
            finish(o1_ref, r1_ref)

    row0 = lambda l, i: i * (1 - l) + (ni - 1) * l
    row1 = lambda l, i: i * l
    lay = pl.BlockSpec((None, tr, c), lambda l, i, me_ref: (l, i, 0))
    grid_spec = pltpu.PrefetchScalarGridSpec(
        num_scalar_prefetch=1, grid=(2, ni),
        in_specs=[pl.BlockSpec((None, tr, c), lambda l, i, me_ref: (me_ref[0], row0(l, i), 0)),
                  pl.BlockSpec((N_DEV, tr, c), lambda l, i, me_ref: (0, row0(l, i), 0)),
                  pl.BlockSpec((None, tr, c), lambda l, i, me_ref: (me_ref[0], row1(l, i), 0)),
                  pl.BlockSpec((N_DEV, tr, c), lambda l, i, me_ref: (0, row1(l, i), 0)),
                  lay, lay, lay],
        out_specs=[lay, lay, lay, lay])
    return pl.pallas_call(
        body, name=name, grid_spec=grid_spec,
        out_shape=[jax.ShapeDtypeStruct(w.shape, F32)] * 4,
        compiler_params=_params(2),
    )(me, own0, recv0, own1, recv1, w, m, v)


def sum_chunks(me, own, recv, *, name):
    _, r, c = own.shape

    def body(me_ref, o_ref, r_ref, out_ref):
        total = None
        for s in range(N_DEV):
            term = jnp.where(me_ref[0] == s, o_ref[...], r_ref[s]).astype(F32)
            total = term if total is None else total + term
        out_ref[...] = total

    grid_spec = pltpu.PrefetchScalarGridSpec(
        num_scalar_prefetch=1, grid=(1,),
        in_specs=[pl.BlockSpec((None, r, c), lambda i, me_ref: (me_ref[0], 0, 0)),
                  pl.BlockSpec((N_DEV, r, c), lambda i, me_ref: (0, 0, 0))],
        out_specs=pl.BlockSpec((r, c), lambda i, me_ref: (0, 0)))
    return pl.pallas_call(
        body, name=name, grid_spec=grid_spec,
        out_shape=jax.ShapeDtypeStruct((r, c), F32),
        compiler_params=_params(1),
    )(me, own, recv)


def adamw_small(g, w, m, v, *, name):
    def body(g_ref, w_ref, m_ref, v_ref, d_ref, nm_ref, nv_ref):
        delta, nm, nv = _adamw_math(w_ref[...], g_ref[...], m_ref[...], v_ref[...])
        d_ref[...] = delta
        nm_ref[...] = nm
        nv_ref[...] = nv

    return pl.pallas_call(
        body, name=name,
        out_shape=[jax.ShapeDtypeStruct(w.shape, F32)] * 3,
        compiler_params=pltpu.CompilerParams(vmem_limit_bytes=VMEM_LIMIT_BYTES),
    )(g, w, m, v)


def sum_devices(parts, *, name, tr):
    _, r, c = parts.shape
    tr = min(tr, r)

    def body(p_ref, o_ref):
        o_ref[...] = _sum_in_device_order(p_ref)

    return pl.pallas_call(
        body, name=name, grid=(r // tr,),
        in_specs=[pl.BlockSpec((N_DEV, tr, c), lambda i: (0, i, 0))],
        out_specs=pl.BlockSpec((tr, c), lambda i: (i, 0)),
        out_shape=jax.ShapeDtypeStruct((r, c), F32),
        compiler_params=_params(1),
    )(parts)


def _my_place():
    return lax.axis_index("x"), lax.axis_index("y"), lax.axis_index("c")


def all_gather(arrays, after, *, name):
    n = len(arrays)

    def body(*refs):
        ins, outs = refs[:n], refs[n + 1:2 * n + 1]
        send_sems, recv_sems, local_sems = refs[2 * n + 1:]
        x, y, c = _my_place()
        me, sibling = (x, y, c), (x, y, 1 - c)
        chips = [(1 - x, y), (x, 1 - y), (1 - x, 1 - y)]
        waits = []
        for a in range(n):
            def slot(place, a=a):
                px, py, pc = place
                return outs[a].at[4 * px + 2 * py + pc]

            def copy(k, block, to, src=None, a=a, slot=slot):
                return pltpu.make_async_remote_copy(
                    src_ref=slot(block) if src is None else src, dst_ref=slot(block),
                    send_sem=send_sems.at[a, k], recv_sem=recv_sems.at[a, k],
                    device_id=to, device_id_type=MESH)

            mine = pltpu.make_async_copy(ins[a], slot(me), local_sems.at[a])
            mine.start()
            first = [copy(0, me, sibling, src=ins[a])]
            first += [copy(1 + j, me, (*chip, c), src=ins[a]) for j, chip in enumerate(chips)]
            for cp in first:
                cp.start()
            waits.append((copy, mine, first))
        sends = []
        for a in range(n):
            copy, mine, first = waits[a]
            passed = [copy(4 + j, (*chip, c), sibling) for j, chip in enumerate(chips)]
            for j, chip in enumerate(chips):
                copy(1 + j, (*chip, c), me).wait_recv()
                passed[j].start()
            sends.append(first + passed)
        for a in range(n):
            copy, mine, first = waits[a]
            copy(0, sibling, me).wait_recv()
            for j, chip in enumerate(chips):
                copy(4 + j, (*chip, 1 - c), me).wait_recv()
            for cp in sends[a]:
                cp.wait_send()
            mine.wait()

    return pl.pallas_call(
        body, name=name,
        in_specs=[ANY] * (n + 1), out_specs=[ANY] * n,
        out_shape=[jax.ShapeDtypeStruct((N_DEV,) + a.shape, a.dtype) for a in arrays],
        scratch_shapes=[pltpu.SemaphoreType.DMA((n, 7)), pltpu.SemaphoreType.DMA((n, 7)),
                        pltpu.SemaphoreType.DMA((n,))],
        compiler_params=pltpu.CompilerParams(has_side_effects=True),
    )(*arrays, after)


def _peer_place(r, x, y, c):
    fx, fy, fc = (r >> 2) & 1, (r >> 1) & 1, r & 1
    return (1 - x if fx else x, 1 - y if fy else y, 1 - c if fc else c)


def own_slot(me, w, layer, dtype, *, name, tr):
    _, r, c = w.shape
    tr = _row_tile(r, tr)

    def body(me_ref, w_ref, o_ref):
        o_ref[...] = w_ref[...].astype(dtype)

    grid_spec = pltpu.PrefetchScalarGridSpec(
        num_scalar_prefetch=1, grid=(r // tr,),
        in_specs=[pl.BlockSpec((None, tr, c), lambda i, me_ref: (layer, i, 0))],
        out_specs=pl.BlockSpec((None, tr, c), lambda i, me_ref: (me_ref[0], i, 0)))
    return pl.pallas_call(
        body, name=name, grid_spec=grid_spec,
        out_shape=jax.ShapeDtypeStruct((N_DEV, r, c), dtype),
        compiler_params=_params(1),
    )(me, w)


EXCHANGES = {
    "scatter": [(0, r) for r in range(1, N_DEV)],
    "gather": [(0, r) for r in range(1, N_DEV)],
    "gather_chips": [(0, r) for r in (1, 2, 4, 6)],
    "gather_forward": [(q, 1) for q in (2, 4, 6)],
}


def _split_copy(k, entry, src, land, send_sem, recv_sem, arriving):
    slot, peer = entry
    x, y, c = _my_place()

    def index(relation):
        px, py, pc = _peer_place(relation, x, y, c)
        return 4 * px + 2 * py + pc

    return pltpu.make_async_remote_copy(
        src_ref=land.at[index(slot)] if src is None else src.at[index(peer)],
        dst_ref=land.at[index(slot ^ peer if arriving else slot)],
        send_sem=send_sem.at[k], recv_sem=recv_sem.at[k],
        device_id=_peer_place(peer, x, y, c), device_id_type=MESH)


def start_copies(srcs, lands, *, mode, name, after=None):
    n = len(lands)
    entries = EXCHANGES[mode]
    bufs = (list(srcs) if srcs is not None else []) + list(lands)
    nb = len(bufs)

    def body(*refs):
        src = refs[:n] if srcs is not None else [None] * n
        land = refs[nb - n:nb]
        outs = refs[nb + len(extra):]
        send_sems, recv_sems = outs[:n], outs[n:2 * n]
        token = outs[2 * n + nb]
        for a in range(n):
            for k, entry in enumerate(entries):
                _split_copy(k, entry, src[a], land[a], send_sems[a], recv_sems[a], False).start()
        token[...] = jnp.zeros_like(token)

    extra = [] if after is None else [after]
    outs = pl.pallas_call(
        body, name=name,
        in_specs=[HBM_SPEC] * nb + [ANY] * len(extra),
        out_specs=[SEM_SPEC] * (2 * n) + [HBM_SPEC] * nb + [pl.BlockSpec(memory_space=pltpu.VMEM)],
        out_shape=([pltpu.SemaphoreType.DMA((len(entries),))] * (2 * n)
                   + [pltpu.HBM(a.shape, a.dtype) for a in bufs]
                   + [jax.ShapeDtypeStruct((8, 128), F32)]),
        input_output_aliases={i: 2 * n + i for i in range(nb)},
        compiler_params=pltpu.CompilerParams(has_side_effects=DATAFLOW),
    )(*[pltpu.with_memory_space_constraint(a, pltpu.HBM) for a in bufs], *extra)
    thru = list(outs[2 * n:2 * n + nb])
    return dict(send=outs[:n], recv=outs[n:2 * n], src=thru[:n] if srcs is not None else None, land=thru[nb - n:],
                token=outs[2 * n + nb], mode=mode)


def finish_copies(started, which, after, *, name):
    n = len(which)
    entries = EXCHANGES[started["mode"]]
    has_src = started["src"] is not None
    bufs = ([started["src"][i] for i in which] if has_src else []) + [started["land"][i] for i in which]
    nb = len(bufs)

    def body(*refs):
        src = refs[:n] if has_src else [None] * n
        land = refs[nb - n:nb]
        send_sems, recv_sems = refs[nb:nb + n], refs[nb + n:nb + 2 * n]
        for a in range(n):
            for k, entry in enumerate(entries):
                cp = _split_copy(k, entry, src[a], land[a], send_sems[a], recv_sems[a], True)
                cp.wait_send()
                cp.wait_recv()

    outs = pl.pallas_call(
        body, name=name,
        in_specs=[HBM_SPEC] * nb + [SEM_SPEC] * (2 * n) + [ANY],
        out_specs=[HBM_SPEC] * nb,
        out_shape=[pltpu.HBM(a.shape, a.dtype) for a in bufs],
        input_output_aliases={i: i for i in range(nb)},
        compiler_params=pltpu.CompilerParams(has_side_effects=DATAFLOW),
    )(*bufs, *[started["send"][i] for i in which], *[started["recv"][i] for i in which], after)
    return (list(outs[:n]) if has_src else None), list(outs[nb - n:])


def _pad_rows(a, rows):
    pad = [(0, 0)] * a.ndim
    pad[-2] = (0, rows - a.shape[-2])
    return jnp.pad(a, pad)


def kernel(x, mix_norm_g, w_in, conv_a_w, ln_v_g, ln_v_b, w_s, b_s, w_out, ffn_norm_g, w_up, conv_ffn_w, w_down, final_norm_g, loss_target, m_mix_norm_g, m_w_in, m_conv_a_w, m_ln_v_g, m_ln_v_b, m_w_s, m_b_s, m_w_out, m_ffn_norm_g, m_w_up, m_conv_ffn_w, m_w_down, m_final_norm_g, v_mix_norm_g, v_w_in, v_conv_a_w, v_ln_v_g, v_ln_v_b, v_w_s, v_b_s, v_w_out, v_ffn_norm_g, v_w_up, v_conv_ffn_w, v_w_down, v_final_norm_g):
    nb, seq, d = x.shape
    t = nb * seq
    depth = w_in.shape[0]
    f = w_up.shape[2]
    me = 4 * lax.axis_index("x") + 2 * lax.axis_index("y") + lax.axis_index("c")
    xt = x.reshape(t, d)
    tgt = loss_target.reshape(t, d)

    conv_pack = jnp.concatenate([_pad_rows(conv_a_w, HALO), _pad_rows(conv_ffn_w, HALO)], axis=-1)
    me_arr = me.astype(jnp.int32).reshape(1)
    w_up_t, m_w_up_t, v_w_up_t = (jnp.swapaxes(a, 1, 2) for a in (w_up, m_w_up, v_w_up))
    zones, slot_of = [], {}
    for l in range(depth):
        for key, w in (("win", w_in), ("conv", None), ("wout", w_out), ("wup", w_up_t), ("wd", w_down)):
            if key == "conv":
                if l == 0:
                    slot_of["conv"] = len(zones)
                    packed = conv_pack.reshape(1, depth * HALO, conv_pack.shape[-1])
                    zones.append(own_slot(me_arr, packed, 0, F32, name="own_slot_conv", tr=256))
                continue
            slot_of[key, l] = len(zones)
            zones.append(own_slot(me_arr, w, l, BF16, name=f"own_slot_{key}_{l}", tr=256))
    first = [slot_of["win", 0], slot_of["wout", 0], slot_of["conv"]]
    rest = [i for i in range(len(zones)) if i not in first]
    to_chips = start_copies(None, [zones[i] for i in first], mode="gather_chips", name="gather_first_chips")
    gathering = start_copies(None, [zones[i] for i in rest], mode="gather", name="gather_start", after=to_chips["token"])
    _, at_chips = finish_copies(to_chips, [0, 1, 2], gathering["token"], name="wait_first_chips")
    to_sibling = start_copies(None, at_chips, mode="gather_forward", name="gather_first_forward")

    def gathered(keys, after, name):
        return finish_copies(gathering, [rest.index(slot_of[k]) for k in keys], after, name=name)[1]

    saved, layers = [], []
    cur = xt
    for l in range(depth):
        p = dict(mix_g=mix_norm_g[l][None], ffn_g=ffn_norm_g[l][None], lng=ln_v_g[l][None], lnb=ln_v_b[l][None],
                 ws=w_s[l], wst=jnp.swapaxes(w_s[l], 1, 2),
                 bias=jnp.repeat(b_s[l].T, d // N_GROUPS, axis=1))
        if l == 0:
            _, (p["win"], wout_g, conv_g) = finish_copies(to_sibling, [0, 1, 2], to_sibling["token"],
                                                          name=f"wait_w_mixer_{l}")
            conv_g = conv_g.reshape(N_DEV, depth, HALO, -1)
            ca = conv_g.shape[-1] - f
        else:
            p["win"], wout_g = gathered([("win", l), ("wout", l)], after, f"wait_w_mixer_{l}")
        p["wout"] = wout_g.reshape(d, d)
        p["cw_a"] = jnp.transpose(conv_g[:, l, :, :ca], (1, 0, 2)).reshape(HALO, d)
        p["cw_f"] = conv_g[:, l, :, ca:]
        h, proj, merged, x1, kept = mixer_fwd(cur, p["mix_g"], p["win"], p["wout"], p["cw_a"], p["lng"], p["lnb"],
                                              p["ws"], p["bias"], seq=seq, name=f"mixer_fwd_{l}", tm=256)
        p["wup"], wd_g = gathered([("wup", l), ("wd", l)], merged, f"wait_w_ffn_{l}")
        p["wd"] = wd_g.reshape(N_DEV // 2, 2 * wd_g.shape[1], d)
        head = (final_norm_g[None], tgt) if l == depth - 1 else None
        h2, up0, fac, act, x2, *of_loss = ffn_fwd(x1, p["ffn_g"], p["wup"], p["wd"], p["cw_f"],
                                                  seq=seq, name=f"ffn_fwd_{l}", tm=256, head=head)
        saved.append(dict(x0=cur, h=h, proj=proj, kept=kept, merged=merged, x1=x1, h2=h2, up0=up0, fac=fac, act=act))
        layers.append(p)
        cur, after = x2, act
    dx = cur
    d_final_g, loss_tile = of_loss

    def exchange(parts, name):
        return start_copies(parts, [lax.empty(a.shape, a.dtype) for a in parts], mode="scatter", name=name)

    def tied(g, started):
        return g + started["token"][0:1, 0:1]

    part = [None] * depth
    mix_ex = None
    for l in reversed(range(depth)):
        p, s = layers[l], saved[l]
        ffn_g = p["ffn_g"] if mix_ex is None else tied(p["ffn_g"], mix_ex)
        dup0, dcw_f, dx1, d_ffn_g = ffn_bwd(dx, s["up0"], s["fac"], p["wd"], p["cw_f"], p["wup"], s["x1"], ffn_g,
                                            seq=seq, name=f"ffn_bwd_{l}", tm=256)
        g_wd = wgrad(s["act"], dx, nj=N_DEV // 2, a_mode="lead", b_mode="full", name=f"wgrad_down_{l}", tm=2048)
        g_wup = wgrad(dup0, s["h2"], nj=N_DEV, a_mode="lead", b_mode="full", name=f"wgrad_up_{l}", tm=2048)
        ffn_ex = exchange([g_wd.reshape(N_DEV, g_wd.shape[1] // 2, d), g_wup], f"exchange_ffn_{l}")
        fused = l > 0
        dproj, dcw_a, dln, dws, dbs, *to_x0 = mixer_bwd(
            dx1, s["proj"], s["kept"], p["wout"], tied(p["cw_a"], ffn_ex), p["lng"], p["wst"],
            seq=seq, name=f"mixer_bwd_{l}", tm=256, on_to_x0=(p["win"], s["x0"], p["mix_g"]) if fused else None)
        g_wout = wgrad(s["merged"], dx1, nj=1, a_mode="full", b_mode="full", name=f"wgrad_out_{l}", tm=2048)
        g_win = wgrad(s["h"], dproj, nj=N_DEV // 2, a_mode="full", b_mode="cols", name=f"wgrad_in_{l}", tm=2048, split=2)
        cwa_chunks = jnp.transpose(dcw_a.reshape(HALO, N_DEV, d // N_DEV), (1, 0, 2))
        mix_ex = exchange([g_wout.reshape(N_DEV, d // N_DEV, d), g_win, dws, dcw_f, cwa_chunks], f"exchange_mix_{l}")
        if fused:
            dx, d_mix_g = to_x0
        else:
            dx, d_mix_g = dgrad_rms(dproj, p["win"], s["x0"], tied(p["mix_g"], mix_ex), dx1,
                                    name=f"dgrad_in_{l}", tm=1024)
        part[l] = dict(
            ffn_ex=ffn_ex, mix_ex=mix_ex,
            vectors=jnp.concatenate([d_mix_g[0:1], d_ffn_g[0:1], dln[0:2],
                                     dbs[:, ::d // N_GROUPS].T.reshape(1, d)], axis=0))
    grad_x = dx.reshape(nb, seq, d)

    own, recv = {}, {}

    def arrived(l, ex, keys, after):
        srcs, lands = finish_copies(part[l][ex], list(range(len(keys))), after, name=f"wait_{ex}_{l}")
        for k, key in enumerate(keys):
            own[key, l], recv[key, l] = srcs[k], lands[k]
        return lands[1]

    def big(key, w, m, v, name):
        return adamw_sharded(me_arr, own[key, 0], recv[key, 0], own[key, 1], recv[key, 1], w, m, v, name=name, tr=256)

    mix_keys = ("wout", "win", "ws", "cwf", "cwa")
    after = grad_x
    for l in reversed(range(depth)):
        after = arrived(l, "ffn_ex", ("wd", "wup"), after)
        if l > 0:
            after = arrived(l, "mix_ex", mix_keys, after)
    u_wd = big("wd", w_down, m_w_down, v_w_down, "adamw_w_down")
    u_wup = tuple(jnp.swapaxes(a, 1, 2) for a in big("wup", w_up_t, m_w_up_t, v_w_up_t, "adamw_w_up"))
    arrived(0, "mix_ex", mix_keys, u_wup[1])
    u_wout = big("wout", w_out, m_w_out, v_w_out, "adamw_w_out")
    u_win = big("win", w_in, m_w_in, v_w_in, "adamw_w_in")

    def owned(key, l):
        return sum_chunks(me_arr, own[key, l], recv[key, l], name=f"sum_{key}_{l}")

    g_cwf = jnp.stack([owned("cwf", l)[:3] for l in range(depth)])
    g_cwa = jnp.stack([owned("cwa", l)[:3] for l in range(depth)])
    ws_rows = CHUNK * CHUNK // d
    ws_mine = jnp.concatenate([owned("ws", l).reshape(ws_rows, d) for l in range(depth)], axis=0)
    loss_row = jnp.zeros((1, d), F32).at[0, 0].set(loss_tile[0, 0])
    vectors = jnp.concatenate([part[l]["vectors"] for l in range(depth)] + [d_final_g[0:1], loss_row], axis=0)
    vectors = _pad_rows(vectors, -(-vectors.shape[0] // 8) * 8)
    vectors_all, ws_all = all_gather([vectors, ws_mine], u_win[0], name="gather_small_grads")
    vec_sum = sum_devices(vectors_all, name="sum_small", tr=512)
    g_ws = jnp.transpose(ws_all.reshape(N_DEV, depth, CHUNK, CHUNK), (1, 0, 2, 3))
    per_layer = part[0]["vectors"].shape[0]
    g_mix, g_ffn, g_lng, g_lnb = (jnp.stack([vec_sum[l * per_layer + k] for l in range(depth)]) for k in range(4))
    g_bs = jnp.stack([vec_sum[l * per_layer + 4].reshape(N_GROUPS, CHUNK) for l in range(depth)])
    g_final = vec_sum[depth * per_layer]
    loss = vec_sum[depth * per_layer + 1, 0]

    def small_update(g, w, m, v, name):
        shape = w.shape
        two_d = (-1, shape[-1]) if w.ndim > 1 else (1, shape[0])
        out = adamw_small(g.reshape(two_d), w.reshape(two_d), m.reshape(two_d), v.reshape(two_d), name=name)
        return (g.reshape(shape),) + tuple(o.reshape(shape) for o in out)

    u_mix = small_update(g_mix, mix_norm_g, m_mix_norm_g, v_mix_norm_g, "adamw_mix_norm_g")
    u_cwa = small_update(g_cwa, conv_a_w, m_conv_a_w, v_conv_a_w, "adamw_conv_a_w")
    u_lng = small_update(g_lng, ln_v_g, m_ln_v_g, v_ln_v_g, "adamw_ln_v_g")
    u_lnb = small_update(g_lnb, ln_v_b, m_ln_v_b, v_ln_v_b, "adamw_ln_v_b")
    u_ws = small_update(g_ws, w_s, m_w_s, v_w_s, "adamw_w_s")
    u_bs = small_update(g_bs, b_s, m_b_s, v_b_s, "adamw_b_s")
    u_ffn = small_update(g_ffn, ffn_norm_g, m_ffn_norm_g, v_ffn_norm_g, "adamw_ffn_norm_g")
    u_cwf = small_update(g_cwf, conv_ffn_w, m_conv_ffn_w, v_conv_ffn_w, "adamw_conv_ffn_w")
    u_final = small_update(g_final, final_norm_g, m_final_norm_g, v_final_norm_g, "adamw_final_norm_g")

    ordered = [u_mix, u_win, u_cwa, u_lng, u_lnb, u_ws, u_bs, u_wout, u_ffn, u_wup, u_cwf, u_wd, u_final]
    return (loss, grad_x, *[u[0] for u in ordered], *[u[1] for u in ordered],
            *[u[2] for u in ordered], *[u[3] for u in ordered])
```

```python
import functools

import jax
import jax.numpy as jnp
from jax import lax
from jax.experimental import pallas as pl
from jax.experimental.pallas import tpu as pltpu

EPS = 1e-6
CHUNK = 128
N_GROUPS = 8
N_DEV = 8
HALO = 8
ADAM_LR = 0.001
ADAM_B1 = 0.9
ADAM_B2 = 0.999
ADAM_EPS = 1e-08
ADAM_WD = 0.01
ADAM_STEP = 10
VMEM_LIMIT_BYTES = 56 * 1024 * 1024
F32 = jnp.float32
BF16 = jnp.bfloat16
MESH = pl.DeviceIdType.MESH
ANY = pl.BlockSpec(memory_space=pl.ANY)
HBM_SPEC = pl.BlockSpec(memory_space=pltpu.HBM)
SEM_SPEC = pl.BlockSpec(memory_space=pltpu.SEMAPHORE)
DATAFLOW = pltpu.SideEffectType.DATAFLOW_SIDE_EFFECTING
NT_DIMS = (((1,), (1,)), ((), ()))
TN_DIMS = (((0,), (0,)), ((), ()))


def _params(n_grid_axes):
    return pltpu.CompilerParams(dimension_semantics=("arbitrary",) * n_grid_axes,
                                vmem_limit_bytes=VMEM_LIMIT_BYTES)


def _shift_down(cur, prev8, k):
    rolled = pltpu.roll(cur, k, 0)
    prolled = pltpu.roll(prev8, k, 0)
    row = lax.broadcasted_iota(jnp.int32, prev8.shape, 0)
    head = jnp.where(row < k, prolled, rolled[:HALO])
    return jnp.concatenate([head, rolled[HALO:]], axis=0)


def _shift_up(cur, next8, k):
    tm = cur.shape[0]
    rolled = pltpu.roll(cur, tm - k, 0)
    nrolled = pltpu.roll(next8, HALO - k, 0)
    row = lax.broadcasted_iota(jnp.int32, next8.shape, 0)
    tail = jnp.where(row >= HALO - k, nrolled, rolled[tm - HALO:])
    return jnp.concatenate([rolled[:tm - HALO], tail], axis=0)


def _conv_fwd(cur, prev8, cw):
    s1 = _shift_down(cur, prev8, 1)
    s2 = _shift_down(cur, prev8, 2)
    y = s2 * cw[0:1, :] + s1 * cw[1:2, :] + cur * cw[2:3, :]
    return y, s1, s2


def _conv_bwd(d, next8, cw):
    u1 = _shift_up(d, next8, 1)
    u2 = _shift_up(d, next8, 2)
    return d * cw[2:3, :] + u1 * cw[1:2, :] + u2 * cw[0:1, :], u1, u2


def _colsum(a):
    return jnp.sum(a, axis=0, keepdims=True)


def _rms_stats(xv):
    r = lax.rsqrt(jnp.mean(xv * xv, axis=-1, keepdims=True) + EPS)
    return r, xv * r


def _rms_bwd(dh, xv, g):
    r, n = _rms_stats(xv)
    dn = dh * g
    dx = r * (dn - n * jnp.mean(dn * n, axis=-1, keepdims=True))
    return dx, _colsum(dh * n)


def _mixer_forward(p_ref, cprev, xiprev, cw, lng, lnb, ws_ref, bias_ref, mixed_scr, d):
    tm = p_ref.shape[0]
    b = p_ref[:, 0:d]
    c = p_ref[:, d:2 * d]
    xi = p_ref[:, 2 * d:3 * d]
    u = p_ref[:, 3 * d:4 * d]
    v = p_ref[:, 4 * d:5 * d]
    sa = jax.nn.sigmoid(p_ref[:, 5 * d:6 * d])
    sb = jax.nn.sigmoid(p_ref[:, 6 * d:7 * d])
    cx = c * xi
    conv, s1, s2 = _conv_fwd(cx, cprev * xiprev, cw)
    ya = b * conv
    mu = jnp.mean(v, axis=-1, keepdims=True)
    xc = v - mu
    rstd = lax.rsqrt(jnp.mean(xc * xc, axis=-1, keepdims=True) + EPS)
    vhat = xc * rstd
    vnb = (vhat * lng + lnb).astype(BF16)
    tril = (lax.broadcasted_iota(jnp.int32, (CHUNK, CHUNK), 0)
            >= lax.broadcasted_iota(jnp.int32, (CHUNK, CHUNK), 1))
    gd = d // N_GROUPS
    for g in range(N_GROUPS):
        wm = jnp.where(tril, ws_ref[g], 0.0).astype(BF16)
        cols = slice(g * gd, (g + 1) * gd)
        for n in range(tm // CHUNK):
            rows = slice(n * CHUNK, (n + 1) * CHUNK)
            mixed_scr[rows, cols] = (jnp.dot(wm, vnb[rows, cols], preferred_element_type=F32)
                                     + bias_ref[:, cols])
    mixed = mixed_scr[...]
    yb = u * mixed
    merged = sa * ya + sb * yb
    return dict(b=b, c=c, xi=xi, u=u, v=v, sa=sa, sb=sb, cx=cx, s1=s1, s2=s2, conv=conv, ya=ya,
                rstd=rstd, vhat=vhat, vnb=vnb, mixed=mixed, yb=yb, merged=merged, tril=tril)


KEPT = ("b", "c", "xi", "u", "v", "conv", "mixed", "vnb", "sa", "sb")


def _once(block_shape, index_map):
    return pl.BlockSpec(block_shape, index_map, pipeline_mode=pl.Buffered(1))


def mixer_fwd(x, g, win, wout, cw, lng, lnb, ws, bias, *, seq, name, tm):
    t, d = x.shape
    nj, _, n = win.shape
    tm = min(tm, seq)
    tiles_per_seq = seq // tm

    def body(x_ref, g_ref, win_ref, wout_ref, cw_ref, lng_ref, lnb_ref, ws_ref, bias_ref,
             h_ref, merged_ref, x1_ref, kept_ref, p_ref, mixed_scr, carry_ref):
        @pl.when(pl.program_id(0) == 0)
        def _():
            carry_ref[...] = jnp.zeros_like(carry_ref)

        keep = jnp.where(pl.program_id(0) % tiles_per_seq == 0, 0.0, 1.0)
        xv = x_ref[...]
        _, nrm = _rms_stats(xv)
        hb = (nrm * g_ref[...]).astype(BF16)
        h_ref[...] = hb
        for j in range(0, nj, 2):
            pair = jnp.concatenate([win_ref[j], win_ref[j + 1]], axis=1)
            p_ref[:, j * n:(j + 2) * n] = jnp.dot(hb, pair, preferred_element_type=F32)
        f = _mixer_forward(p_ref, carry_ref[...] * keep, 1.0, cw_ref[...], lng_ref[...],
                           lnb_ref[...], ws_ref, bias_ref, mixed_scr, d)
        carry_ref[...] = f["cx"][tm - HALO:]
        for k, key in enumerate(KEPT):
            kept_ref[:, k * d:(k + 1) * d] = f[key].astype(BF16)
        mb = f["merged"].astype(BF16)
        merged_ref[...] = mb
        x1_ref[...] = xv + jnp.dot(mb, wout_ref[...], preferred_element_type=F32)

    const2 = lambda i: (0, 0)
    const3 = lambda i: (0, 0, 0)
    row = lambda i: (i, 0)
    return pl.pallas_call(
        body, name=name, grid=(t // tm,),
        in_specs=[pl.BlockSpec((tm, d), row),
                  _once((1, d), const2),
                  _once((nj, d, n), const3),
                  _once((d, d), const2),
                  _once((HALO, d), const2),
                  _once((1, d), const2),
                  _once((1, d), const2),
                  _once((N_GROUPS, CHUNK, CHUNK), const3),
                  _once((CHUNK, d), const2)],
        out_specs=[pl.BlockSpec((tm, d), row), pl.BlockSpec((tm, d), row), pl.BlockSpec((tm, d), row),
                   pl.BlockSpec((tm, len(KEPT) * d), row)],
        out_shape=[jax.ShapeDtypeStruct((t, d), BF16), jax.ShapeDtypeStruct((t, d), BF16),
                   jax.ShapeDtypeStruct((t, d), F32), jax.ShapeDtypeStruct((t, len(KEPT) * d), BF16)],
        scratch_shapes=[pltpu.VMEM((tm, nj * n), F32), pltpu.VMEM((tm, d), F32), pltpu.VMEM((HALO, d), F32)],
        compiler_params=_params(1),
    )(x, g, win, wout, cw, lng, lnb, ws, bias)


def ffn_fwd(x1, g, wup, wd, cw, *, seq, name, tm, head=None):
    t, d = x1.shape
    nj, f, _ = wup.shape
    half = nj // 2
    tm = min(tm, seq)
    tiles_per_seq = seq // tm

    def body(x1_ref, g_ref, wup_ref, wd_ref, cw_ref, *rest):
        if head is None:
            h2_ref, up_ref, fac_ref, act_ref, x2_ref, carry_ref = rest
        else:
            gf_ref, tgt_ref, h2_ref, up_ref, fac_ref, act_ref, x2_ref, dgf_ref, loss_ref, carry_ref = rest

        @pl.when(pl.program_id(0) == 0)
        def _():
            carry_ref[...] = jnp.zeros_like(carry_ref)
            if head is not None:
                dgf_ref[...] = jnp.zeros_like(dgf_ref)
                loss_ref[...] = jnp.zeros_like(loss_ref)

        keep = jnp.where(pl.program_id(0) % tiles_per_seq == 0, 0.0, 1.0)
        xv = x1_ref[...]
        _, nrm = _rms_stats(xv)
        hb = (nrm * g_ref[...]).astype(BF16)
        h2_ref[...] = hb

        for j in range(nj):
            up_ref[j] = lax.dot_general(hb, wup_ref[j], NT_DIMS, preferred_element_type=F32)

        def conv_of(j):
            up0 = up_ref[j]
            y, _, _ = _conv_fwd(up0, carry_ref[j] * keep, cw_ref[j])
            carry_ref[j] = up0[tm - HALO:]
            return y

        acc = xv
        for k in range(half):
            gate, val = conv_of(k), conv_of(k + half)
            sg = jax.nn.sigmoid(gate)
            silu = gate * sg
            fac_ref[k] = (val * (sg * (1.0 + gate * (1.0 - sg)))).astype(BF16)
            fac_ref[k + half] = silu.astype(BF16)
            a = (silu * val).astype(BF16)
            act_ref[k] = a
            acc = acc + jnp.dot(a, wd_ref[k], preferred_element_type=F32)
        if head is None:
            x2_ref[...] = acc
        else:
            gv = gf_ref[...]
            r, n = _rms_stats(acc)
            err = n * gv - tgt_ref[...]
            loss_ref[...] += 0.5 * jnp.sum(jnp.mean(err * err, axis=-1, keepdims=True))
            dy = err * (1.0 / d)
            dn = dy * gv
            x2_ref[...] = r * (dn - n * jnp.mean(dn * n, axis=-1, keepdims=True))
            dgf_ref[0:1, :] += _colsum(dy * n)

    const3 = lambda i: (0, 0, 0)
    row = lambda i: (i, 0)
    in_specs = [pl.BlockSpec((tm, d), row), _once((1, d), lambda i: (0, 0)), _once((nj, f, d), const3),
                _once((half, f, d), const3), _once((nj, HALO, f), const3)]
    out_specs = [pl.BlockSpec((tm, d), row), pl.BlockSpec((nj, tm, f), lambda i: (0, i, 0)),
                 pl.BlockSpec((nj, tm, f), lambda i: (0, i, 0)), pl.BlockSpec((half, tm, f), lambda i: (0, i, 0)),
                 pl.BlockSpec((tm, d), row)]
    out_shape = [jax.ShapeDtypeStruct((t, d), BF16), jax.ShapeDtypeStruct((nj, t, f), F32),
                 jax.ShapeDtypeStruct((nj, t, f), BF16), jax.ShapeDtypeStruct((half, t, f), BF16),
                 jax.ShapeDtypeStruct((t, d), F32)]
    args = [x1, g, wup, wd, cw]
    if head is not None:
        in_specs += [_once((1, d), lambda i: (0, 0)), pl.BlockSpec((tm, d), row)]
        out_specs += [pl.BlockSpec((HALO, d), lambda i: (0, 0)), pl.BlockSpec((8, 128), lambda i: (0, 0))]
        out_shape += [jax.ShapeDtypeStruct((HALO, d), F32), jax.ShapeDtypeStruct((8, 128), F32)]
        args += list(head)
    return pl.pallas_call(
        body, name=name, grid=(t // tm,), in_specs=in_specs, out_specs=out_specs, out_shape=out_shape,
        scratch_shapes=[pltpu.VMEM((nj, HALO, f), F32)],
        compiler_params=_params(1),
    )(*args)


def ffn_bwd(dx2, up0, fac, wd, cw, wup, x1, g, *, seq, name, tm):
    t, d = dx2.shape
    nj, _, f = up0.shape
    half = nj // 2
    tm = min(tm, seq)
    tiles_per_seq = seq // tm
    nt = t // tm

    def body(dx_ref, up_ref, fac_ref, wd_ref, cw_ref, wup_ref, x1_ref, g_ref,
             dup_ref, dcw_ref, dx1_ref, dg_ref, carry_ref):
        i = pl.program_id(0)
        tile = nt - 1 - i

        @pl.when(i == 0)
        def _():
            dcw_ref[...] = jnp.zeros_like(dcw_ref)
            dg_ref[...] = jnp.zeros_like(dg_ref)
            carry_ref[...] = jnp.zeros_like(carry_ref)

        keep_next = jnp.where(tile % tiles_per_seq == tiles_per_seq - 1, 0.0, 1.0)
        dx2v = dx_ref[...]
        dxb = dx2v.astype(BF16)
        dh = [jnp.zeros((tm, d), F32)]

        def through_conv(j, dup):
            next8 = carry_ref[j] * keep_next
            carry_ref[j] = dup[:HALO]
            dup0, u1, u2 = _conv_bwd(dup, next8, cw_ref[j])
            up0 = up_ref[j]
            dcw_ref[j, 0:1, :] += _colsum(u2 * up0)
            dcw_ref[j, 1:2, :] += _colsum(u1 * up0)
            dcw_ref[j, 2:3, :] += _colsum(dup * up0)
            dup0 = dup0.astype(BF16)
            dup_ref[j] = dup0
            dh[0] = dh[0] + jnp.dot(dup0, wup_ref[j], preferred_element_type=F32)

        dacts = [lax.dot_general(dxb, wd_ref[k], NT_DIMS, preferred_element_type=F32) for k in range(half)]
        for k in range(half):
            through_conv(k, dacts[k] * fac_ref[k].astype(F32))
            through_conv(k + half, dacts[k] * fac_ref[k + half].astype(F32))

        dx, dg = _rms_bwd(dh[0], x1_ref[...], g_ref[...])
        dx1_ref[...] = dx2v + dx
        dg_ref[0:1, :] += dg

    rev = lambda i: nt - 1 - i
    return pl.pallas_call(
        body, name=name, grid=(nt,),
        in_specs=[pl.BlockSpec((tm, d), lambda i: (rev(i), 0)),
                  pl.BlockSpec((nj, tm, f), lambda i: (0, rev(i), 0)),
                  pl.BlockSpec((nj, tm, f), lambda i: (0, rev(i), 0)),
                  _once((half, f, d), lambda i: (0, 0, 0)),
                  _once((nj, HALO, f), lambda i: (0, 0, 0)),
                  _once((nj, f, d), lambda i: (0, 0, 0)),
                  pl.BlockSpec((tm, d), lambda i: (rev(i), 0)),
                  _once((1, d), lambda i: (0, 0))],
        out_specs=[pl.BlockSpec((nj, tm, f), lambda i: (0, rev(i), 0)),
                   pl.BlockSpec((nj, HALO, f), lambda i: (0, 0, 0)),
                   pl.BlockSpec((tm, d), lambda i: (rev(i), 0)),
                   pl.BlockSpec((HALO, d), lambda i: (0, 0))],
        out_shape=[jax.ShapeDtypeStruct((nj, t, f), BF16), jax.ShapeDtypeStruct((nj, HALO, f), F32),
                   jax.ShapeDtypeStruct((t, d), F32), jax.ShapeDtypeStruct((HALO, d), F32)],
        scratch_shapes=[pltpu.VMEM((nj, HALO, f), F32)],
        compiler_params=_params(1),
    )(dx2, up0, fac, wd, cw, wup, x1, g)


def mixer_bwd(dx1, kept, wout, cw, lng, wst, *, seq, name, tm, on_to_x0=None):
    t, d = dx1.shape
    tm = min(tm, seq)
    tiles_per_seq = seq // tm
    nt = t // tm
    gd = d // N_GROUPS
    if on_to_x0 is not None:
        nj, _, wn = on_to_x0[0].shape

    def body(dx_ref, k_ref, wout_ref, cw_ref, lng_ref, wst_ref, *rest):
        if on_to_x0 is None:
            dp_ref, dcw_ref, dln_ref, dws_ref, dbs_ref, dvn_scr, carry_ref, dbs_acc = rest
        else:
            (win_ref, x0_ref, g_ref, dp_ref, dcw_ref, dln_ref, dws_ref, dbs_ref, dx0_ref, dg_ref,
             dvn_scr, carry_ref, dbs_acc) = rest
        i = pl.program_id(0)
        tile = nt - 1 - i

        @pl.when(i == 0)
        def _():
            dcw_ref[...] = jnp.zeros_like(dcw_ref)
            dln_ref[...] = jnp.zeros_like(dln_ref)
            dws_ref[...] = jnp.zeros_like(dws_ref)
            dbs_acc[...] = jnp.zeros_like(dbs_acc)
            if on_to_x0 is not None:
                dg_ref[...] = jnp.zeros_like(dg_ref)
            carry_ref[...] = jnp.zeros_like(carry_ref)

        keep_next = jnp.where(tile % tiles_per_seq == tiles_per_seq - 1, 0.0, 1.0)
        cw = cw_ref[...]
        lng = lng_ref[...]
        kept_f32 = {key: k_ref[:, k * d:(k + 1) * d].astype(F32) for k, key in enumerate(KEPT) if key != "vnb"}
        b, c, xi, u, v, conv, mixed, sa, sb = (kept_f32[key] for key in KEPT if key != "vnb")
        vnb = k_ref[:, KEPT.index("vnb") * d:(KEPT.index("vnb") + 1) * d]
        xc = v - jnp.mean(v, axis=-1, keepdims=True)
        rstd = lax.rsqrt(jnp.mean(xc * xc, axis=-1, keepdims=True) + EPS)
        vhat = xc * rstd
        dmerged = lax.dot_general(dx_ref[...].astype(BF16), wout_ref[...], NT_DIMS, preferred_element_type=F32)
        dp_ref[:, 5 * d:6 * d] = (dmerged * (b * conv) * (sa * (1.0 - sa))).astype(BF16)
        dp_ref[:, 6 * d:7 * d] = (dmerged * (u * mixed) * (sb * (1.0 - sb))).astype(BF16)
        dya = dmerged * sa
        dyb = dmerged * sb
        dp_ref[:, 0:d] = (dya * conv).astype(BF16)
        dconv = dya * b
        next8 = carry_ref[...] * keep_next
        carry_ref[...] = dconv[:HALO]
        dcx, u1, u2 = _conv_bwd(dconv, next8, cw)
        cx = c * xi
        dcw_ref[0:1, :] += _colsum(u2 * cx)
        dcw_ref[1:2, :] += _colsum(u1 * cx)
        dcw_ref[2:3, :] += _colsum(dconv * cx)
        dp_ref[:, d:2 * d] = (dcx * xi).astype(BF16)
        dp_ref[:, 2 * d:3 * d] = (dcx * c).astype(BF16)
        dp_ref[:, 3 * d:4 * d] = (dyb * mixed).astype(BF16)
        dmixed = dyb * u
        dmb = dmixed.astype(BF16)
        tril = (lax.broadcasted_iota(jnp.int32, (CHUNK, CHUNK), 0)
                >= lax.broadcasted_iota(jnp.int32, (CHUNK, CHUNK), 1))
        triu = (lax.broadcasted_iota(jnp.int32, (CHUNK, CHUNK), 0)
                <= lax.broadcasted_iota(jnp.int32, (CHUNK, CHUNK), 1))
        dbs_tile = dmixed[0:CHUNK]
        for n in range(1, tm // CHUNK):
            dbs_tile = dbs_tile + dmixed[n * CHUNK:(n + 1) * CHUNK]
        dbs_acc[...] += dbs_tile
        for g in range(N_GROUPS):
            wmt = jnp.where(triu, wst_ref[g], 0.0).astype(BF16)
            cols = slice(g * gd, (g + 1) * gd)
            dw = jnp.zeros((CHUNK, CHUNK), F32)
            for n in range(tm // CHUNK):
                rows = slice(n * CHUNK, (n + 1) * CHUNK)
                dvn_scr[rows, cols] = jnp.dot(wmt, dmb[rows, cols], preferred_element_type=F32)
                dw = dw + lax.dot_general(dmb[rows, cols], vnb[rows, cols], NT_DIMS, preferred_element_type=F32)
            dws_ref[g] += jnp.where(tril, dw, 0.0)
        dvn = dvn_scr[...]
        dln_ref[0:1, :] += _colsum(dvn * vhat)
        dln_ref[1:2, :] += _colsum(dvn)
        dvh = dvn * lng
        dv = rstd * (dvh - jnp.mean(dvh, axis=-1, keepdims=True)
                     - vhat * jnp.mean(dvh * vhat, axis=-1, keepdims=True))
        dp_ref[:, 4 * d:5 * d] = dv.astype(BF16)
        if on_to_x0 is not None:
            dh = jnp.zeros((tm, d), F32)
            for j in range(0, nj, 2):
                pair = jnp.concatenate([win_ref[j], win_ref[j + 1]], axis=1)
                dh = dh + lax.dot_general(dp_ref[:, j * wn:(j + 2) * wn], pair, NT_DIMS, preferred_element_type=F32)
            dx, dg = _rms_bwd(dh, x0_ref[...], g_ref[...])
            dx0_ref[...] = dx_ref[...] + dx
            dg_ref[0:1, :] += dg

        @pl.when(i == nt - 1)
        def _():
            for g in range(N_GROUPS):
                cols = slice(g * gd, (g + 1) * gd)
                s = jnp.sum(dbs_acc[:, cols], axis=1, keepdims=True)
                dbs_ref[:, cols] = jnp.broadcast_to(s, (CHUNK, gd))

    rev = lambda i: nt - 1 - i
    const2 = lambda i: (0, 0)
    const3 = lambda i: (0, 0, 0)
    row = lambda i: (rev(i), 0)
    in_specs = [pl.BlockSpec((tm, d), row), pl.BlockSpec((tm, len(KEPT) * d), row),
                _once((d, d), const2), _once((HALO, d), const2), _once((1, d), const2),
                _once((N_GROUPS, CHUNK, CHUNK), const3)]
    out_specs = [pl.BlockSpec((tm, 7 * d), row), pl.BlockSpec((HALO, d), const2), pl.BlockSpec((HALO, d), const2),
                 pl.BlockSpec((N_GROUPS, CHUNK, CHUNK), const3), pl.BlockSpec((CHUNK, d), const2)]
    out_shape = [jax.ShapeDtypeStruct((t, 7 * d), BF16), jax.ShapeDtypeStruct((HALO, d), F32),
                 jax.ShapeDtypeStruct((HALO, d), F32), jax.ShapeDtypeStruct((N_GROUPS, CHUNK, CHUNK), F32),
                 jax.ShapeDtypeStruct((CHUNK, d), F32)]
    args = [dx1, kept, wout, cw, lng, wst]
    if on_to_x0 is not None:
        in_specs += [_once((nj, d, wn), const3), pl.BlockSpec((tm, d), row), _once((1, d), const2)]
        out_specs += [pl.BlockSpec((tm, d), row), pl.BlockSpec((HALO, d), const2)]
        out_shape += [jax.ShapeDtypeStruct((t, d), F32), jax.ShapeDtypeStruct((HALO, d), F32)]
        args += list(on_to_x0)
    return pl.pallas_call(
        body, name=name, grid=(nt,), in_specs=in_specs, out_specs=out_specs, out_shape=out_shape,
        scratch_shapes=[pltpu.VMEM((tm, d), F32), pltpu.VMEM((HALO, d), F32), pltpu.VMEM((CHUNK, d), F32)],
        compiler_params=_params(1),
    )(*args)


def dgrad_rms(dy, w, x, g, res, *, name, tm):
    t, d = x.shape
    n = w.shape[2]
    w = w.reshape(w.shape[0] // 2, 2, d, n)
    nj = w.shape[0]
    tm = min(tm, t)

    def body(dy_ref, w_ref, x_ref, g_ref, res_ref, dx_ref, dg_ref, acc_ref):
        i, j = pl.program_id(0), pl.program_id(1)

        @pl.when((i == 0) & (j == 0))
        def _():
            dg_ref[...] = jnp.zeros_like(dg_ref)

        pair = jnp.concatenate([w_ref[0], w_ref[1]], axis=1)
        part = lax.dot_general(dy_ref[...], pair, NT_DIMS, preferred_element_type=F32)

        @pl.when(j == 0)
        def _():
            acc_ref[...] = part

        @pl.when(j > 0)
        def _():
            acc_ref[...] += part

        @pl.when(j == nj - 1)
        def _():
            dx, dg = _rms_bwd(acc_ref[...], x_ref[...], g_ref[...])
            dx_ref[...] = res_ref[...] + dx
            dg_ref[0:1, :] += dg

    return pl.pallas_call(
        body, name=name, grid=(t // tm, nj),
        in_specs=[pl.BlockSpec((tm, 2 * n), lambda i, j: (i, j)),
                  pl.BlockSpec((None, 2, d, n), lambda i, j: (j, 0, 0, 0)),
                  pl.BlockSpec((tm, d), lambda i, j: (i, 0)),
                  pl.BlockSpec((1, d), lambda i, j: (0, 0)),
                  pl.BlockSpec((tm, d), lambda i, j: (i, 0))],
        out_specs=[pl.BlockSpec((tm, d), lambda i, j: (i, 0)), pl.BlockSpec((HALO, d), lambda i, j: (0, 0))],
        out_shape=[jax.ShapeDtypeStruct((t, d), F32), jax.ShapeDtypeStruct((HALO, d), F32)],
        scratch_shapes=[pltpu.VMEM((tm, d), F32)],
        compiler_params=_params(2),
    )(dy, w, x, g, res)


def wgrad(a, b, *, nj, a_mode, b_mode, name, tm, split=1):
    def describe(arr, mode):
        if mode == "full":
            return arr.shape[0], arr.shape[1], pl.BlockSpec((tm_, arr.shape[1]), lambda j, s: (s, 0))
        if mode == "cols":
            c = arr.shape[1] // nj
            return arr.shape[0], c, pl.BlockSpec((tm_, c), lambda j, s: (s, j))
        return arr.shape[1], arr.shape[2], pl.BlockSpec((None, tm_, arr.shape[2]), lambda j, s: (j, s, 0))

    t = a.shape[0] if a_mode != "lead" else a.shape[1]
    tm_ = min(tm, t)
    _, k, a_spec = describe(a, a_mode)
    _, n, b_spec = describe(b, b_mode)

    ns = t // tm_
    nc = n // split

    def body(a_ref, b_ref, o_ref, acc_ref):
        s = pl.program_id(1)
        part = lax.dot_general(a_ref[...], b_ref[...], TN_DIMS, preferred_element_type=F32)

        def finish(total):
            for q in range(split):
                o_ref[q] = total[:, q * nc:(q + 1) * nc].astype(BF16)

        if ns == 1:
            finish(part)
            return

        @pl.when(s == 0)
        def _():
            acc_ref[...] = part

        @pl.when((s > 0) & (s < ns - 1))
        def _():
            acc_ref[...] += part

        @pl.when(s == ns - 1)
        def _():
            finish(acc_ref[...] + part)

    return pl.pallas_call(
        body, name=name, grid=(nj, ns),
        in_specs=[a_spec, b_spec],
        out_specs=pl.BlockSpec((split, k, nc), lambda j, s: (j, 0, 0)),
        out_shape=jax.ShapeDtypeStruct((nj * split, k, nc), BF16),
        scratch_shapes=[pltpu.VMEM((k, n), F32)],
        compiler_params=_params(2),
    )(a, b)


def _adamw_math(w, g, m, v):
    m = ADAM_B1 * m + (1.0 - ADAM_B1) * g
    v = ADAM_B2 * v + (1.0 - ADAM_B2) * (g * g)
    m_hat = m / (1.0 - ADAM_B1 ** ADAM_STEP)
    v_hat = v / (1.0 - ADAM_B2 ** ADAM_STEP)
    delta = -ADAM_LR * (m_hat / (jnp.sqrt(v_hat) + ADAM_EPS) + ADAM_WD * w)
    return delta, m, v


def _row_tile(rows, at_most):
    if rows <= at_most:
        return rows
    return max(k for k in range(16, at_most + 1, 16) if rows % k == 0)


def _sum_in_device_order(ref):
    total = ref[0]
    for s in range(1, N_DEV):
        total = total + ref[s]
    return total


def adamw_sharded(me, own0, recv0, own1, recv1, w, m, v, *, name, tr):
    _, r, c = w.shape
    tr = _row_tile(r, tr)
    ni = r // tr

    def body(me_ref, o0_ref, r0_ref, o1_ref, r1_ref, w_ref, m_ref, v_ref, g_ref, d_ref, nm_ref, nv_ref):
        def finish(own_ref, recv_ref):
            g = None
            for s in range(N_DEV):
                term = jnp.where(me_ref[0] == s, own_ref[...], recv_ref[s]).astype(F32)
                g = term if g is None else g + term
            delta, nm, nv = _adamw_math(w_ref[...], g, m_ref[...], v_ref[...])
            g_ref[...] = g
            d_ref[...] = delta
            nm_ref[...] = nm
            nv_ref[...] = nv

        @pl.when(pl.program_id(0) == 0)
        def _():
            finish(o0_ref, r0_ref)

        @pl.when(pl.program_id(0) == 1)
        def _():
            finish(o1_ref, r1_ref)

    row0 = lambda l, i: i * (1 - l) + (ni - 1) * l
    row1 = lambda l, i: i * l
    lay = pl.BlockSpec((None, tr, c), lambda l, i, me_ref: (l, i, 0))
    grid_spec = pltpu.PrefetchScalarGridSpec(
        num_scalar_prefetch=1, grid=(2, ni),
        in_specs=[pl.BlockSpec((None, tr, c), lambda l, i, me_ref: (me_ref[0], row0(l, i), 0)),
                  pl.BlockSpec((N_DEV, tr, c), lambda l, i, me_ref: (0, row0(l, i), 0)),
                  pl.BlockSpec((None, tr, c), lambda l, i, me_ref: (me_ref[0], row1(l, i), 0)),
                  pl.BlockSpec((N_DEV, tr, c), lambda l, i, me_ref: (0, row1(l, i), 0)),
                  lay, lay, lay],
        out_specs=[lay, lay, lay, lay])
    return pl.pallas_call(
        body, name=name, grid_spec=grid_spec,
        out_shape=[jax.ShapeDtypeStruct(w.shape, F32)] * 4,
        compiler_params=_params(2),
    )(me, own0, recv0, own1, recv1, w, m, v)


def sum_chunks(me, own, recv, *, name):
    _, r, c = own.shape

    def body(me_ref, o_ref, r_ref, out_ref):
        total = None
        for s in range(N_DEV):
            term = jnp.where(me_ref[0] == s, o_ref[...], r_ref[s]).astype(F32)
            total = term if total is None else total + term
        out_ref[...] = total

    grid_spec = pltpu.PrefetchScalarGridSpec(
        num_scalar_prefetch=1, grid=(1,),
        in_specs=[pl.BlockSpec((None, r, c), lambda i, me_ref: (me_ref[0], 0, 0)),
                  pl.BlockSpec((N_DEV, r, c), lambda i, me_ref: (0, 0, 0))],
        out_specs=pl.BlockSpec((r, c), lambda i, me_ref: (0, 0)))
    return pl.pallas_call(
        body, name=name, grid_spec=grid_spec,
        out_shape=jax.ShapeDtypeStruct((r, c), F32),
        compiler_params=_params(1),
    )(me, own, recv)


def adamw_small(g, w, m, v, *, name):
    def body(g_ref, w_ref, m_ref, v_ref, d_ref, nm_ref, nv_ref):
        delta, nm, nv = _adamw_math(w_ref[...], g_ref[...], m_ref[...], v_ref[...])
        d_ref[...] = delta
        nm_ref[...] = nm
        nv_ref[...] = nv

    return pl.pallas_call(
        body, name=name,
        out_shape=[jax.ShapeDtypeStruct(w.shape, F32)] * 3,
        compiler_params=pltpu.CompilerParams(vmem_limit_bytes=VMEM_LIMIT_BYTES),
    )(g, w, m, v)


def sum_devices(parts, *, name, tr):
    _, r, c = parts.shape
    tr = min(tr, r)

    def body(p_ref, o_ref):
        o_ref[...] = _sum_in_device_order(p_ref)

    return pl.pallas_call(
        body, name=name, grid=(r // tr,),
        in_specs=[pl.BlockSpec((N_DEV, tr, c), lambda i: (0, i, 0))],
        out_specs=pl.BlockSpec((tr, c), lambda i: (i, 0)),
        out_shape=jax.ShapeDtypeStruct((r, c), F32),
        compiler_params=_params(1),
    )(parts)


def _my_place():
    return lax.axis_index("x"), lax.axis_index("y"), lax.axis_index("c")


def all_gather(arrays, after, *, name):
    n = len(arrays)

    def body(*refs):
        ins, outs = refs[:n], refs[n + 1:2 * n + 1]
        send_sems, recv_sems, local_sems = refs[2 * n + 1:]
        x, y, c = _my_place()
        me, sibling = (x, y, c), (x, y, 1 - c)
        chips = [(1 - x, y), (x, 1 - y), (1 - x, 1 - y)]
        waits = []
        for a in range(n):
            def slot(place, a=a):
                px, py, pc = place
                return outs[a].at[4 * px + 2 * py + pc]

            def copy(k, block, to, src=None, a=a, slot=slot):
                return pltpu.make_async_remote_copy(
                    src_ref=slot(block) if src is None else src, dst_ref=slot(block),
                    send_sem=send_sems.at[a, k], recv_sem=recv_sems.at[a, k],
                    device_id=to, device_id_type=MESH)

            mine = pltpu.make_async_copy(ins[a], slot(me), local_sems.at[a])
            mine.start()
            first = [copy(0, me, sibling, src=ins[a])]
            first += [copy(1 + j, me, (*chip, c), src=ins[a]) for j, chip in enumerate(chips)]
            for cp in first:
                cp.start()
            waits.append((copy, mine, first))
        sends = []
        for a in range(n):
            copy, mine, first = waits[a]
            passed = [copy(4 + j, (*chip, c), sibling) for j, chip in enumerate(chips)]
            for j, chip in enumerate(chips):
                copy(1 + j, (*chip, c), me).wait_recv()
                passed[j].start()
            sends.append(first + passed)
        for a in range(n):
            copy, mine, first = waits[a]
            copy(0, sibling, me).wait_recv()
            for j, chip in enumerate(chips):
                copy(4 + j, (*chip, 1 - c), me).wait_recv()
            for cp in sends[a]:
                cp.wait_send()
            mine.wait()

    return pl.pallas_call(
        body, name=name,
        in_specs=[ANY] * (n + 1), out_specs=[ANY] * n,
        out_shape=[jax.ShapeDtypeStruct((N_DEV,) + a.shape, a.dtype) for a in arrays],
        scratch_shapes=[pltpu.SemaphoreType.DMA((n, 7)), pltpu.SemaphoreType.DMA((n, 7)),
                        pltpu.SemaphoreType.DMA((n,))],
        compiler_params=pltpu.CompilerParams(has_side_effects=True),
    )(*arrays, after)


def _peer_place(r, x, y, c):
    fx, fy, fc = (r >> 2) & 1, (r >> 1) & 1, r & 1
    return (1 - x if fx else x, 1 - y if fy else y, 1 - c if fc else c)


def own_slot(me, w, layer, dtype, *, name, tr):
    _, r, c = w.shape
    tr = _row_tile(r, tr)

    def body(me_ref, w_ref, o_ref):
        o_ref[...] = w_ref[...].astype(dtype)

    grid_spec = pltpu.PrefetchScalarGridSpec(
        num_scalar_prefetch=1, grid=(r // tr,),
        in_specs=[pl.BlockSpec((None, tr, c), lambda i, me_ref: (layer, i, 0))],
        out_specs=pl.BlockSpec((None, tr, c), lambda i, me_ref: (me_ref[0], i, 0)))
    return pl.pallas_call(
        body, name=name, grid_spec=grid_spec,
        out_shape=jax.ShapeDtypeStruct((N_DEV, r, c), dtype),
        compiler_params=_params(1),
    )(me, w)


EXCHANGES = {
    "scatter": [(0, r) for r in range(1, N_DEV)],
    "gather": [(0, r) for r in range(1, N_DEV)],
    "gather_chips": [(0, r) for r in (1, 2, 4, 6)],
    "gather_forward": [(q, 1) for q in (2, 4, 6)],
}


def _split_copy(k, entry, src, land, send_sem, recv_sem, arriving):
    slot, peer = entry
    x, y, c = _my_place()

    def index(relation):
        px, py, pc = _peer_place(relation, x, y, c)
        return 4 * px + 2 * py + pc

    return pltpu.make_async_remote_copy(
        src_ref=land.at[index(slot)] if src is None else src.at[index(peer)],
        dst_ref=land.at[index(slot ^ peer if arriving else slot)],
        send_sem=send_sem.at[k], recv_sem=recv_sem.at[k],
        device_id=_peer_place(peer, x, y, c), device_id_type=MESH)


def start_copies(srcs, lands, *, mode, name, after=None):
    n = len(lands)
    entries = EXCHANGES[mode]
    bufs = (list(srcs) if srcs is not None else []) + list(lands)
    nb = len(bufs)

    def body(*refs):
        src = refs[:n] if srcs is not None else [None] * n
        land = refs[nb - n:nb]
        outs = refs[nb + len(extra):]
        send_sems, recv_sems = outs[:n], outs[n:2 * n]
        token = outs[2 * n + nb]
        for a in range(n):
            for k, entry in enumerate(entries):
                _split_copy(k, entry, src[a], land[a], send_sems[a], recv_sems[a], False).start()
        token[...] = jnp.zeros_like(token)

    extra = [] if after is None else [after]
    outs = pl.pallas_call(
        body, name=name,
        in_specs=[HBM_SPEC] * nb + [ANY] * len(extra),
        out_specs=[SEM_SPEC] * (2 * n) + [HBM_SPEC] * nb + [pl.BlockSpec(memory_space=pltpu.VMEM)],
        out_shape=([pltpu.SemaphoreType.DMA((len(entries),))] * (2 * n)
                   + [pltpu.HBM(a.shape, a.dtype) for a in bufs]
                   + [jax.ShapeDtypeStruct((8, 128), F32)]),
        input_output_aliases={i: 2 * n + i for i in range(nb)},
        compiler_params=pltpu.CompilerParams(has_side_effects=DATAFLOW),
    )(*[pltpu.with_memory_space_constraint(a, pltpu.HBM) for a in bufs], *extra)
    thru = list(outs[2 * n:2 * n + nb])
    return dict(send=outs[:n], recv=outs[n:2 * n], src=thru[:n] if srcs is not None else None, land=thru[nb - n:],
                token=outs[2 * n + nb], mode=mode)


def finish_copies(started, which, after, *, name):
    n = len(which)
    entries = EXCHANGES[started["mode"]]
    has_src = started["src"] is not None
    bufs = ([started["src"][i] for i in which] if has_src else []) + [started["land"][i] for i in which]
    nb = len(bufs)

    def body(*refs):
        src = refs[:n] if has_src else [None] * n
        land = refs[nb - n:nb]
        send_sems, recv_sems = refs[nb:nb + n], refs[nb + n:nb + 2 * n]
        for a in range(n):
            for k, entry in enumerate(entries):
                cp = _split_copy(k, entry, src[a], land[a], send_sems[a], recv_sems[a], True)
                cp.wait_send()
                cp.wait_recv()

    outs = pl.pallas_call(
        body, name=name,
        in_specs=[HBM_SPEC] * nb + [SEM_SPEC] * (2 * n) + [ANY],
        out_specs=[HBM_SPEC] * nb,
        out_shape=[pltpu.HBM(a.shape, a.dtype) for a in bufs],
        input_output_aliases={i: i for i in range(nb)},
        compiler_params=pltpu.CompilerParams(has_side_effects=DATAFLOW),
    )(*bufs, *[started["send"][i] for i in which], *[started["recv"][i] for i in which], after)
    return (list(outs[:n]) if has_src else None), list(outs[nb - n:])


def _pad_rows(a, rows):
    pad = [(0, 0)] * a.ndim
    pad[-2] = (0, rows - a.shape[-2])
    return jnp.pad(a, pad)


def kernel(x, mix_norm_g, w_in, conv_a_w, ln_v_g, ln_v_b, w_s, b_s, w_out, ffn_norm_g, w_up, conv_ffn_w, w_down, final_norm_g, loss_target, m_mix_norm_g, m_w_in, m_conv_a_w, m_ln_v_g, m_ln_v_b, m_w_s, m_b_s, m_w_out, m_ffn_norm_g, m_w_up, m_conv_ffn_w, m_w_down, m_final_norm_g, v_mix_norm_g, v_w_in, v_conv_a_w, v_ln_v_g, v_ln_v_b, v_w_s, v_b_s, v_w_out, v_ffn_norm_g, v_w_up, v_conv_ffn_w, v_w_down, v_final_norm_g):
    nb, seq, d = x.shape
    t = nb * seq
    depth = w_in.shape[0]
    f = w_up.shape[2]
    me = 4 * lax.axis_index("x") + 2 * lax.axis_index("y") + lax.axis_index("c")
    xt = x.reshape(t, d)
    tgt = loss_target.reshape(t, d)

    conv_pack = jnp.concatenate([_pad_rows(conv_a_w, HALO), _pad_rows(conv_ffn_w, HALO)], axis=-1)
    me_arr = me.astype(jnp.int32).reshape(1)
    w_up_t, m_w_up_t, v_w_up_t = (jnp.swapaxes(a, 1, 2) for a in (w_up, m_w_up, v_w_up))
    zones, slot_of = [], {}
    for l in range(depth):
        for key, w in (("win", w_in), ("conv", None), ("wout", w_out), ("wup", w_up_t), ("wd", w_down)):
            if key == "conv":
                if l == 0:
                    slot_of["conv"] = len(zones)
                    packed = conv_pack.reshape(1, depth * HALO, conv_pack.shape[-1])
                    zones.append(own_slot(me_arr, packed, 0, F32, name="own_slot_conv", tr=256))
                continue
            slot_of[key, l] = len(zones)
            zones.append(own_slot(me_arr, w, l, BF16, name=f"own_slot_{key}_{l}", tr=256))
    first = [slot_of["win", 0], slot_of["wout", 0], slot_of["conv"]]
    rest = [i for i in range(len(zones)) if i not in first]
    to_chips = start_copies(None, [zones[i] for i in first], mode="gather_chips", name="gather_first_chips")
    gathering = start_copies(None, [zones[i] for i in rest], mode="gather", name="gather_start", after=to_chips["token"])
    _, at_chips = finish_copies(to_chips, [0, 1, 2], gathering["token"], name="wait_first_chips")
    to_sibling = start_copies(None, at_chips, mode="gather_forward", name="gather_first_forward")

    def gathered(keys, after, name):
        return finish_copies(gathering, [rest.index(slot_of[k]) for k in keys], after, name=name)[1]

    saved, layers = [], []
    cur = xt
    for l in range(depth):
        p = dict(mix_g=mix_norm_g[l][None], ffn_g=ffn_norm_g[l][None], lng=ln_v_g[l][None], lnb=ln_v_b[l][None],
                 ws=w_s[l], wst=jnp.swapaxes(w_s[l], 1, 2),
                 bias=jnp.repeat(b_s[l].T, d // N_GROUPS, axis=1))
        if l == 0:
            _, (p["win"], wout_g, conv_g) = finish_copies(to_sibling, [0, 1, 2], to_sibling["token"],
                                                          name=f"wait_w_mixer_{l}")
            conv_g = conv_g.reshape(N_DEV, depth, HALO, -1)
            ca = conv_g.shape[-1] - f
        else:
            p["win"], wout_g = gathered([("win", l), ("wout", l)], after, f"wait_w_mixer_{l}")
        p["wout"] = wout_g.reshape(d, d)
        p["cw_a"] = jnp.transpose(conv_g[:, l, :, :ca], (1, 0, 2)).reshape(HALO, d)
        p["cw_f"] = conv_g[:, l, :, ca:]
        h, merged, x1, kept = mixer_fwd(cur, p["mix_g"], p["win"], p["wout"], p["cw_a"], p["lng"], p["lnb"],
                                        p["ws"], p["bias"], seq=seq, name=f"mixer_fwd_{l}", tm=256)
        p["wup"], wd_g = gathered([("wup", l), ("wd", l)], merged, f"wait_w_ffn_{l}")
        p["wd"] = wd_g.reshape(N_DEV // 2, 2 * wd_g.shape[1], d)
        head = (final_norm_g[None], tgt) if l == depth - 1 else None
        h2, up0, fac, act, x2, *of_loss = ffn_fwd(x1, p["ffn_g"], p["wup"], p["wd"], p["cw_f"],
                                                  seq=seq, name=f"ffn_fwd_{l}", tm=256, head=head)
        saved.append(dict(x0=cur, h=h, kept=kept, merged=merged, x1=x1, h2=h2, up0=up0, fac=fac, act=act))
        layers.append(p)
        cur, after = x2, act
    dx = cur
    d_final_g, loss_tile = of_loss

    def exchange(parts, name):
        return start_copies(parts, [lax.empty(a.shape, a.dtype) for a in parts], mode="scatter", name=name)

    def tied(g, started):
        return g + started["token"][0:1, 0:1]

    part = [None] * depth
    mix_ex = None
    for l in reversed(range(depth)):
        p, s = layers[l], saved[l]
        ffn_g = p["ffn_g"] if mix_ex is None else tied(p["ffn_g"], mix_ex)
        dup0, dcw_f, dx1, d_ffn_g = ffn_bwd(dx, s["up0"], s["fac"], p["wd"], p["cw_f"], p["wup"], s["x1"], ffn_g,
                                            seq=seq, name=f"ffn_bwd_{l}", tm=256)
        g_wd = wgrad(s["act"], dx, nj=N_DEV // 2, a_mode="lead", b_mode="full", name=f"wgrad_down_{l}", tm=2048)
        g_wup = wgrad(dup0, s["h2"], nj=N_DEV, a_mode="lead", b_mode="full", name=f"wgrad_up_{l}", tm=2048)
        ffn_ex = exchange([g_wd.reshape(N_DEV, g_wd.shape[1] // 2, d), g_wup], f"exchange_ffn_{l}")
        fused = l > 0
        dproj, dcw_a, dln, dws, dbs, *to_x0 = mixer_bwd(
            dx1, s["kept"], p["wout"], tied(p["cw_a"], ffn_ex), p["lng"], p["wst"],
            seq=seq, name=f"mixer_bwd_{l}", tm=256, on_to_x0=(p["win"], s["x0"], p["mix_g"]) if fused else None)
        g_wout = wgrad(s["merged"], dx1, nj=1, a_mode="full", b_mode="full", name=f"wgrad_out_{l}", tm=2048)
        g_win = wgrad(s["h"], dproj, nj=N_DEV // 2, a_mode="full", b_mode="cols", name=f"wgrad_in_{l}", tm=2048, split=2)
        cwa_chunks = jnp.transpose(dcw_a.reshape(HALO, N_DEV, d // N_DEV), (1, 0, 2))
        mix_ex = exchange([g_wout.reshape(N_DEV, d // N_DEV, d), g_win, dws, dcw_f, cwa_chunks], f"exchange_mix_{l}")
        if fused:
            dx, d_mix_g = to_x0
        else:
            dx, d_mix_g = dgrad_rms(dproj, p["win"], s["x0"], tied(p["mix_g"], mix_ex), dx1,
                                    name=f"dgrad_in_{l}", tm=1024)
        part[l] = dict(
            ffn_ex=ffn_ex, mix_ex=mix_ex,
            vectors=jnp.concatenate([d_mix_g[0:1], d_ffn_g[0:1], dln[0:2],
                                     dbs[:, ::d // N_GROUPS].T.reshape(1, d)], axis=0))
    grad_x = dx.reshape(nb, seq, d)

    own, recv = {}, {}

    def arrived(l, ex, keys, after):
        srcs, lands = finish_copies(part[l][ex], list(range(len(keys))), after, name=f"wait_{ex}_{l}")
        for k, key in enumerate(keys):
            own[key, l], recv[key, l] = srcs[k], lands[k]
        return lands[1]

    def big(key, w, m, v, name):
        return adamw_sharded(me_arr, own[key, 0], recv[key, 0], own[key, 1], recv[key, 1], w, m, v, name=name, tr=256)

    mix_keys = ("wout", "win", "ws", "cwf", "cwa")
    after = grad_x
    for l in reversed(range(depth)):
        after = arrived(l, "ffn_ex", ("wd", "wup"), after)
        if l > 0:
            after = arrived(l, "mix_ex", mix_keys, after)
    u_wd = big("wd", w_down, m_w_down, v_w_down, "adamw_w_down")
    u_wup = tuple(jnp.swapaxes(a, 1, 2) for a in big("wup", w_up_t, m_w_up_t, v_w_up_t, "adamw_w_up"))
    arrived(0, "mix_ex", mix_keys, u_wup[1])
    u_wout = big("wout", w_out, m_w_out, v_w_out, "adamw_w_out")
    u_win = big("win", w_in, m_w_in, v_w_in, "adamw_w_in")

    def owned(key, l):
        return sum_chunks(me_arr, own[key, l], recv[key, l], name=f"sum_{key}_{l}")

    g_cwf = jnp.stack([owned("cwf", l)[:3] for l in range(depth)])
    g_cwa = jnp.stack([owned("cwa", l)[:3] for l in range(depth)])
    ws_rows = CHUNK * CHUNK // d
    ws_mine = jnp.concatenate([owned("ws", l).reshape(ws_rows, d) for l in range(depth)], axis=0)
    loss_row = jnp.zeros((1, d), F32).at[0, 0].set(loss_tile[0, 0])
    vectors = jnp.concatenate([part[l]["vectors"] for l in range(depth)] + [d_final_g[0:1], loss_row], axis=0)
    vectors = _pad_rows(vectors, -(-vectors.shape[0] // 8) * 8)
    vectors_all, ws_all = all_gather([vectors, ws_mine], u_win[0], name="gather_small_grads")
    vec_sum = sum_devices(vectors_all, name="sum_small", tr=512)
    g_ws = jnp.transpose(ws_all.reshape(N_DEV, depth, CHUNK, CHUNK), (1, 0, 2, 3))
    per_layer = part[0]["vectors"].shape[0]
    g_mix, g_ffn, g_lng, g_lnb = (jnp.stack([vec_sum[l * per_layer + k] for l in range(depth)]) for k in range(4))
    g_bs = jnp.stack([vec_sum[l * per_layer + 4].reshape(N_GROUPS, CHUNK) for l in range(depth)])
    g_final = vec_sum[depth * per_layer]
    loss = vec_sum[depth * per_layer + 1, 0]

    def small_update(g, w, m, v, name):
        shape = w.shape
        two_d = (-1, shape[-1]) if w.ndim > 1 else (1, shape[0])
        out = adamw_small(g.reshape(two_d), w.reshape(two_d), m.reshape(two_d), v.reshape(two_d), name=name)
        return (g.reshape(shape),) + tuple(o.reshape(shape) for o in out)

    u_mix = small_update(g_mix, mix_norm_g, m_mix_norm_g, v_mix_norm_g, "adamw_mix_norm_g")
    u_cwa = small_update(g_cwa, conv_a_w, m_conv_a_w, v_conv_a_w, "adamw_conv_a_w")
    u_lng = small_update(g_lng, ln_v_g, m_ln_v_g, v_ln_v_g, "adamw_ln_v_g")
    u_lnb = small_update(g_lnb, ln_v_b, m_ln_v_b, v_ln_v_b, "adamw_ln_v_b")
    u_ws = small_update(g_ws, w_s, m_w_s, v_w_s, "adamw_w_s")
    u_bs = small_update(g_bs, b_s, m_b_s, v_b_s, "adamw_b_s")
    u_ffn = small_update(g_ffn, ffn_norm_g, m_ffn_norm_g, v_ffn_norm_g, "adamw_ffn_norm_g")
    u_cwf = small_update(g_cwf, conv_ffn_w, m_conv_ffn_w, v_conv_ffn_w, "adamw_conv_ffn_w")
    u_final = small_update(g_final, final_norm_g, m_final_norm_g, v_final_norm_g, "adamw_final_norm_g")

    ordered = [u_mix, u_win, u_cwa, u_lng, u_lnb, u_ws, u_bs, u_wout, u_ffn, u_wup, u_cwf, u_wd, u_final]
    return (loss, grad_x, *[u[0] for u in ordered], *[u[1] for u in ordered],
            *[u[2] for u in ordered], *[u[3] for u in ordered])
```

```python
import functools

import jax
import jax.numpy as jnp
from jax import lax
from jax.experimental import pallas as pl
from jax.experimental.pallas import tpu as pltpu

EPS = 1e-6
CHUNK = 128
N_GROUPS = 8
N_DEV = 8
HALO = 8
ADAM_LR = 0.001
ADAM_B1 = 0.9
ADAM_B2 = 0.999
ADAM_EPS = 1e-08
ADAM_WD = 0.01
ADAM_STEP = 10
VMEM_LIMIT_BYTES = 56 * 1024 * 1024
TOKENS_PER_STEP = dict(mixer=256, mixer_bwd_alone=512, ffn=256, wgrad=2048, dgrad=1024)
ROWS_PER_STEP = dict(adamw=256, own_slot=256, sum=512)
F32 = jnp.float32
BF16 = jnp.bfloat16
MESH = pl.DeviceIdType.MESH
ANY = pl.BlockSpec(memory_space=pl.ANY)
HBM_SPEC = pl.BlockSpec(memory_space=pltpu.HBM)
SEM_SPEC = pl.BlockSpec(memory_space=pltpu.SEMAPHORE)
DATAFLOW = pltpu.SideEffectType.DATAFLOW_SIDE_EFFECTING
NT_DIMS = (((1,), (1,)), ((), ()))
TN_DIMS = (((0,), (0,)), ((), ()))


def _params(n_grid_axes):
    return pltpu.CompilerParams(dimension_semantics=("arbitrary",) * n_grid_axes,
                                vmem_limit_bytes=VMEM_LIMIT_BYTES)


def _shift_down(cur, prev8, k):
    rolled = pltpu.roll(cur, k, 0)
    prolled = pltpu.roll(prev8, k, 0)
    row = lax.broadcasted_iota(jnp.int32, prev8.shape, 0)
    head = jnp.where(row < k, prolled, rolled[:HALO])
    return jnp.concatenate([head, rolled[HALO:]], axis=0)


def _shift_up(cur, next8, k):
    tm = cur.shape[0]
    rolled = pltpu.roll(cur, tm - k, 0)
    nrolled = pltpu.roll(next8, HALO - k, 0)
    row = lax.broadcasted_iota(jnp.int32, next8.shape, 0)
    tail = jnp.where(row >= HALO - k, nrolled, rolled[tm - HALO:])
    return jnp.concatenate([rolled[:tm - HALO], tail], axis=0)


def _conv_fwd(cur, prev8, cw):
    s1 = _shift_down(cur, prev8, 1)
    s2 = _shift_down(cur, prev8, 2)
    y = s2 * cw[0:1, :] + s1 * cw[1:2, :] + cur * cw[2:3, :]
    return y, s1, s2


def _conv_bwd(d, next8, cw):
    u1 = _shift_up(d, next8, 1)
    u2 = _shift_up(d, next8, 2)
    return d * cw[2:3, :] + u1 * cw[1:2, :] + u2 * cw[0:1, :], u1, u2


def _colsum(a):
    return jnp.sum(a, axis=0, keepdims=True)


def _rms_stats(xv):
    r = lax.rsqrt(jnp.mean(xv * xv, axis=-1, keepdims=True) + EPS)
    return r, xv * r


def _rms_bwd(dh, xv, g):
    r, n = _rms_stats(xv)
    dn = dh * g
    dx = r * (dn - n * jnp.mean(dn * n, axis=-1, keepdims=True))
    return dx, _colsum(dh * n)


def _mixer_forward(p_ref, cprev, xiprev, cw, lng, lnb, ws_ref, bias_ref, mixed_scr, d):
    tm = p_ref.shape[0]
    b = p_ref[:, 0:d]
    c = p_ref[:, d:2 * d]
    xi = p_ref[:, 2 * d:3 * d]
    u = p_ref[:, 3 * d:4 * d]
    v = p_ref[:, 4 * d:5 * d]
    sa = jax.nn.sigmoid(p_ref[:, 5 * d:6 * d])
    sb = jax.nn.sigmoid(p_ref[:, 6 * d:7 * d])
    cx = c * xi
    conv, s1, s2 = _conv_fwd(cx, cprev * xiprev, cw)
    ya = b * conv
    mu = jnp.mean(v, axis=-1, keepdims=True)
    xc = v - mu
    rstd = lax.rsqrt(jnp.mean(xc * xc, axis=-1, keepdims=True) + EPS)
    vhat = xc * rstd
    vnb = (vhat * lng + lnb).astype(BF16)
    tril = (lax.broadcasted_iota(jnp.int32, (CHUNK, CHUNK), 0)
            >= lax.broadcasted_iota(jnp.int32, (CHUNK, CHUNK), 1))
    gd = d // N_GROUPS
    for g in range(N_GROUPS):
        wm = jnp.where(tril, ws_ref[g], 0.0).astype(BF16)
        cols = slice(g * gd, (g + 1) * gd)
        for n in range(tm // CHUNK):
            rows = slice(n * CHUNK, (n + 1) * CHUNK)
            mixed_scr[rows, cols] = (jnp.dot(wm, vnb[rows, cols], preferred_element_type=F32)
                                     + bias_ref[:, cols])
    mixed = mixed_scr[...]
    yb = u * mixed
    merged = sa * ya + sb * yb
    return dict(b=b, c=c, xi=xi, u=u, v=v, sa=sa, sb=sb, cx=cx, s1=s1, s2=s2, conv=conv, ya=ya,
                rstd=rstd, vhat=vhat, vnb=vnb, mixed=mixed, yb=yb, merged=merged, tril=tril)


KEPT = ("b", "c", "xi", "u", "v", "conv", "mixed", "vnb", "sa", "sb")


def _once(block_shape, index_map):
    return pl.BlockSpec(block_shape, index_map, pipeline_mode=pl.Buffered(1))


def mixer_fwd(x, g, win, wout, cw, lng, lnb, ws, bias, *, seq, name, tm):
    t, d = x.shape
    nj, _, n = win.shape
    tm = min(tm, seq)
    tiles_per_seq = seq // tm

    def body(x_ref, g_ref, win_ref, wout_ref, cw_ref, lng_ref, lnb_ref, ws_ref, bias_ref,
             h_ref, merged_ref, x1_ref, kept_ref, p_ref, mixed_scr, carry_ref):
        @pl.when(pl.program_id(0) == 0)
        def _():
            carry_ref[...] = jnp.zeros_like(carry_ref)

        keep = jnp.where(pl.program_id(0) % tiles_per_seq == 0, 0.0, 1.0)
        xv = x_ref[...]
        _, nrm = _rms_stats(xv)
        hb = (nrm * g_ref[...]).astype(BF16)
        h_ref[...] = hb
        for j in range(0, nj, 2):
            pair = jnp.concatenate([win_ref[j], win_ref[j + 1]], axis=1)
            p_ref[:, j * n:(j + 2) * n] = jnp.dot(hb, pair, preferred_element_type=F32)
        f = _mixer_forward(p_ref, carry_ref[...] * keep, 1.0, cw_ref[...], lng_ref[...],
                           lnb_ref[...], ws_ref, bias_ref, mixed_scr, d)
        carry_ref[...] = f["cx"][tm - HALO:]
        for k, key in enumerate(KEPT):
            kept_ref[:, k * d:(k + 1) * d] = f[key].astype(BF16)
        mb = f["merged"].astype(BF16)
        merged_ref[...] = mb
        x1_ref[...] = xv + jnp.dot(mb, wout_ref[...], preferred_element_type=F32)

    const2 = lambda i: (0, 0)
    const3 = lambda i: (0, 0, 0)
    row = lambda i: (i, 0)
    return pl.pallas_call(
        body, name=name, grid=(t // tm,),
        in_specs=[pl.BlockSpec((tm, d), row),
                  _once((1, d), const2),
                  _once((nj, d, n), const3),
                  _once((d, d), const2),
                  _once((HALO, d), const2),
                  _once((1, d), const2),
                  _once((1, d), const2),
                  _once((N_GROUPS, CHUNK, CHUNK), const3),
                  _once((CHUNK, d), const2)],
        out_specs=[pl.BlockSpec((tm, d), row), pl.BlockSpec((tm, d), row), pl.BlockSpec((tm, d), row),
                   pl.BlockSpec((tm, len(KEPT) * d), row)],
        out_shape=[jax.ShapeDtypeStruct((t, d), BF16), jax.ShapeDtypeStruct((t, d), BF16),
                   jax.ShapeDtypeStruct((t, d), F32), jax.ShapeDtypeStruct((t, len(KEPT) * d), BF16)],
        scratch_shapes=[pltpu.VMEM((tm, nj * n), F32), pltpu.VMEM((tm, d), F32), pltpu.VMEM((HALO, d), F32)],
        compiler_params=_params(1),
    )(x, g, win, wout, cw, lng, lnb, ws, bias)


def ffn_fwd(x1, g, wup, wd, cw, *, seq, name, tm, head=None):
    t, d = x1.shape
    nj, f, _ = wup.shape
    half = nj // 2
    tm = min(tm, seq)
    tiles_per_seq = seq // tm

    def body(x1_ref, g_ref, wup_ref, wd_ref, cw_ref, *rest):
        if head is None:
            h2_ref, up_ref, fac_ref, act_ref, x2_ref, carry_ref = rest
        else:
            gf_ref, tgt_ref, h2_ref, up_ref, fac_ref, act_ref, x2_ref, dgf_ref, loss_ref, carry_ref = rest

        @pl.when(pl.program_id(0) == 0)
        def _():
            carry_ref[...] = jnp.zeros_like(carry_ref)
            if head is not None:
                dgf_ref[...] = jnp.zeros_like(dgf_ref)
                loss_ref[...] = jnp.zeros_like(loss_ref)

        keep = jnp.where(pl.program_id(0) % tiles_per_seq == 0, 0.0, 1.0)
        xv = x1_ref[...]
        _, nrm = _rms_stats(xv)
        hb = (nrm * g_ref[...]).astype(BF16)
        h2_ref[...] = hb

        for j in range(nj):
            up_ref[j] = lax.dot_general(hb, wup_ref[j], NT_DIMS, preferred_element_type=F32)

        def conv_of(j):
            up0 = up_ref[j]
            y, _, _ = _conv_fwd(up0, carry_ref[j] * keep, cw_ref[j])
            carry_ref[j] = up0[tm - HALO:]
            return y

        acc = xv
        for k in range(half):
            gate, val = conv_of(k), conv_of(k + half)
            sg = jax.nn.sigmoid(gate)
            silu = gate * sg
            fac_ref[k] = (val * (sg * (1.0 + gate * (1.0 - sg)))).astype(BF16)
            fac_ref[k + half] = silu.astype(BF16)
            a = (silu * val).astype(BF16)
            act_ref[k] = a
            acc = acc + jnp.dot(a, wd_ref[k], preferred_element_type=F32)
        if head is None:
            x2_ref[...] = acc
        else:
            gv = gf_ref[...]
            r, n = _rms_stats(acc)
            err = n * gv - tgt_ref[...]
            loss_ref[...] += 0.5 * jnp.sum(jnp.mean(err * err, axis=-1, keepdims=True))
            dy = err * (1.0 / d)
            dn = dy * gv
            x2_ref[...] = r * (dn - n * jnp.mean(dn * n, axis=-1, keepdims=True))
            dgf_ref[0:1, :] += _colsum(dy * n)

    const3 = lambda i: (0, 0, 0)
    row = lambda i: (i, 0)
    in_specs = [pl.BlockSpec((tm, d), row), _once((1, d), lambda i: (0, 0)), _once((nj, f, d), const3),
                _once((half, f, d), const3), _once((nj, HALO, f), const3)]
    out_specs = [pl.BlockSpec((tm, d), row), pl.BlockSpec((nj, tm, f), lambda i: (0, i, 0)),
                 pl.BlockSpec((nj, tm, f), lambda i: (0, i, 0)), pl.BlockSpec((half, tm, f), lambda i: (0, i, 0)),
                 pl.BlockSpec((tm, d), row)]
    out_shape = [jax.ShapeDtypeStruct((t, d), BF16), jax.ShapeDtypeStruct((nj, t, f), F32),
                 jax.ShapeDtypeStruct((nj, t, f), BF16), jax.ShapeDtypeStruct((half, t, f), BF16),
                 jax.ShapeDtypeStruct((t, d), F32)]
    args = [x1, g, wup, wd, cw]
    if head is not None:
        in_specs += [_once((1, d), lambda i: (0, 0)), pl.BlockSpec((tm, d), row)]
        out_specs += [pl.BlockSpec((HALO, d), lambda i: (0, 0)), pl.BlockSpec((8, 128), lambda i: (0, 0))]
        out_shape += [jax.ShapeDtypeStruct((HALO, d), F32), jax.ShapeDtypeStruct((8, 128), F32)]
        args += list(head)
    return pl.pallas_call(
        body, name=name, grid=(t // tm,), in_specs=in_specs, out_specs=out_specs, out_shape=out_shape,
        scratch_shapes=[pltpu.VMEM((nj, HALO, f), F32)],
        compiler_params=_params(1),
    )(*args)


def ffn_bwd(dx2, up0, fac, wd, cw, wup, x1, g, *, seq, name, tm):
    t, d = dx2.shape
    nj, _, f = up0.shape
    half = nj // 2
    tm = min(tm, seq)
    tiles_per_seq = seq // tm
    nt = t // tm

    def body(dx_ref, up_ref, fac_ref, wd_ref, cw_ref, wup_ref, x1_ref, g_ref,
             dup_ref, dcw_ref, dx1_ref, dg_ref, carry_ref):
        i = pl.program_id(0)
        tile = nt - 1 - i

        @pl.when(i == 0)
        def _():
            dcw_ref[...] = jnp.zeros_like(dcw_ref)
            dg_ref[...] = jnp.zeros_like(dg_ref)
            carry_ref[...] = jnp.zeros_like(carry_ref)

        keep_next = jnp.where(tile % tiles_per_seq == tiles_per_seq - 1, 0.0, 1.0)
        dx2v = dx_ref[...]
        dxb = dx2v.astype(BF16)
        dh = [jnp.zeros((tm, d), F32)]

        def through_conv(j, dup):
            next8 = carry_ref[j] * keep_next
            carry_ref[j] = dup[:HALO]
            dup0, u1, u2 = _conv_bwd(dup, next8, cw_ref[j])
            up0 = up_ref[j]
            dcw_ref[j, 0:1, :] += _colsum(u2 * up0)
            dcw_ref[j, 1:2, :] += _colsum(u1 * up0)
            dcw_ref[j, 2:3, :] += _colsum(dup * up0)
            dup0 = dup0.astype(BF16)
            dup_ref[j] = dup0
            dh[0] = dh[0] + jnp.dot(dup0, wup_ref[j], preferred_element_type=F32)

        dacts = [lax.dot_general(dxb, wd_ref[k], NT_DIMS, preferred_element_type=F32) for k in range(half)]
        for k in range(half):
            through_conv(k, dacts[k] * fac_ref[k].astype(F32))
            through_conv(k + half, dacts[k] * fac_ref[k + half].astype(F32))

        dx, dg = _rms_bwd(dh[0], x1_ref[...], g_ref[...])
        dx1_ref[...] = dx2v + dx
        dg_ref[0:1, :] += dg

    rev = lambda i: nt - 1 - i
    return pl.pallas_call(
        body, name=name, grid=(nt,),
        in_specs=[pl.BlockSpec((tm, d), lambda i: (rev(i), 0)),
                  pl.BlockSpec((nj, tm, f), lambda i: (0, rev(i), 0)),
                  pl.BlockSpec((nj, tm, f), lambda i: (0, rev(i), 0)),
                  _once((half, f, d), lambda i: (0, 0, 0)),
                  _once((nj, HALO, f), lambda i: (0, 0, 0)),
                  _once((nj, f, d), lambda i: (0, 0, 0)),
                  pl.BlockSpec((tm, d), lambda i: (rev(i), 0)),
                  _once((1, d), lambda i: (0, 0))],
        out_specs=[pl.BlockSpec((nj, tm, f), lambda i: (0, rev(i), 0)),
                   pl.BlockSpec((nj, HALO, f), lambda i: (0, 0, 0)),
                   pl.BlockSpec((tm, d), lambda i: (rev(i), 0)),
                   pl.BlockSpec((HALO, d), lambda i: (0, 0))],
        out_shape=[jax.ShapeDtypeStruct((nj, t, f), BF16), jax.ShapeDtypeStruct((nj, HALO, f), F32),
                   jax.ShapeDtypeStruct((t, d), F32), jax.ShapeDtypeStruct((HALO, d), F32)],
        scratch_shapes=[pltpu.VMEM((nj, HALO, f), F32)],
        compiler_params=_params(1),
    )(dx2, up0, fac, wd, cw, wup, x1, g)


def mixer_bwd(dx1, kept, wout, cw, lng, wst, *, seq, name, tm, on_to_x0=None):
    t, d = dx1.shape
    tm = min(tm, seq)
    tiles_per_seq = seq // tm
    nt = t // tm
    gd = d // N_GROUPS
    if on_to_x0 is not None:
        nj, _, wn = on_to_x0[0].shape

    def body(dx_ref, k_ref, wout_ref, cw_ref, lng_ref, wst_ref, *rest):
        if on_to_x0 is None:
            dp_ref, dcw_ref, dln_ref, dws_ref, dbs_ref, dvn_scr, carry_ref, dbs_acc = rest
        else:
            (win_ref, x0_ref, g_ref, dp_ref, dcw_ref, dln_ref, dws_ref, dbs_ref, dx0_ref, dg_ref,
             dvn_scr, carry_ref, dbs_acc) = rest
        i = pl.program_id(0)
        tile = nt - 1 - i

        @pl.when(i == 0)
        def _():
            dcw_ref[...] = jnp.zeros_like(dcw_ref)
            dln_ref[...] = jnp.zeros_like(dln_ref)
            dws_ref[...] = jnp.zeros_like(dws_ref)
            dbs_acc[...] = jnp.zeros_like(dbs_acc)
            if on_to_x0 is not None:
                dg_ref[...] = jnp.zeros_like(dg_ref)
            carry_ref[...] = jnp.zeros_like(carry_ref)

        keep_next = jnp.where(tile % tiles_per_seq == tiles_per_seq - 1, 0.0, 1.0)
        cw = cw_ref[...]
        lng = lng_ref[...]
        kept_f32 = {key: k_ref[:, k * d:(k + 1) * d].astype(F32) for k, key in enumerate(KEPT) if key != "vnb"}
        b, c, xi, u, v, conv, mixed, sa, sb = (kept_f32[key] for key in KEPT if key != "vnb")
        vnb = k_ref[:, KEPT.index("vnb") * d:(KEPT.index("vnb") + 1) * d]
        xc = v - jnp.mean(v, axis=-1, keepdims=True)
        rstd = lax.rsqrt(jnp.mean(xc * xc, axis=-1, keepdims=True) + EPS)
        vhat = xc * rstd
        dmerged = lax.dot_general(dx_ref[...].astype(BF16), wout_ref[...], NT_DIMS, preferred_element_type=F32)
        dp_ref[:, 5 * d:6 * d] = (dmerged * (b * conv) * (sa * (1.0 - sa))).astype(BF16)
        dp_ref[:, 6 * d:7 * d] = (dmerged * (u * mixed) * (sb * (1.0 - sb))).astype(BF16)
        dya = dmerged * sa
        dyb = dmerged * sb
        dp_ref[:, 0:d] = (dya * conv).astype(BF16)
        dconv = dya * b
        next8 = carry_ref[...] * keep_next
        carry_ref[...] = dconv[:HALO]
        dcx, u1, u2 = _conv_bwd(dconv, next8, cw)
        cx = c * xi
        dcw_ref[0:1, :] += _colsum(u2 * cx)
        dcw_ref[1:2, :] += _colsum(u1 * cx)
        dcw_ref[2:3, :] += _colsum(dconv * cx)
        dp_ref[:, d:2 * d] = (dcx * xi).astype(BF16)
        dp_ref[:, 2 * d:3 * d] = (dcx * c).astype(BF16)
        dp_ref[:, 3 * d:4 * d] = (dyb * mixed).astype(BF16)
        dmixed = dyb * u
        dmb = dmixed.astype(BF16)
        tril = (lax.broadcasted_iota(jnp.int32, (CHUNK, CHUNK), 0)
                >= lax.broadcasted_iota(jnp.int32, (CHUNK, CHUNK), 1))
        triu = (lax.broadcasted_iota(jnp.int32, (CHUNK, CHUNK), 0)
                <= lax.broadcasted_iota(jnp.int32, (CHUNK, CHUNK), 1))
        dbs_tile = dmixed[0:CHUNK]
        for n in range(1, tm // CHUNK):
            dbs_tile = dbs_tile + dmixed[n * CHUNK:(n + 1) * CHUNK]
        dbs_acc[...] += dbs_tile
        for g in range(N_GROUPS):
            wmt = jnp.where(triu, wst_ref[g], 0.0).astype(BF16)
            cols = slice(g * gd, (g + 1) * gd)
            dw = jnp.zeros((CHUNK, CHUNK), F32)
            for n in range(tm // CHUNK):
                rows = slice(n * CHUNK, (n + 1) * CHUNK)
                dvn_scr[rows, cols] = jnp.dot(wmt, dmb[rows, cols], preferred_element_type=F32)
                dw = dw + lax.dot_general(dmb[rows, cols], vnb[rows, cols], NT_DIMS, preferred_element_type=F32)
            dws_ref[g] += jnp.where(tril, dw, 0.0)
        dvn = dvn_scr[...]
        dln_ref[0:1, :] += _colsum(dvn * vhat)
        dln_ref[1:2, :] += _colsum(dvn)
        dvh = dvn * lng
        dv = rstd * (dvh - jnp.mean(dvh, axis=-1, keepdims=True)
                     - vhat * jnp.mean(dvh * vhat, axis=-1, keepdims=True))
        dp_ref[:, 4 * d:5 * d] = dv.astype(BF16)
        if on_to_x0 is not None:
            dh = jnp.zeros((tm, d), F32)
            for j in range(0, nj, 2):
                pair = jnp.concatenate([win_ref[j], win_ref[j + 1]], axis=1)
                dh = dh + lax.dot_general(dp_ref[:, j * wn:(j + 2) * wn], pair, NT_DIMS, preferred_element_type=F32)
            dx, dg = _rms_bwd(dh, x0_ref[...], g_ref[...])
            dx0_ref[...] = dx_ref[...] + dx
            dg_ref[0:1, :] += dg

        @pl.when(i == nt - 1)
        def _():
            for g in range(N_GROUPS):
                cols = slice(g * gd, (g + 1) * gd)
                s = jnp.sum(dbs_acc[:, cols], axis=1, keepdims=True)
                dbs_ref[:, cols] = jnp.broadcast_to(s, (CHUNK, gd))

    rev = lambda i: nt - 1 - i
    const2 = lambda i: (0, 0)
    const3 = lambda i: (0, 0, 0)
    row = lambda i: (rev(i), 0)
    in_specs = [pl.BlockSpec((tm, d), row), pl.BlockSpec((tm, len(KEPT) * d), row),
                _once((d, d), const2), _once((HALO, d), const2), _once((1, d), const2),
                _once((N_GROUPS, CHUNK, CHUNK), const3)]
    out_specs = [pl.BlockSpec((tm, 7 * d), row), pl.BlockSpec((HALO, d), const2), pl.BlockSpec((HALO, d), const2),
                 pl.BlockSpec((N_GROUPS, CHUNK, CHUNK), const3), pl.BlockSpec((CHUNK, d), const2)]
    out_shape = [jax.ShapeDtypeStruct((t, 7 * d), BF16), jax.ShapeDtypeStruct((HALO, d), F32),
                 jax.ShapeDtypeStruct((HALO, d), F32), jax.ShapeDtypeStruct((N_GROUPS, CHUNK, CHUNK), F32),
                 jax.ShapeDtypeStruct((CHUNK, d), F32)]
    args = [dx1, kept, wout, cw, lng, wst]
    if on_to_x0 is not None:
        in_specs += [_once((nj, d, wn), const3), pl.BlockSpec((tm, d), row), _once((1, d), const2)]
        out_specs += [pl.BlockSpec((tm, d), row), pl.BlockSpec((HALO, d), const2)]
        out_shape += [jax.ShapeDtypeStruct((t, d), F32), jax.ShapeDtypeStruct((HALO, d), F32)]
        args += list(on_to_x0)
    return pl.pallas_call(
        body, name=name, grid=(nt,), in_specs=in_specs, out_specs=out_specs, out_shape=out_shape,
        scratch_shapes=[pltpu.VMEM((tm, d), F32), pltpu.VMEM((HALO, d), F32), pltpu.VMEM((CHUNK, d), F32)],
        compiler_params=_params(1),
    )(*args)


def dgrad_rms(dy, w, x, g, res, *, name, tm):
    t, d = x.shape
    n = w.shape[2]
    w = w.reshape(w.shape[0] // 2, 2, d, n)
    nj = w.shape[0]
    tm = min(tm, t)

    def body(dy_ref, w_ref, x_ref, g_ref, res_ref, dx_ref, dg_ref, acc_ref):
        i, j = pl.program_id(0), pl.program_id(1)

        @pl.when((i == 0) & (j == 0))
        def _():
            dg_ref[...] = jnp.zeros_like(dg_ref)

        pair = jnp.concatenate([w_ref[0], w_ref[1]], axis=1)
        part = lax.dot_general(dy_ref[...], pair, NT_DIMS, preferred_element_type=F32)

        @pl.when(j == 0)
        def _():
            acc_ref[...] = part

        @pl.when(j > 0)
        def _():
            acc_ref[...] += part

        @pl.when(j == nj - 1)
        def _():
            dx, dg = _rms_bwd(acc_ref[...], x_ref[...], g_ref[...])
            dx_ref[...] = res_ref[...] + dx
            dg_ref[0:1, :] += dg

    return pl.pallas_call(
        body, name=name, grid=(t // tm, nj),
        in_specs=[pl.BlockSpec((tm, 2 * n), lambda i, j: (i, j)),
                  pl.BlockSpec((None, 2, d, n), lambda i, j: (j, 0, 0, 0)),
                  pl.BlockSpec((tm, d), lambda i, j: (i, 0)),
                  pl.BlockSpec((1, d), lambda i, j: (0, 0)),
                  pl.BlockSpec((tm, d), lambda i, j: (i, 0))],
        out_specs=[pl.BlockSpec((tm, d), lambda i, j: (i, 0)), pl.BlockSpec((HALO, d), lambda i, j: (0, 0))],
        out_shape=[jax.ShapeDtypeStruct((t, d), F32), jax.ShapeDtypeStruct((HALO, d), F32)],
        scratch_shapes=[pltpu.VMEM((tm, d), F32)],
        compiler_params=_params(2),
    )(dy, w, x, g, res)


def wgrad(a, b, *, nj, a_mode, b_mode, name, tm, split=1):
    def describe(arr, mode):
        if mode == "full":
            return arr.shape[0], arr.shape[1], pl.BlockSpec((tm_, arr.shape[1]), lambda j, s: (s, 0))
        if mode == "cols":
            c = arr.shape[1] // nj
            return arr.shape[0], c, pl.BlockSpec((tm_, c), lambda j, s: (s, j))
        return arr.shape[1], arr.shape[2], pl.BlockSpec((None, tm_, arr.shape[2]), lambda j, s: (j, s, 0))

    t = a.shape[0] if a_mode != "lead" else a.shape[1]
    tm_ = min(tm, t)
    _, k, a_spec = describe(a, a_mode)
    _, n, b_spec = describe(b, b_mode)

    ns = t // tm_
    nc = n // split

    def body(a_ref, b_ref, o_ref, acc_ref):
        s = pl.program_id(1)
        part = lax.dot_general(a_ref[...], b_ref[...], TN_DIMS, preferred_element_type=F32)

        def finish(total):
            for q in range(split):
                o_ref[q] = total[:, q * nc:(q + 1) * nc].astype(BF16)

        if ns == 1:
            finish(part)
            return

        @pl.when(s == 0)
        def _():
            acc_ref[...] = part

        @pl.when((s > 0) & (s < ns - 1))
        def _():
            acc_ref[...] += part

        @pl.when(s == ns - 1)
        def _():
            finish(acc_ref[...] + part)

    return pl.pallas_call(
        body, name=name, grid=(nj, ns),
        in_specs=[a_spec, b_spec],
        out_specs=pl.BlockSpec((split, k, nc), lambda j, s: (j, 0, 0)),
        out_shape=jax.ShapeDtypeStruct((nj * split, k, nc), BF16),
        scratch_shapes=[pltpu.VMEM((k, n), F32)],
        compiler_params=_params(2),
    )(a, b)


def _adamw_math(w, g, m, v):
    m = ADAM_B1 * m + (1.0 - ADAM_B1) * g
    v = ADAM_B2 * v + (1.0 - ADAM_B2) * (g * g)
    m_hat = m / (1.0 - ADAM_B1 ** ADAM_STEP)
    v_hat = v / (1.0 - ADAM_B2 ** ADAM_STEP)
    delta = -ADAM_LR * (m_hat / (jnp.sqrt(v_hat) + ADAM_EPS) + ADAM_WD * w)
    return delta, m, v


def _row_tile(rows, at_most):
    if rows <= at_most:
        return rows
    return max(k for k in range(16, at_most + 1, 16) if rows % k == 0)


def _sum_in_device_order(ref):
    total = ref[0]
    for s in range(1, N_DEV):
        total = total + ref[s]
    return total


def adamw_sharded(me, own0, recv0, own1, recv1, w, m, v, *, name, tr):
    _, r, c = w.shape
    tr = _row_tile(r, tr)
    ni = r // tr

    def body(me_ref, o0_ref, r0_ref, o1_ref, r1_ref, w_ref, m_ref, v_ref, g_ref, d_ref, nm_ref, nv_ref):
        def finish(own_ref, recv_ref):
            g = None
            for s in range(N_DEV):
                term = jnp.where(me_ref[0] == s, own_ref[...], recv_ref[s]).astype(F32)
                g = term if g is None else g + term
            delta, nm, nv = _adamw_math(w_ref[...], g, m_ref[...], v_ref[...])
            g_ref[...] = g
            d_ref[...] = delta
            nm_ref[...] = nm
            nv_ref[...] = nv

        @pl.when(pl.program_id(0) == 0)
        def _():
            finish(o0_ref, r0_ref)

        @pl.when(pl.program_id(0) == 1)
        def _():
            finish(o1_ref, r1_ref)

    row0 = lambda l, i: i * (1 - l) + (ni - 1) * l
    row1 = lambda l, i: i * l
    lay = pl.BlockSpec((None, tr, c), lambda l, i, me_ref: (l, i, 0))
    grid_spec = pltpu.PrefetchScalarGridSpec(
        num_scalar_prefetch=1, grid=(2, ni),
        in_specs=[pl.BlockSpec((None, tr, c), lambda l, i, me_ref: (me_ref[0], row0(l, i), 0)),
                  pl.BlockSpec((N_DEV, tr, c), lambda l, i, me_ref: (0, row0(l, i), 0)),
                  pl.BlockSpec((None, tr, c), lambda l, i, me_ref: (me_ref[0], row1(l, i), 0)),
                  pl.BlockSpec((N_DEV, tr, c), lambda l, i, me_ref: (0, row1(l, i), 0)),
                  lay, lay, lay],
        out_specs=[lay, lay, lay, lay])
    return pl.pallas_call(
        body, name=name, grid_spec=grid_spec,
        out_shape=[jax.ShapeDtypeStruct(w.shape, F32)] * 4,
        compiler_params=_params(2),
    )(me, own0, recv0, own1, recv1, w, m, v)


def sum_chunks(me, own, recv, *, name):
    _, r, c = own.shape

    def body(me_ref, o_ref, r_ref, out_ref):
        total = None
        for s in range(N_DEV):
            term = jnp.where(me_ref[0] == s, o_ref[...], r_ref[s]).astype(F32)
            total = term if total is None else total + term
        out_ref[...] = total

    grid_spec = pltpu.PrefetchScalarGridSpec(
        num_scalar_prefetch=1, grid=(1,),
        in_specs=[pl.BlockSpec((None, r, c), lambda i, me_ref: (me_ref[0], 0, 0)),
                  pl.BlockSpec((N_DEV, r, c), lambda i, me_ref: (0, 0, 0))],
        out_specs=pl.BlockSpec((r, c), lambda i, me_ref: (0, 0)))
    return pl.pallas_call(
        body, name=name, grid_spec=grid_spec,
        out_shape=jax.ShapeDtypeStruct((r, c), F32),
        compiler_params=_params(1),
    )(me, own, recv)


def adamw_small(g, w, m, v, *, name):
    def body(g_ref, w_ref, m_ref, v_ref, d_ref, nm_ref, nv_ref):
        delta, nm, nv = _adamw_math(w_ref[...], g_ref[...], m_ref[...], v_ref[...])
        d_ref[...] = delta
        nm_ref[...] = nm
        nv_ref[...] = nv

    return pl.pallas_call(
        body, name=name,
        out_shape=[jax.ShapeDtypeStruct(w.shape, F32)] * 3,
        compiler_params=pltpu.CompilerParams(vmem_limit_bytes=VMEM_LIMIT_BYTES),
    )(g, w, m, v)


def sum_devices(parts, *, name, tr):
    _, r, c = parts.shape
    tr = min(tr, r)

    def body(p_ref, o_ref):
        o_ref[...] = _sum_in_device_order(p_ref)

    return pl.pallas_call(
        body, name=name, grid=(r // tr,),
        in_specs=[pl.BlockSpec((N_DEV, tr, c), lambda i: (0, i, 0))],
        out_specs=pl.BlockSpec((tr, c), lambda i: (i, 0)),
        out_shape=jax.ShapeDtypeStruct((r, c), F32),
        compiler_params=_params(1),
    )(parts)


def _my_place():
    return lax.axis_index("x"), lax.axis_index("y"), lax.axis_index("c")


def all_gather(arrays, after, *, name):
    n = len(arrays)

    def body(*refs):
        ins, outs = refs[:n], refs[n + 1:2 * n + 1]
        send_sems, recv_sems, local_sems = refs[2 * n + 1:]
        x, y, c = _my_place()
        me, sibling = (x, y, c), (x, y, 1 - c)
        chips = [(1 - x, y), (x, 1 - y), (1 - x, 1 - y)]
        waits = []
        for a in range(n):
            def slot(place, a=a):
                px, py, pc = place
                return outs[a].at[4 * px + 2 * py + pc]

            def copy(k, block, to, src=None, a=a, slot=slot):
                return pltpu.make_async_remote_copy(
                    src_ref=slot(block) if src is None else src, dst_ref=slot(block),
                    send_sem=send_sems.at[a, k], recv_sem=recv_sems.at[a, k],
                    device_id=to, device_id_type=MESH)

            mine = pltpu.make_async_copy(ins[a], slot(me), local_sems.at[a])
            mine.start()
            first = [copy(0, me, sibling, src=ins[a])]
            first += [copy(1 + j, me, (*chip, c), src=ins[a]) for j, chip in enumerate(chips)]
            for cp in first:
                cp.start()
            waits.append((copy, mine, first))
        sends = []
        for a in range(n):
            copy, mine, first = waits[a]
            passed = [copy(4 + j, (*chip, c), sibling) for j, chip in enumerate(chips)]
            for j, chip in enumerate(chips):
                copy(1 + j, (*chip, c), me).wait_recv()
                passed[j].start()
            sends.append(first + passed)
        for a in range(n):
            copy, mine, first = waits[a]
            copy(0, sibling, me).wait_recv()
            for j, chip in enumerate(chips):
                copy(4 + j, (*chip, 1 - c), me).wait_recv()
            for cp in sends[a]:
                cp.wait_send()
            mine.wait()

    return pl.pallas_call(
        body, name=name,
        in_specs=[ANY] * (n + 1), out_specs=[ANY] * n,
        out_shape=[jax.ShapeDtypeStruct((N_DEV,) + a.shape, a.dtype) for a in arrays],
        scratch_shapes=[pltpu.SemaphoreType.DMA((n, 7)), pltpu.SemaphoreType.DMA((n, 7)),
                        pltpu.SemaphoreType.DMA((n,))],
        compiler_params=pltpu.CompilerParams(has_side_effects=True),
    )(*arrays, after)


def _peer_place(r, x, y, c):
    fx, fy, fc = (r >> 2) & 1, (r >> 1) & 1, r & 1
    return (1 - x if fx else x, 1 - y if fy else y, 1 - c if fc else c)


def own_slot(me, w, layer, dtype, *, name, tr):
    _, r, c = w.shape
    tr = _row_tile(r, tr)

    def body(me_ref, w_ref, o_ref):
        o_ref[...] = w_ref[...].astype(dtype)

    grid_spec = pltpu.PrefetchScalarGridSpec(
        num_scalar_prefetch=1, grid=(r // tr,),
        in_specs=[pl.BlockSpec((None, tr, c), lambda i, me_ref: (layer, i, 0))],
        out_specs=pl.BlockSpec((None, tr, c), lambda i, me_ref: (me_ref[0], i, 0)))
    return pl.pallas_call(
        body, name=name, grid_spec=grid_spec,
        out_shape=jax.ShapeDtypeStruct((N_DEV, r, c), dtype),
        compiler_params=_params(1),
    )(me, w)


EXCHANGES = {
    "scatter": [(0, r) for r in range(1, N_DEV)],
    "gather": [(0, r) for r in range(1, N_DEV)],
    "gather_chips": [(0, r) for r in (1, 2, 4, 6)],
    "gather_forward": [(q, 1) for q in (2, 4, 6)],
}


def _split_copy(k, entry, src, land, send_sem, recv_sem, arriving):
    slot, peer = entry
    x, y, c = _my_place()

    def index(relation):
        px, py, pc = _peer_place(relation, x, y, c)
        return 4 * px + 2 * py + pc

    return pltpu.make_async_remote_copy(
        src_ref=land.at[index(slot)] if src is None else src.at[index(peer)],
        dst_ref=land.at[index(slot ^ peer if arriving else slot)],
        send_sem=send_sem.at[k], recv_sem=recv_sem.at[k],
        device_id=_peer_place(peer, x, y, c), device_id_type=MESH)


def start_copies(srcs, lands, *, mode, name, after=None):
    n = len(lands)
    entries = EXCHANGES[mode]
    bufs = (list(srcs) if srcs is not None else []) + list(lands)
    nb = len(bufs)

    def body(*refs):
        src = refs[:n] if srcs is not None else [None] * n
        land = refs[nb - n:nb]
        outs = refs[nb + len(extra):]
        send_sems, recv_sems = outs[:n], outs[n:2 * n]
        token = outs[2 * n + nb]
        for a in range(n):
            for k, entry in enumerate(entries):
                _split_copy(k, entry, src[a], land[a], send_sems[a], recv_sems[a], False).start()
        token[...] = jnp.zeros_like(token)

    extra = [] if after is None else [after]
    outs = pl.pallas_call(
        body, name=name,
        in_specs=[HBM_SPEC] * nb + [ANY] * len(extra),
        out_specs=[SEM_SPEC] * (2 * n) + [HBM_SPEC] * nb + [pl.BlockSpec(memory_space=pltpu.VMEM)],
        out_shape=([pltpu.SemaphoreType.DMA((len(entries),))] * (2 * n)
                   + [pltpu.HBM(a.shape, a.dtype) for a in bufs]
                   + [jax.ShapeDtypeStruct((8, 128), F32)]),
        input_output_aliases={i: 2 * n + i for i in range(nb)},
        compiler_params=pltpu.CompilerParams(has_side_effects=DATAFLOW),
    )(*[pltpu.with_memory_space_constraint(a, pltpu.HBM) for a in bufs], *extra)
    thru = list(outs[2 * n:2 * n + nb])
    return dict(send=outs[:n], recv=outs[n:2 * n], src=thru[:n] if srcs is not None else None, land=thru[nb - n:],
                token=outs[2 * n + nb], mode=mode)


def finish_copies(started, which, after, *, name):
    n = len(which)
    entries = EXCHANGES[started["mode"]]
    has_src = started["src"] is not None
    bufs = ([started["src"][i] for i in which] if has_src else []) + [started["land"][i] for i in which]
    nb = len(bufs)

    def body(*refs):
        src = refs[:n] if has_src else [None] * n
        land = refs[nb - n:nb]
        send_sems, recv_sems = refs[nb:nb + n], refs[nb + n:nb + 2 * n]
        for a in range(n):
            for k, entry in enumerate(entries):
                cp = _split_copy(k, entry, src[a], land[a], send_sems[a], recv_sems[a], True)
                cp.wait_send()
                cp.wait_recv()

    outs = pl.pallas_call(
        body, name=name,
        in_specs=[HBM_SPEC] * nb + [SEM_SPEC] * (2 * n) + [ANY],
        out_specs=[HBM_SPEC] * nb,
        out_shape=[pltpu.HBM(a.shape, a.dtype) for a in bufs],
        input_output_aliases={i: i for i in range(nb)},
        compiler_params=pltpu.CompilerParams(has_side_effects=DATAFLOW),
    )(*bufs, *[started["send"][i] for i in which], *[started["recv"][i] for i in which], after)
    return (list(outs[:n]) if has_src else None), list(outs[nb - n:])


def _pad_rows(a, rows):
    pad = [(0, 0)] * a.ndim
    pad[-2] = (0, rows - a.shape[-2])
    return jnp.pad(a, pad)


def kernel(x, mix_norm_g, w_in, conv_a_w, ln_v_g, ln_v_b, w_s, b_s, w_out, ffn_norm_g, w_up, conv_ffn_w, w_down, final_norm_g, loss_target, m_mix_norm_g, m_w_in, m_conv_a_w, m_ln_v_g, m_ln_v_b, m_w_s, m_b_s, m_w_out, m_ffn_norm_g, m_w_up, m_conv_ffn_w, m_w_down, m_final_norm_g, v_mix_norm_g, v_w_in, v_conv_a_w, v_ln_v_g, v_ln_v_b, v_w_s, v_b_s, v_w_out, v_ffn_norm_g, v_w_up, v_conv_ffn_w, v_w_down, v_final_norm_g):
    nb, seq, d = x.shape
    t = nb * seq
    depth = w_in.shape[0]
    f = w_up.shape[2]
    me = 4 * lax.axis_index("x") + 2 * lax.axis_index("y") + lax.axis_index("c")
    xt = x.reshape(t, d)
    tgt = loss_target.reshape(t, d)

    conv_pack = jnp.concatenate([_pad_rows(conv_a_w, HALO), _pad_rows(conv_ffn_w, HALO)], axis=-1)
    me_arr = me.astype(jnp.int32).reshape(1)
    w_up_t, m_w_up_t, v_w_up_t = (jnp.swapaxes(a, 1, 2) for a in (w_up, m_w_up, v_w_up))
    zones, slot_of = [], {}
    for l in range(depth):
        for key, w in (("win", w_in), ("conv", None), ("wout", w_out), ("wup", w_up_t), ("wd", w_down)):
            if key == "conv":
                if l == 0:
                    slot_of["conv"] = len(zones)
                    packed = conv_pack.reshape(1, depth * HALO, conv_pack.shape[-1])
                    zones.append(own_slot(me_arr, packed, 0, F32, name="own_slot_conv", tr=ROWS_PER_STEP["own_slot"]))
                continue
            slot_of[key, l] = len(zones)
            zones.append(own_slot(me_arr, w, l, BF16, name=f"own_slot_{key}_{l}", tr=ROWS_PER_STEP["own_slot"]))
    first = [slot_of["win", 0], slot_of["wout", 0], slot_of["conv"]]
    rest = [i for i in range(len(zones)) if i not in first]
    to_chips = start_copies(None, [zones[i] for i in first], mode="gather_chips", name="gather_first_chips")
    gathering = start_copies(None, [zones[i] for i in rest], mode="gather", name="gather_start", after=to_chips["token"])
    _, at_chips = finish_copies(to_chips, [0, 1, 2], gathering["token"], name="wait_first_chips")
    to_sibling = start_copies(None, at_chips, mode="gather_forward", name="gather_first_forward")

    def gathered(keys, after, name):
        return finish_copies(gathering, [rest.index(slot_of[k]) for k in keys], after, name=name)[1]

    saved, layers = [], []
    cur = xt
    for l in range(depth):
        p = dict(mix_g=mix_norm_g[l][None], ffn_g=ffn_norm_g[l][None], lng=ln_v_g[l][None], lnb=ln_v_b[l][None],
                 ws=w_s[l], wst=jnp.swapaxes(w_s[l], 1, 2),
                 bias=jnp.repeat(b_s[l].T, d // N_GROUPS, axis=1))
        if l == 0:
            _, (p["win"], wout_g, conv_g) = finish_copies(to_sibling, [0, 1, 2], to_sibling["token"],
                                                          name=f"wait_w_mixer_{l}")
            conv_g = conv_g.reshape(N_DEV, depth, HALO, -1)
            ca = conv_g.shape[-1] - f
        else:
            p["win"], wout_g = gathered([("win", l), ("wout", l)], after, f"wait_w_mixer_{l}")
        p["wout"] = wout_g.reshape(d, d)
        p["cw_a"] = jnp.transpose(conv_g[:, l, :, :ca], (1, 0, 2)).reshape(HALO, d)
        p["cw_f"] = conv_g[:, l, :, ca:]
        h, merged, x1, kept = mixer_fwd(cur, p["mix_g"], p["win"], p["wout"], p["cw_a"], p["lng"], p["lnb"],
                                        p["ws"], p["bias"], seq=seq, name=f"mixer_fwd_{l}", tm=TOKENS_PER_STEP["mixer"])
        p["wup"], wd_g = gathered([("wup", l), ("wd", l)], merged, f"wait_w_ffn_{l}")
        p["wd"] = wd_g.reshape(N_DEV // 2, 2 * wd_g.shape[1], d)
        head = (final_norm_g[None], tgt) if l == depth - 1 else None
        h2, up0, fac, act, x2, *of_loss = ffn_fwd(x1, p["ffn_g"], p["wup"], p["wd"], p["cw_f"],
                                                  seq=seq, name=f"ffn_fwd_{l}", tm=TOKENS_PER_STEP["ffn"], head=head)
        saved.append(dict(x0=cur, h=h, kept=kept, merged=merged, x1=x1, h2=h2, up0=up0, fac=fac, act=act))
        layers.append(p)
        cur, after = x2, act
    dx = cur
    d_final_g, loss_tile = of_loss

    def exchange(parts, name):
        return start_copies(parts, [lax.empty(a.shape, a.dtype) for a in parts], mode="scatter", name=name)

    def tied(g, started):
        return g + started["token"][0:1, 0:1]

    part = [None] * depth
    mix_ex = None
    for l in reversed(range(depth)):
        p, s = layers[l], saved[l]
        ffn_g = p["ffn_g"] if mix_ex is None else tied(p["ffn_g"], mix_ex)
        dup0, dcw_f, dx1, d_ffn_g = ffn_bwd(dx, s["up0"], s["fac"], p["wd"], p["cw_f"], p["wup"], s["x1"], ffn_g,
                                            seq=seq, name=f"ffn_bwd_{l}", tm=TOKENS_PER_STEP["ffn"])
        g_wd = wgrad(s["act"], dx, nj=N_DEV // 2, a_mode="lead", b_mode="full", name=f"wgrad_down_{l}", tm=TOKENS_PER_STEP["wgrad"])
        g_wup = wgrad(dup0, s["h2"], nj=N_DEV, a_mode="lead", b_mode="full", name=f"wgrad_up_{l}", tm=TOKENS_PER_STEP["wgrad"])
        ffn_ex = exchange([g_wd.reshape(N_DEV, g_wd.shape[1] // 2, d), g_wup], f"exchange_ffn_{l}")
        fused = l > 0
        dproj, dcw_a, dln, dws, dbs, *to_x0 = mixer_bwd(
            dx1, s["kept"], p["wout"], tied(p["cw_a"], ffn_ex), p["lng"], p["wst"],
            seq=seq, name=f"mixer_bwd_{l}", tm=TOKENS_PER_STEP["mixer" if fused else "mixer_bwd_alone"], on_to_x0=(p["win"], s["x0"], p["mix_g"]) if fused else None)
        g_wout = wgrad(s["merged"], dx1, nj=1, a_mode="full", b_mode="full", name=f"wgrad_out_{l}", tm=TOKENS_PER_STEP["wgrad"])
        g_win = wgrad(s["h"], dproj, nj=N_DEV // 2, a_mode="full", b_mode="cols", name=f"wgrad_in_{l}", tm=TOKENS_PER_STEP["wgrad"], split=2)
        cwa_chunks = jnp.transpose(dcw_a.reshape(HALO, N_DEV, d // N_DEV), (1, 0, 2))
        mix_ex = exchange([g_wout.reshape(N_DEV, d // N_DEV, d), g_win, dws, dcw_f, cwa_chunks], f"exchange_mix_{l}")
        if fused:
            dx, d_mix_g = to_x0
        else:
            dx, d_mix_g = dgrad_rms(dproj, p["win"], s["x0"], tied(p["mix_g"], mix_ex), dx1,
                                    name=f"dgrad_in_{l}", tm=TOKENS_PER_STEP["dgrad"])
        part[l] = dict(
            ffn_ex=ffn_ex, mix_ex=mix_ex,
            vectors=jnp.concatenate([d_mix_g[0:1], d_ffn_g[0:1], dln[0:2],
                                     dbs[:, ::d // N_GROUPS].T.reshape(1, d)], axis=0))
    grad_x = dx.reshape(nb, seq, d)

    own, recv = {}, {}

    def arrived(l, ex, keys, after):
        srcs, lands = finish_copies(part[l][ex], list(range(len(keys))), after, name=f"wait_{ex}_{l}")
        for k, key in enumerate(keys):
            own[key, l], recv[key, l] = srcs[k], lands[k]
        return lands[1]

    def big(key, w, m, v, name):
        return adamw_sharded(me_arr, own[key, 0], recv[key, 0], own[key, 1], recv[key, 1], w, m, v, name=name, tr=ROWS_PER_STEP["adamw"])

    mix_keys = ("wout", "win", "ws", "cwf", "cwa")
    after = grad_x
    for l in reversed(range(depth)):
        after = arrived(l, "ffn_ex", ("wd", "wup"), after)
        if l > 0:
            after = arrived(l, "mix_ex", mix_keys, after)
    u_wd = big("wd", w_down, m_w_down, v_w_down, "adamw_w_down")
    u_wup = tuple(jnp.swapaxes(a, 1, 2) for a in big("wup", w_up_t, m_w_up_t, v_w_up_t, "adamw_w_up"))
    arrived(0, "mix_ex", mix_keys, u_wup[1])
    u_wout = big("wout", w_out, m_w_out, v_w_out, "adamw_w_out")
    u_win = big("win", w_in, m_w_in, v_w_in, "adamw_w_in")

    def owned(key, l):
        return sum_chunks(me_arr, own[key, l], recv[key, l], name=f"sum_{key}_{l}")

    g_cwf = jnp.stack([owned("cwf", l)[:3] for l in range(depth)])
    g_cwa = jnp.stack([owned("cwa", l)[:3] for l in range(depth)])
    ws_rows = CHUNK * CHUNK // d
    ws_mine = jnp.concatenate([owned("ws", l).reshape(ws_rows, d) for l in range(depth)], axis=0)
    loss_row = jnp.zeros((1, d), F32).at[0, 0].set(loss_tile[0, 0])
    vectors = jnp.concatenate([part[l]["vectors"] for l in range(depth)] + [d_final_g[0:1], loss_row], axis=0)
    vectors = _pad_rows(vectors, -(-vectors.shape[0] // 8) * 8)
    vectors_all, ws_all = all_gather([vectors, ws_mine], u_win[0], name="gather_small_grads")
    vec_sum = sum_devices(vectors_all, name="sum_small", tr=ROWS_PER_STEP["sum"])
    g_ws = jnp.transpose(ws_all.reshape(N_DEV, depth, CHUNK, CHUNK), (1, 0, 2, 3))
    per_layer = part[0]["vectors"].shape[0]
    g_mix, g_ffn, g_lng, g_lnb = (jnp.stack([vec_sum[l * per_layer + k] for l in range(depth)]) for k in range(4))
    g_bs = jnp.stack([vec_sum[l * per_layer + 4].reshape(N_GROUPS, CHUNK) for l in range(depth)])
    g_final = vec_sum[depth * per_layer]
    loss = vec_sum[depth * per_layer + 1, 0]

    def small_update(g, w, m, v, name):
        shape = w.shape
        two_d = (-1, shape[-1]) if w.ndim > 1 else (1, shape[0])
        out = adamw_small(g.reshape(two_d), w.reshape(two_d), m.reshape(two_d), v.reshape(two_d), name=name)
        return (g.reshape(shape),) + tuple(o.reshape(shape) for o in out)

    u_mix = small_update(g_mix, mix_norm_g, m_mix_norm_g, v_mix_norm_g, "adamw_mix_norm_g")
    u_cwa = small_update(g_cwa, conv_a_w, m_conv_a_w, v_conv_a_w, "adamw_conv_a_w")
    u_lng = small_update(g_lng, ln_v_g, m_ln_v_g, v_ln_v_g, "adamw_ln_v_g")
    u_lnb = small_update(g_lnb, ln_v_b, m_ln_v_b, v_ln_v_b, "adamw_ln_v_b")
    u_ws = small_update(g_ws, w_s, m_w_s, v_w_s, "adamw_w_s")
    u_bs = small_update(g_bs, b_s, m_b_s, v_b_s, "adamw_b_s")
    u_ffn = small_update(g_ffn, ffn_norm_g, m_ffn_norm_g, v_ffn_norm_g, "adamw_ffn_norm_g")
    u_cwf = small_update(g_cwf, conv_ffn_w, m_conv_ffn_w, v_conv_ffn_w, "adamw_conv_ffn_w")
    u_final = small_update(g_final, final_norm_g, m_final_norm_g, v_final_norm_g, "adamw_final_norm_g")

    ordered = [u_mix, u_win, u_cwa, u_lng, u_lnb, u_ws, u_bs, u_wout, u_ffn, u_wup, u_cwf, u_wd, u_final]
    return (loss, grad_x, *[u[0] for u in ordered], *[u[1] for u in ordered],
            *[u[2] for u in ordered], *[u[3] for u in ordered])
```

```python
import jax
import jax.numpy as jnp
from jax import lax
from jax.experimental import pallas as pl
from jax.experimental.pallas import tpu as pltpu

EPS = 1e-6
CHUNK = 128
N_GROUPS = 8
N_DEV = 8
HALO = 8
ADAM_LR = 0.001
ADAM_B1 = 0.9
ADAM_B2 = 0.999
ADAM_EPS = 1e-08
ADAM_WD = 0.01
ADAM_STEP = 10
VMEM_LIMIT_BYTES = 56 * 1024 * 1024
TOKENS_PER_STEP = dict(mixer=256, mixer_bwd_alone=512, ffn=256, wgrad=2048, dgrad=1024)
ROWS_PER_STEP = dict(adamw=256, own_slot=256, sum=512)
F32 = jnp.float32
BF16 = jnp.bfloat16
MESH = pl.DeviceIdType.MESH
ANY = pl.BlockSpec(memory_space=pl.ANY)
HBM_SPEC = pl.BlockSpec(memory_space=pltpu.HBM)
SEM_SPEC = pl.BlockSpec(memory_space=pltpu.SEMAPHORE)
DATAFLOW = pltpu.SideEffectType.DATAFLOW_SIDE_EFFECTING
NT_DIMS = (((1,), (1,)), ((), ()))
TN_DIMS = (((0,), (0,)), ((), ()))


def _params(n_grid_axes):
    return pltpu.CompilerParams(dimension_semantics=("arbitrary",) * n_grid_axes,
                                vmem_limit_bytes=VMEM_LIMIT_BYTES)


def _shift_down(cur, prev8, k):
    rolled = pltpu.roll(cur, k, 0)
    prolled = pltpu.roll(prev8, k, 0)
    row = lax.broadcasted_iota(jnp.int32, prev8.shape, 0)
    head = jnp.where(row < k, prolled, rolled[:HALO])
    return jnp.concatenate([head, rolled[HALO:]], axis=0)


def _shift_up(cur, next8, k):
    tm = cur.shape[0]
    rolled = pltpu.roll(cur, tm - k, 0)
    nrolled = pltpu.roll(next8, HALO - k, 0)
    row = lax.broadcasted_iota(jnp.int32, next8.shape, 0)
    tail = jnp.where(row >= HALO - k, nrolled, rolled[tm - HALO:])
    return jnp.concatenate([rolled[:tm - HALO], tail], axis=0)


def _conv_fwd(cur, prev8, cw):
    s1 = _shift_down(cur, prev8, 1)
    s2 = _shift_down(cur, prev8, 2)
    y = s2 * cw[0:1, :] + s1 * cw[1:2, :] + cur * cw[2:3, :]
    return y, s1, s2


def _conv_bwd(d, next8, cw):
    u1 = _shift_up(d, next8, 1)
    u2 = _shift_up(d, next8, 2)
    return d * cw[2:3, :] + u1 * cw[1:2, :] + u2 * cw[0:1, :], u1, u2


def _colsum(a):
    return jnp.sum(a, axis=0, keepdims=True)


def _rms_stats(xv):
    r = lax.rsqrt(jnp.mean(xv * xv, axis=-1, keepdims=True) + EPS)
    return r, xv * r


def _rms_bwd(dh, xv, g):
    r, n = _rms_stats(xv)
    dn = dh * g
    dx = r * (dn - n * jnp.mean(dn * n, axis=-1, keepdims=True))
    return dx, _colsum(dh * n)


def _mixer_forward(p_ref, cprev, xiprev, cw, lng, lnb, ws_ref, bias_ref, mixed_scr, d):
    tm = p_ref.shape[0]
    b = p_ref[:, 0:d]
    c = p_ref[:, d:2 * d]
    xi = p_ref[:, 2 * d:3 * d]
    u = p_ref[:, 3 * d:4 * d]
    v = p_ref[:, 4 * d:5 * d]
    sa = jax.nn.sigmoid(p_ref[:, 5 * d:6 * d])
    sb = jax.nn.sigmoid(p_ref[:, 6 * d:7 * d])
    cx = c * xi
    conv, _, _ = _conv_fwd(cx, cprev * xiprev, cw)
    xc = v - jnp.mean(v, axis=-1, keepdims=True)
    vhat = xc * lax.rsqrt(jnp.mean(xc * xc, axis=-1, keepdims=True) + EPS)
    vnb = (vhat * lng + lnb).astype(BF16)
    tril = (lax.broadcasted_iota(jnp.int32, (CHUNK, CHUNK), 0)
            >= lax.broadcasted_iota(jnp.int32, (CHUNK, CHUNK), 1))
    gd = d // N_GROUPS
    for g in range(N_GROUPS):
        wm = jnp.where(tril, ws_ref[g], 0.0).astype(BF16)
        cols = slice(g * gd, (g + 1) * gd)
        for n in range(tm // CHUNK):
            rows = slice(n * CHUNK, (n + 1) * CHUNK)
            mixed_scr[rows, cols] = (jnp.dot(wm, vnb[rows, cols], preferred_element_type=F32)
                                     + bias_ref[:, cols])
    mixed = mixed_scr[...]
    merged = sa * (b * conv) + sb * (u * mixed)
    return dict(b=b, c=c, xi=xi, u=u, v=v, sa=sa, sb=sb, cx=cx, conv=conv, vnb=vnb, mixed=mixed, merged=merged)


KEPT = ("b", "c", "xi", "u", "v", "conv", "mixed", "vnb", "sa", "sb")


def _once(block_shape, index_map):
    return pl.BlockSpec(block_shape, index_map, pipeline_mode=pl.Buffered(1))


def mixer_fwd(x, g, win, wout, cw, lng, lnb, ws, bias, *, seq, name, tm):
    t, d = x.shape
    nj, _, n = win.shape
    tm = min(tm, seq)
    tiles_per_seq = seq // tm

    def body(x_ref, g_ref, win_ref, wout_ref, cw_ref, lng_ref, lnb_ref, ws_ref, bias_ref,
             h_ref, merged_ref, x1_ref, kept_ref, p_ref, mixed_scr, carry_ref):
        @pl.when(pl.program_id(0) == 0)
        def _():
            carry_ref[...] = jnp.zeros_like(carry_ref)

        keep = jnp.where(pl.program_id(0) % tiles_per_seq == 0, 0.0, 1.0)
        xv = x_ref[...]
        _, nrm = _rms_stats(xv)
        hb = (nrm * g_ref[...]).astype(BF16)
        h_ref[...] = hb
        for j in range(0, nj, 2):
            pair = jnp.concatenate([win_ref[j], win_ref[j + 1]], axis=1)
            p_ref[:, j * n:(j + 2) * n] = jnp.dot(hb, pair, preferred_element_type=F32)
        f = _mixer_forward(p_ref, carry_ref[...] * keep, 1.0, cw_ref[...], lng_ref[...],
                           lnb_ref[...], ws_ref, bias_ref, mixed_scr, d)
        carry_ref[...] = f["cx"][tm - HALO:]
        for k, key in enumerate(KEPT):
            kept_ref[:, k * d:(k + 1) * d] = f[key].astype(BF16)
        mb = f["merged"].astype(BF16)
        merged_ref[...] = mb
        x1_ref[...] = xv + jnp.dot(mb, wout_ref[...], preferred_element_type=F32)

    const2 = lambda i: (0, 0)
    const3 = lambda i: (0, 0, 0)
    row = lambda i: (i, 0)
    return pl.pallas_call(
        body, name=name, grid=(t // tm,),
        in_specs=[pl.BlockSpec((tm, d), row),
                  _once((1, d), const2),
                  _once((nj, d, n), const3),
                  _once((d, d), const2),
                  _once((HALO, d), const2),
                  _once((1, d), const2),
                  _once((1, d), const2),
                  _once((N_GROUPS, CHUNK, CHUNK), const3),
                  _once((CHUNK, d), const2)],
        out_specs=[pl.BlockSpec((tm, d), row), pl.BlockSpec((tm, d), row), pl.BlockSpec((tm, d), row),
                   pl.BlockSpec((tm, len(KEPT) * d), row)],
        out_shape=[jax.ShapeDtypeStruct((t, d), BF16), jax.ShapeDtypeStruct((t, d), BF16),
                   jax.ShapeDtypeStruct((t, d), F32), jax.ShapeDtypeStruct((t, len(KEPT) * d), BF16)],
        scratch_shapes=[pltpu.VMEM((tm, nj * n), F32), pltpu.VMEM((tm, d), F32), pltpu.VMEM((HALO, d), F32)],
        compiler_params=_params(1),
    )(x, g, win, wout, cw, lng, lnb, ws, bias)


def ffn_fwd(x1, g, wup, wd, cw, *, seq, name, tm, head=None):
    t, d = x1.shape
    nj, f, _ = wup.shape
    half = nj // 2
    tm = min(tm, seq)
    tiles_per_seq = seq // tm

    def body(x1_ref, g_ref, wup_ref, wd_ref, cw_ref, *rest):
        if head is None:
            h2_ref, up_ref, fac_ref, act_ref, x2_ref, carry_ref = rest
        else:
            gf_ref, tgt_ref, h2_ref, up_ref, fac_ref, act_ref, x2_ref, dgf_ref, loss_ref, carry_ref = rest

        @pl.when(pl.program_id(0) == 0)
        def _():
            carry_ref[...] = jnp.zeros_like(carry_ref)
            if head is not None:
                dgf_ref[...] = jnp.zeros_like(dgf_ref)
                loss_ref[...] = jnp.zeros_like(loss_ref)

        keep = jnp.where(pl.program_id(0) % tiles_per_seq == 0, 0.0, 1.0)
        xv = x1_ref[...]
        _, nrm = _rms_stats(xv)
        hb = (nrm * g_ref[...]).astype(BF16)
        h2_ref[...] = hb

        for j in range(nj):
            up_ref[j] = lax.dot_general(hb, wup_ref[j], NT_DIMS, preferred_element_type=F32)

        def conv_of(j):
            up0 = up_ref[j]
            y, _, _ = _conv_fwd(up0, carry_ref[j] * keep, cw_ref[j])
            carry_ref[j] = up0[tm - HALO:]
            return y

        acc = xv
        for k in range(half):
            gate, val = conv_of(k), conv_of(k + half)
            sg = jax.nn.sigmoid(gate)
            silu = gate * sg
            fac_ref[k] = (val * (sg * (1.0 + gate * (1.0 - sg)))).astype(BF16)
            fac_ref[k + half] = silu.astype(BF16)
            a = (silu * val).astype(BF16)
            act_ref[k] = a
            acc = acc + jnp.dot(a, wd_ref[k], preferred_element_type=F32)
        if head is None:
            x2_ref[...] = acc
        else:
            gv = gf_ref[...]
            r, n = _rms_stats(acc)
            err = n * gv - tgt_ref[...]
            loss_ref[...] += 0.5 * jnp.sum(jnp.mean(err * err, axis=-1, keepdims=True))
            dy = err * (1.0 / d)
            dn = dy * gv
            x2_ref[...] = r * (dn - n * jnp.mean(dn * n, axis=-1, keepdims=True))
            dgf_ref[0:1, :] += _colsum(dy * n)

    const3 = lambda i: (0, 0, 0)
    row = lambda i: (i, 0)
    in_specs = [pl.BlockSpec((tm, d), row), _once((1, d), lambda i: (0, 0)), _once((nj, f, d), const3),
                _once((half, f, d), const3), _once((nj, HALO, f), const3)]
    out_specs = [pl.BlockSpec((tm, d), row), pl.BlockSpec((nj, tm, f), lambda i: (0, i, 0)),
                 pl.BlockSpec((nj, tm, f), lambda i: (0, i, 0)), pl.BlockSpec((half, tm, f), lambda i: (0, i, 0)),
                 pl.BlockSpec((tm, d), row)]
    out_shape = [jax.ShapeDtypeStruct((t, d), BF16), jax.ShapeDtypeStruct((nj, t, f), F32),
                 jax.ShapeDtypeStruct((nj, t, f), BF16), jax.ShapeDtypeStruct((half, t, f), BF16),
                 jax.ShapeDtypeStruct((t, d), F32)]
    args = [x1, g, wup, wd, cw]
    if head is not None:
        in_specs += [_once((1, d), lambda i: (0, 0)), pl.BlockSpec((tm, d), row)]
        out_specs += [pl.BlockSpec((HALO, d), lambda i: (0, 0)), pl.BlockSpec((8, 128), lambda i: (0, 0))]
        out_shape += [jax.ShapeDtypeStruct((HALO, d), F32), jax.ShapeDtypeStruct((8, 128), F32)]
        args += list(head)
    return pl.pallas_call(
        body, name=name, grid=(t // tm,), in_specs=in_specs, out_specs=out_specs, out_shape=out_shape,
        scratch_shapes=[pltpu.VMEM((nj, HALO, f), F32)],
        compiler_params=_params(1),
    )(*args)


def ffn_bwd(dx2, up0, fac, wd, cw, wup, x1, g, *, seq, name, tm):
    t, d = dx2.shape
    nj, _, f = up0.shape
    half = nj // 2
    tm = min(tm, seq)
    tiles_per_seq = seq // tm
    nt = t // tm

    def body(dx_ref, up_ref, fac_ref, wd_ref, cw_ref, wup_ref, x1_ref, g_ref,
             dup_ref, dcw_ref, dx1_ref, dg_ref, carry_ref):
        i = pl.program_id(0)
        tile = nt - 1 - i

        @pl.when(i == 0)
        def _():
            dcw_ref[...] = jnp.zeros_like(dcw_ref)
            dg_ref[...] = jnp.zeros_like(dg_ref)
            carry_ref[...] = jnp.zeros_like(carry_ref)

        keep_next = jnp.where(tile % tiles_per_seq == tiles_per_seq - 1, 0.0, 1.0)
        dx2v = dx_ref[...]
        dxb = dx2v.astype(BF16)
        dh = [jnp.zeros((tm, d), F32)]

        def through_conv(j, dup):
            next8 = carry_ref[j] * keep_next
            carry_ref[j] = dup[:HALO]
            dup0, u1, u2 = _conv_bwd(dup, next8, cw_ref[j])
            up0 = up_ref[j]
            dcw_ref[j, 0:1, :] += _colsum(u2 * up0)
            dcw_ref[j, 1:2, :] += _colsum(u1 * up0)
            dcw_ref[j, 2:3, :] += _colsum(dup * up0)
            dup0 = dup0.astype(BF16)
            dup_ref[j] = dup0
            dh[0] = dh[0] + jnp.dot(dup0, wup_ref[j], preferred_element_type=F32)

        dacts = [lax.dot_general(dxb, wd_ref[k], NT_DIMS, preferred_element_type=F32) for k in range(half)]
        for k in range(half):
            through_conv(k, dacts[k] * fac_ref[k].astype(F32))
            through_conv(k + half, dacts[k] * fac_ref[k + half].astype(F32))

        dx, dg = _rms_bwd(dh[0], x1_ref[...], g_ref[...])
        dx1_ref[...] = dx2v + dx
        dg_ref[0:1, :] += dg

    rev = lambda i: nt - 1 - i
    return pl.pallas_call(
        body, name=name, grid=(nt,),
        in_specs=[pl.BlockSpec((tm, d), lambda i: (rev(i), 0)),
                  pl.BlockSpec((nj, tm, f), lambda i: (0, rev(i), 0)),
                  pl.BlockSpec((nj, tm, f), lambda i: (0, rev(i), 0)),
                  _once((half, f, d), lambda i: (0, 0, 0)),
                  _once((nj, HALO, f), lambda i: (0, 0, 0)),
                  _once((nj, f, d), lambda i: (0, 0, 0)),
                  pl.BlockSpec((tm, d), lambda i: (rev(i), 0)),
                  _once((1, d), lambda i: (0, 0))],
        out_specs=[pl.BlockSpec((nj, tm, f), lambda i: (0, rev(i), 0)),
                   pl.BlockSpec((nj, HALO, f), lambda i: (0, 0, 0)),
                   pl.BlockSpec((tm, d), lambda i: (rev(i), 0)),
                   pl.BlockSpec((HALO, d), lambda i: (0, 0))],
        out_shape=[jax.ShapeDtypeStruct((nj, t, f), BF16), jax.ShapeDtypeStruct((nj, HALO, f), F32),
                   jax.ShapeDtypeStruct((t, d), F32), jax.ShapeDtypeStruct((HALO, d), F32)],
        scratch_shapes=[pltpu.VMEM((nj, HALO, f), F32)],
        compiler_params=_params(1),
    )(dx2, up0, fac, wd, cw, wup, x1, g)


def mixer_bwd(dx1, kept, wout, cw, lng, wst, *, seq, name, tm, on_to_x0=None):
    t, d = dx1.shape
    tm = min(tm, seq)
    tiles_per_seq = seq // tm
    nt = t // tm
    gd = d // N_GROUPS
    if on_to_x0 is not None:
        nj, _, wn = on_to_x0[0].shape

    def body(dx_ref, k_ref, wout_ref, cw_ref, lng_ref, wst_ref, *rest):
        if on_to_x0 is None:
            dp_ref, dcw_ref, dln_ref, dws_ref, dbs_ref, dvn_scr, carry_ref, dbs_acc = rest
        else:
            (win_ref, x0_ref, g_ref, dp_ref, dcw_ref, dln_ref, dws_ref, dbs_ref, dx0_ref, dg_ref,
             dvn_scr, carry_ref, dbs_acc) = rest
        i = pl.program_id(0)
        tile = nt - 1 - i

        @pl.when(i == 0)
        def _():
            dcw_ref[...] = jnp.zeros_like(dcw_ref)
            dln_ref[...] = jnp.zeros_like(dln_ref)
            dws_ref[...] = jnp.zeros_like(dws_ref)
            dbs_acc[...] = jnp.zeros_like(dbs_acc)
            if on_to_x0 is not None:
                dg_ref[...] = jnp.zeros_like(dg_ref)
            carry_ref[...] = jnp.zeros_like(carry_ref)

        keep_next = jnp.where(tile % tiles_per_seq == tiles_per_seq - 1, 0.0, 1.0)
        cw = cw_ref[...]
        lng = lng_ref[...]
        kept_f32 = {key: k_ref[:, k * d:(k + 1) * d].astype(F32) for k, key in enumerate(KEPT) if key != "vnb"}
        b, c, xi, u, v, conv, mixed, sa, sb = (kept_f32[key] for key in KEPT if key != "vnb")
        vnb = k_ref[:, KEPT.index("vnb") * d:(KEPT.index("vnb") + 1) * d]
        xc = v - jnp.mean(v, axis=-1, keepdims=True)
        rstd = lax.rsqrt(jnp.mean(xc * xc, axis=-1, keepdims=True) + EPS)
        vhat = xc * rstd
        dmerged = lax.dot_general(dx_ref[...].astype(BF16), wout_ref[...], NT_DIMS, preferred_element_type=F32)
        dp_ref[:, 5 * d:6 * d] = (dmerged * (b * conv) * (sa * (1.0 - sa))).astype(BF16)
        dp_ref[:, 6 * d:7 * d] = (dmerged * (u * mixed) * (sb * (1.0 - sb))).astype(BF16)
        dya = dmerged * sa
        dyb = dmerged * sb
        dp_ref[:, 0:d] = (dya * conv).astype(BF16)
        dconv = dya * b
        next8 = carry_ref[...] * keep_next
        carry_ref[...] = dconv[:HALO]
        dcx, u1, u2 = _conv_bwd(dconv, next8, cw)
        cx = c * xi
        dcw_ref[0:1, :] += _colsum(u2 * cx)
        dcw_ref[1:2, :] += _colsum(u1 * cx)
        dcw_ref[2:3, :] += _colsum(dconv * cx)
        dp_ref[:, d:2 * d] = (dcx * xi).astype(BF16)
        dp_ref[:, 2 * d:3 * d] = (dcx * c).astype(BF16)
        dp_ref[:, 3 * d:4 * d] = (dyb * mixed).astype(BF16)
        dmixed = dyb * u
        dmb = dmixed.astype(BF16)
        tril = (lax.broadcasted_iota(jnp.int32, (CHUNK, CHUNK), 0)
                >= lax.broadcasted_iota(jnp.int32, (CHUNK, CHUNK), 1))
        triu = (lax.broadcasted_iota(jnp.int32, (CHUNK, CHUNK), 0)
                <= lax.broadcasted_iota(jnp.int32, (CHUNK, CHUNK), 1))
        dbs_tile = dmixed[0:CHUNK]
        for n in range(1, tm // CHUNK):
            dbs_tile = dbs_tile + dmixed[n * CHUNK:(n + 1) * CHUNK]
        dbs_acc[...] += dbs_tile
        for g in range(N_GROUPS):
            wmt = jnp.where(triu, wst_ref[g], 0.0).astype(BF16)
            cols = slice(g * gd, (g + 1) * gd)
            dw = jnp.zeros((CHUNK, CHUNK), F32)
            for n in range(tm // CHUNK):
                rows = slice(n * CHUNK, (n + 1) * CHUNK)
                dvn_scr[rows, cols] = jnp.dot(wmt, dmb[rows, cols], preferred_element_type=F32)
                dw = dw + lax.dot_general(dmb[rows, cols], vnb[rows, cols], NT_DIMS, preferred_element_type=F32)
            dws_ref[g] += jnp.where(tril, dw, 0.0)
        dvn = dvn_scr[...]
        dln_ref[0:1, :] += _colsum(dvn * vhat)
        dln_ref[1:2, :] += _colsum(dvn)
        dvh = dvn * lng
        dv = rstd * (dvh - jnp.mean(dvh, axis=-1, keepdims=True)
                     - vhat * jnp.mean(dvh * vhat, axis=-1, keepdims=True))
        dp_ref[:, 4 * d:5 * d] = dv.astype(BF16)
        if on_to_x0 is not None:
            dh = jnp.zeros((tm, d), F32)
            for j in range(0, nj, 2):
                pair = jnp.concatenate([win_ref[j], win_ref[j + 1]], axis=1)
                dh = dh + lax.dot_general(dp_ref[:, j * wn:(j + 2) * wn], pair, NT_DIMS, preferred_element_type=F32)
            dx, dg = _rms_bwd(dh, x0_ref[...], g_ref[...])
            dx0_ref[...] = dx_ref[...] + dx
            dg_ref[0:1, :] += dg

        @pl.when(i == nt - 1)
        def _():
            for g in range(N_GROUPS):
                cols = slice(g * gd, (g + 1) * gd)
                s = jnp.sum(dbs_acc[:, cols], axis=1, keepdims=True)
                dbs_ref[:, cols] = jnp.broadcast_to(s, (CHUNK, gd))

    rev = lambda i: nt - 1 - i
    const2 = lambda i: (0, 0)
    const3 = lambda i: (0, 0, 0)
    row = lambda i: (rev(i), 0)
    in_specs = [pl.BlockSpec((tm, d), row), pl.BlockSpec((tm, len(KEPT) * d), row),
                _once((d, d), const2), _once((HALO, d), const2), _once((1, d), const2),
                _once((N_GROUPS, CHUNK, CHUNK), const3)]
    out_specs = [pl.BlockSpec((tm, 7 * d), row), pl.BlockSpec((HALO, d), const2), pl.BlockSpec((HALO, d), const2),
                 pl.BlockSpec((N_GROUPS, CHUNK, CHUNK), const3), pl.BlockSpec((CHUNK, d), const2)]
    out_shape = [jax.ShapeDtypeStruct((t, 7 * d), BF16), jax.ShapeDtypeStruct((HALO, d), F32),
                 jax.ShapeDtypeStruct((HALO, d), F32), jax.ShapeDtypeStruct((N_GROUPS, CHUNK, CHUNK), F32),
                 jax.ShapeDtypeStruct((CHUNK, d), F32)]
    args = [dx1, kept, wout, cw, lng, wst]
    if on_to_x0 is not None:
        in_specs += [_once((nj, d, wn), const3), pl.BlockSpec((tm, d), row), _once((1, d), const2)]
        out_specs += [pl.BlockSpec((tm, d), row), pl.BlockSpec((HALO, d), const2)]
        out_shape += [jax.ShapeDtypeStruct((t, d), F32), jax.ShapeDtypeStruct((HALO, d), F32)]
        args += list(on_to_x0)
    return pl.pallas_call(
        body, name=name, grid=(nt,), in_specs=in_specs, out_specs=out_specs, out_shape=out_shape,
        scratch_shapes=[pltpu.VMEM((tm, d), F32), pltpu.VMEM((HALO, d), F32), pltpu.VMEM((CHUNK, d), F32)],
        compiler_params=_params(1),
    )(*args)


def dgrad_rms(dy, w, x, g, res, *, name, tm):
    t, d = x.shape
    n = w.shape[2]
    w = w.reshape(w.shape[0] // 2, 2, d, n)
    nj = w.shape[0]
    tm = min(tm, t)

    def body(dy_ref, w_ref, x_ref, g_ref, res_ref, dx_ref, dg_ref, acc_ref):
        i, j = pl.program_id(0), pl.program_id(1)

        @pl.when((i == 0) & (j == 0))
        def _():
            dg_ref[...] = jnp.zeros_like(dg_ref)

        pair = jnp.concatenate([w_ref[0], w_ref[1]], axis=1)
        part = lax.dot_general(dy_ref[...], pair, NT_DIMS, preferred_element_type=F32)

        @pl.when(j == 0)
        def _():
            acc_ref[...] = part

        @pl.when(j > 0)
        def _():
            acc_ref[...] += part

        @pl.when(j == nj - 1)
        def _():
            dx, dg = _rms_bwd(acc_ref[...], x_ref[...], g_ref[...])
            dx_ref[...] = res_ref[...] + dx
            dg_ref[0:1, :] += dg

    return pl.pallas_call(
        body, name=name, grid=(t // tm, nj),
        in_specs=[pl.BlockSpec((tm, 2 * n), lambda i, j: (i, j)),
                  pl.BlockSpec((None, 2, d, n), lambda i, j: (j, 0, 0, 0)),
                  pl.BlockSpec((tm, d), lambda i, j: (i, 0)),
                  pl.BlockSpec((1, d), lambda i, j: (0, 0)),
                  pl.BlockSpec((tm, d), lambda i, j: (i, 0))],
        out_specs=[pl.BlockSpec((tm, d), lambda i, j: (i, 0)), pl.BlockSpec((HALO, d), lambda i, j: (0, 0))],
        out_shape=[jax.ShapeDtypeStruct((t, d), F32), jax.ShapeDtypeStruct((HALO, d), F32)],
        scratch_shapes=[pltpu.VMEM((tm, d), F32)],
        compiler_params=_params(2),
    )(dy, w, x, g, res)


def wgrad(a, b, *, nj, a_mode, b_mode, name, tm, split=1):
    def describe(arr, mode):
        if mode == "full":
            return arr.shape[0], arr.shape[1], pl.BlockSpec((tm_, arr.shape[1]), lambda j, s: (s, 0))
        if mode == "cols":
            c = arr.shape[1] // nj
            return arr.shape[0], c, pl.BlockSpec((tm_, c), lambda j, s: (s, j))
        return arr.shape[1], arr.shape[2], pl.BlockSpec((None, tm_, arr.shape[2]), lambda j, s: (j, s, 0))

    t = a.shape[0] if a_mode != "lead" else a.shape[1]
    tm_ = min(tm, t)
    _, k, a_spec = describe(a, a_mode)
    _, n, b_spec = describe(b, b_mode)

    ns = t // tm_
    nc = n // split

    def body(a_ref, b_ref, o_ref, acc_ref):
        s = pl.program_id(1)
        part = lax.dot_general(a_ref[...], b_ref[...], TN_DIMS, preferred_element_type=F32)

        def finish(total):
            for q in range(split):
                o_ref[q] = total[:, q * nc:(q + 1) * nc].astype(BF16)

        if ns == 1:
            finish(part)
            return

        @pl.when(s == 0)
        def _():
            acc_ref[...] = part

        @pl.when((s > 0) & (s < ns - 1))
        def _():
            acc_ref[...] += part

        @pl.when(s == ns - 1)
        def _():
            finish(acc_ref[...] + part)

    return pl.pallas_call(
        body, name=name, grid=(nj, ns),
        in_specs=[a_spec, b_spec],
        out_specs=pl.BlockSpec((split, k, nc), lambda j, s: (j, 0, 0)),
        out_shape=jax.ShapeDtypeStruct((nj * split, k, nc), BF16),
        scratch_shapes=[pltpu.VMEM((k, n), F32)],
        compiler_params=_params(2),
    )(a, b)


def _adamw_math(w, g, m, v):
    m = ADAM_B1 * m + (1.0 - ADAM_B1) * g
    v = ADAM_B2 * v + (1.0 - ADAM_B2) * (g * g)
    m_hat = m / (1.0 - ADAM_B1 ** ADAM_STEP)
    v_hat = v / (1.0 - ADAM_B2 ** ADAM_STEP)
    delta = -ADAM_LR * (m_hat / (jnp.sqrt(v_hat) + ADAM_EPS) + ADAM_WD * w)
    return delta, m, v


def _row_tile(rows, at_most):
    if rows <= at_most:
        return rows
    return max(k for k in range(16, at_most + 1, 16) if rows % k == 0)


def _sum_in_device_order(ref):
    total = ref[0]
    for s in range(1, N_DEV):
        total = total + ref[s]
    return total


def adamw_sharded(me, own0, recv0, own1, recv1, w, m, v, *, name, tr):
    _, r, c = w.shape
    tr = _row_tile(r, tr)
    ni = r // tr

    def body(me_ref, o0_ref, r0_ref, o1_ref, r1_ref, w_ref, m_ref, v_ref, g_ref, d_ref, nm_ref, nv_ref):
        def finish(own_ref, recv_ref):
            g = None
            for s in range(N_DEV):
                term = jnp.where(me_ref[0] == s, own_ref[...], recv_ref[s]).astype(F32)
                g = term if g is None else g + term
            delta, nm, nv = _adamw_math(w_ref[...], g, m_ref[...], v_ref[...])
            g_ref[...] = g
            d_ref[...] = delta
            nm_ref[...] = nm
            nv_ref[...] = nv

        @pl.when(pl.program_id(0) == 0)
        def _():
            finish(o0_ref, r0_ref)

        @pl.when(pl.program_id(0) == 1)
        def _():
            finish(o1_ref, r1_ref)

    row0 = lambda l, i: i * (1 - l) + (ni - 1) * l
    row1 = lambda l, i: i * l
    lay = pl.BlockSpec((None, tr, c), lambda l, i, me_ref: (l, i, 0))
    grid_spec = pltpu.PrefetchScalarGridSpec(
        num_scalar_prefetch=1, grid=(2, ni),
        in_specs=[pl.BlockSpec((None, tr, c), lambda l, i, me_ref: (me_ref[0], row0(l, i), 0)),
                  pl.BlockSpec((N_DEV, tr, c), lambda l, i, me_ref: (0, row0(l, i), 0)),
                  pl.BlockSpec((None, tr, c), lambda l, i, me_ref: (me_ref[0], row1(l, i), 0)),
                  pl.BlockSpec((N_DEV, tr, c), lambda l, i, me_ref: (0, row1(l, i), 0)),
                  lay, lay, lay],
        out_specs=[lay, lay, lay, lay])
    return pl.pallas_call(
        body, name=name, grid_spec=grid_spec,
        out_shape=[jax.ShapeDtypeStruct(w.shape, F32)] * 4,
        compiler_params=_params(2),
    )(me, own0, recv0, own1, recv1, w, m, v)


def sum_chunks(me, own, recv, *, name):
    _, r, c = own.shape

    def body(me_ref, o_ref, r_ref, out_ref):
        total = None
        for s in range(N_DEV):
            term = jnp.where(me_ref[0] == s, o_ref[...], r_ref[s]).astype(F32)
            total = term if total is None else total + term
        out_ref[...] = total

    grid_spec = pltpu.PrefetchScalarGridSpec(
        num_scalar_prefetch=1, grid=(1,),
        in_specs=[pl.BlockSpec((None, r, c), lambda i, me_ref: (me_ref[0], 0, 0)),
                  pl.BlockSpec((N_DEV, r, c), lambda i, me_ref: (0, 0, 0))],
        out_specs=pl.BlockSpec((r, c), lambda i, me_ref: (0, 0)))
    return pl.pallas_call(
        body, name=name, grid_spec=grid_spec,
        out_shape=jax.ShapeDtypeStruct((r, c), F32),
        compiler_params=_params(1),
    )(me, own, recv)


def adamw_small(g, w, m, v, *, name):
    def body(g_ref, w_ref, m_ref, v_ref, d_ref, nm_ref, nv_ref):
        delta, nm, nv = _adamw_math(w_ref[...], g_ref[...], m_ref[...], v_ref[...])
        d_ref[...] = delta
        nm_ref[...] = nm
        nv_ref[...] = nv

    return pl.pallas_call(
        body, name=name,
        out_shape=[jax.ShapeDtypeStruct(w.shape, F32)] * 3,
        compiler_params=pltpu.CompilerParams(vmem_limit_bytes=VMEM_LIMIT_BYTES),
    )(g, w, m, v)


def sum_devices(parts, *, name, tr):
    _, r, c = parts.shape
    tr = min(tr, r)

    def body(p_ref, o_ref):
        o_ref[...] = _sum_in_device_order(p_ref)

    return pl.pallas_call(
        body, name=name, grid=(r // tr,),
        in_specs=[pl.BlockSpec((N_DEV, tr, c), lambda i: (0, i, 0))],
        out_specs=pl.BlockSpec((tr, c), lambda i: (i, 0)),
        out_shape=jax.ShapeDtypeStruct((r, c), F32),
        compiler_params=_params(1),
    )(parts)


def _my_place():
    return lax.axis_index("x"), lax.axis_index("y"), lax.axis_index("c")


def all_gather(arrays, after, *, name):
    n = len(arrays)

    def body(*refs):
        ins, outs = refs[:n], refs[n + 1:2 * n + 1]
        send_sems, recv_sems, local_sems = refs[2 * n + 1:]
        x, y, c = _my_place()
        me, sibling = (x, y, c), (x, y, 1 - c)
        chips = [(1 - x, y), (x, 1 - y), (1 - x, 1 - y)]
        waits = []
        for a in range(n):
            def slot(place, a=a):
                px, py, pc = place
                return outs[a].at[4 * px + 2 * py + pc]

            def copy(k, block, to, src=None, a=a, slot=slot):
                return pltpu.make_async_remote_copy(
                    src_ref=slot(block) if src is None else src, dst_ref=slot(block),
                    send_sem=send_sems.at[a, k], recv_sem=recv_sems.at[a, k],
                    device_id=to, device_id_type=MESH)

            mine = pltpu.make_async_copy(ins[a], slot(me), local_sems.at[a])
            mine.start()
            first = [copy(0, me, sibling, src=ins[a])]
            first += [copy(1 + j, me, (*chip, c), src=ins[a]) for j, chip in enumerate(chips)]
            for cp in first:
                cp.start()
            waits.append((copy, mine, first))
        sends = []
        for a in range(n):
            copy, mine, first = waits[a]
            passed = [copy(4 + j, (*chip, c), sibling) for j, chip in enumerate(chips)]
            for j, chip in enumerate(chips):
                copy(1 + j, (*chip, c), me).wait_recv()
                passed[j].start()
            sends.append(first + passed)
        for a in range(n):
            copy, mine, first = waits[a]
            copy(0, sibling, me).wait_recv()
            for j, chip in enumerate(chips):
                copy(4 + j, (*chip, 1 - c), me).wait_recv()
            for cp in sends[a]:
                cp.wait_send()
            mine.wait()

    return pl.pallas_call(
        body, name=name,
        in_specs=[ANY] * (n + 1), out_specs=[ANY] * n,
        out_shape=[jax.ShapeDtypeStruct((N_DEV,) + a.shape, a.dtype) for a in arrays],
        scratch_shapes=[pltpu.SemaphoreType.DMA((n, 7)), pltpu.SemaphoreType.DMA((n, 7)),
                        pltpu.SemaphoreType.DMA((n,))],
        compiler_params=pltpu.CompilerParams(has_side_effects=True),
    )(*arrays, after)


def _peer_place(r, x, y, c):
    fx, fy, fc = (r >> 2) & 1, (r >> 1) & 1, r & 1
    return (1 - x if fx else x, 1 - y if fy else y, 1 - c if fc else c)


def own_slot(me, w, layer, dtype, *, name, tr):
    _, r, c = w.shape
    tr = _row_tile(r, tr)

    def body(me_ref, w_ref, o_ref):
        o_ref[...] = w_ref[...].astype(dtype)

    grid_spec = pltpu.PrefetchScalarGridSpec(
        num_scalar_prefetch=1, grid=(r // tr,),
        in_specs=[pl.BlockSpec((None, tr, c), lambda i, me_ref: (layer, i, 0))],
        out_specs=pl.BlockSpec((None, tr, c), lambda i, me_ref: (me_ref[0], i, 0)))
    return pl.pallas_call(
        body, name=name, grid_spec=grid_spec,
        out_shape=jax.ShapeDtypeStruct((N_DEV, r, c), dtype),
        compiler_params=_params(1),
    )(me, w)


EXCHANGES = {
    "scatter": [(0, r) for r in range(1, N_DEV)],
    "gather": [(0, r) for r in range(1, N_DEV)],
    "gather_chips": [(0, r) for r in (1, 2, 4, 6)],
    "gather_forward": [(q, 1) for q in (2, 4, 6)],
}


def _split_copy(k, entry, src, land, send_sem, recv_sem, arriving):
    slot, peer = entry
    x, y, c = _my_place()

    def index(relation):
        px, py, pc = _peer_place(relation, x, y, c)
        return 4 * px + 2 * py + pc

    return pltpu.make_async_remote_copy(
        src_ref=land.at[index(slot)] if src is None else src.at[index(peer)],
        dst_ref=land.at[index(slot ^ peer if arriving else slot)],
        send_sem=send_sem.at[k], recv_sem=recv_sem.at[k],
        device_id=_peer_place(peer, x, y, c), device_id_type=MESH)


def start_copies(srcs, lands, *, mode, name, after=None):
    n = len(lands)
    entries = EXCHANGES[mode]
    bufs = (list(srcs) if srcs is not None else []) + list(lands)
    nb = len(bufs)

    def body(*refs):
        src = refs[:n] if srcs is not None else [None] * n
        land = refs[nb - n:nb]
        outs = refs[nb + len(extra):]
        send_sems, recv_sems = outs[:n], outs[n:2 * n]
        token = outs[2 * n + nb]
        for a in range(n):
            for k, entry in enumerate(entries):
                _split_copy(k, entry, src[a], land[a], send_sems[a], recv_sems[a], False).start()
        token[...] = jnp.zeros_like(token)

    extra = [] if after is None else [after]
    outs = pl.pallas_call(
        body, name=name,
        in_specs=[HBM_SPEC] * nb + [ANY] * len(extra),
        out_specs=[SEM_SPEC] * (2 * n) + [HBM_SPEC] * nb + [pl.BlockSpec(memory_space=pltpu.VMEM)],
        out_shape=([pltpu.SemaphoreType.DMA((len(entries),))] * (2 * n)
                   + [pltpu.HBM(a.shape, a.dtype) for a in bufs]
                   + [jax.ShapeDtypeStruct((8, 128), F32)]),
        input_output_aliases={i: 2 * n + i for i in range(nb)},
        compiler_params=pltpu.CompilerParams(has_side_effects=DATAFLOW),
    )(*[pltpu.with_memory_space_constraint(a, pltpu.HBM) for a in bufs], *extra)
    thru = list(outs[2 * n:2 * n + nb])
    return dict(send=outs[:n], recv=outs[n:2 * n], src=thru[:n] if srcs is not None else None, land=thru[nb - n:],
                token=outs[2 * n + nb], mode=mode)


def finish_copies(started, which, after, *, name):
    n = len(which)
    entries = EXCHANGES[started["mode"]]
    has_src = started["src"] is not None
    bufs = ([started["src"][i] for i in which] if has_src else []) + [started["land"][i] for i in which]
    nb = len(bufs)

    def body(*refs):
        src = refs[:n] if has_src else [None] * n
        land = refs[nb - n:nb]
        send_sems, recv_sems = refs[nb:nb + n], refs[nb + n:nb + 2 * n]
        for a in range(n):
            for k, entry in enumerate(entries):
                cp = _split_copy(k, entry, src[a], land[a], send_sems[a], recv_sems[a], True)
                cp.wait_send()
                cp.wait_recv()

    outs = pl.pallas_call(
        body, name=name,
        in_specs=[HBM_SPEC] * nb + [SEM_SPEC] * (2 * n) + [ANY],
        out_specs=[HBM_SPEC] * nb,
        out_shape=[pltpu.HBM(a.shape, a.dtype) for a in bufs],
        input_output_aliases={i: i for i in range(nb)},
        compiler_params=pltpu.CompilerParams(has_side_effects=DATAFLOW),
    )(*bufs, *[started["send"][i] for i in which], *[started["recv"][i] for i in which], after)
    return (list(outs[:n]) if has_src else None), list(outs[nb - n:])


def _pad_rows(a, rows):
    pad = [(0, 0)] * a.ndim
    pad[-2] = (0, rows - a.shape[-2])
    return jnp.pad(a, pad)


def kernel(x, mix_norm_g, w_in, conv_a_w, ln_v_g, ln_v_b, w_s, b_s, w_out, ffn_norm_g, w_up, conv_ffn_w, w_down, final_norm_g, loss_target, m_mix_norm_g, m_w_in, m_conv_a_w, m_ln_v_g, m_ln_v_b, m_w_s, m_b_s, m_w_out, m_ffn_norm_g, m_w_up, m_conv_ffn_w, m_w_down, m_final_norm_g, v_mix_norm_g, v_w_in, v_conv_a_w, v_ln_v_g, v_ln_v_b, v_w_s, v_b_s, v_w_out, v_ffn_norm_g, v_w_up, v_conv_ffn_w, v_w_down, v_final_norm_g):
    nb, seq, d = x.shape
    t = nb * seq
    depth = w_in.shape[0]
    f = w_up.shape[2]
    me = 4 * lax.axis_index("x") + 2 * lax.axis_index("y") + lax.axis_index("c")
    xt = x.reshape(t, d)
    tgt = loss_target.reshape(t, d)

    conv_pack = jnp.concatenate([_pad_rows(conv_a_w, HALO), _pad_rows(conv_ffn_w, HALO)], axis=-1)
    me_arr = me.astype(jnp.int32).reshape(1)
    w_up_t, m_w_up_t, v_w_up_t = (jnp.swapaxes(a, 1, 2) for a in (w_up, m_w_up, v_w_up))
    zones, slot_of = [], {}
    for l in range(depth):
        for key, w in (("win", w_in), ("conv", None), ("wout", w_out), ("wup", w_up_t), ("wd", w_down)):
            if key == "conv":
                if l == 0:
                    slot_of["conv"] = len(zones)
                    packed = conv_pack.reshape(1, depth * HALO, conv_pack.shape[-1])
                    zones.append(own_slot(me_arr, packed, 0, F32, name="own_slot_conv", tr=ROWS_PER_STEP["own_slot"]))
                continue
            slot_of[key, l] = len(zones)
            zones.append(own_slot(me_arr, w, l, BF16, name=f"own_slot_{key}_{l}", tr=ROWS_PER_STEP["own_slot"]))
    first = [slot_of["win", 0], slot_of["wout", 0], slot_of["conv"]]
    rest = [i for i in range(len(zones)) if i not in first]
    to_chips = start_copies(None, [zones[i] for i in first], mode="gather_chips", name="gather_first_chips")
    gathering = start_copies(None, [zones[i] for i in rest], mode="gather", name="gather_start", after=to_chips["token"])
    _, at_chips = finish_copies(to_chips, [0, 1, 2], gathering["token"], name="wait_first_chips")
    to_sibling = start_copies(None, at_chips, mode="gather_forward", name="gather_first_forward")

    def gathered(keys, after, name):
        return finish_copies(gathering, [rest.index(slot_of[k]) for k in keys], after, name=name)[1]

    saved, layers = [], []
    cur = xt
    for l in range(depth):
        p = dict(mix_g=mix_norm_g[l][None], ffn_g=ffn_norm_g[l][None], lng=ln_v_g[l][None], lnb=ln_v_b[l][None],
                 ws=w_s[l], wst=jnp.swapaxes(w_s[l], 1, 2),
                 bias=jnp.repeat(b_s[l].T, d // N_GROUPS, axis=1))
        if l == 0:
            _, (p["win"], wout_g, conv_g) = finish_copies(to_sibling, [0, 1, 2], to_sibling["token"],
                                                          name=f"wait_w_mixer_{l}")
            conv_g = conv_g.reshape(N_DEV, depth, HALO, -1)
            ca = conv_g.shape[-1] - f
        else:
            p["win"], wout_g = gathered([("win", l), ("wout", l)], after, f"wait_w_mixer_{l}")
        p["wout"] = wout_g.reshape(d, d)
        p["cw_a"] = jnp.transpose(conv_g[:, l, :, :ca], (1, 0, 2)).reshape(HALO, d)
        p["cw_f"] = conv_g[:, l, :, ca:]
        h, merged, x1, kept = mixer_fwd(cur, p["mix_g"], p["win"], p["wout"], p["cw_a"], p["lng"], p["lnb"],
                                        p["ws"], p["bias"], seq=seq, name=f"mixer_fwd_{l}", tm=TOKENS_PER_STEP["mixer"])
        p["wup"], wd_g = gathered([("wup", l), ("wd", l)], merged, f"wait_w_ffn_{l}")
        p["wd"] = wd_g.reshape(N_DEV // 2, 2 * wd_g.shape[1], d)
        head = (final_norm_g[None], tgt) if l == depth - 1 else None
        h2, up0, fac, act, x2, *of_loss = ffn_fwd(x1, p["ffn_g"], p["wup"], p["wd"], p["cw_f"],
                                                  seq=seq, name=f"ffn_fwd_{l}", tm=TOKENS_PER_STEP["ffn"], head=head)
        saved.append(dict(x0=cur, h=h, kept=kept, merged=merged, x1=x1, h2=h2, up0=up0, fac=fac, act=act))
        layers.append(p)
        cur, after = x2, act
    dx = cur
    d_final_g, loss_tile = of_loss

    def exchange(parts, name):
        return start_copies(parts, [lax.empty(a.shape, a.dtype) for a in parts], mode="scatter", name=name)

    def tied(g, started):
        return g + started["token"][0:1, 0:1]

    part = [None] * depth
    mix_ex = None
    for l in reversed(range(depth)):
        p, s = layers[l], saved[l]
        ffn_g = p["ffn_g"] if mix_ex is None else tied(p["ffn_g"], mix_ex)
        dup0, dcw_f, dx1, d_ffn_g = ffn_bwd(dx, s["up0"], s["fac"], p["wd"], p["cw_f"], p["wup"], s["x1"], ffn_g,
                                            seq=seq, name=f"ffn_bwd_{l}", tm=TOKENS_PER_STEP["ffn"])
        g_wd = wgrad(s["act"], dx, nj=N_DEV // 2, a_mode="lead", b_mode="full", name=f"wgrad_down_{l}", tm=TOKENS_PER_STEP["wgrad"])
        g_wup = wgrad(dup0, s["h2"], nj=N_DEV, a_mode="lead", b_mode="full", name=f"wgrad_up_{l}", tm=TOKENS_PER_STEP["wgrad"])
        ffn_ex = exchange([g_wd.reshape(N_DEV, g_wd.shape[1] // 2, d), g_wup], f"exchange_ffn_{l}")
        fused = l > 0
        dproj, dcw_a, dln, dws, dbs, *to_x0 = mixer_bwd(
            dx1, s["kept"], p["wout"], tied(p["cw_a"], ffn_ex), p["lng"], p["wst"],
            seq=seq, name=f"mixer_bwd_{l}", tm=TOKENS_PER_STEP["mixer" if fused else "mixer_bwd_alone"], on_to_x0=(p["win"], s["x0"], p["mix_g"]) if fused else None)
        g_wout = wgrad(s["merged"], dx1, nj=1, a_mode="full", b_mode="full", name=f"wgrad_out_{l}", tm=TOKENS_PER_STEP["wgrad"])
        g_win = wgrad(s["h"], dproj, nj=N_DEV // 2, a_mode="full", b_mode="cols", name=f"wgrad_in_{l}", tm=TOKENS_PER_STEP["wgrad"], split=2)
        cwa_chunks = jnp.transpose(dcw_a.reshape(HALO, N_DEV, d // N_DEV), (1, 0, 2))
        mix_ex = exchange([g_wout.reshape(N_DEV, d // N_DEV, d), g_win, dws, dcw_f, cwa_chunks], f"exchange_mix_{l}")
        if fused:
            dx, d_mix_g = to_x0
        else:
            dx, d_mix_g = dgrad_rms(dproj, p["win"], s["x0"], tied(p["mix_g"], mix_ex), dx1,
                                    name=f"dgrad_in_{l}", tm=TOKENS_PER_STEP["dgrad"])
        part[l] = dict(
            ffn_ex=ffn_ex, mix_ex=mix_ex,
            vectors=jnp.concatenate([d_mix_g[0:1], d_ffn_g[0:1], dln[0:2],
                                     dbs[:, ::d // N_GROUPS].T.reshape(1, d)], axis=0))
    grad_x = dx.reshape(nb, seq, d)

    own, recv = {}, {}

    def arrived(l, ex, keys, after):
        srcs, lands = finish_copies(part[l][ex], list(range(len(keys))), after, name=f"wait_{ex}_{l}")
        for k, key in enumerate(keys):
            own[key, l], recv[key, l] = srcs[k], lands[k]
        return lands[1]

    def big(key, w, m, v, name):
        return adamw_sharded(me_arr, own[key, 0], recv[key, 0], own[key, 1], recv[key, 1], w, m, v, name=name, tr=ROWS_PER_STEP["adamw"])

    mix_keys = ("wout", "win", "ws", "cwf", "cwa")
    after = grad_x
    for l in reversed(range(depth)):
        after = arrived(l, "ffn_ex", ("wd", "wup"), after)
        if l > 0:
            after = arrived(l, "mix_ex", mix_keys, after)
    u_wd = big("wd", w_down, m_w_down, v_w_down, "adamw_w_down")
    u_wup = tuple(jnp.swapaxes(a, 1, 2) for a in big("wup", w_up_t, m_w_up_t, v_w_up_t, "adamw_w_up"))
    arrived(0, "mix_ex", mix_keys, u_wup[1])
    u_wout = big("wout", w_out, m_w_out, v_w_out, "adamw_w_out")
    u_win = big("win", w_in, m_w_in, v_w_in, "adamw_w_in")

    def owned(key, l):
        return sum_chunks(me_arr, own[key, l], recv[key, l], name=f"sum_{key}_{l}")

    g_cwf = jnp.stack([owned("cwf", l)[:3] for l in range(depth)])
    g_cwa = jnp.stack([owned("cwa", l)[:3] for l in range(depth)])
    ws_rows = CHUNK * CHUNK // d
    ws_mine = jnp.concatenate([owned("ws", l).reshape(ws_rows, d) for l in range(depth)], axis=0)
    loss_row = jnp.zeros((1, d), F32).at[0, 0].set(loss_tile[0, 0])
    vectors = jnp.concatenate([part[l]["vectors"] for l in range(depth)] + [d_final_g[0:1], loss_row], axis=0)
    vectors = _pad_rows(vectors, -(-vectors.shape[0] // 8) * 8)
    vectors_all, ws_all = all_gather([vectors, ws_mine], u_win[0], name="gather_small_grads")
    vec_sum = sum_devices(vectors_all, name="sum_small", tr=ROWS_PER_STEP["sum"])
    g_ws = jnp.transpose(ws_all.reshape(N_DEV, depth, CHUNK, CHUNK), (1, 0, 2, 3))
    per_layer = part[0]["vectors"].shape[0]
    g_mix, g_ffn, g_lng, g_lnb = (jnp.stack([vec_sum[l * per_layer + k] for l in range(depth)]) for k in range(4))
    g_bs = jnp.stack([vec_sum[l * per_layer + 4].reshape(N_GROUPS, CHUNK) for l in range(depth)])
    g_final = vec_sum[depth * per_layer]
    loss = vec_sum[depth * per_layer + 1, 0]

    def small_update(g, w, m, v, name):
        shape = w.shape
        two_d = (-1, shape[-1]) if w.ndim > 1 else (1, shape[0])
        out = adamw_small(g.reshape(two_d), w.reshape(two_d), m.reshape(two_d), v.reshape(two_d), name=name)
        return (g.reshape(shape),) + tuple(o.reshape(shape) for o in out)

    u_mix = small_update(g_mix, mix_norm_g, m_mix_norm_g, v_mix_norm_g, "adamw_mix_norm_g")
    u_cwa = small_update(g_cwa, conv_a_w, m_conv_a_w, v_conv_a_w, "adamw_conv_a_w")
    u_lng = small_update(g_lng, ln_v_g, m_ln_v_g, v_ln_v_g, "adamw_ln_v_g")
    u_lnb = small_update(g_lnb, ln_v_b, m_ln_v_b, v_ln_v_b, "adamw_ln_v_b")
    u_ws = small_update(g_ws, w_s, m_w_s, v_w_s, "adamw_w_s")
    u_bs = small_update(g_bs, b_s, m_b_s, v_b_s, "adamw_b_s")
    u_ffn = small_update(g_ffn, ffn_norm_g, m_ffn_norm_g, v_ffn_norm_g, "adamw_ffn_norm_g")
    u_cwf = small_update(g_cwf, conv_ffn_w, m_conv_ffn_w, v_conv_ffn_w, "adamw_conv_ffn_w")
    u_final = small_update(g_final, final_norm_g, m_final_norm_g, v_final_norm_g, "adamw_final_norm_g")

    ordered = [u_mix, u_win, u_cwa, u_lng, u_lnb, u_ws, u_bs, u_wout, u_ffn, u_wup, u_cwf, u_wd, u_final]
    return (loss, grad_x, *[u[0] for u in ordered], *[u[1] for u in ordered],
            *[u[2] for u in ordered], *[u[3] for u in ordered])
```

```python
import jax
import jax.numpy as jnp
from jax import lax
from jax.experimental import pallas as pl
from jax.experimental.pallas import tpu as pltpu

EPS = 1e-6
CHUNK = 128
N_GROUPS = 8
N_DEV = 8
HALO = 8
ADAM_LR = 0.001
ADAM_B1 = 0.9
ADAM_B2 = 0.999
ADAM_EPS = 1e-08
ADAM_WD = 0.01
ADAM_STEP = 10
VMEM_LIMIT_BYTES = 56 * 1024 * 1024
TOKENS_PER_STEP = dict(mixer=256, mixer_bwd_alone=512, ffn=256, wgrad=2048, dgrad=1024)
ROWS_PER_STEP = dict(adamw=256, own_slot=256, sum=512)
F32 = jnp.float32
BF16 = jnp.bfloat16
MESH = pl.DeviceIdType.MESH
ANY = pl.BlockSpec(memory_space=pl.ANY)
HBM_SPEC = pl.BlockSpec(memory_space=pltpu.HBM)
SEM_SPEC = pl.BlockSpec(memory_space=pltpu.SEMAPHORE)
DATAFLOW = pltpu.SideEffectType.DATAFLOW_SIDE_EFFECTING
NT_DIMS = (((1,), (1,)), ((), ()))
TN_DIMS = (((0,), (0,)), ((), ()))


def _params(n_grid_axes):
    return pltpu.CompilerParams(dimension_semantics=("arbitrary",) * n_grid_axes,
                                vmem_limit_bytes=VMEM_LIMIT_BYTES)


def _shift_down(cur, prev8, k):
    rolled = pltpu.roll(cur, k, 0)
    prolled = pltpu.roll(prev8, k, 0)
    row = lax.broadcasted_iota(jnp.int32, prev8.shape, 0)
    head = jnp.where(row < k, prolled, rolled[:HALO])
    return jnp.concatenate([head, rolled[HALO:]], axis=0)


def _shift_up(cur, next8, k):
    tm = cur.shape[0]
    rolled = pltpu.roll(cur, tm - k, 0)
    nrolled = pltpu.roll(next8, HALO - k, 0)
    row = lax.broadcasted_iota(jnp.int32, next8.shape, 0)
    tail = jnp.where(row >= HALO - k, nrolled, rolled[tm - HALO:])
    return jnp.concatenate([rolled[:tm - HALO], tail], axis=0)


def _conv_fwd(cur, prev8, cw):
    s1 = _shift_down(cur, prev8, 1)
    s2 = _shift_down(cur, prev8, 2)
    y = s2 * cw[0:1, :] + s1 * cw[1:2, :] + cur * cw[2:3, :]
    return y, s1, s2


def _conv_bwd(d, next8, cw):
    u1 = _shift_up(d, next8, 1)
    u2 = _shift_up(d, next8, 2)
    return d * cw[2:3, :] + u1 * cw[1:2, :] + u2 * cw[0:1, :], u1, u2


def _colsum(a):
    return jnp.sum(a, axis=0, keepdims=True)


def _rms_stats(xv):
    r = lax.rsqrt(jnp.mean(xv * xv, axis=-1, keepdims=True) + EPS)
    return r, xv * r


def _rms_bwd(dh, xv, g):
    r, n = _rms_stats(xv)
    dn = dh * g
    dx = r * (dn - n * jnp.mean(dn * n, axis=-1, keepdims=True))
    return dx, _colsum(dh * n)


def _mixer_forward(p_ref, cprev, xiprev, cw, lng, lnb, ws_ref, bias_ref, mixed_scr, d):
    tm = p_ref.shape[0]
    b = p_ref[:, 0:d]
    c = p_ref[:, d:2 * d]
    xi = p_ref[:, 2 * d:3 * d]
    u = p_ref[:, 3 * d:4 * d]
    v = p_ref[:, 4 * d:5 * d]
    sa = jax.nn.sigmoid(p_ref[:, 5 * d:6 * d])
    sb = jax.nn.sigmoid(p_ref[:, 6 * d:7 * d])
    cx = c * xi
    conv, _, _ = _conv_fwd(cx, cprev * xiprev, cw)
    xc = v - jnp.mean(v, axis=-1, keepdims=True)
    vhat = xc * lax.rsqrt(jnp.mean(xc * xc, axis=-1, keepdims=True) + EPS)
    vnb = (vhat * lng + lnb).astype(BF16)
    tril = (lax.broadcasted_iota(jnp.int32, (CHUNK, CHUNK), 0)
            >= lax.broadcasted_iota(jnp.int32, (CHUNK, CHUNK), 1))
    gd = d // N_GROUPS
    for g in range(N_GROUPS):
        wm = jnp.where(tril, ws_ref[g], 0.0).astype(BF16)
        cols = slice(g * gd, (g + 1) * gd)
        for n in range(tm // CHUNK):
            rows = slice(n * CHUNK, (n + 1) * CHUNK)
            mixed_scr[rows, cols] = (jnp.dot(wm, vnb[rows, cols], preferred_element_type=F32)
                                     + bias_ref[:, cols])
    mixed = mixed_scr[...]
    merged = sa * (b * conv) + sb * (u * mixed)
    return dict(b=b, c=c, xi=xi, u=u, v=v, sa=sa, sb=sb, cx=cx, conv=conv, vnb=vnb, mixed=mixed, merged=merged)


KEPT = ("b", "c", "xi", "u", "v", "conv", "mixed", "vnb", "sa", "sb")


def _once(block_shape, index_map):
    return pl.BlockSpec(block_shape, index_map, pipeline_mode=pl.Buffered(1))


def mixer_fwd(x, g, win, wout, cw, lng, lnb, ws, bias, *, seq, name, tm):
    t, d = x.shape
    nj, _, n = win.shape
    tm = min(tm, seq)
    tiles_per_seq = seq // tm

    def body(x_ref, g_ref, win_ref, wout_ref, cw_ref, lng_ref, lnb_ref, ws_ref, bias_ref,
             h_ref, merged_ref, x1_ref, kept_ref, p_ref, mixed_scr, carry_ref):
        @pl.when(pl.program_id(0) == 0)
        def _():
            carry_ref[...] = jnp.zeros_like(carry_ref)

        keep = jnp.where(pl.program_id(0) % tiles_per_seq == 0, 0.0, 1.0)
        xv = x_ref[...]
        _, nrm = _rms_stats(xv)
        hb = (nrm * g_ref[...]).astype(BF16)
        h_ref[...] = hb
        for j in range(0, nj, 2):
            pair = jnp.concatenate([win_ref[j], win_ref[j + 1]], axis=1)
            p_ref[:, j * n:(j + 2) * n] = jnp.dot(hb, pair, preferred_element_type=F32)
        f = _mixer_forward(p_ref, carry_ref[...] * keep, 1.0, cw_ref[...], lng_ref[...],
                           lnb_ref[...], ws_ref, bias_ref, mixed_scr, d)
        carry_ref[...] = f["cx"][tm - HALO:]
        for k, key in enumerate(KEPT):
            kept_ref[:, k * d:(k + 1) * d] = f[key].astype(BF16)
        mb = f["merged"].astype(BF16)
        merged_ref[...] = mb
        x1_ref[...] = xv + jnp.dot(mb, wout_ref[...], preferred_element_type=F32)

    const2 = lambda i: (0, 0)
    const3 = lambda i: (0, 0, 0)
    row = lambda i: (i, 0)
    return pl.pallas_call(
        body, name=name, grid=(t // tm,),
        in_specs=[pl.BlockSpec((tm, d), row),
                  _once((1, d), const2),
                  _once((nj, d, n), const3),
                  _once((d, d), const2),
                  _once((HALO, d), const2),
                  _once((1, d), const2),
                  _once((1, d), const2),
                  _once((N_GROUPS, CHUNK, CHUNK), const3),
                  _once((CHUNK, d), const2)],
        out_specs=[pl.BlockSpec((tm, d), row), pl.BlockSpec((tm, d), row), pl.BlockSpec((tm, d), row),
                   pl.BlockSpec((tm, len(KEPT) * d), row)],
        out_shape=[jax.ShapeDtypeStruct((t, d), BF16), jax.ShapeDtypeStruct((t, d), BF16),
                   jax.ShapeDtypeStruct((t, d), F32), jax.ShapeDtypeStruct((t, len(KEPT) * d), BF16)],
        scratch_shapes=[pltpu.VMEM((tm, nj * n), F32), pltpu.VMEM((tm, d), F32), pltpu.VMEM((HALO, d), F32)],
        compiler_params=_params(1),
    )(x, g, win, wout, cw, lng, lnb, ws, bias)


def ffn_fwd(x1, g, wup, wd, cw, *, seq, name, tm, head=None):
    t, d = x1.shape
    nj, f, _ = wup.shape
    half = nj // 2
    tm = min(tm, seq)
    tiles_per_seq = seq // tm

    def body(x1_ref, g_ref, wup_ref, wd_ref, cw_ref, *rest):
        if head is None:
            h2_ref, up_ref, fac_ref, act_ref, x2_ref, carry_ref = rest
        else:
            gf_ref, tgt_ref, h2_ref, up_ref, fac_ref, act_ref, x2_ref, dgf_ref, loss_ref, carry_ref = rest

        @pl.when(pl.program_id(0) == 0)
        def _():
            carry_ref[...] = jnp.zeros_like(carry_ref)
            if head is not None:
                dgf_ref[...] = jnp.zeros_like(dgf_ref)
                loss_ref[...] = jnp.zeros_like(loss_ref)

        keep = jnp.where(pl.program_id(0) % tiles_per_seq == 0, 0.0, 1.0)
        xv = x1_ref[...]
        _, nrm = _rms_stats(xv)
        hb = (nrm * g_ref[...]).astype(BF16)
        h2_ref[...] = hb

        for j in range(nj):
            up_ref[j] = lax.dot_general(hb, wup_ref[j], NT_DIMS, preferred_element_type=F32)

        def conv_of(j):
            up0 = up_ref[j]
            y, _, _ = _conv_fwd(up0, carry_ref[j] * keep, cw_ref[j])
            carry_ref[j] = up0[tm - HALO:]
            return y

        acc = xv
        for k in range(half):
            gate, val = conv_of(k), conv_of(k + half)
            sg = jax.nn.sigmoid(gate)
            silu = gate * sg
            fac_ref[k] = (val * (sg * (1.0 + gate * (1.0 - sg)))).astype(BF16)
            fac_ref[k + half] = silu.astype(BF16)
            a = (silu * val).astype(BF16)
            act_ref[k] = a
            acc = acc + jnp.dot(a, wd_ref[k], preferred_element_type=F32)
        if head is None:
            x2_ref[...] = acc
        else:
            gv = gf_ref[...]
            r, n = _rms_stats(acc)
            err = n * gv - tgt_ref[...]
            loss_ref[...] += 0.5 * jnp.sum(jnp.mean(err * err, axis=-1, keepdims=True))
            dy = err * (1.0 / d)
            dn = dy * gv
            x2_ref[...] = r * (dn - n * jnp.mean(dn * n, axis=-1, keepdims=True))
            dgf_ref[0:1, :] += _colsum(dy * n)

    const3 = lambda i: (0, 0, 0)
    row = lambda i: (i, 0)
    in_specs = [pl.BlockSpec((tm, d), row), _once((1, d), lambda i: (0, 0)), _once((nj, f, d), const3),
                _once((half, f, d), const3), _once((nj, HALO, f), const3)]
    out_specs = [pl.BlockSpec((tm, d), row), pl.BlockSpec((nj, tm, f), lambda i: (0, i, 0)),
                 pl.BlockSpec((nj, tm, f), lambda i: (0, i, 0)), pl.BlockSpec((half, tm, f), lambda i: (0, i, 0)),
                 pl.BlockSpec((tm, d), row)]
    out_shape = [jax.ShapeDtypeStruct((t, d), BF16), jax.ShapeDtypeStruct((nj, t, f), F32),
                 jax.ShapeDtypeStruct((nj, t, f), BF16), jax.ShapeDtypeStruct((half, t, f), BF16),
                 jax.ShapeDtypeStruct((t, d), F32)]
    args = [x1, g, wup, wd, cw]
    if head is not None:
        in_specs += [_once((1, d), lambda i: (0, 0)), pl.BlockSpec((tm, d), row)]
        out_specs += [pl.BlockSpec((HALO, d), lambda i: (0, 0)), pl.BlockSpec((8, 128), lambda i: (0, 0))]
        out_shape += [jax.ShapeDtypeStruct((HALO, d), F32), jax.ShapeDtypeStruct((8, 128), F32)]
        args += list(head)
    return pl.pallas_call(
        body, name=name, grid=(t // tm,), in_specs=in_specs, out_specs=out_specs, out_shape=out_shape,
        scratch_shapes=[pltpu.VMEM((nj, HALO, f), F32)],
        compiler_params=_params(1),
    )(*args)


def ffn_bwd(dx2, up0, fac, wd, cw, wup, x1, g, *, seq, name, tm):
    t, d = dx2.shape
    nj, _, f = up0.shape
    half = nj // 2
    tm = min(tm, seq)
    tiles_per_seq = seq // tm
    nt = t // tm

    def body(dx_ref, up_ref, fac_ref, wd_ref, cw_ref, wup_ref, x1_ref, g_ref,
             dup_ref, dcw_ref, dx1_ref, dg_ref, carry_ref):
        i = pl.program_id(0)
        tile = nt - 1 - i

        @pl.when(i == 0)
        def _():
            dcw_ref[...] = jnp.zeros_like(dcw_ref)
            dg_ref[...] = jnp.zeros_like(dg_ref)
            carry_ref[...] = jnp.zeros_like(carry_ref)

        keep_next = jnp.where(tile % tiles_per_seq == tiles_per_seq - 1, 0.0, 1.0)
        dx2v = dx_ref[...]
        dxb = dx2v.astype(BF16)
        dh = [jnp.zeros((tm, d), F32)]

        def through_conv(j, dup):
            next8 = carry_ref[j] * keep_next
            carry_ref[j] = dup[:HALO]
            dup0, u1, u2 = _conv_bwd(dup, next8, cw_ref[j])
            up0 = up_ref[j]
            dcw_ref[j, 0:1, :] += _colsum(u2 * up0)
            dcw_ref[j, 1:2, :] += _colsum(u1 * up0)
            dcw_ref[j, 2:3, :] += _colsum(dup * up0)
            dup0 = dup0.astype(BF16)
            dup_ref[j] = dup0
            dh[0] = dh[0] + jnp.dot(dup0, wup_ref[j], preferred_element_type=F32)

        dacts = [lax.dot_general(dxb, wd_ref[k], NT_DIMS, preferred_element_type=F32) for k in range(half)]
        for k in range(half):
            through_conv(k, dacts[k] * fac_ref[k].astype(F32))
            through_conv(k + half, dacts[k] * fac_ref[k + half].astype(F32))

        dx, dg = _rms_bwd(dh[0], x1_ref[...], g_ref[...])
        dx1_ref[...] = dx2v + dx
        dg_ref[0:1, :] += dg

    rev = lambda i: nt - 1 - i
    return pl.pallas_call(
        body, name=name, grid=(nt,),
        in_specs=[pl.BlockSpec((tm, d), lambda i: (rev(i), 0)),
                  pl.BlockSpec((nj, tm, f), lambda i: (0, rev(i), 0)),
                  pl.BlockSpec((nj, tm, f), lambda i: (0, rev(i), 0)),
                  _once((half, f, d), lambda i: (0, 0, 0)),
                  _once((nj, HALO, f), lambda i: (0, 0, 0)),
                  _once((nj, f, d), lambda i: (0, 0, 0)),
                  pl.BlockSpec((tm, d), lambda i: (rev(i), 0)),
                  _once((1, d), lambda i: (0, 0))],
        out_specs=[pl.BlockSpec((nj, tm, f), lambda i: (0, rev(i), 0)),
                   pl.BlockSpec((nj, HALO, f), lambda i: (0, 0, 0)),
                   pl.BlockSpec((tm, d), lambda i: (rev(i), 0)),
                   pl.BlockSpec((HALO, d), lambda i: (0, 0))],
        out_shape=[jax.ShapeDtypeStruct((nj, t, f), BF16), jax.ShapeDtypeStruct((nj, HALO, f), F32),
                   jax.ShapeDtypeStruct((t, d), F32), jax.ShapeDtypeStruct((HALO, d), F32)],
        scratch_shapes=[pltpu.VMEM((nj, HALO, f), F32)],
        compiler_params=_params(1),
    )(dx2, up0, fac, wd, cw, wup, x1, g)


def mixer_bwd(dx1, kept, wout, cw, lng, wst, *, seq, name, tm, on_to_x0=None):
    t, d = dx1.shape
    tm = min(tm, seq)
    tiles_per_seq = seq // tm
    nt = t // tm
    gd = d // N_GROUPS
    if on_to_x0 is not None:
        nj, _, wn = on_to_x0[0].shape

    def body(dx_ref, k_ref, wout_ref, cw_ref, lng_ref, wst_ref, *rest):
        if on_to_x0 is None:
            dp_ref, dcw_ref, dln_ref, dws_ref, dbs_ref, dvn_scr, carry_ref, dbs_acc = rest
        else:
            (win_ref, x0_ref, g_ref, merged_ref, dp_ref, dcw_ref, dln_ref, dws_ref, dbs_ref, dx0_ref, dg_ref,
             gwout_ref, dvn_scr, carry_ref, dbs_acc, gwout_acc) = rest
        i = pl.program_id(0)
        tile = nt - 1 - i

        @pl.when(i == 0)
        def _():
            dcw_ref[...] = jnp.zeros_like(dcw_ref)
            dln_ref[...] = jnp.zeros_like(dln_ref)
            dws_ref[...] = jnp.zeros_like(dws_ref)
            dbs_acc[...] = jnp.zeros_like(dbs_acc)
            if on_to_x0 is not None:
                dg_ref[...] = jnp.zeros_like(dg_ref)
                gwout_acc[...] = jnp.zeros_like(gwout_acc)
            carry_ref[...] = jnp.zeros_like(carry_ref)

        keep_next = jnp.where(tile % tiles_per_seq == tiles_per_seq - 1, 0.0, 1.0)
        cw = cw_ref[...]
        lng = lng_ref[...]
        kept_f32 = {key: k_ref[:, k * d:(k + 1) * d].astype(F32) for k, key in enumerate(KEPT) if key != "vnb"}
        b, c, xi, u, v, conv, mixed, sa, sb = (kept_f32[key] for key in KEPT if key != "vnb")
        vnb = k_ref[:, KEPT.index("vnb") * d:(KEPT.index("vnb") + 1) * d]
        xc = v - jnp.mean(v, axis=-1, keepdims=True)
        rstd = lax.rsqrt(jnp.mean(xc * xc, axis=-1, keepdims=True) + EPS)
        vhat = xc * rstd
        dxb = dx_ref[...].astype(BF16)
        dmerged = lax.dot_general(dxb, wout_ref[...], NT_DIMS, preferred_element_type=F32)
        if on_to_x0 is not None:
            gwout_acc[...] += lax.dot_general(merged_ref[...], dxb, TN_DIMS, preferred_element_type=F32)
        dp_ref[:, 5 * d:6 * d] = (dmerged * (b * conv) * (sa * (1.0 - sa))).astype(BF16)
        dp_ref[:, 6 * d:7 * d] = (dmerged * (u * mixed) * (sb * (1.0 - sb))).astype(BF16)
        dya = dmerged * sa
        dyb = dmerged * sb
        dp_ref[:, 0:d] = (dya * conv).astype(BF16)
        dconv = dya * b
        next8 = carry_ref[...] * keep_next
        carry_ref[...] = dconv[:HALO]
        dcx, u1, u2 = _conv_bwd(dconv, next8, cw)
        cx = c * xi
        dcw_ref[0:1, :] += _colsum(u2 * cx)
        dcw_ref[1:2, :] += _colsum(u1 * cx)
        dcw_ref[2:3, :] += _colsum(dconv * cx)
        dp_ref[:, d:2 * d] = (dcx * xi).astype(BF16)
        dp_ref[:, 2 * d:3 * d] = (dcx * c).astype(BF16)
        dp_ref[:, 3 * d:4 * d] = (dyb * mixed).astype(BF16)
        dmixed = dyb * u
        dmb = dmixed.astype(BF16)
        tril = (lax.broadcasted_iota(jnp.int32, (CHUNK, CHUNK), 0)
                >= lax.broadcasted_iota(jnp.int32, (CHUNK, CHUNK), 1))
        triu = (lax.broadcasted_iota(jnp.int32, (CHUNK, CHUNK), 0)
                <= lax.broadcasted_iota(jnp.int32, (CHUNK, CHUNK), 1))
        dbs_tile = dmixed[0:CHUNK]
        for n in range(1, tm // CHUNK):
            dbs_tile = dbs_tile + dmixed[n * CHUNK:(n + 1) * CHUNK]
        dbs_acc[...] += dbs_tile
        for g in range(N_GROUPS):
            wmt = jnp.where(triu, wst_ref[g], 0.0).astype(BF16)
            cols = slice(g * gd, (g + 1) * gd)
            dw = jnp.zeros((CHUNK, CHUNK), F32)
            for n in range(tm // CHUNK):
                rows = slice(n * CHUNK, (n + 1) * CHUNK)
                dvn_scr[rows, cols] = jnp.dot(wmt, dmb[rows, cols], preferred_element_type=F32)
                dw = dw + lax.dot_general(dmb[rows, cols], vnb[rows, cols], NT_DIMS, preferred_element_type=F32)
            dws_ref[g] += jnp.where(tril, dw, 0.0)
        dvn = dvn_scr[...]
        dln_ref[0:1, :] += _colsum(dvn * vhat)
        dln_ref[1:2, :] += _colsum(dvn)
        dvh = dvn * lng
        dv = rstd * (dvh - jnp.mean(dvh, axis=-1, keepdims=True)
                     - vhat * jnp.mean(dvh * vhat, axis=-1, keepdims=True))
        dp_ref[:, 4 * d:5 * d] = dv.astype(BF16)
        if on_to_x0 is not None:
            dh = jnp.zeros((tm, d), F32)
            for j in range(0, nj, 2):
                pair = jnp.concatenate([win_ref[j], win_ref[j + 1]], axis=1)
                dh = dh + lax.dot_general(dp_ref[:, j * wn:(j + 2) * wn], pair, NT_DIMS, preferred_element_type=F32)
            dx, dg = _rms_bwd(dh, x0_ref[...], g_ref[...])
            dx0_ref[...] = dx_ref[...] + dx
            dg_ref[0:1, :] += dg

        @pl.when(i == nt - 1)
        def _():
            for g in range(N_GROUPS):
                cols = slice(g * gd, (g + 1) * gd)
                s = jnp.sum(dbs_acc[:, cols], axis=1, keepdims=True)
                dbs_ref[:, cols] = jnp.broadcast_to(s, (CHUNK, gd))
            if on_to_x0 is not None:
                gwout_ref[...] = gwout_acc[...].astype(BF16)

    rev = lambda i: nt - 1 - i
    const2 = lambda i: (0, 0)
    const3 = lambda i: (0, 0, 0)
    row = lambda i: (rev(i), 0)
    in_specs = [pl.BlockSpec((tm, d), row), pl.BlockSpec((tm, len(KEPT) * d), row),
                _once((d, d), const2), _once((HALO, d), const2), _once((1, d), const2),
                _once((N_GROUPS, CHUNK, CHUNK), const3)]
    out_specs = [pl.BlockSpec((tm, 7 * d), row), pl.BlockSpec((HALO, d), const2), pl.BlockSpec((HALO, d), const2),
                 pl.BlockSpec((N_GROUPS, CHUNK, CHUNK), const3), pl.BlockSpec((CHUNK, d), const2)]
    out_shape = [jax.ShapeDtypeStruct((t, 7 * d), BF16), jax.ShapeDtypeStruct((HALO, d), F32),
                 jax.ShapeDtypeStruct((HALO, d), F32), jax.ShapeDtypeStruct((N_GROUPS, CHUNK, CHUNK), F32),
                 jax.ShapeDtypeStruct((CHUNK, d), F32)]
    args = [dx1, kept, wout, cw, lng, wst]
    scratch = [pltpu.VMEM((tm, d), F32), pltpu.VMEM((HALO, d), F32), pltpu.VMEM((CHUNK, d), F32)]
    if on_to_x0 is not None:
        in_specs += [_once((nj, d, wn), const3), pl.BlockSpec((tm, d), row), _once((1, d), const2),
                     pl.BlockSpec((tm, d), row)]
        out_specs += [pl.BlockSpec((tm, d), row), pl.BlockSpec((HALO, d), const2), _once((d, d), const2)]
        out_shape += [jax.ShapeDtypeStruct((t, d), F32), jax.ShapeDtypeStruct((HALO, d), F32),
                      jax.ShapeDtypeStruct((d, d), BF16)]
        args += list(on_to_x0)
        scratch += [pltpu.VMEM((d, d), F32)]
    return pl.pallas_call(
        body, name=name, grid=(nt,), in_specs=in_specs, out_specs=out_specs, out_shape=out_shape,
        scratch_shapes=scratch, compiler_params=_params(1),
    )(*args)


def dgrad_rms(dy, w, x, g, res, *, name, tm):
    t, d = x.shape
    n = w.shape[2]
    w = w.reshape(w.shape[0] // 2, 2, d, n)
    nj = w.shape[0]
    tm = min(tm, t)

    def body(dy_ref, w_ref, x_ref, g_ref, res_ref, dx_ref, dg_ref, acc_ref):
        i, j = pl.program_id(0), pl.program_id(1)

        @pl.when((i == 0) & (j == 0))
        def _():
            dg_ref[...] = jnp.zeros_like(dg_ref)

        pair = jnp.concatenate([w_ref[0], w_ref[1]], axis=1)
        part = lax.dot_general(dy_ref[...], pair, NT_DIMS, preferred_element_type=F32)

        @pl.when(j == 0)
        def _():
            acc_ref[...] = part

        @pl.when(j > 0)
        def _():
            acc_ref[...] += part

        @pl.when(j == nj - 1)
        def _():
            dx, dg = _rms_bwd(acc_ref[...], x_ref[...], g_ref[...])
            dx_ref[...] = res_ref[...] + dx
            dg_ref[0:1, :] += dg

    return pl.pallas_call(
        body, name=name, grid=(t // tm, nj),
        in_specs=[pl.BlockSpec((tm, 2 * n), lambda i, j: (i, j)),
                  pl.BlockSpec((None, 2, d, n), lambda i, j: (j, 0, 0, 0)),
                  pl.BlockSpec((tm, d), lambda i, j: (i, 0)),
                  pl.BlockSpec((1, d), lambda i, j: (0, 0)),
                  pl.BlockSpec((tm, d), lambda i, j: (i, 0))],
        out_specs=[pl.BlockSpec((tm, d), lambda i, j: (i, 0)), pl.BlockSpec((HALO, d), lambda i, j: (0, 0))],
        out_shape=[jax.ShapeDtypeStruct((t, d), F32), jax.ShapeDtypeStruct((HALO, d), F32)],
        scratch_shapes=[pltpu.VMEM((tm, d), F32)],
        compiler_params=_params(2),
    )(dy, w, x, g, res)


def wgrad(a, b, *, nj, a_mode, b_mode, name, tm, split=1):
    def describe(arr, mode):
        if mode == "full":
            return arr.shape[0], arr.shape[1], pl.BlockSpec((tm_, arr.shape[1]), lambda j, s: (s, 0))
        if mode == "cols":
            c = arr.shape[1] // nj
            return arr.shape[0], c, pl.BlockSpec((tm_, c), lambda j, s: (s, j))
        return arr.shape[1], arr.shape[2], pl.BlockSpec((None, tm_, arr.shape[2]), lambda j, s: (j, s, 0))

    t = a.shape[0] if a_mode != "lead" else a.shape[1]
    tm_ = min(tm, t)
    _, k, a_spec = describe(a, a_mode)
    _, n, b_spec = describe(b, b_mode)

    ns = t // tm_
    nc = n // split

    def body(a_ref, b_ref, o_ref, acc_ref):
        s = pl.program_id(1)
        part = lax.dot_general(a_ref[...], b_ref[...], TN_DIMS, preferred_element_type=F32)

        def finish(total):
            for q in range(split):
                o_ref[q] = total[:, q * nc:(q + 1) * nc].astype(BF16)

        if ns == 1:
            finish(part)
            return

        @pl.when(s == 0)
        def _():
            acc_ref[...] = part

        @pl.when((s > 0) & (s < ns - 1))
        def _():
            acc_ref[...] += part

        @pl.when(s == ns - 1)
        def _():
            finish(acc_ref[...] + part)

    return pl.pallas_call(
        body, name=name, grid=(nj, ns),
        in_specs=[a_spec, b_spec],
        out_specs=pl.BlockSpec((split, k, nc), lambda j, s: (j, 0, 0)),
        out_shape=jax.ShapeDtypeStruct((nj * split, k, nc), BF16),
        scratch_shapes=[pltpu.VMEM((k, n), F32)],
        compiler_params=_params(2),
    )(a, b)


def _adamw_math(w, g, m, v):
    m = ADAM_B1 * m + (1.0 - ADAM_B1) * g
    v = ADAM_B2 * v + (1.0 - ADAM_B2) * (g * g)
    m_hat = m / (1.0 - ADAM_B1 ** ADAM_STEP)
    v_hat = v / (1.0 - ADAM_B2 ** ADAM_STEP)
    delta = -ADAM_LR * (m_hat / (jnp.sqrt(v_hat) + ADAM_EPS) + ADAM_WD * w)
    return delta, m, v


def _row_tile(rows, at_most):
    if rows <= at_most:
        return rows
    return max(k for k in range(16, at_most + 1, 16) if rows % k == 0)


def _sum_in_device_order(ref):
    total = ref[0]
    for s in range(1, N_DEV):
        total = total + ref[s]
    return total


def adamw_sharded(me, own0, recv0, own1, recv1, w, m, v, *, name, tr):
    _, r, c = w.shape
    tr = _row_tile(r, tr)
    ni = r // tr

    def body(me_ref, o0_ref, r0_ref, o1_ref, r1_ref, w_ref, m_ref, v_ref, g_ref, d_ref, nm_ref, nv_ref):
        def finish(own_ref, recv_ref):
            g = None
            for s in range(N_DEV):
                term = jnp.where(me_ref[0] == s, own_ref[...], recv_ref[s]).astype(F32)
                g = term if g is None else g + term
            delta, nm, nv = _adamw_math(w_ref[...], g, m_ref[...], v_ref[...])
            g_ref[...] = g
            d_ref[...] = delta
            nm_ref[...] = nm
            nv_ref[...] = nv

        @pl.when(pl.program_id(0) == 0)
        def _():
            finish(o0_ref, r0_ref)

        @pl.when(pl.program_id(0) == 1)
        def _():
            finish(o1_ref, r1_ref)

    row0 = lambda l, i: i * (1 - l) + (ni - 1) * l
    row1 = lambda l, i: i * l
    lay = pl.BlockSpec((None, tr, c), lambda l, i, me_ref: (l, i, 0))
    grid_spec = pltpu.PrefetchScalarGridSpec(
        num_scalar_prefetch=1, grid=(2, ni),
        in_specs=[pl.BlockSpec((None, tr, c), lambda l, i, me_ref: (me_ref[0], row0(l, i), 0)),
                  pl.BlockSpec((N_DEV, tr, c), lambda l, i, me_ref: (0, row0(l, i), 0)),
                  pl.BlockSpec((None, tr, c), lambda l, i, me_ref: (me_ref[0], row1(l, i), 0)),
                  pl.BlockSpec((N_DEV, tr, c), lambda l, i, me_ref: (0, row1(l, i), 0)),
                  lay, lay, lay],
        out_specs=[lay, lay, lay, lay])
    return pl.pallas_call(
        body, name=name, grid_spec=grid_spec,
        out_shape=[jax.ShapeDtypeStruct(w.shape, F32)] * 4,
        compiler_params=_params(2),
    )(me, own0, recv0, own1, recv1, w, m, v)


def sum_chunks(me, own, recv, *, name):
    _, r, c = own.shape

    def body(me_ref, o_ref, r_ref, out_ref):
        total = None
        for s in range(N_DEV):
            term = jnp.where(me_ref[0] == s, o_ref[...], r_ref[s]).astype(F32)
            total = term if total is None else total + term
        out_ref[...] = total

    grid_spec = pltpu.PrefetchScalarGridSpec(
        num_scalar_prefetch=1, grid=(1,),
        in_specs=[pl.BlockSpec((None, r, c), lambda i, me_ref: (me_ref[0], 0, 0)),
                  pl.BlockSpec((N_DEV, r, c), lambda i, me_ref: (0, 0, 0))],
        out_specs=pl.BlockSpec((r, c), lambda i, me_ref: (0, 0)))
    return pl.pallas_call(
        body, name=name, grid_spec=grid_spec,
        out_shape=jax.ShapeDtypeStruct((r, c), F32),
        compiler_params=_params(1),
    )(me, own, recv)


def adamw_small(g, w, m, v, *, name):
    def body(g_ref, w_ref, m_ref, v_ref, d_ref, nm_ref, nv_ref):
        delta, nm, nv = _adamw_math(w_ref[...], g_ref[...], m_ref[...], v_ref[...])
        d_ref[...] = delta
        nm_ref[...] = nm
        nv_ref[...] = nv

    return pl.pallas_call(
        body, name=name,
        out_shape=[jax.ShapeDtypeStruct(w.shape, F32)] * 3,
        compiler_params=pltpu.CompilerParams(vmem_limit_bytes=VMEM_LIMIT_BYTES),
    )(g, w, m, v)


def sum_devices(parts, *, name, tr):
    _, r, c = parts.shape
    tr = min(tr, r)

    def body(p_ref, o_ref):
        o_ref[...] = _sum_in_device_order(p_ref)

    return pl.pallas_call(
        body, name=name, grid=(r // tr,),
        in_specs=[pl.BlockSpec((N_DEV, tr, c), lambda i: (0, i, 0))],
        out_specs=pl.BlockSpec((tr, c), lambda i: (i, 0)),
        out_shape=jax.ShapeDtypeStruct((r, c), F32),
        compiler_params=_params(1),
    )(parts)


def _my_place():
    return lax.axis_index("x"), lax.axis_index("y"), lax.axis_index("c")


def all_gather(arrays, after, *, name):
    n = len(arrays)

    def body(*refs):
        ins, outs = refs[:n], refs[n + 1:2 * n + 1]
        send_sems, recv_sems, local_sems = refs[2 * n + 1:]
        x, y, c = _my_place()
        me, sibling = (x, y, c), (x, y, 1 - c)
        chips = [(1 - x, y), (x, 1 - y), (1 - x, 1 - y)]
        waits = []
        for a in range(n):
            def slot(place, a=a):
                px, py, pc = place
                return outs[a].at[4 * px + 2 * py + pc]

            def copy(k, block, to, src=None, a=a, slot=slot):
                return pltpu.make_async_remote_copy(
                    src_ref=slot(block) if src is None else src, dst_ref=slot(block),
                    send_sem=send_sems.at[a, k], recv_sem=recv_sems.at[a, k],
                    device_id=to, device_id_type=MESH)

            mine = pltpu.make_async_copy(ins[a], slot(me), local_sems.at[a])
            mine.start()
            first = [copy(0, me, sibling, src=ins[a])]
            first += [copy(1 + j, me, (*chip, c), src=ins[a]) for j, chip in enumerate(chips)]
            for cp in first:
                cp.start()
            waits.append((copy, mine, first))
        sends = []
        for a in range(n):
            copy, mine, first = waits[a]
            passed = [copy(4 + j, (*chip, c), sibling) for j, chip in enumerate(chips)]
            for j, chip in enumerate(chips):
                copy(1 + j, (*chip, c), me).wait_recv()
                passed[j].start()
            sends.append(first + passed)
        for a in range(n):
            copy, mine, first = waits[a]
            copy(0, sibling, me).wait_recv()
            for j, chip in enumerate(chips):
                copy(4 + j, (*chip, 1 - c), me).wait_recv()
            for cp in sends[a]:
                cp.wait_send()
            mine.wait()

    return pl.pallas_call(
        body, name=name,
        in_specs=[ANY] * (n + 1), out_specs=[ANY] * n,
        out_shape=[jax.ShapeDtypeStruct((N_DEV,) + a.shape, a.dtype) for a in arrays],
        scratch_shapes=[pltpu.SemaphoreType.DMA((n, 7)), pltpu.SemaphoreType.DMA((n, 7)),
                        pltpu.SemaphoreType.DMA((n,))],
        compiler_params=pltpu.CompilerParams(has_side_effects=True),
    )(*arrays, after)


def _peer_place(r, x, y, c):
    fx, fy, fc = (r >> 2) & 1, (r >> 1) & 1, r & 1
    return (1 - x if fx else x, 1 - y if fy else y, 1 - c if fc else c)


def own_slot(me, w, layer, dtype, *, name, tr):
    _, r, c = w.shape
    tr = _row_tile(r, tr)

    def body(me_ref, w_ref, o_ref):
        o_ref[...] = w_ref[...].astype(dtype)

    grid_spec = pltpu.PrefetchScalarGridSpec(
        num_scalar_prefetch=1, grid=(r // tr,),
        in_specs=[pl.BlockSpec((None, tr, c), lambda i, me_ref: (layer, i, 0))],
        out_specs=pl.BlockSpec((None, tr, c), lambda i, me_ref: (me_ref[0], i, 0)))
    return pl.pallas_call(
        body, name=name, grid_spec=grid_spec,
        out_shape=jax.ShapeDtypeStruct((N_DEV, r, c), dtype),
        compiler_params=_params(1),
    )(me, w)


EXCHANGES = {
    "scatter": [(0, r) for r in range(1, N_DEV)],
    "gather": [(0, r) for r in range(1, N_DEV)],
    "gather_chips": [(0, r) for r in (1, 2, 4, 6)],
    "gather_forward": [(q, 1) for q in (2, 4, 6)],
}


def _split_copy(k, entry, src, land, send_sem, recv_sem, arriving):
    slot, peer = entry
    x, y, c = _my_place()

    def index(relation):
        px, py, pc = _peer_place(relation, x, y, c)
        return 4 * px + 2 * py + pc

    return pltpu.make_async_remote_copy(
        src_ref=land.at[index(slot)] if src is None else src.at[index(peer)],
        dst_ref=land.at[index(slot ^ peer if arriving else slot)],
        send_sem=send_sem.at[k], recv_sem=recv_sem.at[k],
        device_id=_peer_place(peer, x, y, c), device_id_type=MESH)


def start_copies(srcs, lands, *, mode, name, after=None):
    n = len(lands)
    entries = EXCHANGES[mode]
    bufs = (list(srcs) if srcs is not None else []) + list(lands)
    nb = len(bufs)

    def body(*refs):
        src = refs[:n] if srcs is not None else [None] * n
        land = refs[nb - n:nb]
        outs = refs[nb + len(extra):]
        send_sems, recv_sems = outs[:n], outs[n:2 * n]
        token = outs[2 * n + nb]
        for a in range(n):
            for k, entry in enumerate(entries):
                _split_copy(k, entry, src[a], land[a], send_sems[a], recv_sems[a], False).start()
        token[...] = jnp.zeros_like(token)

    extra = [] if after is None else [after]
    outs = pl.pallas_call(
        body, name=name,
        in_specs=[HBM_SPEC] * nb + [ANY] * len(extra),
        out_specs=[SEM_SPEC] * (2 * n) + [HBM_SPEC] * nb + [pl.BlockSpec(memory_space=pltpu.VMEM)],
        out_shape=([pltpu.SemaphoreType.DMA((len(entries),))] * (2 * n)
                   + [pltpu.HBM(a.shape, a.dtype) for a in bufs]
                   + [jax.ShapeDtypeStruct((8, 128), F32)]),
        input_output_aliases={i: 2 * n + i for i in range(nb)},
        compiler_params=pltpu.CompilerParams(has_side_effects=DATAFLOW),
    )(*[pltpu.with_memory_space_constraint(a, pltpu.HBM) for a in bufs], *extra)
    thru = list(outs[2 * n:2 * n + nb])
    return dict(send=outs[:n], recv=outs[n:2 * n], src=thru[:n] if srcs is not None else None, land=thru[nb - n:],
                token=outs[2 * n + nb], mode=mode)


def finish_copies(started, which, after, *, name):
    n = len(which)
    entries = EXCHANGES[started["mode"]]
    has_src = started["src"] is not None
    bufs = ([started["src"][i] for i in which] if has_src else []) + [started["land"][i] for i in which]
    nb = len(bufs)

    def body(*refs):
        src = refs[:n] if has_src else [None] * n
        land = refs[nb - n:nb]
        send_sems, recv_sems = refs[nb:nb + n], refs[nb + n:nb + 2 * n]
        for a in range(n):
            for k, entry in enumerate(entries):
                cp = _split_copy(k, entry, src[a], land[a], send_sems[a], recv_sems[a], True)
                cp.wait_send()
                cp.wait_recv()

    outs = pl.pallas_call(
        body, name=name,
        in_specs=[HBM_SPEC] * nb + [SEM_SPEC] * (2 * n) + [ANY],
        out_specs=[HBM_SPEC] * nb,
        out_shape=[pltpu.HBM(a.shape, a.dtype) for a in bufs],
        input_output_aliases={i: i for i in range(nb)},
        compiler_params=pltpu.CompilerParams(has_side_effects=DATAFLOW),
    )(*bufs, *[started["send"][i] for i in which], *[started["recv"][i] for i in which], after)
    return (list(outs[:n]) if has_src else None), list(outs[nb - n:])


def _pad_rows(a, rows):
    pad = [(0, 0)] * a.ndim
    pad[-2] = (0, rows - a.shape[-2])
    return jnp.pad(a, pad)


def kernel(x, mix_norm_g, w_in, conv_a_w, ln_v_g, ln_v_b, w_s, b_s, w_out, ffn_norm_g, w_up, conv_ffn_w, w_down, final_norm_g, loss_target, m_mix_norm_g, m_w_in, m_conv_a_w, m_ln_v_g, m_ln_v_b, m_w_s, m_b_s, m_w_out, m_ffn_norm_g, m_w_up, m_conv_ffn_w, m_w_down, m_final_norm_g, v_mix_norm_g, v_w_in, v_conv_a_w, v_ln_v_g, v_ln_v_b, v_w_s, v_b_s, v_w_out, v_ffn_norm_g, v_w_up, v_conv_ffn_w, v_w_down, v_final_norm_g):
    nb, seq, d = x.shape
    t = nb * seq
    depth = w_in.shape[0]
    f = w_up.shape[2]
    me = 4 * lax.axis_index("x") + 2 * lax.axis_index("y") + lax.axis_index("c")
    xt = x.reshape(t, d)
    tgt = loss_target.reshape(t, d)

    conv_pack = jnp.concatenate([_pad_rows(conv_a_w, HALO), _pad_rows(conv_ffn_w, HALO)], axis=-1)
    me_arr = me.astype(jnp.int32).reshape(1)
    w_up_t, m_w_up_t, v_w_up_t = (jnp.swapaxes(a, 1, 2) for a in (w_up, m_w_up, v_w_up))
    zones, slot_of = [], {}
    for l in range(depth):
        for key, w in (("win", w_in), ("conv", None), ("wout", w_out), ("wup", w_up_t), ("wd", w_down)):
            if key == "conv":
                if l == 0:
                    slot_of["conv"] = len(zones)
                    packed = conv_pack.reshape(1, depth * HALO, conv_pack.shape[-1])
                    zones.append(own_slot(me_arr, packed, 0, F32, name="own_slot_conv", tr=ROWS_PER_STEP["own_slot"]))
                continue
            slot_of[key, l] = len(zones)
            zones.append(own_slot(me_arr, w, l, BF16, name=f"own_slot_{key}_{l}", tr=ROWS_PER_STEP["own_slot"]))
    first = [slot_of["win", 0], slot_of["wout", 0], slot_of["conv"]]
    rest = [i for i in range(len(zones)) if i not in first]
    to_chips = start_copies(None, [zones[i] for i in first], mode="gather_chips", name="gather_first_chips")
    gathering = start_copies(None, [zones[i] for i in rest], mode="gather", name="gather_start", after=to_chips["token"])
    _, at_chips = finish_copies(to_chips, [0, 1, 2], gathering["token"], name="wait_first_chips")
    to_sibling = start_copies(None, at_chips, mode="gather_forward", name="gather_first_forward")

    def gathered(keys, after, name):
        return finish_copies(gathering, [rest.index(slot_of[k]) for k in keys], after, name=name)[1]

    saved, layers = [], []
    cur = xt
    for l in range(depth):
        p = dict(mix_g=mix_norm_g[l][None], ffn_g=ffn_norm_g[l][None], lng=ln_v_g[l][None], lnb=ln_v_b[l][None],
                 ws=w_s[l], wst=jnp.swapaxes(w_s[l], 1, 2),
                 bias=jnp.repeat(b_s[l].T, d // N_GROUPS, axis=1))
        if l == 0:
            _, (p["win"], wout_g, conv_g) = finish_copies(to_sibling, [0, 1, 2], to_sibling["token"],
                                                          name=f"wait_w_mixer_{l}")
            conv_g = conv_g.reshape(N_DEV, depth, HALO, -1)
            ca = conv_g.shape[-1] - f
        else:
            p["win"], wout_g = gathered([("win", l), ("wout", l)], after, f"wait_w_mixer_{l}")
        p["wout"] = wout_g.reshape(d, d)
        p["cw_a"] = jnp.transpose(conv_g[:, l, :, :ca], (1, 0, 2)).reshape(HALO, d)
        p["cw_f"] = conv_g[:, l, :, ca:]
        h, merged, x1, kept = mixer_fwd(cur, p["mix_g"], p["win"], p["wout"], p["cw_a"], p["lng"], p["lnb"],
                                        p["ws"], p["bias"], seq=seq, name=f"mixer_fwd_{l}", tm=TOKENS_PER_STEP["mixer"])
        p["wup"], wd_g = gathered([("wup", l), ("wd", l)], merged, f"wait_w_ffn_{l}")
        p["wd"] = wd_g.reshape(N_DEV // 2, 2 * wd_g.shape[1], d)
        head = (final_norm_g[None], tgt) if l == depth - 1 else None
        h2, up0, fac, act, x2, *of_loss = ffn_fwd(x1, p["ffn_g"], p["wup"], p["wd"], p["cw_f"],
                                                  seq=seq, name=f"ffn_fwd_{l}", tm=TOKENS_PER_STEP["ffn"], head=head)
        saved.append(dict(x0=cur, h=h, kept=kept, merged=merged, x1=x1, h2=h2, up0=up0, fac=fac, act=act))
        layers.append(p)
        cur, after = x2, act
    dx = cur
    d_final_g, loss_tile = of_loss

    def exchange(parts, name):
        return start_copies(parts, [lax.empty(a.shape, a.dtype) for a in parts], mode="scatter", name=name)

    def tied(g, started):
        return g + started["token"][0:1, 0:1]

    part = [None] * depth
    mix_ex = None
    for l in reversed(range(depth)):
        p, s = layers[l], saved[l]
        ffn_g = p["ffn_g"] if mix_ex is None else tied(p["ffn_g"], mix_ex)
        dup0, dcw_f, dx1, d_ffn_g = ffn_bwd(dx, s["up0"], s["fac"], p["wd"], p["cw_f"], p["wup"], s["x1"], ffn_g,
                                            seq=seq, name=f"ffn_bwd_{l}", tm=TOKENS_PER_STEP["ffn"])
        g_wd = wgrad(s["act"], dx, nj=N_DEV // 2, a_mode="lead", b_mode="full", name=f"wgrad_down_{l}", tm=TOKENS_PER_STEP["wgrad"])
        g_wup = wgrad(dup0, s["h2"], nj=N_DEV, a_mode="lead", b_mode="full", name=f"wgrad_up_{l}", tm=TOKENS_PER_STEP["wgrad"])
        ffn_ex = exchange([g_wd.reshape(N_DEV, g_wd.shape[1] // 2, d), g_wup], f"exchange_ffn_{l}")
        fused = l > 0
        dproj, dcw_a, dln, dws, dbs, *to_x0 = mixer_bwd(
            dx1, s["kept"], p["wout"], tied(p["cw_a"], ffn_ex), p["lng"], p["wst"],
            seq=seq, name=f"mixer_bwd_{l}", tm=TOKENS_PER_STEP["mixer" if fused else "mixer_bwd_alone"], on_to_x0=(p["win"], s["x0"], p["mix_g"], s["merged"]) if fused else None)
        if fused:
            g_wout = to_x0.pop()
        else:
            g_wout = wgrad(s["merged"], dx1, nj=1, a_mode="full", b_mode="full", name=f"wgrad_out_{l}", tm=TOKENS_PER_STEP["wgrad"])
        g_win = wgrad(s["h"], dproj, nj=N_DEV // 2, a_mode="full", b_mode="cols", name=f"wgrad_in_{l}", tm=TOKENS_PER_STEP["wgrad"], split=2)
        cwa_chunks = jnp.transpose(dcw_a.reshape(HALO, N_DEV, d // N_DEV), (1, 0, 2))
        mix_ex = exchange([g_wout.reshape(N_DEV, d // N_DEV, d), g_win, dws, dcw_f, cwa_chunks], f"exchange_mix_{l}")
        if fused:
            dx, d_mix_g = to_x0
        else:
            dx, d_mix_g = dgrad_rms(dproj, p["win"], s["x0"], tied(p["mix_g"], mix_ex), dx1,
                                    name=f"dgrad_in_{l}", tm=TOKENS_PER_STEP["dgrad"])
        part[l] = dict(
            ffn_ex=ffn_ex, mix_ex=mix_ex,
            vectors=jnp.concatenate([d_mix_g[0:1], d_ffn_g[0:1], dln[0:2],
                                     dbs[:, ::d // N_GROUPS].T.reshape(1, d)], axis=0))
    grad_x = dx.reshape(nb, seq, d)

    own, recv = {}, {}

    def arrived(l, ex, keys, after):
        srcs, lands = finish_copies(part[l][ex], list(range(len(keys))), after, name=f"wait_{ex}_{l}")
        for k, key in enumerate(keys):
            own[key, l], recv[key, l] = srcs[k], lands[k]
        return lands[1]

    def big(key, w, m, v, name):
        return adamw_sharded(me_arr, own[key, 0], recv[key, 0], own[key, 1], recv[key, 1], w, m, v, name=name, tr=ROWS_PER_STEP["adamw"])

    mix_keys = ("wout", "win", "ws", "cwf", "cwa")
    after = grad_x
    for l in reversed(range(depth)):
        after = arrived(l, "ffn_ex", ("wd", "wup"), after)
        if l > 0:
            after = arrived(l, "mix_ex", mix_keys, after)
    u_wd = big("wd", w_down, m_w_down, v_w_down, "adamw_w_down")
    u_wup = tuple(jnp.swapaxes(a, 1, 2) for a in big("wup", w_up_t, m_w_up_t, v_w_up_t, "adamw_w_up"))
    arrived(0, "mix_ex", mix_keys, u_wup[1])
    u_wout = big("wout", w_out, m_w_out, v_w_out, "adamw_w_out")
    u_win = big("win", w_in, m_w_in, v_w_in, "adamw_w_in")

    def owned(key, l):
        return sum_chunks(me_arr, own[key, l], recv[key, l], name=f"sum_{key}_{l}")

    g_cwf = jnp.stack([owned("cwf", l)[:3] for l in range(depth)])
    g_cwa = jnp.stack([owned("cwa", l)[:3] for l in range(depth)])
    ws_rows = CHUNK * CHUNK // d
    ws_mine = jnp.concatenate([owned("ws", l).reshape(ws_rows, d) for l in range(depth)], axis=0)
    loss_row = jnp.zeros((1, d), F32).at[0, 0].set(loss_tile[0, 0])
    vectors = jnp.concatenate([part[l]["vectors"] for l in range(depth)] + [d_final_g[0:1], loss_row], axis=0)
    vectors = _pad_rows(vectors, -(-vectors.shape[0] // 8) * 8)
    vectors_all, ws_all = all_gather([vectors, ws_mine], u_win[0], name="gather_small_grads")
    vec_sum = sum_devices(vectors_all, name="sum_small", tr=ROWS_PER_STEP["sum"])
    g_ws = jnp.transpose(ws_all.reshape(N_DEV, depth, CHUNK, CHUNK), (1, 0, 2, 3))
    per_layer = part[0]["vectors"].shape[0]
    g_mix, g_ffn, g_lng, g_lnb = (jnp.stack([vec_sum[l * per_layer + k] for l in range(depth)]) for k in range(4))
    g_bs = jnp.stack([vec_sum[l * per_layer + 4].reshape(N_GROUPS, CHUNK) for l in range(depth)])
    g_final = vec_sum[depth * per_layer]
    loss = vec_sum[depth * per_layer + 1, 0]

    def small_update(g, w, m, v, name):
        shape = w.shape
        two_d = (-1, shape[-1]) if w.ndim > 1 else (1, shape[0])
        out = adamw_small(g.reshape(two_d), w.reshape(two_d), m.reshape(two_d), v.reshape(two_d), name=name)
        return (g.reshape(shape),) + tuple(o.reshape(shape) for o in out)

    u_mix = small_update(g_mix, mix_norm_g, m_mix_norm_g, v_mix_norm_g, "adamw_mix_norm_g")
    u_cwa = small_update(g_cwa, conv_a_w, m_conv_a_w, v_conv_a_w, "adamw_conv_a_w")
    u_lng = small_update(g_lng, ln_v_g, m_ln_v_g, v_ln_v_g, "adamw_ln_v_g")
    u_lnb = small_update(g_lnb, ln_v_b, m_ln_v_b, v_ln_v_b, "adamw_ln_v_b")
    u_ws = small_update(g_ws, w_s, m_w_s, v_w_s, "adamw_w_s")
    u_bs = small_update(g_bs, b_s, m_b_s, v_b_s, "adamw_b_s")
    u_ffn = small_update(g_ffn, ffn_norm_g, m_ffn_norm_g, v_ffn_norm_g, "adamw_ffn_norm_g")
    u_cwf = small_update(g_cwf, conv_ffn_w, m_conv_ffn_w, v_conv_ffn_w, "adamw_conv_ffn_w")
    u_final = small_update(g_final, final_norm_g, m_final_norm_g, v_final_norm_g, "adamw_final_norm_g")

    ordered = [u_mix, u_win, u_cwa, u_lng, u_lnb, u_ws, u_bs, u_wout, u_ffn, u_wup, u_cwf, u_wd, u_final]
    return (loss, grad_x, *[u[0] for u in ordered], *[u[1] for u in ordered],
            *[u[2] for u in ordered], *[u[3] for u in ordered])
```

```python
import jax
import jax.numpy as jnp
from jax import lax
from jax.experimental import pallas as pl
from jax.experimental.pallas import tpu as pltpu

EPS = 1e-6
CHUNK = 128
N_GROUPS = 8
N_DEV = 8
HALO = 8
ADAM_LR = 0.001
ADAM_B1 = 0.9
ADAM_B2 = 0.999
ADAM_EPS = 1e-08
ADAM_WD = 0.01
ADAM_STEP = 10
VMEM_LIMIT_BYTES = 56 * 1024 * 1024
TOKENS_PER_STEP = dict(mixer=256, mixer_bwd_alone=512, ffn=256, wgrad=2048, dgrad=1024)
ROWS_PER_STEP = dict(adamw=256, own_slot=256, sum=512)
F32 = jnp.float32
BF16 = jnp.bfloat16
MESH = pl.DeviceIdType.MESH
ANY = pl.BlockSpec(memory_space=pl.ANY)
HBM_SPEC = pl.BlockSpec(memory_space=pltpu.HBM)
SEM_SPEC = pl.BlockSpec(memory_space=pltpu.SEMAPHORE)
DATAFLOW = pltpu.SideEffectType.DATAFLOW_SIDE_EFFECTING
NT_DIMS = (((1,), (1,)), ((), ()))
TN_DIMS = (((0,), (0,)), ((), ()))


def _params(n_grid_axes):
    return pltpu.CompilerParams(dimension_semantics=("arbitrary",) * n_grid_axes,
                                vmem_limit_bytes=VMEM_LIMIT_BYTES)


def _shift_down(cur, prev8, k):
    rolled = pltpu.roll(cur, k, 0)
    prolled = pltpu.roll(prev8, k, 0)
    row = lax.broadcasted_iota(jnp.int32, prev8.shape, 0)
    head = jnp.where(row < k, prolled, rolled[:HALO])
    return jnp.concatenate([head, rolled[HALO:]], axis=0)


def _shift_up(cur, next8, k):
    tm = cur.shape[0]
    rolled = pltpu.roll(cur, tm - k, 0)
    nrolled = pltpu.roll(next8, HALO - k, 0)
    row = lax.broadcasted_iota(jnp.int32, next8.shape, 0)
    tail = jnp.where(row >= HALO - k, nrolled, rolled[tm - HALO:])
    return jnp.concatenate([rolled[:tm - HALO], tail], axis=0)


def _conv_fwd(cur, prev8, cw):
    s1 = _shift_down(cur, prev8, 1)
    s2 = _shift_down(cur, prev8, 2)
    y = s2 * cw[0:1, :] + s1 * cw[1:2, :] + cur * cw[2:3, :]
    return y, s1, s2


def _conv_bwd(d, next8, cw):
    u1 = _shift_up(d, next8, 1)
    u2 = _shift_up(d, next8, 2)
    return d * cw[2:3, :] + u1 * cw[1:2, :] + u2 * cw[0:1, :], u1, u2


def _colsum(a):
    return jnp.sum(a, axis=0, keepdims=True)


def _rms_stats(xv):
    r = lax.rsqrt(jnp.mean(xv * xv, axis=-1, keepdims=True) + EPS)
    return r, xv * r


def _rms_bwd(dh, xv, g):
    r, n = _rms_stats(xv)
    dn = dh * g
    dx = r * (dn - n * jnp.mean(dn * n, axis=-1, keepdims=True))
    return dx, _colsum(dh * n)


def _mixer_forward(p_ref, cprev, xiprev, cw, lng, lnb, ws_ref, bias_ref, mixed_scr, d):
    tm = p_ref.shape[0]
    b = p_ref[:, 0:d]
    c = p_ref[:, d:2 * d]
    xi = p_ref[:, 2 * d:3 * d]
    u = p_ref[:, 3 * d:4 * d]
    v = p_ref[:, 4 * d:5 * d]
    sa = jax.nn.sigmoid(p_ref[:, 5 * d:6 * d])
    sb = jax.nn.sigmoid(p_ref[:, 6 * d:7 * d])
    cx = c * xi
    conv, _, _ = _conv_fwd(cx, cprev * xiprev, cw)
    xc = v - jnp.mean(v, axis=-1, keepdims=True)
    vhat = xc * lax.rsqrt(jnp.mean(xc * xc, axis=-1, keepdims=True) + EPS)
    vnb = (vhat * lng + lnb).astype(BF16)
    tril = (lax.broadcasted_iota(jnp.int32, (CHUNK, CHUNK), 0)
            >= lax.broadcasted_iota(jnp.int32, (CHUNK, CHUNK), 1))
    gd = d // N_GROUPS
    for g in range(N_GROUPS):
        wm = jnp.where(tril, ws_ref[g], 0.0).astype(BF16)
        cols = slice(g * gd, (g + 1) * gd)
        for n in range(tm // CHUNK):
            rows = slice(n * CHUNK, (n + 1) * CHUNK)
            mixed_scr[rows, cols] = (jnp.dot(wm, vnb[rows, cols], preferred_element_type=F32)
                                     + bias_ref[:, cols])
    mixed = mixed_scr[...]
    merged = sa * (b * conv) + sb * (u * mixed)
    return dict(b=b, c=c, xi=xi, u=u, v=v, sa=sa, sb=sb, cx=cx, conv=conv, vnb=vnb, mixed=mixed, merged=merged)


KEPT = ("b", "c", "xi", "u", "v", "conv", "mixed", "vnb", "sa", "sb")


def _once(block_shape, index_map):
    return pl.BlockSpec(block_shape, index_map, pipeline_mode=pl.Buffered(1))


def mixer_fwd(x, g, win, wout, cw, lng, lnb, ws, bias, *, seq, name, tm):
    t, d = x.shape
    nj, _, n = win.shape
    tm = min(tm, seq)
    tiles_per_seq = seq // tm

    def body(x_ref, g_ref, win_ref, wout_ref, cw_ref, lng_ref, lnb_ref, ws_ref, bias_ref,
             h_ref, merged_ref, x1_ref, kept_ref, p_ref, mixed_scr, carry_ref):
        @pl.when(pl.program_id(0) == 0)
        def _():
            carry_ref[...] = jnp.zeros_like(carry_ref)

        keep = jnp.where(pl.program_id(0) % tiles_per_seq == 0, 0.0, 1.0)
        xv = x_ref[...]
        _, nrm = _rms_stats(xv)
        hb = (nrm * g_ref[...]).astype(BF16)
        h_ref[...] = hb
        for j in range(0, nj, 2):
            pair = jnp.concatenate([win_ref[j], win_ref[j + 1]], axis=1)
            p_ref[:, j * n:(j + 2) * n] = jnp.dot(hb, pair, preferred_element_type=F32)
        f = _mixer_forward(p_ref, carry_ref[...] * keep, 1.0, cw_ref[...], lng_ref[...],
                           lnb_ref[...], ws_ref, bias_ref, mixed_scr, d)
        carry_ref[...] = f["cx"][tm - HALO:]
        for k, key in enumerate(KEPT):
            kept_ref[:, k * d:(k + 1) * d] = f[key].astype(BF16)
        mb = f["merged"].astype(BF16)
        merged_ref[...] = mb
        x1_ref[...] = xv + jnp.dot(mb, wout_ref[...], preferred_element_type=F32)

    const2 = lambda i: (0, 0)
    const3 = lambda i: (0, 0, 0)
    row = lambda i: (i, 0)
    return pl.pallas_call(
        body, name=name, grid=(t // tm,),
        in_specs=[pl.BlockSpec((tm, d), row),
                  _once((1, d), const2),
                  _once((nj, d, n), const3),
                  _once((d, d), const2),
                  _once((HALO, d), const2),
                  _once((1, d), const2),
                  _once((1, d), const2),
                  _once((N_GROUPS, CHUNK, CHUNK), const3),
                  _once((CHUNK, d), const2)],
        out_specs=[pl.BlockSpec((tm, d), row), pl.BlockSpec((tm, d), row), pl.BlockSpec((tm, d), row),
                   pl.BlockSpec((tm, len(KEPT) * d), row)],
        out_shape=[jax.ShapeDtypeStruct((t, d), BF16), jax.ShapeDtypeStruct((t, d), BF16),
                   jax.ShapeDtypeStruct((t, d), F32), jax.ShapeDtypeStruct((t, len(KEPT) * d), BF16)],
        scratch_shapes=[pltpu.VMEM((tm, nj * n), F32), pltpu.VMEM((tm, d), F32), pltpu.VMEM((HALO, d), F32)],
        compiler_params=_params(1),
    )(x, g, win, wout, cw, lng, lnb, ws, bias)


def ffn_fwd(x1, g, wup, wd, cw, *, seq, name, tm, head=None):
    t, d = x1.shape
    nj, f, _ = wup.shape
    half = nj // 2
    tm = min(tm, seq)
    tiles_per_seq = seq // tm

    def body(x1_ref, g_ref, wup_ref, wd_ref, cw_ref, *rest):
        if head is None:
            h2_ref, up_ref, fac_ref, act_ref, x2_ref, carry_ref = rest
        else:
            gf_ref, tgt_ref, h2_ref, up_ref, fac_ref, act_ref, x2_ref, dgf_ref, loss_ref, carry_ref = rest

        @pl.when(pl.program_id(0) == 0)
        def _():
            carry_ref[...] = jnp.zeros_like(carry_ref)
            if head is not None:
                dgf_ref[...] = jnp.zeros_like(dgf_ref)
                loss_ref[...] = jnp.zeros_like(loss_ref)

        keep = jnp.where(pl.program_id(0) % tiles_per_seq == 0, 0.0, 1.0)
        xv = x1_ref[...]
        _, nrm = _rms_stats(xv)
        hb = (nrm * g_ref[...]).astype(BF16)
        h2_ref[...] = hb

        for j in range(nj):
            up_ref[j] = lax.dot_general(hb, wup_ref[j], NT_DIMS, preferred_element_type=F32)

        def conv_of(j):
            up0 = up_ref[j]
            y, _, _ = _conv_fwd(up0, carry_ref[j] * keep, cw_ref[j])
            carry_ref[j] = up0[tm - HALO:]
            return y

        acc = xv
        for k in range(half):
            gate, val = conv_of(k), conv_of(k + half)
            sg = jax.nn.sigmoid(gate)
            silu = gate * sg
            fac_ref[k] = (val * (sg * (1.0 + gate * (1.0 - sg)))).astype(BF16)
            fac_ref[k + half] = silu.astype(BF16)
            a = (silu * val).astype(BF16)
            act_ref[k] = a
            acc = acc + jnp.dot(a, wd_ref[k], preferred_element_type=F32)
        if head is None:
            x2_ref[...] = acc
        else:
            gv = gf_ref[...]
            r, n = _rms_stats(acc)
            err = n * gv - tgt_ref[...]
            loss_ref[...] += 0.5 * jnp.sum(jnp.mean(err * err, axis=-1, keepdims=True))
            dy = err * (1.0 / d)
            dn = dy * gv
            x2_ref[...] = r * (dn - n * jnp.mean(dn * n, axis=-1, keepdims=True))
            dgf_ref[0:1, :] += _colsum(dy * n)

    const3 = lambda i: (0, 0, 0)
    row = lambda i: (i, 0)
    in_specs = [pl.BlockSpec((tm, d), row), _once((1, d), lambda i: (0, 0)), _once((nj, f, d), const3),
                _once((half, f, d), const3), _once((nj, HALO, f), const3)]
    out_specs = [pl.BlockSpec((tm, d), row), pl.BlockSpec((nj, tm, f), lambda i: (0, i, 0)),
                 pl.BlockSpec((nj, tm, f), lambda i: (0, i, 0)), pl.BlockSpec((half, tm, f), lambda i: (0, i, 0)),
                 pl.BlockSpec((tm, d), row)]
    out_shape = [jax.ShapeDtypeStruct((t, d), BF16), jax.ShapeDtypeStruct((nj, t, f), F32),
                 jax.ShapeDtypeStruct((nj, t, f), BF16), jax.ShapeDtypeStruct((half, t, f), BF16),
                 jax.ShapeDtypeStruct((t, d), F32)]
    args = [x1, g, wup, wd, cw]
    if head is not None:
        in_specs += [_once((1, d), lambda i: (0, 0)), pl.BlockSpec((tm, d), row)]
        out_specs += [pl.BlockSpec((HALO, d), lambda i: (0, 0)), pl.BlockSpec((8, 128), lambda i: (0, 0))]
        out_shape += [jax.ShapeDtypeStruct((HALO, d), F32), jax.ShapeDtypeStruct((8, 128), F32)]
        args += list(head)
    return pl.pallas_call(
        body, name=name, grid=(t // tm,), in_specs=in_specs, out_specs=out_specs, out_shape=out_shape,
        scratch_shapes=[pltpu.VMEM((nj, HALO, f), F32)],
        compiler_params=_params(1),
    )(*args)


def ffn_bwd(dx2, up0, fac, wd, cw, wup, x1, g, *, seq, name, tm):
    t, d = dx2.shape
    nj, _, f = up0.shape
    half = nj // 2
    tm = min(tm, seq)
    tiles_per_seq = seq // tm
    nt = t // tm

    def body(dx_ref, up_ref, fac_ref, wd_ref, cw_ref, wup_ref, x1_ref, g_ref,
             dup_ref, dcw_ref, dx1_ref, dg_ref, carry_ref):
        i = pl.program_id(0)
        tile = nt - 1 - i

        @pl.when(i == 0)
        def _():
            dcw_ref[...] = jnp.zeros_like(dcw_ref)
            dg_ref[...] = jnp.zeros_like(dg_ref)
            carry_ref[...] = jnp.zeros_like(carry_ref)

        keep_next = jnp.where(tile % tiles_per_seq == tiles_per_seq - 1, 0.0, 1.0)
        dx2v = dx_ref[...]
        dxb = dx2v.astype(BF16)
        dh = [jnp.zeros((tm, d), F32)]

        def through_conv(j, dup):
            next8 = carry_ref[j] * keep_next
            carry_ref[j] = dup[:HALO]
            dup0, u1, u2 = _conv_bwd(dup, next8, cw_ref[j])
            up0 = up_ref[j]
            dcw_ref[j, 0:1, :] += _colsum(u2 * up0)
            dcw_ref[j, 1:2, :] += _colsum(u1 * up0)
            dcw_ref[j, 2:3, :] += _colsum(dup * up0)
            dup0 = dup0.astype(BF16)
            dup_ref[j] = dup0
            dh[0] = dh[0] + jnp.dot(dup0, wup_ref[j], preferred_element_type=F32)

        dacts = [lax.dot_general(dxb, wd_ref[k], NT_DIMS, preferred_element_type=F32) for k in range(half)]
        for k in range(half):
            through_conv(k, dacts[k] * fac_ref[k].astype(F32))
            through_conv(k + half, dacts[k] * fac_ref[k + half].astype(F32))

        dx, dg = _rms_bwd(dh[0], x1_ref[...], g_ref[...])
        dx1_ref[...] = dx2v + dx
        dg_ref[0:1, :] += dg

    rev = lambda i: nt - 1 - i
    return pl.pallas_call(
        body, name=name, grid=(nt,),
        in_specs=[pl.BlockSpec((tm, d), lambda i: (rev(i), 0)),
                  pl.BlockSpec((nj, tm, f), lambda i: (0, rev(i), 0)),
                  pl.BlockSpec((nj, tm, f), lambda i: (0, rev(i), 0)),
                  _once((half, f, d), lambda i: (0, 0, 0)),
                  _once((nj, HALO, f), lambda i: (0, 0, 0)),
                  _once((nj, f, d), lambda i: (0, 0, 0)),
                  pl.BlockSpec((tm, d), lambda i: (rev(i), 0)),
                  _once((1, d), lambda i: (0, 0))],
        out_specs=[pl.BlockSpec((nj, tm, f), lambda i: (0, rev(i), 0)),
                   pl.BlockSpec((nj, HALO, f), lambda i: (0, 0, 0)),
                   pl.BlockSpec((tm, d), lambda i: (rev(i), 0)),
                   pl.BlockSpec((HALO, d), lambda i: (0, 0))],
        out_shape=[jax.ShapeDtypeStruct((nj, t, f), BF16), jax.ShapeDtypeStruct((nj, HALO, f), F32),
                   jax.ShapeDtypeStruct((t, d), F32), jax.ShapeDtypeStruct((HALO, d), F32)],
        scratch_shapes=[pltpu.VMEM((nj, HALO, f), F32)],
        compiler_params=_params(1),
    )(dx2, up0, fac, wd, cw, wup, x1, g)


def mixer_bwd(dx1, kept, wout, cw, lng, wst, *, seq, name, tm, on_to_x0=None):
    t, d = dx1.shape
    tm = min(tm, seq)
    tiles_per_seq = seq // tm
    nt = t // tm
    gd = d // N_GROUPS
    if on_to_x0 is not None:
        nj, _, wn = on_to_x0[0].shape

    def body(dx_ref, k_ref, wout_ref, cw_ref, lng_ref, wst_ref, *rest):
        if on_to_x0 is None:
            dp_ref, dcw_ref, dln_ref, dws_ref, dbs_ref, dvn_scr, carry_ref, dbs_acc = rest
        else:
            (win_ref, x0_ref, g_ref, dp_ref, dcw_ref, dln_ref, dws_ref, dbs_ref, dx0_ref, dg_ref,
             dvn_scr, carry_ref, dbs_acc) = rest
        i = pl.program_id(0)
        tile = nt - 1 - i

        @pl.when(i == 0)
        def _():
            dcw_ref[...] = jnp.zeros_like(dcw_ref)
            dln_ref[...] = jnp.zeros_like(dln_ref)
            dws_ref[...] = jnp.zeros_like(dws_ref)
            dbs_acc[...] = jnp.zeros_like(dbs_acc)
            if on_to_x0 is not None:
                dg_ref[...] = jnp.zeros_like(dg_ref)
            carry_ref[...] = jnp.zeros_like(carry_ref)

        keep_next = jnp.where(tile % tiles_per_seq == tiles_per_seq - 1, 0.0, 1.0)
        cw = cw_ref[...]
        lng = lng_ref[...]
        kept_f32 = {key: k_ref[:, k * d:(k + 1) * d].astype(F32) for k, key in enumerate(KEPT) if key != "vnb"}
        b, c, xi, u, v, conv, mixed, sa, sb = (kept_f32[key] for key in KEPT if key != "vnb")
        vnb = k_ref[:, KEPT.index("vnb") * d:(KEPT.index("vnb") + 1) * d]
        xc = v - jnp.mean(v, axis=-1, keepdims=True)
        rstd = lax.rsqrt(jnp.mean(xc * xc, axis=-1, keepdims=True) + EPS)
        vhat = xc * rstd
        dmerged = lax.dot_general(dx_ref[...].astype(BF16), wout_ref[...], NT_DIMS, preferred_element_type=F32)
        dp_ref[:, 5 * d:6 * d] = (dmerged * (b * conv) * (sa * (1.0 - sa))).astype(BF16)
        dp_ref[:, 6 * d:7 * d] = (dmerged * (u * mixed) * (sb * (1.0 - sb))).astype(BF16)
        dya = dmerged * sa
        dyb = dmerged * sb
        dp_ref[:, 0:d] = (dya * conv).astype(BF16)
        dconv = dya * b
        next8 = carry_ref[...] * keep_next
        carry_ref[...] = dconv[:HALO]
        dcx, u1, u2 = _conv_bwd(dconv, next8, cw)
        cx = c * xi
        dcw_ref[0:1, :] += _colsum(u2 * cx)
        dcw_ref[1:2, :] += _colsum(u1 * cx)
        dcw_ref[2:3, :] += _colsum(dconv * cx)
        dp_ref[:, d:2 * d] = (dcx * xi).astype(BF16)
        dp_ref[:, 2 * d:3 * d] = (dcx * c).astype(BF16)
        dp_ref[:, 3 * d:4 * d] = (dyb * mixed).astype(BF16)
        dmixed = dyb * u
        dmb = dmixed.astype(BF16)
        tril = (lax.broadcasted_iota(jnp.int32, (CHUNK, CHUNK), 0)
                >= lax.broadcasted_iota(jnp.int32, (CHUNK, CHUNK), 1))
        triu = (lax.broadcasted_iota(jnp.int32, (CHUNK, CHUNK), 0)
                <= lax.broadcasted_iota(jnp.int32, (CHUNK, CHUNK), 1))
        dbs_tile = dmixed[0:CHUNK]
        for n in range(1, tm // CHUNK):
            dbs_tile = dbs_tile + dmixed[n * CHUNK:(n + 1) * CHUNK]
        dbs_acc[...] += dbs_tile
        for g in range(N_GROUPS):
            wmt = jnp.where(triu, wst_ref[g], 0.0).astype(BF16)
            cols = slice(g * gd, (g + 1) * gd)
            dw = jnp.zeros((CHUNK, CHUNK), F32)
            for n in range(tm // CHUNK):
                rows = slice(n * CHUNK, (n + 1) * CHUNK)
                dvn_scr[rows, cols] = jnp.dot(wmt, dmb[rows, cols], preferred_element_type=F32)
                dw = dw + lax.dot_general(dmb[rows, cols], vnb[rows, cols], NT_DIMS, preferred_element_type=F32)
            dws_ref[g] += jnp.where(tril, dw, 0.0)
        dvn = dvn_scr[...]
        dln_ref[0:1, :] += _colsum(dvn * vhat)
        dln_ref[1:2, :] += _colsum(dvn)
        dvh = dvn * lng
        dv = rstd * (dvh - jnp.mean(dvh, axis=-1, keepdims=True)
                     - vhat * jnp.mean(dvh * vhat, axis=-1, keepdims=True))
        dp_ref[:, 4 * d:5 * d] = dv.astype(BF16)
        if on_to_x0 is not None:
            dh = jnp.zeros((tm, d), F32)
            for j in range(0, nj, 2):
                pair = jnp.concatenate([win_ref[j], win_ref[j + 1]], axis=1)
                dh = dh + lax.dot_general(dp_ref[:, j * wn:(j + 2) * wn], pair, NT_DIMS, preferred_element_type=F32)
            dx, dg = _rms_bwd(dh, x0_ref[...], g_ref[...])
            dx0_ref[...] = dx_ref[...] + dx
            dg_ref[0:1, :] += dg

        @pl.when(i == nt - 1)
        def _():
            for g in range(N_GROUPS):
                cols = slice(g * gd, (g + 1) * gd)
                s = jnp.sum(dbs_acc[:, cols], axis=1, keepdims=True)
                dbs_ref[:, cols] = jnp.broadcast_to(s, (CHUNK, gd))

    rev = lambda i: nt - 1 - i
    const2 = lambda i: (0, 0)
    const3 = lambda i: (0, 0, 0)
    row = lambda i: (rev(i), 0)
    in_specs = [pl.BlockSpec((tm, d), row), pl.BlockSpec((tm, len(KEPT) * d), row),
                _once((d, d), const2), _once((HALO, d), const2), _once((1, d), const2),
                _once((N_GROUPS, CHUNK, CHUNK), const3)]
    out_specs = [pl.BlockSpec((tm, 7 * d), row), pl.BlockSpec((HALO, d), const2), pl.BlockSpec((HALO, d), const2),
                 pl.BlockSpec((N_GROUPS, CHUNK, CHUNK), const3), pl.BlockSpec((CHUNK, d), const2)]
    out_shape = [jax.ShapeDtypeStruct((t, 7 * d), BF16), jax.ShapeDtypeStruct((HALO, d), F32),
                 jax.ShapeDtypeStruct((HALO, d), F32), jax.ShapeDtypeStruct((N_GROUPS, CHUNK, CHUNK), F32),
                 jax.ShapeDtypeStruct((CHUNK, d), F32)]
    args = [dx1, kept, wout, cw, lng, wst]
    if on_to_x0 is not None:
        in_specs += [_once((nj, d, wn), const3), pl.BlockSpec((tm, d), row), _once((1, d), const2)]
        out_specs += [pl.BlockSpec((tm, d), row), pl.BlockSpec((HALO, d), const2)]
        out_shape += [jax.ShapeDtypeStruct((t, d), F32), jax.ShapeDtypeStruct((HALO, d), F32)]
        args += list(on_to_x0)
    return pl.pallas_call(
        body, name=name, grid=(nt,), in_specs=in_specs, out_specs=out_specs, out_shape=out_shape,
        scratch_shapes=[pltpu.VMEM((tm, d), F32), pltpu.VMEM((HALO, d), F32), pltpu.VMEM((CHUNK, d), F32)],
        compiler_params=_params(1),
    )(*args)


def dgrad_rms(dy, w, x, g, res, *, name, tm):
    t, d = x.shape
    n = w.shape[2]
    w = w.reshape(w.shape[0] // 2, 2, d, n)
    nj = w.shape[0]
    tm = min(tm, t)

    def body(dy_ref, w_ref, x_ref, g_ref, res_ref, dx_ref, dg_ref, acc_ref):
        i, j = pl.program_id(0), pl.program_id(1)

        @pl.when((i == 0) & (j == 0))
        def _():
            dg_ref[...] = jnp.zeros_like(dg_ref)

        pair = jnp.concatenate([w_ref[0], w_ref[1]], axis=1)
        part = lax.dot_general(dy_ref[...], pair, NT_DIMS, preferred_element_type=F32)

        @pl.when(j == 0)
        def _():
            acc_ref[...] = part

        @pl.when(j > 0)
        def _():
            acc_ref[...] += part

        @pl.when(j == nj - 1)
        def _():
            dx, dg = _rms_bwd(acc_ref[...], x_ref[...], g_ref[...])
            dx_ref[...] = res_ref[...] + dx
            dg_ref[0:1, :] += dg

    return pl.pallas_call(
        body, name=name, grid=(t // tm, nj),
        in_specs=[pl.BlockSpec((tm, 2 * n), lambda i, j: (i, j)),
                  pl.BlockSpec((None, 2, d, n), lambda i, j: (j, 0, 0, 0)),
                  pl.BlockSpec((tm, d), lambda i, j: (i, 0)),
                  pl.BlockSpec((1, d), lambda i, j: (0, 0)),
                  pl.BlockSpec((tm, d), lambda i, j: (i, 0))],
        out_specs=[pl.BlockSpec((tm, d), lambda i, j: (i, 0)), pl.BlockSpec((HALO, d), lambda i, j: (0, 0))],
        out_shape=[jax.ShapeDtypeStruct((t, d), F32), jax.ShapeDtypeStruct((HALO, d), F32)],
        scratch_shapes=[pltpu.VMEM((tm, d), F32)],
        compiler_params=_params(2),
    )(dy, w, x, g, res)


def wgrad(a, b, *, nj, a_mode, b_mode, name, tm, split=1):
    def describe(arr, mode):
        if mode == "full":
            return arr.shape[0], arr.shape[1], pl.BlockSpec((tm_, arr.shape[1]), lambda j, s: (s, 0))
        if mode == "cols":
            c = arr.shape[1] // nj
            return arr.shape[0], c, pl.BlockSpec((tm_, c), lambda j, s: (s, j))
        return arr.shape[1], arr.shape[2], pl.BlockSpec((None, tm_, arr.shape[2]), lambda j, s: (j, s, 0))

    t = a.shape[0] if a_mode != "lead" else a.shape[1]
    tm_ = min(tm, t)
    _, k, a_spec = describe(a, a_mode)
    _, n, b_spec = describe(b, b_mode)

    ns = t // tm_
    nc = n // split

    def body(a_ref, b_ref, o_ref, acc_ref):
        s = pl.program_id(1)
        part = lax.dot_general(a_ref[...], b_ref[...], TN_DIMS, preferred_element_type=F32)

        def finish(total):
            for q in range(split):
                o_ref[q] = total[:, q * nc:(q + 1) * nc].astype(BF16)

        if ns == 1:
            finish(part)
            return

        @pl.when(s == 0)
        def _():
            acc_ref[...] = part

        @pl.when((s > 0) & (s < ns - 1))
        def _():
            acc_ref[...] += part

        @pl.when(s == ns - 1)
        def _():
            finish(acc_ref[...] + part)

    return pl.pallas_call(
        body, name=name, grid=(nj, ns),
        in_specs=[a_spec, b_spec],
        out_specs=pl.BlockSpec((split, k, nc), lambda j, s: (j, 0, 0)),
        out_shape=jax.ShapeDtypeStruct((nj * split, k, nc), BF16),
        scratch_shapes=[pltpu.VMEM((k, n), F32)],
        compiler_params=_params(2),
    )(a, b)


def _adamw_math(w, g, m, v):
    m = ADAM_B1 * m + (1.0 - ADAM_B1) * g
    v = ADAM_B2 * v + (1.0 - ADAM_B2) * (g * g)
    m_hat = m / (1.0 - ADAM_B1 ** ADAM_STEP)
    v_hat = v / (1.0 - ADAM_B2 ** ADAM_STEP)
    delta = -ADAM_LR * (m_hat / (jnp.sqrt(v_hat) + ADAM_EPS) + ADAM_WD * w)
    return delta, m, v


def _row_tile(rows, at_most):
    if rows <= at_most:
        return rows
    return max(k for k in range(16, at_most + 1, 16) if rows % k == 0)


def _sum_in_device_order(ref):
    total = ref[0]
    for s in range(1, N_DEV):
        total = total + ref[s]
    return total


def adamw_sharded(me, own0, recv0, own1, recv1, w, m, v, *, name, tr):
    _, r, c = w.shape
    tr = _row_tile(r, tr)
    ni = r // tr

    def body(me_ref, o0_ref, r0_ref, o1_ref, r1_ref, w_ref, m_ref, v_ref, g_ref, d_ref, nm_ref, nv_ref):
        def finish(own_ref, recv_ref):
            g = None
            for s in range(N_DEV):
                term = jnp.where(me_ref[0] == s, own_ref[...], recv_ref[s]).astype(F32)
                g = term if g is None else g + term
            delta, nm, nv = _adamw_math(w_ref[...], g, m_ref[...], v_ref[...])
            g_ref[...] = g
            d_ref[...] = delta
            nm_ref[...] = nm
            nv_ref[...] = nv

        @pl.when(pl.program_id(0) == 0)
        def _():
            finish(o0_ref, r0_ref)

        @pl.when(pl.program_id(0) == 1)
        def _():
            finish(o1_ref, r1_ref)

    row0 = lambda l, i: i * (1 - l) + (ni - 1) * l
    row1 = lambda l, i: i * l
    lay = pl.BlockSpec((None, tr, c), lambda l, i, me_ref: (l, i, 0))
    grid_spec = pltpu.PrefetchScalarGridSpec(
        num_scalar_prefetch=1, grid=(2, ni),
        in_specs=[pl.BlockSpec((None, tr, c), lambda l, i, me_ref: (me_ref[0], row0(l, i), 0)),
                  pl.BlockSpec((N_DEV, tr, c), lambda l, i, me_ref: (0, row0(l, i), 0)),
                  pl.BlockSpec((None, tr, c), lambda l, i, me_ref: (me_ref[0], row1(l, i), 0)),
                  pl.BlockSpec((N_DEV, tr, c), lambda l, i, me_ref: (0, row1(l, i), 0)),
                  lay, lay, lay],
        out_specs=[lay, lay, lay, lay])
    return pl.pallas_call(
        body, name=name, grid_spec=grid_spec,
        out_shape=[jax.ShapeDtypeStruct(w.shape, F32)] * 4,
        compiler_params=_params(2),
    )(me, own0, recv0, own1, recv1, w, m, v)


def sum_chunks(me, own, recv, *, name):
    _, r, c = own.shape

    def body(me_ref, o_ref, r_ref, out_ref):
        total = None
        for s in range(N_DEV):
            term = jnp.where(me_ref[0] == s, o_ref[...], r_ref[s]).astype(F32)
            total = term if total is None else total + term
        out_ref[...] = total

    grid_spec = pltpu.PrefetchScalarGridSpec(
        num_scalar_prefetch=1, grid=(1,),
        in_specs=[pl.BlockSpec((None, r, c), lambda i, me_ref: (me_ref[0], 0, 0)),
                  pl.BlockSpec((N_DEV, r, c), lambda i, me_ref: (0, 0, 0))],
        out_specs=pl.BlockSpec((r, c), lambda i, me_ref: (0, 0)))
    return pl.pallas_call(
        body, name=name, grid_spec=grid_spec,
        out_shape=jax.ShapeDtypeStruct((r, c), F32),
        compiler_params=_params(1),
    )(me, own, recv)


def adamw_small(g, w, m, v, *, name):
    def body(g_ref, w_ref, m_ref, v_ref, d_ref, nm_ref, nv_ref):
        delta, nm, nv = _adamw_math(w_ref[...], g_ref[...], m_ref[...], v_ref[...])
        d_ref[...] = delta
        nm_ref[...] = nm
        nv_ref[...] = nv

    return pl.pallas_call(
        body, name=name,
        out_shape=[jax.ShapeDtypeStruct(w.shape, F32)] * 3,
        compiler_params=pltpu.CompilerParams(vmem_limit_bytes=VMEM_LIMIT_BYTES),
    )(g, w, m, v)


def sum_devices(parts, *, name, tr):
    _, r, c = parts.shape
    tr = min(tr, r)

    def body(p_ref, o_ref):
        o_ref[...] = _sum_in_device_order(p_ref)

    return pl.pallas_call(
        body, name=name, grid=(r // tr,),
        in_specs=[pl.BlockSpec((N_DEV, tr, c), lambda i: (0, i, 0))],
        out_specs=pl.BlockSpec((tr, c), lambda i: (i, 0)),
        out_shape=jax.ShapeDtypeStruct((r, c), F32),
        compiler_params=_params(1),
    )(parts)


def _my_place():
    return lax.axis_index("x"), lax.axis_index("y"), lax.axis_index("c")


def _peer_place(r, x, y, c):
    fx, fy, fc = (r >> 2) & 1, (r >> 1) & 1, r & 1
    return (1 - x if fx else x, 1 - y if fy else y, 1 - c if fc else c)


def own_slot(me, w, layer, dtype, *, name, tr):
    _, r, c = w.shape
    tr = _row_tile(r, tr)

    def body(me_ref, w_ref, o_ref):
        o_ref[...] = w_ref[...].astype(dtype)

    grid_spec = pltpu.PrefetchScalarGridSpec(
        num_scalar_prefetch=1, grid=(r // tr,),
        in_specs=[pl.BlockSpec((None, tr, c), lambda i, me_ref: (layer, i, 0))],
        out_specs=pl.BlockSpec((None, tr, c), lambda i, me_ref: (me_ref[0], i, 0)))
    return pl.pallas_call(
        body, name=name, grid_spec=grid_spec,
        out_shape=jax.ShapeDtypeStruct((N_DEV, r, c), dtype),
        compiler_params=_params(1),
    )(me, w)


EXCHANGES = {
    "scatter": [(0, r) for r in range(1, N_DEV)],
    "gather": [(0, r) for r in range(1, N_DEV)],
    "gather_chips": [(0, r) for r in (1, 2, 4, 6)],
    "gather_forward": [(q, 1) for q in (2, 4, 6)],
}


def _split_copy(k, entry, src, land, send_sem, recv_sem, arriving):
    slot, peer = entry
    x, y, c = _my_place()

    def index(relation):
        px, py, pc = _peer_place(relation, x, y, c)
        return 4 * px + 2 * py + pc

    return pltpu.make_async_remote_copy(
        src_ref=land.at[index(slot)] if src is None else src.at[index(peer)],
        dst_ref=land.at[index(slot ^ peer if arriving else slot)],
        send_sem=send_sem.at[k], recv_sem=recv_sem.at[k],
        device_id=_peer_place(peer, x, y, c), device_id_type=MESH)


def start_copies(srcs, lands, *, mode, name, after=None):
    n = len(lands)
    entries = EXCHANGES[mode]
    bufs = (list(srcs) if srcs is not None else []) + list(lands)
    nb = len(bufs)

    def body(*refs):
        src = refs[:n] if srcs is not None else [None] * n
        land = refs[nb - n:nb]
        outs = refs[nb + len(extra):]
        send_sems, recv_sems = outs[:n], outs[n:2 * n]
        token = outs[2 * n + nb]
        for a in range(n):
            for k, entry in enumerate(entries):
                _split_copy(k, entry, src[a], land[a], send_sems[a], recv_sems[a], False).start()
        token[...] = jnp.zeros_like(token)

    extra = [] if after is None else [after]
    outs = pl.pallas_call(
        body, name=name,
        in_specs=[HBM_SPEC] * nb + [ANY] * len(extra),
        out_specs=[SEM_SPEC] * (2 * n) + [HBM_SPEC] * nb + [pl.BlockSpec(memory_space=pltpu.VMEM)],
        out_shape=([pltpu.SemaphoreType.DMA((len(entries),))] * (2 * n)
                   + [pltpu.HBM(a.shape, a.dtype) for a in bufs]
                   + [jax.ShapeDtypeStruct((8, 128), F32)]),
        input_output_aliases={i: 2 * n + i for i in range(nb)},
        compiler_params=pltpu.CompilerParams(has_side_effects=DATAFLOW),
    )(*[pltpu.with_memory_space_constraint(a, pltpu.HBM) for a in bufs], *extra)
    thru = list(outs[2 * n:2 * n + nb])
    return dict(send=outs[:n], recv=outs[n:2 * n], src=thru[:n] if srcs is not None else None, land=thru[nb - n:],
                token=outs[2 * n + nb], mode=mode)


def finish_copies(started, which, after, *, name):
    n = len(which)
    entries = EXCHANGES[started["mode"]]
    has_src = started["src"] is not None
    bufs = ([started["src"][i] for i in which] if has_src else []) + [started["land"][i] for i in which]
    nb = len(bufs)

    def body(*refs):
        src = refs[:n] if has_src else [None] * n
        land = refs[nb - n:nb]
        send_sems, recv_sems = refs[nb:nb + n], refs[nb + n:nb + 2 * n]
        for a in range(n):
            for k, entry in enumerate(entries):
                cp = _split_copy(k, entry, src[a], land[a], send_sems[a], recv_sems[a], True)
                cp.wait_send()
                cp.wait_recv()

    outs = pl.pallas_call(
        body, name=name,
        in_specs=[HBM_SPEC] * nb + [SEM_SPEC] * (2 * n) + [ANY],
        out_specs=[HBM_SPEC] * nb,
        out_shape=[pltpu.HBM(a.shape, a.dtype) for a in bufs],
        input_output_aliases={i: i for i in range(nb)},
        compiler_params=pltpu.CompilerParams(has_side_effects=DATAFLOW),
    )(*bufs, *[started["send"][i] for i in which], *[started["recv"][i] for i in which], after)
    return (list(outs[:n]) if has_src else None), list(outs[nb - n:])


def _pad_rows(a, rows):
    pad = [(0, 0)] * a.ndim
    pad[-2] = (0, rows - a.shape[-2])
    return jnp.pad(a, pad)


def kernel(x, mix_norm_g, w_in, conv_a_w, ln_v_g, ln_v_b, w_s, b_s, w_out, ffn_norm_g, w_up, conv_ffn_w, w_down, final_norm_g, loss_target, m_mix_norm_g, m_w_in, m_conv_a_w, m_ln_v_g, m_ln_v_b, m_w_s, m_b_s, m_w_out, m_ffn_norm_g, m_w_up, m_conv_ffn_w, m_w_down, m_final_norm_g, v_mix_norm_g, v_w_in, v_conv_a_w, v_ln_v_g, v_ln_v_b, v_w_s, v_b_s, v_w_out, v_ffn_norm_g, v_w_up, v_conv_ffn_w, v_w_down, v_final_norm_g):
    nb, seq, d = x.shape
    t = nb * seq
    depth = w_in.shape[0]
    f = w_up.shape[2]
    me = 4 * lax.axis_index("x") + 2 * lax.axis_index("y") + lax.axis_index("c")
    xt = x.reshape(t, d)
    tgt = loss_target.reshape(t, d)

    conv_pack = jnp.concatenate([_pad_rows(conv_a_w, HALO), _pad_rows(conv_ffn_w, HALO)], axis=-1)
    me_arr = me.astype(jnp.int32).reshape(1)
    w_up_t, m_w_up_t, v_w_up_t = (jnp.swapaxes(a, 1, 2) for a in (w_up, m_w_up, v_w_up))
    zones, slot_of = [], {}
    for l in range(depth):
        for key, w in (("win", w_in), ("conv", None), ("wout", w_out), ("wup", w_up_t), ("wd", w_down)):
            if key == "conv":
                if l == 0:
                    slot_of["conv"] = len(zones)
                    packed = conv_pack.reshape(1, depth * HALO, conv_pack.shape[-1])
                    zones.append(own_slot(me_arr, packed, 0, F32, name="own_slot_conv", tr=ROWS_PER_STEP["own_slot"]))
                continue
            slot_of[key, l] = len(zones)
            zones.append(own_slot(me_arr, w, l, BF16, name=f"own_slot_{key}_{l}", tr=ROWS_PER_STEP["own_slot"]))
    first = [slot_of["win", 0], slot_of["wout", 0], slot_of["conv"]]
    rest = [i for i in range(len(zones)) if i not in first]
    to_chips = start_copies(None, [zones[i] for i in first], mode="gather_chips", name="gather_first_chips")
    gathering = start_copies(None, [zones[i] for i in rest], mode="gather", name="gather_start", after=to_chips["token"])
    _, at_chips = finish_copies(to_chips, [0, 1, 2], gathering["token"], name="wait_first_chips")
    to_sibling = start_copies(None, at_chips, mode="gather_forward", name="gather_first_forward")

    def gathered(keys, after, name):
        return finish_copies(gathering, [rest.index(slot_of[k]) for k in keys], after, name=name)[1]

    saved, layers = [], []
    cur = xt
    for l in range(depth):
        p = dict(mix_g=mix_norm_g[l][None], ffn_g=ffn_norm_g[l][None], lng=ln_v_g[l][None], lnb=ln_v_b[l][None],
                 ws=w_s[l], wst=jnp.swapaxes(w_s[l], 1, 2),
                 bias=jnp.repeat(b_s[l].T, d // N_GROUPS, axis=1))
        if l == 0:
            _, (p["win"], wout_g, conv_g) = finish_copies(to_sibling, [0, 1, 2], to_sibling["token"],
                                                          name=f"wait_w_mixer_{l}")
            conv_g = conv_g.reshape(N_DEV, depth, HALO, -1)
            ca = conv_g.shape[-1] - f
        else:
            p["win"], wout_g = gathered([("win", l), ("wout", l)], after, f"wait_w_mixer_{l}")
        p["wout"] = wout_g.reshape(d, d)
        p["cw_a"] = jnp.transpose(conv_g[:, l, :, :ca], (1, 0, 2)).reshape(HALO, d)
        p["cw_f"] = conv_g[:, l, :, ca:]
        h, merged, x1, kept = mixer_fwd(cur, p["mix_g"], p["win"], p["wout"], p["cw_a"], p["lng"], p["lnb"],
                                        p["ws"], p["bias"], seq=seq, name=f"mixer_fwd_{l}", tm=TOKENS_PER_STEP["mixer"])
        p["wup"], wd_g = gathered([("wup", l), ("wd", l)], merged, f"wait_w_ffn_{l}")
        p["wd"] = wd_g.reshape(N_DEV // 2, 2 * wd_g.shape[1], d)
        head = (final_norm_g[None], tgt) if l == depth - 1 else None
        h2, up0, fac, act, x2, *of_loss = ffn_fwd(x1, p["ffn_g"], p["wup"], p["wd"], p["cw_f"],
                                                  seq=seq, name=f"ffn_fwd_{l}", tm=TOKENS_PER_STEP["ffn"], head=head)
        saved.append(dict(x0=cur, h=h, kept=kept, merged=merged, x1=x1, h2=h2, up0=up0, fac=fac, act=act))
        layers.append(p)
        cur, after = x2, act
    dx = cur
    d_final_g, loss_tile = of_loss

    def exchange(parts, name):
        return start_copies(parts, [lax.empty(a.shape, a.dtype) for a in parts], mode="scatter", name=name)

    def tied(g, started):
        return g + started["token"][0:1, 0:1]

    part = [None] * depth
    mix_ex = None
    for l in reversed(range(depth)):
        p, s = layers[l], saved[l]
        ffn_g = p["ffn_g"] if mix_ex is None else tied(p["ffn_g"], mix_ex)
        dup0, dcw_f, dx1, d_ffn_g = ffn_bwd(dx, s["up0"], s["fac"], p["wd"], p["cw_f"], p["wup"], s["x1"], ffn_g,
                                            seq=seq, name=f"ffn_bwd_{l}", tm=TOKENS_PER_STEP["ffn"])
        g_wd = wgrad(s["act"], dx, nj=N_DEV // 2, a_mode="lead", b_mode="full", name=f"wgrad_down_{l}", tm=TOKENS_PER_STEP["wgrad"])
        g_wup = wgrad(dup0, s["h2"], nj=N_DEV, a_mode="lead", b_mode="full", name=f"wgrad_up_{l}", tm=TOKENS_PER_STEP["wgrad"])
        ffn_ex = exchange([g_wd.reshape(N_DEV, g_wd.shape[1] // 2, d), g_wup], f"exchange_ffn_{l}")
        fused = l > 0
        dproj, dcw_a, dln, dws, dbs, *to_x0 = mixer_bwd(
            dx1, s["kept"], p["wout"], tied(p["cw_a"], ffn_ex), p["lng"], p["wst"],
            seq=seq, name=f"mixer_bwd_{l}", tm=TOKENS_PER_STEP["mixer" if fused else "mixer_bwd_alone"], on_to_x0=(p["win"], s["x0"], p["mix_g"]) if fused else None)
        g_wout = wgrad(s["merged"], dx1, nj=1, a_mode="full", b_mode="full", name=f"wgrad_out_{l}", tm=TOKENS_PER_STEP["wgrad"])
        g_win = wgrad(s["h"], dproj, nj=N_DEV // 2, a_mode="full", b_mode="cols", name=f"wgrad_in_{l}", tm=TOKENS_PER_STEP["wgrad"], split=2)
        cwa_chunks = jnp.transpose(dcw_a.reshape(HALO, N_DEV, d // N_DEV), (1, 0, 2))
        mix_ex = exchange([g_wout.reshape(N_DEV, d // N_DEV, d), g_win, dws, dcw_f, cwa_chunks], f"exchange_mix_{l}")
        if fused:
            dx, d_mix_g = to_x0
        else:
            dx, d_mix_g = dgrad_rms(dproj, p["win"], s["x0"], tied(p["mix_g"], mix_ex), dx1,
                                    name=f"dgrad_in_{l}", tm=TOKENS_PER_STEP["dgrad"])
        part[l] = dict(
            ffn_ex=ffn_ex, mix_ex=mix_ex,
            vectors=jnp.concatenate([d_mix_g[0:1], d_ffn_g[0:1], dln[0:2],
                                     dbs[:, ::d // N_GROUPS].T.reshape(1, d)], axis=0))
    grad_x = dx.reshape(nb, seq, d)

    own, recv = {}, {}

    def arrived(l, ex, keys, after):
        srcs, lands = finish_copies(part[l][ex], list(range(len(keys))), after, name=f"wait_{ex}_{l}")
        for k, key in enumerate(keys):
            own[key, l], recv[key, l] = srcs[k], lands[k]
        return lands[1]

    def big(key, w, m, v, name, me=me_arr):
        return adamw_sharded(me, own[key, 0], recv[key, 0], own[key, 1], recv[key, 1], w, m, v, name=name, tr=ROWS_PER_STEP["adamw"])

    mix_keys = ("wout", "win", "ws", "cwf", "cwa")
    after = grad_x
    for l in reversed(range(depth)):
        after = arrived(l, "ffn_ex", ("wd", "wup"), after)
        if l > 0:
            after = arrived(l, "mix_ex", mix_keys, after)
    u_wd = big("wd", w_down, m_w_down, v_w_down, "adamw_w_down")
    u_wup = tuple(jnp.swapaxes(a, 1, 2) for a in big("wup", w_up_t, m_w_up_t, v_w_up_t, "adamw_w_up"))
    arrived(0, "mix_ex", mix_keys, u_wup[1])

    def owned(key, l):
        return sum_chunks(me_arr, own[key, l], recv[key, l], name=f"sum_{key}_{l}")

    g_cwf = jnp.stack([owned("cwf", l)[:3] for l in range(depth)])
    g_cwa = jnp.stack([owned("cwa", l)[:3] for l in range(depth)])
    ws_rows = CHUNK * CHUNK // d
    ws_mine = jnp.concatenate([owned("ws", l).reshape(ws_rows, d) for l in range(depth)], axis=0)
    loss_row = jnp.zeros((1, d), F32).at[0, 0].set(loss_tile[0, 0])
    vectors = jnp.concatenate([part[l]["vectors"] for l in range(depth)] + [d_final_g[0:1], loss_row], axis=0)
    vectors = _pad_rows(vectors, -(-vectors.shape[0] // 8) * 8)
    small_zones = [own_slot(me_arr, a[None], 0, F32, name=f"own_slot_{key}", tr=ROWS_PER_STEP["own_slot"])
                   for key, a in (("vectors", vectors), ("ws_sums", ws_mine))]
    small_ex = start_copies(None, small_zones, mode="gather", name="gather_small_start")
    me_then = me_arr + small_ex["token"][0, 0:1].astype(jnp.int32)
    u_wout = big("wout", w_out, m_w_out, v_w_out, "adamw_w_out", me_then)
    u_win = big("win", w_in, m_w_in, v_w_in, "adamw_w_in", me_then)
    _, (vectors_all, ws_all) = finish_copies(small_ex, [0, 1], u_win[0], name="wait_small_grads")
    vec_sum = sum_devices(vectors_all, name="sum_small", tr=ROWS_PER_STEP["sum"])
    g_ws = jnp.transpose(ws_all.reshape(N_DEV, depth, CHUNK, CHUNK), (1, 0, 2, 3))
    per_layer = part[0]["vectors"].shape[0]
    g_mix, g_ffn, g_lng, g_lnb = (jnp.stack([vec_sum[l * per_layer + k] for l in range(depth)]) for k in range(4))
    g_bs = jnp.stack([vec_sum[l * per_layer + 4].reshape(N_GROUPS, CHUNK) for l in range(depth)])
    g_final = vec_sum[depth * per_layer]
    loss = vec_sum[depth * per_layer + 1, 0]

    def small_update(g, w, m, v, name):
        shape = w.shape
        two_d = (-1, shape[-1]) if w.ndim > 1 else (1, shape[0])
        out = adamw_small(g.reshape(two_d), w.reshape(two_d), m.reshape(two_d), v.reshape(two_d), name=name)
        return (g.reshape(shape),) + tuple(o.reshape(shape) for o in out)

    u_mix = small_update(g_mix, mix_norm_g, m_mix_norm_g, v_mix_norm_g, "adamw_mix_norm_g")
    u_cwa = small_update(g_cwa, conv_a_w, m_conv_a_w, v_conv_a_w, "adamw_conv_a_w")
    u_lng = small_update(g_lng, ln_v_g, m_ln_v_g, v_ln_v_g, "adamw_ln_v_g")
    u_lnb = small_update(g_lnb, ln_v_b, m_ln_v_b, v_ln_v_b, "adamw_ln_v_b")
    u_ws = small_update(g_ws, w_s, m_w_s, v_w_s, "adamw_w_s")
    u_bs = small_update(g_bs, b_s, m_b_s, v_b_s, "adamw_b_s")
    u_ffn = small_update(g_ffn, ffn_norm_g, m_ffn_norm_g, v_ffn_norm_g, "adamw_ffn_norm_g")
    u_cwf = small_update(g_cwf, conv_ffn_w, m_conv_ffn_w, v_conv_ffn_w, "adamw_conv_ffn_w")
    u_final = small_update(g_final, final_norm_g, m_final_norm_g, v_final_norm_g, "adamw_final_norm_g")

    ordered = [u_mix, u_win, u_cwa, u_lng, u_lnb, u_ws, u_bs, u_wout, u_ffn, u_wup, u_cwf, u_wd, u_final]
    return (loss, grad_x, *[u[0] for u in ordered], *[u[1] for u in ordered],
            *[u[2] for u in ordered], *[u[3] for u in ordered])
```

```python
import jax
import jax.numpy as jnp
from jax import lax
from jax.experimental import pallas as pl
from jax.experimental.pallas import tpu as pltpu

EPS = 1e-6
CHUNK = 128
N_GROUPS = 8
N_DEV = 8
HALO = 8
ADAM_LR = 0.001
ADAM_B1 = 0.9
ADAM_B2 = 0.999
ADAM_EPS = 1e-08
ADAM_WD = 0.01
ADAM_STEP = 10
VMEM_LIMIT_BYTES = 56 * 1024 * 1024
TOKENS_PER_STEP = dict(mixer=256, mixer_bwd_alone=512, ffn=256, wgrad=2048, dgrad=1024)
ROWS_PER_STEP = dict(adamw=256, own_slot=256, sum=512)
F32 = jnp.float32
BF16 = jnp.bfloat16
MESH = pl.DeviceIdType.MESH
ANY = pl.BlockSpec(memory_space=pl.ANY)
HBM_SPEC = pl.BlockSpec(memory_space=pltpu.HBM)
SEM_SPEC = pl.BlockSpec(memory_space=pltpu.SEMAPHORE)
DATAFLOW = pltpu.SideEffectType.DATAFLOW_SIDE_EFFECTING
NT_DIMS = (((1,), (1,)), ((), ()))
TN_DIMS = (((0,), (0,)), ((), ()))


def _params(n_grid_axes):
    return pltpu.CompilerParams(dimension_semantics=("arbitrary",) * n_grid_axes,
                                vmem_limit_bytes=VMEM_LIMIT_BYTES)


def _shift_down(cur, prev8, k):
    rolled = pltpu.roll(cur, k, 0)
    prolled = pltpu.roll(prev8, k, 0)
    row = lax.broadcasted_iota(jnp.int32, prev8.shape, 0)
    head = jnp.where(row < k, prolled, rolled[:HALO])
    return jnp.concatenate([head, rolled[HALO:]], axis=0)


def _shift_up(cur, next8, k):
    tm = cur.shape[0]
    rolled = pltpu.roll(cur, tm - k, 0)
    nrolled = pltpu.roll(next8, HALO - k, 0)
    row = lax.broadcasted_iota(jnp.int32, next8.shape, 0)
    tail = jnp.where(row >= HALO - k, nrolled, rolled[tm - HALO:])
    return jnp.concatenate([rolled[:tm - HALO], tail], axis=0)


def _conv_fwd(cur, prev8, cw):
    s1 = _shift_down(cur, prev8, 1)
    s2 = _shift_down(cur, prev8, 2)
    y = s2 * cw[0:1, :] + s1 * cw[1:2, :] + cur * cw[2:3, :]
    return y, s1, s2


def _conv_bwd(d, next8, cw):
    u1 = _shift_up(d, next8, 1)
    u2 = _shift_up(d, next8, 2)
    return d * cw[2:3, :] + u1 * cw[1:2, :] + u2 * cw[0:1, :], u1, u2


def _colsum(a):
    return jnp.sum(a, axis=0, keepdims=True)


def _rms_stats(xv):
    r = lax.rsqrt(jnp.mean(xv * xv, axis=-1, keepdims=True) + EPS)
    return r, xv * r


def _rms_bwd(dh, xv, g):
    r, n = _rms_stats(xv)
    dn = dh * g
    dx = r * (dn - n * jnp.mean(dn * n, axis=-1, keepdims=True))
    return dx, _colsum(dh * n)


def _mixer_forward(p_ref, cprev, xiprev, cw, lng, lnb, ws_ref, bias_ref, mixed_scr, d):
    tm = p_ref.shape[0]
    b = p_ref[:, 0:d]
    c = p_ref[:, d:2 * d]
    xi = p_ref[:, 2 * d:3 * d]
    u = p_ref[:, 3 * d:4 * d]
    v = p_ref[:, 4 * d:5 * d]
    sa = jax.nn.sigmoid(p_ref[:, 5 * d:6 * d])
    sb = jax.nn.sigmoid(p_ref[:, 6 * d:7 * d])
    cx = c * xi
    conv, _, _ = _conv_fwd(cx, cprev * xiprev, cw)
    xc = v - jnp.mean(v, axis=-1, keepdims=True)
    vhat = xc * lax.rsqrt(jnp.mean(xc * xc, axis=-1, keepdims=True) + EPS)
    vnb = (vhat * lng + lnb).astype(BF16)
    tril = (lax.broadcasted_iota(jnp.int32, (CHUNK, CHUNK), 0)
            >= lax.broadcasted_iota(jnp.int32, (CHUNK, CHUNK), 1))
    gd = d // N_GROUPS
    for g in range(N_GROUPS):
        wm = jnp.where(tril, ws_ref[g], 0.0).astype(BF16)
        cols = slice(g * gd, (g + 1) * gd)
        for n in range(tm // CHUNK):
            rows = slice(n * CHUNK, (n + 1) * CHUNK)
            mixed_scr[rows, cols] = (jnp.dot(wm, vnb[rows, cols], preferred_element_type=F32)
                                     + bias_ref[:, cols])
    mixed = mixed_scr[...]
    merged = sa * (b * conv) + sb * (u * mixed)
    return dict(b=b, c=c, xi=xi, u=u, v=v, sa=sa, sb=sb, cx=cx, conv=conv, vnb=vnb, mixed=mixed, merged=merged)


KEPT = ("b", "c", "xi", "u", "v", "conv", "mixed", "vnb", "sa", "sb")


def _once(block_shape, index_map):
    return pl.BlockSpec(block_shape, index_map, pipeline_mode=pl.Buffered(1))


def mixer_fwd(x, g, win, wout, cw, lng, lnb, ws, bias, *, seq, name, tm):
    t, d = x.shape
    nj, _, n = win.shape
    tm = min(tm, seq)
    tiles_per_seq = seq // tm

    def body(x_ref, g_ref, win_ref, wout_ref, cw_ref, lng_ref, lnb_ref, ws_ref, bias_ref,
             h_ref, merged_ref, x1_ref, kept_ref, p_ref, mixed_scr, carry_ref):
        @pl.when(pl.program_id(0) == 0)
        def _():
            carry_ref[...] = jnp.zeros_like(carry_ref)

        keep = jnp.where(pl.program_id(0) % tiles_per_seq == 0, 0.0, 1.0)
        xv = x_ref[...]
        _, nrm = _rms_stats(xv)
        hb = (nrm * g_ref[...]).astype(BF16)
        h_ref[...] = hb
        for j in range(0, nj, 2):
            pair = jnp.concatenate([win_ref[j], win_ref[j + 1]], axis=1)
            p_ref[:, j * n:(j + 2) * n] = jnp.dot(hb, pair, preferred_element_type=F32)
        f = _mixer_forward(p_ref, carry_ref[...] * keep, 1.0, cw_ref[...], lng_ref[...],
                           lnb_ref[...], ws_ref, bias_ref, mixed_scr, d)
        carry_ref[...] = f["cx"][tm - HALO:]
        for k, key in enumerate(KEPT):
            kept_ref[:, k * d:(k + 1) * d] = f[key].astype(BF16)
        mb = f["merged"].astype(BF16)
        merged_ref[...] = mb
        x1_ref[...] = xv + jnp.dot(mb, wout_ref[...], preferred_element_type=F32)

    const2 = lambda i: (0, 0)
    const3 = lambda i: (0, 0, 0)
    row = lambda i: (i, 0)
    return pl.pallas_call(
        body, name=name, grid=(t // tm,),
        in_specs=[pl.BlockSpec((tm, d), row),
                  _once((1, d), const2),
                  _once((nj, d, n), const3),
                  _once((d, d), const2),
                  _once((HALO, d), const2),
                  _once((1, d), const2),
                  _once((1, d), const2),
                  _once((N_GROUPS, CHUNK, CHUNK), const3),
                  _once((CHUNK, d), const2)],
        out_specs=[pl.BlockSpec((tm, d), row), pl.BlockSpec((tm, d), row), pl.BlockSpec((tm, d), row),
                   pl.BlockSpec((tm, len(KEPT) * d), row)],
        out_shape=[jax.ShapeDtypeStruct((t, d), BF16), jax.ShapeDtypeStruct((t, d), BF16),
                   jax.ShapeDtypeStruct((t, d), F32), jax.ShapeDtypeStruct((t, len(KEPT) * d), BF16)],
        scratch_shapes=[pltpu.VMEM((tm, nj * n), F32), pltpu.VMEM((tm, d), F32), pltpu.VMEM((HALO, d), F32)],
        compiler_params=_params(1),
    )(x, g, win, wout, cw, lng, lnb, ws, bias)


def ffn_fwd(x1, g, wup, wd, cw, *, seq, name, tm, head=None):
    t, d = x1.shape
    nj, f, _ = wup.shape
    half = nj // 2
    tm = min(tm, seq)
    tiles_per_seq = seq // tm

    def body(x1_ref, g_ref, wup_ref, wd_ref, cw_ref, *rest):
        if head is None:
            h2_ref, up_ref, fac_ref, act_ref, x2_ref, carry_ref = rest
        else:
            gf_ref, tgt_ref, h2_ref, up_ref, fac_ref, act_ref, x2_ref, dgf_ref, loss_ref, carry_ref = rest

        @pl.when(pl.program_id(0) == 0)
        def _():
            carry_ref[...] = jnp.zeros_like(carry_ref)
            if head is not None:
                dgf_ref[...] = jnp.zeros_like(dgf_ref)
                loss_ref[...] = jnp.zeros_like(loss_ref)

        keep = jnp.where(pl.program_id(0) % tiles_per_seq == 0, 0.0, 1.0)
        xv = x1_ref[...]
        _, nrm = _rms_stats(xv)
        hb = (nrm * g_ref[...]).astype(BF16)
        h2_ref[...] = hb

        for j in range(nj):
            up_ref[j] = lax.dot_general(hb, wup_ref[j], NT_DIMS, preferred_element_type=F32)

        def conv_of(j):
            up0 = up_ref[j]
            y, _, _ = _conv_fwd(up0, carry_ref[j] * keep, cw_ref[j])
            carry_ref[j] = up0[tm - HALO:]
            return y

        acc = xv
        for k in range(half):
            gate, val = conv_of(k), conv_of(k + half)
            sg = jax.nn.sigmoid(gate)
            silu = gate * sg
            fac_ref[k] = (val * (sg * (1.0 + gate * (1.0 - sg)))).astype(BF16)
            fac_ref[k + half] = silu.astype(BF16)
            a = (silu * val).astype(BF16)
            act_ref[k] = a
            acc = acc + jnp.dot(a, wd_ref[k], preferred_element_type=F32)
        if head is None:
            x2_ref[...] = acc
        else:
            gv = gf_ref[...]
            r, n = _rms_stats(acc)
            err = n * gv - tgt_ref[...]
            loss_ref[...] += 0.5 * jnp.sum(jnp.mean(err * err, axis=-1, keepdims=True))
            dy = err * (1.0 / d)
            dn = dy * gv
            x2_ref[...] = r * (dn - n * jnp.mean(dn * n, axis=-1, keepdims=True))
            dgf_ref[0:1, :] += _colsum(dy * n)

    const3 = lambda i: (0, 0, 0)
    row = lambda i: (i, 0)
    in_specs = [pl.BlockSpec((tm, d), row), _once((1, d), lambda i: (0, 0)), _once((nj, f, d), const3),
                _once((half, f, d), const3), _once((nj, HALO, f), const3)]
    out_specs = [pl.BlockSpec((tm, d), row), pl.BlockSpec((nj, tm, f), lambda i: (0, i, 0)),
                 pl.BlockSpec((nj, tm, f), lambda i: (0, i, 0)), pl.BlockSpec((half, tm, f), lambda i: (0, i, 0)),
                 pl.BlockSpec((tm, d), row)]
    out_shape = [jax.ShapeDtypeStruct((t, d), BF16), jax.ShapeDtypeStruct((nj, t, f), F32),
                 jax.ShapeDtypeStruct((nj, t, f), BF16), jax.ShapeDtypeStruct((half, t, f), BF16),
                 jax.ShapeDtypeStruct((t, d), F32)]
    args = [x1, g, wup, wd, cw]
    if head is not None:
        in_specs += [_once((1, d), lambda i: (0, 0)), pl.BlockSpec((tm, d), row)]
        out_specs += [pl.BlockSpec((HALO, d), lambda i: (0, 0)), pl.BlockSpec((8, 128), lambda i: (0, 0))]
        out_shape += [jax.ShapeDtypeStruct((HALO, d), F32), jax.ShapeDtypeStruct((8, 128), F32)]
        args += list(head)
    return pl.pallas_call(
        body, name=name, grid=(t // tm,), in_specs=in_specs, out_specs=out_specs, out_shape=out_shape,
        scratch_shapes=[pltpu.VMEM((nj, HALO, f), F32)],
        compiler_params=_params(1),
    )(*args)


def ffn_bwd(dx2, up0, fac, wd, cw, wup, x1, g, *, seq, name, tm):
    t, d = dx2.shape
    nj, _, f = up0.shape
    half = nj // 2
    tm = min(tm, seq)
    tiles_per_seq = seq // tm
    nt = t // tm

    def body(dx_ref, up_ref, fac_ref, wd_ref, cw_ref, wup_ref, x1_ref, g_ref,
             dup_ref, dcw_ref, dx1_ref, dg_ref, carry_ref):
        i = pl.program_id(0)
        tile = nt - 1 - i

        @pl.when(i == 0)
        def _():
            dcw_ref[...] = jnp.zeros_like(dcw_ref)
            dg_ref[...] = jnp.zeros_like(dg_ref)
            carry_ref[...] = jnp.zeros_like(carry_ref)

        keep_next = jnp.where(tile % tiles_per_seq == tiles_per_seq - 1, 0.0, 1.0)
        dx2v = dx_ref[...]
        dxb = dx2v.astype(BF16)
        dh = [jnp.zeros((tm, d), F32)]

        def through_conv(j, dup):
            next8 = carry_ref[j] * keep_next
            carry_ref[j] = dup[:HALO]
            dup0, u1, u2 = _conv_bwd(dup, next8, cw_ref[j])
            up0 = up_ref[j]
            dcw_ref[j, 0:1, :] += _colsum(u2 * up0)
            dcw_ref[j, 1:2, :] += _colsum(u1 * up0)
            dcw_ref[j, 2:3, :] += _colsum(dup * up0)
            dup0 = dup0.astype(BF16)
            dup_ref[j] = dup0
            dh[0] = dh[0] + jnp.dot(dup0, wup_ref[j], preferred_element_type=F32)

        dacts = [lax.dot_general(dxb, wd_ref[k], NT_DIMS, preferred_element_type=F32) for k in range(half)]
        for k in range(half):
            through_conv(k, dacts[k] * fac_ref[k].astype(F32))
            through_conv(k + half, dacts[k] * fac_ref[k + half].astype(F32))

        dx, dg = _rms_bwd(dh[0], x1_ref[...], g_ref[...])
        dx1_ref[...] = dx2v + dx
        dg_ref[0:1, :] += dg

    rev = lambda i: nt - 1 - i
    return pl.pallas_call(
        body, name=name, grid=(nt,),
        in_specs=[pl.BlockSpec((tm, d), lambda i: (rev(i), 0)),
                  pl.BlockSpec((nj, tm, f), lambda i: (0, rev(i), 0)),
                  pl.BlockSpec((nj, tm, f), lambda i: (0, rev(i), 0)),
                  _once((half, f, d), lambda i: (0, 0, 0)),
                  _once((nj, HALO, f), lambda i: (0, 0, 0)),
                  _once((nj, f, d), lambda i: (0, 0, 0)),
                  pl.BlockSpec((tm, d), lambda i: (rev(i), 0)),
                  _once((1, d), lambda i: (0, 0))],
        out_specs=[pl.BlockSpec((nj, tm, f), lambda i: (0, rev(i), 0)),
                   pl.BlockSpec((nj, HALO, f), lambda i: (0, 0, 0)),
                   pl.BlockSpec((tm, d), lambda i: (rev(i), 0)),
                   pl.BlockSpec((HALO, d), lambda i: (0, 0))],
        out_shape=[jax.ShapeDtypeStruct((nj, t, f), BF16), jax.ShapeDtypeStruct((nj, HALO, f), F32),
                   jax.ShapeDtypeStruct((t, d), F32), jax.ShapeDtypeStruct((HALO, d), F32)],
        scratch_shapes=[pltpu.VMEM((nj, HALO, f), F32)],
        compiler_params=_params(1),
    )(dx2, up0, fac, wd, cw, wup, x1, g)


def mixer_bwd(dx1, kept, wout, cw, lng, wst, *, seq, name, tm, on_to_x0=None):
    t, d = dx1.shape
    tm = min(tm, seq)
    tiles_per_seq = seq // tm
    nt = t // tm
    gd = d // N_GROUPS
    if on_to_x0 is not None:
        nj, _, wn = on_to_x0[0].shape

    def body(dx_ref, k_ref, wout_ref, cw_ref, lng_ref, wst_ref, *rest):
        if on_to_x0 is None:
            dp_ref, dcw_ref, dln_ref, dws_ref, dbs_ref, dvn_scr, carry_ref, dbs_acc = rest
        else:
            (win_ref, x0_ref, g_ref, dp_ref, dcw_ref, dln_ref, dws_ref, dbs_ref, dx0_ref, dg_ref,
             dvn_scr, carry_ref, dbs_acc) = rest
        i = pl.program_id(0)
        tile = nt - 1 - i

        @pl.when(i == 0)
        def _():
            dcw_ref[...] = jnp.zeros_like(dcw_ref)
            dln_ref[...] = jnp.zeros_like(dln_ref)
            dws_ref[...] = jnp.zeros_like(dws_ref)
            dbs_acc[...] = jnp.zeros_like(dbs_acc)
            if on_to_x0 is not None:
                dg_ref[...] = jnp.zeros_like(dg_ref)
            carry_ref[...] = jnp.zeros_like(carry_ref)

        keep_next = jnp.where(tile % tiles_per_seq == tiles_per_seq - 1, 0.0, 1.0)
        cw = cw_ref[...]
        lng = lng_ref[...]
        kept_f32 = {key: k_ref[:, k * d:(k + 1) * d].astype(F32) for k, key in enumerate(KEPT) if key != "vnb"}
        b, c, xi, u, v, conv, mixed, sa, sb = (kept_f32[key] for key in KEPT if key != "vnb")
        vnb = k_ref[:, KEPT.index("vnb") * d:(KEPT.index("vnb") + 1) * d]
        xc = v - jnp.mean(v, axis=-1, keepdims=True)
        rstd = lax.rsqrt(jnp.mean(xc * xc, axis=-1, keepdims=True) + EPS)
        vhat = xc * rstd
        dmerged = lax.dot_general(dx_ref[...].astype(BF16), wout_ref[...], NT_DIMS, preferred_element_type=F32)
        dp_ref[:, 5 * d:6 * d] = (dmerged * (b * conv) * (sa * (1.0 - sa))).astype(BF16)
        dp_ref[:, 6 * d:7 * d] = (dmerged * (u * mixed) * (sb * (1.0 - sb))).astype(BF16)
        dya = dmerged * sa
        dyb = dmerged * sb
        dp_ref[:, 0:d] = (dya * conv).astype(BF16)
        dconv = dya * b
        next8 = carry_ref[...] * keep_next
        carry_ref[...] = dconv[:HALO]
        dcx, u1, u2 = _conv_bwd(dconv, next8, cw)
        cx = c * xi
        dcw_ref[0:1, :] += _colsum(u2 * cx)
        dcw_ref[1:2, :] += _colsum(u1 * cx)
        dcw_ref[2:3, :] += _colsum(dconv * cx)
        dp_ref[:, d:2 * d] = (dcx * xi).astype(BF16)
        dp_ref[:, 2 * d:3 * d] = (dcx * c).astype(BF16)
        dp_ref[:, 3 * d:4 * d] = (dyb * mixed).astype(BF16)
        dmixed = dyb * u
        dmb = dmixed.astype(BF16)
        tril = (lax.broadcasted_iota(jnp.int32, (CHUNK, CHUNK), 0)
                >= lax.broadcasted_iota(jnp.int32, (CHUNK, CHUNK), 1))
        triu = (lax.broadcasted_iota(jnp.int32, (CHUNK, CHUNK), 0)
                <= lax.broadcasted_iota(jnp.int32, (CHUNK, CHUNK), 1))
        dbs_tile = dmixed[0:CHUNK]
        for n in range(1, tm // CHUNK):
            dbs_tile = dbs_tile + dmixed[n * CHUNK:(n + 1) * CHUNK]
        dbs_acc[...] += dbs_tile
        for g in range(N_GROUPS):
            wmt = jnp.where(triu, wst_ref[g], 0.0).astype(BF16)
            cols = slice(g * gd, (g + 1) * gd)
            dw = jnp.zeros((CHUNK, CHUNK), F32)
            for n in range(tm // CHUNK):
                rows = slice(n * CHUNK, (n + 1) * CHUNK)
                dvn_scr[rows, cols] = jnp.dot(wmt, dmb[rows, cols], preferred_element_type=F32)
                dw = dw + lax.dot_general(dmb[rows, cols], vnb[rows, cols], NT_DIMS, preferred_element_type=F32)
            dws_ref[g] += jnp.where(tril, dw, 0.0)
        dvn = dvn_scr[...]
        dln_ref[0:1, :] += _colsum(dvn * vhat)
        dln_ref[1:2, :] += _colsum(dvn)
        dvh = dvn * lng
        dv = rstd * (dvh - jnp.mean(dvh, axis=-1, keepdims=True)
                     - vhat * jnp.mean(dvh * vhat, axis=-1, keepdims=True))
        dp_ref[:, 4 * d:5 * d] = dv.astype(BF16)
        if on_to_x0 is not None:
            dh = jnp.zeros((tm, d), F32)
            for j in range(0, nj, 2):
                pair = jnp.concatenate([win_ref[j], win_ref[j + 1]], axis=1)
                dh = dh + lax.dot_general(dp_ref[:, j * wn:(j + 2) * wn], pair, NT_DIMS, preferred_element_type=F32)
            dx, dg = _rms_bwd(dh, x0_ref[...], g_ref[...])
            dx0_ref[...] = dx_ref[...] + dx
            dg_ref[0:1, :] += dg

        @pl.when(i == nt - 1)
        def _():
            for g in range(N_GROUPS):
                cols = slice(g * gd, (g + 1) * gd)
                s = jnp.sum(dbs_acc[:, cols], axis=1, keepdims=True)
                dbs_ref[:, cols] = jnp.broadcast_to(s, (CHUNK, gd))

    rev = lambda i: nt - 1 - i
    const2 = lambda i: (0, 0)
    const3 = lambda i: (0, 0, 0)
    row = lambda i: (rev(i), 0)
    in_specs = [pl.BlockSpec((tm, d), row), pl.BlockSpec((tm, len(KEPT) * d), row),
                _once((d, d), const2), _once((HALO, d), const2), _once((1, d), const2),
                _once((N_GROUPS, CHUNK, CHUNK), const3)]
    out_specs = [pl.BlockSpec((tm, 7 * d), row), pl.BlockSpec((HALO, d), const2), pl.BlockSpec((HALO, d), const2),
                 pl.BlockSpec((N_GROUPS, CHUNK, CHUNK), const3), pl.BlockSpec((CHUNK, d), const2)]
    out_shape = [jax.ShapeDtypeStruct((t, 7 * d), BF16), jax.ShapeDtypeStruct((HALO, d), F32),
                 jax.ShapeDtypeStruct((HALO, d), F32), jax.ShapeDtypeStruct((N_GROUPS, CHUNK, CHUNK), F32),
                 jax.ShapeDtypeStruct((CHUNK, d), F32)]
    args = [dx1, kept, wout, cw, lng, wst]
    if on_to_x0 is not None:
        in_specs += [_once((nj, d, wn), const3), pl.BlockSpec((tm, d), row), _once((1, d), const2)]
        out_specs += [pl.BlockSpec((tm, d), row), pl.BlockSpec((HALO, d), const2)]
        out_shape += [jax.ShapeDtypeStruct((t, d), F32), jax.ShapeDtypeStruct((HALO, d), F32)]
        args += list(on_to_x0)
    return pl.pallas_call(
        body, name=name, grid=(nt,), in_specs=in_specs, out_specs=out_specs, out_shape=out_shape,
        scratch_shapes=[pltpu.VMEM((tm, d), F32), pltpu.VMEM((HALO, d), F32), pltpu.VMEM((CHUNK, d), F32)],
        compiler_params=_params(1),
    )(*args)


def dgrad_rms(dy, w, x, g, res, *, name, tm):
    t, d = x.shape
    n = w.shape[2]
    w = w.reshape(w.shape[0] // 2, 2, d, n)
    nj = w.shape[0]
    tm = min(tm, t)

    def body(dy_ref, w_ref, x_ref, g_ref, res_ref, dx_ref, dg_ref, acc_ref):
        i, j = pl.program_id(0), pl.program_id(1)

        @pl.when((i == 0) & (j == 0))
        def _():
            dg_ref[...] = jnp.zeros_like(dg_ref)

        pair = jnp.concatenate([w_ref[0], w_ref[1]], axis=1)
        part = lax.dot_general(dy_ref[...], pair, NT_DIMS, preferred_element_type=F32)

        @pl.when(j == 0)
        def _():
            acc_ref[...] = part

        @pl.when(j > 0)
        def _():
            acc_ref[...] += part

        @pl.when(j == nj - 1)
        def _():
            dx, dg = _rms_bwd(acc_ref[...], x_ref[...], g_ref[...])
            dx_ref[...] = res_ref[...] + dx
            dg_ref[0:1, :] += dg

    return pl.pallas_call(
        body, name=name, grid=(t // tm, nj),
        in_specs=[pl.BlockSpec((tm, 2 * n), lambda i, j: (i, j)),
                  pl.BlockSpec((None, 2, d, n), lambda i, j: (j, 0, 0, 0)),
                  pl.BlockSpec((tm, d), lambda i, j: (i, 0)),
                  pl.BlockSpec((1, d), lambda i, j: (0, 0)),
                  pl.BlockSpec((tm, d), lambda i, j: (i, 0))],
        out_specs=[pl.BlockSpec((tm, d), lambda i, j: (i, 0)), pl.BlockSpec((HALO, d), lambda i, j: (0, 0))],
        out_shape=[jax.ShapeDtypeStruct((t, d), F32), jax.ShapeDtypeStruct((HALO, d), F32)],
        scratch_shapes=[pltpu.VMEM((tm, d), F32)],
        compiler_params=_params(2),
    )(dy, w, x, g, res)


def wgrad(a, b, *, nj, a_mode, b_mode, name, tm, split=1):
    def describe(arr, mode):
        if mode == "full":
            return arr.shape[0], arr.shape[1], pl.BlockSpec((tm_, arr.shape[1]), lambda j, s: (s, 0))
        if mode == "cols":
            c = arr.shape[1] // nj
            return arr.shape[0], c, pl.BlockSpec((tm_, c), lambda j, s: (s, j))
        return arr.shape[1], arr.shape[2], pl.BlockSpec((None, tm_, arr.shape[2]), lambda j, s: (j, s, 0))

    t = a.shape[0] if a_mode != "lead" else a.shape[1]
    tm_ = min(tm, t)
    _, k, a_spec = describe(a, a_mode)
    _, n, b_spec = describe(b, b_mode)

    ns = t // tm_
    nc = n // split

    def body(a_ref, b_ref, o_ref, acc_ref):
        s = pl.program_id(1)
        part = lax.dot_general(a_ref[...], b_ref[...], TN_DIMS, preferred_element_type=F32)

        def finish(total):
            for q in range(split):
                o_ref[q] = total[:, q * nc:(q + 1) * nc].astype(BF16)

        if ns == 1:
            finish(part)
            return

        @pl.when(s == 0)
        def _():
            acc_ref[...] = part

        @pl.when((s > 0) & (s < ns - 1))
        def _():
            acc_ref[...] += part

        @pl.when(s == ns - 1)
        def _():
            finish(acc_ref[...] + part)

    return pl.pallas_call(
        body, name=name, grid=(nj, ns),
        in_specs=[a_spec, b_spec],
        out_specs=pl.BlockSpec((split, k, nc), lambda j, s: (j, 0, 0)),
        out_shape=jax.ShapeDtypeStruct((nj * split, k, nc), BF16),
        scratch_shapes=[pltpu.VMEM((k, n), F32)],
        compiler_params=_params(2),
    )(a, b)


def _adamw_math(w, g, m, v):
    m = ADAM_B1 * m + (1.0 - ADAM_B1) * g
    v = ADAM_B2 * v + (1.0 - ADAM_B2) * (g * g)
    m_hat = m / (1.0 - ADAM_B1 ** ADAM_STEP)
    v_hat = v / (1.0 - ADAM_B2 ** ADAM_STEP)
    delta = -ADAM_LR * (m_hat / (jnp.sqrt(v_hat) + ADAM_EPS) + ADAM_WD * w)
    return delta, m, v


def _row_tile(rows, at_most):
    if rows <= at_most:
        return rows
    return max(k for k in range(16, at_most + 1, 16) if rows % k == 0)


def _sum_in_device_order(ref):
    total = ref[0]
    for s in range(1, N_DEV):
        total = total + ref[s]
    return total


def adamw_sharded(me, own0, recv0, own1, recv1, w, m, v, *, name, tr):
    _, r, c = w.shape
    tr = _row_tile(r, tr)
    ni = r // tr

    def body(me_ref, o0_ref, r0_ref, o1_ref, r1_ref, w_ref, m_ref, v_ref, g_ref, d_ref, nm_ref, nv_ref):
        def finish(own_ref, recv_ref):
            g = None
            for s in range(N_DEV):
                term = jnp.where(me_ref[0] == s, own_ref[...], recv_ref[s]).astype(F32)
                g = term if g is None else g + term
            delta, nm, nv = _adamw_math(w_ref[...], g, m_ref[...], v_ref[...])
            g_ref[...] = g
            d_ref[...] = delta
            nm_ref[...] = nm
            nv_ref[...] = nv

        @pl.when(pl.program_id(0) == 0)
        def _():
            finish(o0_ref, r0_ref)

        @pl.when(pl.program_id(0) == 1)
        def _():
            finish(o1_ref, r1_ref)

    row0 = lambda l, i: i * (1 - l) + (ni - 1) * l
    row1 = lambda l, i: i * l
    lay = pl.BlockSpec((None, tr, c), lambda l, i, me_ref: (l, i, 0))
    grid_spec = pltpu.PrefetchScalarGridSpec(
        num_scalar_prefetch=1, grid=(2, ni),
        in_specs=[pl.BlockSpec((None, tr, c), lambda l, i, me_ref: (me_ref[0], row0(l, i), 0)),
                  pl.BlockSpec((N_DEV, tr, c), lambda l, i, me_ref: (0, row0(l, i), 0)),
                  pl.BlockSpec((None, tr, c), lambda l, i, me_ref: (me_ref[0], row1(l, i), 0)),
                  pl.BlockSpec((N_DEV, tr, c), lambda l, i, me_ref: (0, row1(l, i), 0)),
                  lay, lay, lay],
        out_specs=[lay, lay, lay, lay])
    return pl.pallas_call(
        body, name=name, grid_spec=grid_spec,
        out_shape=[jax.ShapeDtypeStruct(w.shape, F32)] * 4,
        compiler_params=_params(2),
    )(me, own0, recv0, own1, recv1, w, m, v)


def sum_chunks(me, own, recv, *, name):
    _, r, c = own.shape

    def body(me_ref, o_ref, r_ref, out_ref):
        total = None
        for s in range(N_DEV):
            term = jnp.where(me_ref[0] == s, o_ref[...], r_ref[s]).astype(F32)
            total = term if total is None else total + term
        out_ref[...] = total

    grid_spec = pltpu.PrefetchScalarGridSpec(
        num_scalar_prefetch=1, grid=(1,),
        in_specs=[pl.BlockSpec((None, r, c), lambda i, me_ref: (me_ref[0], 0, 0)),
                  pl.BlockSpec((N_DEV, r, c), lambda i, me_ref: (0, 0, 0))],
        out_specs=pl.BlockSpec((r, c), lambda i, me_ref: (0, 0)))
    return pl.pallas_call(
        body, name=name, grid_spec=grid_spec,
        out_shape=jax.ShapeDtypeStruct((r, c), F32),
        compiler_params=_params(1),
    )(me, own, recv)


def adamw_small(g, w, m, v, *, name):
    def body(g_ref, w_ref, m_ref, v_ref, d_ref, nm_ref, nv_ref):
        delta, nm, nv = _adamw_math(w_ref[...], g_ref[...], m_ref[...], v_ref[...])
        d_ref[...] = delta
        nm_ref[...] = nm
        nv_ref[...] = nv

    return pl.pallas_call(
        body, name=name,
        out_shape=[jax.ShapeDtypeStruct(w.shape, F32)] * 3,
        compiler_params=pltpu.CompilerParams(vmem_limit_bytes=VMEM_LIMIT_BYTES),
    )(g, w, m, v)


def sum_devices(parts, *, name, tr):
    _, r, c = parts.shape
    tr = min(tr, r)

    def body(p_ref, o_ref):
        o_ref[...] = _sum_in_device_order(p_ref)

    return pl.pallas_call(
        body, name=name, grid=(r // tr,),
        in_specs=[pl.BlockSpec((N_DEV, tr, c), lambda i: (0, i, 0))],
        out_specs=pl.BlockSpec((tr, c), lambda i: (i, 0)),
        out_shape=jax.ShapeDtypeStruct((r, c), F32),
        compiler_params=_params(1),
    )(parts)


def _my_place():
    return lax.axis_index("x"), lax.axis_index("y"), lax.axis_index("c")


def _peer_place(r, x, y, c):
    fx, fy, fc = (r >> 2) & 1, (r >> 1) & 1, r & 1
    return (1 - x if fx else x, 1 - y if fy else y, 1 - c if fc else c)


def own_slot(me, w, layer, dtype, *, name, tr):
    _, r, c = w.shape
    tr = _row_tile(r, tr)

    def body(me_ref, w_ref, o_ref):
        o_ref[...] = w_ref[...].astype(dtype)

    grid_spec = pltpu.PrefetchScalarGridSpec(
        num_scalar_prefetch=1, grid=(r // tr,),
        in_specs=[pl.BlockSpec((None, tr, c), lambda i, me_ref: (layer, i, 0))],
        out_specs=pl.BlockSpec((None, tr, c), lambda i, me_ref: (me_ref[0], i, 0)))
    return pl.pallas_call(
        body, name=name, grid_spec=grid_spec,
        out_shape=jax.ShapeDtypeStruct((N_DEV, r, c), dtype),
        compiler_params=_params(1),
    )(me, w)


EXCHANGES = {
    "scatter": [(0, r) for r in range(1, N_DEV)],
    "gather": [(0, r) for r in range(1, N_DEV)],
    "gather_chips": [(0, r) for r in (1, 2, 4, 6)],
    "gather_forward": [(q, 1) for q in (2, 4, 6)],
}


def _split_copy(k, entry, src, land, send_sem, recv_sem, arriving):
    slot, peer = entry
    x, y, c = _my_place()

    def index(relation):
        px, py, pc = _peer_place(relation, x, y, c)
        return 4 * px + 2 * py + pc

    return pltpu.make_async_remote_copy(
        src_ref=land.at[index(slot)] if src is None else src.at[index(peer)],
        dst_ref=land.at[index(slot ^ peer if arriving else slot)],
        send_sem=send_sem.at[k], recv_sem=recv_sem.at[k],
        device_id=_peer_place(peer, x, y, c), device_id_type=MESH)


def start_copies(srcs, lands, *, mode, name, after=None):
    n = len(lands)
    entries = EXCHANGES[mode]
    bufs = (list(srcs) if srcs is not None else []) + list(lands)
    nb = len(bufs)

    def body(*refs):
        src = refs[:n] if srcs is not None else [None] * n
        land = refs[nb - n:nb]
        outs = refs[nb + len(extra):]
        send_sems, recv_sems = outs[:n], outs[n:2 * n]
        token = outs[2 * n + nb]
        for a in range(n):
            for k, entry in enumerate(entries):
                _split_copy(k, entry, src[a], land[a], send_sems[a], recv_sems[a], False).start()
        token[...] = jnp.zeros_like(token)

    extra = [] if after is None else [after]
    outs = pl.pallas_call(
        body, name=name,
        in_specs=[HBM_SPEC] * nb + [ANY] * len(extra),
        out_specs=[SEM_SPEC] * (2 * n) + [HBM_SPEC] * nb + [pl.BlockSpec(memory_space=pltpu.VMEM)],
        out_shape=([pltpu.SemaphoreType.DMA((len(entries),))] * (2 * n)
                   + [pltpu.HBM(a.shape, a.dtype) for a in bufs]
                   + [jax.ShapeDtypeStruct((8, 128), F32)]),
        input_output_aliases={i: 2 * n + i for i in range(nb)},
        compiler_params=pltpu.CompilerParams(has_side_effects=DATAFLOW),
    )(*[pltpu.with_memory_space_constraint(a, pltpu.HBM) for a in bufs], *extra)
    thru = list(outs[2 * n:2 * n + nb])
    return dict(send=outs[:n], recv=outs[n:2 * n], src=thru[:n] if srcs is not None else None, land=thru[nb - n:],
                token=outs[2 * n + nb], mode=mode)


def finish_copies(started, which, after, *, name):
    n = len(which)
    entries = EXCHANGES[started["mode"]]
    has_src = started["src"] is not None
    bufs = ([started["src"][i] for i in which] if has_src else []) + [started["land"][i] for i in which]
    nb = len(bufs)

    def body(*refs):
        src = refs[:n] if has_src else [None] * n
        land = refs[nb - n:nb]
        send_sems, recv_sems = refs[nb:nb + n], refs[nb + n:nb + 2 * n]
        for a in range(n):
            for k, entry in enumerate(entries):
                cp = _split_copy(k, entry, src[a], land[a], send_sems[a], recv_sems[a], True)
                cp.wait_send()
                cp.wait_recv()

    outs = pl.pallas_call(
        body, name=name,
        in_specs=[HBM_SPEC] * nb + [SEM_SPEC] * (2 * n) + [ANY],
        out_specs=[HBM_SPEC] * nb,
        out_shape=[pltpu.HBM(a.shape, a.dtype) for a in bufs],
        input_output_aliases={i: i for i in range(nb)},
        compiler_params=pltpu.CompilerParams(has_side_effects=DATAFLOW),
    )(*bufs, *[started["send"][i] for i in which], *[started["recv"][i] for i in which], after)
    return (list(outs[:n]) if has_src else None), list(outs[nb - n:])


def _pad_rows(a, rows):
    pad = [(0, 0)] * a.ndim
    pad[-2] = (0, rows - a.shape[-2])
    return jnp.pad(a, pad)


def kernel(x, mix_norm_g, w_in, conv_a_w, ln_v_g, ln_v_b, w_s, b_s, w_out, ffn_norm_g, w_up, conv_ffn_w, w_down, final_norm_g, loss_target, m_mix_norm_g, m_w_in, m_conv_a_w, m_ln_v_g, m_ln_v_b, m_w_s, m_b_s, m_w_out, m_ffn_norm_g, m_w_up, m_conv_ffn_w, m_w_down, m_final_norm_g, v_mix_norm_g, v_w_in, v_conv_a_w, v_ln_v_g, v_ln_v_b, v_w_s, v_b_s, v_w_out, v_ffn_norm_g, v_w_up, v_conv_ffn_w, v_w_down, v_final_norm_g):
    nb, seq, d = x.shape
    t = nb * seq
    depth = w_in.shape[0]
    f = w_up.shape[2]
    me = 4 * lax.axis_index("x") + 2 * lax.axis_index("y") + lax.axis_index("c")
    xt = x.reshape(t, d)
    tgt = loss_target.reshape(t, d)

    conv_pack = jnp.concatenate([_pad_rows(conv_a_w, HALO), _pad_rows(conv_ffn_w, HALO)], axis=-1)
    me_arr = me.astype(jnp.int32).reshape(1)
    w_up_t, m_w_up_t, v_w_up_t = (jnp.swapaxes(a, 1, 2) for a in (w_up, m_w_up, v_w_up))
    zones, slot_of = [], {}
    for l in range(depth):
        for key, w in (("win", w_in), ("conv", None), ("wout", w_out), ("wup", w_up_t), ("wd", w_down)):
            if key == "conv":
                if l == 0:
                    slot_of["conv"] = len(zones)
                    packed = conv_pack.reshape(1, depth * HALO, conv_pack.shape[-1])
                    zones.append(own_slot(me_arr, packed, 0, F32, name="own_slot_conv", tr=ROWS_PER_STEP["own_slot"]))
                continue
            slot_of[key, l] = len(zones)
            zones.append(own_slot(me_arr, w, l, BF16, name=f"own_slot_{key}_{l}", tr=ROWS_PER_STEP["own_slot"]))
    first = [slot_of["win", 0], slot_of["wout", 0], slot_of["conv"]]
    rest = [i for i in range(len(zones)) if i not in first]
    to_chips = start_copies(None, [zones[i] for i in first], mode="gather_chips", name="gather_first_chips")
    _, at_chips = finish_copies(to_chips, [0, 1, 2], zones[rest[-1]], name="wait_first_chips")
    to_sibling = start_copies(None, at_chips, mode="gather_forward", name="gather_first_forward")
    gathering = start_copies(None, [zones[i] for i in rest], mode="gather", name="gather_start", after=to_sibling["token"])

    def gathered(keys, after, name):
        return finish_copies(gathering, [rest.index(slot_of[k]) for k in keys], after, name=name)[1]

    saved, layers = [], []
    cur = xt
    for l in range(depth):
        p = dict(mix_g=mix_norm_g[l][None], ffn_g=ffn_norm_g[l][None], lng=ln_v_g[l][None], lnb=ln_v_b[l][None],
                 ws=w_s[l], wst=jnp.swapaxes(w_s[l], 1, 2),
                 bias=jnp.repeat(b_s[l].T, d // N_GROUPS, axis=1))
        if l == 0:
            _, (p["win"], wout_g, conv_g) = finish_copies(to_sibling, [0, 1, 2], gathering["token"],
                                                          name=f"wait_w_mixer_{l}")
            conv_g = conv_g.reshape(N_DEV, depth, HALO, -1)
            ca = conv_g.shape[-1] - f
        else:
            p["win"], wout_g = gathered([("win", l), ("wout", l)], after, f"wait_w_mixer_{l}")
        p["wout"] = wout_g.reshape(d, d)
        p["cw_a"] = jnp.transpose(conv_g[:, l, :, :ca], (1, 0, 2)).reshape(HALO, d)
        p["cw_f"] = conv_g[:, l, :, ca:]
        h, merged, x1, kept = mixer_fwd(cur, p["mix_g"], p["win"], p["wout"], p["cw_a"], p["lng"], p["lnb"],
                                        p["ws"], p["bias"], seq=seq, name=f"mixer_fwd_{l}", tm=TOKENS_PER_STEP["mixer"])
        p["wup"], wd_g = gathered([("wup", l), ("wd", l)], merged, f"wait_w_ffn_{l}")
        p["wd"] = wd_g.reshape(N_DEV // 2, 2 * wd_g.shape[1], d)
        head = (final_norm_g[None], tgt) if l == depth - 1 else None
        h2, up0, fac, act, x2, *of_loss = ffn_fwd(x1, p["ffn_g"], p["wup"], p["wd"], p["cw_f"],
                                                  seq=seq, name=f"ffn_fwd_{l}", tm=TOKENS_PER_STEP["ffn"], head=head)
        saved.append(dict(x0=cur, h=h, kept=kept, merged=merged, x1=x1, h2=h2, up0=up0, fac=fac, act=act))
        layers.append(p)
        cur, after = x2, act
    dx = cur
    d_final_g, loss_tile = of_loss

    def exchange(parts, name):
        return start_copies(parts, [lax.empty(a.shape, a.dtype) for a in parts], mode="scatter", name=name)

    def tied(g, started):
        return g + started["token"][0:1, 0:1]

    part = [None] * depth
    mix_ex = None
    for l in reversed(range(depth)):
        p, s = layers[l], saved[l]
        ffn_g = p["ffn_g"] if mix_ex is None else tied(p["ffn_g"], mix_ex)
        dup0, dcw_f, dx1, d_ffn_g = ffn_bwd(dx, s["up0"], s["fac"], p["wd"], p["cw_f"], p["wup"], s["x1"], ffn_g,
                                            seq=seq, name=f"ffn_bwd_{l}", tm=TOKENS_PER_STEP["ffn"])
        g_wd = wgrad(s["act"], dx, nj=N_DEV // 2, a_mode="lead", b_mode="full", name=f"wgrad_down_{l}", tm=TOKENS_PER_STEP["wgrad"])
        g_wup = wgrad(dup0, s["h2"], nj=N_DEV, a_mode="lead", b_mode="full", name=f"wgrad_up_{l}", tm=TOKENS_PER_STEP["wgrad"])
        ffn_ex = exchange([g_wd.reshape(N_DEV, g_wd.shape[1] // 2, d), g_wup], f"exchange_ffn_{l}")
        fused = l > 0
        dproj, dcw_a, dln, dws, dbs, *to_x0 = mixer_bwd(
            dx1, s["kept"], p["wout"], tied(p["cw_a"], ffn_ex), p["lng"], p["wst"],
            seq=seq, name=f"mixer_bwd_{l}", tm=TOKENS_PER_STEP["mixer" if fused else "mixer_bwd_alone"], on_to_x0=(p["win"], s["x0"], p["mix_g"]) if fused else None)
        g_wout = wgrad(s["merged"], dx1, nj=1, a_mode="full", b_mode="full", name=f"wgrad_out_{l}", tm=TOKENS_PER_STEP["wgrad"])
        g_win = wgrad(s["h"], dproj, nj=N_DEV // 2, a_mode="full", b_mode="cols", name=f"wgrad_in_{l}", tm=TOKENS_PER_STEP["wgrad"], split=2)
        cwa_chunks = jnp.transpose(dcw_a.reshape(HALO, N_DEV, d // N_DEV), (1, 0, 2))
        mix_ex = exchange([g_wout.reshape(N_DEV, d // N_DEV, d), g_win, dws, dcw_f, cwa_chunks], f"exchange_mix_{l}")
        if fused:
            dx, d_mix_g = to_x0
        else:
            dx, d_mix_g = dgrad_rms(dproj, p["win"], s["x0"], tied(p["mix_g"], mix_ex), dx1,
                                    name=f"dgrad_in_{l}", tm=TOKENS_PER_STEP["dgrad"])
        part[l] = dict(
            ffn_ex=ffn_ex, mix_ex=mix_ex,
            vectors=jnp.concatenate([d_mix_g[0:1], d_ffn_g[0:1], dln[0:2],
                                     dbs[:, ::d // N_GROUPS].T.reshape(1, d)], axis=0))
    grad_x = dx.reshape(nb, seq, d)

    own, recv = {}, {}

    def arrived(l, ex, keys, after):
        srcs, lands = finish_copies(part[l][ex], list(range(len(keys))), after, name=f"wait_{ex}_{l}")
        for k, key in enumerate(keys):
            own[key, l], recv[key, l] = srcs[k], lands[k]
        return lands[1]

    def big(key, w, m, v, name, me=me_arr):
        return adamw_sharded(me, own[key, 0], recv[key, 0], own[key, 1], recv[key, 1], w, m, v, name=name, tr=ROWS_PER_STEP["adamw"])

    mix_keys = ("wout", "win", "ws", "cwf", "cwa")
    after = grad_x
    for l in reversed(range(depth)):
        after = arrived(l, "ffn_ex", ("wd", "wup"), after)
        if l > 0:
            after = arrived(l, "mix_ex", mix_keys, after)
    u_wd = big("wd", w_down, m_w_down, v_w_down, "adamw_w_down")
    u_wup = tuple(jnp.swapaxes(a, 1, 2) for a in big("wup", w_up_t, m_w_up_t, v_w_up_t, "adamw_w_up"))
    arrived(0, "mix_ex", mix_keys, u_wup[1])

    def owned(key, l):
        return sum_chunks(me_arr, own[key, l], recv[key, l], name=f"sum_{key}_{l}")

    g_cwf = jnp.stack([owned("cwf", l)[:3] for l in range(depth)])
    g_cwa = jnp.stack([owned("cwa", l)[:3] for l in range(depth)])
    ws_rows = CHUNK * CHUNK // d
    ws_mine = jnp.concatenate([owned("ws", l).reshape(ws_rows, d) for l in range(depth)], axis=0)
    loss_row = jnp.zeros((1, d), F32).at[0, 0].set(loss_tile[0, 0])
    vectors = jnp.concatenate([part[l]["vectors"] for l in range(depth)] + [d_final_g[0:1], loss_row], axis=0)
    vectors = _pad_rows(vectors, -(-vectors.shape[0] // 8) * 8)
    small_zones = [own_slot(me_arr, a[None], 0, F32, name=f"own_slot_{key}", tr=ROWS_PER_STEP["own_slot"])
                   for key, a in (("vectors", vectors), ("ws_sums", ws_mine))]
    small_ex = start_copies(None, small_zones, mode="gather", name="gather_small_start")
    me_then = me_arr + small_ex["token"][0, 0:1].astype(jnp.int32)
    u_wout = big("wout", w_out, m_w_out, v_w_out, "adamw_w_out", me_then)
    u_win = big("win", w_in, m_w_in, v_w_in, "adamw_w_in", me_then)
    _, (vectors_all, ws_all) = finish_copies(small_ex, [0, 1], u_win[0], name="wait_small_grads")
    vec_sum = sum_devices(vectors_all, name="sum_small", tr=ROWS_PER_STEP["sum"])
    g_ws = jnp.transpose(ws_all.reshape(N_DEV, depth, CHUNK, CHUNK), (1, 0, 2, 3))
    per_layer = part[0]["vectors"].shape[0]
    g_mix, g_ffn, g_lng, g_lnb = (jnp.stack([vec_sum[l * per_layer + k] for l in range(depth)]) for k in range(4))
    g_bs = jnp.stack([vec_sum[l * per_layer + 4].reshape(N_GROUPS, CHUNK) for l in range(depth)])
    g_final = vec_sum[depth * per_layer]
    loss = vec_sum[depth * per_layer + 1, 0]

    def small_update(g, w, m, v, name):
        shape = w.shape
        two_d = (-1, shape[-1]) if w.ndim > 1 else (1, shape[0])
        out = adamw_small(g.reshape(two_d), w.reshape(two_d), m.reshape(two_d), v.reshape(two_d), name=name)
        return (g.reshape(shape),) + tuple(o.reshape(shape) for o in out)

    u_mix = small_update(g_mix, mix_norm_g, m_mix_norm_g, v_mix_norm_g, "adamw_mix_norm_g")
    u_cwa = small_update(g_cwa, conv_a_w, m_conv_a_w, v_conv_a_w, "adamw_conv_a_w")
    u_lng = small_update(g_lng, ln_v_g, m_ln_v_g, v_ln_v_g, "adamw_ln_v_g")
    u_lnb = small_update(g_lnb, ln_v_b, m_ln_v_b, v_ln_v_b, "adamw_ln_v_b")
    u_ws = small_update(g_ws, w_s, m_w_s, v_w_s, "adamw_w_s")
    u_bs = small_update(g_bs, b_s, m_b_s, v_b_s, "adamw_b_s")
    u_ffn = small_update(g_ffn, ffn_norm_g, m_ffn_norm_g, v_ffn_norm_g, "adamw_ffn_norm_g")
    u_cwf = small_update(g_cwf, conv_ffn_w, m_conv_ffn_w, v_conv_ffn_w, "adamw_conv_ffn_w")
    u_final = small_update(g_final, final_norm_g, m_final_norm_g, v_final_norm_g, "adamw_final_norm_g")

    ordered = [u_mix, u_win, u_cwa, u_lng, u_lnb, u_ws, u_bs, u_wout, u_ffn, u_wup, u_cwf, u_wd, u_final]
    return (loss, grad_x, *[u[0] for u in ordered], *[u[1] for u in ordered],
            *[u[2] for u in ordered], *[u[3] for u in ordered])
```

```python
import jax
import jax.numpy as jnp
from jax import lax
from jax.experimental import pallas as pl
from jax.experimental.pallas import tpu as pltpu

EPS = 1e-6
CHUNK = 128
N_GROUPS = 8
N_DEV = 8
HALO = 8
ADAM_LR = 0.001
ADAM_B1 = 0.9
ADAM_B2 = 0.999
ADAM_EPS = 1e-08
ADAM_WD = 0.01
ADAM_STEP = 10
VMEM_LIMIT_BYTES = 56 * 1024 * 1024
TOKENS_PER_STEP = dict(mixer=256, mixer_bwd_alone=512, ffn=256, wgrad=2048, dgrad=1024)
ROWS_PER_STEP = dict(adamw=256, own_slot=256, sum=512)
F32 = jnp.float32
BF16 = jnp.bfloat16
MESH = pl.DeviceIdType.MESH
ANY = pl.BlockSpec(memory_space=pl.ANY)
HBM_SPEC = pl.BlockSpec(memory_space=pltpu.HBM)
SEM_SPEC = pl.BlockSpec(memory_space=pltpu.SEMAPHORE)
DATAFLOW = pltpu.SideEffectType.DATAFLOW_SIDE_EFFECTING
NT_DIMS = (((1,), (1,)), ((), ()))
TN_DIMS = (((0,), (0,)), ((), ()))


def _params(n_grid_axes):
    return pltpu.CompilerParams(dimension_semantics=("arbitrary",) * n_grid_axes,
                                vmem_limit_bytes=VMEM_LIMIT_BYTES)


def _shift_down(cur, prev8, k):
    rolled = pltpu.roll(cur, k, 0)
    prolled = pltpu.roll(prev8, k, 0)
    row = lax.broadcasted_iota(jnp.int32, prev8.shape, 0)
    head = jnp.where(row < k, prolled, rolled[:HALO])
    return jnp.concatenate([head, rolled[HALO:]], axis=0)


def _shift_up(cur, next8, k):
    tm = cur.shape[0]
    rolled = pltpu.roll(cur, tm - k, 0)
    nrolled = pltpu.roll(next8, HALO - k, 0)
    row = lax.broadcasted_iota(jnp.int32, next8.shape, 0)
    tail = jnp.where(row >= HALO - k, nrolled, rolled[tm - HALO:])
    return jnp.concatenate([rolled[:tm - HALO], tail], axis=0)


def _conv_fwd(cur, prev8, cw):
    s1 = _shift_down(cur, prev8, 1)
    s2 = _shift_down(cur, prev8, 2)
    y = s2 * cw[0:1, :] + s1 * cw[1:2, :] + cur * cw[2:3, :]
    return y, s1, s2


def _conv_bwd(d, next8, cw):
    u1 = _shift_up(d, next8, 1)
    u2 = _shift_up(d, next8, 2)
    return d * cw[2:3, :] + u1 * cw[1:2, :] + u2 * cw[0:1, :], u1, u2


def _colsum(a):
    return jnp.sum(a, axis=0, keepdims=True)


def _rms_stats(xv):
    r = lax.rsqrt(jnp.mean(xv * xv, axis=-1, keepdims=True) + EPS)
    return r, xv * r


def _rms_bwd(dh, xv, g):
    r, n = _rms_stats(xv)
    dn = dh * g
    dx = r * (dn - n * jnp.mean(dn * n, axis=-1, keepdims=True))
    return dx, _colsum(dh * n)


def _mixer_forward(p_ref, cprev, xiprev, cw, lng, lnb, ws_ref, bias_ref, mixed_scr, d):
    tm = p_ref.shape[0]
    b = p_ref[:, 0:d]
    c = p_ref[:, d:2 * d]
    xi = p_ref[:, 2 * d:3 * d]
    u = p_ref[:, 3 * d:4 * d]
    v = p_ref[:, 4 * d:5 * d]
    sa = jax.nn.sigmoid(p_ref[:, 5 * d:6 * d])
    sb = jax.nn.sigmoid(p_ref[:, 6 * d:7 * d])
    cx = c * xi
    conv, _, _ = _conv_fwd(cx, cprev * xiprev, cw)
    xc = v - jnp.mean(v, axis=-1, keepdims=True)
    vhat = xc * lax.rsqrt(jnp.mean(xc * xc, axis=-1, keepdims=True) + EPS)
    vnb = (vhat * lng + lnb).astype(BF16)
    tril = (lax.broadcasted_iota(jnp.int32, (CHUNK, CHUNK), 0)
            >= lax.broadcasted_iota(jnp.int32, (CHUNK, CHUNK), 1))
    gd = d // N_GROUPS
    for g in range(N_GROUPS):
        wm = jnp.where(tril, ws_ref[g], 0.0).astype(BF16)
        cols = slice(g * gd, (g + 1) * gd)
        for n in range(tm // CHUNK):
            rows = slice(n * CHUNK, (n + 1) * CHUNK)
            mixed_scr[rows, cols] = (jnp.dot(wm, vnb[rows, cols], preferred_element_type=F32)
                                     + bias_ref[:, cols])
    mixed = mixed_scr[...]
    merged = sa * (b * conv) + sb * (u * mixed)
    return dict(b=b, c=c, xi=xi, u=u, v=v, sa=sa, sb=sb, cx=cx, conv=conv, vnb=vnb, mixed=mixed, merged=merged)


KEPT = ("b", "c", "xi", "u", "v", "conv", "mixed", "vnb", "sa", "sb")


def _once(block_shape, index_map):
    return pl.BlockSpec(block_shape, index_map, pipeline_mode=pl.Buffered(1))


def mixer_fwd(x, g, win, wout, cw, lng, lnb, ws, bias, *, seq, name, tm):
    t, d = x.shape
    nj, _, n = win.shape
    tm = min(tm, seq)
    tiles_per_seq = seq // tm

    def body(x_ref, g_ref, win_ref, wout_ref, cw_ref, lng_ref, lnb_ref, ws_ref, bias_ref,
             h_ref, merged_ref, x1_ref, kept_ref, p_ref, mixed_scr, carry_ref):
        @pl.when(pl.program_id(0) == 0)
        def _():
            carry_ref[...] = jnp.zeros_like(carry_ref)

        keep = jnp.where(pl.program_id(0) % tiles_per_seq == 0, 0.0, 1.0)
        xv = x_ref[...]
        _, nrm = _rms_stats(xv)
        hb = (nrm * g_ref[...]).astype(BF16)
        h_ref[...] = hb
        for j in range(0, nj, 2):
            pair = jnp.concatenate([win_ref[j], win_ref[j + 1]], axis=1)
            p_ref[:, j * n:(j + 2) * n] = jnp.dot(hb, pair, preferred_element_type=F32)
        f = _mixer_forward(p_ref, carry_ref[...] * keep, 1.0, cw_ref[...], lng_ref[...],
                           lnb_ref[...], ws_ref, bias_ref, mixed_scr, d)
        carry_ref[...] = f["cx"][tm - HALO:]
        for k, key in enumerate(KEPT):
            kept_ref[:, k * d:(k + 1) * d] = f[key].astype(BF16)
        mb = f["merged"].astype(BF16)
        merged_ref[...] = mb
        x1_ref[...] = xv + jnp.dot(mb, wout_ref[...], preferred_element_type=F32)

    const2 = lambda i: (0, 0)
    const3 = lambda i: (0, 0, 0)
    row = lambda i: (i, 0)
    return pl.pallas_call(
        body, name=name, grid=(t // tm,),
        in_specs=[pl.BlockSpec((tm, d), row),
                  _once((1, d), const2),
                  _once((nj, d, n), const3),
                  _once((d, d), const2),
                  _once((HALO, d), const2),
                  _once((1, d), const2),
                  _once((1, d), const2),
                  _once((N_GROUPS, CHUNK, CHUNK), const3),
                  _once((CHUNK, d), const2)],
        out_specs=[pl.BlockSpec((tm, d), row), pl.BlockSpec((tm, d), row), pl.BlockSpec((tm, d), row),
                   pl.BlockSpec((tm, len(KEPT) * d), row)],
        out_shape=[jax.ShapeDtypeStruct((t, d), BF16), jax.ShapeDtypeStruct((t, d), BF16),
                   jax.ShapeDtypeStruct((t, d), F32), jax.ShapeDtypeStruct((t, len(KEPT) * d), BF16)],
        scratch_shapes=[pltpu.VMEM((tm, nj * n), F32), pltpu.VMEM((tm, d), F32), pltpu.VMEM((HALO, d), F32)],
        compiler_params=_params(1),
    )(x, g, win, wout, cw, lng, lnb, ws, bias)


def ffn_fwd(x1, g, wup, wd, cw, *, seq, name, tm, head=None):
    t, d = x1.shape
    nj, f, _ = wup.shape
    half = nj // 2
    tm = min(tm, seq)
    tiles_per_seq = seq // tm

    def body(x1_ref, g_ref, wup_ref, wd_ref, cw_ref, *rest):
        if head is None:
            h2_ref, up_ref, fac_ref, act_ref, x2_ref, carry_ref = rest
        else:
            gf_ref, tgt_ref, h2_ref, up_ref, fac_ref, act_ref, x2_ref, dgf_ref, loss_ref, carry_ref = rest

        @pl.when(pl.program_id(0) == 0)
        def _():
            carry_ref[...] = jnp.zeros_like(carry_ref)
            if head is not None:
                dgf_ref[...] = jnp.zeros_like(dgf_ref)
                loss_ref[...] = jnp.zeros_like(loss_ref)

        keep = jnp.where(pl.program_id(0) % tiles_per_seq == 0, 0.0, 1.0)
        xv = x1_ref[...]
        _, nrm = _rms_stats(xv)
        hb = (nrm * g_ref[...]).astype(BF16)
        h2_ref[...] = hb

        for j in range(nj):
            up_ref[j] = lax.dot_general(hb, wup_ref[j], NT_DIMS, preferred_element_type=F32)

        def conv_of(j):
            up0 = up_ref[j]
            y, _, _ = _conv_fwd(up0, carry_ref[j] * keep, cw_ref[j])
            carry_ref[j] = up0[tm - HALO:]
            return y

        acc = xv
        for k in range(half):
            gate, val = conv_of(k), conv_of(k + half)
            sg = jax.nn.sigmoid(gate)
            silu = gate * sg
            fac_ref[k] = (val * (sg * (1.0 + gate * (1.0 - sg)))).astype(BF16)
            fac_ref[k + half] = silu.astype(BF16)
            a = (silu * val).astype(BF16)
            act_ref[k] = a
            acc = acc + jnp.dot(a, wd_ref[k], preferred_element_type=F32)
        if head is None:
            x2_ref[...] = acc
        else:
            gv = gf_ref[...]
            r, n = _rms_stats(acc)
            err = n * gv - tgt_ref[...]
            loss_ref[...] += 0.5 * jnp.sum(jnp.mean(err * err, axis=-1, keepdims=True))
            dy = err * (1.0 / d)
            dn = dy * gv
            x2_ref[...] = r * (dn - n * jnp.mean(dn * n, axis=-1, keepdims=True))
            dgf_ref[0:1, :] += _colsum(dy * n)

    const3 = lambda i: (0, 0, 0)
    row = lambda i: (i, 0)
    in_specs = [pl.BlockSpec((tm, d), row), _once((1, d), lambda i: (0, 0)), _once((nj, f, d), const3),
                _once((half, f, d), const3), _once((nj, HALO, f), const3)]
    out_specs = [pl.BlockSpec((tm, d), row), pl.BlockSpec((nj, tm, f), lambda i: (0, i, 0)),
                 pl.BlockSpec((nj, tm, f), lambda i: (0, i, 0)), pl.BlockSpec((half, tm, f), lambda i: (0, i, 0)),
                 pl.BlockSpec((tm, d), row)]
    out_shape = [jax.ShapeDtypeStruct((t, d), BF16), jax.ShapeDtypeStruct((nj, t, f), F32),
                 jax.ShapeDtypeStruct((nj, t, f), BF16), jax.ShapeDtypeStruct((half, t, f), BF16),
                 jax.ShapeDtypeStruct((t, d), F32)]
    args = [x1, g, wup, wd, cw]
    if head is not None:
        in_specs += [_once((1, d), lambda i: (0, 0)), pl.BlockSpec((tm, d), row)]
        out_specs += [pl.BlockSpec((HALO, d), lambda i: (0, 0)), pl.BlockSpec((8, 128), lambda i: (0, 0))]
        out_shape += [jax.ShapeDtypeStruct((HALO, d), F32), jax.ShapeDtypeStruct((8, 128), F32)]
        args += list(head)
    return pl.pallas_call(
        body, name=name, grid=(t // tm,), in_specs=in_specs, out_specs=out_specs, out_shape=out_shape,
        scratch_shapes=[pltpu.VMEM((nj, HALO, f), F32)],
        compiler_params=_params(1),
    )(*args)


def ffn_bwd(dx2, up0, fac, wd, cw, wup, x1, g, *, seq, name, tm):
    t, d = dx2.shape
    nj, _, f = up0.shape
    half = nj // 2
    tm = min(tm, seq)
    tiles_per_seq = seq // tm
    nt = t // tm

    def body(dx_ref, up_ref, fac_ref, wd_ref, cw_ref, wup_ref, x1_ref, g_ref,
             dup_ref, dcw_ref, dx1_ref, dg_ref, carry_ref):
        i = pl.program_id(0)
        tile = nt - 1 - i

        @pl.when(i == 0)
        def _():
            dcw_ref[...] = jnp.zeros_like(dcw_ref)
            dg_ref[...] = jnp.zeros_like(dg_ref)
            carry_ref[...] = jnp.zeros_like(carry_ref)

        keep_next = jnp.where(tile % tiles_per_seq == tiles_per_seq - 1, 0.0, 1.0)
        dx2v = dx_ref[...]
        dxb = dx2v.astype(BF16)
        dh = [jnp.zeros((tm, d), F32)]

        def through_conv(j, dup):
            next8 = carry_ref[j] * keep_next
            carry_ref[j] = dup[:HALO]
            dup0, u1, u2 = _conv_bwd(dup, next8, cw_ref[j])
            up0 = up_ref[j]
            dcw_ref[j, 0:1, :] += _colsum(u2 * up0)
            dcw_ref[j, 1:2, :] += _colsum(u1 * up0)
            dcw_ref[j, 2:3, :] += _colsum(dup * up0)
            dup0 = dup0.astype(BF16)
            dup_ref[j] = dup0
            dh[0] = dh[0] + jnp.dot(dup0, wup_ref[j], preferred_element_type=F32)

        dacts = [lax.dot_general(dxb, wd_ref[k], NT_DIMS, preferred_element_type=F32) for k in range(half)]
        for k in range(half):
            through_conv(k, dacts[k] * fac_ref[k].astype(F32))
            through_conv(k + half, dacts[k] * fac_ref[k + half].astype(F32))

        dx, dg = _rms_bwd(dh[0], x1_ref[...], g_ref[...])
        dx1_ref[...] = dx2v + dx
        dg_ref[0:1, :] += dg

    rev = lambda i: nt - 1 - i
    return pl.pallas_call(
        body, name=name, grid=(nt,),
        in_specs=[pl.BlockSpec((tm, d), lambda i: (rev(i), 0)),
                  pl.BlockSpec((nj, tm, f), lambda i: (0, rev(i), 0)),
                  pl.BlockSpec((nj, tm, f), lambda i: (0, rev(i), 0)),
                  _once((half, f, d), lambda i: (0, 0, 0)),
                  _once((nj, HALO, f), lambda i: (0, 0, 0)),
                  _once((nj, f, d), lambda i: (0, 0, 0)),
                  pl.BlockSpec((tm, d), lambda i: (rev(i), 0)),
                  _once((1, d), lambda i: (0, 0))],
        out_specs=[pl.BlockSpec((nj, tm, f), lambda i: (0, rev(i), 0)),
                   pl.BlockSpec((nj, HALO, f), lambda i: (0, 0, 0)),
                   pl.BlockSpec((tm, d), lambda i: (rev(i), 0)),
                   pl.BlockSpec((HALO, d), lambda i: (0, 0))],
        out_shape=[jax.ShapeDtypeStruct((nj, t, f), BF16), jax.ShapeDtypeStruct((nj, HALO, f), F32),
                   jax.ShapeDtypeStruct((t, d), F32), jax.ShapeDtypeStruct((HALO, d), F32)],
        scratch_shapes=[pltpu.VMEM((nj, HALO, f), F32)],
        compiler_params=_params(1),
    )(dx2, up0, fac, wd, cw, wup, x1, g)


def mixer_bwd(dx1, kept, wout, cw, lng, wst, *, seq, name, tm, on_to_x0=None):
    t, d = dx1.shape
    tm = min(tm, seq)
    tiles_per_seq = seq // tm
    nt = t // tm
    gd = d // N_GROUPS
    if on_to_x0 is not None:
        nj, _, wn = on_to_x0[0].shape

    def body(dx_ref, k_ref, wout_ref, cw_ref, lng_ref, wst_ref, *rest):
        if on_to_x0 is None:
            dp_ref, dcw_ref, dln_ref, dws_ref, dbs_ref, dvn_scr, carry_ref, dbs_acc = rest
        else:
            (win_ref, x0_ref, g_ref, dp_ref, dcw_ref, dln_ref, dws_ref, dbs_ref, dx0_ref, dg_ref,
             dvn_scr, carry_ref, dbs_acc) = rest
        i = pl.program_id(0)
        tile = nt - 1 - i

        @pl.when(i == 0)
        def _():
            dcw_ref[...] = jnp.zeros_like(dcw_ref)
            dln_ref[...] = jnp.zeros_like(dln_ref)
            dws_ref[...] = jnp.zeros_like(dws_ref)
            dbs_acc[...] = jnp.zeros_like(dbs_acc)
            if on_to_x0 is not None:
                dg_ref[...] = jnp.zeros_like(dg_ref)
            carry_ref[...] = jnp.zeros_like(carry_ref)

        keep_next = jnp.where(tile % tiles_per_seq == tiles_per_seq - 1, 0.0, 1.0)
        cw = cw_ref[...]
        lng = lng_ref[...]
        kept_f32 = {key: k_ref[:, k * d:(k + 1) * d].astype(F32) for k, key in enumerate(KEPT) if key != "vnb"}
        b, c, xi, u, v, conv, mixed, sa, sb = (kept_f32[key] for key in KEPT if key != "vnb")
        vnb = k_ref[:, KEPT.index("vnb") * d:(KEPT.index("vnb") + 1) * d]
        xc = v - jnp.mean(v, axis=-1, keepdims=True)
        rstd = lax.rsqrt(jnp.mean(xc * xc, axis=-1, keepdims=True) + EPS)
        vhat = xc * rstd
        dmerged = lax.dot_general(dx_ref[...].astype(BF16), wout_ref[...], NT_DIMS, preferred_element_type=F32)
        dp_ref[:, 5 * d:6 * d] = (dmerged * (b * conv) * (sa * (1.0 - sa))).astype(BF16)
        dp_ref[:, 6 * d:7 * d] = (dmerged * (u * mixed) * (sb * (1.0 - sb))).astype(BF16)
        dya = dmerged * sa
        dyb = dmerged * sb
        dp_ref[:, 0:d] = (dya * conv).astype(BF16)
        dconv = dya * b
        next8 = carry_ref[...] * keep_next
        carry_ref[...] = dconv[:HALO]
        dcx, u1, u2 = _conv_bwd(dconv, next8, cw)
        cx = c * xi
        dcw_ref[0:1, :] += _colsum(u2 * cx)
        dcw_ref[1:2, :] += _colsum(u1 * cx)
        dcw_ref[2:3, :] += _colsum(dconv * cx)
        dp_ref[:, d:2 * d] = (dcx * xi).astype(BF16)
        dp_ref[:, 2 * d:3 * d] = (dcx * c).astype(BF16)
        dp_ref[:, 3 * d:4 * d] = (dyb * mixed).astype(BF16)
        dmixed = dyb * u
        dmb = dmixed.astype(BF16)
        tril = (lax.broadcasted_iota(jnp.int32, (CHUNK, CHUNK), 0)
                >= lax.broadcasted_iota(jnp.int32, (CHUNK, CHUNK), 1))
        triu = (lax.broadcasted_iota(jnp.int32, (CHUNK, CHUNK), 0)
                <= lax.broadcasted_iota(jnp.int32, (CHUNK, CHUNK), 1))
        dbs_tile = dmixed[0:CHUNK]
        for n in range(1, tm // CHUNK):
            dbs_tile = dbs_tile + dmixed[n * CHUNK:(n + 1) * CHUNK]
        dbs_acc[...] += dbs_tile
        for g in range(N_GROUPS):
            wmt = jnp.where(triu, wst_ref[g], 0.0).astype(BF16)
            cols = slice(g * gd, (g + 1) * gd)
            dw = jnp.zeros((CHUNK, CHUNK), F32)
            for n in range(tm // CHUNK):
                rows = slice(n * CHUNK, (n + 1) * CHUNK)
                dvn_scr[rows, cols] = jnp.dot(wmt, dmb[rows, cols], preferred_element_type=F32)
                dw = dw + lax.dot_general(dmb[rows, cols], vnb[rows, cols], NT_DIMS, preferred_element_type=F32)
            dws_ref[g] += jnp.where(tril, dw, 0.0)
        dvn = dvn_scr[...]
        dln_ref[0:1, :] += _colsum(dvn * vhat)
        dln_ref[1:2, :] += _colsum(dvn)
        dvh = dvn * lng
        dv = rstd * (dvh - jnp.mean(dvh, axis=-1, keepdims=True)
                     - vhat * jnp.mean(dvh * vhat, axis=-1, keepdims=True))
        dp_ref[:, 4 * d:5 * d] = dv.astype(BF16)
        if on_to_x0 is not None:
            dh = jnp.zeros((tm, d), F32)
            for j in range(0, nj, 2):
                pair = jnp.concatenate([win_ref[j], win_ref[j + 1]], axis=1)
                dh = dh + lax.dot_general(dp_ref[:, j * wn:(j + 2) * wn], pair, NT_DIMS, preferred_element_type=F32)
            dx, dg = _rms_bwd(dh, x0_ref[...], g_ref[...])
            dx0_ref[...] = dx_ref[...] + dx
            dg_ref[0:1, :] += dg

        @pl.when(i == nt - 1)
        def _():
            for g in range(N_GROUPS):
                cols = slice(g * gd, (g + 1) * gd)
                s = jnp.sum(dbs_acc[:, cols], axis=1, keepdims=True)
                dbs_ref[:, cols] = jnp.broadcast_to(s, (CHUNK, gd))

    rev = lambda i: nt - 1 - i
    const2 = lambda i: (0, 0)
    const3 = lambda i: (0, 0, 0)
    row = lambda i: (rev(i), 0)
    in_specs = [pl.BlockSpec((tm, d), row), pl.BlockSpec((tm, len(KEPT) * d), row),
                _once((d, d), const2), _once((HALO, d), const2), _once((1, d), const2),
                _once((N_GROUPS, CHUNK, CHUNK), const3)]
    out_specs = [pl.BlockSpec((tm, 7 * d), row), pl.BlockSpec((HALO, d), const2), pl.BlockSpec((HALO, d), const2),
                 pl.BlockSpec((N_GROUPS, CHUNK, CHUNK), const3), pl.BlockSpec((CHUNK, d), const2)]
    out_shape = [jax.ShapeDtypeStruct((t, 7 * d), BF16), jax.ShapeDtypeStruct((HALO, d), F32),
                 jax.ShapeDtypeStruct((HALO, d), F32), jax.ShapeDtypeStruct((N_GROUPS, CHUNK, CHUNK), F32),
                 jax.ShapeDtypeStruct((CHUNK, d), F32)]
    args = [dx1, kept, wout, cw, lng, wst]
    if on_to_x0 is not None:
        in_specs += [_once((nj, d, wn), const3), pl.BlockSpec((tm, d), row), _once((1, d), const2)]
        out_specs += [pl.BlockSpec((tm, d), row), pl.BlockSpec((HALO, d), const2)]
        out_shape += [jax.ShapeDtypeStruct((t, d), F32), jax.ShapeDtypeStruct((HALO, d), F32)]
        args += list(on_to_x0)
    return pl.pallas_call(
        body, name=name, grid=(nt,), in_specs=in_specs, out_specs=out_specs, out_shape=out_shape,
        scratch_shapes=[pltpu.VMEM((tm, d), F32), pltpu.VMEM((HALO, d), F32), pltpu.VMEM((CHUNK, d), F32)],
        compiler_params=_params(1),
    )(*args)


def dgrad_rms(dy, w, x, g, res, *, name, tm):
    t, d = x.shape
    n = w.shape[2]
    w = w.reshape(w.shape[0] // 2, 2, d, n)
    nj = w.shape[0]
    tm = min(tm, t)

    def body(dy_ref, w_ref, x_ref, g_ref, res_ref, dx_ref, dg_ref, acc_ref):
        i, j = pl.program_id(0), pl.program_id(1)

        @pl.when((i == 0) & (j == 0))
        def _():
            dg_ref[...] = jnp.zeros_like(dg_ref)

        pair = jnp.concatenate([w_ref[0], w_ref[1]], axis=1)
        part = lax.dot_general(dy_ref[...], pair, NT_DIMS, preferred_element_type=F32)

        @pl.when(j == 0)
        def _():
            acc_ref[...] = part

        @pl.when(j > 0)
        def _():
            acc_ref[...] += part

        @pl.when(j == nj - 1)
        def _():
            dx, dg = _rms_bwd(acc_ref[...], x_ref[...], g_ref[...])
            dx_ref[...] = res_ref[...] + dx
            dg_ref[0:1, :] += dg

    return pl.pallas_call(
        body, name=name, grid=(t // tm, nj),
        in_specs=[pl.BlockSpec((tm, 2 * n), lambda i, j: (i, j)),
                  pl.BlockSpec((None, 2, d, n), lambda i, j: (j, 0, 0, 0)),
                  pl.BlockSpec((tm, d), lambda i, j: (i, 0)),
                  pl.BlockSpec((1, d), lambda i, j: (0, 0)),
                  pl.BlockSpec((tm, d), lambda i, j: (i, 0))],
        out_specs=[pl.BlockSpec((tm, d), lambda i, j: (i, 0)), pl.BlockSpec((HALO, d), lambda i, j: (0, 0))],
        out_shape=[jax.ShapeDtypeStruct((t, d), F32), jax.ShapeDtypeStruct((HALO, d), F32)],
        scratch_shapes=[pltpu.VMEM((tm, d), F32)],
        compiler_params=_params(2),
    )(dy, w, x, g, res)


def wgrad(a, b, *, nj, a_mode, b_mode, name, tm, split=1):
    def describe(arr, mode):
        if mode == "full":
            return arr.shape[0], arr.shape[1], pl.BlockSpec((tm_, arr.shape[1]), lambda j, s: (s, 0))
        if mode == "cols":
            c = arr.shape[1] // nj
            return arr.shape[0], c, pl.BlockSpec((tm_, c), lambda j, s: (s, j))
        return arr.shape[1], arr.shape[2], pl.BlockSpec((None, tm_, arr.shape[2]), lambda j, s: (j, s, 0))

    t = a.shape[0] if a_mode != "lead" else a.shape[1]
    tm_ = min(tm, t)
    _, k, a_spec = describe(a, a_mode)
    _, n, b_spec = describe(b, b_mode)

    ns = t // tm_
    nc = n // split

    def body(a_ref, b_ref, o_ref, acc_ref):
        s = pl.program_id(1)
        part = lax.dot_general(a_ref[...], b_ref[...], TN_DIMS, preferred_element_type=F32)

        def finish(total):
            for q in range(split):
                o_ref[q] = total[:, q * nc:(q + 1) * nc].astype(BF16)

        if ns == 1:
            finish(part)
            return

        @pl.when(s == 0)
        def _():
            acc_ref[...] = part

        @pl.when((s > 0) & (s < ns - 1))
        def _():
            acc_ref[...] += part

        @pl.when(s == ns - 1)
        def _():
            finish(acc_ref[...] + part)

    return pl.pallas_call(
        body, name=name, grid=(nj, ns),
        in_specs=[a_spec, b_spec],
        out_specs=pl.BlockSpec((split, k, nc), lambda j, s: (j, 0, 0)),
        out_shape=jax.ShapeDtypeStruct((nj * split, k, nc), BF16),
        scratch_shapes=[pltpu.VMEM((k, n), F32)],
        compiler_params=_params(2),
    )(a, b)


def _adamw_math(w, g, m, v):
    m = ADAM_B1 * m + (1.0 - ADAM_B1) * g
    v = ADAM_B2 * v + (1.0 - ADAM_B2) * (g * g)
    m_hat = m / (1.0 - ADAM_B1 ** ADAM_STEP)
    v_hat = v / (1.0 - ADAM_B2 ** ADAM_STEP)
    delta = -ADAM_LR * (m_hat / (jnp.sqrt(v_hat) + ADAM_EPS) + ADAM_WD * w)
    return delta, m, v


def _row_tile(rows, at_most):
    if rows <= at_most:
        return rows
    return max(k for k in range(16, at_most + 1, 16) if rows % k == 0)


def _sum_in_device_order(ref):
    total = ref[0]
    for s in range(1, N_DEV):
        total = total + ref[s]
    return total


def adamw_sharded(me, own0, recv0, own1, recv1, w, m, v, *, name, tr):
    _, r, c = w.shape
    tr = _row_tile(r, tr)
    ni = r // tr

    def body(me_ref, o0_ref, r0_ref, o1_ref, r1_ref, w_ref, m_ref, v_ref, g_ref, d_ref, nm_ref, nv_ref):
        def finish(own_ref, recv_ref):
            g = None
            for s in range(N_DEV):
                term = jnp.where(me_ref[0] == s, own_ref[...], recv_ref[s]).astype(F32)
                g = term if g is None else g + term
            delta, nm, nv = _adamw_math(w_ref[...], g, m_ref[...], v_ref[...])
            g_ref[...] = g
            d_ref[...] = delta
            nm_ref[...] = nm
            nv_ref[...] = nv

        @pl.when(pl.program_id(0) == 0)
        def _():
            finish(o0_ref, r0_ref)

        @pl.when(pl.program_id(0) == 1)
        def _():
            finish(o1_ref, r1_ref)

    row0 = lambda l, i: i * (1 - l) + (ni - 1) * l
    row1 = lambda l, i: i * l
    lay = pl.BlockSpec((None, tr, c), lambda l, i, me_ref: (l, i, 0))
    grid_spec = pltpu.PrefetchScalarGridSpec(
        num_scalar_prefetch=1, grid=(2, ni),
        in_specs=[pl.BlockSpec((None, tr, c), lambda l, i, me_ref: (me_ref[0], row0(l, i), 0)),
                  pl.BlockSpec((N_DEV, tr, c), lambda l, i, me_ref: (0, row0(l, i), 0)),
                  pl.BlockSpec((None, tr, c), lambda l, i, me_ref: (me_ref[0], row1(l, i), 0)),
                  pl.BlockSpec((N_DEV, tr, c), lambda l, i, me_ref: (0, row1(l, i), 0)),
                  lay, lay, lay],
        out_specs=[lay, lay, lay, lay])
    return pl.pallas_call(
        body, name=name, grid_spec=grid_spec,
        out_shape=[jax.ShapeDtypeStruct(w.shape, F32)] * 4,
        compiler_params=_params(2),
    )(me, own0, recv0, own1, recv1, w, m, v)


def sum_chunks(me, own, recv, *, name):
    _, r, c = own.shape

    def body(me_ref, o_ref, r_ref, out_ref):
        total = None
        for s in range(N_DEV):
            term = jnp.where(me_ref[0] == s, o_ref[...], r_ref[s]).astype(F32)
            total = term if total is None else total + term
        out_ref[...] = total

    grid_spec = pltpu.PrefetchScalarGridSpec(
        num_scalar_prefetch=1, grid=(1,),
        in_specs=[pl.BlockSpec((None, r, c), lambda i, me_ref: (me_ref[0], 0, 0)),
                  pl.BlockSpec((N_DEV, r, c), lambda i, me_ref: (0, 0, 0))],
        out_specs=pl.BlockSpec((r, c), lambda i, me_ref: (0, 0)))
    return pl.pallas_call(
        body, name=name, grid_spec=grid_spec,
        out_shape=jax.ShapeDtypeStruct((r, c), F32),
        compiler_params=_params(1),
    )(me, own, recv)


def adamw_small(g, w, m, v, *, name):
    def body(g_ref, w_ref, m_ref, v_ref, d_ref, nm_ref, nv_ref):
        delta, nm, nv = _adamw_math(w_ref[...], g_ref[...], m_ref[...], v_ref[...])
        d_ref[...] = delta
        nm_ref[...] = nm
        nv_ref[...] = nv

    return pl.pallas_call(
        body, name=name,
        out_shape=[jax.ShapeDtypeStruct(w.shape, F32)] * 3,
        compiler_params=pltpu.CompilerParams(vmem_limit_bytes=VMEM_LIMIT_BYTES),
    )(g, w, m, v)


def sum_devices(parts, *, name, tr):
    _, r, c = parts.shape
    tr = min(tr, r)

    def body(p_ref, o_ref):
        o_ref[...] = _sum_in_device_order(p_ref)

    return pl.pallas_call(
        body, name=name, grid=(r // tr,),
        in_specs=[pl.BlockSpec((N_DEV, tr, c), lambda i: (0, i, 0))],
        out_specs=pl.BlockSpec((tr, c), lambda i: (i, 0)),
        out_shape=jax.ShapeDtypeStruct((r, c), F32),
        compiler_params=_params(1),
    )(parts)


def _my_place():
    return lax.axis_index("x"), lax.axis_index("y"), lax.axis_index("c")


def _peer_place(r, x, y, c):
    fx, fy, fc = (r >> 2) & 1, (r >> 1) & 1, r & 1
    return (1 - x if fx else x, 1 - y if fy else y, 1 - c if fc else c)


def own_slot(me, w, layer, dtype, *, name, tr):
    _, r, c = w.shape
    tr = _row_tile(r, tr)

    def body(me_ref, w_ref, o_ref):
        o_ref[...] = w_ref[...].astype(dtype)

    grid_spec = pltpu.PrefetchScalarGridSpec(
        num_scalar_prefetch=1, grid=(r // tr,),
        in_specs=[pl.BlockSpec((None, tr, c), lambda i, me_ref: (layer, i, 0))],
        out_specs=pl.BlockSpec((None, tr, c), lambda i, me_ref: (me_ref[0], i, 0)))
    return pl.pallas_call(
        body, name=name, grid_spec=grid_spec,
        out_shape=jax.ShapeDtypeStruct((N_DEV, r, c), dtype),
        compiler_params=_params(1),
    )(me, w)


EXCHANGES = {
    "scatter": [(0, r) for r in range(1, N_DEV)],
    "gather": [(0, r) for r in range(1, N_DEV)],
    "gather_chips": [(0, r) for r in (1, 2, 4, 6)],
    "gather_forward": [(q, 1) for q in (2, 4, 6)],
}


def _split_copy(k, entry, src, land, send_sem, recv_sem, arriving):
    slot, peer = entry
    x, y, c = _my_place()

    def index(relation):
        px, py, pc = _peer_place(relation, x, y, c)
        return 4 * px + 2 * py + pc

    return pltpu.make_async_remote_copy(
        src_ref=land.at[index(slot)] if src is None else src.at[index(peer)],
        dst_ref=land.at[index(slot ^ peer if arriving else slot)],
        send_sem=send_sem.at[k], recv_sem=recv_sem.at[k],
        device_id=_peer_place(peer, x, y, c), device_id_type=MESH)


def start_copies(srcs, lands, *, mode, name, after=None):
    n = len(lands)
    entries = EXCHANGES[mode]
    bufs = (list(srcs) if srcs is not None else []) + list(lands)
    nb = len(bufs)

    def body(*refs):
        src = refs[:n] if srcs is not None else [None] * n
        land = refs[nb - n:nb]
        outs = refs[nb + len(extra):]
        send_sems, recv_sems = outs[:n], outs[n:2 * n]
        token = outs[2 * n + nb]
        for a in range(n):
            for k, entry in enumerate(entries):
                _split_copy(k, entry, src[a], land[a], send_sems[a], recv_sems[a], False).start()
        token[...] = jnp.zeros_like(token)

    extra = [] if after is None else [after]
    outs = pl.pallas_call(
        body, name=name,
        in_specs=[HBM_SPEC] * nb + [ANY] * len(extra),
        out_specs=[SEM_SPEC] * (2 * n) + [HBM_SPEC] * nb + [pl.BlockSpec(memory_space=pltpu.VMEM)],
        out_shape=([pltpu.SemaphoreType.DMA((len(entries),))] * (2 * n)
                   + [pltpu.HBM(a.shape, a.dtype) for a in bufs]
                   + [jax.ShapeDtypeStruct((8, 128), F32)]),
        input_output_aliases={i: 2 * n + i for i in range(nb)},
        compiler_params=pltpu.CompilerParams(has_side_effects=DATAFLOW),
    )(*[pltpu.with_memory_space_constraint(a, pltpu.HBM) for a in bufs], *extra)
    thru = list(outs[2 * n:2 * n + nb])
    return dict(send=outs[:n], recv=outs[n:2 * n], src=thru[:n] if srcs is not None else None, land=thru[nb - n:],
                token=outs[2 * n + nb], mode=mode)


def finish_copies(started, which, after, *, name):
    n = len(which)
    entries = EXCHANGES[started["mode"]]
    has_src = started["src"] is not None
    bufs = ([started["src"][i] for i in which] if has_src else []) + [started["land"][i] for i in which]
    nb = len(bufs)

    def body(*refs):
        src = refs[:n] if has_src else [None] * n
        land = refs[nb - n:nb]
        send_sems, recv_sems = refs[nb:nb + n], refs[nb + n:nb + 2 * n]
        for a in range(n):
            for k, entry in enumerate(entries):
                cp = _split_copy(k, entry, src[a], land[a], send_sems[a], recv_sems[a], True)
                cp.wait_send()
                cp.wait_recv()

    outs = pl.pallas_call(
        body, name=name,
        in_specs=[HBM_SPEC] * nb + [SEM_SPEC] * (2 * n) + [ANY],
        out_specs=[HBM_SPEC] * nb,
        out_shape=[pltpu.HBM(a.shape, a.dtype) for a in bufs],
        input_output_aliases={i: i for i in range(nb)},
        compiler_params=pltpu.CompilerParams(has_side_effects=DATAFLOW),
    )(*bufs, *[started["send"][i] for i in which], *[started["recv"][i] for i in which], after)
    return (list(outs[:n]) if has_src else None), list(outs[nb - n:])


def _pad_rows(a, rows):
    pad = [(0, 0)] * a.ndim
    pad[-2] = (0, rows - a.shape[-2])
    return jnp.pad(a, pad)


def kernel(x, mix_norm_g, w_in, conv_a_w, ln_v_g, ln_v_b, w_s, b_s, w_out, ffn_norm_g, w_up, conv_ffn_w, w_down, final_norm_g, loss_target, m_mix_norm_g, m_w_in, m_conv_a_w, m_ln_v_g, m_ln_v_b, m_w_s, m_b_s, m_w_out, m_ffn_norm_g, m_w_up, m_conv_ffn_w, m_w_down, m_final_norm_g, v_mix_norm_g, v_w_in, v_conv_a_w, v_ln_v_g, v_ln_v_b, v_w_s, v_b_s, v_w_out, v_ffn_norm_g, v_w_up, v_conv_ffn_w, v_w_down, v_final_norm_g):
    nb, seq, d = x.shape
    t = nb * seq
    depth = w_in.shape[0]
    f = w_up.shape[2]
    me = 4 * lax.axis_index("x") + 2 * lax.axis_index("y") + lax.axis_index("c")
    xt = x.reshape(t, d)
    tgt = loss_target.reshape(t, d)

    conv_pack = jnp.concatenate([_pad_rows(conv_a_w, HALO), _pad_rows(conv_ffn_w, HALO)], axis=-1)
    me_arr = me.astype(jnp.int32).reshape(1)
    w_up_t, m_w_up_t, v_w_up_t = (jnp.swapaxes(a, 1, 2) for a in (w_up, m_w_up, v_w_up))
    zones, slot_of = [], {}
    for l in range(depth):
        for key, w in (("win", w_in), ("conv", None), ("wout", w_out), ("wup", w_up_t), ("wd", w_down)):
            if key == "conv":
                if l == 0:
                    slot_of["conv"] = len(zones)
                    packed = conv_pack.reshape(1, depth * HALO, conv_pack.shape[-1])
                    zones.append(own_slot(me_arr, packed, 0, F32, name="own_slot_conv", tr=ROWS_PER_STEP["own_slot"]))
                continue
            slot_of[key, l] = len(zones)
            zones.append(own_slot(me_arr, w, l, BF16, name=f"own_slot_{key}_{l}", tr=ROWS_PER_STEP["own_slot"]))
    first = [slot_of["win", 0], slot_of["wout", 0], slot_of["conv"]]
    rest = [i for i in range(len(zones)) if i not in first]
    to_chips = start_copies(None, [zones[i] for i in first], mode="gather_chips", name="gather_first_chips")
    gathering = start_copies(None, [zones[i] for i in rest], mode="gather", name="gather_start", after=to_chips["token"])
    _, at_chips = finish_copies(to_chips, [0, 1, 2], gathering["token"], name="wait_first_chips")
    to_sibling = start_copies(None, at_chips, mode="gather_forward", name="gather_first_forward")

    def gathered(keys, after, name):
        return finish_copies(gathering, [rest.index(slot_of[k]) for k in keys], after, name=name)[1]

    saved, layers = [], []
    cur = xt
    for l in range(depth):
        p = dict(mix_g=mix_norm_g[l][None], ffn_g=ffn_norm_g[l][None], lng=ln_v_g[l][None], lnb=ln_v_b[l][None],
                 ws=w_s[l], wst=jnp.swapaxes(w_s[l], 1, 2),
                 bias=jnp.repeat(b_s[l].T, d // N_GROUPS, axis=1))
        if l == 0:
            _, (p["win"], wout_g, conv_g) = finish_copies(to_sibling, [0, 1, 2], to_sibling["token"],
                                                          name=f"wait_w_mixer_{l}")
            conv_g = conv_g.reshape(N_DEV, depth, HALO, -1)
            ca = conv_g.shape[-1] - f
        else:
            p["win"], wout_g = gathered([("win", l), ("wout", l)], after, f"wait_w_mixer_{l}")
        p["wout"] = wout_g.reshape(d, d)
        p["cw_a"] = jnp.transpose(conv_g[:, l, :, :ca], (1, 0, 2)).reshape(HALO, d)
        p["cw_f"] = conv_g[:, l, :, ca:]
        h, merged, x1, kept = mixer_fwd(cur, p["mix_g"], p["win"], p["wout"], p["cw_a"], p["lng"], p["lnb"],
                                        p["ws"], p["bias"], seq=seq, name=f"mixer_fwd_{l}", tm=TOKENS_PER_STEP["mixer"])
        p["wup"], wd_g = gathered([("wup", l), ("wd", l)], merged, f"wait_w_ffn_{l}")
        p["wd"] = wd_g.reshape(N_DEV // 2, 2 * wd_g.shape[1], d)
        head = (final_norm_g[None], tgt) if l == depth - 1 else None
        h2, up0, fac, act, x2, *of_loss = ffn_fwd(x1, p["ffn_g"], p["wup"], p["wd"], p["cw_f"],
                                                  seq=seq, name=f"ffn_fwd_{l}", tm=TOKENS_PER_STEP["ffn"], head=head)
        saved.append(dict(x0=cur, h=h, kept=kept, merged=merged, x1=x1, h2=h2, up0=up0, fac=fac, act=act))
        layers.append(p)
        cur, after = x2, act
    dx = cur
    d_final_g, loss_tile = of_loss

    def exchange(parts, name):
        return start_copies(parts, [lax.empty(a.shape, a.dtype) for a in parts], mode="scatter", name=name)

    def tied(g, started):
        return g + started["token"][0:1, 0:1]

    part = [None] * depth
    mix_ex = None
    for l in reversed(range(depth)):
        p, s = layers[l], saved[l]
        ffn_g = p["ffn_g"] if mix_ex is None else tied(p["ffn_g"], mix_ex)
        dup0, dcw_f, dx1, d_ffn_g = ffn_bwd(dx, s["up0"], s["fac"], p["wd"], p["cw_f"], p["wup"], s["x1"], ffn_g,
                                            seq=seq, name=f"ffn_bwd_{l}", tm=TOKENS_PER_STEP["ffn"])
        g_wd = wgrad(s["act"], dx, nj=N_DEV // 2, a_mode="lead", b_mode="full", name=f"wgrad_down_{l}", tm=TOKENS_PER_STEP["wgrad"])
        g_wup = wgrad(dup0, s["h2"], nj=N_DEV, a_mode="lead", b_mode="full", name=f"wgrad_up_{l}", tm=TOKENS_PER_STEP["wgrad"])
        ffn_parts = [g_wd.reshape(N_DEV, g_wd.shape[1] // 2, d), g_wup]
        ffn_ex = exchange(ffn_parts, f"exchange_ffn_{l}") if l == 0 else None
        fused = l > 0
        dproj, dcw_a, dln, dws, dbs, *to_x0 = mixer_bwd(
            dx1, s["kept"], p["wout"], p["cw_a"] if ffn_ex is None else tied(p["cw_a"], ffn_ex), p["lng"], p["wst"],
            seq=seq, name=f"mixer_bwd_{l}", tm=TOKENS_PER_STEP["mixer" if fused else "mixer_bwd_alone"], on_to_x0=(p["win"], s["x0"], p["mix_g"]) if fused else None)
        g_wout = wgrad(s["merged"], dx1, nj=1, a_mode="full", b_mode="full", name=f"wgrad_out_{l}", tm=TOKENS_PER_STEP["wgrad"])
        g_win = wgrad(s["h"], dproj, nj=N_DEV // 2, a_mode="full", b_mode="cols", name=f"wgrad_in_{l}", tm=TOKENS_PER_STEP["wgrad"], split=2)
        cwa_chunks = jnp.transpose(dcw_a.reshape(HALO, N_DEV, d // N_DEV), (1, 0, 2))
        mix_ex = exchange([g_wout.reshape(N_DEV, d // N_DEV, d), g_win, dws, dcw_f, cwa_chunks]
                          + (ffn_parts if ffn_ex is None else []), f"exchange_mix_{l}")
        if fused:
            dx, d_mix_g = to_x0
        else:
            dx, d_mix_g = dgrad_rms(dproj, p["win"], s["x0"], tied(p["mix_g"], mix_ex), dx1,
                                    name=f"dgrad_in_{l}", tm=TOKENS_PER_STEP["dgrad"])
        part[l] = dict(
            ffn_ex=ffn_ex, mix_ex=mix_ex,
            vectors=jnp.concatenate([d_mix_g[0:1], d_ffn_g[0:1], dln[0:2],
                                     dbs[:, ::d // N_GROUPS].T.reshape(1, d)], axis=0))
    grad_x = dx.reshape(nb, seq, d)

    own, recv = {}, {}

    def arrived(l, ex, keys, after):
        srcs, lands = finish_copies(part[l][ex], list(range(len(keys))), after, name=f"wait_{ex}_{l}")
        for k, key in enumerate(keys):
            own[key, l], recv[key, l] = srcs[k], lands[k]
        return lands[1]

    def big(key, w, m, v, name, me=me_arr):
        return adamw_sharded(me, own[key, 0], recv[key, 0], own[key, 1], recv[key, 1], w, m, v, name=name, tr=ROWS_PER_STEP["adamw"])

    mix_keys = ("wout", "win", "ws", "cwf", "cwa")
    after = grad_x
    for l in reversed(range(depth)):
        if l > 0:
            after = arrived(l, "mix_ex", mix_keys + ("wd", "wup"), after)
        else:
            after = arrived(l, "ffn_ex", ("wd", "wup"), after)
    u_wd = big("wd", w_down, m_w_down, v_w_down, "adamw_w_down")
    u_wup = tuple(jnp.swapaxes(a, 1, 2) for a in big("wup", w_up_t, m_w_up_t, v_w_up_t, "adamw_w_up"))
    arrived(0, "mix_ex", mix_keys, u_wup[1])

    def owned(key, l):
        return sum_chunks(me_arr, own[key, l], recv[key, l], name=f"sum_{key}_{l}")

    g_cwf = jnp.stack([owned("cwf", l)[:3] for l in range(depth)])
    g_cwa = jnp.stack([owned("cwa", l)[:3] for l in range(depth)])
    ws_rows = CHUNK * CHUNK // d
    ws_mine = jnp.concatenate([owned("ws", l).reshape(ws_rows, d) for l in range(depth)], axis=0)
    loss_row = jnp.zeros((1, d), F32).at[0, 0].set(loss_tile[0, 0])
    vectors = jnp.concatenate([part[l]["vectors"] for l in range(depth)] + [d_final_g[0:1], loss_row], axis=0)
    vectors = _pad_rows(vectors, -(-vectors.shape[0] // 8) * 8)
    small_zones = [own_slot(me_arr, a[None], 0, F32, name=f"own_slot_{key}", tr=ROWS_PER_STEP["own_slot"])
                   for key, a in (("vectors", vectors), ("ws_sums", ws_mine))]
    small_ex = start_copies(None, small_zones, mode="gather", name="gather_small_start")
    me_then = me_arr + small_ex["token"][0, 0:1].astype(jnp.int32)
    u_wout = big("wout", w_out, m_w_out, v_w_out, "adamw_w_out", me_then)
    u_win = big("win", w_in, m_w_in, v_w_in, "adamw_w_in", me_then)
    _, (vectors_all, ws_all) = finish_copies(small_ex, [0, 1], u_win[0], name="wait_small_grads")
    vec_sum = sum_devices(vectors_all, name="sum_small", tr=ROWS_PER_STEP["sum"])
    g_ws = jnp.transpose(ws_all.reshape(N_DEV, depth, CHUNK, CHUNK), (1, 0, 2, 3))
    per_layer = part[0]["vectors"].shape[0]
    g_mix, g_ffn, g_lng, g_lnb = (jnp.stack([vec_sum[l * per_layer + k] for l in range(depth)]) for k in range(4))
    g_bs = jnp.stack([vec_sum[l * per_layer + 4].reshape(N_GROUPS, CHUNK) for l in range(depth)])
    g_final = vec_sum[depth * per_layer]
    loss = vec_sum[depth * per_layer + 1, 0]

    def small_update(g, w, m, v, name):
        shape = w.shape
        two_d = (-1, shape[-1]) if w.ndim > 1 else (1, shape[0])
        out = adamw_small(g.reshape(two_d), w.reshape(two_d), m.reshape(two_d), v.reshape(two_d), name=name)
        return (g.reshape(shape),) + tuple(o.reshape(shape) for o in out)

    u_mix = small_update(g_mix, mix_norm_g, m_mix_norm_g, v_mix_norm_g, "adamw_mix_norm_g")
    u_cwa = small_update(g_cwa, conv_a_w, m_conv_a_w, v_conv_a_w, "adamw_conv_a_w")
    u_lng = small_update(g_lng, ln_v_g, m_ln_v_g, v_ln_v_g, "adamw_ln_v_g")
    u_lnb = small_update(g_lnb, ln_v_b, m_ln_v_b, v_ln_v_b, "adamw_ln_v_b")
    u_ws = small_update(g_ws, w_s, m_w_s, v_w_s, "adamw_w_s")
    u_bs = small_update(g_bs, b_s, m_b_s, v_b_s, "adamw_b_s")
    u_ffn = small_update(g_ffn, ffn_norm_g, m_ffn_norm_g, v_ffn_norm_g, "adamw_ffn_norm_g")
    u_cwf = small_update(g_cwf, conv_ffn_w, m_conv_ffn_w, v_conv_ffn_w, "adamw_conv_ffn_w")
    u_final = small_update(g_final, final_norm_g, m_final_norm_g, v_final_norm_g, "adamw_final_norm_g")

    ordered = [u_mix, u_win, u_cwa, u_lng, u_lnb, u_ws, u_bs, u_wout, u_ffn, u_wup, u_cwf, u_wd, u_final]
    return (loss, grad_x, *[u[0] for u in ordered], *[u[1] for u in ordered],
            *[u[2] for u in ordered], *[u[3] for u in ordered])
```
